```python
import jax, jax.numpy as jnp
from jax import lax
import numpy as np


D_MODEL = 1024
BATCH = 8
SEQ = 8192
DEPTH = 1

N_Q_HEADS = 16
N_KV_HEADS = 4
HEAD_DIM = 64
WINDOW = 128
ATTN_BLOCK = 128
ROPE_THETA = 500000.0
ROPE_DIM = HEAD_DIM // 4
HGRN_HEADS = 8
HGRN_DK = 128
HGRN_DV = 128
HGRN_CHUNK = 64
D_FF = 2816
ATTN_WIDTH = N_Q_HEADS * HEAD_DIM
KV_WIDTH = N_KV_HEADS * HEAD_DIM
HGRN_KWIDTH = HGRN_HEADS * HGRN_DK
HGRN_VWIDTH = HGRN_HEADS * HGRN_DV
IN_WIDTHS = (ATTN_WIDTH, KV_WIDTH, KV_WIDTH, HGRN_KWIDTH, HGRN_KWIDTH, HGRN_VWIDTH, HGRN_VWIDTH, D_MODEL, D_MODEL)
D_IN = sum(IN_WIDTHS)
DEEPNORM_ALPHA = (2 * DEPTH) ** 0.25
DEEPNORM_BETA = (8 * DEPTH) ** -0.25
LN_EPS = 1e-5
RMS_EPS = 1e-6
NEG_INF = -1e30

kernel_name = 'hybrid_swa_sink_hgrn2_macaron_deepnorm'


def layer_norm(x, g, b):
    xf = x.astype(jnp.float32)
    mu = jnp.mean(xf, axis=-1, keepdims=True)
    var = jnp.mean(jnp.square(xf - mu), axis=-1, keepdims=True)
    y = (xf - mu) * lax.rsqrt(var + LN_EPS) * g.astype(jnp.float32) + b.astype(jnp.float32)
    return y.astype(x.dtype)


def swiglu(x, w1, w3, w2):
    return (jax.nn.silu(x @ w1) * (x @ w3)) @ w2


def rope_tables(seq_len):
    pos = jnp.arange(seq_len, dtype=jnp.float32)
    inv_freq = ROPE_THETA ** (-jnp.arange(0, ROPE_DIM, 2, dtype=jnp.float32) / ROPE_DIM)
    ang = pos[:, None] * inv_freq[None, :]
    return jnp.cos(ang)[None, :, None, :], jnp.sin(ang)[None, :, None, :]


def partial_rope(t, cos, sin):
    tf = t.astype(jnp.float32)
    half = ROPE_DIM // 2
    t1, t2, rest = tf[..., :half], tf[..., half:ROPE_DIM], tf[..., ROPE_DIM:]
    rot = jnp.concatenate([t1 * cos - t2 * sin, t2 * cos + t1 * sin, rest], axis=-1)
    return rot.astype(t.dtype)


def sliding_window_attention(q, k, v, sinks):
    B, S = q.shape[0], q.shape[1]
    nb = S // ATTN_BLOCK
    grp = N_Q_HEADS // N_KV_HEADS
    qb = q.reshape(B, nb, ATTN_BLOCK, N_KV_HEADS, grp, HEAD_DIM)

    def band(t):
        tp = jnp.pad(t, ((0, 0), (ATTN_BLOCK, 0), (0, 0), (0, 0)))
        tp = tp.reshape(B, nb + 1, ATTN_BLOCK, N_KV_HEADS, HEAD_DIM)
        return jnp.concatenate([tp[:, :-1], tp[:, 1:]], axis=2)

    kb, vb = band(k), band(v)
    scores = jnp.einsum('bnqkgd,bnskd->bnkgqs', qb, kb).astype(jnp.float32) * (HEAD_DIM ** -0.5)
    qi = jnp.arange(ATTN_BLOCK)[:, None]
    kj = jnp.arange(2 * ATTN_BLOCK)[None, :]
    blk = jnp.arange(nb)[:, None, None]
    dist = qi + ATTN_BLOCK - kj
    mask = (dist >= 0) & (dist < WINDOW) & (blk * ATTN_BLOCK + kj - ATTN_BLOCK >= 0)
    scores = jnp.where(mask[None, :, None, None], scores, NEG_INF)
    sink = sinks.astype(jnp.float32).reshape(1, 1, N_KV_HEADS, grp, 1, 1)
    m = jnp.maximum(jnp.max(scores, axis=-1, keepdims=True), sink)
    p = jnp.exp(scores - m)
    denom = jnp.sum(p, axis=-1, keepdims=True) + jnp.exp(sink - m)
    probs = (p / denom).astype(v.dtype)
    out = jnp.einsum('bnkgqs,bnskd->bnqkgd', probs, vb)
    return out.reshape(B, S, ATTN_WIDTH)


def hgrn2_recurrence(q, f_logit, v, lb):
    B, S = q.shape[0], q.shape[1]
    nc = S // HGRN_CHUNK
    lb = lb.reshape(HGRN_HEADS, HGRN_DK)
    f = lb + (1.0 - lb) * jax.nn.sigmoid(f_logit.astype(jnp.float32))
    k = 1.0 - f

    def chunks(t):
        return t.reshape(B, nc, HGRN_CHUNK, HGRN_HEADS, t.shape[-1]).transpose(0, 3, 1, 2, 4)

    qc, kc, vc = chunks(q.astype(jnp.float32)), chunks(k), chunks(v.astype(jnp.float32))
    gc = jnp.cumsum(chunks(jnp.log(f)), axis=3)
    g_last = gc[:, :, :, -1:, :]
    q_dec = qc * jnp.exp(gc)
    k_inv = kc * jnp.exp(-gc)
    k_end = kc * jnp.exp(g_last - gc)
    causal = jnp.tril(jnp.ones((HGRN_CHUNK, HGRN_CHUNK), dtype=bool))
    scores = jnp.where(causal, jnp.einsum('bhntd,bhnsd->bhnts', q_dec, k_inv), 0.0)
    o_intra = jnp.einsum('bhnts,bhnse->bhnte', scores, vc)
    upd = jnp.einsum('bhnsd,bhnse->bhnde', k_end, vc)
    decay = jnp.exp(g_last[:, :, :, 0, :])

    def step(state, inp):
        a_n, u_n = inp
        return state * a_n[..., None] + u_n, state

    s0 = jnp.zeros((B, HGRN_HEADS, HGRN_DK, HGRN_DV), jnp.float32)
    _, s_start = lax.scan(step, s0, (jnp.moveaxis(decay, 2, 0), jnp.moveaxis(upd, 2, 0)))
    s_start = jnp.moveaxis(s_start, 0, 2)
    o = o_intra + jnp.einsum('bhntd,bhnde->bhnte', q_dec, s_start)
    return o.transpose(0, 2, 3, 1, 4).reshape(B, S, HGRN_HEADS, HGRN_DV)


def token_mixer(h, w_in, b_in, sinks, lb, norm_g, w_pa, w_ph, w_out, cos, sin):
    B, S = h.shape[0], h.shape[1]
    proj = h @ w_in + b_in
    splits = np.cumsum(IN_WIDTHS)[:-1].tolist()
    q_a, k_a, v_a, f_h, q_h, i_h, og_h, gate_a, gate_h = jnp.split(proj, splits, axis=-1)
    q_a = partial_rope(q_a.reshape(B, S, N_Q_HEADS, HEAD_DIM), cos, sin)
    k_a = partial_rope(k_a.reshape(B, S, N_KV_HEADS, HEAD_DIM), cos, sin)
    v_a = v_a.reshape(B, S, N_KV_HEADS, HEAD_DIM)
    y_attn = sliding_window_attention(q_a, k_a, v_a, sinks)
    q_h = jax.nn.silu(q_h).reshape(B, S, HGRN_HEADS, HGRN_DK)
    o_h = hgrn2_recurrence(q_h, f_h.reshape(B, S, HGRN_HEADS, HGRN_DK),
                           i_h.reshape(B, S, HGRN_HEADS, HGRN_DV), lb)
    o_h = o_h * lax.rsqrt(jnp.mean(jnp.square(o_h), axis=-1, keepdims=True) + RMS_EPS) * norm_g.astype(jnp.float32)
    y_hgrn = o_h.reshape(B, S, HGRN_VWIDTH).astype(h.dtype) * jax.nn.silu(og_h)
    merged = jax.nn.sigmoid(gate_a) * (y_attn @ w_pa) + jax.nn.sigmoid(gate_h) * (y_hgrn @ w_ph)
    return merged @ w_out


def _fwd_setup_inputs(seed: int = 0) -> dict:
    key = jax.random.key(seed)
    ks = jax.random.split(key, 24)

    def nrm(k, shape, scale):
        return jax.random.normal(k, shape, jnp.float32) * scale

    D = D_MODEL
    return {
        'x': nrm(ks[0], (BATCH, SEQ, D), 1.0),
        'ln1_g': 1.0 + nrm(ks[1], (DEPTH, D), 0.02),
        'ln1_b': nrm(ks[2], (DEPTH, D), 0.02),
        'ffn1_w1': nrm(ks[3], (DEPTH, D, D_FF), D ** -0.5),
        'ffn1_w3': nrm(ks[4], (DEPTH, D, D_FF), D ** -0.5),
        'ffn1_w2': nrm(ks[5], (DEPTH, D_FF, D), D_FF ** -0.5 * DEEPNORM_BETA),
        'ln2_g': 1.0 + nrm(ks[6], (DEPTH, D), 0.02),
        'ln2_b': nrm(ks[7], (DEPTH, D), 0.02),
        'w_in': nrm(ks[8], (DEPTH, D, D_IN), D ** -0.5),
        'b_in': nrm(ks[9], (DEPTH, D_IN), 0.02),
        'attn_sinks': nrm(ks[10], (DEPTH, N_Q_HEADS), 0.5),
        'hgrn_lb_logits': nrm(ks[11], (DEPTH + 1, HGRN_KWIDTH), 0.1),
        'hgrn_norm_g': 1.0 + nrm(ks[12], (DEPTH, HGRN_DV), 0.02),
        'w_proj_attn': nrm(ks[13], (DEPTH, ATTN_WIDTH, D), ATTN_WIDTH ** -0.5 * DEEPNORM_BETA),
        'w_proj_hgrn': nrm(ks[14], (DEPTH, HGRN_VWIDTH, D), HGRN_VWIDTH ** -0.5 * DEEPNORM_BETA),
        'w_out': nrm(ks[15], (DEPTH, D, D), D ** -0.5 * DEEPNORM_BETA),
        'ln3_g': 1.0 + nrm(ks[16], (DEPTH, D), 0.02),
        'ln3_b': nrm(ks[17], (DEPTH, D), 0.02),
        'ffn2_w1': nrm(ks[18], (DEPTH, D, D_FF), D ** -0.5),
        'ffn2_w3': nrm(ks[19], (DEPTH, D, D_FF), D ** -0.5),
        'ffn2_w2': nrm(ks[20], (DEPTH, D_FF, D), D_FF ** -0.5 * DEEPNORM_BETA),
    }


def _fwd_reference(x, ln1_g, ln1_b, ffn1_w1, ffn1_w3, ffn1_w2, ln2_g, ln2_b, w_in, b_in,
              attn_sinks, hgrn_lb_logits, hgrn_norm_g, w_proj_attn, w_proj_hgrn, w_out,
              ln3_g, ln3_b, ffn2_w1, ffn2_w3, ffn2_w2):
    cos, sin = rope_tables(x.shape[1])
    lb_all = jnp.cumsum(jax.nn.softmax(hgrn_lb_logits.astype(jnp.float32), axis=0), axis=0)
    for l in range(DEPTH):
        x = layer_norm(DEEPNORM_ALPHA * x + 0.5 * swiglu(x, ffn1_w1[l], ffn1_w3[l], ffn1_w2[l]),
                       ln1_g[l], ln1_b[l])
        mix = token_mixer(x, w_in[l], b_in[l], attn_sinks[l], lb_all[l], hgrn_norm_g[l],
                          w_proj_attn[l], w_proj_hgrn[l], w_out[l], cos, sin)
        x = layer_norm(DEEPNORM_ALPHA * x + mix, ln2_g[l], ln2_b[l])
        x = layer_norm(DEEPNORM_ALPHA * x + 0.5 * swiglu(x, ffn2_w1[l], ffn2_w3[l], ffn2_w2[l]),
                       ln3_g[l], ln3_b[l])
    return x


import jax as _jax
import jax.numpy as _jnp

TWIN_FORMAT = 'train_step'
FWD_PARAMS = ['x', 'ln1_g', 'ln1_b', 'ffn1_w1', 'ffn1_w3', 'ffn1_w2', 'ln2_g', 'ln2_b', 'w_in', 'b_in', 'attn_sinks', 'hgrn_lb_logits', 'hgrn_norm_g', 'w_proj_attn', 'w_proj_hgrn', 'w_out', 'ln3_g', 'ln3_b', 'ffn2_w1', 'ffn2_w3', 'ffn2_w2']
TWIN_WEIGHTS = ['ln1_g', 'ln1_b', 'ffn1_w1', 'ffn1_w3', 'ffn1_w2', 'ln2_g', 'ln2_b', 'w_in', 'b_in', 'attn_sinks', 'hgrn_lb_logits', 'hgrn_norm_g', 'w_proj_attn', 'w_proj_hgrn', 'w_out', 'ln3_g', 'ln3_b', 'ffn2_w1', 'ffn2_w3', 'ffn2_w2']
TWIN_DIFF_INPUT = 'x'
TWIN_INPUTS = ['x', 'ln1_g', 'ln1_b', 'ffn1_w1', 'ffn1_w3', 'ffn1_w2', 'ln2_g', 'ln2_b', 'w_in', 'b_in', 'attn_sinks', 'hgrn_lb_logits', 'hgrn_norm_g', 'w_proj_attn', 'w_proj_hgrn', 'w_out', 'ln3_g', 'ln3_b', 'ffn2_w1', 'ffn2_w3', 'ffn2_w2', 'loss_target', 'm_ln1_g', 'm_ln1_b', 'm_ffn1_w1', 'm_ffn1_w3', 'm_ffn1_w2', 'm_ln2_g', 'm_ln2_b', 'm_w_in', 'm_b_in', 'm_attn_sinks', 'm_hgrn_lb_logits', 'm_hgrn_norm_g', 'm_w_proj_attn', 'm_w_proj_hgrn', 'm_w_out', 'm_ln3_g', 'm_ln3_b', 'm_ffn2_w1', 'm_ffn2_w3', 'm_ffn2_w2', 'v_ln1_g', 'v_ln1_b', 'v_ffn1_w1', 'v_ffn1_w3', 'v_ffn1_w2', 'v_ln2_g', 'v_ln2_b', 'v_w_in', 'v_b_in', 'v_attn_sinks', 'v_hgrn_lb_logits', 'v_hgrn_norm_g', 'v_w_proj_attn', 'v_w_proj_hgrn', 'v_w_out', 'v_ln3_g', 'v_ln3_b', 'v_ffn2_w1', 'v_ffn2_w3', 'v_ffn2_w2']
TWIN_OUTPUTS = ['loss', 'grad_x', 'grad_ln1_g', 'grad_ln1_b', 'grad_ffn1_w1', 'grad_ffn1_w3', 'grad_ffn1_w2', 'grad_ln2_g', 'grad_ln2_b', 'grad_w_in', 'grad_b_in', 'grad_attn_sinks', 'grad_hgrn_lb_logits', 'grad_hgrn_norm_g', 'grad_w_proj_attn', 'grad_w_proj_hgrn', 'grad_w_out', 'grad_ln3_g', 'grad_ln3_b', 'grad_ffn2_w1', 'grad_ffn2_w3', 'grad_ffn2_w2', 'delta_ln1_g', 'delta_ln1_b', 'delta_ffn1_w1', 'delta_ffn1_w3', 'delta_ffn1_w2', 'delta_ln2_g', 'delta_ln2_b', 'delta_w_in', 'delta_b_in', 'delta_attn_sinks', 'delta_hgrn_lb_logits', 'delta_hgrn_norm_g', 'delta_w_proj_attn', 'delta_w_proj_hgrn', 'delta_w_out', 'delta_ln3_g', 'delta_ln3_b', 'delta_ffn2_w1', 'delta_ffn2_w3', 'delta_ffn2_w2', 'new_m_ln1_g', 'new_m_ln1_b', 'new_m_ffn1_w1', 'new_m_ffn1_w3', 'new_m_ffn1_w2', 'new_m_ln2_g', 'new_m_ln2_b', 'new_m_w_in', 'new_m_b_in', 'new_m_attn_sinks', 'new_m_hgrn_lb_logits', 'new_m_hgrn_norm_g', 'new_m_w_proj_attn', 'new_m_w_proj_hgrn', 'new_m_w_out', 'new_m_ln3_g', 'new_m_ln3_b', 'new_m_ffn2_w1', 'new_m_ffn2_w3', 'new_m_ffn2_w2', 'new_v_ln1_g', 'new_v_ln1_b', 'new_v_ffn1_w1', 'new_v_ffn1_w3', 'new_v_ffn1_w2', 'new_v_ln2_g', 'new_v_ln2_b', 'new_v_w_in', 'new_v_b_in', 'new_v_attn_sinks', 'new_v_hgrn_lb_logits', 'new_v_hgrn_norm_g', 'new_v_w_proj_attn', 'new_v_w_proj_hgrn', 'new_v_w_out', 'new_v_ln3_g', 'new_v_ln3_b', 'new_v_ffn2_w1', 'new_v_ffn2_w3', 'new_v_ffn2_w2']
TWIN_LEAF_KINDS = {'loss': 'loss', 'grad_x': 'grad_x', 'grad_ln1_g': 'grad_w', 'grad_ln1_b': 'grad_w', 'grad_ffn1_w1': 'grad_w', 'grad_ffn1_w3': 'grad_w', 'grad_ffn1_w2': 'grad_w', 'grad_ln2_g': 'grad_w', 'grad_ln2_b': 'grad_w', 'grad_w_in': 'grad_w', 'grad_b_in': 'grad_w', 'grad_attn_sinks': 'grad_w', 'grad_hgrn_lb_logits': 'grad_w', 'grad_hgrn_norm_g': 'grad_w', 'grad_w_proj_attn': 'grad_w', 'grad_w_proj_hgrn': 'grad_w', 'grad_w_out': 'grad_w', 'grad_ln3_g': 'grad_w', 'grad_ln3_b': 'grad_w', 'grad_ffn2_w1': 'grad_w', 'grad_ffn2_w3': 'grad_w', 'grad_ffn2_w2': 'grad_w', 'delta_ln1_g': 'delta_w', 'delta_ln1_b': 'delta_w', 'delta_ffn1_w1': 'delta_w', 'delta_ffn1_w3': 'delta_w', 'delta_ffn1_w2': 'delta_w', 'delta_ln2_g': 'delta_w', 'delta_ln2_b': 'delta_w', 'delta_w_in': 'delta_w', 'delta_b_in': 'delta_w', 'delta_attn_sinks': 'delta_w', 'delta_hgrn_lb_logits': 'delta_w', 'delta_hgrn_norm_g': 'delta_w', 'delta_w_proj_attn': 'delta_w', 'delta_w_proj_hgrn': 'delta_w', 'delta_w_out': 'delta_w', 'delta_ln3_g': 'delta_w', 'delta_ln3_b': 'delta_w', 'delta_ffn2_w1': 'delta_w', 'delta_ffn2_w3': 'delta_w', 'delta_ffn2_w2': 'delta_w', 'new_m_ln1_g': 'new_m', 'new_m_ln1_b': 'new_m', 'new_m_ffn1_w1': 'new_m', 'new_m_ffn1_w3': 'new_m', 'new_m_ffn1_w2': 'new_m', 'new_m_ln2_g': 'new_m', 'new_m_ln2_b': 'new_m', 'new_m_w_in': 'new_m', 'new_m_b_in': 'new_m', 'new_m_attn_sinks': 'new_m', 'new_m_hgrn_lb_logits': 'new_m', 'new_m_hgrn_norm_g': 'new_m', 'new_m_w_proj_attn': 'new_m', 'new_m_w_proj_hgrn': 'new_m', 'new_m_w_out': 'new_m', 'new_m_ln3_g': 'new_m', 'new_m_ln3_b': 'new_m', 'new_m_ffn2_w1': 'new_m', 'new_m_ffn2_w3': 'new_m', 'new_m_ffn2_w2': 'new_m', 'new_v_ln1_g': 'new_v', 'new_v_ln1_b': 'new_v', 'new_v_ffn1_w1': 'new_v', 'new_v_ffn1_w3': 'new_v', 'new_v_ffn1_w2': 'new_v', 'new_v_ln2_g': 'new_v', 'new_v_ln2_b': 'new_v', 'new_v_w_in': 'new_v', 'new_v_b_in': 'new_v', 'new_v_attn_sinks': 'new_v', 'new_v_hgrn_lb_logits': 'new_v', 'new_v_hgrn_norm_g': 'new_v', 'new_v_w_proj_attn': 'new_v', 'new_v_w_proj_hgrn': 'new_v', 'new_v_w_out': 'new_v', 'new_v_ln3_g': 'new_v', 'new_v_ln3_b': 'new_v', 'new_v_ffn2_w1': 'new_v', 'new_v_ffn2_w3': 'new_v', 'new_v_ffn2_w2': 'new_v'}


def _forward(args):
    return _fwd_reference(*[args[k] for k in FWD_PARAMS])


def _output_shape():
    out = _jax.eval_shape(lambda: _forward(_fwd_setup_inputs(0)))
    return out.shape, out.dtype

N_MICROBATCH = 1
ADAM_LR = 0.001
ADAM_B1 = 0.9
ADAM_B2 = 0.999
ADAM_EPS = 1e-08
ADAM_WD = 0.01
ADAM_STEP = 10
PER_EXAMPLE_BATCH_AXIS = {'x': 0, 'loss_target': 0}
SHARED_INPUTS = []
_WEIGHT_DTYPES = {'ln1_g': _jnp.float32, 'ln1_b': _jnp.float32, 'ffn1_w1': _jnp.float32, 'ffn1_w3': _jnp.float32, 'ffn1_w2': _jnp.float32, 'ln2_g': _jnp.float32, 'ln2_b': _jnp.float32, 'w_in': _jnp.float32, 'b_in': _jnp.float32, 'attn_sinks': _jnp.float32, 'hgrn_lb_logits': _jnp.float32, 'hgrn_norm_g': _jnp.float32, 'w_proj_attn': _jnp.float32, 'w_proj_hgrn': _jnp.float32, 'w_out': _jnp.float32, 'ln3_g': _jnp.float32, 'ln3_b': _jnp.float32, 'ffn2_w1': _jnp.float32, 'ffn2_w3': _jnp.float32, 'ffn2_w2': _jnp.float32}
MOMENT_SCALE = {'ln1_g': 1.940681e+00, 'ln1_b': 7.176994e-01, 'ffn1_w1': 2.340945e-02, 'ffn1_w3': 2.275077e-02, 'ffn1_w2': 6.324974e-02, 'ln2_g': 1.978312e+00, 'ln2_b': 7.318569e-01, 'w_in': 1.423607e-02, 'b_in': 3.632105e-02, 'attn_sinks': 4.608703e-03, 'hgrn_lb_logits': 2.416888e-03, 'hgrn_norm_g': 7.444128e-02, 'w_proj_attn': 1.421682e-02, 'w_proj_hgrn': 4.084535e-02, 'w_out': 4.325433e-02, 'ln3_g': 6.405518e+01, 'ln3_b': 1.429588e+00, 'ffn2_w1': 2.302583e-02, 'ffn2_w3': 2.238892e-02, 'ffn2_w2': 6.245136e-02}


def _to_microbatches(a, axis):
    t = _jnp.moveaxis(a, axis, 0)
    t = t.reshape((N_MICROBATCH, t.shape[0] // N_MICROBATCH) + t.shape[1:])
    return _jnp.moveaxis(t, 1, axis + 1)


def setup_inputs(seed: int = 0) -> dict:
    inp = _fwd_setup_inputs(seed)
    key = _jax.random.fold_in(_jax.random.key(seed), 7919)
    shape, _ = _output_shape()
    out = dict(inp)
    out["loss_target"] = _jax.random.normal(_jax.random.fold_in(key, 0), shape, _jnp.float32)
    for i, name in enumerate(TWIN_WEIGHTS):
        w = inp[name].astype(_jnp.float32)
        if MOMENT_SCALE is None:
            s = _jnp.sqrt(_jnp.mean(_jnp.square(w)) + 1e-30)
        else:
            s = MOMENT_SCALE[name]
        km, kv = _jax.random.split(_jax.random.fold_in(key, i + 1))
        out[name] = w
        out["m_" + name] = s * _jax.random.normal(km, w.shape, _jnp.float32)
        out["v_" + name] = (s * s) * _jax.random.uniform(kv, w.shape, _jnp.float32, 0.5, 1.5)
    if N_MICROBATCH > 1:
        for name, axis in PER_EXAMPLE_BATCH_AXIS.items():
            out[name] = _to_microbatches(out[name], axis)
    return {'x': out['x'], 'ln1_g': out['ln1_g'], 'ln1_b': out['ln1_b'], 'ffn1_w1': out['ffn1_w1'], 'ffn1_w3': out['ffn1_w3'], 'ffn1_w2': out['ffn1_w2'], 'ln2_g': out['ln2_g'], 'ln2_b': out['ln2_b'], 'w_in': out['w_in'], 'b_in': out['b_in'], 'attn_sinks': out['attn_sinks'], 'hgrn_lb_logits': out['hgrn_lb_logits'], 'hgrn_norm_g': out['hgrn_norm_g'], 'w_proj_attn': out['w_proj_attn'], 'w_proj_hgrn': out['w_proj_hgrn'], 'w_out': out['w_out'], 'ln3_g': out['ln3_g'], 'ln3_b': out['ln3_b'], 'ffn2_w1': out['ffn2_w1'], 'ffn2_w3': out['ffn2_w3'], 'ffn2_w2': out['ffn2_w2'], 'loss_target': out['loss_target'], 'm_ln1_g': out['m_ln1_g'], 'm_ln1_b': out['m_ln1_b'], 'm_ffn1_w1': out['m_ffn1_w1'], 'm_ffn1_w3': out['m_ffn1_w3'], 'm_ffn1_w2': out['m_ffn1_w2'], 'm_ln2_g': out['m_ln2_g'], 'm_ln2_b': out['m_ln2_b'], 'm_w_in': out['m_w_in'], 'm_b_in': out['m_b_in'], 'm_attn_sinks': out['m_attn_sinks'], 'm_hgrn_lb_logits': out['m_hgrn_lb_logits'], 'm_hgrn_norm_g': out['m_hgrn_norm_g'], 'm_w_proj_attn': out['m_w_proj_attn'], 'm_w_proj_hgrn': out['m_w_proj_hgrn'], 'm_w_out': out['m_w_out'], 'm_ln3_g': out['m_ln3_g'], 'm_ln3_b': out['m_ln3_b'], 'm_ffn2_w1': out['m_ffn2_w1'], 'm_ffn2_w3': out['m_ffn2_w3'], 'm_ffn2_w2': out['m_ffn2_w2'], 'v_ln1_g': out['v_ln1_g'], 'v_ln1_b': out['v_ln1_b'], 'v_ffn1_w1': out['v_ffn1_w1'], 'v_ffn1_w3': out['v_ffn1_w3'], 'v_ffn1_w2': out['v_ffn1_w2'], 'v_ln2_g': out['v_ln2_g'], 'v_ln2_b': out['v_ln2_b'], 'v_w_in': out['v_w_in'], 'v_b_in': out['v_b_in'], 'v_attn_sinks': out['v_attn_sinks'], 'v_hgrn_lb_logits': out['v_hgrn_lb_logits'], 'v_hgrn_norm_g': out['v_hgrn_norm_g'], 'v_w_proj_attn': out['v_w_proj_attn'], 'v_w_proj_hgrn': out['v_w_proj_hgrn'], 'v_w_out': out['v_w_out'], 'v_ln3_g': out['v_ln3_g'], 'v_ln3_b': out['v_ln3_b'], 'v_ffn2_w1': out['v_ffn2_w1'], 'v_ffn2_w3': out['v_ffn2_w3'], 'v_ffn2_w2': out['v_ffn2_w2']}


def _loss(weights, diff, rest, loss_target):
    with _jax.named_scope("forward"):
        args = {**rest, TWIN_DIFF_INPUT: diff, **{k: w.astype(_WEIGHT_DTYPES[k]) for k, w in weights.items()}}
        y = _forward(args)
    with _jax.named_scope("loss_head"):
        err = _jnp.square(y.astype(_jnp.float32) - loss_target)
        return 0.5 * _jnp.sum(_jnp.mean(err, axis=-1)) if err.ndim else 0.5 * err


def _adamw(w, g, m, v):
    m = ADAM_B1 * m + (1.0 - ADAM_B1) * g
    v = ADAM_B2 * v + (1.0 - ADAM_B2) * _jnp.square(g)
    m_hat = m / (1.0 - ADAM_B1 ** ADAM_STEP)
    v_hat = v / (1.0 - ADAM_B2 ** ADAM_STEP)
    delta = -ADAM_LR * (m_hat / (_jnp.sqrt(v_hat) + ADAM_EPS) + ADAM_WD * w)
    return delta, m, v


def reference(x, ln1_g, ln1_b, ffn1_w1, ffn1_w3, ffn1_w2, ln2_g, ln2_b, w_in, b_in, attn_sinks, hgrn_lb_logits, hgrn_norm_g, w_proj_attn, w_proj_hgrn, w_out, ln3_g, ln3_b, ffn2_w1, ffn2_w3, ffn2_w2, loss_target, m_ln1_g, m_ln1_b, m_ffn1_w1, m_ffn1_w3, m_ffn1_w2, m_ln2_g, m_ln2_b, m_w_in, m_b_in, m_attn_sinks, m_hgrn_lb_logits, m_hgrn_norm_g, m_w_proj_attn, m_w_proj_hgrn, m_w_out, m_ln3_g, m_ln3_b, m_ffn2_w1, m_ffn2_w3, m_ffn2_w2, v_ln1_g, v_ln1_b, v_ffn1_w1, v_ffn1_w3, v_ffn1_w2, v_ln2_g, v_ln2_b, v_w_in, v_b_in, v_attn_sinks, v_hgrn_lb_logits, v_hgrn_norm_g, v_w_proj_attn, v_w_proj_hgrn, v_w_out, v_ln3_g, v_ln3_b, v_ffn2_w1, v_ffn2_w3, v_ffn2_w2):
    given = dict(x=x, ln1_g=ln1_g, ln1_b=ln1_b, ffn1_w1=ffn1_w1, ffn1_w3=ffn1_w3, ffn1_w2=ffn1_w2, ln2_g=ln2_g, ln2_b=ln2_b, w_in=w_in, b_in=b_in, attn_sinks=attn_sinks, hgrn_lb_logits=hgrn_lb_logits, hgrn_norm_g=hgrn_norm_g, w_proj_attn=w_proj_attn, w_proj_hgrn=w_proj_hgrn, w_out=w_out, ln3_g=ln3_g, ln3_b=ln3_b, ffn2_w1=ffn2_w1, ffn2_w3=ffn2_w3, ffn2_w2=ffn2_w2, loss_target=loss_target, m_ln1_g=m_ln1_g, m_ln1_b=m_ln1_b, m_ffn1_w1=m_ffn1_w1, m_ffn1_w3=m_ffn1_w3, m_ffn1_w2=m_ffn1_w2, m_ln2_g=m_ln2_g, m_ln2_b=m_ln2_b, m_w_in=m_w_in, m_b_in=m_b_in, m_attn_sinks=m_attn_sinks, m_hgrn_lb_logits=m_hgrn_lb_logits, m_hgrn_norm_g=m_hgrn_norm_g, m_w_proj_attn=m_w_proj_attn, m_w_proj_hgrn=m_w_proj_hgrn, m_w_out=m_w_out, m_ln3_g=m_ln3_g, m_ln3_b=m_ln3_b, m_ffn2_w1=m_ffn2_w1, m_ffn2_w3=m_ffn2_w3, m_ffn2_w2=m_ffn2_w2, v_ln1_g=v_ln1_g, v_ln1_b=v_ln1_b, v_ffn1_w1=v_ffn1_w1, v_ffn1_w3=v_ffn1_w3, v_ffn1_w2=v_ffn1_w2, v_ln2_g=v_ln2_g, v_ln2_b=v_ln2_b, v_w_in=v_w_in, v_b_in=v_b_in, v_attn_sinks=v_attn_sinks, v_hgrn_lb_logits=v_hgrn_lb_logits, v_hgrn_norm_g=v_hgrn_norm_g, v_w_proj_attn=v_w_proj_attn, v_w_proj_hgrn=v_w_proj_hgrn, v_w_out=v_w_out, v_ln3_g=v_ln3_g, v_ln3_b=v_ln3_b, v_ffn2_w1=v_ffn2_w1, v_ffn2_w3=v_ffn2_w3, v_ffn2_w2=v_ffn2_w2)
    weights = {n: given[n] for n in TWIN_WEIGHTS}
    shared = {n: given[n] for n in SHARED_INPUTS}
    per_example = {n: given[n] for n in ['x']}
    grad_fn = _jax.value_and_grad(_loss, argnums=(0, 1))

    def one_microbatch(ex, loss_target):
        ex = dict(ex)
        diff = ex.pop(TWIN_DIFF_INPUT)
        return grad_fn(weights, diff, {**shared, **ex}, loss_target)

    if N_MICROBATCH == 1:
        loss, (grad_w, grad_x) = one_microbatch(per_example, given["loss_target"])
    else:
        def body(carry, xs):
            loss_sum, grad_sum = carry
            l_k, (gw_k, gx_k) = one_microbatch(xs[0], xs[1])
            with _jax.named_scope("update"):
                return (loss_sum + l_k, _jax.tree.map(_jnp.add, grad_sum, gw_k)), gx_k

        init = (_jnp.zeros((), _jnp.float32), _jax.tree.map(_jnp.zeros_like, weights))
        (loss, grad_w), grad_x = _jax.lax.scan(body, init, (per_example, given["loss_target"]))
    with _jax.named_scope("update"):
        delta_w, new_m, new_v = {}, {}, {}
        for n in TWIN_WEIGHTS:
            delta_w[n], new_m[n], new_v[n] = _adamw(weights[n], grad_w[n], given["m_" + n], given["v_" + n])
    return (loss, grad_x, *[grad_w[n] for n in TWIN_WEIGHTS], *[delta_w[n] for n in TWIN_WEIGHTS],
            *[new_m[n] for n in TWIN_WEIGHTS], *[new_v[n] for n in TWIN_WEIGHTS])
```

```python
import functools

import jax
import jax.numpy as jnp
from jax import lax
from jax.experimental import pallas as pl
from jax.experimental.pallas import tpu as pltpu

F32 = jnp.float32
BF16 = jnp.bfloat16

D_MODEL = 1024
N_Q_HEADS = 16
N_KV_HEADS = 4
HEAD_DIM = 64
ATTN_BLOCK = 128
ROPE_THETA = 500000.0
ROPE_DIM = HEAD_DIM // 4
HGRN_HEADS = 8
HGRN_DK = 128
HGRN_CHUNK = 64
D_FF = 2816
D_IN = 7680
DEEPNORM_ALPHA = 2 ** 0.25
LN_EPS = 1e-5
RMS_EPS = 1e-6
NEG_INF = -1e30

ADAM_LR = 0.001
ADAM_B1 = 0.9
ADAM_B2 = 0.999
ADAM_EPS = 1e-08
ADAM_WD = 0.01
ADAM_STEP = 10

N_CHIPS = 4
N_DEV = 8
FF_SHARD = D_FF // N_CHIPS
LANES = 128
MXU_COLS = 256
FF_PAD = -(-FF_SHARD // MXU_COLS) * MXU_COLS
IN_SHARD = D_IN // N_CHIPS
PROJ_SHARD = D_MODEL // N_CHIPS
ROW_TILE = 512
MIX_TILE = 256
UPDATE_ROWS = 128
HGRN_CHUNKS_PER_STEP = 8
VMEM_LIMIT = 56 * 1024 * 1024

COL_K = 1024 // 256
COL_V = 1280 // 256
COL_F = 1536 // LANES
COL_QH = 2560 // LANES
COL_IH = 3584 // LANES
COL_OG = 4608 // LANES
COL_GA = 5632 // 512
COL_GH = 6656 // 512

SM_LN = 0
SM_BIN = 6 * D_MODEL
SM_SINK = SM_BIN + D_IN
SM_NG = SM_SINK + LANES
SM_LB = SM_NG + LANES
SM_LOSS = SM_LB + D_MODEL
SM_LEN = SM_LOSS + LANES
PK_LB0 = SM_LB
PK_LB1 = SM_LB + D_MODEL
PK_LEN = PK_LB1 + D_MODEL

MESH = pl.DeviceIdType.MESH


def _mm(a, b):
    return lax.dot_general(a, b, (((1,), (0,)), ((), ())), preferred_element_type=F32)


def _mm_nt(a, b):
    return lax.dot_general(a, b, (((1,), (1,)), ((), ())), preferred_element_type=F32)


def _mm_tn(a, b):
    return lax.dot_general(a, b, (((0,), (0,)), ((), ())), preferred_element_type=F32)


def _bf(v):
    return v.astype(BF16)


def _sig(v):
    return jax.nn.sigmoid(v)


def _ln(z, g, b):
    mu = jnp.mean(z, axis=-1, keepdims=True)
    zc = z - mu
    var = jnp.mean(zc * zc, axis=-1, keepdims=True)
    return zc * lax.rsqrt(var + LN_EPS) * g + b


def _swiglu_act(h1, h3):
    return (h1 * _sig(h1)) * h3


def _params(sem=None):
    return pltpu.CompilerParams(dimension_semantics=sem, vmem_limit_bytes=VMEM_LIMIT)


def _full(shape):
    nd = len(shape)
    return pl.BlockSpec(shape, lambda *_: (0,) * nd)


def _update_rows(rows):
    return max(t for t in range(8, UPDATE_ROWS + 1, 8) if rows % t == 0)


def _ffn_fwd(name, x, w1g, w3g, w2g, g, b, target=None):
    n = x.shape[0]
    tm = min(ROW_TILE, n)
    final = target is not None

    def body(*refs):
        if final:
            x_ref, w1_ref, w3_ref, w2_ref, g_ref, b_ref, t_ref, z_ref, o_ref, loss_ref, acc, xb = refs
        else:
            x_ref, w1_ref, w3_ref, w2_ref, g_ref, b_ref, z_ref, o_ref, acc, xb = refs
        i = pl.program_id(0)
        j = pl.program_id(1)

        @pl.when(j == 0)
        def _():
            xb[...] = _bf(x_ref[...])
            acc[...] = jnp.zeros_like(acc)

        h1 = _mm(xb[...], w1_ref[0])
        h3 = _mm(xb[...], w3_ref[0])
        acc[...] += _mm(_bf(_swiglu_act(h1, h3)), w2_ref[0])

        @pl.when(j == N_CHIPS - 1)
        def _():
            z = DEEPNORM_ALPHA * x_ref[...] + 0.5 * acc[...]
            z_ref[...] = z
            y = _ln(z, g_ref[...], b_ref[...])
            if final:
                e = y - t_ref[...]

                @pl.when(i == 0)
                def _():
                    loss_ref[...] = jnp.zeros_like(loss_ref)

                loss_ref[...] += jnp.sum(e * e) * (0.5 / D_MODEL)
                o_ref[...] = e * (1.0 / D_MODEL)
            else:
                o_ref[...] = y

    row = pl.BlockSpec((tm, D_MODEL), lambda i, j: (i, 0))
    wcol = pl.BlockSpec((1, D_MODEL, FF_PAD), lambda i, j: (j, 0, 0))
    wrow = pl.BlockSpec((1, FF_PAD, D_MODEL), lambda i, j: (j, 0, 0))
    vec = pl.BlockSpec((1, D_MODEL), lambda i, j: (0, 0))
    in_specs = [row, wcol, wcol, wrow, vec, vec]
    args = [x, w1g, w3g, w2g, g, b]
    out_specs = [row, row]
    out_shape = [jax.ShapeDtypeStruct((n, D_MODEL), F32)] * 2
    if final:
        in_specs.append(row)
        args.append(target)
        out_specs.append(pl.BlockSpec((1, LANES), lambda i, j: (0, 0)))
        out_shape.append(jax.ShapeDtypeStruct((1, LANES), F32))
    return pl.pallas_call(
        body, name=name, grid=(n // tm, N_CHIPS), in_specs=in_specs, out_specs=out_specs, out_shape=out_shape,
        scratch_shapes=[pltpu.VMEM((tm, D_MODEL), F32), pltpu.VMEM((tm, D_MODEL), BF16)],
        compiler_params=_params(("arbitrary", "arbitrary")),
    )(*args)


def _ffn_bwd(name, x, z, dout, w1g, w3g, w2g, g, b):
    n = x.shape[0]
    tm = min(ROW_TILE, n)

    def body(x_ref, z_ref, do_ref, w1_ref, w3_ref, w2_ref, g_ref, b_ref,
             dx_ref, a_ref, dh1_ref, dh3_ref, df_ref, dg_ref, db_ref, acc, xb, dzs):
        i = pl.program_id(0)
        j = pl.program_id(1)

        @pl.when(j == 0)
        def _():
            xb[...] = _bf(x_ref[...])
            acc[...] = jnp.zeros_like(acc)
            _, vjp = jax.vjp(_ln, z_ref[...], g_ref[...], b_ref[...])
            dz, dg, db = vjp(do_ref[...])
            dzs[...] = dz
            df_ref[...] = _bf(0.5 * dz)

            @pl.when(i == 0)
            def _():
                dg_ref[...] = jnp.zeros_like(dg_ref)
                db_ref[...] = jnp.zeros_like(db_ref)

            dg_ref[...] += dg
            db_ref[...] += db

        h1 = _mm(xb[...], w1_ref[0])
        h3 = _mm(xb[...], w3_ref[0])
        a, act_vjp = jax.vjp(_swiglu_act, h1, h3)
        da = _mm_nt(df_ref[...], w2_ref[0])
        dh1, dh3 = act_vjp(da)
        dh1 = _bf(dh1)
        dh3 = _bf(dh3)
        a_ref[0] = _bf(a)
        dh1_ref[0] = dh1
        dh3_ref[0] = dh3
        acc[...] += _mm_nt(dh1, w1_ref[0]) + _mm_nt(dh3, w3_ref[0])

        @pl.when(j == N_CHIPS - 1)
        def _():
            dx_ref[...] = DEEPNORM_ALPHA * dzs[...] + acc[...]

    row = pl.BlockSpec((tm, D_MODEL), lambda i, j: (i, 0))
    wcol = pl.BlockSpec((1, D_MODEL, FF_PAD), lambda i, j: (j, 0, 0))
    wrow = pl.BlockSpec((1, FF_PAD, D_MODEL), lambda i, j: (j, 0, 0))
    vec = pl.BlockSpec((1, D_MODEL), lambda i, j: (0, 0))
    hid = pl.BlockSpec((1, tm, FF_PAD), lambda i, j: (j, i, 0))
    hid_shape = jax.ShapeDtypeStruct((N_CHIPS, n, FF_PAD), BF16)
    return pl.pallas_call(
        body, name=name, grid=(n // tm, N_CHIPS),
        in_specs=[row, row, row, wcol, wcol, wrow, vec, vec],
        out_specs=[row, hid, hid, hid, row, vec, vec],
        out_shape=[jax.ShapeDtypeStruct((n, D_MODEL), F32), hid_shape, hid_shape, hid_shape,
                   jax.ShapeDtypeStruct((n, D_MODEL), BF16),
                   jax.ShapeDtypeStruct((1, D_MODEL), F32), jax.ShapeDtypeStruct((1, D_MODEL), F32)],
        scratch_shapes=[pltpu.VMEM((tm, D_MODEL), F32), pltpu.VMEM((tm, D_MODEL), BF16), pltpu.VMEM((tm, D_MODEL), F32)],
        compiler_params=_params(("arbitrary", "arbitrary")),
    )(x, z, dout, w1g, w3g, w2g, g, b)


def _operand_spec(arr, mode, tn, width):
    if mode == "shared":
        return pl.BlockSpec((tn, width), lambda s, k: (k, 0))
    if mode == "batch":
        return pl.BlockSpec((1, tn, width), lambda s, k: (s, k, 0))
    assert mode == "cols" and arr.shape[1] == N_CHIPS * width
    return pl.BlockSpec((tn, width), lambda s, k: (k, s))


def _grad_matmul(name, a, a_mode, ka, b, b_mode, kb, colsum=False):
    n = a.shape[-2]
    tn = min(ROW_TILE, n)
    nk = n // tn

    def body(*refs):
        if colsum:
            a_ref, b_ref, o_ref, cs_ref, acc = refs
        else:
            a_ref, b_ref, o_ref, acc = refs
        k = pl.program_id(1)
        av = a_ref[0] if a_mode == "batch" else a_ref[...]
        bv = b_ref[0] if b_mode == "batch" else b_ref[...]

        @pl.when(k == 0)
        def _():
            acc[...] = jnp.zeros_like(acc)
            if colsum:
                cs_ref[...] = jnp.zeros_like(cs_ref)

        acc[...] += _mm_tn(_bf(av), _bf(bv))
        if colsum:
            cs_ref[...] += jnp.sum(bv.astype(F32), axis=0, keepdims=True)

        @pl.when(k == nk - 1)
        def _():
            o_ref[0] = _bf(acc[...])

    out_specs = [pl.BlockSpec((1, ka, kb), lambda s, k: (s, 0, 0))]
    out_shape = [jax.ShapeDtypeStruct((N_CHIPS, ka, kb), BF16)]
    if colsum:
        out_specs.append(pl.BlockSpec((1, kb), lambda s, k: (0, s)))
        out_shape.append(jax.ShapeDtypeStruct((1, N_CHIPS * kb), F32))
    res = pl.pallas_call(
        body, name=name, grid=(N_CHIPS, nk),
        in_specs=[_operand_spec(a, a_mode, tn, ka), _operand_spec(b, b_mode, tn, kb)],
        out_specs=out_specs, out_shape=out_shape,
        scratch_shapes=[pltpu.VMEM((ka, kb), F32)],
        compiler_params=_params(("arbitrary", "arbitrary")),
    )(a, b)
    return res if colsum else res[0]


def _in_proj(x1, w_in_g, b_in):
    n = x1.shape[0]
    tm = min(ROW_TILE, n)

    def body(x_ref, w_ref, b_ref, o_ref, xb):
        @pl.when(pl.program_id(1) == 0)
        def _():
            xb[...] = _bf(x_ref[...])

        o_ref[...] = _mm(xb[...], w_ref[0]) + b_ref[...]

    return pl.pallas_call(
        body, name="in_proj", grid=(n // tm, N_CHIPS),
        in_specs=[pl.BlockSpec((tm, D_MODEL), lambda i, j: (i, 0)),
                  pl.BlockSpec((1, D_MODEL, IN_SHARD), lambda i, j: (j, 0, 0)),
                  pl.BlockSpec((1, IN_SHARD), lambda i, j: (0, j))],
        out_specs=pl.BlockSpec((tm, IN_SHARD), lambda i, j: (i, j)),
        out_shape=jax.ShapeDtypeStruct((n, D_IN), F32),
        scratch_shapes=[pltpu.VMEM((tm, D_MODEL), BF16)],
        compiler_params=_params(("arbitrary", "arbitrary")),
    )(x1, w_in_g, b_in)


def _in_proj_dx(dproj, w_in_g, dz2):
    n = dproj.shape[0]
    tm = min(ROW_TILE, n)

    def body(dp_ref, w_ref, dz_ref, o_ref, acc):
        j = pl.program_id(1)

        @pl.when(j == 0)
        def _():
            acc[...] = jnp.zeros_like(acc)

        acc[...] += _mm_nt(dp_ref[...], w_ref[0])

        @pl.when(j == N_CHIPS - 1)
        def _():
            o_ref[...] = DEEPNORM_ALPHA * dz_ref[...] + acc[...]

    return pl.pallas_call(
        body, name="in_proj_dx", grid=(n // tm, N_CHIPS),
        in_specs=[pl.BlockSpec((tm, IN_SHARD), lambda i, j: (i, j)),
                  pl.BlockSpec((1, D_MODEL, IN_SHARD), lambda i, j: (j, 0, 0)),
                  pl.BlockSpec((tm, D_MODEL), lambda i, j: (i, 0))],
        out_specs=pl.BlockSpec((tm, D_MODEL), lambda i, j: (i, 0)),
        out_shape=jax.ShapeDtypeStruct((n, D_MODEL), F32),
        scratch_shapes=[pltpu.VMEM((tm, D_MODEL), F32)],
        compiler_params=_params(("arbitrary", "arbitrary")),
    )(dproj, w_in_g, dz2)


def _rope_tables(seq_len):
    pos = jnp.arange(seq_len, dtype=F32)
    inv_freq = ROPE_THETA ** (-jnp.arange(0, ROPE_DIM, 2, dtype=F32) / ROPE_DIM)
    ang = pos[:, None] * inv_freq[None, :]
    cos, sin = jnp.cos(ang), jnp.sin(ang)
    half = ROPE_DIM // 2
    rest = HEAD_DIM - ROPE_DIM
    ones = jnp.ones((seq_len, rest), F32)
    zeros = jnp.zeros((seq_len, rest), F32)
    zh = jnp.zeros((seq_len, half), F32)
    c = jnp.concatenate([cos, cos, ones], axis=1)
    sa = jnp.concatenate([-sin, zh, zeros], axis=1)
    sb = jnp.concatenate([zh, sin, zeros], axis=1)
    reps = LANES // HEAD_DIM
    return tuple(jnp.tile(t, (1, reps)) for t in (c, sa, sb))


def _rope(t, c, sa, sb):
    w = t.shape[1]
    reps = w // LANES
    half = ROPE_DIM // 2
    return (t * jnp.tile(c, (1, reps)) + pltpu.roll(t, w - half, 1) * jnp.tile(sa, (1, reps))
            + pltpu.roll(t, half, 1) * jnp.tile(sb, (1, reps)))


def _rope_transposed(g, c, sa, sb):
    w = g.shape[1]
    reps = w // LANES
    half = ROPE_DIM // 2
    return (g * jnp.tile(c, (1, reps)) + pltpu.roll(g * jnp.tile(sa, (1, reps)), half, 1)
            + pltpu.roll(g * jnp.tile(sb, (1, reps)), w - half, 1))


def _attn_mask(n):
    qi = lax.broadcasted_iota(jnp.int32, (ATTN_BLOCK, 2 * ATTN_BLOCK), 0)
    kj = lax.broadcasted_iota(jnp.int32, (ATTN_BLOCK, 2 * ATTN_BLOCK), 1)
    dist = qi + ATTN_BLOCK - kj
    return (dist >= 0) & (dist < ATTN_BLOCK) & (n * ATTN_BLOCK + kj - ATTN_BLOCK >= 0)


def _head_halves(t_pair):
    swapped = pltpu.roll(t_pair, HEAD_DIM, 1)
    return ((_bf(t_pair), _bf(swapped)), (_bf(swapped), _bf(t_pair)))


def _attn_probs(q_masked, k_sel, mask, sink):
    s = _mm_nt(q_masked, k_sel) * (HEAD_DIM ** -0.5)
    s = jnp.where(mask, s, NEG_INF)
    m = jnp.maximum(jnp.max(s, axis=-1, keepdims=True), sink)
    p = jnp.exp(s - m)
    e_sink = jnp.exp(sink - m)
    denom = jnp.sum(p, axis=-1, keepdims=True) + e_sink
    return p / denom, e_sink / denom


def _attn_fwd(proj, tabs, sinks):
    n_tok = proj.shape[0]
    nb = n_tok // ATTN_BLOCK
    group = N_Q_HEADS // N_KV_HEADS

    def body(q_ref, k_ref, v_ref, c_ref, sa_ref, sb_ref, sink_ref, y_ref, kprev, vprev):
        n = pl.program_id(0)

        @pl.when(n == 0)
        def _():
            kprev[...] = jnp.zeros_like(kprev)
            vprev[...] = jnp.zeros_like(vprev)

        c, sa, sb = c_ref[...], sa_ref[...], sb_ref[...]
        qr = _bf(_rope(q_ref[...], c, sa, sb))
        kr = _rope(k_ref[...], c, sa, sb)
        vc = v_ref[...]
        kk = jnp.concatenate([kprev[...], kr], axis=0)
        vv = jnp.concatenate([vprev[...], vc], axis=0)
        kprev[...] = kr
        vprev[...] = vc
        mask = _attn_mask(n)
        lo = lax.broadcasted_iota(jnp.int32, (ATTN_BLOCK, LANES), 1) < HEAD_DIM
        for r in range(N_KV_HEADS // 2):
            k_var = _head_halves(kk[:, r * LANES:(r + 1) * LANES])
            v_var = _head_halves(vv[:, r * LANES:(r + 1) * LANES])
            for kh2 in range(2):
                kh = 2 * r + kh2
                for gp in range(group // 2):
                    pair = (kh * group) // 2 + gp
                    q2 = qr[:, pair * LANES:(pair + 1) * LANES]
                    outs = []
                    for half in range(2):
                        h = 2 * pair + half
                        qm = jnp.where(lo if half == 0 else ~lo, q2, jnp.zeros_like(q2))
                        probs, _ = _attn_probs(qm, k_var[kh2][half], mask, sink_ref[0, h])
                        outs.append(_mm(_bf(probs), v_var[kh2][half]))
                    y_ref[:, pair * LANES:(pair + 1) * LANES] = jnp.where(lo, outs[0], outs[1])

    blk = lambda width, col: pl.BlockSpec((ATTN_BLOCK, width), lambda n: (n, col))
    tab = pl.BlockSpec((ATTN_BLOCK, LANES), lambda n: (n, 0))
    kvw = N_KV_HEADS * HEAD_DIM
    return pl.pallas_call(
        body, name="attn_fwd", grid=(nb,),
        in_specs=[blk(D_MODEL, 0), blk(kvw, COL_K), blk(kvw, COL_V), tab, tab, tab,
                  pl.BlockSpec(memory_space=pltpu.SMEM)],
        out_specs=pl.BlockSpec((ATTN_BLOCK, D_MODEL), lambda n: (n, 0)),
        out_shape=jax.ShapeDtypeStruct((n_tok, D_MODEL), F32),
        scratch_shapes=[pltpu.VMEM((ATTN_BLOCK, kvw), F32), pltpu.VMEM((ATTN_BLOCK, kvw), F32)],
        compiler_params=_params(("arbitrary",)),
    )(proj, proj, proj, *tabs, sinks)


def _attn_bwd(proj, dy, tabs, sinks):
    n_tok = proj.shape[0]
    nb = n_tok // ATTN_BLOCK
    group = N_Q_HEADS // N_KV_HEADS
    kvw = N_KV_HEADS * HEAD_DIM

    def body(q_ref, k_ref, v_ref, do_ref, c_ref, sa_ref, sb_ref, cp_ref, sap_ref, sbp_ref, sink_ref,
             dq_ref, dkv_ref, dsink_ref, kprev, vprev, dkc, dvc):
        n = pl.program_id(0)

        @pl.when(n == 0)
        def _():
            for ref in (kprev, vprev, dkc, dvc, dsink_ref):
                ref[...] = jnp.zeros_like(ref)

        prev_tabs = (cp_ref[...], sap_ref[...], sbp_ref[...])

        @pl.when(n < nb)
        def _():
            c, sa, sb = c_ref[...], sa_ref[...], sb_ref[...]
            qr = _bf(_rope(q_ref[...], c, sa, sb))
            kr = _rope(k_ref[...], c, sa, sb)
            vc = v_ref[...]
            kk = jnp.concatenate([kprev[...], kr], axis=0)
            vv = jnp.concatenate([vprev[...], vc], axis=0)
            kprev[...] = kr
            vprev[...] = vc
            mask = _attn_mask(n)
            lane = lax.broadcasted_iota(jnp.int32, (ATTN_BLOCK, LANES), 1)
            lo = lane < HEAD_DIM
            lo2 = lax.broadcasted_iota(jnp.int32, (2 * ATTN_BLOCK, LANES), 1) < HEAD_DIM
            dsink = jnp.zeros((1, LANES), F32)
            dq_pairs = []
            dk_pairs = []
            dv_pairs = []
            for r in range(N_KV_HEADS // 2):
                k_var = _head_halves(kk[:, r * LANES:(r + 1) * LANES])
                v_var = _head_halves(vv[:, r * LANES:(r + 1) * LANES])
                dk_full = []
                dv_full = []
                for kh2 in range(2):
                    kh = 2 * r + kh2
                    dk_acc = jnp.zeros((2 * ATTN_BLOCK, LANES), F32)
                    dv_acc = jnp.zeros((2 * ATTN_BLOCK, LANES), F32)
                    for gp in range(group // 2):
                        pair = (kh * group) // 2 + gp
                        q2 = qr[:, pair * LANES:(pair + 1) * LANES]
                        do2 = do_ref[:, pair * LANES:(pair + 1) * LANES]
                        dqs = []
                        for half in range(2):
                            h = 2 * pair + half
                            sel = lo if half == 0 else ~lo
                            qm = jnp.where(sel, q2, jnp.zeros_like(q2))
                            dom = _bf(jnp.where(sel, do2, 0.0))
                            k_sel = k_var[kh2][half]
                            v_sel = v_var[kh2][half]
                            probs, p_sink = _attn_probs(qm, k_sel, mask, sink_ref[0, h])
                            dp = _mm_nt(dom, v_sel)
                            delta = jnp.sum(probs * dp, axis=-1, keepdims=True)
                            ds = _bf(probs * (dp - delta) * (HEAD_DIM ** -0.5))
                            dsink = dsink + jnp.where(lane[:1] == h, -jnp.sum(p_sink * delta), 0.0)
                            dqs.append(_mm(ds, k_sel))
                            dk_acc = dk_acc + _mm_tn(ds, qm)
                            dv_acc = dv_acc + _mm_tn(_bf(probs), dom)
                        dq_pairs.append(jnp.where(lo, dqs[0], dqs[1]))
                    dk_full.append(dk_acc + pltpu.roll(dk_acc, HEAD_DIM, 1))
                    dv_full.append(dv_acc + pltpu.roll(dv_acc, HEAD_DIM, 1))
                dk_pairs.append(jnp.where(lo2, dk_full[0], dk_full[1]))
                dv_pairs.append(jnp.where(lo2, dv_full[0], dv_full[1]))
            dsink_ref[...] += dsink
            dq_ref[...] = _bf(_rope_transposed(jnp.concatenate(dq_pairs, axis=1), c, sa, sb))
            dk_all = jnp.concatenate(dk_pairs, axis=1)
            dv_all = jnp.concatenate(dv_pairs, axis=1)
            dkv_ref[:, :kvw] = _bf(_rope_transposed(dkc[...] + dk_all[:ATTN_BLOCK], *prev_tabs))
            dkv_ref[:, kvw:] = _bf(dvc[...] + dv_all[:ATTN_BLOCK])
            dkc[...] = dk_all[ATTN_BLOCK:]
            dvc[...] = dv_all[ATTN_BLOCK:]

        @pl.when(n == nb)
        def _():
            dkv_ref[:, :kvw] = _bf(_rope_transposed(dkc[...], *prev_tabs))
            dkv_ref[:, kvw:] = _bf(dvc[...])

    cur = lambda n: jnp.minimum(n, nb - 1)
    prev = lambda n: jnp.maximum(n - 1, 0)
    blk = lambda width, col: pl.BlockSpec((ATTN_BLOCK, width), lambda n: (cur(n), col))
    tab = pl.BlockSpec((ATTN_BLOCK, LANES), lambda n: (cur(n), 0))
    tabp = pl.BlockSpec((ATTN_BLOCK, LANES), lambda n: (prev(n), 0))
    return pl.pallas_call(
        body, name="attn_bwd", grid=(nb + 1,),
        in_specs=[blk(D_MODEL, 0), blk(kvw, COL_K), blk(kvw, COL_V), blk(D_MODEL, 0), tab, tab, tab, tabp, tabp, tabp,
                  pl.BlockSpec(memory_space=pltpu.SMEM)],
        out_specs=[pl.BlockSpec((ATTN_BLOCK, D_MODEL), lambda n: (cur(n), 0)),
                   pl.BlockSpec((ATTN_BLOCK, 2 * kvw), lambda n: (prev(n), 0)),
                   pl.BlockSpec((1, LANES), lambda n: (0, 0))],
        out_shape=[jax.ShapeDtypeStruct((n_tok, D_MODEL), BF16), jax.ShapeDtypeStruct((n_tok, 2 * kvw), BF16),
                   jax.ShapeDtypeStruct((1, LANES), F32)],
        scratch_shapes=[pltpu.VMEM((ATTN_BLOCK, kvw), F32)] * 4,
        compiler_params=_params(("arbitrary",)),
    )(proj, proj, proj, dy, *tabs, *tabs, sinks)


def _tril(shape):
    return lax.broadcasted_iota(jnp.int32, shape, 0) >= lax.broadcasted_iota(jnp.int32, shape, 1)


def _hg_elem(fl, qh, lb):
    c = HGRN_CHUNK
    f = lb + (1.0 - lb) * _sig(fl)
    k = 1.0 - f
    gc = lax.dot_general(_tril((c, c)).astype(F32), jnp.log(f), (((1,), (0,)), ((), ())),
                         precision=lax.Precision.HIGHEST, preferred_element_type=F32)
    last = lax.broadcasted_iota(jnp.int32, gc.shape, 0) == c - 1
    g_last = jnp.sum(jnp.where(last, gc, 0.0), axis=0, keepdims=True)
    q = qh * _sig(qh)
    return q * jnp.exp(gc), k * jnp.exp(-gc), k * jnp.exp(g_last - gc), jnp.exp(g_last)


def _hg_out(q_dec, k_inv, v, st):
    sc = jnp.where(_tril((HGRN_CHUNK, HGRN_CHUNK)), _mm_nt(_bf(q_dec), _bf(k_inv)), 0.0)
    return _mm(_bf(sc), _bf(v)) + _mm_nt(_bf(q_dec), _bf(st)), sc


def _hg_post(o, og, ng):
    on = o * lax.rsqrt(jnp.mean(o * o, axis=-1, keepdims=True) + RMS_EPS) * ng
    return on * (og * _sig(og))


def _hgrn_specs(n_tok, rev):
    nc = n_tok // HGRN_CHUNK
    cb = min(HGRN_CHUNKS_PER_STEP, nc)
    nt = nc // cb
    rows = cb * HGRN_CHUNK
    tt = (lambda t: nt - 1 - t) if rev else (lambda t: t)
    col = lambda base: pl.BlockSpec((rows, LANES), lambda h, t: (tt(t), base + h))
    head_vec = pl.BlockSpec((1, LANES), lambda h, t: (0, h))
    one_vec = pl.BlockSpec((1, LANES), lambda h, t: (0, 0))
    state = pl.BlockSpec((1, cb, HGRN_DK, HGRN_DK), lambda h, t: (h, tt(t), 0, 0))
    return nc, cb, nt, col, head_vec, one_vec, state


def _hgrn_fwd(proj, lb, ng):
    n_tok = proj.shape[0]
    nc, cb, nt, col, head_vec, one_vec, state = _hgrn_specs(n_tok, False)

    def body(fl_ref, qh_ref, ih_ref, og_ref, lb_ref, ng_ref, y_ref, st_ref, s_acc):
        @pl.when(pl.program_id(1) == 0)
        def _():
            s_acc[...] = jnp.zeros_like(s_acc)

        def chunk(ci, carry):
            rows = pl.ds(pl.multiple_of(ci * HGRN_CHUNK, HGRN_CHUNK), HGRN_CHUNK)
            q_dec, k_inv, k_end, decay = _hg_elem(fl_ref[rows, :], qh_ref[rows, :], lb_ref[...])
            v = ih_ref[rows, :]
            st = s_acc[...]
            st_ref[0, ci] = st
            o, _ = _hg_out(q_dec, k_inv, v, st)
            y_ref[rows, :] = _hg_post(o, og_ref[rows, :], ng_ref[...])
            s_acc[...] = st * decay + _mm_tn(_bf(v), _bf(k_end))
            return carry

        lax.fori_loop(0, cb, chunk, 0)

    return pl.pallas_call(
        body, name="hgrn_fwd", grid=(HGRN_HEADS, nt),
        in_specs=[col(COL_F), col(COL_QH), col(COL_IH), col(COL_OG), head_vec, one_vec],
        out_specs=[col(0), state],
        out_shape=[jax.ShapeDtypeStruct((n_tok, D_MODEL), F32),
                   jax.ShapeDtypeStruct((HGRN_HEADS, nc, HGRN_DK, HGRN_DK), F32)],
        scratch_shapes=[pltpu.VMEM((HGRN_DK, HGRN_DK), F32)],
        compiler_params=_params(("arbitrary", "arbitrary")),
    )(proj, proj, proj, proj, lb, ng)


def _hgrn_bwd(proj, lb, ng, states, dy):
    n_tok = proj.shape[0]
    nc, cb, nt, col, head_vec, one_vec, state = _hgrn_specs(n_tok, True)

    def body(fl_ref, qh_ref, ih_ref, og_ref, lb_ref, ng_ref, st_ref, dy_ref,
             dfl_ref, dqh_ref, dih_ref, dog_ref, dlb_ref, dng_ref, g_acc):
        h = pl.program_id(0)
        t = pl.program_id(1)

        @pl.when(t == 0)
        def _():
            g_acc[...] = jnp.zeros_like(g_acc)
            dlb_ref[...] = jnp.zeros_like(dlb_ref)

        @pl.when((t == 0) & (h == 0))
        def _():
            dng_ref[...] = jnp.zeros_like(dng_ref)

        def chunk(i, carry):
            ci = cb - 1 - i
            rows = pl.ds(pl.multiple_of(ci * HGRN_CHUNK, HGRN_CHUNK), HGRN_CHUNK)
            (q_dec, k_inv, k_end, decay), elem_vjp = jax.vjp(_hg_elem, fl_ref[rows, :], qh_ref[rows, :], lb_ref[...])
            v = ih_ref[rows, :]
            st = st_ref[0, ci]
            o, sc = _hg_out(q_dec, k_inv, v, st)
            _, post_vjp = jax.vjp(_hg_post, o, og_ref[rows, :], ng_ref[...])
            do, dog, dng = post_vjp(dy_ref[rows, :])
            dob, vb, qb = _bf(do), _bf(v), _bf(q_dec)
            dsc = _bf(jnp.where(_tril((HGRN_CHUNK, HGRN_CHUNK)), _mm_nt(dob, vb), 0.0))
            g = g_acc[...]
            gb = _bf(g)
            dq_dec = _mm(dsc, _bf(k_inv)) + _mm(dob, _bf(st))
            dk_inv = _mm_tn(dsc, qb)
            dv = _mm_tn(_bf(sc), dob) + _mm_nt(_bf(k_end), gb)
            dk_end = _mm(vb, gb)
            ddecay = jnp.sum(st * g, axis=0, keepdims=True)
            g_acc[...] = g * decay + _mm_tn(dob, qb)
            dfl, dqh, dlb = elem_vjp((dq_dec, dk_inv, dk_end, ddecay))
            dfl_ref[rows, :] = _bf(dfl)
            dqh_ref[rows, :] = _bf(dqh)
            dih_ref[rows, :] = _bf(dv)
            dog_ref[rows, :] = _bf(dog)
            dlb_ref[...] += dlb
            dng_ref[...] += dng
            return carry

        lax.fori_loop(0, cb, chunk, 0)

    out_col = jax.ShapeDtypeStruct((n_tok, D_MODEL), BF16)
    return pl.pallas_call(
        body, name="hgrn_bwd", grid=(HGRN_HEADS, nt),
        in_specs=[col(COL_F), col(COL_QH), col(COL_IH), col(COL_OG), head_vec, one_vec, state, col(0)],
        out_specs=[col(0), col(0), col(0), col(0), head_vec, one_vec],
        out_shape=[out_col, out_col, out_col, out_col,
                   jax.ShapeDtypeStruct((1, D_MODEL), F32), jax.ShapeDtypeStruct((1, LANES), F32)],
        scratch_shapes=[pltpu.VMEM((HGRN_DK, HGRN_DK), F32)],
        compiler_params=_params(("arbitrary", "arbitrary")),
    )(proj, proj, proj, proj, lb, ng, states, dy)


def _lb_fwd(lb_logits):
    def lb_of(l0, l1):
        m = jnp.maximum(l0, l1)
        e0, e1 = jnp.exp(l0 - m), jnp.exp(l1 - m)
        return e0 / (e0 + e1)

    def body(l_ref, o_ref):
        o_ref[...] = lb_of(l_ref[0:1, :], l_ref[1:2, :])

    lb = pl.pallas_call(body, name="lb_fwd", out_shape=jax.ShapeDtypeStruct((1, D_MODEL), F32))(lb_logits)
    return lb, lb_of


def _gate_specs(tm):
    return [pl.BlockSpec((tm, 512), lambda i, c=c: (i, c)) for c in (COL_GA, COL_GA + 1, COL_GH, COL_GH + 1)]


def _mix_fwd(y_attn, y_hgrn, proj, x1, w_pa, w_ph, w_out, g, b):
    n = x1.shape[0]
    tm = min(MIX_TILE, n)

    def body(ya_ref, yh_ref, ga0, ga1, gh0, gh1, x_ref, wpa, wph, wo, g_ref, b_ref, z_ref, o_ref):
        ya = _mm(_bf(ya_ref[...]), wpa[...])
        yh = _mm(_bf(yh_ref[...]), wph[...])
        ga = jnp.concatenate([ga0[...], ga1[...]], axis=1)
        gh = jnp.concatenate([gh0[...], gh1[...]], axis=1)
        merged = _sig(ga) * ya + _sig(gh) * yh
        z = DEEPNORM_ALPHA * x_ref[...] + _mm(_bf(merged), wo[...])
        z_ref[...] = z
        o_ref[...] = _ln(z, g_ref[...], b_ref[...])

    row = pl.BlockSpec((tm, D_MODEL), lambda i: (i, 0))
    sq = _full((D_MODEL, D_MODEL))
    vec = _full((1, D_MODEL))
    return pl.pallas_call(
        body, name="mix_fwd", grid=(n // tm,),
        in_specs=[row, row, *_gate_specs(tm), row, sq, sq, sq, vec, vec],
        out_specs=[row, row], out_shape=[jax.ShapeDtypeStruct((n, D_MODEL), F32)] * 2,
        compiler_params=_params(("arbitrary",)),
    )(y_attn, y_hgrn, proj, proj, proj, proj, x1, w_pa, w_ph, w_out, g, b)


def _mix_bwd(dx2, z2, y_attn, y_hgrn, proj, w_pa, w_ph, w_out, g, b):
    n = z2.shape[0]
    tm = min(MIX_TILE, n)

    def body(do_ref, z_ref, ya_ref, yh_ref, ga0, ga1, gh0, gh1, wpa, wph, wo, g_ref, b_ref,
             dz_ref, mg_ref, dya_ref, dyh_ref, dyat_ref, dyhg_ref, dgt_ref, dg_ref, db_ref):
        _, vjp = jax.vjp(_ln, z_ref[...], g_ref[...], b_ref[...])
        dz, dg, db = vjp(do_ref[...])

        @pl.when(pl.program_id(0) == 0)
        def _():
            dg_ref[...] = jnp.zeros_like(dg_ref)
            db_ref[...] = jnp.zeros_like(db_ref)

        dg_ref[...] += dg
        db_ref[...] += db
        dz_ref[...] = dz
        ya = _mm(_bf(ya_ref[...]), wpa[...])
        yh = _mm(_bf(yh_ref[...]), wph[...])
        ga = jnp.concatenate([ga0[...], ga1[...]], axis=1)
        gh = jnp.concatenate([gh0[...], gh1[...]], axis=1)

        def merge(ga, gh, ya, yh):
            return _sig(ga) * ya + _sig(gh) * yh

        merged, merge_vjp = jax.vjp(merge, ga, gh, ya, yh)
        mg_ref[...] = _bf(merged)
        dmerged = _mm_nt(_bf(dz), wo[...])
        dga, dgh, dya, dyh = merge_vjp(dmerged)
        dya = _bf(dya)
        dyh = _bf(dyh)
        dya_ref[...] = dya
        dyh_ref[...] = dyh
        dgt_ref[:, :D_MODEL] = _bf(dga)
        dgt_ref[:, D_MODEL:] = _bf(dgh)
        dyat_ref[...] = _mm_nt(dya, wpa[...])
        dyhg_ref[...] = _mm_nt(dyh, wph[...])

    row = pl.BlockSpec((tm, D_MODEL), lambda i: (i, 0))
    row2 = pl.BlockSpec((tm, 2 * D_MODEL), lambda i: (i, 0))
    sq = _full((D_MODEL, D_MODEL))
    vec = _full((1, D_MODEL))
    f32_row = jax.ShapeDtypeStruct((n, D_MODEL), F32)
    bf_row = jax.ShapeDtypeStruct((n, D_MODEL), BF16)
    vec_shape = jax.ShapeDtypeStruct((1, D_MODEL), F32)
    return pl.pallas_call(
        body, name="mix_bwd", grid=(n // tm,),
        in_specs=[row, row, row, row, *_gate_specs(tm), sq, sq, sq, vec, vec],
        out_specs=[row, row, row, row, row, row, row2, vec, vec],
        out_shape=[f32_row, bf_row, bf_row, bf_row, f32_row, f32_row,
                   jax.ShapeDtypeStruct((n, 2 * D_MODEL), BF16), vec_shape, vec_shape],
        compiler_params=_params(("arbitrary",)),
    )(dx2, z2, y_attn, y_hgrn, proj, proj, proj, proj, w_pa, w_ph, w_out, g, b)


def _position():
    x, y, c = lax.axis_index("x"), lax.axis_index("y"), lax.axis_index("c")
    chips = [(1 - x, y), (x, 1 - y), (1 - x, 1 - y)]
    return x, y, c, chips


def _any_specs(k):
    return [pl.BlockSpec(memory_space=pl.ANY)] * k


def _gather_weights(shards):
    nw = len(shards)

    def body(*refs):
        ins, outs = refs[:nw], refs[nw:2 * nw]
        local_sem, send_sem, recv_sem = refs[2 * nw:]
        x, y, c, chips = _position()
        me = 2 * x + y
        sibling = (x, y, 1 - c)

        def half(w, chip_idx, which):
            hr = shards[w].shape[0] // 2
            return outs[w].at[chip_idx, pl.ds(which * hr, hr)]

        def own_half(w):
            hr = shards[w].shape[0] // 2
            return ins[w].at[pl.ds(c * hr, hr)]

        def remote(w, k, src, dst, to):
            return pltpu.make_async_remote_copy(src_ref=src, dst_ref=dst, send_sem=send_sem.at[w * 6 + k],
                                                recv_sem=recv_sem.at[w * 6 + k], device_id=to, device_id_type=MESH)

        local = [pltpu.make_async_copy(ins[w], outs[w].at[me], local_sem.at[w]) for w in range(nw)]
        for cp in local:
            cp.start()
        first = [remote(w, j, own_half(w), half(w, me, c), (px, py, c))
                 for w in range(nw) for j, (px, py) in enumerate(chips)]
        for cp in first:
            cp.start()
        passed = []
        for w in range(nw):
            for j, (px, py) in enumerate(chips):
                landed = half(w, 2 * px + py, c)
                remote(w, j, landed, landed, (px, py, c)).wait_recv()
                cp = remote(w, 3 + j, landed, landed, sibling)
                cp.start()
                passed.append(cp)
        for w in range(nw):
            for j, (px, py) in enumerate(chips):
                from_sibling = half(w, 2 * px + py, 1 - c)
                remote(w, 3 + j, from_sibling, from_sibling, sibling).wait_recv()
        for cp in first + passed:
            cp.wait_send()
        for cp in local:
            cp.wait()

    return pl.pallas_call(
        body, name="gather_weights",
        in_specs=_any_specs(nw), out_specs=_any_specs(nw),
        out_shape=[jax.ShapeDtypeStruct((N_CHIPS, *s.shape), s.dtype) for s in shards],
        scratch_shapes=[pltpu.SemaphoreType.DMA((nw,)), pltpu.SemaphoreType.DMA((nw * 6,)), pltpu.SemaphoreType.DMA((nw * 6,))],
    )(*shards)


def _exchange_grads(grads):
    nw = len(grads)

    def body(*refs):
        ins, outs = refs[:nw], refs[nw:2 * nw]
        local_sem, send_sem, recv_sem = refs[2 * nw:]
        x, y, c, chips = _position()
        me = 2 * x + y

        def remote(w, j, to_chip):
            px, py = to_chip
            return pltpu.make_async_remote_copy(src_ref=ins[w].at[2 * px + py], dst_ref=outs[w].at[me],
                                                send_sem=send_sem.at[w * 3 + j], recv_sem=recv_sem.at[w * 3 + j],
                                                device_id=(px, py, c), device_id_type=MESH)

        local = [pltpu.make_async_copy(ins[w].at[me], outs[w].at[me], local_sem.at[w]) for w in range(nw)]
        sends = [remote(w, j, chip) for w in range(nw) for j, chip in enumerate(chips)]
        for cp in local + sends:
            cp.start()
        for w in range(nw):
            for j, (px, py) in enumerate(chips):
                slot = outs[w].at[2 * px + py]
                pltpu.make_async_remote_copy(src_ref=slot, dst_ref=slot, send_sem=send_sem.at[w * 3 + j],
                                             recv_sem=recv_sem.at[w * 3 + j], device_id=(px, py, c),
                                             device_id_type=MESH).wait_recv()
        for cp in sends:
            cp.wait_send()
        for cp in local:
            cp.wait()

    return pl.pallas_call(
        body, name="exchange_grads",
        in_specs=_any_specs(nw), out_specs=_any_specs(nw),
        out_shape=[jax.ShapeDtypeStruct(g.shape, g.dtype) for g in grads],
        scratch_shapes=[pltpu.SemaphoreType.DMA((nw,)), pltpu.SemaphoreType.DMA((nw * 3,)), pltpu.SemaphoreType.DMA((nw * 3,))],
    )(*grads)


def _sum_slots(name, slots):
    _, rows, cols = slots.shape
    tr = _update_rows(rows)

    def body(s_ref, o_ref):
        acc = s_ref[0].astype(F32)
        for i in range(1, N_CHIPS):
            acc = acc + s_ref[i].astype(F32)
        o_ref[...] = acc

    return pl.pallas_call(
        body, name=name, grid=(rows // tr,),
        in_specs=[pl.BlockSpec((N_CHIPS, tr, cols), lambda i: (0, i, 0))],
        out_specs=pl.BlockSpec((tr, cols), lambda i: (i, 0)),
        out_shape=jax.ShapeDtypeStruct((rows, cols), F32),
        compiler_params=_params(("arbitrary",)),
    )(slots)


def _swap_with_sibling(parts):
    nw = len(parts)

    def body(*refs):
        ins, outs = refs[:nw], refs[nw:2 * nw]
        send_sem, recv_sem = refs[2 * nw:]
        x, y, c, _ = _position()
        copies = [pltpu.make_async_remote_copy(src_ref=ins[w], dst_ref=outs[w], send_sem=send_sem.at[w],
                                               recv_sem=recv_sem.at[w], device_id=(x, y, 1 - c), device_id_type=MESH)
                  for w in range(nw)]
        for cp in copies:
            cp.start()
        for cp in copies:
            cp.wait()

    return pl.pallas_call(
        body, name="swap_with_sibling",
        in_specs=_any_specs(nw), out_specs=_any_specs(nw),
        out_shape=[jax.ShapeDtypeStruct(p.shape, p.dtype) for p in parts],
        scratch_shapes=[pltpu.SemaphoreType.DMA((nw,)), pltpu.SemaphoreType.DMA((nw,))],
    )(*parts)


def _sum_small(part):
    def body(p_ref, o_ref, buf, send_sem, recv_sem):
        x, y, c, _ = _position()
        me = 4 * x + 2 * y + c
        buf[me] = p_ref[...]
        copies = []
        for k in range(1, N_DEV):
            peer = tuple(1 - v if (k >> s) & 1 else v for v, s in ((x, 2), (y, 1), (c, 0)))
            copies.append(pltpu.make_async_remote_copy(src_ref=p_ref, dst_ref=buf.at[me], send_sem=send_sem.at[k - 1],
                                                       recv_sem=recv_sem.at[k - 1], device_id=peer, device_id_type=MESH))
        for cp in copies:
            cp.start()
        for cp in copies:
            cp.wait()
        acc = buf[0]
        for d in range(1, N_DEV):
            acc = acc + buf[d]
        o_ref[...] = acc

    vm = pl.BlockSpec(memory_space=pltpu.VMEM)
    return pl.pallas_call(
        body, name="sum_small", in_specs=[vm], out_specs=vm,
        out_shape=jax.ShapeDtypeStruct((1, SM_LEN), F32),
        scratch_shapes=[pltpu.VMEM((N_DEV, 1, SM_LEN), F32), pltpu.SemaphoreType.DMA((N_DEV - 1,)),
                        pltpu.SemaphoreType.DMA((N_DEV - 1,))],
    )(part)


def _adamw(w, g, m, v):
    m = ADAM_B1 * m + (1.0 - ADAM_B1) * g
    v = ADAM_B2 * v + (1.0 - ADAM_B2) * (g * g)
    m_hat = m / (1.0 - ADAM_B1 ** ADAM_STEP)
    v_hat = v / (1.0 - ADAM_B2 ** ADAM_STEP)
    delta = -ADAM_LR * (m_hat / (jnp.sqrt(v_hat) + ADAM_EPS) + ADAM_WD * w)
    return delta, m, v


def _adam_big(name, p_own, p_sibling, w, m, v):
    rows, cols = w.shape
    tr = _update_rows(rows)

    def body(p_ref, q_ref, w_ref, m_ref, v_ref, g_ref, d_ref, nm_ref, nv_ref):
        g = p_ref[...] + q_ref[...]
        g_ref[...] = g
        d_ref[...], nm_ref[...], nv_ref[...] = _adamw(w_ref[...], g, m_ref[...], v_ref[...])

    spec = pl.BlockSpec((tr, cols), lambda i: (i, 0))
    return pl.pallas_call(
        body, name=name, grid=(rows // tr,), in_specs=[spec] * 5, out_specs=[spec] * 4,
        out_shape=[jax.ShapeDtypeStruct((rows, cols), F32)] * 4,
        compiler_params=_params(("arbitrary",)),
    )(p_own, p_sibling, w, m, v)


def _adam_small(total, wvec, mvec, vvec, lb_of):
    def body(t_ref, w_ref, m_ref, v_ref, g_ref, d_ref, nm_ref, nv_ref):
        g_ref[:, :SM_LB] = t_ref[:, :SM_LB]
        _, vjp = jax.vjp(lb_of, w_ref[:, PK_LB0:PK_LB1], w_ref[:, PK_LB1:PK_LEN])
        dl0, dl1 = vjp(t_ref[:, SM_LB:SM_LOSS])
        g_ref[:, PK_LB0:PK_LB1] = dl0
        g_ref[:, PK_LB1:PK_LEN] = dl1
        d_ref[...], nm_ref[...], nv_ref[...] = _adamw(w_ref[...], g_ref[...], m_ref[...], v_ref[...])

    return pl.pallas_call(body, name="adam_small", out_shape=[jax.ShapeDtypeStruct((1, PK_LEN), F32)] * 4)(
        total, wvec, mvec, vvec)


_BIG = ("ffn1_w1", "ffn1_w3", "ffn1_w2", "w_in", "w_proj_attn", "w_proj_hgrn", "w_out", "ffn2_w1", "ffn2_w3", "ffn2_w2")
_SMALL = ("ln1_g", "ln1_b", "ln2_g", "ln2_b", "ln3_g", "ln3_b", "b_in", "attn_sinks", "hgrn_norm_g", "hgrn_lb_logits")
_ORDER = ("ln1_g", "ln1_b", "ffn1_w1", "ffn1_w3", "ffn1_w2", "ln2_g", "ln2_b", "w_in", "b_in", "attn_sinks",
          "hgrn_lb_logits", "hgrn_norm_g", "w_proj_attn", "w_proj_hgrn", "w_out", "ln3_g", "ln3_b",
          "ffn2_w1", "ffn2_w3", "ffn2_w2")


def _pad_to(a, rows, cols):
    return jnp.pad(a, ((0, rows - a.shape[0]), (0, cols - a.shape[1])))


def _send_form(name, w):
    wb = _bf(w)
    if name.endswith(("w1", "w3")) and name.startswith("ffn"):
        return _pad_to(wb, D_MODEL, FF_PAD)
    if name.endswith("w2") and name.startswith("ffn"):
        return _pad_to(wb, FF_PAD, D_MODEL)
    return wb


def _pack_small(p):
    pieces = [p[k].reshape(1, -1) for k in ("ln1_g", "ln1_b", "ln2_g", "ln2_b", "ln3_g", "ln3_b", "b_in")]
    pieces.append(_pad_to(p["attn_sinks"].reshape(1, -1), 1, LANES))
    pieces.append(p["hgrn_norm_g"].reshape(1, -1))
    pieces.append(p["hgrn_lb_logits"].reshape(1, -1))
    return jnp.concatenate(pieces, axis=1)


def _unpack_small(vec, like):
    out = {}
    for i, k in enumerate(("ln1_g", "ln1_b", "ln2_g", "ln2_b", "ln3_g", "ln3_b")):
        out[k] = vec[:, i * D_MODEL:(i + 1) * D_MODEL]
    out["b_in"] = vec[:, SM_BIN:SM_SINK]
    out["attn_sinks"] = vec[:, SM_SINK:SM_SINK + N_Q_HEADS]
    out["hgrn_norm_g"] = vec[:, SM_NG:SM_LB]
    out["hgrn_lb_logits"] = vec[:, PK_LB0:PK_LEN].reshape(2, D_MODEL)
    return {k: v.reshape(like[k].shape) for k, v in out.items()}


def kernel(x, ln1_g, ln1_b, ffn1_w1, ffn1_w3, ffn1_w2, ln2_g, ln2_b, w_in, b_in, attn_sinks, hgrn_lb_logits, hgrn_norm_g, w_proj_attn, w_proj_hgrn, w_out, ln3_g, ln3_b, ffn2_w1, ffn2_w3, ffn2_w2, loss_target, m_ln1_g, m_ln1_b, m_ffn1_w1, m_ffn1_w3, m_ffn1_w2, m_ln2_g, m_ln2_b, m_w_in, m_b_in, m_attn_sinks, m_hgrn_lb_logits, m_hgrn_norm_g, m_w_proj_attn, m_w_proj_hgrn, m_w_out, m_ln3_g, m_ln3_b, m_ffn2_w1, m_ffn2_w3, m_ffn2_w2, v_ln1_g, v_ln1_b, v_ffn1_w1, v_ffn1_w3, v_ffn1_w2, v_ln2_g, v_ln2_b, v_w_in, v_b_in, v_attn_sinks, v_hgrn_lb_logits, v_hgrn_norm_g, v_w_proj_attn, v_w_proj_hgrn, v_w_out, v_ln3_g, v_ln3_b, v_ffn2_w1, v_ffn2_w3, v_ffn2_w2):
    w = dict(ln1_g=ln1_g, ln1_b=ln1_b, ffn1_w1=ffn1_w1, ffn1_w3=ffn1_w3, ffn1_w2=ffn1_w2, ln2_g=ln2_g, ln2_b=ln2_b,
             w_in=w_in, b_in=b_in, attn_sinks=attn_sinks, hgrn_lb_logits=hgrn_lb_logits, hgrn_norm_g=hgrn_norm_g,
             w_proj_attn=w_proj_attn, w_proj_hgrn=w_proj_hgrn, w_out=w_out, ln3_g=ln3_g, ln3_b=ln3_b,
             ffn2_w1=ffn2_w1, ffn2_w3=ffn2_w3, ffn2_w2=ffn2_w2)
    mom = dict(ln1_g=m_ln1_g, ln1_b=m_ln1_b, ffn1_w1=m_ffn1_w1, ffn1_w3=m_ffn1_w3, ffn1_w2=m_ffn1_w2, ln2_g=m_ln2_g,
               ln2_b=m_ln2_b, w_in=m_w_in, b_in=m_b_in, attn_sinks=m_attn_sinks, hgrn_lb_logits=m_hgrn_lb_logits,
               hgrn_norm_g=m_hgrn_norm_g, w_proj_attn=m_w_proj_attn, w_proj_hgrn=m_w_proj_hgrn, w_out=m_w_out,
               ln3_g=m_ln3_g, ln3_b=m_ln3_b, ffn2_w1=m_ffn2_w1, ffn2_w3=m_ffn2_w3, ffn2_w2=m_ffn2_w2)
    var = dict(ln1_g=v_ln1_g, ln1_b=v_ln1_b, ffn1_w1=v_ffn1_w1, ffn1_w3=v_ffn1_w3, ffn1_w2=v_ffn1_w2, ln2_g=v_ln2_g,
               ln2_b=v_ln2_b, w_in=v_w_in, b_in=v_b_in, attn_sinks=v_attn_sinks, hgrn_lb_logits=v_hgrn_lb_logits,
               hgrn_norm_g=v_hgrn_norm_g, w_proj_attn=v_w_proj_attn, w_proj_hgrn=v_w_proj_hgrn, w_out=v_w_out,
               ln3_g=v_ln3_g, ln3_b=v_ln3_b, ffn2_w1=v_ffn2_w1, ffn2_w3=v_ffn2_w3, ffn2_w2=v_ffn2_w2)

    n_tok = x.shape[1]
    x0 = x.reshape(n_tok, D_MODEL)
    target = loss_target.reshape(n_tok, D_MODEL)

    gathered = dict(zip(_BIG, _gather_weights([_send_form(k, w[k][0]) for k in _BIG])))
    sq = lambda k: gathered[k].reshape(D_MODEL, D_MODEL)
    w_pa, w_ph, w_o = sq("w_proj_attn"), sq("w_proj_hgrn"), sq("w_out")
    f1 = (gathered["ffn1_w1"], gathered["ffn1_w3"], gathered["ffn1_w2"])
    f2 = (gathered["ffn2_w1"], gathered["ffn2_w3"], gathered["ffn2_w2"])

    tabs = _rope_tables(n_tok)
    lb, lb_of = _lb_fwd(hgrn_lb_logits)
    z1, x1 = _ffn_fwd("ffn1_fwd", x0, *f1, ln1_g, ln1_b)
    proj = _in_proj(x1, gathered["w_in"], b_in)
    y_attn = _attn_fwd(proj, tabs, attn_sinks)
    y_hgrn, states = _hgrn_fwd(proj, lb, hgrn_norm_g)
    z2, x2 = _mix_fwd(y_attn, y_hgrn, proj, x1, w_pa, w_ph, w_o, ln2_g, ln2_b)
    z3, dy, loss_part = _ffn_fwd("ffn2_fwd", x2, *f2, ln3_g, ln3_b, target=target)

    big = {}
    small = {}
    dx2, a2, dh1_2, dh3_2, df2, small["ln3_g"], small["ln3_b"] = _ffn_bwd("ffn2_bwd", x2, z3, dy, *f2, ln3_g, ln3_b)
    big["ffn2_w1"] = _grad_matmul("ffn2_dw1", x2, "shared", D_MODEL, dh1_2, "batch", FF_PAD)
    big["ffn2_w3"] = _grad_matmul("ffn2_dw3", x2, "shared", D_MODEL, dh3_2, "batch", FF_PAD)
    big["ffn2_w2"] = _grad_matmul("ffn2_dw2", a2, "batch", FF_PAD, df2, "shared", D_MODEL)
    (dz2, merged, dya, dyh, dy_attn, dy_hgrn, dgates, small["ln2_g"], small["ln2_b"]) = _mix_bwd(
        dx2, z2, y_attn, y_hgrn, proj, w_pa, w_ph, w_o, ln2_g, ln2_b)
    big["w_out"] = _grad_matmul("dw_out", merged, "cols", PROJ_SHARD, dz2, "shared", D_MODEL)
    big["w_proj_attn"] = _grad_matmul("dw_proj_attn", y_attn, "cols", PROJ_SHARD, dya, "shared", D_MODEL)
    big["w_proj_hgrn"] = _grad_matmul("dw_proj_hgrn", y_hgrn, "cols", PROJ_SHARD, dyh, "shared", D_MODEL)
    dfl, dqh, dih, dog, dlb, small["hgrn_norm_g"] = _hgrn_bwd(proj, lb, hgrn_norm_g, states, dy_hgrn)
    dq, dkv, dsinks = _attn_bwd(proj, dy_attn, tabs, attn_sinks)
    dproj = jnp.concatenate([dq, dkv, dfl, dqh, dih, dog, dgates], axis=1)
    big["w_in"], small["b_in"] = _grad_matmul("dw_in", x1, "shared", D_MODEL, dproj, "cols", IN_SHARD, colsum=True)
    dx1 = _in_proj_dx(dproj, gathered["w_in"], dz2)
    grad_x, a1, dh1_1, dh3_1, df1, small["ln1_g"], small["ln1_b"] = _ffn_bwd("ffn1_bwd", x0, z1, dx1, *f1, ln1_g, ln1_b)
    big["ffn1_w1"] = _grad_matmul("ffn1_dw1", x0, "shared", D_MODEL, dh1_1, "batch", FF_PAD)
    big["ffn1_w3"] = _grad_matmul("ffn1_dw3", x0, "shared", D_MODEL, dh3_1, "batch", FF_PAD)
    big["ffn1_w2"] = _grad_matmul("ffn1_dw2", a1, "batch", FF_PAD, df1, "shared", D_MODEL)

    slots = _exchange_grads([big[k] for k in _BIG])
    partial = []
    for k, s in zip(_BIG, slots):
        p = _sum_slots("sum_" + k, s)
        partial.append(p[:w[k].shape[1], :w[k].shape[2]])
    from_sibling = _swap_with_sibling(partial)

    outs = {"grad": {}, "delta": {}, "m": {}, "v": {}}
    for k, p, q in zip(_BIG, partial, from_sibling):
        res = _adam_big("adam_" + k, p, q, w[k][0], mom[k][0], var[k][0])
        for kind, r in zip(("grad", "delta", "m", "v"), res):
            outs[kind][k] = r.reshape(w[k].shape)

    part = jnp.concatenate(
        [small[k] for k in ("ln1_g", "ln1_b", "ln2_g", "ln2_b", "ln3_g", "ln3_b", "b_in")]
        + [dsinks, small["hgrn_norm_g"], dlb, loss_part], axis=1)
    total = _sum_small(part)
    res = _adam_small(total, _pack_small(w), _pack_small(mom), _pack_small(var), lb_of)
    for kind, r in zip(("grad", "delta", "m", "v"), res):
        outs[kind].update(_unpack_small(r, w))
    loss = total[0, SM_LOSS]

    return (loss, grad_x.reshape(x.shape), *[outs["grad"][k] for k in _ORDER], *[outs["delta"][k] for k in _ORDER],
            *[outs["m"][k] for k in _ORDER], *[outs["v"][k] for k in _ORDER])
```

```python
import functools

import jax
import jax.numpy as jnp
from jax import lax
from jax.experimental import pallas as pl
from jax.experimental.pallas import tpu as pltpu

F32 = jnp.float32
BF16 = jnp.bfloat16

D_MODEL = 1024
N_Q_HEADS = 16
N_KV_HEADS = 4
HEAD_DIM = 64
ATTN_BLOCK = 128
ROPE_THETA = 500000.0
ROPE_DIM = HEAD_DIM // 4
HGRN_HEADS = 8
HGRN_DK = 128
HGRN_CHUNK = 64
D_FF = 2816
D_IN = 7680
DEEPNORM_ALPHA = 2 ** 0.25
LN_EPS = 1e-5
RMS_EPS = 1e-6
NEG_INF = -1e30

ADAM_LR = 0.001
ADAM_B1 = 0.9
ADAM_B2 = 0.999
ADAM_EPS = 1e-08
ADAM_WD = 0.01
ADAM_STEP = 10

N_CHIPS = 4
N_DEV = 8
FF_SHARD = D_FF // N_CHIPS
LANES = 128
MXU_COLS = 256
FF_PAD = -(-FF_SHARD // MXU_COLS) * MXU_COLS
IN_SHARD = D_IN // N_CHIPS
PROJ_SHARD = D_MODEL // N_CHIPS
ROW_TILE = 512
MIX_TILE = 256
UPDATE_ROWS = 128
HGRN_CHUNKS_PER_STEP = 8
VMEM_LIMIT = 56 * 1024 * 1024

COL_K = 1024 // 256
COL_V = 1280 // 256
COL_F = 1536 // LANES
COL_QH = 2560 // LANES
COL_IH = 3584 // LANES
COL_OG = 4608 // LANES
COL_GA = 5632 // 512
COL_GH = 6656 // 512

SM_LN = 0
SM_BIN = 6 * D_MODEL
SM_SINK = SM_BIN + D_IN
SM_NG = SM_SINK + LANES
SM_LB = SM_NG + LANES
SM_LOSS = SM_LB + D_MODEL
SM_LEN = SM_LOSS + LANES
PK_LB0 = SM_LB
PK_LB1 = SM_LB + D_MODEL
PK_LEN = PK_LB1 + D_MODEL

MESH = pl.DeviceIdType.MESH


def _mm(a, b):
    return lax.dot_general(a, b, (((1,), (0,)), ((), ())), preferred_element_type=F32)


def _mm_nt(a, b):
    return lax.dot_general(a, b, (((1,), (1,)), ((), ())), preferred_element_type=F32)


def _mm_tn(a, b):
    return lax.dot_general(a, b, (((0,), (0,)), ((), ())), preferred_element_type=F32)


def _bf(v):
    return v.astype(BF16)


def _sig(v):
    return jax.nn.sigmoid(v)


def _ln(z, g, b):
    mu = jnp.mean(z, axis=-1, keepdims=True)
    zc = z - mu
    var = jnp.mean(zc * zc, axis=-1, keepdims=True)
    return zc * lax.rsqrt(var + LN_EPS) * g + b


def _swiglu_act(h1, h3):
    return (h1 * _sig(h1)) * h3


def _params(sem=None):
    return pltpu.CompilerParams(dimension_semantics=sem, vmem_limit_bytes=VMEM_LIMIT)


def _full(shape):
    nd = len(shape)
    return pl.BlockSpec(shape, lambda *_: (0,) * nd)


def _update_rows(rows):
    return max(t for t in range(8, UPDATE_ROWS + 1, 8) if rows % t == 0)


def _ffn_fwd(name, x, w1g, w3g, w2g, g, b, target=None):
    n = x.shape[0]
    tm = min(ROW_TILE, n)
    final = target is not None

    def body(*refs):
        if final:
            x_ref, w1_ref, w3_ref, w2_ref, g_ref, b_ref, t_ref, z_ref, o_ref, loss_ref, acc, xb = refs
        else:
            x_ref, w1_ref, w3_ref, w2_ref, g_ref, b_ref, z_ref, o_ref, acc, xb = refs
        i = pl.program_id(0)
        j = pl.program_id(1)

        @pl.when(j == 0)
        def _():
            xb[...] = _bf(x_ref[...])
            acc[...] = jnp.zeros_like(acc)

        h1 = _mm(xb[...], w1_ref[0])
        h3 = _mm(xb[...], w3_ref[0])
        acc[...] += _mm(_bf(_swiglu_act(h1, h3)), w2_ref[0])

        @pl.when(j == N_CHIPS - 1)
        def _():
            z = DEEPNORM_ALPHA * x_ref[...] + 0.5 * acc[...]
            z_ref[...] = z
            y = _ln(z, g_ref[...], b_ref[...])
            if final:
                e = y - t_ref[...]

                @pl.when(i == 0)
                def _():
                    loss_ref[...] = jnp.zeros_like(loss_ref)

                loss_ref[...] += jnp.sum(e * e) * (0.5 / D_MODEL)
                o_ref[...] = e * (1.0 / D_MODEL)
            else:
                o_ref[...] = y

    row = pl.BlockSpec((tm, D_MODEL), lambda i, j: (i, 0))
    wcol = pl.BlockSpec((1, D_MODEL, FF_PAD), lambda i, j: (j, 0, 0))
    wrow = pl.BlockSpec((1, FF_PAD, D_MODEL), lambda i, j: (j, 0, 0))
    vec = pl.BlockSpec((1, D_MODEL), lambda i, j: (0, 0))
    in_specs = [row, wcol, wcol, wrow, vec, vec]
    args = [x, w1g, w3g, w2g, g, b]
    out_specs = [row, row]
    out_shape = [jax.ShapeDtypeStruct((n, D_MODEL), F32)] * 2
    if final:
        in_specs.append(row)
        args.append(target)
        out_specs.append(pl.BlockSpec((1, LANES), lambda i, j: (0, 0)))
        out_shape.append(jax.ShapeDtypeStruct((1, LANES), F32))
    return pl.pallas_call(
        body, name=name, grid=(n // tm, N_CHIPS), in_specs=in_specs, out_specs=out_specs, out_shape=out_shape,
        scratch_shapes=[pltpu.VMEM((tm, D_MODEL), F32), pltpu.VMEM((tm, D_MODEL), BF16)],
        compiler_params=_params(("arbitrary", "arbitrary")),
    )(*args)


def _ffn_bwd(name, x, z, dout, w1g, w3g, w2g, g, b):
    n = x.shape[0]
    tm = min(ROW_TILE, n)

    def body(x_ref, z_ref, do_ref, w1_ref, w3_ref, w2_ref, g_ref, b_ref,
             dx_ref, a_ref, dh1_ref, dh3_ref, df_ref, dg_ref, db_ref, acc, xb, dzs):
        i = pl.program_id(0)
        j = pl.program_id(1)

        @pl.when(j == 0)
        def _():
            xb[...] = _bf(x_ref[...])
            acc[...] = jnp.zeros_like(acc)
            _, vjp = jax.vjp(_ln, z_ref[...], g_ref[...], b_ref[...])
            dz, dg, db = vjp(do_ref[...])
            dzs[...] = dz
            df_ref[...] = _bf(0.5 * dz)

            @pl.when(i == 0)
            def _():
                dg_ref[...] = jnp.zeros_like(dg_ref)
                db_ref[...] = jnp.zeros_like(db_ref)

            dg_ref[...] += dg
            db_ref[...] += db

        h1 = _mm(xb[...], w1_ref[0])
        h3 = _mm(xb[...], w3_ref[0])
        a, act_vjp = jax.vjp(_swiglu_act, h1, h3)
        da = _mm_nt(df_ref[...], w2_ref[0])
        dh1, dh3 = act_vjp(da)
        dh1 = _bf(dh1)
        dh3 = _bf(dh3)
        a_ref[0] = _bf(a)
        dh1_ref[0] = dh1
        dh3_ref[0] = dh3
        acc[...] += _mm_nt(dh1, w1_ref[0]) + _mm_nt(dh3, w3_ref[0])

        @pl.when(j == N_CHIPS - 1)
        def _():
            dx_ref[...] = DEEPNORM_ALPHA * dzs[...] + acc[...]

    row = pl.BlockSpec((tm, D_MODEL), lambda i, j: (i, 0))
    wcol = pl.BlockSpec((1, D_MODEL, FF_PAD), lambda i, j: (j, 0, 0))
    wrow = pl.BlockSpec((1, FF_PAD, D_MODEL), lambda i, j: (j, 0, 0))
    vec = pl.BlockSpec((1, D_MODEL), lambda i, j: (0, 0))
    hid = pl.BlockSpec((1, tm, FF_PAD), lambda i, j: (j, i, 0))
    hid_shape = jax.ShapeDtypeStruct((N_CHIPS, n, FF_PAD), BF16)
    return pl.pallas_call(
        body, name=name, grid=(n // tm, N_CHIPS),
        in_specs=[row, row, row, wcol, wcol, wrow, vec, vec],
        out_specs=[row, hid, hid, hid, row, vec, vec],
        out_shape=[jax.ShapeDtypeStruct((n, D_MODEL), F32), hid_shape, hid_shape, hid_shape,
                   jax.ShapeDtypeStruct((n, D_MODEL), BF16),
                   jax.ShapeDtypeStruct((1, D_MODEL), F32), jax.ShapeDtypeStruct((1, D_MODEL), F32)],
        scratch_shapes=[pltpu.VMEM((tm, D_MODEL), F32), pltpu.VMEM((tm, D_MODEL), BF16), pltpu.VMEM((tm, D_MODEL), F32)],
        compiler_params=_params(("arbitrary", "arbitrary")),
    )(x, z, dout, w1g, w3g, w2g, g, b)


def _operand_spec(arr, mode, tn, width):
    if mode == "shared":
        return pl.BlockSpec((tn, width), lambda s, k: (k, 0))
    if mode == "batch":
        return pl.BlockSpec((1, tn, width), lambda s, k: (s, k, 0))
    assert mode == "cols" and arr.shape[1] == N_CHIPS * width
    return pl.BlockSpec((tn, width), lambda s, k: (k, s))


def _grad_matmul(name, a, a_mode, ka, b, b_mode, kb, colsum=False):
    n = a.shape[-2]
    tn = min(ROW_TILE, n)
    nk = n // tn

    def body(*refs):
        if colsum:
            a_ref, b_ref, o_ref, cs_ref, acc = refs
        else:
            a_ref, b_ref, o_ref, acc = refs
        k = pl.program_id(1)
        av = a_ref[0] if a_mode == "batch" else a_ref[...]
        bv = b_ref[0] if b_mode == "batch" else b_ref[...]

        @pl.when(k == 0)
        def _():
            acc[...] = jnp.zeros_like(acc)
            if colsum:
                cs_ref[...] = jnp.zeros_like(cs_ref)

        acc[...] += _mm_tn(_bf(av), _bf(bv))
        if colsum:
            cs_ref[...] += jnp.sum(bv.astype(F32), axis=0, keepdims=True)

        @pl.when(k == nk - 1)
        def _():
            o_ref[0] = _bf(acc[...])

    out_specs = [pl.BlockSpec((1, ka, kb), lambda s, k: (s, 0, 0))]
    out_shape = [jax.ShapeDtypeStruct((N_CHIPS, ka, kb), BF16)]
    if colsum:
        out_specs.append(pl.BlockSpec((1, kb), lambda s, k: (0, s)))
        out_shape.append(jax.ShapeDtypeStruct((1, N_CHIPS * kb), F32))
    res = pl.pallas_call(
        body, name=name, grid=(N_CHIPS, nk),
        in_specs=[_operand_spec(a, a_mode, tn, ka), _operand_spec(b, b_mode, tn, kb)],
        out_specs=out_specs, out_shape=out_shape,
        scratch_shapes=[pltpu.VMEM((ka, kb), F32)],
        compiler_params=_params(("arbitrary", "arbitrary")),
    )(a, b)
    return res if colsum else res[0]


def _in_proj(x1, w_in_g, b_in):
    n = x1.shape[0]
    tm = min(ROW_TILE, n)

    def body(x_ref, w_ref, b_ref, o_ref, xb):
        @pl.when(pl.program_id(1) == 0)
        def _():
            xb[...] = _bf(x_ref[...])

        o_ref[...] = _mm(xb[...], w_ref[0]) + b_ref[...]

    return pl.pallas_call(
        body, name="in_proj", grid=(n // tm, N_CHIPS),
        in_specs=[pl.BlockSpec((tm, D_MODEL), lambda i, j: (i, 0)),
                  pl.BlockSpec((1, D_MODEL, IN_SHARD), lambda i, j: (j, 0, 0)),
                  pl.BlockSpec((1, IN_SHARD), lambda i, j: (0, j))],
        out_specs=pl.BlockSpec((tm, IN_SHARD), lambda i, j: (i, j)),
        out_shape=jax.ShapeDtypeStruct((n, D_IN), F32),
        scratch_shapes=[pltpu.VMEM((tm, D_MODEL), BF16)],
        compiler_params=_params(("arbitrary", "arbitrary")),
    )(x1, w_in_g, b_in)


def _in_proj_dx(dproj, w_in_g, dz2):
    n = dproj.shape[0]
    tm = min(ROW_TILE, n)

    def body(dp_ref, w_ref, dz_ref, o_ref, acc):
        j = pl.program_id(1)

        @pl.when(j == 0)
        def _():
            acc[...] = jnp.zeros_like(acc)

        acc[...] += _mm_nt(dp_ref[...], w_ref[0])

        @pl.when(j == N_CHIPS - 1)
        def _():
            o_ref[...] = DEEPNORM_ALPHA * dz_ref[...] + acc[...]

    return pl.pallas_call(
        body, name="in_proj_dx", grid=(n // tm, N_CHIPS),
        in_specs=[pl.BlockSpec((tm, IN_SHARD), lambda i, j: (i, j)),
                  pl.BlockSpec((1, D_MODEL, IN_SHARD), lambda i, j: (j, 0, 0)),
                  pl.BlockSpec((tm, D_MODEL), lambda i, j: (i, 0))],
        out_specs=pl.BlockSpec((tm, D_MODEL), lambda i, j: (i, 0)),
        out_shape=jax.ShapeDtypeStruct((n, D_MODEL), F32),
        scratch_shapes=[pltpu.VMEM((tm, D_MODEL), F32)],
        compiler_params=_params(("arbitrary", "arbitrary")),
    )(dproj, w_in_g, dz2)


def _rope_tables(seq_len):
    pos = jnp.arange(seq_len, dtype=F32)
    inv_freq = ROPE_THETA ** (-jnp.arange(0, ROPE_DIM, 2, dtype=F32) / ROPE_DIM)
    ang = pos[:, None] * inv_freq[None, :]
    cos, sin = jnp.cos(ang), jnp.sin(ang)
    half = ROPE_DIM // 2
    rest = HEAD_DIM - ROPE_DIM
    ones = jnp.ones((seq_len, rest), F32)
    zeros = jnp.zeros((seq_len, rest), F32)
    zh = jnp.zeros((seq_len, half), F32)
    c = jnp.concatenate([cos, cos, ones], axis=1)
    sa = jnp.concatenate([-sin, zh, zeros], axis=1)
    sb = jnp.concatenate([zh, sin, zeros], axis=1)
    reps = LANES // HEAD_DIM
    return tuple(jnp.tile(t, (1, reps)) for t in (c, sa, sb))


def _rope(t, c, sa, sb):
    w = t.shape[1]
    reps = w // LANES
    half = ROPE_DIM // 2
    return (t * jnp.tile(c, (1, reps)) + pltpu.roll(t, w - half, 1) * jnp.tile(sa, (1, reps))
            + pltpu.roll(t, half, 1) * jnp.tile(sb, (1, reps)))


def _rope_transposed(g, c, sa, sb):
    w = g.shape[1]
    reps = w // LANES
    half = ROPE_DIM // 2
    return (g * jnp.tile(c, (1, reps)) + pltpu.roll(g * jnp.tile(sa, (1, reps)), half, 1)
            + pltpu.roll(g * jnp.tile(sb, (1, reps)), w - half, 1))


def _attn_mask(n):
    qi = lax.broadcasted_iota(jnp.int32, (ATTN_BLOCK, 2 * ATTN_BLOCK), 0)
    kj = lax.broadcasted_iota(jnp.int32, (ATTN_BLOCK, 2 * ATTN_BLOCK), 1)
    dist = qi + ATTN_BLOCK - kj
    return (dist >= 0) & (dist < ATTN_BLOCK) & (n * ATTN_BLOCK + kj - ATTN_BLOCK >= 0)


def _head_halves(t_pair):
    swapped = pltpu.roll(t_pair, HEAD_DIM, 1)
    return ((_bf(t_pair), _bf(swapped)), (_bf(swapped), _bf(t_pair)))


def _attn_probs(q_masked, k_sel, mask, sink):
    s = _mm_nt(q_masked, k_sel) * (HEAD_DIM ** -0.5)
    s = jnp.where(mask, s, NEG_INF)
    m = jnp.maximum(jnp.max(s, axis=-1, keepdims=True), sink)
    p = jnp.exp(s - m)
    e_sink = jnp.exp(sink - m)
    denom = jnp.sum(p, axis=-1, keepdims=True) + e_sink
    return p / denom, e_sink / denom


def _attn_fwd(proj, tabs, sinks):
    n_tok = proj.shape[0]
    nb = n_tok // ATTN_BLOCK
    group = N_Q_HEADS // N_KV_HEADS

    def body(q_ref, k_ref, v_ref, c_ref, sa_ref, sb_ref, sink_ref, y_ref, kprev, vprev):
        n = pl.program_id(0)

        @pl.when(n == 0)
        def _():
            kprev[...] = jnp.zeros_like(kprev)
            vprev[...] = jnp.zeros_like(vprev)

        c, sa, sb = c_ref[...], sa_ref[...], sb_ref[...]
        qr = _bf(_rope(q_ref[...], c, sa, sb))
        kr = _rope(k_ref[...], c, sa, sb)
        vc = v_ref[...]
        kk = jnp.concatenate([kprev[...], kr], axis=0)
        vv = jnp.concatenate([vprev[...], vc], axis=0)
        kprev[...] = kr
        vprev[...] = vc
        mask = _attn_mask(n)
        lo = lax.broadcasted_iota(jnp.int32, (ATTN_BLOCK, LANES), 1) < HEAD_DIM
        for r in range(N_KV_HEADS // 2):
            k_var = _head_halves(kk[:, r * LANES:(r + 1) * LANES])
            v_var = _head_halves(vv[:, r * LANES:(r + 1) * LANES])
            for kh2 in range(2):
                kh = 2 * r + kh2
                for gp in range(group // 2):
                    pair = (kh * group) // 2 + gp
                    q2 = qr[:, pair * LANES:(pair + 1) * LANES]
                    outs = []
                    for half in range(2):
                        h = 2 * pair + half
                        qm = jnp.where(lo if half == 0 else ~lo, q2, jnp.zeros_like(q2))
                        probs, _ = _attn_probs(qm, k_var[kh2][half], mask, sink_ref[0, h])
                        outs.append(_mm(_bf(probs), v_var[kh2][half]))
                    y_ref[:, pair * LANES:(pair + 1) * LANES] = jnp.where(lo, outs[0], outs[1])

    blk = lambda width, col: pl.BlockSpec((ATTN_BLOCK, width), lambda n: (n, col))
    tab = pl.BlockSpec((ATTN_BLOCK, LANES), lambda n: (n, 0))
    kvw = N_KV_HEADS * HEAD_DIM
    return pl.pallas_call(
        body, name="attn_fwd", grid=(nb,),
        in_specs=[blk(D_MODEL, 0), blk(kvw, COL_K), blk(kvw, COL_V), tab, tab, tab,
                  pl.BlockSpec(memory_space=pltpu.SMEM)],
        out_specs=pl.BlockSpec((ATTN_BLOCK, D_MODEL), lambda n: (n, 0)),
        out_shape=jax.ShapeDtypeStruct((n_tok, D_MODEL), F32),
        scratch_shapes=[pltpu.VMEM((ATTN_BLOCK, kvw), F32), pltpu.VMEM((ATTN_BLOCK, kvw), F32)],
        compiler_params=_params(("arbitrary",)),
    )(proj, proj, proj, *tabs, sinks)


def _attn_bwd(proj, dy, tabs, sinks):
    n_tok = proj.shape[0]
    nb = n_tok // ATTN_BLOCK
    group = N_Q_HEADS // N_KV_HEADS
    kvw = N_KV_HEADS * HEAD_DIM

    def body(q_ref, k_ref, v_ref, do_ref, c_ref, sa_ref, sb_ref, cp_ref, sap_ref, sbp_ref, sink_ref,
             dq_ref, dkv_ref, dsink_ref, kprev, vprev, dkc, dvc):
        n = pl.program_id(0)

        @pl.when(n == 0)
        def _():
            for ref in (kprev, vprev, dkc, dvc, dsink_ref):
                ref[...] = jnp.zeros_like(ref)

        prev_tabs = (cp_ref[...], sap_ref[...], sbp_ref[...])

        @pl.when(n < nb)
        def _():
            c, sa, sb = c_ref[...], sa_ref[...], sb_ref[...]
            qr = _bf(_rope(q_ref[...], c, sa, sb))
            kr = _rope(k_ref[...], c, sa, sb)
            vc = v_ref[...]
            kk = jnp.concatenate([kprev[...], kr], axis=0)
            vv = jnp.concatenate([vprev[...], vc], axis=0)
            kprev[...] = kr
            vprev[...] = vc
            mask = _attn_mask(n)
            lane = lax.broadcasted_iota(jnp.int32, (ATTN_BLOCK, LANES), 1)
            lo = lane < HEAD_DIM
            lo2 = lax.broadcasted_iota(jnp.int32, (2 * ATTN_BLOCK, LANES), 1) < HEAD_DIM
            dsink = jnp.zeros((1, LANES), F32)
            dq_pairs = []
            dk_pairs = []
            dv_pairs = []
            for r in range(N_KV_HEADS // 2):
                k_var = _head_halves(kk[:, r * LANES:(r + 1) * LANES])
                v_var = _head_halves(vv[:, r * LANES:(r + 1) * LANES])
                dk_full = []
                dv_full = []
                for kh2 in range(2):
                    kh = 2 * r + kh2
                    dk_acc = jnp.zeros((2 * ATTN_BLOCK, LANES), F32)
                    dv_acc = jnp.zeros((2 * ATTN_BLOCK, LANES), F32)
                    for gp in range(group // 2):
                        pair = (kh * group) // 2 + gp
                        q2 = qr[:, pair * LANES:(pair + 1) * LANES]
                        do2 = do_ref[:, pair * LANES:(pair + 1) * LANES]
                        dqs = []
                        for half in range(2):
                            h = 2 * pair + half
                            sel = lo if half == 0 else ~lo
                            qm = jnp.where(sel, q2, jnp.zeros_like(q2))
                            dom = _bf(jnp.where(sel, do2, 0.0))
                            k_sel = k_var[kh2][half]
                            v_sel = v_var[kh2][half]
                            probs, p_sink = _attn_probs(qm, k_sel, mask, sink_ref[0, h])
                            dp = _mm_nt(dom, v_sel)
                            delta = jnp.sum(probs * dp, axis=-1, keepdims=True)
                            ds = _bf(probs * (dp - delta) * (HEAD_DIM ** -0.5))
                            dsink = dsink + jnp.where(lane[:1] == h, -jnp.sum(p_sink * delta), 0.0)
                            dqs.append(_mm(ds, k_sel))
                            dk_acc = dk_acc + _mm_tn(ds, qm)
                            dv_acc = dv_acc + _mm_tn(_bf(probs), dom)
                        dq_pairs.append(jnp.where(lo, dqs[0], dqs[1]))
                    dk_full.append(dk_acc + pltpu.roll(dk_acc, HEAD_DIM, 1))
                    dv_full.append(dv_acc + pltpu.roll(dv_acc, HEAD_DIM, 1))
                dk_pairs.append(jnp.where(lo2, dk_full[0], dk_full[1]))
                dv_pairs.append(jnp.where(lo2, dv_full[0], dv_full[1]))
            dsink_ref[...] += dsink
            dq_ref[...] = _bf(_rope_transposed(jnp.concatenate(dq_pairs, axis=1), c, sa, sb))
            dk_all = jnp.concatenate(dk_pairs, axis=1)
            dv_all = jnp.concatenate(dv_pairs, axis=1)
            dkv_ref[:, :kvw] = _bf(_rope_transposed(dkc[...] + dk_all[:ATTN_BLOCK], *prev_tabs))
            dkv_ref[:, kvw:] = _bf(dvc[...] + dv_all[:ATTN_BLOCK])
            dkc[...] = dk_all[ATTN_BLOCK:]
            dvc[...] = dv_all[ATTN_BLOCK:]

        @pl.when(n == nb)
        def _():
            dkv_ref[:, :kvw] = _bf(_rope_transposed(dkc[...], *prev_tabs))
            dkv_ref[:, kvw:] = _bf(dvc[...])

    cur = lambda n: jnp.minimum(n, nb - 1)
    prev = lambda n: jnp.maximum(n - 1, 0)
    blk = lambda width, col: pl.BlockSpec((ATTN_BLOCK, width), lambda n: (cur(n), col))
    tab = pl.BlockSpec((ATTN_BLOCK, LANES), lambda n: (cur(n), 0))
    tabp = pl.BlockSpec((ATTN_BLOCK, LANES), lambda n: (prev(n), 0))
    return pl.pallas_call(
        body, name="attn_bwd", grid=(nb + 1,),
        in_specs=[blk(D_MODEL, 0), blk(kvw, COL_K), blk(kvw, COL_V), blk(D_MODEL, 0), tab, tab, tab, tabp, tabp, tabp,
                  pl.BlockSpec(memory_space=pltpu.SMEM)],
        out_specs=[pl.BlockSpec((ATTN_BLOCK, D_MODEL), lambda n: (cur(n), 0)),
                   pl.BlockSpec((ATTN_BLOCK, 2 * kvw), lambda n: (prev(n), 0)),
                   pl.BlockSpec((1, LANES), lambda n: (0, 0))],
        out_shape=[jax.ShapeDtypeStruct((n_tok, D_MODEL), BF16), jax.ShapeDtypeStruct((n_tok, 2 * kvw), BF16),
                   jax.ShapeDtypeStruct((1, LANES), F32)],
        scratch_shapes=[pltpu.VMEM((ATTN_BLOCK, kvw), F32)] * 4,
        compiler_params=_params(("arbitrary",)),
    )(proj, proj, proj, dy, *tabs, *tabs, sinks)


def _bmm(a, b):
    return lax.dot_general(a, b, (((2,), (1,)), ((0,), (0,))), preferred_element_type=F32)


def _bmm_nt(a, b):
    return lax.dot_general(a, b, (((2,), (2,)), ((0,), (0,))), preferred_element_type=F32)


def _bmm_tn(a, b):
    return lax.dot_general(a, b, (((1,), (1,)), ((0,), (0,))), preferred_element_type=F32)


def _tril(cb, upper=False):
    shape = (cb, HGRN_CHUNK, HGRN_CHUNK)
    r, c = lax.broadcasted_iota(jnp.int32, shape, 1), lax.broadcasted_iota(jnp.int32, shape, 2)
    return (r <= c) if upper else (r >= c)


def _tri_matmul(x, upper):
    return lax.dot_general(_tril(x.shape[0], upper).astype(F32), x, (((2,), (1,)), ((0,), (0,))),
                           precision=lax.Precision.HIGHEST, preferred_element_type=F32)


@jax.custom_vjp
def _chunk_cumsum(x):
    return _tri_matmul(x, False)


_chunk_cumsum.defvjp(lambda x: (_tri_matmul(x, False), None), lambda _, g: (_tri_matmul(g, True),))


def _hg_elem(fl, qh, lb):
    f = lb + (1.0 - lb) * _sig(fl)
    k = 1.0 - f
    gc = _chunk_cumsum(jnp.log(f))
    last = lax.broadcasted_iota(jnp.int32, gc.shape, 1) == HGRN_CHUNK - 1
    g_last = jnp.sum(jnp.where(last, gc, 0.0), axis=1, keepdims=True)
    q = qh * _sig(qh)
    return q * jnp.exp(gc), k * jnp.exp(-gc), k * jnp.exp(g_last - gc), jnp.exp(g_last)


def _hg_out(q_dec, k_inv, v, st):
    sc = jnp.where(_tril(q_dec.shape[0]), _bmm_nt(_bf(q_dec), _bf(k_inv)), 0.0)
    return _bmm(_bf(sc), _bf(v)) + _bmm_nt(_bf(q_dec), _bf(st)), sc


def _hg_post(o, og, ng):
    on = o * lax.rsqrt(jnp.mean(o * o, axis=-1, keepdims=True) + RMS_EPS) * ng
    return on * (og * _sig(og))


def _hgrn_specs(n_tok, rev):
    nc = n_tok // HGRN_CHUNK
    cb = min(HGRN_CHUNKS_PER_STEP, nc)
    nt = nc // cb
    rows = cb * HGRN_CHUNK
    tt = (lambda t: nt - 1 - t) if rev else (lambda t: t)
    col = lambda base: pl.BlockSpec((rows, LANES), lambda h, t: (tt(t), base + h))
    head_vec = pl.BlockSpec((1, LANES), lambda h, t: (0, h))
    one_vec = pl.BlockSpec((1, LANES), lambda h, t: (0, 0))
    state = pl.BlockSpec((1, cb, HGRN_DK, HGRN_DK), lambda h, t: (h, tt(t), 0, 0))
    return nc, cb, nt, col, head_vec, one_vec, state


def _hgrn_fwd(proj, lb, ng):
    n_tok = proj.shape[0]
    nc, cb, nt, col, head_vec, one_vec, state = _hgrn_specs(n_tok, False)

    def body(fl_ref, qh_ref, ih_ref, og_ref, lb_ref, ng_ref, y_ref, st_ref, s_acc):
        @pl.when(pl.program_id(1) == 0)
        def _():
            s_acc[...] = jnp.zeros_like(s_acc)

        chunks = lambda ref: ref[...].reshape(cb, HGRN_CHUNK, LANES)
        q_dec, k_inv, k_end, decay = _hg_elem(chunks(fl_ref), chunks(qh_ref), lb_ref[...])
        v = chunks(ih_ref)
        upd = _bmm_tn(_bf(v), _bf(k_end))
        st = s_acc[...]
        for ci in range(cb):
            st_ref[0, ci] = st
            st = st * decay[ci] + upd[ci]
        s_acc[...] = st
        o, _ = _hg_out(q_dec, k_inv, v, st_ref[0])
        y_ref[...] = _hg_post(o, chunks(og_ref), ng_ref[...]).reshape(cb * HGRN_CHUNK, LANES)

    return pl.pallas_call(
        body, name="hgrn_fwd", grid=(HGRN_HEADS, nt),
        in_specs=[col(COL_F), col(COL_QH), col(COL_IH), col(COL_OG), head_vec, one_vec],
        out_specs=[col(0), state],
        out_shape=[jax.ShapeDtypeStruct((n_tok, D_MODEL), F32),
                   jax.ShapeDtypeStruct((HGRN_HEADS, nc, HGRN_DK, HGRN_DK), F32)],
        scratch_shapes=[pltpu.VMEM((HGRN_DK, HGRN_DK), F32)],
        compiler_params=_params(("arbitrary", "arbitrary")),
    )(proj, proj, proj, proj, lb, ng)


def _hgrn_bwd(proj, lb, ng, states, dy):
    n_tok = proj.shape[0]
    nc, cb, nt, col, head_vec, one_vec, state = _hgrn_specs(n_tok, True)

    def body(fl_ref, qh_ref, ih_ref, og_ref, lb_ref, ng_ref, st_ref, dy_ref,
             dfl_ref, dqh_ref, dih_ref, dog_ref, dlb_ref, dng_ref, g_acc, g_all):
        h = pl.program_id(0)
        t = pl.program_id(1)

        @pl.when(t == 0)
        def _():
            g_acc[...] = jnp.zeros_like(g_acc)
            dlb_ref[...] = jnp.zeros_like(dlb_ref)

        @pl.when((t == 0) & (h == 0))
        def _():
            dng_ref[...] = jnp.zeros_like(dng_ref)

        chunks = lambda ref: ref[...].reshape(cb, HGRN_CHUNK, LANES)
        flat = lambda val: _bf(val.reshape(cb * HGRN_CHUNK, LANES))
        (q_dec, k_inv, k_end, decay), elem_vjp = jax.vjp(_hg_elem, chunks(fl_ref), chunks(qh_ref), lb_ref[...])
        v = chunks(ih_ref)
        st = st_ref[0]
        o, sc = _hg_out(q_dec, k_inv, v, st)
        _, post_vjp = jax.vjp(_hg_post, o, chunks(og_ref), ng_ref[...])
        do, dog, dng = post_vjp(chunks(dy_ref))
        dob, vb, qb = _bf(do), _bf(v), _bf(q_dec)
        dsc = _bf(jnp.where(_tril(cb), _bmm_nt(dob, vb), 0.0))
        p = _bmm_tn(dob, qb)
        g = g_acc[...]
        for ci in reversed(range(cb)):
            g_all[ci] = g
            g = g * decay[ci] + p[ci]
        g_acc[...] = g
        g = g_all[...]
        gb = _bf(g)
        dq_dec = _bmm(dsc, _bf(k_inv)) + _bmm(dob, _bf(st))
        dk_inv = _bmm_tn(dsc, qb)
        dv = _bmm_tn(_bf(sc), dob) + _bmm_nt(_bf(k_end), gb)
        dk_end = _bmm(vb, gb)
        ddecay = jnp.sum(st * g, axis=1, keepdims=True)
        dfl, dqh, dlb = elem_vjp((dq_dec, dk_inv, dk_end, ddecay))
        dfl_ref[...] = flat(dfl)
        dqh_ref[...] = flat(dqh)
        dih_ref[...] = flat(dv)
        dog_ref[...] = flat(dog)
        dlb_ref[...] += dlb
        dng_ref[...] += dng

    out_col = jax.ShapeDtypeStruct((n_tok, D_MODEL), BF16)
    return pl.pallas_call(
        body, name="hgrn_bwd", grid=(HGRN_HEADS, nt),
        in_specs=[col(COL_F), col(COL_QH), col(COL_IH), col(COL_OG), head_vec, one_vec, state, col(0)],
        out_specs=[col(0), col(0), col(0), col(0), head_vec, one_vec],
        out_shape=[out_col, out_col, out_col, out_col,
                   jax.ShapeDtypeStruct((1, D_MODEL), F32), jax.ShapeDtypeStruct((1, LANES), F32)],
        scratch_shapes=[pltpu.VMEM((HGRN_DK, HGRN_DK), F32), pltpu.VMEM((cb, HGRN_DK, HGRN_DK), F32)],
        compiler_params=_params(("arbitrary", "arbitrary")),
    )(proj, proj, proj, proj, lb, ng, states, dy)


def _lb_fwd(lb_logits):
    def lb_of(l0, l1):
        m = jnp.maximum(l0, l1)
        e0, e1 = jnp.exp(l0 - m), jnp.exp(l1 - m)
        return e0 / (e0 + e1)

    def body(l_ref, o_ref):
        o_ref[...] = lb_of(l_ref[0:1, :], l_ref[1:2, :])

    lb = pl.pallas_call(body, name="lb_fwd", out_shape=jax.ShapeDtypeStruct((1, D_MODEL), F32))(lb_logits)
    return lb, lb_of


def _gate_specs(tm):
    return [pl.BlockSpec((tm, 512), lambda i, c=c: (i, c)) for c in (COL_GA, COL_GA + 1, COL_GH, COL_GH + 1)]


def _mix_fwd(y_attn, y_hgrn, proj, x1, w_pa, w_ph, w_out, g, b):
    n = x1.shape[0]
    tm = min(MIX_TILE, n)

    def body(ya_ref, yh_ref, ga0, ga1, gh0, gh1, x_ref, wpa, wph, wo, g_ref, b_ref, z_ref, o_ref):
        ya = _mm(_bf(ya_ref[...]), wpa[...])
        yh = _mm(_bf(yh_ref[...]), wph[...])
        ga = jnp.concatenate([ga0[...], ga1[...]], axis=1)
        gh = jnp.concatenate([gh0[...], gh1[...]], axis=1)
        merged = _sig(ga) * ya + _sig(gh) * yh
        z = DEEPNORM_ALPHA * x_ref[...] + _mm(_bf(merged), wo[...])
        z_ref[...] = z
        o_ref[...] = _ln(z, g_ref[...], b_ref[...])

    row = pl.BlockSpec((tm, D_MODEL), lambda i: (i, 0))
    sq = _full((D_MODEL, D_MODEL))
    vec = _full((1, D_MODEL))
    return pl.pallas_call(
        body, name="mix_fwd", grid=(n // tm,),
        in_specs=[row, row, *_gate_specs(tm), row, sq, sq, sq, vec, vec],
        out_specs=[row, row], out_shape=[jax.ShapeDtypeStruct((n, D_MODEL), F32)] * 2,
        compiler_params=_params(("arbitrary",)),
    )(y_attn, y_hgrn, proj, proj, proj, proj, x1, w_pa, w_ph, w_out, g, b)


def _mix_bwd(dx2, z2, y_attn, y_hgrn, proj, w_pa, w_ph, w_out, g, b):
    n = z2.shape[0]
    tm = min(MIX_TILE, n)

    def body(do_ref, z_ref, ya_ref, yh_ref, ga0, ga1, gh0, gh1, wpa, wph, wo, g_ref, b_ref,
             dz_ref, mg_ref, dya_ref, dyh_ref, dyat_ref, dyhg_ref, dgt_ref, dg_ref, db_ref):
        _, vjp = jax.vjp(_ln, z_ref[...], g_ref[...], b_ref[...])
        dz, dg, db = vjp(do_ref[...])

        @pl.when(pl.program_id(0) == 0)
        def _():
            dg_ref[...] = jnp.zeros_like(dg_ref)
            db_ref[...] = jnp.zeros_like(db_ref)

        dg_ref[...] += dg
        db_ref[...] += db
        dz_ref[...] = dz
        ya = _mm(_bf(ya_ref[...]), wpa[...])
        yh = _mm(_bf(yh_ref[...]), wph[...])
        ga = jnp.concatenate([ga0[...], ga1[...]], axis=1)
        gh = jnp.concatenate([gh0[...], gh1[...]], axis=1)

        def merge(ga, gh, ya, yh):
            return _sig(ga) * ya + _sig(gh) * yh

        merged, merge_vjp = jax.vjp(merge, ga, gh, ya, yh)
        mg_ref[...] = _bf(merged)
        dmerged = _mm_nt(_bf(dz), wo[...])
        dga, dgh, dya, dyh = merge_vjp(dmerged)
        dya = _bf(dya)
        dyh = _bf(dyh)
        dya_ref[...] = dya
        dyh_ref[...] = dyh
        dgt_ref[:, :D_MODEL] = _bf(dga)
        dgt_ref[:, D_MODEL:] = _bf(dgh)
        dyat_ref[...] = _mm_nt(dya, wpa[...])
        dyhg_ref[...] = _mm_nt(dyh, wph[...])

    row = pl.BlockSpec((tm, D_MODEL), lambda i: (i, 0))
    row2 = pl.BlockSpec((tm, 2 * D_MODEL), lambda i: (i, 0))
    sq = _full((D_MODEL, D_MODEL))
    vec = _full((1, D_MODEL))
    f32_row = jax.ShapeDtypeStruct((n, D_MODEL), F32)
    bf_row = jax.ShapeDtypeStruct((n, D_MODEL), BF16)
    vec_shape = jax.ShapeDtypeStruct((1, D_MODEL), F32)
    return pl.pallas_call(
        body, name="mix_bwd", grid=(n // tm,),
        in_specs=[row, row, row, row, *_gate_specs(tm), sq, sq, sq, vec, vec],
        out_specs=[row, row, row, row, row, row, row2, vec, vec],
        out_shape=[f32_row, bf_row, bf_row, bf_row, f32_row, f32_row,
                   jax.ShapeDtypeStruct((n, 2 * D_MODEL), BF16), vec_shape, vec_shape],
        compiler_params=_params(("arbitrary",)),
    )(dx2, z2, y_attn, y_hgrn, proj, proj, proj, proj, w_pa, w_ph, w_out, g, b)


def _position():
    x, y, c = lax.axis_index("x"), lax.axis_index("y"), lax.axis_index("c")
    chips = [(1 - x, y), (x, 1 - y), (1 - x, 1 - y)]
    return x, y, c, chips


def _any_specs(k):
    return [pl.BlockSpec(memory_space=pl.ANY)] * k


def _gather_weights(shards):
    nw = len(shards)

    def body(*refs):
        ins, outs = refs[:nw], refs[nw:2 * nw]
        local_sem, send_sem, recv_sem = refs[2 * nw:]
        x, y, c, chips = _position()
        me = 2 * x + y
        sibling = (x, y, 1 - c)

        def half(w, chip_idx, which):
            hr = shards[w].shape[0] // 2
            return outs[w].at[chip_idx, pl.ds(which * hr, hr)]

        def own_half(w):
            hr = shards[w].shape[0] // 2
            return ins[w].at[pl.ds(c * hr, hr)]

        def remote(w, k, src, dst, to):
            return pltpu.make_async_remote_copy(src_ref=src, dst_ref=dst, send_sem=send_sem.at[w * 6 + k],
                                                recv_sem=recv_sem.at[w * 6 + k], device_id=to, device_id_type=MESH)

        local = [pltpu.make_async_copy(ins[w], outs[w].at[me], local_sem.at[w]) for w in range(nw)]
        for cp in local:
            cp.start()
        first = [remote(w, j, own_half(w), half(w, me, c), (px, py, c))
                 for w in range(nw) for j, (px, py) in enumerate(chips)]
        for cp in first:
            cp.start()
        passed = []
        for w in range(nw):
            for j, (px, py) in enumerate(chips):
                landed = half(w, 2 * px + py, c)
                remote(w, j, landed, landed, (px, py, c)).wait_recv()
                cp = remote(w, 3 + j, landed, landed, sibling)
                cp.start()
                passed.append(cp)
        for w in range(nw):
            for j, (px, py) in enumerate(chips):
                from_sibling = half(w, 2 * px + py, 1 - c)
                remote(w, 3 + j, from_sibling, from_sibling, sibling).wait_recv()
        for cp in first + passed:
            cp.wait_send()
        for cp in local:
            cp.wait()

    return pl.pallas_call(
        body, name="gather_weights",
        in_specs=_any_specs(nw), out_specs=_any_specs(nw),
        out_shape=[jax.ShapeDtypeStruct((N_CHIPS, *s.shape), s.dtype) for s in shards],
        scratch_shapes=[pltpu.SemaphoreType.DMA((nw,)), pltpu.SemaphoreType.DMA((nw * 6,)), pltpu.SemaphoreType.DMA((nw * 6,))],
    )(*shards)


def _exchange_grads(grads):
    nw = len(grads)

    def body(*refs):
        ins, outs = refs[:nw], refs[nw:2 * nw]
        local_sem, send_sem, recv_sem = refs[2 * nw:]
        x, y, c, chips = _position()
        me = 2 * x + y

        def remote(w, j, to_chip):
            px, py = to_chip
            return pltpu.make_async_remote_copy(src_ref=ins[w].at[2 * px + py], dst_ref=outs[w].at[me],
                                                send_sem=send_sem.at[w * 3 + j], recv_sem=recv_sem.at[w * 3 + j],
                                                device_id=(px, py, c), device_id_type=MESH)

        local = [pltpu.make_async_copy(ins[w].at[me], outs[w].at[me], local_sem.at[w]) for w in range(nw)]
        sends = [remote(w, j, chip) for w in range(nw) for j, chip in enumerate(chips)]
        for cp in local + sends:
            cp.start()
        for w in range(nw):
            for j, (px, py) in enumerate(chips):
                slot = outs[w].at[2 * px + py]
                pltpu.make_async_remote_copy(src_ref=slot, dst_ref=slot, send_sem=send_sem.at[w * 3 + j],
                                             recv_sem=recv_sem.at[w * 3 + j], device_id=(px, py, c),
                                             device_id_type=MESH).wait_recv()
        for cp in sends:
            cp.wait_send()
        for cp in local:
            cp.wait()

    return pl.pallas_call(
        body, name="exchange_grads",
        in_specs=_any_specs(nw), out_specs=_any_specs(nw),
        out_shape=[jax.ShapeDtypeStruct(g.shape, g.dtype) for g in grads],
        scratch_shapes=[pltpu.SemaphoreType.DMA((nw,)), pltpu.SemaphoreType.DMA((nw * 3,)), pltpu.SemaphoreType.DMA((nw * 3,))],
    )(*grads)


def _sum_slots(name, slots):
    _, rows, cols = slots.shape
    tr = _update_rows(rows)

    def body(s_ref, o_ref):
        acc = s_ref[0].astype(F32)
        for i in range(1, N_CHIPS):
            acc = acc + s_ref[i].astype(F32)
        o_ref[...] = acc

    return pl.pallas_call(
        body, name=name, grid=(rows // tr,),
        in_specs=[pl.BlockSpec((N_CHIPS, tr, cols), lambda i: (0, i, 0))],
        out_specs=pl.BlockSpec((tr, cols), lambda i: (i, 0)),
        out_shape=jax.ShapeDtypeStruct((rows, cols), F32),
        compiler_params=_params(("arbitrary",)),
    )(slots)


def _swap_with_sibling(parts):
    nw = len(parts)

    def body(*refs):
        ins, outs = refs[:nw], refs[nw:2 * nw]
        send_sem, recv_sem = refs[2 * nw:]
        x, y, c, _ = _position()
        copies = [pltpu.make_async_remote_copy(src_ref=ins[w], dst_ref=outs[w], send_sem=send_sem.at[w],
                                               recv_sem=recv_sem.at[w], device_id=(x, y, 1 - c), device_id_type=MESH)
                  for w in range(nw)]
        for cp in copies:
            cp.start()
        for cp in copies:
            cp.wait()

    return pl.pallas_call(
        body, name="swap_with_sibling",
        in_specs=_any_specs(nw), out_specs=_any_specs(nw),
        out_shape=[jax.ShapeDtypeStruct(p.shape, p.dtype) for p in parts],
        scratch_shapes=[pltpu.SemaphoreType.DMA((nw,)), pltpu.SemaphoreType.DMA((nw,))],
    )(*parts)


def _sum_small(part):
    def body(p_ref, o_ref, buf, send_sem, recv_sem):
        x, y, c, _ = _position()
        me = 4 * x + 2 * y + c
        buf[me] = p_ref[...]
        copies = []
        for k in range(1, N_DEV):
            peer = tuple(1 - v if (k >> s) & 1 else v for v, s in ((x, 2), (y, 1), (c, 0)))
            copies.append(pltpu.make_async_remote_copy(src_ref=p_ref, dst_ref=buf.at[me], send_sem=send_sem.at[k - 1],
                                                       recv_sem=recv_sem.at[k - 1], device_id=peer, device_id_type=MESH))
        for cp in copies:
            cp.start()
        for cp in copies:
            cp.wait()
        acc = buf[0]
        for d in range(1, N_DEV):
            acc = acc + buf[d]
        o_ref[...] = acc

    vm = pl.BlockSpec(memory_space=pltpu.VMEM)
    return pl.pallas_call(
        body, name="sum_small", in_specs=[vm], out_specs=vm,
        out_shape=jax.ShapeDtypeStruct((1, SM_LEN), F32),
        scratch_shapes=[pltpu.VMEM((N_DEV, 1, SM_LEN), F32), pltpu.SemaphoreType.DMA((N_DEV - 1,)),
                        pltpu.SemaphoreType.DMA((N_DEV - 1,))],
    )(part)


def _adamw(w, g, m, v):
    m = ADAM_B1 * m + (1.0 - ADAM_B1) * g
    v = ADAM_B2 * v + (1.0 - ADAM_B2) * (g * g)
    m_hat = m / (1.0 - ADAM_B1 ** ADAM_STEP)
    v_hat = v / (1.0 - ADAM_B2 ** ADAM_STEP)
    delta = -ADAM_LR * (m_hat / (jnp.sqrt(v_hat) + ADAM_EPS) + ADAM_WD * w)
    return delta, m, v


def _adam_big(name, p_own, p_sibling, w, m, v):
    rows, cols = w.shape
    tr = _update_rows(rows)

    def body(p_ref, q_ref, w_ref, m_ref, v_ref, g_ref, d_ref, nm_ref, nv_ref):
        g = p_ref[...] + q_ref[...]
        g_ref[...] = g
        d_ref[...], nm_ref[...], nv_ref[...] = _adamw(w_ref[...], g, m_ref[...], v_ref[...])

    spec = pl.BlockSpec((tr, cols), lambda i: (i, 0))
    return pl.pallas_call(
        body, name=name, grid=(rows // tr,), in_specs=[spec] * 5, out_specs=[spec] * 4,
        out_shape=[jax.ShapeDtypeStruct((rows, cols), F32)] * 4,
        compiler_params=_params(("arbitrary",)),
    )(p_own, p_sibling, w, m, v)


def _adam_small(total, wvec, mvec, vvec, lb_of):
    def body(t_ref, w_ref, m_ref, v_ref, g_ref, d_ref, nm_ref, nv_ref):
        g_ref[:, :SM_LB] = t_ref[:, :SM_LB]
        _, vjp = jax.vjp(lb_of, w_ref[:, PK_LB0:PK_LB1], w_ref[:, PK_LB1:PK_LEN])
        dl0, dl1 = vjp(t_ref[:, SM_LB:SM_LOSS])
        g_ref[:, PK_LB0:PK_LB1] = dl0
        g_ref[:, PK_LB1:PK_LEN] = dl1
        d_ref[...], nm_ref[...], nv_ref[...] = _adamw(w_ref[...], g_ref[...], m_ref[...], v_ref[...])

    return pl.pallas_call(body, name="adam_small", out_shape=[jax.ShapeDtypeStruct((1, PK_LEN), F32)] * 4)(
        total, wvec, mvec, vvec)


_BIG = ("ffn1_w1", "ffn1_w3", "ffn1_w2", "w_in", "w_proj_attn", "w_proj_hgrn", "w_out", "ffn2_w1", "ffn2_w3", "ffn2_w2")
_SMALL = ("ln1_g", "ln1_b", "ln2_g", "ln2_b", "ln3_g", "ln3_b", "b_in", "attn_sinks", "hgrn_norm_g", "hgrn_lb_logits")
_ORDER = ("ln1_g", "ln1_b", "ffn1_w1", "ffn1_w3", "ffn1_w2", "ln2_g", "ln2_b", "w_in", "b_in", "attn_sinks",
          "hgrn_lb_logits", "hgrn_norm_g", "w_proj_attn", "w_proj_hgrn", "w_out", "ln3_g", "ln3_b",
          "ffn2_w1", "ffn2_w3", "ffn2_w2")


def _pad_to(a, rows, cols):
    return jnp.pad(a, ((0, rows - a.shape[0]), (0, cols - a.shape[1])))


def _send_form(name, w):
    wb = _bf(w)
    if name.endswith(("w1", "w3")) and name.startswith("ffn"):
        return _pad_to(wb, D_MODEL, FF_PAD)
    if name.endswith("w2") and name.startswith("ffn"):
        return _pad_to(wb, FF_PAD, D_MODEL)
    return wb


def _pack_small(p):
    pieces = [p[k].reshape(1, -1) for k in ("ln1_g", "ln1_b", "ln2_g", "ln2_b", "ln3_g", "ln3_b", "b_in")]
    pieces.append(_pad_to(p["attn_sinks"].reshape(1, -1), 1, LANES))
    pieces.append(p["hgrn_norm_g"].reshape(1, -1))
    pieces.append(p["hgrn_lb_logits"].reshape(1, -1))
    return jnp.concatenate(pieces, axis=1)


def _unpack_small(vec, like):
    out = {}
    for i, k in enumerate(("ln1_g", "ln1_b", "ln2_g", "ln2_b", "ln3_g", "ln3_b")):
        out[k] = vec[:, i * D_MODEL:(i + 1) * D_MODEL]
    out["b_in"] = vec[:, SM_BIN:SM_SINK]
    out["attn_sinks"] = vec[:, SM_SINK:SM_SINK + N_Q_HEADS]
    out["hgrn_norm_g"] = vec[:, SM_NG:SM_LB]
    out["hgrn_lb_logits"] = vec[:, PK_LB0:PK_LEN].reshape(2, D_MODEL)
    return {k: v.reshape(like[k].shape) for k, v in out.items()}


def kernel(x, ln1_g, ln1_b, ffn1_w1, ffn1_w3, ffn1_w2, ln2_g, ln2_b, w_in, b_in, attn_sinks, hgrn_lb_logits, hgrn_norm_g, w_proj_attn, w_proj_hgrn, w_out, ln3_g, ln3_b, ffn2_w1, ffn2_w3, ffn2_w2, loss_target, m_ln1_g, m_ln1_b, m_ffn1_w1, m_ffn1_w3, m_ffn1_w2, m_ln2_g, m_ln2_b, m_w_in, m_b_in, m_attn_sinks, m_hgrn_lb_logits, m_hgrn_norm_g, m_w_proj_attn, m_w_proj_hgrn, m_w_out, m_ln3_g, m_ln3_b, m_ffn2_w1, m_ffn2_w3, m_ffn2_w2, v_ln1_g, v_ln1_b, v_ffn1_w1, v_ffn1_w3, v_ffn1_w2, v_ln2_g, v_ln2_b, v_w_in, v_b_in, v_attn_sinks, v_hgrn_lb_logits, v_hgrn_norm_g, v_w_proj_attn, v_w_proj_hgrn, v_w_out, v_ln3_g, v_ln3_b, v_ffn2_w1, v_ffn2_w3, v_ffn2_w2):
    w = dict(ln1_g=ln1_g, ln1_b=ln1_b, ffn1_w1=ffn1_w1, ffn1_w3=ffn1_w3, ffn1_w2=ffn1_w2, ln2_g=ln2_g, ln2_b=ln2_b,
             w_in=w_in, b_in=b_in, attn_sinks=attn_sinks, hgrn_lb_logits=hgrn_lb_logits, hgrn_norm_g=hgrn_norm_g,
             w_proj_attn=w_proj_attn, w_proj_hgrn=w_proj_hgrn, w_out=w_out, ln3_g=ln3_g, ln3_b=ln3_b,
             ffn2_w1=ffn2_w1, ffn2_w3=ffn2_w3, ffn2_w2=ffn2_w2)
    mom = dict(ln1_g=m_ln1_g, ln1_b=m_ln1_b, ffn1_w1=m_ffn1_w1, ffn1_w3=m_ffn1_w3, ffn1_w2=m_ffn1_w2, ln2_g=m_ln2_g,
               ln2_b=m_ln2_b, w_in=m_w_in, b_in=m_b_in, attn_sinks=m_attn_sinks, hgrn_lb_logits=m_hgrn_lb_logits,
               hgrn_norm_g=m_hgrn_norm_g, w_proj_attn=m_w_proj_attn, w_proj_hgrn=m_w_proj_hgrn, w_out=m_w_out,
               ln3_g=m_ln3_g, ln3_b=m_ln3_b, ffn2_w1=m_ffn2_w1, ffn2_w3=m_ffn2_w3, ffn2_w2=m_ffn2_w2)
    var = dict(ln1_g=v_ln1_g, ln1_b=v_ln1_b, ffn1_w1=v_ffn1_w1, ffn1_w3=v_ffn1_w3, ffn1_w2=v_ffn1_w2, ln2_g=v_ln2_g,
               ln2_b=v_ln2_b, w_in=v_w_in, b_in=v_b_in, attn_sinks=v_attn_sinks, hgrn_lb_logits=v_hgrn_lb_logits,
               hgrn_norm_g=v_hgrn_norm_g, w_proj_attn=v_w_proj_attn, w_proj_hgrn=v_w_proj_hgrn, w_out=v_w_out,
               ln3_g=v_ln3_g, ln3_b=v_ln3_b, ffn2_w1=v_ffn2_w1, ffn2_w3=v_ffn2_w3, ffn2_w2=v_ffn2_w2)

    n_tok = x.shape[1]
    x0 = x.reshape(n_tok, D_MODEL)
    target = loss_target.reshape(n_tok, D_MODEL)

    gathered = dict(zip(_BIG, _gather_weights([_send_form(k, w[k][0]) for k in _BIG])))
    sq = lambda k: gathered[k].reshape(D_MODEL, D_MODEL)
    w_pa, w_ph, w_o = sq("w_proj_attn"), sq("w_proj_hgrn"), sq("w_out")
    f1 = (gathered["ffn1_w1"], gathered["ffn1_w3"], gathered["ffn1_w2"])
    f2 = (gathered["ffn2_w1"], gathered["ffn2_w3"], gathered["ffn2_w2"])

    tabs = _rope_tables(n_tok)
    lb, lb_of = _lb_fwd(hgrn_lb_logits)
    z1, x1 = _ffn_fwd("ffn1_fwd", x0, *f1, ln1_g, ln1_b)
    proj = _in_proj(x1, gathered["w_in"], b_in)
    y_attn = _attn_fwd(proj, tabs, attn_sinks)
    y_hgrn, states = _hgrn_fwd(proj, lb, hgrn_norm_g)
    z2, x2 = _mix_fwd(y_attn, y_hgrn, proj, x1, w_pa, w_ph, w_o, ln2_g, ln2_b)
    z3, dy, loss_part = _ffn_fwd("ffn2_fwd", x2, *f2, ln3_g, ln3_b, target=target)

    big = {}
    small = {}
    dx2, a2, dh1_2, dh3_2, df2, small["ln3_g"], small["ln3_b"] = _ffn_bwd("ffn2_bwd", x2, z3, dy, *f2, ln3_g, ln3_b)
    big["ffn2_w1"] = _grad_matmul("ffn2_dw1", x2, "shared", D_MODEL, dh1_2, "batch", FF_PAD)
    big["ffn2_w3"] = _grad_matmul("ffn2_dw3", x2, "shared", D_MODEL, dh3_2, "batch", FF_PAD)
    big["ffn2_w2"] = _grad_matmul("ffn2_dw2", a2, "batch", FF_PAD, df2, "shared", D_MODEL)
    (dz2, merged, dya, dyh, dy_attn, dy_hgrn, dgates, small["ln2_g"], small["ln2_b"]) = _mix_bwd(
        dx2, z2, y_attn, y_hgrn, proj, w_pa, w_ph, w_o, ln2_g, ln2_b)
    big["w_out"] = _grad_matmul("dw_out", merged, "cols", PROJ_SHARD, dz2, "shared", D_MODEL)
    big["w_proj_attn"] = _grad_matmul("dw_proj_attn", y_attn, "cols", PROJ_SHARD, dya, "shared", D_MODEL)
    big["w_proj_hgrn"] = _grad_matmul("dw_proj_hgrn", y_hgrn, "cols", PROJ_SHARD, dyh, "shared", D_MODEL)
    dfl, dqh, dih, dog, dlb, small["hgrn_norm_g"] = _hgrn_bwd(proj, lb, hgrn_norm_g, states, dy_hgrn)
    dq, dkv, dsinks = _attn_bwd(proj, dy_attn, tabs, attn_sinks)
    dproj = jnp.concatenate([dq, dkv, dfl, dqh, dih, dog, dgates], axis=1)
    big["w_in"], small["b_in"] = _grad_matmul("dw_in", x1, "shared", D_MODEL, dproj, "cols", IN_SHARD, colsum=True)
    dx1 = _in_proj_dx(dproj, gathered["w_in"], dz2)
    grad_x, a1, dh1_1, dh3_1, df1, small["ln1_g"], small["ln1_b"] = _ffn_bwd("ffn1_bwd", x0, z1, dx1, *f1, ln1_g, ln1_b)
    big["ffn1_w1"] = _grad_matmul("ffn1_dw1", x0, "shared", D_MODEL, dh1_1, "batch", FF_PAD)
    big["ffn1_w3"] = _grad_matmul("ffn1_dw3", x0, "shared", D_MODEL, dh3_1, "batch", FF_PAD)
    big["ffn1_w2"] = _grad_matmul("ffn1_dw2", a1, "batch", FF_PAD, df1, "shared", D_MODEL)

    slots = _exchange_grads([big[k] for k in _BIG])
    partial = []
    for k, s in zip(_BIG, slots):
        p = _sum_slots("sum_" + k, s)
        partial.append(p[:w[k].shape[1], :w[k].shape[2]])
    from_sibling = _swap_with_sibling(partial)

    outs = {"grad": {}, "delta": {}, "m": {}, "v": {}}
    for k, p, q in zip(_BIG, partial, from_sibling):
        res = _adam_big("adam_" + k, p, q, w[k][0], mom[k][0], var[k][0])
        for kind, r in zip(("grad", "delta", "m", "v"), res):
            outs[kind][k] = r.reshape(w[k].shape)

    part = jnp.concatenate(
        [small[k] for k in ("ln1_g", "ln1_b", "ln2_g", "ln2_b", "ln3_g", "ln3_b", "b_in")]
        + [dsinks, small["hgrn_norm_g"], dlb, loss_part], axis=1)
    total = _sum_small(part)
    res = _adam_small(total, _pack_small(w), _pack_small(mom), _pack_small(var), lb_of)
    for kind, r in zip(("grad", "delta", "m", "v"), res):
        outs[kind].update(_unpack_small(r, w))
    loss = total[0, SM_LOSS]

    return (loss, grad_x.reshape(x.shape), *[outs["grad"][k] for k in _ORDER], *[outs["delta"][k] for k in _ORDER],
            *[outs["m"][k] for k in _ORDER], *[outs["v"][k] for k in _ORDER])
```

```python
import functools

import jax
import jax.numpy as jnp
from jax import lax
from jax.experimental import pallas as pl
from jax.experimental.pallas import tpu as pltpu

F32 = jnp.float32
BF16 = jnp.bfloat16

D_MODEL = 1024
N_Q_HEADS = 16
N_KV_HEADS = 4
HEAD_DIM = 64
ATTN_BLOCK = 128
ROPE_THETA = 500000.0
ROPE_DIM = HEAD_DIM // 4
HGRN_HEADS = 8
HGRN_DK = 128
HGRN_CHUNK = 64
D_FF = 2816
D_IN = 7680
DEEPNORM_ALPHA = 2 ** 0.25
LN_EPS = 1e-5
RMS_EPS = 1e-6
NEG_INF = -1e30

ADAM_LR = 0.001
ADAM_B1 = 0.9
ADAM_B2 = 0.999
ADAM_EPS = 1e-08
ADAM_WD = 0.01
ADAM_STEP = 10

N_CHIPS = 4
N_DEV = 8
FF_SHARD = D_FF // N_CHIPS
LANES = 128
MXU_COLS = 256
FF_PAD = -(-FF_SHARD // MXU_COLS) * MXU_COLS
IN_SHARD = D_IN // N_CHIPS
PROJ_SHARD = D_MODEL // N_CHIPS
ROW_TILE = 512
MIX_TILE = 256
UPDATE_ROWS = 128
HGRN_CHUNKS_PER_STEP = 8
VMEM_LIMIT = 56 * 1024 * 1024

COL_K = 1024 // 256
COL_V = 1280 // 256
COL_F = 1536 // LANES
COL_QH = 2560 // LANES
COL_IH = 3584 // LANES
COL_OG = 4608 // LANES
COL_GA = 5632 // 512
COL_GH = 6656 // 512

SM_LN = 0
SM_BIN = 6 * D_MODEL
SM_SINK = SM_BIN + D_IN
SM_NG = SM_SINK + LANES
SM_LB = SM_NG + LANES
SM_LOSS = SM_LB + D_MODEL
SM_LEN = SM_LOSS + LANES
PK_LB0 = SM_LB
PK_LB1 = SM_LB + D_MODEL
PK_LEN = PK_LB1 + D_MODEL

MESH = pl.DeviceIdType.MESH


def _mm(a, b):
    return lax.dot_general(a, b, (((1,), (0,)), ((), ())), preferred_element_type=F32)


def _mm_nt(a, b):
    return lax.dot_general(a, b, (((1,), (1,)), ((), ())), preferred_element_type=F32)


def _mm_tn(a, b):
    return lax.dot_general(a, b, (((0,), (0,)), ((), ())), preferred_element_type=F32)


def _bf(v):
    return v.astype(BF16)


def _sig(v):
    return jax.nn.sigmoid(v)


def _ln(z, g, b):
    mu = jnp.mean(z, axis=-1, keepdims=True)
    zc = z - mu
    var = jnp.mean(zc * zc, axis=-1, keepdims=True)
    return zc * lax.rsqrt(var + LN_EPS) * g + b


def _swiglu_act(h1, h3):
    return (h1 * _sig(h1)) * h3


def _params(sem=None):
    return pltpu.CompilerParams(dimension_semantics=sem, vmem_limit_bytes=VMEM_LIMIT)


def _full(shape):
    nd = len(shape)
    return pl.BlockSpec(shape, lambda *_: (0,) * nd)


def _update_rows(rows):
    return max(t for t in range(8, UPDATE_ROWS + 1, 8) if rows % t == 0)


def _ffn_fwd(name, x, w1g, w3g, w2g, g, b, target=None, comm=None):
    n = x.shape[0]
    tm = min(ROW_TILE, n)
    final = target is not None

    def body(*refs):
        if final:
            x_ref, w1_ref, w3_ref, w2_ref, g_ref, b_ref, t_ref, z_ref, o_ref, loss_ref, acc, xb = refs
        else:
            x_ref, w1_ref, w3_ref, w2_ref, g_ref, b_ref, z_ref, o_ref, acc, xb = refs
        i = pl.program_id(0)
        j = pl.program_id(1)

        @pl.when(j == 0)
        def _():
            xb[...] = _bf(x_ref[...])
            acc[...] = jnp.zeros_like(acc)

        h1 = _mm(xb[...], w1_ref[0])
        h3 = _mm(xb[...], w3_ref[0])
        acc[...] += _mm(_bf(_swiglu_act(h1, h3)), w2_ref[0])

        @pl.when(j == N_CHIPS - 1)
        def _():
            z = DEEPNORM_ALPHA * x_ref[...] + 0.5 * acc[...]
            z_ref[...] = z
            y = _ln(z, g_ref[...], b_ref[...])
            if final:
                e = y - t_ref[...]

                @pl.when(i == 0)
                def _():
                    loss_ref[...] = jnp.zeros_like(loss_ref)

                loss_ref[...] += jnp.sum(e * e) * (0.5 / D_MODEL)
                o_ref[...] = e * (1.0 / D_MODEL)
            else:
                o_ref[...] = y

    row = pl.BlockSpec((tm, D_MODEL), lambda i, j: (i, 0))
    wcol = pl.BlockSpec((1, D_MODEL, FF_PAD), lambda i, j: (j, 0, 0))
    wrow = pl.BlockSpec((1, FF_PAD, D_MODEL), lambda i, j: (j, 0, 0))
    vec = pl.BlockSpec((1, D_MODEL), lambda i, j: (0, 0))
    in_specs = [row, wcol, wcol, wrow, vec, vec]
    args = [x, w1g, w3g, w2g, g, b]
    out_specs = [row, row]
    out_shape = [jax.ShapeDtypeStruct((n, D_MODEL), F32)] * 2
    if final:
        in_specs.append(row)
        args.append(target)
        out_specs.append(pl.BlockSpec((1, LANES), lambda i, j: (0, 0)))
        out_shape.append(jax.ShapeDtypeStruct((1, LANES), F32))
    return _hosted(
        body, comm, name=name, grid=(n // tm, N_CHIPS), in_specs=in_specs, out_specs=out_specs, out_shape=out_shape,
        scratch_shapes=[pltpu.VMEM((tm, D_MODEL), F32), pltpu.VMEM((tm, D_MODEL), BF16)],
        compiler_params=_params(("arbitrary", "arbitrary")), args=args)


def _ffn_bwd(name, x, z, dout, w1g, w3g, w2g, g, b, comm=None):
    n = x.shape[0]
    tm = min(ROW_TILE, n)

    def body(x_ref, z_ref, do_ref, w1_ref, w3_ref, w2_ref, g_ref, b_ref,
             dx_ref, a_ref, dh1_ref, dh3_ref, df_ref, dg_ref, db_ref, acc, xb, dzs):
        i = pl.program_id(0)
        j = pl.program_id(1)

        @pl.when(j == 0)
        def _():
            xb[...] = _bf(x_ref[...])
            acc[...] = jnp.zeros_like(acc)
            _, vjp = jax.vjp(_ln, z_ref[...], g_ref[...], b_ref[...])
            dz, dg, db = vjp(do_ref[...])
            dzs[...] = dz
            df_ref[...] = _bf(0.5 * dz)

            @pl.when(i == 0)
            def _():
                dg_ref[...] = jnp.zeros_like(dg_ref)
                db_ref[...] = jnp.zeros_like(db_ref)

            dg_ref[...] += dg
            db_ref[...] += db

        h1 = _mm(xb[...], w1_ref[0])
        h3 = _mm(xb[...], w3_ref[0])
        a, act_vjp = jax.vjp(_swiglu_act, h1, h3)
        da = _mm_nt(df_ref[...], w2_ref[0])
        dh1, dh3 = act_vjp(da)
        dh1 = _bf(dh1)
        dh3 = _bf(dh3)
        a_ref[0] = _bf(a)
        dh1_ref[0] = dh1
        dh3_ref[0] = dh3
        acc[...] += _mm_nt(dh1, w1_ref[0]) + _mm_nt(dh3, w3_ref[0])

        @pl.when(j == N_CHIPS - 1)
        def _():
            dx_ref[...] = DEEPNORM_ALPHA * dzs[...] + acc[...]

    row = pl.BlockSpec((tm, D_MODEL), lambda i, j: (i, 0))
    wcol = pl.BlockSpec((1, D_MODEL, FF_PAD), lambda i, j: (j, 0, 0))
    wrow = pl.BlockSpec((1, FF_PAD, D_MODEL), lambda i, j: (j, 0, 0))
    vec = pl.BlockSpec((1, D_MODEL), lambda i, j: (0, 0))
    hid = pl.BlockSpec((1, tm, FF_PAD), lambda i, j: (j, i, 0))
    hid_shape = jax.ShapeDtypeStruct((N_CHIPS, n, FF_PAD), BF16)
    return _hosted(
        body, comm, name=name, grid=(n // tm, N_CHIPS),
        in_specs=[row, row, row, wcol, wcol, wrow, vec, vec],
        out_specs=[row, hid, hid, hid, row, vec, vec],
        out_shape=[jax.ShapeDtypeStruct((n, D_MODEL), F32), hid_shape, hid_shape, hid_shape,
                   jax.ShapeDtypeStruct((n, D_MODEL), BF16),
                   jax.ShapeDtypeStruct((1, D_MODEL), F32), jax.ShapeDtypeStruct((1, D_MODEL), F32)],
        scratch_shapes=[pltpu.VMEM((tm, D_MODEL), F32), pltpu.VMEM((tm, D_MODEL), BF16), pltpu.VMEM((tm, D_MODEL), F32)],
        compiler_params=_params(("arbitrary", "arbitrary")), args=[x, z, dout, w1g, w3g, w2g, g, b])


def _operand_spec(arr, mode, tn, width):
    if mode == "shared":
        return pl.BlockSpec((tn, width), lambda s, k: (k, 0))
    if mode == "batch":
        return pl.BlockSpec((1, tn, width), lambda s, k: (s, k, 0))
    assert mode == "cols" and arr.shape[1] == N_CHIPS * width
    return pl.BlockSpec((tn, width), lambda s, k: (k, s))


def _grad_matmul(name, a, a_mode, ka, b, b_mode, kb, colsum=False, comm=None):
    n = a.shape[-2]
    tn = min(ROW_TILE, n)
    nk = n // tn

    def body(*refs):
        if colsum:
            a_ref, b_ref, o_ref, cs_ref, acc = refs
        else:
            a_ref, b_ref, o_ref, acc = refs
        k = pl.program_id(1)
        av = a_ref[0] if a_mode == "batch" else a_ref[...]
        bv = b_ref[0] if b_mode == "batch" else b_ref[...]

        @pl.when(k == 0)
        def _():
            acc[...] = jnp.zeros_like(acc)
            if colsum:
                cs_ref[...] = jnp.zeros_like(cs_ref)

        acc[...] += _mm_tn(_bf(av), _bf(bv))
        if colsum:
            cs_ref[...] += jnp.sum(bv.astype(F32), axis=0, keepdims=True)

        @pl.when(k == nk - 1)
        def _():
            o_ref[0] = _bf(acc[...])

    out_specs = [pl.BlockSpec((1, ka, kb), lambda s, k: (s, 0, 0))]
    out_shape = [jax.ShapeDtypeStruct((N_CHIPS, ka, kb), BF16)]
    if colsum:
        out_specs.append(pl.BlockSpec((1, kb), lambda s, k: (0, s)))
        out_shape.append(jax.ShapeDtypeStruct((1, N_CHIPS * kb), F32))
    return _hosted(
        body, comm, name=name, grid=(N_CHIPS, nk),
        in_specs=[_operand_spec(a, a_mode, tn, ka), _operand_spec(b, b_mode, tn, kb)],
        out_specs=out_specs, out_shape=out_shape,
        scratch_shapes=[pltpu.VMEM((ka, kb), F32)],
        compiler_params=_params(("arbitrary", "arbitrary")), args=[a, b])


def _in_proj(x1, w_in_g, b_in, comm=None):
    n = x1.shape[0]
    tm = min(ROW_TILE, n)

    def body(x_ref, w_ref, b_ref, o_ref, xb):
        @pl.when(pl.program_id(1) == 0)
        def _():
            xb[...] = _bf(x_ref[...])

        o_ref[...] = _mm(xb[...], w_ref[0]) + b_ref[...]

    return _hosted(
        body, comm, name="in_proj", grid=(n // tm, N_CHIPS),
        in_specs=[pl.BlockSpec((tm, D_MODEL), lambda i, j: (i, 0)),
                  pl.BlockSpec((1, D_MODEL, IN_SHARD), lambda i, j: (j, 0, 0)),
                  pl.BlockSpec((1, IN_SHARD), lambda i, j: (0, j))],
        out_specs=[pl.BlockSpec((tm, IN_SHARD), lambda i, j: (i, j))],
        out_shape=[jax.ShapeDtypeStruct((n, D_IN), F32)],
        scratch_shapes=[pltpu.VMEM((tm, D_MODEL), BF16)],
        compiler_params=_params(("arbitrary", "arbitrary")), args=[x1, w_in_g, b_in])


def _in_proj_dx(dproj, w_in_g, dz2):
    n = dproj.shape[0]
    tm = min(ROW_TILE, n)

    def body(dp_ref, w_ref, dz_ref, o_ref, acc):
        j = pl.program_id(1)

        @pl.when(j == 0)
        def _():
            acc[...] = jnp.zeros_like(acc)

        acc[...] += _mm_nt(dp_ref[...], w_ref[0])

        @pl.when(j == N_CHIPS - 1)
        def _():
            o_ref[...] = DEEPNORM_ALPHA * dz_ref[...] + acc[...]

    return pl.pallas_call(
        body, name="in_proj_dx", grid=(n // tm, N_CHIPS),
        in_specs=[pl.BlockSpec((tm, IN_SHARD), lambda i, j: (i, j)),
                  pl.BlockSpec((1, D_MODEL, IN_SHARD), lambda i, j: (j, 0, 0)),
                  pl.BlockSpec((tm, D_MODEL), lambda i, j: (i, 0))],
        out_specs=pl.BlockSpec((tm, D_MODEL), lambda i, j: (i, 0)),
        out_shape=jax.ShapeDtypeStruct((n, D_MODEL), F32),
        scratch_shapes=[pltpu.VMEM((tm, D_MODEL), F32)],
        compiler_params=_params(("arbitrary", "arbitrary")),
    )(dproj, w_in_g, dz2)


def _rope_tables(seq_len):
    pos = jnp.arange(seq_len, dtype=F32)
    inv_freq = ROPE_THETA ** (-jnp.arange(0, ROPE_DIM, 2, dtype=F32) / ROPE_DIM)
    ang = pos[:, None] * inv_freq[None, :]
    cos, sin = jnp.cos(ang), jnp.sin(ang)
    half = ROPE_DIM // 2
    rest = HEAD_DIM - ROPE_DIM
    ones = jnp.ones((seq_len, rest), F32)
    zeros = jnp.zeros((seq_len, rest), F32)
    zh = jnp.zeros((seq_len, half), F32)
    c = jnp.concatenate([cos, cos, ones], axis=1)
    sa = jnp.concatenate([-sin, zh, zeros], axis=1)
    sb = jnp.concatenate([zh, sin, zeros], axis=1)
    reps = LANES // HEAD_DIM
    return tuple(jnp.tile(t, (1, reps)) for t in (c, sa, sb))


def _rope(t, c, sa, sb):
    w = t.shape[1]
    reps = w // LANES
    half = ROPE_DIM // 2
    return (t * jnp.tile(c, (1, reps)) + pltpu.roll(t, w - half, 1) * jnp.tile(sa, (1, reps))
            + pltpu.roll(t, half, 1) * jnp.tile(sb, (1, reps)))


def _rope_transposed(g, c, sa, sb):
    w = g.shape[1]
    reps = w // LANES
    half = ROPE_DIM // 2
    return (g * jnp.tile(c, (1, reps)) + pltpu.roll(g * jnp.tile(sa, (1, reps)), half, 1)
            + pltpu.roll(g * jnp.tile(sb, (1, reps)), w - half, 1))


def _attn_mask(n):
    qi = lax.broadcasted_iota(jnp.int32, (ATTN_BLOCK, 2 * ATTN_BLOCK), 0)
    kj = lax.broadcasted_iota(jnp.int32, (ATTN_BLOCK, 2 * ATTN_BLOCK), 1)
    dist = qi + ATTN_BLOCK - kj
    return (dist >= 0) & (dist < ATTN_BLOCK) & (n * ATTN_BLOCK + kj - ATTN_BLOCK >= 0)


def _head_halves(t_pair):
    swapped = pltpu.roll(t_pair, HEAD_DIM, 1)
    return ((_bf(t_pair), _bf(swapped)), (_bf(swapped), _bf(t_pair)))


def _attn_probs(q_masked, k_sel, mask, sink):
    s = _mm_nt(q_masked, k_sel) * (HEAD_DIM ** -0.5)
    s = jnp.where(mask, s, NEG_INF)
    m = jnp.maximum(jnp.max(s, axis=-1, keepdims=True), sink)
    p = jnp.exp(s - m)
    e_sink = jnp.exp(sink - m)
    denom = jnp.sum(p, axis=-1, keepdims=True) + e_sink
    return p / denom, e_sink / denom


def _attn_fwd(proj, tabs, sinks):
    n_tok = proj.shape[0]
    nb = n_tok // ATTN_BLOCK
    group = N_Q_HEADS // N_KV_HEADS

    def body(q_ref, k_ref, v_ref, c_ref, sa_ref, sb_ref, sink_ref, y_ref, kprev, vprev):
        n = pl.program_id(0)

        @pl.when(n == 0)
        def _():
            kprev[...] = jnp.zeros_like(kprev)
            vprev[...] = jnp.zeros_like(vprev)

        c, sa, sb = c_ref[...], sa_ref[...], sb_ref[...]
        qr = _bf(_rope(q_ref[...], c, sa, sb))
        kr = _rope(k_ref[...], c, sa, sb)
        vc = v_ref[...]
        kk = jnp.concatenate([kprev[...], kr], axis=0)
        vv = jnp.concatenate([vprev[...], vc], axis=0)
        kprev[...] = kr
        vprev[...] = vc
        mask = _attn_mask(n)
        lo = lax.broadcasted_iota(jnp.int32, (ATTN_BLOCK, LANES), 1) < HEAD_DIM
        for r in range(N_KV_HEADS // 2):
            k_var = _head_halves(kk[:, r * LANES:(r + 1) * LANES])
            v_var = _head_halves(vv[:, r * LANES:(r + 1) * LANES])
            for kh2 in range(2):
                kh = 2 * r + kh2
                for gp in range(group // 2):
                    pair = (kh * group) // 2 + gp
                    q2 = qr[:, pair * LANES:(pair + 1) * LANES]
                    outs = []
                    for half in range(2):
                        h = 2 * pair + half
                        qm = jnp.where(lo if half == 0 else ~lo, q2, jnp.zeros_like(q2))
                        probs, _ = _attn_probs(qm, k_var[kh2][half], mask, sink_ref[0, h])
                        outs.append(_mm(_bf(probs), v_var[kh2][half]))
                    y_ref[:, pair * LANES:(pair + 1) * LANES] = jnp.where(lo, outs[0], outs[1])

    blk = lambda width, col: pl.BlockSpec((ATTN_BLOCK, width), lambda n: (n, col))
    tab = pl.BlockSpec((ATTN_BLOCK, LANES), lambda n: (n, 0))
    kvw = N_KV_HEADS * HEAD_DIM
    return pl.pallas_call(
        body, name="attn_fwd", grid=(nb,),
        in_specs=[blk(D_MODEL, 0), blk(kvw, COL_K), blk(kvw, COL_V), tab, tab, tab,
                  pl.BlockSpec(memory_space=pltpu.SMEM)],
        out_specs=pl.BlockSpec((ATTN_BLOCK, D_MODEL), lambda n: (n, 0)),
        out_shape=jax.ShapeDtypeStruct((n_tok, D_MODEL), F32),
        scratch_shapes=[pltpu.VMEM((ATTN_BLOCK, kvw), F32), pltpu.VMEM((ATTN_BLOCK, kvw), F32)],
        compiler_params=_params(("arbitrary",)),
    )(proj, proj, proj, *tabs, sinks)


def _attn_bwd(proj, dy, tabs, sinks, comm=None):
    n_tok = proj.shape[0]
    nb = n_tok // ATTN_BLOCK
    group = N_Q_HEADS // N_KV_HEADS
    kvw = N_KV_HEADS * HEAD_DIM

    def body(q_ref, k_ref, v_ref, do_ref, c_ref, sa_ref, sb_ref, cp_ref, sap_ref, sbp_ref, sink_ref,
             dq_ref, dkv_ref, dsink_ref, kprev, vprev, dkc, dvc):
        n = pl.program_id(0)

        @pl.when(n == 0)
        def _():
            for ref in (kprev, vprev, dkc, dvc, dsink_ref):
                ref[...] = jnp.zeros_like(ref)

        prev_tabs = (cp_ref[...], sap_ref[...], sbp_ref[...])

        @pl.when(n < nb)
        def _():
            c, sa, sb = c_ref[...], sa_ref[...], sb_ref[...]
            qr = _bf(_rope(q_ref[...], c, sa, sb))
            kr = _rope(k_ref[...], c, sa, sb)
            vc = v_ref[...]
            kk = jnp.concatenate([kprev[...], kr], axis=0)
            vv = jnp.concatenate([vprev[...], vc], axis=0)
            kprev[...] = kr
            vprev[...] = vc
            mask = _attn_mask(n)
            lane = lax.broadcasted_iota(jnp.int32, (ATTN_BLOCK, LANES), 1)
            lo = lane < HEAD_DIM
            lo2 = lax.broadcasted_iota(jnp.int32, (2 * ATTN_BLOCK, LANES), 1) < HEAD_DIM
            dsink = jnp.zeros((1, LANES), F32)
            dq_pairs = []
            dk_pairs = []
            dv_pairs = []
            for r in range(N_KV_HEADS // 2):
                k_var = _head_halves(kk[:, r * LANES:(r + 1) * LANES])
                v_var = _head_halves(vv[:, r * LANES:(r + 1) * LANES])
                dk_full = []
                dv_full = []
                for kh2 in range(2):
                    kh = 2 * r + kh2
                    dk_acc = jnp.zeros((2 * ATTN_BLOCK, LANES), F32)
                    dv_acc = jnp.zeros((2 * ATTN_BLOCK, LANES), F32)
                    for gp in range(group // 2):
                        pair = (kh * group) // 2 + gp
                        q2 = qr[:, pair * LANES:(pair + 1) * LANES]
                        do2 = do_ref[:, pair * LANES:(pair + 1) * LANES]
                        dqs = []
                        for half in range(2):
                            h = 2 * pair + half
                            sel = lo if half == 0 else ~lo
                            qm = jnp.where(sel, q2, jnp.zeros_like(q2))
                            dom = _bf(jnp.where(sel, do2, 0.0))
                            k_sel = k_var[kh2][half]
                            v_sel = v_var[kh2][half]
                            probs, p_sink = _attn_probs(qm, k_sel, mask, sink_ref[0, h])
                            dp = _mm_nt(dom, v_sel)
                            delta = jnp.sum(probs * dp, axis=-1, keepdims=True)
                            ds = _bf(probs * (dp - delta) * (HEAD_DIM ** -0.5))
                            dsink = dsink + jnp.where(lane[:1] == h, -jnp.sum(p_sink * delta), 0.0)
                            dqs.append(_mm(ds, k_sel))
                            dk_acc = dk_acc + _mm_tn(ds, qm)
                            dv_acc = dv_acc + _mm_tn(_bf(probs), dom)
                        dq_pairs.append(jnp.where(lo, dqs[0], dqs[1]))
                    dk_full.append(dk_acc + pltpu.roll(dk_acc, HEAD_DIM, 1))
                    dv_full.append(dv_acc + pltpu.roll(dv_acc, HEAD_DIM, 1))
                dk_pairs.append(jnp.where(lo2, dk_full[0], dk_full[1]))
                dv_pairs.append(jnp.where(lo2, dv_full[0], dv_full[1]))
            dsink_ref[...] += dsink
            dq_ref[...] = _bf(_rope_transposed(jnp.concatenate(dq_pairs, axis=1), c, sa, sb))
            dk_all = jnp.concatenate(dk_pairs, axis=1)
            dv_all = jnp.concatenate(dv_pairs, axis=1)
            dkv_ref[:, :kvw] = _bf(_rope_transposed(dkc[...] + dk_all[:ATTN_BLOCK], *prev_tabs))
            dkv_ref[:, kvw:] = _bf(dvc[...] + dv_all[:ATTN_BLOCK])
            dkc[...] = dk_all[ATTN_BLOCK:]
            dvc[...] = dv_all[ATTN_BLOCK:]

        @pl.when(n == nb)
        def _():
            dkv_ref[:, :kvw] = _bf(_rope_transposed(dkc[...], *prev_tabs))
            dkv_ref[:, kvw:] = _bf(dvc[...])

    cur = lambda n: jnp.minimum(n, nb - 1)
    prev = lambda n: jnp.maximum(n - 1, 0)
    blk = lambda width, col: pl.BlockSpec((ATTN_BLOCK, width), lambda n: (cur(n), col))
    tab = pl.BlockSpec((ATTN_BLOCK, LANES), lambda n: (cur(n), 0))
    tabp = pl.BlockSpec((ATTN_BLOCK, LANES), lambda n: (prev(n), 0))
    return _hosted(
        body, comm, name="attn_bwd", grid=(nb + 1,),
        in_specs=[blk(D_MODEL, 0), blk(kvw, COL_K), blk(kvw, COL_V), blk(D_MODEL, 0), tab, tab, tab, tabp, tabp, tabp,
                  pl.BlockSpec(memory_space=pltpu.SMEM)],
        out_specs=[pl.BlockSpec((ATTN_BLOCK, D_MODEL), lambda n: (cur(n), 0)),
                   pl.BlockSpec((ATTN_BLOCK, 2 * kvw), lambda n: (prev(n), 0)),
                   pl.BlockSpec((1, LANES), lambda n: (0, 0))],
        out_shape=[jax.ShapeDtypeStruct((n_tok, D_MODEL), BF16), jax.ShapeDtypeStruct((n_tok, 2 * kvw), BF16),
                   jax.ShapeDtypeStruct((1, LANES), F32)],
        scratch_shapes=[pltpu.VMEM((ATTN_BLOCK, kvw), F32)] * 4,
        compiler_params=_params(("arbitrary",)), args=[proj, proj, proj, dy, *tabs, *tabs, sinks])


def _bmm(a, b):
    return lax.dot_general(a, b, (((2,), (1,)), ((0,), (0,))), preferred_element_type=F32)


def _bmm_nt(a, b):
    return lax.dot_general(a, b, (((2,), (2,)), ((0,), (0,))), preferred_element_type=F32)


def _bmm_tn(a, b):
    return lax.dot_general(a, b, (((1,), (1,)), ((0,), (0,))), preferred_element_type=F32)


def _tril(cb, upper=False):
    shape = (cb, HGRN_CHUNK, HGRN_CHUNK)
    r, c = lax.broadcasted_iota(jnp.int32, shape, 1), lax.broadcasted_iota(jnp.int32, shape, 2)
    return (r <= c) if upper else (r >= c)


def _tri_matmul(x, upper):
    return lax.dot_general(_tril(x.shape[0], upper).astype(F32), x, (((2,), (1,)), ((0,), (0,))),
                           precision=lax.Precision.HIGHEST, preferred_element_type=F32)


@jax.custom_vjp
def _chunk_cumsum(x):
    return _tri_matmul(x, False)


_chunk_cumsum.defvjp(lambda x: (_tri_matmul(x, False), None), lambda _, g: (_tri_matmul(g, True),))


def _hg_elem(fl, qh, lb):
    f = lb + (1.0 - lb) * _sig(fl)
    k = 1.0 - f
    gc = _chunk_cumsum(jnp.log(f))
    last = lax.broadcasted_iota(jnp.int32, gc.shape, 1) == HGRN_CHUNK - 1
    g_last = jnp.sum(jnp.where(last, gc, 0.0), axis=1, keepdims=True)
    q = qh * _sig(qh)
    return q * jnp.exp(gc), k * jnp.exp(-gc), k * jnp.exp(g_last - gc), jnp.exp(g_last)


def _hg_out(q_dec, k_inv, v, st):
    sc = jnp.where(_tril(q_dec.shape[0]), _bmm_nt(_bf(q_dec), _bf(k_inv)), 0.0)
    return _bmm(_bf(sc), _bf(v)) + _bmm_nt(_bf(q_dec), _bf(st)), sc


def _hg_post(o, og, ng):
    on = o * lax.rsqrt(jnp.mean(o * o, axis=-1, keepdims=True) + RMS_EPS) * ng
    return on * (og * _sig(og))


def _hgrn_specs(n_tok, rev):
    nc = n_tok // HGRN_CHUNK
    cb = min(HGRN_CHUNKS_PER_STEP, nc)
    nt = nc // cb
    rows = cb * HGRN_CHUNK
    tt = (lambda t: nt - 1 - t) if rev else (lambda t: t)
    col = lambda base: pl.BlockSpec((rows, LANES), lambda h, t: (tt(t), base + h))
    head_vec = pl.BlockSpec((1, LANES), lambda h, t: (0, h))
    one_vec = pl.BlockSpec((1, LANES), lambda h, t: (0, 0))
    state = pl.BlockSpec((1, cb, HGRN_DK, HGRN_DK), lambda h, t: (h, tt(t), 0, 0))
    return nc, cb, nt, col, head_vec, one_vec, state


def _hgrn_fwd(proj, lb, ng):
    n_tok = proj.shape[0]
    nc, cb, nt, col, head_vec, one_vec, state = _hgrn_specs(n_tok, False)

    def body(fl_ref, qh_ref, ih_ref, og_ref, lb_ref, ng_ref, y_ref, st_ref, s_acc):
        @pl.when(pl.program_id(1) == 0)
        def _():
            s_acc[...] = jnp.zeros_like(s_acc)

        chunks = lambda ref: ref[...].reshape(cb, HGRN_CHUNK, LANES)
        q_dec, k_inv, k_end, decay = _hg_elem(chunks(fl_ref), chunks(qh_ref), lb_ref[...])
        v = chunks(ih_ref)
        upd = _bmm_tn(_bf(v), _bf(k_end))
        st = s_acc[...]
        for ci in range(cb):
            st_ref[0, ci] = st
            st = st * decay[ci] + upd[ci]
        s_acc[...] = st
        o, _ = _hg_out(q_dec, k_inv, v, st_ref[0])
        y_ref[...] = _hg_post(o, chunks(og_ref), ng_ref[...]).reshape(cb * HGRN_CHUNK, LANES)

    return pl.pallas_call(
        body, name="hgrn_fwd", grid=(HGRN_HEADS, nt),
        in_specs=[col(COL_F), col(COL_QH), col(COL_IH), col(COL_OG), head_vec, one_vec],
        out_specs=[col(0), state],
        out_shape=[jax.ShapeDtypeStruct((n_tok, D_MODEL), F32),
                   jax.ShapeDtypeStruct((HGRN_HEADS, nc, HGRN_DK, HGRN_DK), F32)],
        scratch_shapes=[pltpu.VMEM((HGRN_DK, HGRN_DK), F32)],
        compiler_params=_params(("arbitrary", "arbitrary")),
    )(proj, proj, proj, proj, lb, ng)


def _hgrn_bwd(proj, lb, ng, states, dy):
    n_tok = proj.shape[0]
    nc, cb, nt, col, head_vec, one_vec, state = _hgrn_specs(n_tok, True)

    def body(fl_ref, qh_ref, ih_ref, og_ref, lb_ref, ng_ref, st_ref, dy_ref,
             dfl_ref, dqh_ref, dih_ref, dog_ref, dlb_ref, dng_ref, g_acc, g_all):
        h = pl.program_id(0)
        t = pl.program_id(1)

        @pl.when(t == 0)
        def _():
            g_acc[...] = jnp.zeros_like(g_acc)
            dlb_ref[...] = jnp.zeros_like(dlb_ref)

        @pl.when((t == 0) & (h == 0))
        def _():
            dng_ref[...] = jnp.zeros_like(dng_ref)

        chunks = lambda ref: ref[...].reshape(cb, HGRN_CHUNK, LANES)
        flat = lambda val: _bf(val.reshape(cb * HGRN_CHUNK, LANES))
        (q_dec, k_inv, k_end, decay), elem_vjp = jax.vjp(_hg_elem, chunks(fl_ref), chunks(qh_ref), lb_ref[...])
        v = chunks(ih_ref)
        st = st_ref[0]
        o, sc = _hg_out(q_dec, k_inv, v, st)
        _, post_vjp = jax.vjp(_hg_post, o, chunks(og_ref), ng_ref[...])
        do, dog, dng = post_vjp(chunks(dy_ref))
        dob, vb, qb = _bf(do), _bf(v), _bf(q_dec)
        dsc = _bf(jnp.where(_tril(cb), _bmm_nt(dob, vb), 0.0))
        p = _bmm_tn(dob, qb)
        g = g_acc[...]
        for ci in reversed(range(cb)):
            g_all[ci] = g
            g = g * decay[ci] + p[ci]
        g_acc[...] = g
        g = g_all[...]
        gb = _bf(g)
        dq_dec = _bmm(dsc, _bf(k_inv)) + _bmm(dob, _bf(st))
        dk_inv = _bmm_tn(dsc, qb)
        dv = _bmm_tn(_bf(sc), dob) + _bmm_nt(_bf(k_end), gb)
        dk_end = _bmm(vb, gb)
        ddecay = jnp.sum(st * g, axis=1, keepdims=True)
        dfl, dqh, dlb = elem_vjp((dq_dec, dk_inv, dk_end, ddecay))
        dfl_ref[...] = flat(dfl)
        dqh_ref[...] = flat(dqh)
        dih_ref[...] = flat(dv)
        dog_ref[...] = flat(dog)
        dlb_ref[...] += dlb
        dng_ref[...] += dng

    out_col = jax.ShapeDtypeStruct((n_tok, D_MODEL), BF16)
    return pl.pallas_call(
        body, name="hgrn_bwd", grid=(HGRN_HEADS, nt),
        in_specs=[col(COL_F), col(COL_QH), col(COL_IH), col(COL_OG), head_vec, one_vec, state, col(0)],
        out_specs=[col(0), col(0), col(0), col(0), head_vec, one_vec],
        out_shape=[out_col, out_col, out_col, out_col,
                   jax.ShapeDtypeStruct((1, D_MODEL), F32), jax.ShapeDtypeStruct((1, LANES), F32)],
        scratch_shapes=[pltpu.VMEM((HGRN_DK, HGRN_DK), F32), pltpu.VMEM((cb, HGRN_DK, HGRN_DK), F32)],
        compiler_params=_params(("arbitrary", "arbitrary")),
    )(proj, proj, proj, proj, lb, ng, states, dy)


def _lb_fwd(lb_logits):
    def lb_of(l0, l1):
        m = jnp.maximum(l0, l1)
        e0, e1 = jnp.exp(l0 - m), jnp.exp(l1 - m)
        return e0 / (e0 + e1)

    def body(l_ref, o_ref):
        o_ref[...] = lb_of(l_ref[0:1, :], l_ref[1:2, :])

    lb = pl.pallas_call(body, name="lb_fwd", out_shape=jax.ShapeDtypeStruct((1, D_MODEL), F32))(lb_logits)
    return lb, lb_of


def _gate_specs(tm):
    return [pl.BlockSpec((tm, 512), lambda i, c=c: (i, c)) for c in (COL_GA, COL_GA + 1, COL_GH, COL_GH + 1)]


def _mix_fwd(y_attn, y_hgrn, proj, x1, w_pa, w_ph, w_out, g, b):
    n = x1.shape[0]
    tm = min(MIX_TILE, n)

    def body(ya_ref, yh_ref, ga0, ga1, gh0, gh1, x_ref, wpa, wph, wo, g_ref, b_ref, z_ref, o_ref):
        ya = _mm(_bf(ya_ref[...]), wpa[...])
        yh = _mm(_bf(yh_ref[...]), wph[...])
        ga = jnp.concatenate([ga0[...], ga1[...]], axis=1)
        gh = jnp.concatenate([gh0[...], gh1[...]], axis=1)
        merged = _sig(ga) * ya + _sig(gh) * yh
        z = DEEPNORM_ALPHA * x_ref[...] + _mm(_bf(merged), wo[...])
        z_ref[...] = z
        o_ref[...] = _ln(z, g_ref[...], b_ref[...])

    row = pl.BlockSpec((tm, D_MODEL), lambda i: (i, 0))
    sq = _full((D_MODEL, D_MODEL))
    vec = _full((1, D_MODEL))
    return pl.pallas_call(
        body, name="mix_fwd", grid=(n // tm,),
        in_specs=[row, row, *_gate_specs(tm), row, sq, sq, sq, vec, vec],
        out_specs=[row, row], out_shape=[jax.ShapeDtypeStruct((n, D_MODEL), F32)] * 2,
        compiler_params=_params(("arbitrary",)),
    )(y_attn, y_hgrn, proj, proj, proj, proj, x1, w_pa, w_ph, w_out, g, b)


def _mix_bwd(dx2, z2, y_attn, y_hgrn, proj, w_pa, w_ph, w_out, g, b):
    n = z2.shape[0]
    tm = min(MIX_TILE, n)

    def body(do_ref, z_ref, ya_ref, yh_ref, ga0, ga1, gh0, gh1, wpa, wph, wo, g_ref, b_ref,
             dz_ref, mg_ref, dya_ref, dyh_ref, dyat_ref, dyhg_ref, dgt_ref, dg_ref, db_ref):
        _, vjp = jax.vjp(_ln, z_ref[...], g_ref[...], b_ref[...])
        dz, dg, db = vjp(do_ref[...])

        @pl.when(pl.program_id(0) == 0)
        def _():
            dg_ref[...] = jnp.zeros_like(dg_ref)
            db_ref[...] = jnp.zeros_like(db_ref)

        dg_ref[...] += dg
        db_ref[...] += db
        dz_ref[...] = dz
        ya = _mm(_bf(ya_ref[...]), wpa[...])
        yh = _mm(_bf(yh_ref[...]), wph[...])
        ga = jnp.concatenate([ga0[...], ga1[...]], axis=1)
        gh = jnp.concatenate([gh0[...], gh1[...]], axis=1)

        def merge(ga, gh, ya, yh):
            return _sig(ga) * ya + _sig(gh) * yh

        merged, merge_vjp = jax.vjp(merge, ga, gh, ya, yh)
        mg_ref[...] = _bf(merged)
        dmerged = _mm_nt(_bf(dz), wo[...])
        dga, dgh, dya, dyh = merge_vjp(dmerged)
        dya = _bf(dya)
        dyh = _bf(dyh)
        dya_ref[...] = dya
        dyh_ref[...] = dyh
        dgt_ref[:, :D_MODEL] = _bf(dga)
        dgt_ref[:, D_MODEL:] = _bf(dgh)
        dyat_ref[...] = _mm_nt(dya, wpa[...])
        dyhg_ref[...] = _mm_nt(dyh, wph[...])

    row = pl.BlockSpec((tm, D_MODEL), lambda i: (i, 0))
    row2 = pl.BlockSpec((tm, 2 * D_MODEL), lambda i: (i, 0))
    sq = _full((D_MODEL, D_MODEL))
    vec = _full((1, D_MODEL))
    f32_row = jax.ShapeDtypeStruct((n, D_MODEL), F32)
    bf_row = jax.ShapeDtypeStruct((n, D_MODEL), BF16)
    vec_shape = jax.ShapeDtypeStruct((1, D_MODEL), F32)
    return pl.pallas_call(
        body, name="mix_bwd", grid=(n // tm,),
        in_specs=[row, row, row, row, *_gate_specs(tm), sq, sq, sq, vec, vec],
        out_specs=[row, row, row, row, row, row, row2, vec, vec],
        out_shape=[f32_row, bf_row, bf_row, bf_row, f32_row, f32_row,
                   jax.ShapeDtypeStruct((n, 2 * D_MODEL), BF16), vec_shape, vec_shape],
        compiler_params=_params(("arbitrary",)),
    )(dx2, z2, y_attn, y_hgrn, proj, proj, proj, proj, w_pa, w_ph, w_out, g, b)


def _position():
    x, y, c = lax.axis_index("x"), lax.axis_index("y"), lax.axis_index("c")
    chips = [(1 - x, y), (x, 1 - y), (1 - x, 1 - y)]
    return x, y, c, chips


def _any_specs(k):
    return [pl.BlockSpec(memory_space=pl.ANY)] * k


class _GatherWeights:
    def __init__(self, shards):
        nw = len(shards)
        self.inputs = list(shards)
        self.out_shape = [jax.ShapeDtypeStruct((N_CHIPS, *s.shape), s.dtype) for s in shards]
        self.scratch = [pltpu.SemaphoreType.DMA((nw,)), pltpu.SemaphoreType.DMA((nw * 6,)),
                        pltpu.SemaphoreType.DMA((nw * 6,))]

    def _copies(self, ins, outs, sems):
        nw = len(ins)
        local_sem, send_sem, recv_sem = sems
        x, y, c, chips = _position()
        me = 2 * x + y
        sibling = (x, y, 1 - c)
        half_rows = [s.shape[0] // 2 for s in self.inputs]

        def half(w, chip_idx, which):
            return outs[w].at[chip_idx, pl.ds(which * half_rows[w], half_rows[w])]

        def remote(w, k, src, dst, to):
            return pltpu.make_async_remote_copy(src_ref=src, dst_ref=dst, send_sem=send_sem.at[w * 6 + k],
                                                recv_sem=recv_sem.at[w * 6 + k], device_id=to, device_id_type=MESH)

        local = [pltpu.make_async_copy(ins[w], outs[w].at[me], local_sem.at[w]) for w in range(nw)]
        first = [remote(w, j, ins[w].at[pl.ds(c * half_rows[w], half_rows[w])], half(w, me, c), (px, py, c))
                 for w in range(nw) for j, (px, py) in enumerate(chips)]
        landed = [half(w, 2 * px + py, c) for w in range(nw) for (px, py) in chips]
        arrive = [remote(w, j, landed[w * 3 + j], landed[w * 3 + j], (px, py, c))
                  for w in range(nw) for j, (px, py) in enumerate(chips)]
        passed = [remote(w, 3 + j, landed[w * 3 + j], landed[w * 3 + j], sibling) for w in range(nw) for j in range(3)]
        from_sibling = [remote(w, 3 + j, half(w, 2 * px + py, 1 - c), half(w, 2 * px + py, 1 - c), sibling)
                        for w in range(nw) for j, (px, py) in enumerate(chips)]
        return local, first, arrive, passed, from_sibling

    def start(self, ins, outs, sems):
        local, first, _, _, _ = self._copies(ins, outs, sems)
        for cp in local + first:
            cp.start()

    def finish(self, ins, outs, sems):
        local, first, arrive, passed, from_sibling = self._copies(ins, outs, sems)
        for cp_in, cp_on in zip(arrive, passed):
            cp_in.wait_recv()
            cp_on.start()
        for cp in from_sibling:
            cp.wait_recv()
        for cp in first + passed:
            cp.wait_send()
        for cp in local:
            cp.wait()


class _ExchangeGrads:
    def __init__(self, grads):
        nw = len(grads)
        self.inputs = list(grads)
        self.out_shape = [jax.ShapeDtypeStruct(g.shape, g.dtype) for g in grads]
        self.scratch = [pltpu.SemaphoreType.DMA((nw,)), pltpu.SemaphoreType.DMA((nw * 3,)),
                        pltpu.SemaphoreType.DMA((nw * 3,))]

    def _copies(self, ins, outs, sems):
        nw = len(ins)
        local_sem, send_sem, recv_sem = sems
        x, y, c, chips = _position()
        me = 2 * x + y

        def remote(w, j, src, dst, chip):
            return pltpu.make_async_remote_copy(src_ref=src, dst_ref=dst, send_sem=send_sem.at[w * 3 + j],
                                                recv_sem=recv_sem.at[w * 3 + j], device_id=(*chip, c),
                                                device_id_type=MESH)

        local = [pltpu.make_async_copy(ins[w].at[me], outs[w].at[me], local_sem.at[w]) for w in range(nw)]
        sends = [remote(w, j, ins[w].at[2 * px + py], outs[w].at[me], (px, py))
                 for w in range(nw) for j, (px, py) in enumerate(chips)]
        arrive = [remote(w, j, outs[w].at[2 * px + py], outs[w].at[2 * px + py], (px, py))
                  for w in range(nw) for j, (px, py) in enumerate(chips)]
        return local, sends, arrive

    def start(self, ins, outs, sems):
        local, sends, _ = self._copies(ins, outs, sems)
        for cp in local + sends:
            cp.start()

    def finish(self, ins, outs, sems):
        local, sends, arrive = self._copies(ins, outs, sems)
        for cp in arrive:
            cp.wait_recv()
        for cp in sends:
            cp.wait_send()
        for cp in local:
            cp.wait()


def _hosted(body, comm, *, name, grid, in_specs, out_specs, out_shape, scratch_shapes, compiler_params, args):
    if comm is None:
        res = pl.pallas_call(body, name=name, grid=grid, in_specs=in_specs, out_specs=out_specs, out_shape=out_shape,
                             scratch_shapes=scratch_shapes, compiler_params=compiler_params)(*args)
        return list(res), []
    n_in, n_out, n_scr = len(in_specs), len(out_specs), len(scratch_shapes)
    c_in, c_out = len(comm.inputs), len(comm.out_shape)

    def hosted_body(*refs):
        refs = list(refs)
        cut = lambda k: (refs[:k], refs[k:])
        main_in, refs = cut(n_in)
        comm_in, refs = cut(c_in)
        main_out, refs = cut(n_out)
        comm_out, refs = cut(c_out)
        main_scr, comm_scr = cut(n_scr)
        ids = [pl.program_id(a) for a in range(len(grid))]
        first = functools.reduce(jnp.logical_and, [i == 0 for i in ids])
        last = functools.reduce(jnp.logical_and, [i == g - 1 for i, g in zip(ids, grid)])

        @pl.when(first)
        def _():
            comm.start(comm_in, comm_out, comm_scr)

        body(*main_in, *main_out, *main_scr)

        @pl.when(last)
        def _():
            comm.finish(comm_in, comm_out, comm_scr)

    res = pl.pallas_call(
        hosted_body, name=name, grid=grid, in_specs=[*in_specs, *_any_specs(c_in)],
        out_specs=[*out_specs, *_any_specs(c_out)], out_shape=[*out_shape, *comm.out_shape],
        scratch_shapes=[*scratch_shapes, *comm.scratch], compiler_params=compiler_params,
    )(*args, *comm.inputs)
    return list(res[:n_out]), list(res[n_out:])


def _run_comm(name, comm):
    def body(*refs):
        refs = list(refs)
        c_in, c_out = len(comm.inputs), len(comm.out_shape)
        ins, outs, sems = refs[:c_in], refs[c_in:c_in + c_out], refs[c_in + c_out:]
        comm.start(ins, outs, sems)
        comm.finish(ins, outs, sems)

    return list(pl.pallas_call(
        body, name=name, in_specs=_any_specs(len(comm.inputs)), out_specs=_any_specs(len(comm.out_shape)),
        out_shape=comm.out_shape, scratch_shapes=comm.scratch,
    )(*comm.inputs))


def _sum_slots(name, slots):
    _, rows, cols = slots.shape
    tr = _update_rows(rows)

    def body(s_ref, o_ref):
        acc = s_ref[0].astype(F32)
        for i in range(1, N_CHIPS):
            acc = acc + s_ref[i].astype(F32)
        o_ref[...] = acc

    return pl.pallas_call(
        body, name=name, grid=(rows // tr,),
        in_specs=[pl.BlockSpec((N_CHIPS, tr, cols), lambda i: (0, i, 0))],
        out_specs=pl.BlockSpec((tr, cols), lambda i: (i, 0)),
        out_shape=jax.ShapeDtypeStruct((rows, cols), F32),
        compiler_params=_params(("arbitrary",)),
    )(slots)


def _swap_with_sibling(parts):
    nw = len(parts)

    def body(*refs):
        ins, outs = refs[:nw], refs[nw:2 * nw]
        send_sem, recv_sem = refs[2 * nw:]
        x, y, c, _ = _position()
        copies = [pltpu.make_async_remote_copy(src_ref=ins[w], dst_ref=outs[w], send_sem=send_sem.at[w],
                                               recv_sem=recv_sem.at[w], device_id=(x, y, 1 - c), device_id_type=MESH)
                  for w in range(nw)]
        for cp in copies:
            cp.start()
        for cp in copies:
            cp.wait()

    return pl.pallas_call(
        body, name="swap_with_sibling",
        in_specs=_any_specs(nw), out_specs=_any_specs(nw),
        out_shape=[jax.ShapeDtypeStruct(p.shape, p.dtype) for p in parts],
        scratch_shapes=[pltpu.SemaphoreType.DMA((nw,)), pltpu.SemaphoreType.DMA((nw,))],
    )(*parts)


def _sum_small(part):
    def body(p_ref, o_ref, buf, send_sem, recv_sem):
        x, y, c, _ = _position()
        me = 4 * x + 2 * y + c
        buf[me] = p_ref[...]
        copies = []
        for k in range(1, N_DEV):
            peer = tuple(1 - v if (k >> s) & 1 else v for v, s in ((x, 2), (y, 1), (c, 0)))
            copies.append(pltpu.make_async_remote_copy(src_ref=p_ref, dst_ref=buf.at[me], send_sem=send_sem.at[k - 1],
                                                       recv_sem=recv_sem.at[k - 1], device_id=peer, device_id_type=MESH))
        for cp in copies:
            cp.start()
        for cp in copies:
            cp.wait()
        acc = buf[0]
        for d in range(1, N_DEV):
            acc = acc + buf[d]
        o_ref[...] = acc

    vm = pl.BlockSpec(memory_space=pltpu.VMEM)
    return pl.pallas_call(
        body, name="sum_small", in_specs=[vm], out_specs=vm,
        out_shape=jax.ShapeDtypeStruct((1, SM_LEN), F32),
        scratch_shapes=[pltpu.VMEM((N_DEV, 1, SM_LEN), F32), pltpu.SemaphoreType.DMA((N_DEV - 1,)),
                        pltpu.SemaphoreType.DMA((N_DEV - 1,))],
    )(part)


def _adamw(w, g, m, v):
    m = ADAM_B1 * m + (1.0 - ADAM_B1) * g
    v = ADAM_B2 * v + (1.0 - ADAM_B2) * (g * g)
    m_hat = m / (1.0 - ADAM_B1 ** ADAM_STEP)
    v_hat = v / (1.0 - ADAM_B2 ** ADAM_STEP)
    delta = -ADAM_LR * (m_hat / (jnp.sqrt(v_hat) + ADAM_EPS) + ADAM_WD * w)
    return delta, m, v


def _adam_big(name, p_own, p_sibling, w, m, v):
    rows, cols = w.shape
    tr = _update_rows(rows)

    def body(p_ref, q_ref, w_ref, m_ref, v_ref, g_ref, d_ref, nm_ref, nv_ref):
        g = p_ref[...] + q_ref[...]
        g_ref[...] = g
        d_ref[...], nm_ref[...], nv_ref[...] = _adamw(w_ref[...], g, m_ref[...], v_ref[...])

    spec = pl.BlockSpec((tr, cols), lambda i: (i, 0))
    return pl.pallas_call(
        body, name=name, grid=(rows // tr,), in_specs=[spec] * 5, out_specs=[spec] * 4,
        out_shape=[jax.ShapeDtypeStruct((rows, cols), F32)] * 4,
        compiler_params=_params(("arbitrary",)),
    )(p_own, p_sibling, w, m, v)


def _adam_small(total, wvec, mvec, vvec, lb_of):
    def body(t_ref, w_ref, m_ref, v_ref, g_ref, d_ref, nm_ref, nv_ref):
        g_ref[:, :SM_LB] = t_ref[:, :SM_LB]
        _, vjp = jax.vjp(lb_of, w_ref[:, PK_LB0:PK_LB1], w_ref[:, PK_LB1:PK_LEN])
        dl0, dl1 = vjp(t_ref[:, SM_LB:SM_LOSS])
        g_ref[:, PK_LB0:PK_LB1] = dl0
        g_ref[:, PK_LB1:PK_LEN] = dl1
        d_ref[...], nm_ref[...], nv_ref[...] = _adamw(w_ref[...], g_ref[...], m_ref[...], v_ref[...])

    return pl.pallas_call(body, name="adam_small", out_shape=[jax.ShapeDtypeStruct((1, PK_LEN), F32)] * 4)(
        total, wvec, mvec, vvec)


_BIG = ("ffn1_w1", "ffn1_w3", "ffn1_w2", "w_in", "w_proj_attn", "w_proj_hgrn", "w_out", "ffn2_w1", "ffn2_w3", "ffn2_w2")
_SMALL = ("ln1_g", "ln1_b", "ln2_g", "ln2_b", "ln3_g", "ln3_b", "b_in", "attn_sinks", "hgrn_norm_g", "hgrn_lb_logits")
_ORDER = ("ln1_g", "ln1_b", "ffn1_w1", "ffn1_w3", "ffn1_w2", "ln2_g", "ln2_b", "w_in", "b_in", "attn_sinks",
          "hgrn_lb_logits", "hgrn_norm_g", "w_proj_attn", "w_proj_hgrn", "w_out", "ln3_g", "ln3_b",
          "ffn2_w1", "ffn2_w3", "ffn2_w2")


def _pad_to(a, rows, cols):
    return jnp.pad(a, ((0, rows - a.shape[0]), (0, cols - a.shape[1])))


def _send_form(name, w):
    wb = _bf(w)
    if name.endswith(("w1", "w3")) and name.startswith("ffn"):
        return _pad_to(wb, D_MODEL, FF_PAD)
    if name.endswith("w2") and name.startswith("ffn"):
        return _pad_to(wb, FF_PAD, D_MODEL)
    return wb


def _pack_small(p):
    pieces = [p[k].reshape(1, -1) for k in ("ln1_g", "ln1_b", "ln2_g", "ln2_b", "ln3_g", "ln3_b", "b_in")]
    pieces.append(_pad_to(p["attn_sinks"].reshape(1, -1), 1, LANES))
    pieces.append(p["hgrn_norm_g"].reshape(1, -1))
    pieces.append(p["hgrn_lb_logits"].reshape(1, -1))
    return jnp.concatenate(pieces, axis=1)


def _unpack_small(vec, like):
    out = {}
    for i, k in enumerate(("ln1_g", "ln1_b", "ln2_g", "ln2_b", "ln3_g", "ln3_b")):
        out[k] = vec[:, i * D_MODEL:(i + 1) * D_MODEL]
    out["b_in"] = vec[:, SM_BIN:SM_SINK]
    out["attn_sinks"] = vec[:, SM_SINK:SM_SINK + N_Q_HEADS]
    out["hgrn_norm_g"] = vec[:, SM_NG:SM_LB]
    out["hgrn_lb_logits"] = vec[:, PK_LB0:PK_LEN].reshape(2, D_MODEL)
    return {k: v.reshape(like[k].shape) for k, v in out.items()}


def kernel(x, ln1_g, ln1_b, ffn1_w1, ffn1_w3, ffn1_w2, ln2_g, ln2_b, w_in, b_in, attn_sinks, hgrn_lb_logits, hgrn_norm_g, w_proj_attn, w_proj_hgrn, w_out, ln3_g, ln3_b, ffn2_w1, ffn2_w3, ffn2_w2, loss_target, m_ln1_g, m_ln1_b, m_ffn1_w1, m_ffn1_w3, m_ffn1_w2, m_ln2_g, m_ln2_b, m_w_in, m_b_in, m_attn_sinks, m_hgrn_lb_logits, m_hgrn_norm_g, m_w_proj_attn, m_w_proj_hgrn, m_w_out, m_ln3_g, m_ln3_b, m_ffn2_w1, m_ffn2_w3, m_ffn2_w2, v_ln1_g, v_ln1_b, v_ffn1_w1, v_ffn1_w3, v_ffn1_w2, v_ln2_g, v_ln2_b, v_w_in, v_b_in, v_attn_sinks, v_hgrn_lb_logits, v_hgrn_norm_g, v_w_proj_attn, v_w_proj_hgrn, v_w_out, v_ln3_g, v_ln3_b, v_ffn2_w1, v_ffn2_w3, v_ffn2_w2):
    w = dict(ln1_g=ln1_g, ln1_b=ln1_b, ffn1_w1=ffn1_w1, ffn1_w3=ffn1_w3, ffn1_w2=ffn1_w2, ln2_g=ln2_g, ln2_b=ln2_b,
             w_in=w_in, b_in=b_in, attn_sinks=attn_sinks, hgrn_lb_logits=hgrn_lb_logits, hgrn_norm_g=hgrn_norm_g,
             w_proj_attn=w_proj_attn, w_proj_hgrn=w_proj_hgrn, w_out=w_out, ln3_g=ln3_g, ln3_b=ln3_b,
             ffn2_w1=ffn2_w1, ffn2_w3=ffn2_w3, ffn2_w2=ffn2_w2)
    mom = dict(ln1_g=m_ln1_g, ln1_b=m_ln1_b, ffn1_w1=m_ffn1_w1, ffn1_w3=m_ffn1_w3, ffn1_w2=m_ffn1_w2, ln2_g=m_ln2_g,
               ln2_b=m_ln2_b, w_in=m_w_in, b_in=m_b_in, attn_sinks=m_attn_sinks, hgrn_lb_logits=m_hgrn_lb_logits,
               hgrn_norm_g=m_hgrn_norm_g, w_proj_attn=m_w_proj_attn, w_proj_hgrn=m_w_proj_hgrn, w_out=m_w_out,
               ln3_g=m_ln3_g, ln3_b=m_ln3_b, ffn2_w1=m_ffn2_w1, ffn2_w3=m_ffn2_w3, ffn2_w2=m_ffn2_w2)
    var = dict(ln1_g=v_ln1_g, ln1_b=v_ln1_b, ffn1_w1=v_ffn1_w1, ffn1_w3=v_ffn1_w3, ffn1_w2=v_ffn1_w2, ln2_g=v_ln2_g,
               ln2_b=v_ln2_b, w_in=v_w_in, b_in=v_b_in, attn_sinks=v_attn_sinks, hgrn_lb_logits=v_hgrn_lb_logits,
               hgrn_norm_g=v_hgrn_norm_g, w_proj_attn=v_w_proj_attn, w_proj_hgrn=v_w_proj_hgrn, w_out=v_w_out,
               ln3_g=v_ln3_g, ln3_b=v_ln3_b, ffn2_w1=v_ffn2_w1, ffn2_w3=v_ffn2_w3, ffn2_w2=v_ffn2_w2)

    n_tok = x.shape[1]
    x0 = x.reshape(n_tok, D_MODEL)
    target = loss_target.reshape(n_tok, D_MODEL)

    shard = {k: _send_form(k, w[k][0]) for k in _BIG}
    gather = lambda keys: _GatherWeights([shard[k] for k in keys])
    slots = {}
    exchange = lambda keys: _ExchangeGrads([big[k] for k in keys])
    ffn1_keys = ("ffn1_w1", "ffn1_w3", "ffn1_w2")
    mixer_keys = ("w_in", "w_proj_attn", "w_proj_hgrn", "w_out")
    ffn2_keys = ("ffn2_w1", "ffn2_w3", "ffn2_w2")
    f1 = _run_comm("gather_ffn1", gather(ffn1_keys))

    tabs = _rope_tables(n_tok)
    lb, lb_of = _lb_fwd(hgrn_lb_logits)
    (z1, x1), (w_in_g, w_pa, w_ph, w_o) = _ffn_fwd("ffn1_fwd", x0, *f1, ln1_g, ln1_b, comm=gather(mixer_keys))
    w_pa, w_ph, w_o = (t.reshape(D_MODEL, D_MODEL) for t in (w_pa, w_ph, w_o))
    (proj,), f2 = _in_proj(x1, w_in_g, b_in, comm=gather(ffn2_keys))
    y_attn = _attn_fwd(proj, tabs, attn_sinks)
    y_hgrn, states = _hgrn_fwd(proj, lb, hgrn_norm_g)
    z2, x2 = _mix_fwd(y_attn, y_hgrn, proj, x1, w_pa, w_ph, w_o, ln2_g, ln2_b)
    (z3, dy, loss_part), _ = _ffn_fwd("ffn2_fwd", x2, *f2, ln3_g, ln3_b, target=target)

    big = {}
    small = {}
    (dx2, a2, dh1_2, dh3_2, df2, small["ln3_g"], small["ln3_b"]), _ = _ffn_bwd(
        "ffn2_bwd", x2, z3, dy, *f2, ln3_g, ln3_b)
    (big["ffn2_w1"],), _ = _grad_matmul("ffn2_dw1", x2, "shared", D_MODEL, dh1_2, "batch", FF_PAD)
    (big["ffn2_w3"],), _ = _grad_matmul("ffn2_dw3", x2, "shared", D_MODEL, dh3_2, "batch", FF_PAD)
    (big["ffn2_w2"],), _ = _grad_matmul("ffn2_dw2", a2, "batch", FF_PAD, df2, "shared", D_MODEL)
    (dz2, merged, dya, dyh, dy_attn, dy_hgrn, dgates, small["ln2_g"], small["ln2_b"]) = _mix_bwd(
        dx2, z2, y_attn, y_hgrn, proj, w_pa, w_ph, w_o, ln2_g, ln2_b)
    (big["w_out"],), _ = _grad_matmul("dw_out", merged, "cols", PROJ_SHARD, dz2, "shared", D_MODEL)
    (big["w_proj_attn"],), _ = _grad_matmul("dw_proj_attn", y_attn, "cols", PROJ_SHARD, dya, "shared", D_MODEL)
    (big["w_proj_hgrn"],), _ = _grad_matmul("dw_proj_hgrn", y_hgrn, "cols", PROJ_SHARD, dyh, "shared", D_MODEL)
    dfl, dqh, dih, dog, dlb, small["hgrn_norm_g"] = _hgrn_bwd(proj, lb, hgrn_norm_g, states, dy_hgrn)
    early_keys = ffn2_keys + mixer_keys[1:]
    (dq, dkv, dsinks), early = _attn_bwd(proj, dy_attn, tabs, attn_sinks, comm=exchange(early_keys))
    slots.update(zip(early_keys, early))
    dproj = jnp.concatenate([dq, dkv, dfl, dqh, dih, dog, dgates], axis=1)
    (big["w_in"], small["b_in"]), _ = _grad_matmul(
        "dw_in", x1, "shared", D_MODEL, dproj, "cols", IN_SHARD, colsum=True)
    dx1 = _in_proj_dx(dproj, w_in_g, dz2)
    (grad_x, a1, dh1_1, dh3_1, df1, small["ln1_g"], small["ln1_b"]), (slots["w_in"],) = _ffn_bwd(
        "ffn1_bwd", x0, z1, dx1, *f1, ln1_g, ln1_b, comm=exchange(("w_in",)))
    (big["ffn1_w1"],), _ = _grad_matmul("ffn1_dw1", x0, "shared", D_MODEL, dh1_1, "batch", FF_PAD)
    (big["ffn1_w3"],), (slots["ffn1_w1"],) = _grad_matmul(
        "ffn1_dw3", x0, "shared", D_MODEL, dh3_1, "batch", FF_PAD, comm=exchange(("ffn1_w1",)))
    (big["ffn1_w2"],), (slots["ffn1_w3"],) = _grad_matmul(
        "ffn1_dw2", a1, "batch", FF_PAD, df1, "shared", D_MODEL, comm=exchange(("ffn1_w3",)))
    (slots["ffn1_w2"],) = _run_comm("exchange_last", exchange(("ffn1_w2",)))

    partial = []
    for k in _BIG:
        p = _sum_slots("sum_" + k, slots[k])
        partial.append(p[:w[k].shape[1], :w[k].shape[2]])
    from_sibling = _swap_with_sibling(partial)

    outs = {"grad": {}, "delta": {}, "m": {}, "v": {}}
    for k, p, q in zip(_BIG, partial, from_sibling):
        res = _adam_big("adam_" + k, p, q, w[k][0], mom[k][0], var[k][0])
        for kind, r in zip(("grad", "delta", "m", "v"), res):
            outs[kind][k] = r.reshape(w[k].shape)

    part = jnp.concatenate(
        [small[k] for k in ("ln1_g", "ln1_b", "ln2_g", "ln2_b", "ln3_g", "ln3_b", "b_in")]
        + [dsinks, small["hgrn_norm_g"], dlb, loss_part], axis=1)
    total = _sum_small(part)
    res = _adam_small(total, _pack_small(w), _pack_small(mom), _pack_small(var), lb_of)
    for kind, r in zip(("grad", "delta", "m", "v"), res):
        outs[kind].update(_unpack_small(r, w))
    loss = total[0, SM_LOSS]

    return (loss, grad_x.reshape(x.shape), *[outs["grad"][k] for k in _ORDER], *[outs["delta"][k] for k in _ORDER],
            *[outs["m"][k] for k in _ORDER], *[outs["v"][k] for k in _ORDER])
```

```python
import functools

import jax
import jax.numpy as jnp
from jax import lax
from jax.experimental import pallas as pl
from jax.experimental.pallas import tpu as pltpu

F32 = jnp.float32
BF16 = jnp.bfloat16

D_MODEL = 1024
N_Q_HEADS = 16
N_KV_HEADS = 4
HEAD_DIM = 64
ATTN_BLOCK = 128
ROPE_THETA = 500000.0
ROPE_DIM = HEAD_DIM // 4
HGRN_HEADS = 8
HGRN_DK = 128
HGRN_CHUNK = 64
D_FF = 2816
D_IN = 7680
DEEPNORM_ALPHA = 2 ** 0.25
LN_EPS = 1e-5
RMS_EPS = 1e-6
NEG_INF = -1e30

ADAM_LR = 0.001
ADAM_B1 = 0.9
ADAM_B2 = 0.999
ADAM_EPS = 1e-08
ADAM_WD = 0.01
ADAM_STEP = 10

N_CHIPS = 4
N_DEV = 8
FF_SHARD = D_FF // N_CHIPS
LANES = 128
MXU_COLS = 256
FF_PAD = -(-FF_SHARD // MXU_COLS) * MXU_COLS
IN_SHARD = D_IN // N_CHIPS
PROJ_SHARD = D_MODEL // N_CHIPS
ROW_TILE = 512
GRAD_ROWS = 2048
MIX_TILE = 256
UPDATE_ROWS = 128
HGRN_CHUNKS_PER_STEP = 8
VMEM_LIMIT = 56 * 1024 * 1024

COL_K = 1024 // 256
COL_V = 1280 // 256
COL_F = 1536 // LANES
COL_QH = 2560 // LANES
COL_IH = 3584 // LANES
COL_OG = 4608 // LANES
COL_GA = 5632 // 512
COL_GH = 6656 // 512

SM_LN = 0
SM_BIN = 6 * D_MODEL
SM_SINK = SM_BIN + D_IN
SM_NG = SM_SINK + LANES
SM_LB = SM_NG + LANES
SM_LOSS = SM_LB + D_MODEL
SM_LEN = SM_LOSS + LANES
PK_LB0 = SM_LB
PK_LB1 = SM_LB + D_MODEL
PK_LEN = PK_LB1 + D_MODEL

MESH = pl.DeviceIdType.MESH


def _mm(a, b):
    return lax.dot_general(a, b, (((1,), (0,)), ((), ())), preferred_element_type=F32)


def _mm_nt(a, b):
    return lax.dot_general(a, b, (((1,), (1,)), ((), ())), preferred_element_type=F32)


def _mm_tn(a, b):
    return lax.dot_general(a, b, (((0,), (0,)), ((), ())), preferred_element_type=F32)


def _bf(v):
    return v.astype(BF16)


def _sig(v):
    return jax.nn.sigmoid(v)


def _ln(z, g, b):
    mu = jnp.mean(z, axis=-1, keepdims=True)
    zc = z - mu
    var = jnp.mean(zc * zc, axis=-1, keepdims=True)
    return zc * lax.rsqrt(var + LN_EPS) * g + b


def _swiglu_act(h1, h3):
    return (h1 * _sig(h1)) * h3


def _params(sem=None):
    return pltpu.CompilerParams(dimension_semantics=sem, vmem_limit_bytes=VMEM_LIMIT)


def _full(shape):
    nd = len(shape)
    return pl.BlockSpec(shape, lambda *_: (0,) * nd)


def _update_rows(rows):
    return max(t for t in range(8, UPDATE_ROWS + 1, 8) if rows % t == 0)


def _ffn_fwd(name, x, w1g, w3g, w2g, g, b, target=None, comm=None):
    n = x.shape[0]
    tm = min(ROW_TILE, n)
    final = target is not None

    def body(*refs):
        if final:
            x_ref, w1_ref, w3_ref, w2_ref, g_ref, b_ref, t_ref, z_ref, o_ref, xb, loss_ref, acc = refs
        else:
            x_ref, w1_ref, w3_ref, w2_ref, g_ref, b_ref, z_ref, o_ref, xb, ob_ref, acc = refs
        i = pl.program_id(0)
        j = pl.program_id(1)

        @pl.when(j == 0)
        def _():
            xb[...] = _bf(x_ref[...])
            acc[...] = jnp.zeros_like(acc)

        xv = xb[...]
        part = None
        for c in range(FF_PAD // MXU_COLS):
            cols = slice(c * MXU_COLS, (c + 1) * MXU_COLS)
            h1 = _mm(xv, w1_ref[0, :, cols])
            h3 = _mm(xv, w3_ref[0, :, cols])
            d = _mm(_bf(_swiglu_act(h1, h3)), w2_ref[0, cols, :])
            part = d if part is None else part + d
        acc[...] += part

        @pl.when(j == N_CHIPS - 1)
        def _():
            z = DEEPNORM_ALPHA * x_ref[...] + 0.5 * acc[...]
            z_ref[...] = z
            y = _ln(z, g_ref[...], b_ref[...])
            if final:
                e = y - t_ref[...]

                @pl.when(i == 0)
                def _():
                    loss_ref[...] = jnp.zeros_like(loss_ref)

                loss_ref[...] += jnp.sum(e * e) * (0.5 / D_MODEL)
                o_ref[...] = e * (1.0 / D_MODEL)
            else:
                o_ref[...] = y
                ob_ref[...] = _bf(y)

    row = pl.BlockSpec((tm, D_MODEL), lambda i, j: (i, 0))
    wcol = pl.BlockSpec((1, D_MODEL, FF_PAD), lambda i, j: (j, 0, 0))
    wrow = pl.BlockSpec((1, FF_PAD, D_MODEL), lambda i, j: (j, 0, 0))
    vec = pl.BlockSpec((1, D_MODEL), lambda i, j: (0, 0))
    in_specs = [row, wcol, wcol, wrow, vec, vec]
    args = [x, w1g, w3g, w2g, g, b]
    out_specs = [row, row, row]
    out_shape = [jax.ShapeDtypeStruct((n, D_MODEL), F32)] * 2 + [jax.ShapeDtypeStruct((n, D_MODEL), BF16)]
    if final:
        in_specs.append(row)
        args.append(target)
        out_specs.append(pl.BlockSpec((1, LANES), lambda i, j: (0, 0)))
        out_shape.append(jax.ShapeDtypeStruct((1, LANES), F32))
    else:
        out_specs.append(row)
        out_shape.append(jax.ShapeDtypeStruct((n, D_MODEL), BF16))
    return _hosted(
        body, comm, name=name, grid=(n // tm, N_CHIPS), in_specs=in_specs, out_specs=out_specs, out_shape=out_shape,
        scratch_shapes=[pltpu.VMEM((tm, D_MODEL), F32)],
        compiler_params=_params(("arbitrary", "arbitrary")), args=args)


def _ffn_bwd(name, x, z, dout, w1g, w3g, w2g, g, b, comm=None):
    n = x.shape[0]
    tm = min(ROW_TILE, n)

    def body(xb, z_ref, do_ref, w1_ref, w3_ref, w2_ref, g_ref, b_ref,
             dx_ref, a_ref, dh1_ref, dh3_ref, df_ref, dg_ref, db_ref, acc, dzs):
        i = pl.program_id(0)
        j = pl.program_id(1)

        @pl.when(j == 0)
        def _():
            acc[...] = jnp.zeros_like(acc)
            _, vjp = jax.vjp(_ln, z_ref[...], g_ref[...], b_ref[...])
            dz, dg, db = vjp(do_ref[...])
            dzs[...] = dz
            df_ref[...] = _bf(0.5 * dz)

            @pl.when(i == 0)
            def _():
                dg_ref[...] = jnp.zeros_like(dg_ref)
                db_ref[...] = jnp.zeros_like(db_ref)

            dg_ref[...] += dg
            db_ref[...] += db

        h1 = _mm(xb[...], w1_ref[0])
        h3 = _mm(xb[...], w3_ref[0])
        a, act_vjp = jax.vjp(_swiglu_act, h1, h3)
        da = _mm_nt(df_ref[...], w2_ref[0])
        dh1, dh3 = act_vjp(da)
        dh1 = _bf(dh1)
        dh3 = _bf(dh3)
        a_ref[0] = _bf(a)
        dh1_ref[0] = dh1
        dh3_ref[0] = dh3
        acc[...] += _mm_nt(dh1, w1_ref[0]) + _mm_nt(dh3, w3_ref[0])

        @pl.when(j == N_CHIPS - 1)
        def _():
            dx_ref[...] = DEEPNORM_ALPHA * dzs[...] + acc[...]

    row = pl.BlockSpec((tm, D_MODEL), lambda i, j: (i, 0))
    wcol = pl.BlockSpec((1, D_MODEL, FF_PAD), lambda i, j: (j, 0, 0))
    wrow = pl.BlockSpec((1, FF_PAD, D_MODEL), lambda i, j: (j, 0, 0))
    vec = pl.BlockSpec((1, D_MODEL), lambda i, j: (0, 0))
    hid = pl.BlockSpec((1, tm, FF_PAD), lambda i, j: (j, i, 0))
    hid_shape = jax.ShapeDtypeStruct((N_CHIPS, n, FF_PAD), BF16)
    return _hosted(
        body, comm, name=name, grid=(n // tm, N_CHIPS),
        in_specs=[row, row, row, wcol, wcol, wrow, vec, vec],
        out_specs=[row, hid, hid, hid, row, vec, vec],
        out_shape=[jax.ShapeDtypeStruct((n, D_MODEL), F32), hid_shape, hid_shape, hid_shape,
                   jax.ShapeDtypeStruct((n, D_MODEL), BF16),
                   jax.ShapeDtypeStruct((1, D_MODEL), F32), jax.ShapeDtypeStruct((1, D_MODEL), F32)],
        scratch_shapes=[pltpu.VMEM((tm, D_MODEL), F32), pltpu.VMEM((tm, D_MODEL), F32)],
        compiler_params=_params(("arbitrary", "arbitrary")), args=[x, z, dout, w1g, w3g, w2g, g, b])


def _operand_spec(arr, mode, tn, width):
    if mode == "shared":
        return pl.BlockSpec((tn, width), lambda s, k: (k, 0))
    if mode == "batch":
        return pl.BlockSpec((1, tn, width), lambda s, k: (s, k, 0))
    assert mode == "cols" and arr.shape[1] == N_CHIPS * width
    return pl.BlockSpec((tn, width), lambda s, k: (k, s))


def _grad_matmul(name, a, a_mode, ka, b, b_mode, kb, colsum=False, comm=None):
    n = a.shape[-2]
    tn = min(GRAD_ROWS, n)
    nk = n // tn

    def body(*refs):
        if colsum:
            a_ref, b_ref, o_ref, cs_ref, acc = refs
        else:
            a_ref, b_ref, o_ref, acc = refs
        k = pl.program_id(1)
        av = a_ref[0] if a_mode == "batch" else a_ref[...]
        bv = b_ref[0] if b_mode == "batch" else b_ref[...]

        @pl.when(k == 0)
        def _():
            acc[...] = jnp.zeros_like(acc)
            if colsum:
                cs_ref[...] = jnp.zeros_like(cs_ref)

        acc[...] += _mm_tn(av, bv)
        if colsum:
            cs_ref[...] += jnp.sum(bv.astype(F32), axis=0, keepdims=True)

        @pl.when(k == nk - 1)
        def _():
            o_ref[0] = _bf(acc[...])

    out_specs = [pl.BlockSpec((1, ka, kb), lambda s, k: (s, 0, 0))]
    out_shape = [jax.ShapeDtypeStruct((N_CHIPS, ka, kb), BF16)]
    if colsum:
        out_specs.append(pl.BlockSpec((1, kb), lambda s, k: (0, s)))
        out_shape.append(jax.ShapeDtypeStruct((1, N_CHIPS * kb), F32))
    return _hosted(
        body, comm, name=name, grid=(N_CHIPS, nk),
        in_specs=[_operand_spec(a, a_mode, tn, ka), _operand_spec(b, b_mode, tn, kb)],
        out_specs=out_specs, out_shape=out_shape,
        scratch_shapes=[pltpu.VMEM((ka, kb), F32)],
        compiler_params=_params(("arbitrary", "arbitrary")), args=[a, b])


def _in_proj(x1, w_in_g, b_in, comm=None):
    n = x1.shape[0]
    tm = min(ROW_TILE, n)

    def body(x_ref, w_ref, b_ref, o_ref):
        o_ref[...] = _mm(x_ref[...], w_ref[0]) + b_ref[...]

    return _hosted(
        body, comm, name="in_proj", grid=(N_CHIPS, n // tm),
        in_specs=[pl.BlockSpec((tm, D_MODEL), lambda j, i: (i, 0)),
                  pl.BlockSpec((1, D_MODEL, IN_SHARD), lambda j, i: (j, 0, 0)),
                  pl.BlockSpec((1, IN_SHARD), lambda j, i: (0, j))],
        out_specs=[pl.BlockSpec((tm, IN_SHARD), lambda j, i: (i, j))],
        out_shape=[jax.ShapeDtypeStruct((n, D_IN), F32)],
        scratch_shapes=[],
        compiler_params=_params(("arbitrary", "arbitrary")), args=[x1, w_in_g, b_in])


def _in_proj_dx(dproj, w_in_g, dz2):
    n = dproj.shape[0]
    tm = min(ROW_TILE, n)

    def body(dp_ref, w_ref, dz_ref, o_ref, acc):
        j = pl.program_id(1)

        @pl.when(j == 0)
        def _():
            acc[...] = jnp.zeros_like(acc)

        acc[...] += _mm_nt(dp_ref[...], w_ref[0])

        @pl.when(j == N_CHIPS - 1)
        def _():
            o_ref[...] = DEEPNORM_ALPHA * dz_ref[...] + acc[...]

    return pl.pallas_call(
        body, name="in_proj_dx", grid=(n // tm, N_CHIPS),
        in_specs=[pl.BlockSpec((tm, IN_SHARD), lambda i, j: (i, j)),
                  pl.BlockSpec((1, D_MODEL, IN_SHARD), lambda i, j: (j, 0, 0)),
                  pl.BlockSpec((tm, D_MODEL), lambda i, j: (i, 0))],
        out_specs=pl.BlockSpec((tm, D_MODEL), lambda i, j: (i, 0)),
        out_shape=jax.ShapeDtypeStruct((n, D_MODEL), F32),
        scratch_shapes=[pltpu.VMEM((tm, D_MODEL), F32)],
        compiler_params=_params(("arbitrary", "arbitrary")),
    )(dproj, w_in_g, dz2)


def _rope_tables(seq_len):
    pos = jnp.arange(seq_len, dtype=F32)
    inv_freq = ROPE_THETA ** (-jnp.arange(0, ROPE_DIM, 2, dtype=F32) / ROPE_DIM)
    ang = pos[:, None] * inv_freq[None, :]
    cos, sin = jnp.cos(ang), jnp.sin(ang)
    half = ROPE_DIM // 2
    rest = HEAD_DIM - ROPE_DIM
    ones = jnp.ones((seq_len, rest), F32)
    zeros = jnp.zeros((seq_len, rest), F32)
    zh = jnp.zeros((seq_len, half), F32)
    c = jnp.concatenate([cos, cos, ones], axis=1)
    sa = jnp.concatenate([-sin, zh, zeros], axis=1)
    sb = jnp.concatenate([zh, sin, zeros], axis=1)
    reps = LANES // HEAD_DIM
    return tuple(jnp.tile(t, (1, reps)) for t in (c, sa, sb))


def _rope(t, c, sa, sb):
    w = t.shape[1]
    reps = w // LANES
    half = ROPE_DIM // 2
    return (t * jnp.tile(c, (1, reps)) + pltpu.roll(t, w - half, 1) * jnp.tile(sa, (1, reps))
            + pltpu.roll(t, half, 1) * jnp.tile(sb, (1, reps)))


def _rope_transposed(g, c, sa, sb):
    w = g.shape[1]
    reps = w // LANES
    half = ROPE_DIM // 2
    return (g * jnp.tile(c, (1, reps)) + pltpu.roll(g * jnp.tile(sa, (1, reps)), half, 1)
            + pltpu.roll(g * jnp.tile(sb, (1, reps)), w - half, 1))


def _attn_mask(n):
    qi = lax.broadcasted_iota(jnp.int32, (ATTN_BLOCK, 2 * ATTN_BLOCK), 0)
    kj = lax.broadcasted_iota(jnp.int32, (ATTN_BLOCK, 2 * ATTN_BLOCK), 1)
    dist = qi + ATTN_BLOCK - kj
    return (dist >= 0) & (dist < ATTN_BLOCK) & (n * ATTN_BLOCK + kj - ATTN_BLOCK >= 0)


def _head_halves(t_pair):
    swapped = pltpu.roll(t_pair, HEAD_DIM, 1)
    return ((_bf(t_pair), _bf(swapped)), (_bf(swapped), _bf(t_pair)))


def _attn_probs(q_masked, k_sel, mask, sink):
    s = _mm_nt(q_masked, k_sel) * (HEAD_DIM ** -0.5)
    s = jnp.where(mask, s, NEG_INF)
    m = jnp.maximum(jnp.max(s, axis=-1, keepdims=True), sink)
    p = jnp.exp(s - m)
    e_sink = jnp.exp(sink - m)
    denom = jnp.sum(p, axis=-1, keepdims=True) + e_sink
    return p / denom, e_sink / denom


def _attn_fwd(proj, tabs, sinks):
    n_tok = proj.shape[0]
    nb = n_tok // ATTN_BLOCK
    group = N_Q_HEADS // N_KV_HEADS

    def body(q_ref, k_ref, v_ref, c_ref, sa_ref, sb_ref, sink_ref, y_ref, kprev, vprev):
        n = pl.program_id(0)

        @pl.when(n == 0)
        def _():
            kprev[...] = jnp.zeros_like(kprev)
            vprev[...] = jnp.zeros_like(vprev)

        c, sa, sb = c_ref[...], sa_ref[...], sb_ref[...]
        qr = _bf(_rope(q_ref[...], c, sa, sb))
        kr = _rope(k_ref[...], c, sa, sb)
        vc = v_ref[...]
        kk = jnp.concatenate([kprev[...], kr], axis=0)
        vv = jnp.concatenate([vprev[...], vc], axis=0)
        kprev[...] = kr
        vprev[...] = vc
        mask = _attn_mask(n)
        lo = lax.broadcasted_iota(jnp.int32, (ATTN_BLOCK, LANES), 1) < HEAD_DIM
        for r in range(N_KV_HEADS // 2):
            k_var = _head_halves(kk[:, r * LANES:(r + 1) * LANES])
            v_var = _head_halves(vv[:, r * LANES:(r + 1) * LANES])
            for kh2 in range(2):
                kh = 2 * r + kh2
                for gp in range(group // 2):
                    pair = (kh * group) // 2 + gp
                    q2 = qr[:, pair * LANES:(pair + 1) * LANES]
                    outs = []
                    for half in range(2):
                        h = 2 * pair + half
                        qm = jnp.where(lo if half == 0 else ~lo, q2, jnp.zeros_like(q2))
                        probs, _ = _attn_probs(qm, k_var[kh2][half], mask, sink_ref[0, h])
                        outs.append(_mm(_bf(probs), v_var[kh2][half]))
                    y_ref[:, pair * LANES:(pair + 1) * LANES] = _bf(jnp.where(lo, outs[0], outs[1]))

    blk = lambda width, col: pl.BlockSpec((ATTN_BLOCK, width), lambda n: (n, col))
    tab = pl.BlockSpec((ATTN_BLOCK, LANES), lambda n: (n, 0))
    kvw = N_KV_HEADS * HEAD_DIM
    return pl.pallas_call(
        body, name="attn_fwd", grid=(nb,),
        in_specs=[blk(D_MODEL, 0), blk(kvw, COL_K), blk(kvw, COL_V), tab, tab, tab,
                  pl.BlockSpec(memory_space=pltpu.SMEM)],
        out_specs=pl.BlockSpec((ATTN_BLOCK, D_MODEL), lambda n: (n, 0)),
        out_shape=jax.ShapeDtypeStruct((n_tok, D_MODEL), BF16),
        scratch_shapes=[pltpu.VMEM((ATTN_BLOCK, kvw), F32), pltpu.VMEM((ATTN_BLOCK, kvw), F32)],
        compiler_params=_params(("arbitrary",)),
    )(proj, proj, proj, *tabs, sinks)


def _attn_bwd(proj, dy, tabs, sinks, comm=None):
    n_tok = proj.shape[0]
    nb = n_tok // ATTN_BLOCK
    group = N_Q_HEADS // N_KV_HEADS
    kvw = N_KV_HEADS * HEAD_DIM

    def body(q_ref, k_ref, v_ref, do_ref, c_ref, sa_ref, sb_ref, cp_ref, sap_ref, sbp_ref, sink_ref,
             dq_ref, dkv_ref, dsink_ref, kprev, vprev, dkc, dvc):
        n = pl.program_id(0)

        @pl.when(n == 0)
        def _():
            for ref in (kprev, vprev, dkc, dvc, dsink_ref):
                ref[...] = jnp.zeros_like(ref)

        prev_tabs = (cp_ref[...], sap_ref[...], sbp_ref[...])

        @pl.when(n < nb)
        def _():
            c, sa, sb = c_ref[...], sa_ref[...], sb_ref[...]
            qr = _bf(_rope(q_ref[...], c, sa, sb))
            kr = _rope(k_ref[...], c, sa, sb)
            vc = v_ref[...]
            kk = jnp.concatenate([kprev[...], kr], axis=0)
            vv = jnp.concatenate([vprev[...], vc], axis=0)
            kprev[...] = kr
            vprev[...] = vc
            mask = _attn_mask(n)
            lane = lax.broadcasted_iota(jnp.int32, (ATTN_BLOCK, LANES), 1)
            lo = lane < HEAD_DIM
            lo2 = lax.broadcasted_iota(jnp.int32, (2 * ATTN_BLOCK, LANES), 1) < HEAD_DIM
            dsink = jnp.zeros((1, LANES), F32)
            dq_pairs = []
            dk_pairs = []
            dv_pairs = []
            for r in range(N_KV_HEADS // 2):
                k_var = _head_halves(kk[:, r * LANES:(r + 1) * LANES])
                v_var = _head_halves(vv[:, r * LANES:(r + 1) * LANES])
                dk_full = []
                dv_full = []
                for kh2 in range(2):
                    kh = 2 * r + kh2
                    dk_acc = jnp.zeros((2 * ATTN_BLOCK, LANES), F32)
                    dv_acc = jnp.zeros((2 * ATTN_BLOCK, LANES), F32)
                    for gp in range(group // 2):
                        pair = (kh * group) // 2 + gp
                        q2 = qr[:, pair * LANES:(pair + 1) * LANES]
                        do2 = do_ref[:, pair * LANES:(pair + 1) * LANES]
                        dqs = []
                        for half in range(2):
                            h = 2 * pair + half
                            sel = lo if half == 0 else ~lo
                            qm = jnp.where(sel, q2, jnp.zeros_like(q2))
                            dom = _bf(jnp.where(sel, do2, 0.0))
                            k_sel = k_var[kh2][half]
                            v_sel = v_var[kh2][half]
                            probs, p_sink = _attn_probs(qm, k_sel, mask, sink_ref[0, h])
                            dp = _mm_nt(dom, v_sel)
                            delta = jnp.sum(probs * dp, axis=-1, keepdims=True)
                            ds = _bf(probs * (dp - delta) * (HEAD_DIM ** -0.5))
                            dsink = dsink + jnp.where(lane[:1] == h, -jnp.sum(p_sink * delta), 0.0)
                            dqs.append(_mm(ds, k_sel))
                            dk_acc = dk_acc + _mm_tn(ds, qm)
                            dv_acc = dv_acc + _mm_tn(_bf(probs), dom)
                        dq_pairs.append(jnp.where(lo, dqs[0], dqs[1]))
                    dk_full.append(dk_acc + pltpu.roll(dk_acc, HEAD_DIM, 1))
                    dv_full.append(dv_acc + pltpu.roll(dv_acc, HEAD_DIM, 1))
                dk_pairs.append(jnp.where(lo2, dk_full[0], dk_full[1]))
                dv_pairs.append(jnp.where(lo2, dv_full[0], dv_full[1]))
            dsink_ref[...] += dsink
            dq_ref[...] = _bf(_rope_transposed(jnp.concatenate(dq_pairs, axis=1), c, sa, sb))
            dk_all = jnp.concatenate(dk_pairs, axis=1)
            dv_all = jnp.concatenate(dv_pairs, axis=1)
            dkv_ref[:, :kvw] = _bf(_rope_transposed(dkc[...] + dk_all[:ATTN_BLOCK], *prev_tabs))
            dkv_ref[:, kvw:] = _bf(dvc[...] + dv_all[:ATTN_BLOCK])
            dkc[...] = dk_all[ATTN_BLOCK:]
            dvc[...] = dv_all[ATTN_BLOCK:]

        @pl.when(n == nb)
        def _():
            dkv_ref[:, :kvw] = _bf(_rope_transposed(dkc[...], *prev_tabs))
            dkv_ref[:, kvw:] = _bf(dvc[...])

    cur = lambda n: jnp.minimum(n, nb - 1)
    prev = lambda n: jnp.maximum(n - 1, 0)
    blk = lambda width, col: pl.BlockSpec((ATTN_BLOCK, width), lambda n: (cur(n), col))
    tab = pl.BlockSpec((ATTN_BLOCK, LANES), lambda n: (cur(n), 0))
    tabp = pl.BlockSpec((ATTN_BLOCK, LANES), lambda n: (prev(n), 0))
    return _hosted(
        body, comm, name="attn_bwd", grid=(nb + 1,),
        in_specs=[blk(D_MODEL, 0), blk(kvw, COL_K), blk(kvw, COL_V), blk(D_MODEL, 0), tab, tab, tab, tabp, tabp, tabp,
                  pl.BlockSpec(memory_space=pltpu.SMEM)],
        out_specs=[pl.BlockSpec((ATTN_BLOCK, D_MODEL), lambda n: (cur(n), 0)),
                   pl.BlockSpec((ATTN_BLOCK, 2 * kvw), lambda n: (prev(n), 0)),
                   pl.BlockSpec((1, LANES), lambda n: (0, 0))],
        out_shape=[jax.ShapeDtypeStruct((n_tok, D_MODEL), BF16), jax.ShapeDtypeStruct((n_tok, 2 * kvw), BF16),
                   jax.ShapeDtypeStruct((1, LANES), F32)],
        scratch_shapes=[pltpu.VMEM((ATTN_BLOCK, kvw), F32)] * 4,
        compiler_params=_params(("arbitrary",)), args=[proj, proj, proj, dy, *tabs, *tabs, sinks])


def _bmm(a, b):
    return lax.dot_general(a, b, (((2,), (1,)), ((0,), (0,))), preferred_element_type=F32)


def _bmm_nt(a, b):
    return lax.dot_general(a, b, (((2,), (2,)), ((0,), (0,))), preferred_element_type=F32)


def _bmm_tn(a, b):
    return lax.dot_general(a, b, (((1,), (1,)), ((0,), (0,))), preferred_element_type=F32)


def _tril(cb, upper=False):
    shape = (cb, HGRN_CHUNK, HGRN_CHUNK)
    r, c = lax.broadcasted_iota(jnp.int32, shape, 1), lax.broadcasted_iota(jnp.int32, shape, 2)
    return (r <= c) if upper else (r >= c)


def _tri_matmul(x, upper):
    return lax.dot_general(_tril(x.shape[0], upper).astype(F32), x, (((2,), (1,)), ((0,), (0,))),
                           precision=lax.Precision.HIGHEST, preferred_element_type=F32)


@jax.custom_vjp
def _chunk_cumsum(x):
    return _tri_matmul(x, False)


_chunk_cumsum.defvjp(lambda x: (_tri_matmul(x, False), None), lambda _, g: (_tri_matmul(g, True),))


def _hg_elem(fl, qh, lb):
    f = lb + (1.0 - lb) * _sig(fl)
    k = 1.0 - f
    gc = _chunk_cumsum(jnp.log(f))
    last = lax.broadcasted_iota(jnp.int32, gc.shape, 1) == HGRN_CHUNK - 1
    g_last = jnp.sum(jnp.where(last, gc, 0.0), axis=1, keepdims=True)
    q = qh * _sig(qh)
    return q * jnp.exp(gc), k * jnp.exp(-gc), k * jnp.exp(g_last - gc), jnp.exp(g_last)


def _hg_out(q_dec, k_inv, v, st):
    sc = jnp.where(_tril(q_dec.shape[0]), _bmm_nt(_bf(q_dec), _bf(k_inv)), 0.0)
    return _bmm(_bf(sc), _bf(v)) + _bmm_nt(_bf(q_dec), _bf(st)), sc


def _hg_post(o, og, ng):
    on = o * lax.rsqrt(jnp.mean(o * o, axis=-1, keepdims=True) + RMS_EPS) * ng
    return on * (og * _sig(og))


def _hgrn_specs(n_tok, rev):
    nc = n_tok // HGRN_CHUNK
    cb = min(HGRN_CHUNKS_PER_STEP, nc)
    nt = nc // cb
    rows = cb * HGRN_CHUNK
    tt = (lambda t: nt - 1 - t) if rev else (lambda t: t)
    col = lambda base: pl.BlockSpec((rows, LANES), lambda h, t: (tt(t), base + h))
    head_vec = pl.BlockSpec((1, LANES), lambda h, t: (0, h))
    one_vec = pl.BlockSpec((1, LANES), lambda h, t: (0, 0))
    state = pl.BlockSpec((1, cb, HGRN_DK, HGRN_DK), lambda h, t: (h, tt(t), 0, 0))
    return nc, cb, nt, col, head_vec, one_vec, state


def _hgrn_fwd(proj, lb, ng):
    n_tok = proj.shape[0]
    nc, cb, nt, col, head_vec, one_vec, state = _hgrn_specs(n_tok, False)

    def body(fl_ref, qh_ref, ih_ref, og_ref, lb_ref, ng_ref, y_ref, st_ref, s_acc):
        @pl.when(pl.program_id(1) == 0)
        def _():
            s_acc[...] = jnp.zeros_like(s_acc)

        chunks = lambda ref: ref[...].reshape(cb, HGRN_CHUNK, LANES)
        q_dec, k_inv, k_end, decay = _hg_elem(chunks(fl_ref), chunks(qh_ref), lb_ref[...])
        v = chunks(ih_ref)
        upd = _bmm_tn(_bf(v), _bf(k_end))
        st = s_acc[...]
        for ci in range(cb):
            st_ref[0, ci] = st
            st = st * decay[ci] + upd[ci]
        s_acc[...] = st
        o, _ = _hg_out(q_dec, k_inv, v, st_ref[0])
        y_ref[...] = _bf(_hg_post(o, chunks(og_ref), ng_ref[...]).reshape(cb * HGRN_CHUNK, LANES))

    return pl.pallas_call(
        body, name="hgrn_fwd", grid=(HGRN_HEADS, nt),
        in_specs=[col(COL_F), col(COL_QH), col(COL_IH), col(COL_OG), head_vec, one_vec],
        out_specs=[col(0), state],
        out_shape=[jax.ShapeDtypeStruct((n_tok, D_MODEL), BF16),
                   jax.ShapeDtypeStruct((HGRN_HEADS, nc, HGRN_DK, HGRN_DK), F32)],
        scratch_shapes=[pltpu.VMEM((HGRN_DK, HGRN_DK), F32)],
        compiler_params=_params(("arbitrary", "arbitrary")),
    )(proj, proj, proj, proj, lb, ng)


def _hgrn_bwd(proj, lb, ng, states, dy):
    n_tok = proj.shape[0]
    nc, cb, nt, col, head_vec, one_vec, state = _hgrn_specs(n_tok, True)

    def body(fl_ref, qh_ref, ih_ref, og_ref, lb_ref, ng_ref, st_ref, dy_ref,
             dfl_ref, dqh_ref, dih_ref, dog_ref, dlb_ref, dng_ref, g_acc, g_all):
        h = pl.program_id(0)
        t = pl.program_id(1)

        @pl.when(t == 0)
        def _():
            g_acc[...] = jnp.zeros_like(g_acc)
            dlb_ref[...] = jnp.zeros_like(dlb_ref)

        @pl.when((t == 0) & (h == 0))
        def _():
            dng_ref[...] = jnp.zeros_like(dng_ref)

        chunks = lambda ref: ref[...].reshape(cb, HGRN_CHUNK, LANES)
        flat = lambda val: _bf(val.reshape(cb * HGRN_CHUNK, LANES))
        (q_dec, k_inv, k_end, decay), elem_vjp = jax.vjp(_hg_elem, chunks(fl_ref), chunks(qh_ref), lb_ref[...])
        v = chunks(ih_ref)
        st = st_ref[0]
        o, sc = _hg_out(q_dec, k_inv, v, st)
        _, post_vjp = jax.vjp(_hg_post, o, chunks(og_ref), ng_ref[...])
        do, dog, dng = post_vjp(chunks(dy_ref))
        dob, vb, qb = _bf(do), _bf(v), _bf(q_dec)
        dsc = _bf(jnp.where(_tril(cb), _bmm_nt(dob, vb), 0.0))
        p = _bmm_tn(dob, qb)
        g = g_acc[...]
        for ci in reversed(range(cb)):
            g_all[ci] = g
            g = g * decay[ci] + p[ci]
        g_acc[...] = g
        g = g_all[...]
        gb = _bf(g)
        dq_dec = _bmm(dsc, _bf(k_inv)) + _bmm(dob, _bf(st))
        dk_inv = _bmm_tn(dsc, qb)
        dv = _bmm_tn(_bf(sc), dob) + _bmm_nt(_bf(k_end), gb)
        dk_end = _bmm(vb, gb)
        ddecay = jnp.sum(st * g, axis=1, keepdims=True)
        dfl, dqh, dlb = elem_vjp((dq_dec, dk_inv, dk_end, ddecay))
        dfl_ref[...] = flat(dfl)
        dqh_ref[...] = flat(dqh)
        dih_ref[...] = flat(dv)
        dog_ref[...] = flat(dog)
        dlb_ref[...] += dlb
        dng_ref[...] += dng

    out_col = jax.ShapeDtypeStruct((n_tok, D_MODEL), BF16)
    return pl.pallas_call(
        body, name="hgrn_bwd", grid=(HGRN_HEADS, nt),
        in_specs=[col(COL_F), col(COL_QH), col(COL_IH), col(COL_OG), head_vec, one_vec, state, col(0)],
        out_specs=[col(0), col(0), col(0), col(0), head_vec, one_vec],
        out_shape=[out_col, out_col, out_col, out_col,
                   jax.ShapeDtypeStruct((1, D_MODEL), F32), jax.ShapeDtypeStruct((1, LANES), F32)],
        scratch_shapes=[pltpu.VMEM((HGRN_DK, HGRN_DK), F32), pltpu.VMEM((cb, HGRN_DK, HGRN_DK), F32)],
        compiler_params=_params(("arbitrary", "arbitrary")),
    )(proj, proj, proj, proj, lb, ng, states, dy)


def _lb_fwd(lb_logits):
    def lb_of(l0, l1):
        m = jnp.maximum(l0, l1)
        e0, e1 = jnp.exp(l0 - m), jnp.exp(l1 - m)
        return e0 / (e0 + e1)

    def body(l_ref, o_ref):
        o_ref[...] = lb_of(l_ref[0:1, :], l_ref[1:2, :])

    lb = pl.pallas_call(body, name="lb_fwd", out_shape=jax.ShapeDtypeStruct((1, D_MODEL), F32))(lb_logits)
    return lb, lb_of


def _gate_specs(tm):
    return [pl.BlockSpec((tm, 512), lambda i, c=c: (i, c)) for c in (COL_GA, COL_GA + 1, COL_GH, COL_GH + 1)]


def _mix_fwd(y_attn, y_hgrn, proj, x1, w_pa, w_ph, w_out, g, b):
    n = x1.shape[0]
    tm = min(MIX_TILE, n)

    def body(ya_ref, yh_ref, ga0, ga1, gh0, gh1, x_ref, wpa, wph, wo, g_ref, b_ref, z_ref, o_ref):
        ya = _mm(ya_ref[...], wpa[...])
        yh = _mm(yh_ref[...], wph[...])
        ga = jnp.concatenate([ga0[...], ga1[...]], axis=1)
        gh = jnp.concatenate([gh0[...], gh1[...]], axis=1)
        merged = _sig(ga) * ya + _sig(gh) * yh
        z = DEEPNORM_ALPHA * x_ref[...] + _mm(_bf(merged), wo[...])
        z_ref[...] = z
        o_ref[...] = _ln(z, g_ref[...], b_ref[...])

    row = pl.BlockSpec((tm, D_MODEL), lambda i: (i, 0))
    sq = _full((D_MODEL, D_MODEL))
    vec = _full((1, D_MODEL))
    return pl.pallas_call(
        body, name="mix_fwd", grid=(n // tm,),
        in_specs=[row, row, *_gate_specs(tm), row, sq, sq, sq, vec, vec],
        out_specs=[row, row], out_shape=[jax.ShapeDtypeStruct((n, D_MODEL), F32)] * 2,
        compiler_params=_params(("arbitrary",)),
    )(y_attn, y_hgrn, proj, proj, proj, proj, x1, w_pa, w_ph, w_out, g, b)


def _mix_bwd(dx2, z2, y_attn, y_hgrn, proj, w_pa, w_ph, w_out, g, b):
    n = z2.shape[0]
    tm = min(MIX_TILE, n)

    def body(do_ref, z_ref, ya_ref, yh_ref, ga0, ga1, gh0, gh1, wpa, wph, wo, g_ref, b_ref,
             dz_ref, dzb_ref, mg_ref, dya_ref, dyh_ref, dyat_ref, dyhg_ref, dgt_ref, dg_ref, db_ref):
        _, vjp = jax.vjp(_ln, z_ref[...], g_ref[...], b_ref[...])
        dz, dg, db = vjp(do_ref[...])

        @pl.when(pl.program_id(0) == 0)
        def _():
            dg_ref[...] = jnp.zeros_like(dg_ref)
            db_ref[...] = jnp.zeros_like(db_ref)

        dg_ref[...] += dg
        db_ref[...] += db
        dz_ref[...] = dz
        ya = _mm(ya_ref[...], wpa[...])
        yh = _mm(yh_ref[...], wph[...])
        ga = jnp.concatenate([ga0[...], ga1[...]], axis=1)
        gh = jnp.concatenate([gh0[...], gh1[...]], axis=1)

        def merge(ga, gh, ya, yh):
            return _sig(ga) * ya + _sig(gh) * yh

        merged, merge_vjp = jax.vjp(merge, ga, gh, ya, yh)
        mg_ref[...] = _bf(merged)
        dzb = _bf(dz)
        dzb_ref[...] = dzb
        dmerged = _mm_nt(dzb, wo[...])
        dga, dgh, dya, dyh = merge_vjp(dmerged)
        dya = _bf(dya)
        dyh = _bf(dyh)
        dya_ref[...] = dya
        dyh_ref[...] = dyh
        dgt_ref[:, :D_MODEL] = _bf(dga)
        dgt_ref[:, D_MODEL:] = _bf(dgh)
        dyat_ref[...] = _mm_nt(dya, wpa[...])
        dyhg_ref[...] = _mm_nt(dyh, wph[...])

    row = pl.BlockSpec((tm, D_MODEL), lambda i: (i, 0))
    row2 = pl.BlockSpec((tm, 2 * D_MODEL), lambda i: (i, 0))
    sq = _full((D_MODEL, D_MODEL))
    vec = _full((1, D_MODEL))
    f32_row = jax.ShapeDtypeStruct((n, D_MODEL), F32)
    bf_row = jax.ShapeDtypeStruct((n, D_MODEL), BF16)
    vec_shape = jax.ShapeDtypeStruct((1, D_MODEL), F32)
    return pl.pallas_call(
        body, name="mix_bwd", grid=(n // tm,),
        in_specs=[row, row, row, row, *_gate_specs(tm), sq, sq, sq, vec, vec],
        out_specs=[row, row, row, row, row, row, row, row2, vec, vec],
        out_shape=[f32_row, bf_row, bf_row, bf_row, bf_row, f32_row, f32_row,
                   jax.ShapeDtypeStruct((n, 2 * D_MODEL), BF16), vec_shape, vec_shape],
        compiler_params=_params(("arbitrary",)),
    )(dx2, z2, y_attn, y_hgrn, proj, proj, proj, proj, w_pa, w_ph, w_out, g, b)


def _position():
    x, y, c = lax.axis_index("x"), lax.axis_index("y"), lax.axis_index("c")
    chips = [(1 - x, y), (x, 1 - y), (1 - x, 1 - y)]
    return x, y, c, chips


def _any_specs(k):
    return [pl.BlockSpec(memory_space=pl.ANY)] * k


class _GatherWeights:
    def __init__(self, shards):
        nw = len(shards)
        self.inputs = list(shards)
        self.out_shape = [jax.ShapeDtypeStruct((N_CHIPS, *s.shape), s.dtype) for s in shards]
        self.scratch = [pltpu.SemaphoreType.DMA((nw,)), pltpu.SemaphoreType.DMA((nw * 6,)),
                        pltpu.SemaphoreType.DMA((nw * 6,))]

    def _copies(self, ins, outs, sems):
        nw = len(ins)
        local_sem, send_sem, recv_sem = sems
        x, y, c, chips = _position()
        me = 2 * x + y
        sibling = (x, y, 1 - c)
        half_rows = [s.shape[0] // 2 for s in self.inputs]

        def half(w, chip_idx, which):
            return outs[w].at[chip_idx, pl.ds(which * half_rows[w], half_rows[w])]

        def remote(w, k, src, dst, to):
            return pltpu.make_async_remote_copy(src_ref=src, dst_ref=dst, send_sem=send_sem.at[w * 6 + k],
                                                recv_sem=recv_sem.at[w * 6 + k], device_id=to, device_id_type=MESH)

        local = [pltpu.make_async_copy(ins[w], outs[w].at[me], local_sem.at[w]) for w in range(nw)]
        first = [remote(w, j, ins[w].at[pl.ds(c * half_rows[w], half_rows[w])], half(w, me, c), (px, py, c))
                 for w in range(nw) for j, (px, py) in enumerate(chips)]
        landed = [half(w, 2 * px + py, c) for w in range(nw) for (px, py) in chips]
        arrive = [remote(w, j, landed[w * 3 + j], landed[w * 3 + j], (px, py, c))
                  for w in range(nw) for j, (px, py) in enumerate(chips)]
        passed = [remote(w, 3 + j, landed[w * 3 + j], landed[w * 3 + j], sibling) for w in range(nw) for j in range(3)]
        from_sibling = [remote(w, 3 + j, half(w, 2 * px + py, 1 - c), half(w, 2 * px + py, 1 - c), sibling)
                        for w in range(nw) for j, (px, py) in enumerate(chips)]
        return local, first, arrive, passed, from_sibling

    def start(self, ins, outs, sems):
        local, first, _, _, _ = self._copies(ins, outs, sems)
        for cp in local + first:
            cp.start()

    def finish(self, ins, outs, sems):
        local, first, arrive, passed, from_sibling = self._copies(ins, outs, sems)
        for cp_in, cp_on in zip(arrive, passed):
            cp_in.wait_recv()
            cp_on.start()
        for cp in from_sibling:
            cp.wait_recv()
        for cp in first + passed:
            cp.wait_send()
        for cp in local:
            cp.wait()


class _ExchangeGrads:
    def __init__(self, grads):
        nw = len(grads)
        self.inputs = list(grads)
        self.out_shape = [jax.ShapeDtypeStruct(g.shape, g.dtype) for g in grads]
        self.scratch = [pltpu.SemaphoreType.DMA((nw,)), pltpu.SemaphoreType.DMA((nw * 3,)),
                        pltpu.SemaphoreType.DMA((nw * 3,))]

    def _copies(self, ins, outs, sems):
        nw = len(ins)
        local_sem, send_sem, recv_sem = sems
        x, y, c, chips = _position()
        me = 2 * x + y

        def remote(w, j, src, dst, chip):
            return pltpu.make_async_remote_copy(src_ref=src, dst_ref=dst, send_sem=send_sem.at[w * 3 + j],
                                                recv_sem=recv_sem.at[w * 3 + j], device_id=(*chip, c),
                                                device_id_type=MESH)

        local = [pltpu.make_async_copy(ins[w].at[me], outs[w].at[me], local_sem.at[w]) for w in range(nw)]
        sends = [remote(w, j, ins[w].at[2 * px + py], outs[w].at[me], (px, py))
                 for w in range(nw) for j, (px, py) in enumerate(chips)]
        arrive = [remote(w, j, outs[w].at[2 * px + py], outs[w].at[2 * px + py], (px, py))
                  for w in range(nw) for j, (px, py) in enumerate(chips)]
        return local, sends, arrive

    def start(self, ins, outs, sems):
        local, sends, _ = self._copies(ins, outs, sems)
        for cp in local + sends:
            cp.start()

    def finish(self, ins, outs, sems):
        local, sends, arrive = self._copies(ins, outs, sems)
        for cp in arrive:
            cp.wait_recv()
        for cp in sends:
            cp.wait_send()
        for cp in local:
            cp.wait()


def _hosted(body, comm, *, name, grid, in_specs, out_specs, out_shape, scratch_shapes, compiler_params, args):
    if comm is None:
        res = pl.pallas_call(body, name=name, grid=grid, in_specs=in_specs, out_specs=out_specs, out_shape=out_shape,
                             scratch_shapes=scratch_shapes, compiler_params=compiler_params)(*args)
        return list(res), []
    n_in, n_out, n_scr = len(in_specs), len(out_specs), len(scratch_shapes)
    c_in, c_out = len(comm.inputs), len(comm.out_shape)

    def hosted_body(*refs):
        refs = list(refs)
        cut = lambda k: (refs[:k], refs[k:])
        main_in, refs = cut(n_in)
        comm_in, refs = cut(c_in)
        main_out, refs = cut(n_out)
        comm_out, refs = cut(c_out)
        main_scr, comm_scr = cut(n_scr)
        ids = [pl.program_id(a) for a in range(len(grid))]
        first = functools.reduce(jnp.logical_and, [i == 0 for i in ids])
        last = functools.reduce(jnp.logical_and, [i == g - 1 for i, g in zip(ids, grid)])

        @pl.when(first)
        def _():
            comm.start(comm_in, comm_out, comm_scr)

        body(*main_in, *main_out, *main_scr)

        @pl.when(last)
        def _():
            comm.finish(comm_in, comm_out, comm_scr)

    res = pl.pallas_call(
        hosted_body, name=name, grid=grid, in_specs=[*in_specs, *_any_specs(c_in)],
        out_specs=[*out_specs, *_any_specs(c_out)], out_shape=[*out_shape, *comm.out_shape],
        scratch_shapes=[*scratch_shapes, *comm.scratch], compiler_params=compiler_params,
    )(*args, *comm.inputs)
    return list(res[:n_out]), list(res[n_out:])


def _run_comm(name, comm):
    def body(*refs):
        refs = list(refs)
        c_in, c_out = len(comm.inputs), len(comm.out_shape)
        ins, outs, sems = refs[:c_in], refs[c_in:c_in + c_out], refs[c_in + c_out:]
        comm.start(ins, outs, sems)
        comm.finish(ins, outs, sems)

    return list(pl.pallas_call(
        body, name=name, in_specs=_any_specs(len(comm.inputs)), out_specs=_any_specs(len(comm.out_shape)),
        out_shape=comm.out_shape, scratch_shapes=comm.scratch,
    )(*comm.inputs))


def _sum_slots(name, slots):
    _, rows, cols = slots.shape
    tr = _update_rows(rows)

    def body(s_ref, o_ref):
        acc = s_ref[0].astype(F32)
        for i in range(1, N_CHIPS):
            acc = acc + s_ref[i].astype(F32)
        o_ref[...] = acc

    return pl.pallas_call(
        body, name=name, grid=(rows // tr,),
        in_specs=[pl.BlockSpec((N_CHIPS, tr, cols), lambda i: (0, i, 0))],
        out_specs=pl.BlockSpec((tr, cols), lambda i: (i, 0)),
        out_shape=jax.ShapeDtypeStruct((rows, cols), F32),
        compiler_params=_params(("arbitrary",)),
    )(slots)


def _swap_with_sibling(parts):
    nw = len(parts)

    def body(*refs):
        ins, outs = refs[:nw], refs[nw:2 * nw]
        send_sem, recv_sem = refs[2 * nw:]
        x, y, c, _ = _position()
        copies = [pltpu.make_async_remote_copy(src_ref=ins[w], dst_ref=outs[w], send_sem=send_sem.at[w],
                                               recv_sem=recv_sem.at[w], device_id=(x, y, 1 - c), device_id_type=MESH)
                  for w in range(nw)]
        for cp in copies:
            cp.start()
        for cp in copies:
            cp.wait()

    return pl.pallas_call(
        body, name="swap_with_sibling",
        in_specs=_any_specs(nw), out_specs=_any_specs(nw),
        out_shape=[jax.ShapeDtypeStruct(p.shape, p.dtype) for p in parts],
        scratch_shapes=[pltpu.SemaphoreType.DMA((nw,)), pltpu.SemaphoreType.DMA((nw,))],
    )(*parts)


def _sum_small(part):
    def body(p_ref, o_ref, buf, send_sem, recv_sem):
        x, y, c, _ = _position()
        me = 4 * x + 2 * y + c
        buf[me] = p_ref[...]
        copies = []
        for k in range(1, N_DEV):
            peer = tuple(1 - v if (k >> s) & 1 else v for v, s in ((x, 2), (y, 1), (c, 0)))
            copies.append(pltpu.make_async_remote_copy(src_ref=p_ref, dst_ref=buf.at[me], send_sem=send_sem.at[k - 1],
                                                       recv_sem=recv_sem.at[k - 1], device_id=peer, device_id_type=MESH))
        for cp in copies:
            cp.start()
        for cp in copies:
            cp.wait()
        acc = buf[0]
        for d in range(1, N_DEV):
            acc = acc + buf[d]
        o_ref[...] = acc

    vm = pl.BlockSpec(memory_space=pltpu.VMEM)
    return pl.pallas_call(
        body, name="sum_small", in_specs=[vm], out_specs=vm,
        out_shape=jax.ShapeDtypeStruct((1, SM_LEN), F32),
        scratch_shapes=[pltpu.VMEM((N_DEV, 1, SM_LEN), F32), pltpu.SemaphoreType.DMA((N_DEV - 1,)),
                        pltpu.SemaphoreType.DMA((N_DEV - 1,))],
    )(part)


def _adamw(w, g, m, v):
    m = ADAM_B1 * m + (1.0 - ADAM_B1) * g
    v = ADAM_B2 * v + (1.0 - ADAM_B2) * (g * g)
    m_hat = m / (1.0 - ADAM_B1 ** ADAM_STEP)
    v_hat = v / (1.0 - ADAM_B2 ** ADAM_STEP)
    delta = -ADAM_LR * (m_hat / (jnp.sqrt(v_hat) + ADAM_EPS) + ADAM_WD * w)
    return delta, m, v


def _adam_big(name, p_own, p_sibling, w, m, v):
    rows, cols = w.shape
    tr = _update_rows(rows)

    def body(p_ref, q_ref, w_ref, m_ref, v_ref, g_ref, d_ref, nm_ref, nv_ref):
        g = p_ref[...] + q_ref[...]
        g_ref[...] = g
        d_ref[...], nm_ref[...], nv_ref[...] = _adamw(w_ref[...], g, m_ref[...], v_ref[...])

    spec = pl.BlockSpec((tr, cols), lambda i: (i, 0))
    return pl.pallas_call(
        body, name=name, grid=(rows // tr,), in_specs=[spec] * 5, out_specs=[spec] * 4,
        out_shape=[jax.ShapeDtypeStruct((rows, cols), F32)] * 4,
        compiler_params=_params(("arbitrary",)),
    )(p_own, p_sibling, w, m, v)


def _adam_small(total, wvec, mvec, vvec, lb_of):
    def body(t_ref, w_ref, m_ref, v_ref, g_ref, d_ref, nm_ref, nv_ref):
        g_ref[:, :SM_LB] = t_ref[:, :SM_LB]
        _, vjp = jax.vjp(lb_of, w_ref[:, PK_LB0:PK_LB1], w_ref[:, PK_LB1:PK_LEN])
        dl0, dl1 = vjp(t_ref[:, SM_LB:SM_LOSS])
        g_ref[:, PK_LB0:PK_LB1] = dl0
        g_ref[:, PK_LB1:PK_LEN] = dl1
        d_ref[...], nm_ref[...], nv_ref[...] = _adamw(w_ref[...], g_ref[...], m_ref[...], v_ref[...])

    return pl.pallas_call(body, name="adam_small", out_shape=[jax.ShapeDtypeStruct((1, PK_LEN), F32)] * 4)(
        total, wvec, mvec, vvec)


_BIG = ("ffn1_w1", "ffn1_w3", "ffn1_w2", "w_in", "w_proj_attn", "w_proj_hgrn", "w_out", "ffn2_w1", "ffn2_w3", "ffn2_w2")
_SMALL = ("ln1_g", "ln1_b", "ln2_g", "ln2_b", "ln3_g", "ln3_b", "b_in", "attn_sinks", "hgrn_norm_g", "hgrn_lb_logits")
_ORDER = ("ln1_g", "ln1_b", "ffn1_w1", "ffn1_w3", "ffn1_w2", "ln2_g", "ln2_b", "w_in", "b_in", "attn_sinks",
          "hgrn_lb_logits", "hgrn_norm_g", "w_proj_attn", "w_proj_hgrn", "w_out", "ln3_g", "ln3_b",
          "ffn2_w1", "ffn2_w3", "ffn2_w2")


def _pad_to(a, rows, cols):
    return jnp.pad(a, ((0, rows - a.shape[0]), (0, cols - a.shape[1])))


def _send_form(name, w):
    wb = _bf(w)
    if name.endswith(("w1", "w3")) and name.startswith("ffn"):
        return _pad_to(wb, D_MODEL, FF_PAD)
    if name.endswith("w2") and name.startswith("ffn"):
        return _pad_to(wb, FF_PAD, D_MODEL)
    return wb


def _pack_small(p):
    pieces = [p[k].reshape(1, -1) for k in ("ln1_g", "ln1_b", "ln2_g", "ln2_b", "ln3_g", "ln3_b", "b_in")]
    pieces.append(_pad_to(p["attn_sinks"].reshape(1, -1), 1, LANES))
    pieces.append(p["hgrn_norm_g"].reshape(1, -1))
    pieces.append(p["hgrn_lb_logits"].reshape(1, -1))
    return jnp.concatenate(pieces, axis=1)


def _unpack_small(vec, like):
    out = {}
    for i, k in enumerate(("ln1_g", "ln1_b", "ln2_g", "ln2_b", "ln3_g", "ln3_b")):
        out[k] = vec[:, i * D_MODEL:(i + 1) * D_MODEL]
    out["b_in"] = vec[:, SM_BIN:SM_SINK]
    out["attn_sinks"] = vec[:, SM_SINK:SM_SINK + N_Q_HEADS]
    out["hgrn_norm_g"] = vec[:, SM_NG:SM_LB]
    out["hgrn_lb_logits"] = vec[:, PK_LB0:PK_LEN].reshape(2, D_MODEL)
    return {k: v.reshape(like[k].shape) for k, v in out.items()}


def kernel(x, ln1_g, ln1_b, ffn1_w1, ffn1_w3, ffn1_w2, ln2_g, ln2_b, w_in, b_in, attn_sinks, hgrn_lb_logits, hgrn_norm_g, w_proj_attn, w_proj_hgrn, w_out, ln3_g, ln3_b, ffn2_w1, ffn2_w3, ffn2_w2, loss_target, m_ln1_g, m_ln1_b, m_ffn1_w1, m_ffn1_w3, m_ffn1_w2, m_ln2_g, m_ln2_b, m_w_in, m_b_in, m_attn_sinks, m_hgrn_lb_logits, m_hgrn_norm_g, m_w_proj_attn, m_w_proj_hgrn, m_w_out, m_ln3_g, m_ln3_b, m_ffn2_w1, m_ffn2_w3, m_ffn2_w2, v_ln1_g, v_ln1_b, v_ffn1_w1, v_ffn1_w3, v_ffn1_w2, v_ln2_g, v_ln2_b, v_w_in, v_b_in, v_attn_sinks, v_hgrn_lb_logits, v_hgrn_norm_g, v_w_proj_attn, v_w_proj_hgrn, v_w_out, v_ln3_g, v_ln3_b, v_ffn2_w1, v_ffn2_w3, v_ffn2_w2):
    w = dict(ln1_g=ln1_g, ln1_b=ln1_b, ffn1_w1=ffn1_w1, ffn1_w3=ffn1_w3, ffn1_w2=ffn1_w2, ln2_g=ln2_g, ln2_b=ln2_b,
             w_in=w_in, b_in=b_in, attn_sinks=attn_sinks, hgrn_lb_logits=hgrn_lb_logits, hgrn_norm_g=hgrn_norm_g,
             w_proj_attn=w_proj_attn, w_proj_hgrn=w_proj_hgrn, w_out=w_out, ln3_g=ln3_g, ln3_b=ln3_b,
             ffn2_w1=ffn2_w1, ffn2_w3=ffn2_w3, ffn2_w2=ffn2_w2)
    mom = dict(ln1_g=m_ln1_g, ln1_b=m_ln1_b, ffn1_w1=m_ffn1_w1, ffn1_w3=m_ffn1_w3, ffn1_w2=m_ffn1_w2, ln2_g=m_ln2_g,
               ln2_b=m_ln2_b, w_in=m_w_in, b_in=m_b_in, attn_sinks=m_attn_sinks, hgrn_lb_logits=m_hgrn_lb_logits,
               hgrn_norm_g=m_hgrn_norm_g, w_proj_attn=m_w_proj_attn, w_proj_hgrn=m_w_proj_hgrn, w_out=m_w_out,
               ln3_g=m_ln3_g, ln3_b=m_ln3_b, ffn2_w1=m_ffn2_w1, ffn2_w3=m_ffn2_w3, ffn2_w2=m_ffn2_w2)
    var = dict(ln1_g=v_ln1_g, ln1_b=v_ln1_b, ffn1_w1=v_ffn1_w1, ffn1_w3=v_ffn1_w3, ffn1_w2=v_ffn1_w2, ln2_g=v_ln2_g,
               ln2_b=v_ln2_b, w_in=v_w_in, b_in=v_b_in, attn_sinks=v_attn_sinks, hgrn_lb_logits=v_hgrn_lb_logits,
               hgrn_norm_g=v_hgrn_norm_g, w_proj_attn=v_w_proj_attn, w_proj_hgrn=v_w_proj_hgrn, w_out=v_w_out,
               ln3_g=v_ln3_g, ln3_b=v_ln3_b, ffn2_w1=v_ffn2_w1, ffn2_w3=v_ffn2_w3, ffn2_w2=v_ffn2_w2)

    n_tok = x.shape[1]
    x0 = x.reshape(n_tok, D_MODEL)
    target = loss_target.reshape(n_tok, D_MODEL)

    shard = {k: _send_form(k, w[k][0]) for k in _BIG}
    gather = lambda keys: _GatherWeights([shard[k] for k in keys])
    slots = {}
    exchange = lambda keys: _ExchangeGrads([big[k] for k in keys])
    ffn1_keys = ("ffn1_w1", "ffn1_w3", "ffn1_w2")
    mixer_keys = ("w_in", "w_proj_attn", "w_proj_hgrn", "w_out")
    ffn2_keys = ("ffn2_w1", "ffn2_w3", "ffn2_w2")
    f1 = _run_comm("gather_ffn1", gather(ffn1_keys))

    tabs = _rope_tables(n_tok)
    lb, lb_of = _lb_fwd(hgrn_lb_logits)
    (z1, x1, x0b, x1b), (w_in_g, w_pa, w_ph, w_o) = _ffn_fwd(
        "ffn1_fwd", x0, *f1, ln1_g, ln1_b, comm=gather(mixer_keys))
    w_pa, w_ph, w_o = (t.reshape(D_MODEL, D_MODEL) for t in (w_pa, w_ph, w_o))
    (proj,), f2 = _in_proj(x1b, w_in_g, b_in, comm=gather(ffn2_keys))
    y_attn = _attn_fwd(proj, tabs, attn_sinks)
    y_hgrn, states = _hgrn_fwd(proj, lb, hgrn_norm_g)
    z2, x2 = _mix_fwd(y_attn, y_hgrn, proj, x1, w_pa, w_ph, w_o, ln2_g, ln2_b)
    (z3, dy, x2b, loss_part), _ = _ffn_fwd("ffn2_fwd", x2, *f2, ln3_g, ln3_b, target=target)

    big = {}
    small = {}
    (dx2, a2, dh1_2, dh3_2, df2, small["ln3_g"], small["ln3_b"]), _ = _ffn_bwd(
        "ffn2_bwd", x2b, z3, dy, *f2, ln3_g, ln3_b)
    (big["ffn2_w1"],), _ = _grad_matmul("ffn2_dw1", x2b, "shared", D_MODEL, dh1_2, "batch", FF_PAD)
    (big["ffn2_w3"],), _ = _grad_matmul("ffn2_dw3", x2b, "shared", D_MODEL, dh3_2, "batch", FF_PAD)
    (big["ffn2_w2"],), _ = _grad_matmul("ffn2_dw2", a2, "batch", FF_PAD, df2, "shared", D_MODEL)
    (dz2, dz2b, merged, dya, dyh, dy_attn, dy_hgrn, dgates, small["ln2_g"], small["ln2_b"]) = _mix_bwd(
        dx2, z2, y_attn, y_hgrn, proj, w_pa, w_ph, w_o, ln2_g, ln2_b)
    (big["w_out"],), _ = _grad_matmul("dw_out", merged, "cols", PROJ_SHARD, dz2b, "shared", D_MODEL)
    (big["w_proj_attn"],), _ = _grad_matmul("dw_proj_attn", y_attn, "cols", PROJ_SHARD, dya, "shared", D_MODEL)
    (big["w_proj_hgrn"],), _ = _grad_matmul("dw_proj_hgrn", y_hgrn, "cols", PROJ_SHARD, dyh, "shared", D_MODEL)
    dfl, dqh, dih, dog, dlb, small["hgrn_norm_g"] = _hgrn_bwd(proj, lb, hgrn_norm_g, states, dy_hgrn)
    early_keys = ffn2_keys + mixer_keys[1:]
    (dq, dkv, dsinks), early = _attn_bwd(proj, dy_attn, tabs, attn_sinks, comm=exchange(early_keys))
    slots.update(zip(early_keys, early))
    dproj = jnp.concatenate([dq, dkv, dfl, dqh, dih, dog, dgates], axis=1)
    (big["w_in"], small["b_in"]), _ = _grad_matmul(
        "dw_in", x1b, "shared", D_MODEL, dproj, "cols", IN_SHARD, colsum=True)
    dx1 = _in_proj_dx(dproj, w_in_g, dz2)
    (grad_x, a1, dh1_1, dh3_1, df1, small["ln1_g"], small["ln1_b"]), (slots["w_in"],) = _ffn_bwd(
        "ffn1_bwd", x0b, z1, dx1, *f1, ln1_g, ln1_b, comm=exchange(("w_in",)))
    (big["ffn1_w1"],), _ = _grad_matmul("ffn1_dw1", x0b, "shared", D_MODEL, dh1_1, "batch", FF_PAD)
    (big["ffn1_w3"],), (slots["ffn1_w1"],) = _grad_matmul(
        "ffn1_dw3", x0b, "shared", D_MODEL, dh3_1, "batch", FF_PAD, comm=exchange(("ffn1_w1",)))
    (big["ffn1_w2"],), (slots["ffn1_w3"],) = _grad_matmul(
        "ffn1_dw2", a1, "batch", FF_PAD, df1, "shared", D_MODEL, comm=exchange(("ffn1_w3",)))
    (slots["ffn1_w2"],) = _run_comm("exchange_last", exchange(("ffn1_w2",)))

    partial = []
    for k in _BIG:
        p = _sum_slots("sum_" + k, slots[k])
        partial.append(p[:w[k].shape[1], :w[k].shape[2]])
    from_sibling = _swap_with_sibling(partial)

    outs = {"grad": {}, "delta": {}, "m": {}, "v": {}}
    for k, p, q in zip(_BIG, partial, from_sibling):
        res = _adam_big("adam_" + k, p, q, w[k][0], mom[k][0], var[k][0])
        for kind, r in zip(("grad", "delta", "m", "v"), res):
            outs[kind][k] = r.reshape(w[k].shape)

    part = jnp.concatenate(
        [small[k] for k in ("ln1_g", "ln1_b", "ln2_g", "ln2_b", "ln3_g", "ln3_b", "b_in")]
        + [dsinks, small["hgrn_norm_g"], dlb, loss_part], axis=1)
    total = _sum_small(part)
    res = _adam_small(total, _pack_small(w), _pack_small(mom), _pack_small(var), lb_of)
    for kind, r in zip(("grad", "delta", "m", "v"), res):
        outs[kind].update(_unpack_small(r, w))
    loss = total[0, SM_LOSS]

    return (loss, grad_x.reshape(x.shape), *[outs["grad"][k] for k in _ORDER], *[outs["delta"][k] for k in _ORDER],
            *[outs["m"][k] for k in _ORDER], *[outs["v"][k] for k in _ORDER])
```

```python
import functools

import jax
import jax.numpy as jnp
from jax import lax
from jax.experimental import pallas as pl
from jax.experimental.pallas import tpu as pltpu

F32 = jnp.float32
BF16 = jnp.bfloat16

D_MODEL = 1024
N_Q_HEADS = 16
N_KV_HEADS = 4
HEAD_DIM = 64
ATTN_BLOCK = 128
ROPE_THETA = 500000.0
ROPE_DIM = HEAD_DIM // 4
HGRN_HEADS = 8
HGRN_DK = 128
HGRN_CHUNK = 64
D_FF = 2816
D_IN = 7680
DEEPNORM_ALPHA = 2 ** 0.25
LN_EPS = 1e-5
RMS_EPS = 1e-6
NEG_INF = -1e30

ADAM_LR = 0.001
ADAM_B1 = 0.9
ADAM_B2 = 0.999
ADAM_EPS = 1e-08
ADAM_WD = 0.01
ADAM_STEP = 10

N_CHIPS = 4
N_DEV = 8
FF_SHARD = D_FF // N_CHIPS
LANES = 128
MXU_COLS = 256
FF_PAD = -(-FF_SHARD // MXU_COLS) * MXU_COLS
IN_SHARD = D_IN // N_CHIPS
PROJ_SHARD = D_MODEL // N_CHIPS
ROW_TILE = 512
GRAD_ROWS = 2048
MIX_TILE = 256
UPDATE_ROWS = 128
HGRN_CHUNKS_PER_STEP = 8
VMEM_LIMIT = 56 * 1024 * 1024

COL_K = 1024 // 256
COL_V = 1280 // 256
COL_F = 1536 // LANES
COL_QH = 2560 // LANES
COL_IH = 3584 // LANES
COL_OG = 4608 // LANES
COL_GA = 5632 // 512
COL_GH = 6656 // 512

SM_LN = 0
SM_BIN = 6 * D_MODEL
SM_SINK = SM_BIN + D_IN
SM_NG = SM_SINK + LANES
SM_LB = SM_NG + LANES
SM_LOSS = SM_LB + D_MODEL
SM_LEN = SM_LOSS + LANES
PK_LB0 = SM_LB
PK_LB1 = SM_LB + D_MODEL
PK_LEN = PK_LB1 + D_MODEL

MESH = pl.DeviceIdType.MESH


def _mm(a, b):
    return lax.dot_general(a, b, (((1,), (0,)), ((), ())), preferred_element_type=F32)


def _mm_nt(a, b):
    return lax.dot_general(a, b, (((1,), (1,)), ((), ())), preferred_element_type=F32)


def _mm_tn(a, b):
    return lax.dot_general(a, b, (((0,), (0,)), ((), ())), preferred_element_type=F32)


def _bf(v):
    return v.astype(BF16)


def _sig(v):
    return jax.nn.sigmoid(v)


def _ln(z, g, b):
    mu = jnp.mean(z, axis=-1, keepdims=True)
    zc = z - mu
    var = jnp.mean(zc * zc, axis=-1, keepdims=True)
    return zc * lax.rsqrt(var + LN_EPS) * g + b


def _swiglu_act(h1, h3):
    return (h1 * _sig(h1)) * h3


def _params(sem=None):
    return pltpu.CompilerParams(dimension_semantics=sem, vmem_limit_bytes=VMEM_LIMIT)


def _full(shape):
    nd = len(shape)
    return pl.BlockSpec(shape, lambda *_: (0,) * nd)


def _update_rows(rows):
    return max(t for t in range(8, UPDATE_ROWS + 1, 8) if rows % t == 0)


def _ffn_fwd(name, x, w1g, w3g, w2g, g, b, target=None, comm=None):
    n = x.shape[0]
    tm = min(ROW_TILE, n)
    final = target is not None

    def body(*refs):
        if final:
            x_ref, w1_ref, w3_ref, w2_ref, g_ref, b_ref, t_ref, z_ref, o_ref, xb, h1_ref, h3_ref, loss_ref, acc = refs
        else:
            x_ref, w1_ref, w3_ref, w2_ref, g_ref, b_ref, z_ref, o_ref, xb, h1_ref, h3_ref, ob_ref, acc = refs
        i = pl.program_id(0)
        j = pl.program_id(1)

        @pl.when(j == 0)
        def _():
            xb[...] = _bf(x_ref[...])
            acc[...] = jnp.zeros_like(acc)

        xv = xb[...]
        part = None
        for c in range(FF_PAD // MXU_COLS):
            cols = slice(c * MXU_COLS, (c + 1) * MXU_COLS)
            h1 = _mm(xv, w1_ref[0, :, cols])
            h3 = _mm(xv, w3_ref[0, :, cols])
            h1_ref[0, :, cols] = _bf(h1)
            h3_ref[0, :, cols] = _bf(h3)
            d = _mm(_bf(_swiglu_act(h1, h3)), w2_ref[0, cols, :])
            part = d if part is None else part + d
        acc[...] += part

        @pl.when(j == N_CHIPS - 1)
        def _():
            z = DEEPNORM_ALPHA * x_ref[...] + 0.5 * acc[...]
            z_ref[...] = z
            y = _ln(z, g_ref[...], b_ref[...])
            if final:
                e = y - t_ref[...]

                @pl.when(i == 0)
                def _():
                    loss_ref[...] = jnp.zeros_like(loss_ref)

                loss_ref[...] += jnp.sum(e * e) * (0.5 / D_MODEL)
                o_ref[...] = e * (1.0 / D_MODEL)
            else:
                o_ref[...] = y
                ob_ref[...] = _bf(y)

    row = pl.BlockSpec((tm, D_MODEL), lambda i, j: (i, 0))
    wcol = pl.BlockSpec((1, D_MODEL, FF_PAD), lambda i, j: (j, 0, 0))
    wrow = pl.BlockSpec((1, FF_PAD, D_MODEL), lambda i, j: (j, 0, 0))
    vec = pl.BlockSpec((1, D_MODEL), lambda i, j: (0, 0))
    in_specs = [row, wcol, wcol, wrow, vec, vec]
    args = [x, w1g, w3g, w2g, g, b]
    hid = pl.BlockSpec((1, tm, FF_PAD), lambda i, j: (j, i, 0))
    hid_shape = jax.ShapeDtypeStruct((N_CHIPS, n, FF_PAD), BF16)
    out_specs = [row, row, row, hid, hid]
    out_shape = ([jax.ShapeDtypeStruct((n, D_MODEL), F32)] * 2 + [jax.ShapeDtypeStruct((n, D_MODEL), BF16)]
                 + [hid_shape] * 2)
    if final:
        in_specs.append(row)
        args.append(target)
        out_specs.append(pl.BlockSpec((1, LANES), lambda i, j: (0, 0)))
        out_shape.append(jax.ShapeDtypeStruct((1, LANES), F32))
    else:
        out_specs.append(row)
        out_shape.append(jax.ShapeDtypeStruct((n, D_MODEL), BF16))
    return _hosted(
        body, comm, name=name, grid=(n // tm, N_CHIPS), in_specs=in_specs, out_specs=out_specs, out_shape=out_shape,
        scratch_shapes=[pltpu.VMEM((tm, D_MODEL), F32)],
        compiler_params=_params(("arbitrary", "arbitrary")), args=args)


def _ffn_bwd(name, h1s, h3s, z, dout, w1g, w3g, w2g, g, b, comm=None):
    n = z.shape[0]
    tm = min(ROW_TILE, n)

    def body(h1_ref, h3_ref, z_ref, do_ref, w1_ref, w3_ref, w2_ref, g_ref, b_ref,
             dx_ref, a_ref, dh1_ref, dh3_ref, df_ref, dg_ref, db_ref, acc, dzs):
        i = pl.program_id(0)
        j = pl.program_id(1)

        @pl.when(j == 0)
        def _():
            acc[...] = jnp.zeros_like(acc)
            _, vjp = jax.vjp(_ln, z_ref[...], g_ref[...], b_ref[...])
            dz, dg, db = vjp(do_ref[...])
            dzs[...] = dz
            df_ref[...] = _bf(0.5 * dz)

            @pl.when(i == 0)
            def _():
                dg_ref[...] = jnp.zeros_like(dg_ref)
                db_ref[...] = jnp.zeros_like(db_ref)

            dg_ref[...] += dg
            db_ref[...] += db

        a, act_vjp = jax.vjp(_swiglu_act, h1_ref[0].astype(F32), h3_ref[0].astype(F32))
        da = _mm_nt(df_ref[...], w2_ref[0])
        dh1, dh3 = act_vjp(da)
        dh1 = _bf(dh1)
        dh3 = _bf(dh3)
        a_ref[0] = _bf(a)
        dh1_ref[0] = dh1
        dh3_ref[0] = dh3
        acc[...] += _mm_nt(dh1, w1_ref[0]) + _mm_nt(dh3, w3_ref[0])

        @pl.when(j == N_CHIPS - 1)
        def _():
            dx_ref[...] = DEEPNORM_ALPHA * dzs[...] + acc[...]

    row = pl.BlockSpec((tm, D_MODEL), lambda i, j: (i, 0))
    wcol = pl.BlockSpec((1, D_MODEL, FF_PAD), lambda i, j: (j, 0, 0))
    wrow = pl.BlockSpec((1, FF_PAD, D_MODEL), lambda i, j: (j, 0, 0))
    vec = pl.BlockSpec((1, D_MODEL), lambda i, j: (0, 0))
    hid = pl.BlockSpec((1, tm, FF_PAD), lambda i, j: (j, i, 0))
    hid_shape = jax.ShapeDtypeStruct((N_CHIPS, n, FF_PAD), BF16)
    return _hosted(
        body, comm, name=name, grid=(n // tm, N_CHIPS),
        in_specs=[hid, hid, row, row, wcol, wcol, wrow, vec, vec],
        out_specs=[row, hid, hid, hid, row, vec, vec],
        out_shape=[jax.ShapeDtypeStruct((n, D_MODEL), F32), hid_shape, hid_shape, hid_shape,
                   jax.ShapeDtypeStruct((n, D_MODEL), BF16),
                   jax.ShapeDtypeStruct((1, D_MODEL), F32), jax.ShapeDtypeStruct((1, D_MODEL), F32)],
        scratch_shapes=[pltpu.VMEM((tm, D_MODEL), F32), pltpu.VMEM((tm, D_MODEL), F32)],
        compiler_params=_params(("arbitrary", "arbitrary")), args=[h1s, h3s, z, dout, w1g, w3g, w2g, g, b])


def _operand_spec(arr, mode, tn, width):
    if mode == "shared":
        return pl.BlockSpec((tn, width), lambda s, k: (k, 0))
    if mode == "batch":
        return pl.BlockSpec((1, tn, width), lambda s, k: (s, k, 0))
    assert mode == "cols" and arr.shape[1] == N_CHIPS * width
    return pl.BlockSpec((tn, width), lambda s, k: (k, s))


def _grad_matmul(name, a, a_mode, ka, b, b_mode, kb, colsum=False, comm=None):
    n = a.shape[-2]
    tn = min(GRAD_ROWS, n)
    nk = n // tn

    def body(*refs):
        if colsum:
            a_ref, b_ref, o_ref, cs_ref, acc = refs
        else:
            a_ref, b_ref, o_ref, acc = refs
        k = pl.program_id(1)
        av = a_ref[0] if a_mode == "batch" else a_ref[...]
        bv = b_ref[0] if b_mode == "batch" else b_ref[...]

        @pl.when(k == 0)
        def _():
            acc[...] = jnp.zeros_like(acc)
            if colsum:
                cs_ref[...] = jnp.zeros_like(cs_ref)

        acc[...] += _mm_tn(av, bv)
        if colsum:
            cs_ref[...] += jnp.sum(bv.astype(F32), axis=0, keepdims=True)

        @pl.when(k == nk - 1)
        def _():
            o_ref[0] = _bf(acc[...])

    out_specs = [pl.BlockSpec((1, ka, kb), lambda s, k: (s, 0, 0))]
    out_shape = [jax.ShapeDtypeStruct((N_CHIPS, ka, kb), BF16)]
    if colsum:
        out_specs.append(pl.BlockSpec((1, kb), lambda s, k: (0, s)))
        out_shape.append(jax.ShapeDtypeStruct((1, N_CHIPS * kb), F32))
    return _hosted(
        body, comm, name=name, grid=(N_CHIPS, nk),
        in_specs=[_operand_spec(a, a_mode, tn, ka), _operand_spec(b, b_mode, tn, kb)],
        out_specs=out_specs, out_shape=out_shape,
        scratch_shapes=[pltpu.VMEM((ka, kb), F32)],
        compiler_params=_params(("arbitrary", "arbitrary")), args=[a, b])


def _in_proj(x1, w_in_g, b_in, comm=None):
    n = x1.shape[0]
    tm = min(ROW_TILE, n)

    def body(x_ref, w_ref, b_ref, o_ref):
        o_ref[...] = _mm(x_ref[...], w_ref[0]) + b_ref[...]

    return _hosted(
        body, comm, name="in_proj", grid=(N_CHIPS, n // tm),
        in_specs=[pl.BlockSpec((tm, D_MODEL), lambda j, i: (i, 0)),
                  pl.BlockSpec((1, D_MODEL, IN_SHARD), lambda j, i: (j, 0, 0)),
                  pl.BlockSpec((1, IN_SHARD), lambda j, i: (0, j))],
        out_specs=[pl.BlockSpec((tm, IN_SHARD), lambda j, i: (i, j))],
        out_shape=[jax.ShapeDtypeStruct((n, D_IN), F32)],
        scratch_shapes=[],
        compiler_params=_params(("arbitrary", "arbitrary")), args=[x1, w_in_g, b_in])


def _in_proj_dx(dproj, w_in_g, dz2):
    n = dproj.shape[0]
    tm = min(ROW_TILE, n)

    def body(dp_ref, w_ref, dz_ref, o_ref, acc):
        j = pl.program_id(1)

        @pl.when(j == 0)
        def _():
            acc[...] = jnp.zeros_like(acc)

        acc[...] += _mm_nt(dp_ref[...], w_ref[0])

        @pl.when(j == N_CHIPS - 1)
        def _():
            o_ref[...] = DEEPNORM_ALPHA * dz_ref[...] + acc[...]

    return pl.pallas_call(
        body, name="in_proj_dx", grid=(n // tm, N_CHIPS),
        in_specs=[pl.BlockSpec((tm, IN_SHARD), lambda i, j: (i, j)),
                  pl.BlockSpec((1, D_MODEL, IN_SHARD), lambda i, j: (j, 0, 0)),
                  pl.BlockSpec((tm, D_MODEL), lambda i, j: (i, 0))],
        out_specs=pl.BlockSpec((tm, D_MODEL), lambda i, j: (i, 0)),
        out_shape=jax.ShapeDtypeStruct((n, D_MODEL), F32),
        scratch_shapes=[pltpu.VMEM((tm, D_MODEL), F32)],
        compiler_params=_params(("arbitrary", "arbitrary")),
    )(dproj, w_in_g, dz2)


def _rope_tables(seq_len):
    pos = jnp.arange(seq_len, dtype=F32)
    inv_freq = ROPE_THETA ** (-jnp.arange(0, ROPE_DIM, 2, dtype=F32) / ROPE_DIM)
    ang = pos[:, None] * inv_freq[None, :]
    cos, sin = jnp.cos(ang), jnp.sin(ang)
    half = ROPE_DIM // 2
    rest = HEAD_DIM - ROPE_DIM
    ones = jnp.ones((seq_len, rest), F32)
    zeros = jnp.zeros((seq_len, rest), F32)
    zh = jnp.zeros((seq_len, half), F32)
    c = jnp.concatenate([cos, cos, ones], axis=1)
    sa = jnp.concatenate([-sin, zh, zeros], axis=1)
    sb = jnp.concatenate([zh, sin, zeros], axis=1)
    reps = LANES // HEAD_DIM
    return tuple(jnp.tile(t, (1, reps)) for t in (c, sa, sb))


def _rope(t, c, sa, sb):
    w = t.shape[1]
    reps = w // LANES
    half = ROPE_DIM // 2
    return (t * jnp.tile(c, (1, reps)) + pltpu.roll(t, w - half, 1) * jnp.tile(sa, (1, reps))
            + pltpu.roll(t, half, 1) * jnp.tile(sb, (1, reps)))


def _rope_transposed(g, c, sa, sb):
    w = g.shape[1]
    reps = w // LANES
    half = ROPE_DIM // 2
    return (g * jnp.tile(c, (1, reps)) + pltpu.roll(g * jnp.tile(sa, (1, reps)), half, 1)
            + pltpu.roll(g * jnp.tile(sb, (1, reps)), w - half, 1))


GROUP = N_Q_HEADS // N_KV_HEADS


def _attn_mask(n):
    rows = GROUP * ATTN_BLOCK
    qi = lax.broadcasted_iota(jnp.int32, (rows, 2 * ATTN_BLOCK), 0) % ATTN_BLOCK
    kj = lax.broadcasted_iota(jnp.int32, (rows, 2 * ATTN_BLOCK), 1)
    dist = qi + ATTN_BLOCK - kj
    return (dist >= 0) & (dist < ATTN_BLOCK) & (n * ATTN_BLOCK + kj - ATTN_BLOCK >= 0)


def _both_halves(t_pair, which):
    lo = lax.broadcasted_iota(jnp.int32, t_pair.shape, 1) < HEAD_DIM
    swapped = pltpu.roll(t_pair, HEAD_DIM, 1)
    return _bf(jnp.where(lo, t_pair, swapped) if which == 0 else jnp.where(lo, swapped, t_pair))


def _stack_heads(ref_or_val, kh):
    lo = lax.broadcasted_iota(jnp.int32, (ATTN_BLOCK, LANES), 1) < HEAD_DIM
    rows = []
    for gp in range(GROUP // 2):
        pair = kh * (GROUP // 2) + gp
        t = ref_or_val[:, pair * LANES:(pair + 1) * LANES]
        rows += [jnp.where(lo, t, jnp.zeros_like(t)), jnp.where(lo, jnp.zeros_like(t), t)]
    return jnp.concatenate(rows, axis=0)


def _unstack_pairs(stacked):
    lo = lax.broadcasted_iota(jnp.int32, (ATTN_BLOCK, LANES), 1) < HEAD_DIM
    b = ATTN_BLOCK
    return [jnp.where(lo, stacked[2 * gp * b:(2 * gp + 1) * b], stacked[(2 * gp + 1) * b:(2 * gp + 2) * b])
            for gp in range(GROUP // 2)]


def _sink_column(sink_ref, kh):
    row = lax.broadcasted_iota(jnp.int32, (GROUP * ATTN_BLOCK, 1), 0)
    col = jnp.full((GROUP * ATTN_BLOCK, 1), sink_ref[0, kh * GROUP + GROUP - 1], F32)
    for i in reversed(range(GROUP - 1)):
        col = jnp.where(row < (i + 1) * ATTN_BLOCK, sink_ref[0, kh * GROUP + i], col)
    return col


def _attn_probs(q_masked, k_sel, mask, sink):
    s = _mm_nt(q_masked, k_sel) * (HEAD_DIM ** -0.5)
    s = jnp.where(mask, s, NEG_INF)
    m = jnp.maximum(jnp.max(s, axis=-1, keepdims=True), sink)
    p = jnp.exp(s - m)
    e_sink = jnp.exp(sink - m)
    denom = jnp.sum(p, axis=-1, keepdims=True) + e_sink
    return p / denom, e_sink / denom


def _attn_fwd(proj, tabs, sinks):
    n_tok = proj.shape[0]
    nb = n_tok // ATTN_BLOCK
    group = N_Q_HEADS // N_KV_HEADS

    def body(q_ref, k_ref, v_ref, c_ref, sa_ref, sb_ref, sink_ref, y_ref, kprev, vprev):
        n = pl.program_id(0)

        @pl.when(n == 0)
        def _():
            kprev[...] = jnp.zeros_like(kprev)
            vprev[...] = jnp.zeros_like(vprev)

        c, sa, sb = c_ref[...], sa_ref[...], sb_ref[...]
        qr = _bf(_rope(q_ref[...], c, sa, sb))
        kr = _rope(k_ref[...], c, sa, sb)
        vc = v_ref[...]
        kk = jnp.concatenate([kprev[...], kr], axis=0)
        vv = jnp.concatenate([vprev[...], vc], axis=0)
        kprev[...] = kr
        vprev[...] = vc
        mask = _attn_mask(n)
        for kh in range(N_KV_HEADS):
            r, which = divmod(kh, 2)
            kb = _both_halves(kk[:, r * LANES:(r + 1) * LANES], which)
            vb = _both_halves(vv[:, r * LANES:(r + 1) * LANES], which)
            probs, _ = _attn_probs(_stack_heads(qr, kh), kb, mask, _sink_column(sink_ref, kh))
            for gp, out in enumerate(_unstack_pairs(_mm(_bf(probs), vb))):
                pair = kh * (GROUP // 2) + gp
                y_ref[:, pair * LANES:(pair + 1) * LANES] = _bf(out)

    blk = lambda width, col: pl.BlockSpec((ATTN_BLOCK, width), lambda n: (n, col))
    tab = pl.BlockSpec((ATTN_BLOCK, LANES), lambda n: (n, 0))
    kvw = N_KV_HEADS * HEAD_DIM
    return pl.pallas_call(
        body, name="attn_fwd", grid=(nb,),
        in_specs=[blk(D_MODEL, 0), blk(kvw, COL_K), blk(kvw, COL_V), tab, tab, tab,
                  pl.BlockSpec(memory_space=pltpu.SMEM)],
        out_specs=pl.BlockSpec((ATTN_BLOCK, D_MODEL), lambda n: (n, 0)),
        out_shape=jax.ShapeDtypeStruct((n_tok, D_MODEL), BF16),
        scratch_shapes=[pltpu.VMEM((ATTN_BLOCK, kvw), F32), pltpu.VMEM((ATTN_BLOCK, kvw), F32)],
        compiler_params=_params(("arbitrary",)),
    )(proj, proj, proj, *tabs, sinks)


def _attn_bwd(proj, dy, tabs, sinks, comm=None):
    n_tok = proj.shape[0]
    nb = n_tok // ATTN_BLOCK
    group = N_Q_HEADS // N_KV_HEADS
    kvw = N_KV_HEADS * HEAD_DIM

    def body(q_ref, k_ref, v_ref, do_ref, c_ref, sa_ref, sb_ref, cp_ref, sap_ref, sbp_ref, sink_ref,
             dq_ref, dkv_ref, dsink_ref, kprev, vprev, dkc, dvc):
        n = pl.program_id(0)

        @pl.when(n == 0)
        def _():
            for ref in (kprev, vprev, dkc, dvc, dsink_ref):
                ref[...] = jnp.zeros_like(ref)

        prev_tabs = (cp_ref[...], sap_ref[...], sbp_ref[...])

        @pl.when(n < nb)
        def _():
            c, sa, sb = c_ref[...], sa_ref[...], sb_ref[...]
            qr = _bf(_rope(q_ref[...], c, sa, sb))
            kr = _rope(k_ref[...], c, sa, sb)
            vc = v_ref[...]
            kk = jnp.concatenate([kprev[...], kr], axis=0)
            vv = jnp.concatenate([vprev[...], vc], axis=0)
            kprev[...] = kr
            vprev[...] = vc
            mask = _attn_mask(n)
            lane = lax.broadcasted_iota(jnp.int32, (1, LANES), 1)
            lo2 = lax.broadcasted_iota(jnp.int32, (2 * ATTN_BLOCK, LANES), 1) < HEAD_DIM
            dsink = jnp.zeros((1, LANES), F32)
            dq_pairs = []
            dk_full = []
            dv_full = []
            for kh in range(N_KV_HEADS):
                r, which = divmod(kh, 2)
                kb = _both_halves(kk[:, r * LANES:(r + 1) * LANES], which)
                vb = _both_halves(vv[:, r * LANES:(r + 1) * LANES], which)
                qs = _stack_heads(qr, kh)
                dos = _bf(_stack_heads(do_ref, kh))
                probs, p_sink = _attn_probs(qs, kb, mask, _sink_column(sink_ref, kh))
                dp = _mm_nt(dos, vb)
                delta = jnp.sum(probs * dp, axis=-1, keepdims=True)
                ds = _bf(probs * (dp - delta) * (HEAD_DIM ** -0.5))
                sink_terms = p_sink * delta
                for i in range(GROUP):
                    head_sum = jnp.sum(sink_terms[i * ATTN_BLOCK:(i + 1) * ATTN_BLOCK])
                    dsink = dsink + jnp.where(lane == kh * GROUP + i, -head_sum, 0.0)
                dq_pairs += _unstack_pairs(_mm(ds, kb))
                dk_acc = _mm_tn(ds, qs)
                dv_acc = _mm_tn(_bf(probs), dos)
                dk_full.append(dk_acc + pltpu.roll(dk_acc, HEAD_DIM, 1))
                dv_full.append(dv_acc + pltpu.roll(dv_acc, HEAD_DIM, 1))
            dk_pairs = [jnp.where(lo2, dk_full[2 * r], dk_full[2 * r + 1]) for r in range(N_KV_HEADS // 2)]
            dv_pairs = [jnp.where(lo2, dv_full[2 * r], dv_full[2 * r + 1]) for r in range(N_KV_HEADS // 2)]
            dsink_ref[...] += dsink
            dq_ref[...] = _bf(_rope_transposed(jnp.concatenate(dq_pairs, axis=1), c, sa, sb))
            dk_all = jnp.concatenate(dk_pairs, axis=1)
            dv_all = jnp.concatenate(dv_pairs, axis=1)
            dkv_ref[:, :kvw] = _bf(_rope_transposed(dkc[...] + dk_all[:ATTN_BLOCK], *prev_tabs))
            dkv_ref[:, kvw:] = _bf(dvc[...] + dv_all[:ATTN_BLOCK])
            dkc[...] = dk_all[ATTN_BLOCK:]
            dvc[...] = dv_all[ATTN_BLOCK:]

        @pl.when(n == nb)
        def _():
            dkv_ref[:, :kvw] = _bf(_rope_transposed(dkc[...], *prev_tabs))
            dkv_ref[:, kvw:] = _bf(dvc[...])

    cur = lambda n: jnp.minimum(n, nb - 1)
    prev = lambda n: jnp.maximum(n - 1, 0)
    blk = lambda width, col: pl.BlockSpec((ATTN_BLOCK, width), lambda n: (cur(n), col))
    tab = pl.BlockSpec((ATTN_BLOCK, LANES), lambda n: (cur(n), 0))
    tabp = pl.BlockSpec((ATTN_BLOCK, LANES), lambda n: (prev(n), 0))
    return _hosted(
        body, comm, name="attn_bwd", grid=(nb + 1,),
        in_specs=[blk(D_MODEL, 0), blk(kvw, COL_K), blk(kvw, COL_V), blk(D_MODEL, 0), tab, tab, tab, tabp, tabp, tabp,
                  pl.BlockSpec(memory_space=pltpu.SMEM)],
        out_specs=[pl.BlockSpec((ATTN_BLOCK, D_MODEL), lambda n: (cur(n), 0)),
                   pl.BlockSpec((ATTN_BLOCK, 2 * kvw), lambda n: (prev(n), 0)),
                   pl.BlockSpec((1, LANES), lambda n: (0, 0))],
        out_shape=[jax.ShapeDtypeStruct((n_tok, D_MODEL), BF16), jax.ShapeDtypeStruct((n_tok, 2 * kvw), BF16),
                   jax.ShapeDtypeStruct((1, LANES), F32)],
        scratch_shapes=[pltpu.VMEM((ATTN_BLOCK, kvw), F32)] * 4,
        compiler_params=_params(("arbitrary",)), args=[proj, proj, proj, dy, *tabs, *tabs, sinks])


def _bmm(a, b):
    return lax.dot_general(a, b, (((2,), (1,)), ((0,), (0,))), preferred_element_type=F32)


def _bmm_nt(a, b):
    return lax.dot_general(a, b, (((2,), (2,)), ((0,), (0,))), preferred_element_type=F32)


def _bmm_tn(a, b):
    return lax.dot_general(a, b, (((1,), (1,)), ((0,), (0,))), preferred_element_type=F32)


def _tril(cb, upper=False):
    shape = (cb, HGRN_CHUNK, HGRN_CHUNK)
    r, c = lax.broadcasted_iota(jnp.int32, shape, 1), lax.broadcasted_iota(jnp.int32, shape, 2)
    return (r <= c) if upper else (r >= c)


def _tri_matmul(x, upper):
    return lax.dot_general(_tril(x.shape[0], upper).astype(F32), x, (((2,), (1,)), ((0,), (0,))),
                           precision=lax.Precision.HIGHEST, preferred_element_type=F32)


@jax.custom_vjp
def _chunk_cumsum(x):
    return _tri_matmul(x, False)


_chunk_cumsum.defvjp(lambda x: (_tri_matmul(x, False), None), lambda _, g: (_tri_matmul(g, True),))


def _hg_elem(fl, qh, lb):
    f = lb + (1.0 - lb) * _sig(fl)
    k = 1.0 - f
    gc = _chunk_cumsum(jnp.log(f))
    last = lax.broadcasted_iota(jnp.int32, gc.shape, 1) == HGRN_CHUNK - 1
    g_last = jnp.sum(jnp.where(last, gc, 0.0), axis=1, keepdims=True)
    q = qh * _sig(qh)
    return q * jnp.exp(gc), k * jnp.exp(-gc), k * jnp.exp(g_last - gc), jnp.exp(g_last)


def _hg_out(q_dec, k_inv, v, st):
    sc = jnp.where(_tril(q_dec.shape[0]), _bmm_nt(_bf(q_dec), _bf(k_inv)), 0.0)
    return _bmm(_bf(sc), _bf(v)) + _bmm_nt(_bf(q_dec), _bf(st)), sc


def _hg_post(o, og, ng):
    on = o * lax.rsqrt(jnp.mean(o * o, axis=-1, keepdims=True) + RMS_EPS) * ng
    return on * (og * _sig(og))


def _hgrn_specs(n_tok, rev):
    nc = n_tok // HGRN_CHUNK
    cb = min(HGRN_CHUNKS_PER_STEP, nc)
    nt = nc // cb
    rows = cb * HGRN_CHUNK
    tt = (lambda t: nt - 1 - t) if rev else (lambda t: t)
    col = lambda base: pl.BlockSpec((rows, LANES), lambda h, t: (tt(t), base + h))
    head_vec = pl.BlockSpec((1, LANES), lambda h, t: (0, h))
    one_vec = pl.BlockSpec((1, LANES), lambda h, t: (0, 0))
    state = pl.BlockSpec((1, cb, HGRN_DK, HGRN_DK), lambda h, t: (h, tt(t), 0, 0))
    return nc, cb, nt, col, head_vec, one_vec, state


def _hgrn_fwd(proj, lb, ng):
    n_tok = proj.shape[0]
    nc, cb, nt, col, head_vec, one_vec, state = _hgrn_specs(n_tok, False)

    def body(fl_ref, qh_ref, ih_ref, og_ref, lb_ref, ng_ref, y_ref, st_ref, s_acc):
        @pl.when(pl.program_id(1) == 0)
        def _():
            s_acc[...] = jnp.zeros_like(s_acc)

        chunks = lambda ref: ref[...].reshape(cb, HGRN_CHUNK, LANES)
        q_dec, k_inv, k_end, decay = _hg_elem(chunks(fl_ref), chunks(qh_ref), lb_ref[...])
        v = chunks(ih_ref)
        upd = _bmm_tn(_bf(v), _bf(k_end))
        st = s_acc[...]
        for ci in range(cb):
            st_ref[0, ci] = st
            st = st * decay[ci] + upd[ci]
        s_acc[...] = st
        o, _ = _hg_out(q_dec, k_inv, v, st_ref[0])
        y_ref[...] = _bf(_hg_post(o, chunks(og_ref), ng_ref[...]).reshape(cb * HGRN_CHUNK, LANES))

    return pl.pallas_call(
        body, name="hgrn_fwd", grid=(HGRN_HEADS, nt),
        in_specs=[col(COL_F), col(COL_QH), col(COL_IH), col(COL_OG), head_vec, one_vec],
        out_specs=[col(0), state],
        out_shape=[jax.ShapeDtypeStruct((n_tok, D_MODEL), BF16),
                   jax.ShapeDtypeStruct((HGRN_HEADS, nc, HGRN_DK, HGRN_DK), F32)],
        scratch_shapes=[pltpu.VMEM((HGRN_DK, HGRN_DK), F32)],
        compiler_params=_params(("arbitrary", "arbitrary")),
    )(proj, proj, proj, proj, lb, ng)


def _hgrn_bwd(proj, lb, ng, states, dy):
    n_tok = proj.shape[0]
    nc, cb, nt, col, head_vec, one_vec, state = _hgrn_specs(n_tok, True)

    def body(fl_ref, qh_ref, ih_ref, og_ref, lb_ref, ng_ref, st_ref, dy_ref,
             dfl_ref, dqh_ref, dih_ref, dog_ref, dlb_ref, dng_ref, g_acc, g_all):
        h = pl.program_id(0)
        t = pl.program_id(1)

        @pl.when(t == 0)
        def _():
            g_acc[...] = jnp.zeros_like(g_acc)
            dlb_ref[...] = jnp.zeros_like(dlb_ref)

        @pl.when((t == 0) & (h == 0))
        def _():
            dng_ref[...] = jnp.zeros_like(dng_ref)

        chunks = lambda ref: ref[...].reshape(cb, HGRN_CHUNK, LANES)
        flat = lambda val: _bf(val.reshape(cb * HGRN_CHUNK, LANES))
        (q_dec, k_inv, k_end, decay), elem_vjp = jax.vjp(_hg_elem, chunks(fl_ref), chunks(qh_ref), lb_ref[...])
        v = chunks(ih_ref)
        st = st_ref[0]
        o, sc = _hg_out(q_dec, k_inv, v, st)
        _, post_vjp = jax.vjp(_hg_post, o, chunks(og_ref), ng_ref[...])
        do, dog, dng = post_vjp(chunks(dy_ref))
        dob, vb, qb = _bf(do), _bf(v), _bf(q_dec)
        dsc = _bf(jnp.where(_tril(cb), _bmm_nt(dob, vb), 0.0))
        p = _bmm_tn(dob, qb)
        g = g_acc[...]
        for ci in reversed(range(cb)):
            g_all[ci] = g
            g = g * decay[ci] + p[ci]
        g_acc[...] = g
        g = g_all[...]
        gb = _bf(g)
        dq_dec = _bmm(dsc, _bf(k_inv)) + _bmm(dob, _bf(st))
        dk_inv = _bmm_tn(dsc, qb)
        dv = _bmm_tn(_bf(sc), dob) + _bmm_nt(_bf(k_end), gb)
        dk_end = _bmm(vb, gb)
        ddecay = jnp.sum(st * g, axis=1, keepdims=True)
        dfl, dqh, dlb = elem_vjp((dq_dec, dk_inv, dk_end, ddecay))
        dfl_ref[...] = flat(dfl)
        dqh_ref[...] = flat(dqh)
        dih_ref[...] = flat(dv)
        dog_ref[...] = flat(dog)
        dlb_ref[...] += dlb
        dng_ref[...] += dng

    out_col = jax.ShapeDtypeStruct((n_tok, D_MODEL), BF16)
    return pl.pallas_call(
        body, name="hgrn_bwd", grid=(HGRN_HEADS, nt),
        in_specs=[col(COL_F), col(COL_QH), col(COL_IH), col(COL_OG), head_vec, one_vec, state, col(0)],
        out_specs=[col(0), col(0), col(0), col(0), head_vec, one_vec],
        out_shape=[out_col, out_col, out_col, out_col,
                   jax.ShapeDtypeStruct((1, D_MODEL), F32), jax.ShapeDtypeStruct((1, LANES), F32)],
        scratch_shapes=[pltpu.VMEM((HGRN_DK, HGRN_DK), F32), pltpu.VMEM((cb, HGRN_DK, HGRN_DK), F32)],
        compiler_params=_params(("arbitrary", "arbitrary")),
    )(proj, proj, proj, proj, lb, ng, states, dy)


def _lb_fwd(lb_logits):
    def lb_of(l0, l1):
        m = jnp.maximum(l0, l1)
        e0, e1 = jnp.exp(l0 - m), jnp.exp(l1 - m)
        return e0 / (e0 + e1)

    def body(l_ref, o_ref):
        o_ref[...] = lb_of(l_ref[0:1, :], l_ref[1:2, :])

    lb = pl.pallas_call(body, name="lb_fwd", out_shape=jax.ShapeDtypeStruct((1, D_MODEL), F32))(lb_logits)
    return lb, lb_of


def _gate_specs(tm):
    return [pl.BlockSpec((tm, 512), lambda i, c=c: (i, c)) for c in (COL_GA, COL_GA + 1, COL_GH, COL_GH + 1)]


def _mix_fwd(y_attn, y_hgrn, proj, x1, w_pa, w_ph, w_out, g, b):
    n = x1.shape[0]
    tm = min(MIX_TILE, n)

    def body(ya_ref, yh_ref, ga0, ga1, gh0, gh1, x_ref, wpa, wph, wo, g_ref, b_ref, z_ref, o_ref):
        ya = _mm(ya_ref[...], wpa[...])
        yh = _mm(yh_ref[...], wph[...])
        ga = jnp.concatenate([ga0[...], ga1[...]], axis=1)
        gh = jnp.concatenate([gh0[...], gh1[...]], axis=1)
        merged = _sig(ga) * ya + _sig(gh) * yh
        z = DEEPNORM_ALPHA * x_ref[...] + _mm(_bf(merged), wo[...])
        z_ref[...] = z
        o_ref[...] = _ln(z, g_ref[...], b_ref[...])

    row = pl.BlockSpec((tm, D_MODEL), lambda i: (i, 0))
    sq = _full((D_MODEL, D_MODEL))
    vec = _full((1, D_MODEL))
    return pl.pallas_call(
        body, name="mix_fwd", grid=(n // tm,),
        in_specs=[row, row, *_gate_specs(tm), row, sq, sq, sq, vec, vec],
        out_specs=[row, row], out_shape=[jax.ShapeDtypeStruct((n, D_MODEL), F32)] * 2,
        compiler_params=_params(("arbitrary",)),
    )(y_attn, y_hgrn, proj, proj, proj, proj, x1, w_pa, w_ph, w_out, g, b)


def _mix_bwd(dx2, z2, y_attn, y_hgrn, proj, w_pa, w_ph, w_out, g, b):
    n = z2.shape[0]
    tm = min(MIX_TILE, n)

    def body(do_ref, z_ref, ya_ref, yh_ref, ga0, ga1, gh0, gh1, wpa, wph, wo, g_ref, b_ref,
             dz_ref, dzb_ref, mg_ref, dya_ref, dyh_ref, dyat_ref, dyhg_ref, dgt_ref, dg_ref, db_ref):
        _, vjp = jax.vjp(_ln, z_ref[...], g_ref[...], b_ref[...])
        dz, dg, db = vjp(do_ref[...])

        @pl.when(pl.program_id(0) == 0)
        def _():
            dg_ref[...] = jnp.zeros_like(dg_ref)
            db_ref[...] = jnp.zeros_like(db_ref)

        dg_ref[...] += dg
        db_ref[...] += db
        dz_ref[...] = dz
        ya = _mm(ya_ref[...], wpa[...])
        yh = _mm(yh_ref[...], wph[...])
        ga = jnp.concatenate([ga0[...], ga1[...]], axis=1)
        gh = jnp.concatenate([gh0[...], gh1[...]], axis=1)

        def merge(ga, gh, ya, yh):
            return _sig(ga) * ya + _sig(gh) * yh

        merged, merge_vjp = jax.vjp(merge, ga, gh, ya, yh)
        mg_ref[...] = _bf(merged)
        dzb = _bf(dz)
        dzb_ref[...] = dzb
        dmerged = _mm_nt(dzb, wo[...])
        dga, dgh, dya, dyh = merge_vjp(dmerged)
        dya = _bf(dya)
        dyh = _bf(dyh)
        dya_ref[...] = dya
        dyh_ref[...] = dyh
        dgt_ref[:, :D_MODEL] = _bf(dga)
        dgt_ref[:, D_MODEL:] = _bf(dgh)
        dyat_ref[...] = _mm_nt(dya, wpa[...])
        dyhg_ref[...] = _mm_nt(dyh, wph[...])

    row = pl.BlockSpec((tm, D_MODEL), lambda i: (i, 0))
    row2 = pl.BlockSpec((tm, 2 * D_MODEL), lambda i: (i, 0))
    sq = _full((D_MODEL, D_MODEL))
    vec = _full((1, D_MODEL))
    f32_row = jax.ShapeDtypeStruct((n, D_MODEL), F32)
    bf_row = jax.ShapeDtypeStruct((n, D_MODEL), BF16)
    vec_shape = jax.ShapeDtypeStruct((1, D_MODEL), F32)
    return pl.pallas_call(
        body, name="mix_bwd", grid=(n // tm,),
        in_specs=[row, row, row, row, *_gate_specs(tm), sq, sq, sq, vec, vec],
        out_specs=[row, row, row, row, row, row, row, row2, vec, vec],
        out_shape=[f32_row, bf_row, bf_row, bf_row, bf_row, f32_row, f32_row,
                   jax.ShapeDtypeStruct((n, 2 * D_MODEL), BF16), vec_shape, vec_shape],
        compiler_params=_params(("arbitrary",)),
    )(dx2, z2, y_attn, y_hgrn, proj, proj, proj, proj, w_pa, w_ph, w_out, g, b)


def _position():
    x, y, c = lax.axis_index("x"), lax.axis_index("y"), lax.axis_index("c")
    chips = [(1 - x, y), (x, 1 - y), (1 - x, 1 - y)]
    return x, y, c, chips


def _any_specs(k):
    return [pl.BlockSpec(memory_space=pl.ANY)] * k


class _GatherWeights:
    def __init__(self, shards):
        nw = len(shards)
        self.inputs = list(shards)
        self.out_shape = [jax.ShapeDtypeStruct((N_CHIPS, *s.shape), s.dtype) for s in shards]
        self.scratch = [pltpu.SemaphoreType.DMA((nw,)), pltpu.SemaphoreType.DMA((nw * 6,)),
                        pltpu.SemaphoreType.DMA((nw * 6,))]

    def _copies(self, ins, outs, sems):
        nw = len(ins)
        local_sem, send_sem, recv_sem = sems
        x, y, c, chips = _position()
        me = 2 * x + y
        sibling = (x, y, 1 - c)
        half_rows = [s.shape[0] // 2 for s in self.inputs]

        def half(w, chip_idx, which):
            return outs[w].at[chip_idx, pl.ds(which * half_rows[w], half_rows[w])]

        def remote(w, k, src, dst, to):
            return pltpu.make_async_remote_copy(src_ref=src, dst_ref=dst, send_sem=send_sem.at[w * 6 + k],
                                                recv_sem=recv_sem.at[w * 6 + k], device_id=to, device_id_type=MESH)

        local = [pltpu.make_async_copy(ins[w], outs[w].at[me], local_sem.at[w]) for w in range(nw)]
        first = [remote(w, j, ins[w].at[pl.ds(c * half_rows[w], half_rows[w])], half(w, me, c), (px, py, c))
                 for w in range(nw) for j, (px, py) in enumerate(chips)]
        landed = [half(w, 2 * px + py, c) for w in range(nw) for (px, py) in chips]
        arrive = [remote(w, j, landed[w * 3 + j], landed[w * 3 + j], (px, py, c))
                  for w in range(nw) for j, (px, py) in enumerate(chips)]
        passed = [remote(w, 3 + j, landed[w * 3 + j], landed[w * 3 + j], sibling) for w in range(nw) for j in range(3)]
        from_sibling = [remote(w, 3 + j, half(w, 2 * px + py, 1 - c), half(w, 2 * px + py, 1 - c), sibling)
                        for w in range(nw) for j, (px, py) in enumerate(chips)]
        return local, first, arrive, passed, from_sibling

    def start(self, ins, outs, sems):
        local, first, _, _, _ = self._copies(ins, outs, sems)
        for cp in local + first:
            cp.start()

    def finish(self, ins, outs, sems):
        local, first, arrive, passed, from_sibling = self._copies(ins, outs, sems)
        for cp_in, cp_on in zip(arrive, passed):
            cp_in.wait_recv()
            cp_on.start()
        for cp in from_sibling:
            cp.wait_recv()
        for cp in first + passed:
            cp.wait_send()
        for cp in local:
            cp.wait()


class _ExchangeGrads:
    def __init__(self, grads):
        nw = len(grads)
        self.inputs = list(grads)
        self.out_shape = [jax.ShapeDtypeStruct(g.shape, g.dtype) for g in grads]
        self.scratch = [pltpu.SemaphoreType.DMA((nw,)), pltpu.SemaphoreType.DMA((nw * 3,)),
                        pltpu.SemaphoreType.DMA((nw * 3,))]

    def _copies(self, ins, outs, sems):
        nw = len(ins)
        local_sem, send_sem, recv_sem = sems
        x, y, c, chips = _position()
        me = 2 * x + y

        def remote(w, j, src, dst, chip):
            return pltpu.make_async_remote_copy(src_ref=src, dst_ref=dst, send_sem=send_sem.at[w * 3 + j],
                                                recv_sem=recv_sem.at[w * 3 + j], device_id=(*chip, c),
                                                device_id_type=MESH)

        local = [pltpu.make_async_copy(ins[w].at[me], outs[w].at[me], local_sem.at[w]) for w in range(nw)]
        sends = [remote(w, j, ins[w].at[2 * px + py], outs[w].at[me], (px, py))
                 for w in range(nw) for j, (px, py) in enumerate(chips)]
        arrive = [remote(w, j, outs[w].at[2 * px + py], outs[w].at[2 * px + py], (px, py))
                  for w in range(nw) for j, (px, py) in enumerate(chips)]
        return local, sends, arrive

    def start(self, ins, outs, sems):
        local, sends, _ = self._copies(ins, outs, sems)
        for cp in local + sends:
            cp.start()

    def finish(self, ins, outs, sems):
        local, sends, arrive = self._copies(ins, outs, sems)
        for cp in arrive:
            cp.wait_recv()
        for cp in sends:
            cp.wait_send()
        for cp in local:
            cp.wait()


def _hosted(body, comm, *, name, grid, in_specs, out_specs, out_shape, scratch_shapes, compiler_params, args):
    if comm is None:
        res = pl.pallas_call(body, name=name, grid=grid, in_specs=in_specs, out_specs=out_specs, out_shape=out_shape,
                             scratch_shapes=scratch_shapes, compiler_params=compiler_params)(*args)
        return list(res), []
    n_in, n_out, n_scr = len(in_specs), len(out_specs), len(scratch_shapes)
    c_in, c_out = len(comm.inputs), len(comm.out_shape)

    def hosted_body(*refs):
        refs = list(refs)
        cut = lambda k: (refs[:k], refs[k:])
        main_in, refs = cut(n_in)
        comm_in, refs = cut(c_in)
        main_out, refs = cut(n_out)
        comm_out, refs = cut(c_out)
        main_scr, comm_scr = cut(n_scr)
        ids = [pl.program_id(a) for a in range(len(grid))]
        first = functools.reduce(jnp.logical_and, [i == 0 for i in ids])
        last = functools.reduce(jnp.logical_and, [i == g - 1 for i, g in zip(ids, grid)])

        @pl.when(first)
        def _():
            comm.start(comm_in, comm_out, comm_scr)

        body(*main_in, *main_out, *main_scr)

        @pl.when(last)
        def _():
            comm.finish(comm_in, comm_out, comm_scr)

    res = pl.pallas_call(
        hosted_body, name=name, grid=grid, in_specs=[*in_specs, *_any_specs(c_in)],
        out_specs=[*out_specs, *_any_specs(c_out)], out_shape=[*out_shape, *comm.out_shape],
        scratch_shapes=[*scratch_shapes, *comm.scratch], compiler_params=compiler_params,
    )(*args, *comm.inputs)
    return list(res[:n_out]), list(res[n_out:])


def _run_comm(name, comm):
    def body(*refs):
        refs = list(refs)
        c_in, c_out = len(comm.inputs), len(comm.out_shape)
        ins, outs, sems = refs[:c_in], refs[c_in:c_in + c_out], refs[c_in + c_out:]
        comm.start(ins, outs, sems)
        comm.finish(ins, outs, sems)

    return list(pl.pallas_call(
        body, name=name, in_specs=_any_specs(len(comm.inputs)), out_specs=_any_specs(len(comm.out_shape)),
        out_shape=comm.out_shape, scratch_shapes=comm.scratch,
    )(*comm.inputs))


def _sum_slots(name, slots):
    _, rows, cols = slots.shape
    tr = _update_rows(rows)

    def body(s_ref, o_ref):
        acc = s_ref[0].astype(F32)
        for i in range(1, N_CHIPS):
            acc = acc + s_ref[i].astype(F32)
        o_ref[...] = acc

    return pl.pallas_call(
        body, name=name, grid=(rows // tr,),
        in_specs=[pl.BlockSpec((N_CHIPS, tr, cols), lambda i: (0, i, 0))],
        out_specs=pl.BlockSpec((tr, cols), lambda i: (i, 0)),
        out_shape=jax.ShapeDtypeStruct((rows, cols), F32),
        compiler_params=_params(("arbitrary",)),
    )(slots)


def _swap_with_sibling(parts):
    nw = len(parts)

    def body(*refs):
        ins, outs = refs[:nw], refs[nw:2 * nw]
        send_sem, recv_sem = refs[2 * nw:]
        x, y, c, _ = _position()
        copies = [pltpu.make_async_remote_copy(src_ref=ins[w], dst_ref=outs[w], send_sem=send_sem.at[w],
                                               recv_sem=recv_sem.at[w], device_id=(x, y, 1 - c), device_id_type=MESH)
                  for w in range(nw)]
        for cp in copies:
            cp.start()
        for cp in copies:
            cp.wait()

    return pl.pallas_call(
        body, name="swap_with_sibling",
        in_specs=_any_specs(nw), out_specs=_any_specs(nw),
        out_shape=[jax.ShapeDtypeStruct(p.shape, p.dtype) for p in parts],
        scratch_shapes=[pltpu.SemaphoreType.DMA((nw,)), pltpu.SemaphoreType.DMA((nw,))],
    )(*parts)


def _sum_small(part):
    def body(p_ref, o_ref, buf, send_sem, recv_sem):
        x, y, c, _ = _position()
        me = 4 * x + 2 * y + c
        buf[me] = p_ref[...]
        copies = []
        for k in range(1, N_DEV):
            peer = tuple(1 - v if (k >> s) & 1 else v for v, s in ((x, 2), (y, 1), (c, 0)))
            copies.append(pltpu.make_async_remote_copy(src_ref=p_ref, dst_ref=buf.at[me], send_sem=send_sem.at[k - 1],
                                                       recv_sem=recv_sem.at[k - 1], device_id=peer, device_id_type=MESH))
        for cp in copies:
            cp.start()
        for cp in copies:
            cp.wait()
        acc = buf[0]
        for d in range(1, N_DEV):
            acc = acc + buf[d]
        o_ref[...] = acc

    vm = pl.BlockSpec(memory_space=pltpu.VMEM)
    return pl.pallas_call(
        body, name="sum_small", in_specs=[vm], out_specs=vm,
        out_shape=jax.ShapeDtypeStruct((1, SM_LEN), F32),
        scratch_shapes=[pltpu.VMEM((N_DEV, 1, SM_LEN), F32), pltpu.SemaphoreType.DMA((N_DEV - 1,)),
                        pltpu.SemaphoreType.DMA((N_DEV - 1,))],
    )(part)


def _adamw(w, g, m, v):
    m = ADAM_B1 * m + (1.0 - ADAM_B1) * g
    v = ADAM_B2 * v + (1.0 - ADAM_B2) * (g * g)
    m_hat = m / (1.0 - ADAM_B1 ** ADAM_STEP)
    v_hat = v / (1.0 - ADAM_B2 ** ADAM_STEP)
    delta = -ADAM_LR * (m_hat / (jnp.sqrt(v_hat) + ADAM_EPS) + ADAM_WD * w)
    return delta, m, v


def _adam_big(name, p_own, p_sibling, w, m, v):
    rows, cols = w.shape
    tr = _update_rows(rows)

    def body(p_ref, q_ref, w_ref, m_ref, v_ref, g_ref, d_ref, nm_ref, nv_ref):
        g = p_ref[...] + q_ref[...]
        g_ref[...] = g
        d_ref[...], nm_ref[...], nv_ref[...] = _adamw(w_ref[...], g, m_ref[...], v_ref[...])

    spec = pl.BlockSpec((tr, cols), lambda i: (i, 0))
    return pl.pallas_call(
        body, name=name, grid=(rows // tr,), in_specs=[spec] * 5, out_specs=[spec] * 4,
        out_shape=[jax.ShapeDtypeStruct((rows, cols), F32)] * 4,
        compiler_params=_params(("arbitrary",)),
    )(p_own, p_sibling, w, m, v)


def _adam_small(total, wvec, mvec, vvec, lb_of):
    def body(t_ref, w_ref, m_ref, v_ref, g_ref, d_ref, nm_ref, nv_ref):
        g_ref[:, :SM_LB] = t_ref[:, :SM_LB]
        _, vjp = jax.vjp(lb_of, w_ref[:, PK_LB0:PK_LB1], w_ref[:, PK_LB1:PK_LEN])
        dl0, dl1 = vjp(t_ref[:, SM_LB:SM_LOSS])
        g_ref[:, PK_LB0:PK_LB1] = dl0
        g_ref[:, PK_LB1:PK_LEN] = dl1
        d_ref[...], nm_ref[...], nv_ref[...] = _adamw(w_ref[...], g_ref[...], m_ref[...], v_ref[...])

    return pl.pallas_call(body, name="adam_small", out_shape=[jax.ShapeDtypeStruct((1, PK_LEN), F32)] * 4)(
        total, wvec, mvec, vvec)


_BIG = ("ffn1_w1", "ffn1_w3", "ffn1_w2", "w_in", "w_proj_attn", "w_proj_hgrn", "w_out", "ffn2_w1", "ffn2_w3", "ffn2_w2")
_SMALL = ("ln1_g", "ln1_b", "ln2_g", "ln2_b", "ln3_g", "ln3_b", "b_in", "attn_sinks", "hgrn_norm_g", "hgrn_lb_logits")
_ORDER = ("ln1_g", "ln1_b", "ffn1_w1", "ffn1_w3", "ffn1_w2", "ln2_g", "ln2_b", "w_in", "b_in", "attn_sinks",
          "hgrn_lb_logits", "hgrn_norm_g", "w_proj_attn", "w_proj_hgrn", "w_out", "ln3_g", "ln3_b",
          "ffn2_w1", "ffn2_w3", "ffn2_w2")


def _pad_to(a, rows, cols):
    return jnp.pad(a, ((0, rows - a.shape[0]), (0, cols - a.shape[1])))


def _send_form(name, w):
    wb = _bf(w)
    if name.endswith(("w1", "w3")) and name.startswith("ffn"):
        return _pad_to(wb, D_MODEL, FF_PAD)
    if name.endswith("w2") and name.startswith("ffn"):
        return _pad_to(wb, FF_PAD, D_MODEL)
    return wb


def _pack_small(p):
    pieces = [p[k].reshape(1, -1) for k in ("ln1_g", "ln1_b", "ln2_g", "ln2_b", "ln3_g", "ln3_b", "b_in")]
    pieces.append(_pad_to(p["attn_sinks"].reshape(1, -1), 1, LANES))
    pieces.append(p["hgrn_norm_g"].reshape(1, -1))
    pieces.append(p["hgrn_lb_logits"].reshape(1, -1))
    return jnp.concatenate(pieces, axis=1)


def _unpack_small(vec, like):
    out = {}
    for i, k in enumerate(("ln1_g", "ln1_b", "ln2_g", "ln2_b", "ln3_g", "ln3_b")):
        out[k] = vec[:, i * D_MODEL:(i + 1) * D_MODEL]
    out["b_in"] = vec[:, SM_BIN:SM_SINK]
    out["attn_sinks"] = vec[:, SM_SINK:SM_SINK + N_Q_HEADS]
    out["hgrn_norm_g"] = vec[:, SM_NG:SM_LB]
    out["hgrn_lb_logits"] = vec[:, PK_LB0:PK_LEN].reshape(2, D_MODEL)
    return {k: v.reshape(like[k].shape) for k, v in out.items()}


def kernel(x, ln1_g, ln1_b, ffn1_w1, ffn1_w3, ffn1_w2, ln2_g, ln2_b, w_in, b_in, attn_sinks, hgrn_lb_logits, hgrn_norm_g, w_proj_attn, w_proj_hgrn, w_out, ln3_g, ln3_b, ffn2_w1, ffn2_w3, ffn2_w2, loss_target, m_ln1_g, m_ln1_b, m_ffn1_w1, m_ffn1_w3, m_ffn1_w2, m_ln2_g, m_ln2_b, m_w_in, m_b_in, m_attn_sinks, m_hgrn_lb_logits, m_hgrn_norm_g, m_w_proj_attn, m_w_proj_hgrn, m_w_out, m_ln3_g, m_ln3_b, m_ffn2_w1, m_ffn2_w3, m_ffn2_w2, v_ln1_g, v_ln1_b, v_ffn1_w1, v_ffn1_w3, v_ffn1_w2, v_ln2_g, v_ln2_b, v_w_in, v_b_in, v_attn_sinks, v_hgrn_lb_logits, v_hgrn_norm_g, v_w_proj_attn, v_w_proj_hgrn, v_w_out, v_ln3_g, v_ln3_b, v_ffn2_w1, v_ffn2_w3, v_ffn2_w2):
    w = dict(ln1_g=ln1_g, ln1_b=ln1_b, ffn1_w1=ffn1_w1, ffn1_w3=ffn1_w3, ffn1_w2=ffn1_w2, ln2_g=ln2_g, ln2_b=ln2_b,
             w_in=w_in, b_in=b_in, attn_sinks=attn_sinks, hgrn_lb_logits=hgrn_lb_logits, hgrn_norm_g=hgrn_norm_g,
             w_proj_attn=w_proj_attn, w_proj_hgrn=w_proj_hgrn, w_out=w_out, ln3_g=ln3_g, ln3_b=ln3_b,
             ffn2_w1=ffn2_w1, ffn2_w3=ffn2_w3, ffn2_w2=ffn2_w2)
    mom = dict(ln1_g=m_ln1_g, ln1_b=m_ln1_b, ffn1_w1=m_ffn1_w1, ffn1_w3=m_ffn1_w3, ffn1_w2=m_ffn1_w2, ln2_g=m_ln2_g,
               ln2_b=m_ln2_b, w_in=m_w_in, b_in=m_b_in, attn_sinks=m_attn_sinks, hgrn_lb_logits=m_hgrn_lb_logits,
               hgrn_norm_g=m_hgrn_norm_g, w_proj_attn=m_w_proj_attn, w_proj_hgrn=m_w_proj_hgrn, w_out=m_w_out,
               ln3_g=m_ln3_g, ln3_b=m_ln3_b, ffn2_w1=m_ffn2_w1, ffn2_w3=m_ffn2_w3, ffn2_w2=m_ffn2_w2)
    var = dict(ln1_g=v_ln1_g, ln1_b=v_ln1_b, ffn1_w1=v_ffn1_w1, ffn1_w3=v_ffn1_w3, ffn1_w2=v_ffn1_w2, ln2_g=v_ln2_g,
               ln2_b=v_ln2_b, w_in=v_w_in, b_in=v_b_in, attn_sinks=v_attn_sinks, hgrn_lb_logits=v_hgrn_lb_logits,
               hgrn_norm_g=v_hgrn_norm_g, w_proj_attn=v_w_proj_attn, w_proj_hgrn=v_w_proj_hgrn, w_out=v_w_out,
               ln3_g=v_ln3_g, ln3_b=v_ln3_b, ffn2_w1=v_ffn2_w1, ffn2_w3=v_ffn2_w3, ffn2_w2=v_ffn2_w2)

    n_tok = x.shape[1]
    x0 = x.reshape(n_tok, D_MODEL)
    target = loss_target.reshape(n_tok, D_MODEL)

    shard = {k: _send_form(k, w[k][0]) for k in _BIG}
    gather = lambda keys: _GatherWeights([shard[k] for k in keys])
    slots = {}
    exchange = lambda keys: _ExchangeGrads([big[k] for k in keys])
    ffn1_keys = ("ffn1_w1", "ffn1_w3", "ffn1_w2")
    mixer_keys = ("w_in", "w_proj_attn", "w_proj_hgrn", "w_out")
    ffn2_keys = ("ffn2_w1", "ffn2_w3", "ffn2_w2")
    f1 = _run_comm("gather_ffn1", gather(ffn1_keys))

    tabs = _rope_tables(n_tok)
    lb, lb_of = _lb_fwd(hgrn_lb_logits)
    (z1, x1, x0b, h1_1, h3_1, x1b), (w_in_g, w_pa, w_ph, w_o) = _ffn_fwd(
        "ffn1_fwd", x0, *f1, ln1_g, ln1_b, comm=gather(mixer_keys))
    w_pa, w_ph, w_o = (t.reshape(D_MODEL, D_MODEL) for t in (w_pa, w_ph, w_o))
    (proj,), f2 = _in_proj(x1b, w_in_g, b_in, comm=gather(ffn2_keys))
    y_attn = _attn_fwd(proj, tabs, attn_sinks)
    y_hgrn, states = _hgrn_fwd(proj, lb, hgrn_norm_g)
    z2, x2 = _mix_fwd(y_attn, y_hgrn, proj, x1, w_pa, w_ph, w_o, ln2_g, ln2_b)
    (z3, dy, x2b, h1_2, h3_2, loss_part), _ = _ffn_fwd("ffn2_fwd", x2, *f2, ln3_g, ln3_b, target=target)

    big = {}
    small = {}
    (dx2, a2, dh1_2, dh3_2, df2, small["ln3_g"], small["ln3_b"]), _ = _ffn_bwd(
        "ffn2_bwd", h1_2, h3_2, z3, dy, *f2, ln3_g, ln3_b)
    (big["ffn2_w1"],), _ = _grad_matmul("ffn2_dw1", x2b, "shared", D_MODEL, dh1_2, "batch", FF_PAD)
    (big["ffn2_w3"],), _ = _grad_matmul("ffn2_dw3", x2b, "shared", D_MODEL, dh3_2, "batch", FF_PAD)
    (big["ffn2_w2"],), _ = _grad_matmul("ffn2_dw2", a2, "batch", FF_PAD, df2, "shared", D_MODEL)
    (dz2, dz2b, merged, dya, dyh, dy_attn, dy_hgrn, dgates, small["ln2_g"], small["ln2_b"]) = _mix_bwd(
        dx2, z2, y_attn, y_hgrn, proj, w_pa, w_ph, w_o, ln2_g, ln2_b)
    (big["w_out"],), _ = _grad_matmul("dw_out", merged, "cols", PROJ_SHARD, dz2b, "shared", D_MODEL)
    (big["w_proj_attn"],), _ = _grad_matmul("dw_proj_attn", y_attn, "cols", PROJ_SHARD, dya, "shared", D_MODEL)
    (big["w_proj_hgrn"],), _ = _grad_matmul("dw_proj_hgrn", y_hgrn, "cols", PROJ_SHARD, dyh, "shared", D_MODEL)
    dfl, dqh, dih, dog, dlb, small["hgrn_norm_g"] = _hgrn_bwd(proj, lb, hgrn_norm_g, states, dy_hgrn)
    early_keys = ffn2_keys + mixer_keys[1:]
    (dq, dkv, dsinks), early = _attn_bwd(proj, dy_attn, tabs, attn_sinks, comm=exchange(early_keys))
    slots.update(zip(early_keys, early))
    dproj = jnp.concatenate([dq, dkv, dfl, dqh, dih, dog, dgates], axis=1)
    (big["w_in"], small["b_in"]), _ = _grad_matmul(
        "dw_in", x1b, "shared", D_MODEL, dproj, "cols", IN_SHARD, colsum=True)
    dx1 = _in_proj_dx(dproj, w_in_g, dz2)
    (grad_x, a1, dh1_1, dh3_1, df1, small["ln1_g"], small["ln1_b"]), (slots["w_in"],) = _ffn_bwd(
        "ffn1_bwd", h1_1, h3_1, z1, dx1, *f1, ln1_g, ln1_b, comm=exchange(("w_in",)))
    (big["ffn1_w1"],), _ = _grad_matmul("ffn1_dw1", x0b, "shared", D_MODEL, dh1_1, "batch", FF_PAD)
    (big["ffn1_w3"],), (slots["ffn1_w1"],) = _grad_matmul(
        "ffn1_dw3", x0b, "shared", D_MODEL, dh3_1, "batch", FF_PAD, comm=exchange(("ffn1_w1",)))
    (big["ffn1_w2"],), (slots["ffn1_w3"],) = _grad_matmul(
        "ffn1_dw2", a1, "batch", FF_PAD, df1, "shared", D_MODEL, comm=exchange(("ffn1_w3",)))
    (slots["ffn1_w2"],) = _run_comm("exchange_last", exchange(("ffn1_w2",)))

    partial = []
    for k in _BIG:
        p = _sum_slots("sum_" + k, slots[k])
        partial.append(p[:w[k].shape[1], :w[k].shape[2]])
    from_sibling = _swap_with_sibling(partial)

    outs = {"grad": {}, "delta": {}, "m": {}, "v": {}}
    for k, p, q in zip(_BIG, partial, from_sibling):
        res = _adam_big("adam_" + k, p, q, w[k][0], mom[k][0], var[k][0])
        for kind, r in zip(("grad", "delta", "m", "v"), res):
            outs[kind][k] = r.reshape(w[k].shape)

    part = jnp.concatenate(
        [small[k] for k in ("ln1_g", "ln1_b", "ln2_g", "ln2_b", "ln3_g", "ln3_b", "b_in")]
        + [dsinks, small["hgrn_norm_g"], dlb, loss_part], axis=1)
    total = _sum_small(part)
    res = _adam_small(total, _pack_small(w), _pack_small(mom), _pack_small(var), lb_of)
    for kind, r in zip(("grad", "delta", "m", "v"), res):
        outs[kind].update(_unpack_small(r, w))
    loss = total[0, SM_LOSS]

    return (loss, grad_x.reshape(x.shape), *[outs["grad"][k] for k in _ORDER], *[outs["delta"][k] for k in _ORDER],
            *[outs["m"][k] for k in _ORDER], *[outs["v"][k] for k in _ORDER])
```

```python
import functools

import jax
import jax.numpy as jnp
from jax import lax
from jax.experimental import pallas as pl
from jax.experimental.pallas import tpu as pltpu

F32 = jnp.float32
BF16 = jnp.bfloat16

D_MODEL = 1024
N_Q_HEADS = 16
N_KV_HEADS = 4
HEAD_DIM = 64
ATTN_BLOCK = 128
ROPE_THETA = 500000.0
ROPE_DIM = HEAD_DIM // 4
HGRN_HEADS = 8
HGRN_DK = 128
HGRN_CHUNK = 64
D_FF = 2816
D_IN = 7680
DEEPNORM_ALPHA = 2 ** 0.25
LN_EPS = 1e-5
RMS_EPS = 1e-6
NEG_INF = -1e30

ADAM_LR = 0.001
ADAM_B1 = 0.9
ADAM_B2 = 0.999
ADAM_EPS = 1e-08
ADAM_WD = 0.01
ADAM_STEP = 10

N_CHIPS = 4
N_DEV = 8
FF_SHARD = D_FF // N_CHIPS
LANES = 128
MXU_COLS = 256
FF_PAD = -(-FF_SHARD // MXU_COLS) * MXU_COLS
IN_SHARD = D_IN // N_CHIPS
PROJ_SHARD = D_MODEL // N_CHIPS
ROW_TILE = 512
GRAD_ROWS = 2048
MIX_TILE = 256
UPDATE_ROWS = 128
HGRN_CHUNKS_PER_STEP = 8
VMEM_LIMIT = 56 * 1024 * 1024

COL_K = 1024 // 256
COL_V = 1280 // 256
COL_F = 1536 // LANES
COL_QH = 2560 // LANES
COL_IH = 3584 // LANES
COL_OG = 4608 // LANES
COL_GA = 5632 // 512
COL_GH = 6656 // 512

SM_LN = 0
SM_BIN = 6 * D_MODEL
SM_SINK = SM_BIN + D_IN
SM_NG = SM_SINK + LANES
SM_LB = SM_NG + LANES
SM_LOSS = SM_LB + D_MODEL
SM_LEN = SM_LOSS + LANES
PK_LB0 = SM_LB
PK_LB1 = SM_LB + D_MODEL
PK_LEN = PK_LB1 + D_MODEL

MESH = pl.DeviceIdType.MESH


def _mm(a, b):
    return lax.dot_general(a, b, (((1,), (0,)), ((), ())), preferred_element_type=F32)


def _mm_nt(a, b):
    return lax.dot_general(a, b, (((1,), (1,)), ((), ())), preferred_element_type=F32)


def _mm_tn(a, b):
    return lax.dot_general(a, b, (((0,), (0,)), ((), ())), preferred_element_type=F32)


def _bf(v):
    return v.astype(BF16)


def _sig(v):
    return jax.nn.sigmoid(v)


def _ln(z, g, b):
    mu = jnp.mean(z, axis=-1, keepdims=True)
    zc = z - mu
    var = jnp.mean(zc * zc, axis=-1, keepdims=True)
    return zc * lax.rsqrt(var + LN_EPS) * g + b


def _swiglu_act(h1, h3):
    return (h1 * _sig(h1)) * h3


def _params(sem=None):
    return pltpu.CompilerParams(dimension_semantics=sem, vmem_limit_bytes=VMEM_LIMIT)


def _full(shape):
    nd = len(shape)
    return pl.BlockSpec(shape, lambda *_: (0,) * nd)


def _update_rows(rows):
    return max(t for t in range(8, UPDATE_ROWS + 1, 8) if rows % t == 0)


def _ffn_fwd(name, x, w1g, w3g, w2g, g, b, target=None, comm=None):
    n = x.shape[0]
    tm = min(ROW_TILE, n)
    final = target is not None

    def body(*refs):
        if final:
            x_ref, w1_ref, w3_ref, w2_ref, g_ref, b_ref, t_ref, z_ref, o_ref, xb, h1_ref, h3_ref, loss_ref, acc = refs
        else:
            x_ref, w1_ref, w3_ref, w2_ref, g_ref, b_ref, z_ref, o_ref, xb, h1_ref, h3_ref, ob_ref, acc = refs
        i = pl.program_id(0)
        j = pl.program_id(1)

        @pl.when(j == 0)
        def _():
            xb[...] = _bf(x_ref[...])
            acc[...] = jnp.zeros_like(acc)

        xv = xb[...]
        part = None
        for c in range(FF_PAD // MXU_COLS):
            cols = slice(c * MXU_COLS, (c + 1) * MXU_COLS)
            h1 = _mm_nt(xv, w1_ref[0, cols, :])
            h3 = _mm_nt(xv, w3_ref[0, cols, :])
            h1_ref[0, :, cols] = _bf(h1)
            h3_ref[0, :, cols] = _bf(h3)
            d = _mm(_bf(_swiglu_act(h1, h3)), w2_ref[0, cols, :])
            part = d if part is None else part + d
        acc[...] += part

        @pl.when(j == N_CHIPS - 1)
        def _():
            z = DEEPNORM_ALPHA * x_ref[...] + 0.5 * acc[...]
            z_ref[...] = z
            y = _ln(z, g_ref[...], b_ref[...])
            if final:
                e = y - t_ref[...]

                @pl.when(i == 0)
                def _():
                    loss_ref[...] = jnp.zeros_like(loss_ref)

                loss_ref[...] += jnp.sum(e * e) * (0.5 / D_MODEL)
                o_ref[...] = e * (1.0 / D_MODEL)
            else:
                o_ref[...] = y
                ob_ref[...] = _bf(y)

    row = pl.BlockSpec((tm, D_MODEL), lambda i, j: (i, 0))
    wrow = pl.BlockSpec((1, FF_PAD, D_MODEL), lambda i, j: (j, 0, 0))
    vec = pl.BlockSpec((1, D_MODEL), lambda i, j: (0, 0))
    in_specs = [row, wrow, wrow, wrow, vec, vec]
    args = [x, w1g, w3g, w2g, g, b]
    hid = pl.BlockSpec((1, tm, FF_PAD), lambda i, j: (j, i, 0))
    hid_shape = jax.ShapeDtypeStruct((N_CHIPS, n, FF_PAD), BF16)
    out_specs = [row, row, row, hid, hid]
    out_shape = ([jax.ShapeDtypeStruct((n, D_MODEL), F32)] * 2 + [jax.ShapeDtypeStruct((n, D_MODEL), BF16)]
                 + [hid_shape] * 2)
    if final:
        in_specs.append(row)
        args.append(target)
        out_specs.append(pl.BlockSpec((1, LANES), lambda i, j: (0, 0)))
        out_shape.append(jax.ShapeDtypeStruct((1, LANES), F32))
    else:
        out_specs.append(row)
        out_shape.append(jax.ShapeDtypeStruct((n, D_MODEL), BF16))
    return _hosted(
        body, comm, name=name, grid=(n // tm, N_CHIPS), in_specs=in_specs, out_specs=out_specs, out_shape=out_shape,
        scratch_shapes=[pltpu.VMEM((tm, D_MODEL), F32)],
        compiler_params=_params(("arbitrary", "arbitrary")), args=args)


def _ffn_bwd(name, h1s, h3s, z, dout, w1g, w3g, w2g, g, b, comm=None):
    n = z.shape[0]
    tm = min(ROW_TILE, n)

    def body(h1_ref, h3_ref, z_ref, do_ref, w1_ref, w3_ref, w2_ref, g_ref, b_ref,
             dx_ref, a_ref, dh1_ref, dh3_ref, df_ref, dg_ref, db_ref, acc, dzs):
        i = pl.program_id(0)
        j = pl.program_id(1)

        @pl.when(j == 0)
        def _():
            acc[...] = jnp.zeros_like(acc)
            _, vjp = jax.vjp(_ln, z_ref[...], g_ref[...], b_ref[...])
            dz, dg, db = vjp(do_ref[...])
            dzs[...] = dz
            df_ref[...] = _bf(0.5 * dz)

            @pl.when(i == 0)
            def _():
                dg_ref[...] = jnp.zeros_like(dg_ref)
                db_ref[...] = jnp.zeros_like(db_ref)

            dg_ref[...] += dg
            db_ref[...] += db

        a, act_vjp = jax.vjp(_swiglu_act, h1_ref[0].astype(F32), h3_ref[0].astype(F32))
        da = _mm_nt(df_ref[...], w2_ref[0])
        dh1, dh3 = act_vjp(da)
        dh1 = _bf(dh1)
        dh3 = _bf(dh3)
        a_ref[0] = _bf(a)
        dh1_ref[0] = dh1
        dh3_ref[0] = dh3
        acc[...] += _mm(dh1, w1_ref[0]) + _mm(dh3, w3_ref[0])

        @pl.when(j == N_CHIPS - 1)
        def _():
            dx_ref[...] = DEEPNORM_ALPHA * dzs[...] + acc[...]

    row = pl.BlockSpec((tm, D_MODEL), lambda i, j: (i, 0))
    wrow = pl.BlockSpec((1, FF_PAD, D_MODEL), lambda i, j: (j, 0, 0))
    vec = pl.BlockSpec((1, D_MODEL), lambda i, j: (0, 0))
    hid = pl.BlockSpec((1, tm, FF_PAD), lambda i, j: (j, i, 0))
    hid_shape = jax.ShapeDtypeStruct((N_CHIPS, n, FF_PAD), BF16)
    return _hosted(
        body, comm, name=name, grid=(n // tm, N_CHIPS),
        in_specs=[hid, hid, row, row, wrow, wrow, wrow, vec, vec],
        out_specs=[row, hid, hid, hid, row, vec, vec],
        out_shape=[jax.ShapeDtypeStruct((n, D_MODEL), F32), hid_shape, hid_shape, hid_shape,
                   jax.ShapeDtypeStruct((n, D_MODEL), BF16),
                   jax.ShapeDtypeStruct((1, D_MODEL), F32), jax.ShapeDtypeStruct((1, D_MODEL), F32)],
        scratch_shapes=[pltpu.VMEM((tm, D_MODEL), F32), pltpu.VMEM((tm, D_MODEL), F32)],
        compiler_params=_params(("arbitrary", "arbitrary")), args=[h1s, h3s, z, dout, w1g, w3g, w2g, g, b])


def _operand_spec(arr, mode, tn, width):
    if mode == "shared":
        return pl.BlockSpec((tn, width), lambda s, k: (k, 0))
    if mode == "batch":
        return pl.BlockSpec((1, tn, width), lambda s, k: (s, k, 0))
    assert mode == "cols" and arr.shape[1] == N_CHIPS * width
    return pl.BlockSpec((tn, width), lambda s, k: (k, s))


def _grad_matmul(name, a, a_mode, ka, b, b_mode, kb, colsum=False, comm=None):
    n = a.shape[-2]
    tn = min(GRAD_ROWS, n)
    nk = n // tn

    def body(*refs):
        if colsum:
            a_ref, b_ref, o_ref, cs_ref, acc = refs
        else:
            a_ref, b_ref, o_ref, acc = refs
        k = pl.program_id(1)
        av = a_ref[0] if a_mode == "batch" else a_ref[...]
        bv = b_ref[0] if b_mode == "batch" else b_ref[...]

        @pl.when(k == 0)
        def _():
            acc[...] = jnp.zeros_like(acc)
            if colsum:
                cs_ref[...] = jnp.zeros_like(cs_ref)

        acc[...] += _mm_tn(av, bv)
        if colsum:
            cs_ref[...] += jnp.sum(bv.astype(F32), axis=0, keepdims=True)

        @pl.when(k == nk - 1)
        def _():
            o_ref[0] = _bf(acc[...])

    out_specs = [pl.BlockSpec((1, ka, kb), lambda s, k: (s, 0, 0))]
    out_shape = [jax.ShapeDtypeStruct((N_CHIPS, ka, kb), BF16)]
    if colsum:
        out_specs.append(pl.BlockSpec((1, kb), lambda s, k: (0, s)))
        out_shape.append(jax.ShapeDtypeStruct((1, N_CHIPS * kb), F32))
    return _hosted(
        body, comm, name=name, grid=(N_CHIPS, nk),
        in_specs=[_operand_spec(a, a_mode, tn, ka), _operand_spec(b, b_mode, tn, kb)],
        out_specs=out_specs, out_shape=out_shape,
        scratch_shapes=[pltpu.VMEM((ka, kb), F32)],
        compiler_params=_params(("arbitrary", "arbitrary")), args=[a, b])


def _in_proj(x1, w_in_g, b_in, comm=None):
    n = x1.shape[0]
    tm = min(ROW_TILE, n)

    def body(x_ref, w_ref, b_ref, o_ref):
        o_ref[...] = _mm(x_ref[...], w_ref[0]) + b_ref[...]

    return _hosted(
        body, comm, name="in_proj", grid=(N_CHIPS, n // tm),
        in_specs=[pl.BlockSpec((tm, D_MODEL), lambda j, i: (i, 0)),
                  pl.BlockSpec((1, D_MODEL, IN_SHARD), lambda j, i: (j, 0, 0)),
                  pl.BlockSpec((1, IN_SHARD), lambda j, i: (0, j))],
        out_specs=[pl.BlockSpec((tm, IN_SHARD), lambda j, i: (i, j))],
        out_shape=[jax.ShapeDtypeStruct((n, D_IN), F32)],
        scratch_shapes=[],
        compiler_params=_params(("arbitrary", "arbitrary")), args=[x1, w_in_g, b_in])


def _in_proj_dx(dproj, w_in_g, dz2, comm=None):
    n = dproj.shape[0]
    tm = min(ROW_TILE, n)

    def body(dp_ref, w_ref, dz_ref, o_ref, acc):
        j = pl.program_id(1)

        @pl.when(j == 0)
        def _():
            acc[...] = jnp.zeros_like(acc)

        acc[...] += _mm_nt(dp_ref[...], w_ref[0])

        @pl.when(j == N_CHIPS - 1)
        def _():
            o_ref[...] = DEEPNORM_ALPHA * dz_ref[...] + acc[...]

    return _hosted(
        body, comm, name="in_proj_dx", grid=(n // tm, N_CHIPS),
        in_specs=[pl.BlockSpec((tm, IN_SHARD), lambda i, j: (i, j)),
                  pl.BlockSpec((1, D_MODEL, IN_SHARD), lambda i, j: (j, 0, 0)),
                  pl.BlockSpec((tm, D_MODEL), lambda i, j: (i, 0))],
        out_specs=[pl.BlockSpec((tm, D_MODEL), lambda i, j: (i, 0))],
        out_shape=[jax.ShapeDtypeStruct((n, D_MODEL), F32)],
        scratch_shapes=[pltpu.VMEM((tm, D_MODEL), F32)],
        compiler_params=_params(("arbitrary", "arbitrary")), args=[dproj, w_in_g, dz2])


def _rope_tables(seq_len):
    pos = jnp.arange(seq_len, dtype=F32)
    inv_freq = ROPE_THETA ** (-jnp.arange(0, ROPE_DIM, 2, dtype=F32) / ROPE_DIM)
    ang = pos[:, None] * inv_freq[None, :]
    cos, sin = jnp.cos(ang), jnp.sin(ang)
    half = ROPE_DIM // 2
    rest = HEAD_DIM - ROPE_DIM
    ones = jnp.ones((seq_len, rest), F32)
    zeros = jnp.zeros((seq_len, rest), F32)
    zh = jnp.zeros((seq_len, half), F32)
    c = jnp.concatenate([cos, cos, ones], axis=1)
    sa = jnp.concatenate([-sin, zh, zeros], axis=1)
    sb = jnp.concatenate([zh, sin, zeros], axis=1)
    reps = LANES // HEAD_DIM
    return tuple(jnp.tile(t, (1, reps)) for t in (c, sa, sb))


def _rope(t, c, sa, sb):
    w = t.shape[1]
    reps = w // LANES
    half = ROPE_DIM // 2
    return (t * jnp.tile(c, (1, reps)) + pltpu.roll(t, w - half, 1) * jnp.tile(sa, (1, reps))
            + pltpu.roll(t, half, 1) * jnp.tile(sb, (1, reps)))


def _rope_transposed(g, c, sa, sb):
    w = g.shape[1]
    reps = w // LANES
    half = ROPE_DIM // 2
    return (g * jnp.tile(c, (1, reps)) + pltpu.roll(g * jnp.tile(sa, (1, reps)), half, 1)
            + pltpu.roll(g * jnp.tile(sb, (1, reps)), w - half, 1))


GROUP = N_Q_HEADS // N_KV_HEADS


def _attn_mask(n):
    rows = GROUP * ATTN_BLOCK
    qi = lax.broadcasted_iota(jnp.int32, (rows, 2 * ATTN_BLOCK), 0) % ATTN_BLOCK
    kj = lax.broadcasted_iota(jnp.int32, (rows, 2 * ATTN_BLOCK), 1)
    dist = qi + ATTN_BLOCK - kj
    return (dist >= 0) & (dist < ATTN_BLOCK) & (n * ATTN_BLOCK + kj - ATTN_BLOCK >= 0)


def _both_halves(t_pair, which):
    lo = lax.broadcasted_iota(jnp.int32, t_pair.shape, 1) < HEAD_DIM
    swapped = pltpu.roll(t_pair, HEAD_DIM, 1)
    return _bf(jnp.where(lo, t_pair, swapped) if which == 0 else jnp.where(lo, swapped, t_pair))


def _stack_heads(ref_or_val, kh):
    lo = lax.broadcasted_iota(jnp.int32, (ATTN_BLOCK, LANES), 1) < HEAD_DIM
    rows = []
    for gp in range(GROUP // 2):
        pair = kh * (GROUP // 2) + gp
        t = ref_or_val[:, pair * LANES:(pair + 1) * LANES]
        rows += [jnp.where(lo, t, jnp.zeros_like(t)), jnp.where(lo, jnp.zeros_like(t), t)]
    return jnp.concatenate(rows, axis=0)


def _unstack_pairs(stacked):
    lo = lax.broadcasted_iota(jnp.int32, (ATTN_BLOCK, LANES), 1) < HEAD_DIM
    b = ATTN_BLOCK
    return [jnp.where(lo, stacked[2 * gp * b:(2 * gp + 1) * b], stacked[(2 * gp + 1) * b:(2 * gp + 2) * b])
            for gp in range(GROUP // 2)]


def _sink_column(sink_ref, kh):
    row = lax.broadcasted_iota(jnp.int32, (GROUP * ATTN_BLOCK, 1), 0)
    col = jnp.full((GROUP * ATTN_BLOCK, 1), sink_ref[0, kh * GROUP + GROUP - 1], F32)
    for i in reversed(range(GROUP - 1)):
        col = jnp.where(row < (i + 1) * ATTN_BLOCK, sink_ref[0, kh * GROUP + i], col)
    return col


def _attn_probs(q_masked, k_sel, mask, sink):
    s = _mm_nt(q_masked, k_sel) * (HEAD_DIM ** -0.5)
    s = jnp.where(mask, s, NEG_INF)
    m = jnp.maximum(jnp.max(s, axis=-1, keepdims=True), sink)
    p = jnp.exp(s - m)
    e_sink = jnp.exp(sink - m)
    denom = jnp.sum(p, axis=-1, keepdims=True) + e_sink
    return p / denom, e_sink / denom


def _attn_fwd(proj, tabs, sinks):
    n_tok = proj.shape[0]
    nb = n_tok // ATTN_BLOCK
    group = N_Q_HEADS // N_KV_HEADS

    def body(q_ref, k_ref, v_ref, c_ref, sa_ref, sb_ref, sink_ref, y_ref, kprev, vprev):
        n = pl.program_id(0)

        @pl.when(n == 0)
        def _():
            kprev[...] = jnp.zeros_like(kprev)
            vprev[...] = jnp.zeros_like(vprev)

        c, sa, sb = c_ref[...], sa_ref[...], sb_ref[...]
        qr = _bf(_rope(q_ref[...], c, sa, sb))
        kr = _rope(k_ref[...], c, sa, sb)
        vc = v_ref[...]
        kk = jnp.concatenate([kprev[...], kr], axis=0)
        vv = jnp.concatenate([vprev[...], vc], axis=0)
        kprev[...] = kr
        vprev[...] = vc
        mask = _attn_mask(n)
        for kh in range(N_KV_HEADS):
            r, which = divmod(kh, 2)
            kb = _both_halves(kk[:, r * LANES:(r + 1) * LANES], which)
            vb = _both_halves(vv[:, r * LANES:(r + 1) * LANES], which)
            probs, _ = _attn_probs(_stack_heads(qr, kh), kb, mask, _sink_column(sink_ref, kh))
            for gp, out in enumerate(_unstack_pairs(_mm(_bf(probs), vb))):
                pair = kh * (GROUP // 2) + gp
                y_ref[:, pair * LANES:(pair + 1) * LANES] = _bf(out)

    blk = lambda width, col: pl.BlockSpec((ATTN_BLOCK, width), lambda n: (n, col))
    tab = pl.BlockSpec((ATTN_BLOCK, LANES), lambda n: (n, 0))
    kvw = N_KV_HEADS * HEAD_DIM
    return pl.pallas_call(
        body, name="attn_fwd", grid=(nb,),
        in_specs=[blk(D_MODEL, 0), blk(kvw, COL_K), blk(kvw, COL_V), tab, tab, tab,
                  pl.BlockSpec(memory_space=pltpu.SMEM)],
        out_specs=pl.BlockSpec((ATTN_BLOCK, D_MODEL), lambda n: (n, 0)),
        out_shape=jax.ShapeDtypeStruct((n_tok, D_MODEL), BF16),
        scratch_shapes=[pltpu.VMEM((ATTN_BLOCK, kvw), F32), pltpu.VMEM((ATTN_BLOCK, kvw), F32)],
        compiler_params=_params(("arbitrary",)),
    )(proj, proj, proj, *tabs, sinks)


def _attn_bwd(proj, dy, tabs, sinks, comm=None):
    n_tok = proj.shape[0]
    nb = n_tok // ATTN_BLOCK
    group = N_Q_HEADS // N_KV_HEADS
    kvw = N_KV_HEADS * HEAD_DIM

    def body(q_ref, k_ref, v_ref, do_ref, c_ref, sa_ref, sb_ref, cp_ref, sap_ref, sbp_ref, sink_ref,
             dq_ref, dkv_ref, dsink_ref, kprev, vprev, dkc, dvc):
        n = pl.program_id(0)

        @pl.when(n == 0)
        def _():
            for ref in (kprev, vprev, dkc, dvc, dsink_ref):
                ref[...] = jnp.zeros_like(ref)

        prev_tabs = (cp_ref[...], sap_ref[...], sbp_ref[...])

        @pl.when(n < nb)
        def _():
            c, sa, sb = c_ref[...], sa_ref[...], sb_ref[...]
            qr = _bf(_rope(q_ref[...], c, sa, sb))
            kr = _rope(k_ref[...], c, sa, sb)
            vc = v_ref[...]
            kk = jnp.concatenate([kprev[...], kr], axis=0)
            vv = jnp.concatenate([vprev[...], vc], axis=0)
            kprev[...] = kr
            vprev[...] = vc
            mask = _attn_mask(n)
            lane = lax.broadcasted_iota(jnp.int32, (1, LANES), 1)
            lo2 = lax.broadcasted_iota(jnp.int32, (2 * ATTN_BLOCK, LANES), 1) < HEAD_DIM
            dsink = jnp.zeros((1, LANES), F32)
            dq_pairs = []
            dk_full = []
            dv_full = []
            for kh in range(N_KV_HEADS):
                r, which = divmod(kh, 2)
                kb = _both_halves(kk[:, r * LANES:(r + 1) * LANES], which)
                vb = _both_halves(vv[:, r * LANES:(r + 1) * LANES], which)
                qs = _stack_heads(qr, kh)
                dos = _bf(_stack_heads(do_ref, kh))
                probs, p_sink = _attn_probs(qs, kb, mask, _sink_column(sink_ref, kh))
                dp = _mm_nt(dos, vb)
                delta = jnp.sum(probs * dp, axis=-1, keepdims=True)
                ds = _bf(probs * (dp - delta) * (HEAD_DIM ** -0.5))
                sink_terms = p_sink * delta
                for i in range(GROUP):
                    head_sum = jnp.sum(sink_terms[i * ATTN_BLOCK:(i + 1) * ATTN_BLOCK])
                    dsink = dsink + jnp.where(lane == kh * GROUP + i, -head_sum, 0.0)
                dq_pairs += _unstack_pairs(_mm(ds, kb))
                dk_acc = _mm_tn(ds, qs)
                dv_acc = _mm_tn(_bf(probs), dos)
                dk_full.append(dk_acc + pltpu.roll(dk_acc, HEAD_DIM, 1))
                dv_full.append(dv_acc + pltpu.roll(dv_acc, HEAD_DIM, 1))
            dk_pairs = [jnp.where(lo2, dk_full[2 * r], dk_full[2 * r + 1]) for r in range(N_KV_HEADS // 2)]
            dv_pairs = [jnp.where(lo2, dv_full[2 * r], dv_full[2 * r + 1]) for r in range(N_KV_HEADS // 2)]
            dsink_ref[...] += dsink
            dq_ref[...] = _bf(_rope_transposed(jnp.concatenate(dq_pairs, axis=1), c, sa, sb))
            dk_all = jnp.concatenate(dk_pairs, axis=1)
            dv_all = jnp.concatenate(dv_pairs, axis=1)
            dkv_ref[:, :kvw] = _bf(_rope_transposed(dkc[...] + dk_all[:ATTN_BLOCK], *prev_tabs))
            dkv_ref[:, kvw:] = _bf(dvc[...] + dv_all[:ATTN_BLOCK])
            dkc[...] = dk_all[ATTN_BLOCK:]
            dvc[...] = dv_all[ATTN_BLOCK:]

        @pl.when(n == nb)
        def _():
            dkv_ref[:, :kvw] = _bf(_rope_transposed(dkc[...], *prev_tabs))
            dkv_ref[:, kvw:] = _bf(dvc[...])

    cur = lambda n: jnp.minimum(n, nb - 1)
    prev = lambda n: jnp.maximum(n - 1, 0)
    blk = lambda width, col: pl.BlockSpec((ATTN_BLOCK, width), lambda n: (cur(n), col))
    tab = pl.BlockSpec((ATTN_BLOCK, LANES), lambda n: (cur(n), 0))
    tabp = pl.BlockSpec((ATTN_BLOCK, LANES), lambda n: (prev(n), 0))
    return _hosted(
        body, comm, name="attn_bwd", grid=(nb + 1,),
        in_specs=[blk(D_MODEL, 0), blk(kvw, COL_K), blk(kvw, COL_V), blk(D_MODEL, 0), tab, tab, tab, tabp, tabp, tabp,
                  pl.BlockSpec(memory_space=pltpu.SMEM)],
        out_specs=[pl.BlockSpec((ATTN_BLOCK, D_MODEL), lambda n: (cur(n), 0)),
                   pl.BlockSpec((ATTN_BLOCK, 2 * kvw), lambda n: (prev(n), 0)),
                   pl.BlockSpec((1, LANES), lambda n: (0, 0))],
        out_shape=[jax.ShapeDtypeStruct((n_tok, D_MODEL), BF16), jax.ShapeDtypeStruct((n_tok, 2 * kvw), BF16),
                   jax.ShapeDtypeStruct((1, LANES), F32)],
        scratch_shapes=[pltpu.VMEM((ATTN_BLOCK, kvw), F32)] * 4,
        compiler_params=_params(("arbitrary",)), args=[proj, proj, proj, dy, *tabs, *tabs, sinks])


def _bmm(a, b):
    return lax.dot_general(a, b, (((2,), (1,)), ((0,), (0,))), preferred_element_type=F32)


def _bmm_nt(a, b):
    return lax.dot_general(a, b, (((2,), (2,)), ((0,), (0,))), preferred_element_type=F32)


def _bmm_tn(a, b):
    return lax.dot_general(a, b, (((1,), (1,)), ((0,), (0,))), preferred_element_type=F32)


def _tril(cb, upper=False):
    shape = (cb, HGRN_CHUNK, HGRN_CHUNK)
    r, c = lax.broadcasted_iota(jnp.int32, shape, 1), lax.broadcasted_iota(jnp.int32, shape, 2)
    return (r <= c) if upper else (r >= c)


def _tri_matmul(x, upper):
    return lax.dot_general(_tril(x.shape[0], upper).astype(F32), x, (((2,), (1,)), ((0,), (0,))),
                           precision=lax.Precision.HIGHEST, preferred_element_type=F32)


@jax.custom_vjp
def _chunk_cumsum(x):
    return _tri_matmul(x, False)


_chunk_cumsum.defvjp(lambda x: (_tri_matmul(x, False), None), lambda _, g: (_tri_matmul(g, True),))


def _hg_elem(fl, qh, lb):
    f = lb + (1.0 - lb) * _sig(fl)
    k = 1.0 - f
    gc = _chunk_cumsum(jnp.log(f))
    last = lax.broadcasted_iota(jnp.int32, gc.shape, 1) == HGRN_CHUNK - 1
    g_last = jnp.sum(jnp.where(last, gc, 0.0), axis=1, keepdims=True)
    q = qh * _sig(qh)
    return q * jnp.exp(gc), k * jnp.exp(-gc), k * jnp.exp(g_last - gc), jnp.exp(g_last)


def _hg_out(q_dec, k_inv, v, st):
    sc = jnp.where(_tril(q_dec.shape[0]), _bmm_nt(_bf(q_dec), _bf(k_inv)), 0.0)
    return _bmm(_bf(sc), _bf(v)) + _bmm_nt(_bf(q_dec), _bf(st)), sc


def _hg_post(o, og, ng):
    on = o * lax.rsqrt(jnp.mean(o * o, axis=-1, keepdims=True) + RMS_EPS) * ng
    return on * (og * _sig(og))


def _hgrn_specs(n_tok, rev):
    nc = n_tok // HGRN_CHUNK
    cb = min(HGRN_CHUNKS_PER_STEP, nc)
    nt = nc // cb
    rows = cb * HGRN_CHUNK
    tt = (lambda t: nt - 1 - t) if rev else (lambda t: t)
    col = lambda base: pl.BlockSpec((rows, LANES), lambda h, t: (tt(t), base + h))
    head_vec = pl.BlockSpec((1, LANES), lambda h, t: (0, h))
    one_vec = pl.BlockSpec((1, LANES), lambda h, t: (0, 0))
    state = pl.BlockSpec((1, cb, HGRN_DK, HGRN_DK), lambda h, t: (h, tt(t), 0, 0))
    return nc, cb, nt, col, head_vec, one_vec, state


def _hgrn_fwd(proj, lb, ng):
    n_tok = proj.shape[0]
    nc, cb, nt, col, head_vec, one_vec, state = _hgrn_specs(n_tok, False)

    def body(fl_ref, qh_ref, ih_ref, og_ref, lb_ref, ng_ref, y_ref, st_ref, s_acc):
        @pl.when(pl.program_id(1) == 0)
        def _():
            s_acc[...] = jnp.zeros_like(s_acc)

        chunks = lambda ref: ref[...].reshape(cb, HGRN_CHUNK, LANES)
        q_dec, k_inv, k_end, decay = _hg_elem(chunks(fl_ref), chunks(qh_ref), lb_ref[...])
        v = chunks(ih_ref)
        upd = _bmm_tn(_bf(v), _bf(k_end))
        st = s_acc[...]
        for ci in range(cb):
            st_ref[0, ci] = st
            st = st * decay[ci] + upd[ci]
        s_acc[...] = st
        o, _ = _hg_out(q_dec, k_inv, v, st_ref[0])
        y_ref[...] = _bf(_hg_post(o, chunks(og_ref), ng_ref[...]).reshape(cb * HGRN_CHUNK, LANES))

    return pl.pallas_call(
        body, name="hgrn_fwd", grid=(HGRN_HEADS, nt),
        in_specs=[col(COL_F), col(COL_QH), col(COL_IH), col(COL_OG), head_vec, one_vec],
        out_specs=[col(0), state],
        out_shape=[jax.ShapeDtypeStruct((n_tok, D_MODEL), BF16),
                   jax.ShapeDtypeStruct((HGRN_HEADS, nc, HGRN_DK, HGRN_DK), F32)],
        scratch_shapes=[pltpu.VMEM((HGRN_DK, HGRN_DK), F32)],
        compiler_params=_params(("arbitrary", "arbitrary")),
    )(proj, proj, proj, proj, lb, ng)


def _hgrn_bwd(proj, lb, ng, states, dy):
    n_tok = proj.shape[0]
    nc, cb, nt, col, head_vec, one_vec, state = _hgrn_specs(n_tok, True)

    def body(fl_ref, qh_ref, ih_ref, og_ref, lb_ref, ng_ref, st_ref, dy_ref,
             dfl_ref, dqh_ref, dih_ref, dog_ref, dlb_ref, dng_ref, g_acc, g_all):
        h = pl.program_id(0)
        t = pl.program_id(1)

        @pl.when(t == 0)
        def _():
            g_acc[...] = jnp.zeros_like(g_acc)
            dlb_ref[...] = jnp.zeros_like(dlb_ref)

        @pl.when((t == 0) & (h == 0))
        def _():
            dng_ref[...] = jnp.zeros_like(dng_ref)

        chunks = lambda ref: ref[...].reshape(cb, HGRN_CHUNK, LANES)
        flat = lambda val: _bf(val.reshape(cb * HGRN_CHUNK, LANES))
        (q_dec, k_inv, k_end, decay), elem_vjp = jax.vjp(_hg_elem, chunks(fl_ref), chunks(qh_ref), lb_ref[...])
        v = chunks(ih_ref)
        st = st_ref[0]
        o, sc = _hg_out(q_dec, k_inv, v, st)
        _, post_vjp = jax.vjp(_hg_post, o, chunks(og_ref), ng_ref[...])
        do, dog, dng = post_vjp(chunks(dy_ref))
        dob, vb, qb = _bf(do), _bf(v), _bf(q_dec)
        dsc = _bf(jnp.where(_tril(cb), _bmm_nt(dob, vb), 0.0))
        p = _bmm_tn(dob, qb)
        g = g_acc[...]
        for ci in reversed(range(cb)):
            g_all[ci] = g
            g = g * decay[ci] + p[ci]
        g_acc[...] = g
        g = g_all[...]
        gb = _bf(g)
        dq_dec = _bmm(dsc, _bf(k_inv)) + _bmm(dob, _bf(st))
        dk_inv = _bmm_tn(dsc, qb)
        dv = _bmm_tn(_bf(sc), dob) + _bmm_nt(_bf(k_end), gb)
        dk_end = _bmm(vb, gb)
        ddecay = jnp.sum(st * g, axis=1, keepdims=True)
        dfl, dqh, dlb = elem_vjp((dq_dec, dk_inv, dk_end, ddecay))
        dfl_ref[...] = flat(dfl)
        dqh_ref[...] = flat(dqh)
        dih_ref[...] = flat(dv)
        dog_ref[...] = flat(dog)
        dlb_ref[...] += dlb
        dng_ref[...] += dng

    out_col = jax.ShapeDtypeStruct((n_tok, D_MODEL), BF16)
    return pl.pallas_call(
        body, name="hgrn_bwd", grid=(HGRN_HEADS, nt),
        in_specs=[col(COL_F), col(COL_QH), col(COL_IH), col(COL_OG), head_vec, one_vec, state, col(0)],
        out_specs=[col(0), col(0), col(0), col(0), head_vec, one_vec],
        out_shape=[out_col, out_col, out_col, out_col,
                   jax.ShapeDtypeStruct((1, D_MODEL), F32), jax.ShapeDtypeStruct((1, LANES), F32)],
        scratch_shapes=[pltpu.VMEM((HGRN_DK, HGRN_DK), F32), pltpu.VMEM((cb, HGRN_DK, HGRN_DK), F32)],
        compiler_params=_params(("arbitrary", "arbitrary")),
    )(proj, proj, proj, proj, lb, ng, states, dy)


def _lb_fwd(lb_logits):
    def lb_of(l0, l1):
        m = jnp.maximum(l0, l1)
        e0, e1 = jnp.exp(l0 - m), jnp.exp(l1 - m)
        return e0 / (e0 + e1)

    def body(l_ref, o_ref):
        o_ref[...] = lb_of(l_ref[0:1, :], l_ref[1:2, :])

    lb = pl.pallas_call(body, name="lb_fwd", out_shape=jax.ShapeDtypeStruct((1, D_MODEL), F32))(lb_logits)
    return lb, lb_of


def _gate_specs(tm):
    return [pl.BlockSpec((tm, 512), lambda i, c=c: (i, c)) for c in (COL_GA, COL_GA + 1, COL_GH, COL_GH + 1)]


def _mix_fwd(y_attn, y_hgrn, proj, x1, w_pa, w_ph, w_out, g, b):
    n = x1.shape[0]
    tm = min(MIX_TILE, n)

    def body(ya_ref, yh_ref, ga0, ga1, gh0, gh1, x_ref, wpa, wph, wo, g_ref, b_ref, z_ref, o_ref):
        ya = _mm(ya_ref[...], wpa[...])
        yh = _mm(yh_ref[...], wph[...])
        ga = jnp.concatenate([ga0[...], ga1[...]], axis=1)
        gh = jnp.concatenate([gh0[...], gh1[...]], axis=1)
        merged = _sig(ga) * ya + _sig(gh) * yh
        z = DEEPNORM_ALPHA * x_ref[...] + _mm(_bf(merged), wo[...])
        z_ref[...] = z
        o_ref[...] = _ln(z, g_ref[...], b_ref[...])

    row = pl.BlockSpec((tm, D_MODEL), lambda i: (i, 0))
    sq = _full((D_MODEL, D_MODEL))
    vec = _full((1, D_MODEL))
    return pl.pallas_call(
        body, name="mix_fwd", grid=(n // tm,),
        in_specs=[row, row, *_gate_specs(tm), row, sq, sq, sq, vec, vec],
        out_specs=[row, row], out_shape=[jax.ShapeDtypeStruct((n, D_MODEL), F32)] * 2,
        compiler_params=_params(("arbitrary",)),
    )(y_attn, y_hgrn, proj, proj, proj, proj, x1, w_pa, w_ph, w_out, g, b)


def _mix_bwd(dx2, z2, y_attn, y_hgrn, proj, w_pa, w_ph, w_out, g, b):
    n = z2.shape[0]
    tm = min(MIX_TILE, n)

    def body(do_ref, z_ref, ya_ref, yh_ref, ga0, ga1, gh0, gh1, wpa, wph, wo, g_ref, b_ref,
             dz_ref, dzb_ref, mg_ref, dya_ref, dyh_ref, dyat_ref, dyhg_ref, dgt_ref, dg_ref, db_ref):
        _, vjp = jax.vjp(_ln, z_ref[...], g_ref[...], b_ref[...])
        dz, dg, db = vjp(do_ref[...])

        @pl.when(pl.program_id(0) == 0)
        def _():
            dg_ref[...] = jnp.zeros_like(dg_ref)
            db_ref[...] = jnp.zeros_like(db_ref)

        dg_ref[...] += dg
        db_ref[...] += db
        dz_ref[...] = dz
        ya = _mm(ya_ref[...], wpa[...])
        yh = _mm(yh_ref[...], wph[...])
        ga = jnp.concatenate([ga0[...], ga1[...]], axis=1)
        gh = jnp.concatenate([gh0[...], gh1[...]], axis=1)

        def merge(ga, gh, ya, yh):
            return _sig(ga) * ya + _sig(gh) * yh

        merged, merge_vjp = jax.vjp(merge, ga, gh, ya, yh)
        mg_ref[...] = _bf(merged)
        dzb = _bf(dz)
        dzb_ref[...] = dzb
        dmerged = _mm_nt(dzb, wo[...])
        dga, dgh, dya, dyh = merge_vjp(dmerged)
        dya = _bf(dya)
        dyh = _bf(dyh)
        dya_ref[...] = dya
        dyh_ref[...] = dyh
        dgt_ref[:, :D_MODEL] = _bf(dga)
        dgt_ref[:, D_MODEL:] = _bf(dgh)
        dyat_ref[...] = _mm_nt(dya, wpa[...])
        dyhg_ref[...] = _mm_nt(dyh, wph[...])

    row = pl.BlockSpec((tm, D_MODEL), lambda i: (i, 0))
    row2 = pl.BlockSpec((tm, 2 * D_MODEL), lambda i: (i, 0))
    sq = _full((D_MODEL, D_MODEL))
    vec = _full((1, D_MODEL))
    f32_row = jax.ShapeDtypeStruct((n, D_MODEL), F32)
    bf_row = jax.ShapeDtypeStruct((n, D_MODEL), BF16)
    vec_shape = jax.ShapeDtypeStruct((1, D_MODEL), F32)
    return pl.pallas_call(
        body, name="mix_bwd", grid=(n // tm,),
        in_specs=[row, row, row, row, *_gate_specs(tm), sq, sq, sq, vec, vec],
        out_specs=[row, row, row, row, row, row, row, row2, vec, vec],
        out_shape=[f32_row, bf_row, bf_row, bf_row, bf_row, f32_row, f32_row,
                   jax.ShapeDtypeStruct((n, 2 * D_MODEL), BF16), vec_shape, vec_shape],
        compiler_params=_params(("arbitrary",)),
    )(dx2, z2, y_attn, y_hgrn, proj, proj, proj, proj, w_pa, w_ph, w_out, g, b)


def _position():
    x, y, c = lax.axis_index("x"), lax.axis_index("y"), lax.axis_index("c")
    chips = [(1 - x, y), (x, 1 - y), (1 - x, 1 - y)]
    return x, y, c, chips


def _any_specs(k):
    return [pl.BlockSpec(memory_space=pl.ANY)] * k


class _GatherWeights:
    def __init__(self, shards):
        nw = len(shards)
        self.inputs = list(shards)
        self.out_shape = [jax.ShapeDtypeStruct((N_CHIPS, *s.shape), s.dtype) for s in shards]
        self.scratch = [pltpu.SemaphoreType.DMA((nw,)), pltpu.SemaphoreType.DMA((nw * 6,)),
                        pltpu.SemaphoreType.DMA((nw * 6,))]

    def _copies(self, ins, outs, sems):
        nw = len(ins)
        local_sem, send_sem, recv_sem = sems
        x, y, c, chips = _position()
        me = 2 * x + y
        sibling = (x, y, 1 - c)
        half_rows = [s.shape[0] // 2 for s in self.inputs]

        def half(w, chip_idx, which):
            return outs[w].at[chip_idx, pl.ds(which * half_rows[w], half_rows[w])]

        def remote(w, k, src, dst, to):
            return pltpu.make_async_remote_copy(src_ref=src, dst_ref=dst, send_sem=send_sem.at[w * 6 + k],
                                                recv_sem=recv_sem.at[w * 6 + k], device_id=to, device_id_type=MESH)

        local = [pltpu.make_async_copy(ins[w], outs[w].at[me], local_sem.at[w]) for w in range(nw)]
        first = [remote(w, j, ins[w].at[pl.ds(c * half_rows[w], half_rows[w])], half(w, me, c), (px, py, c))
                 for w in range(nw) for j, (px, py) in enumerate(chips)]
        landed = [half(w, 2 * px + py, c) for w in range(nw) for (px, py) in chips]
        arrive = [remote(w, j, landed[w * 3 + j], landed[w * 3 + j], (px, py, c))
                  for w in range(nw) for j, (px, py) in enumerate(chips)]
        passed = [remote(w, 3 + j, landed[w * 3 + j], landed[w * 3 + j], sibling) for w in range(nw) for j in range(3)]
        from_sibling = [remote(w, 3 + j, half(w, 2 * px + py, 1 - c), half(w, 2 * px + py, 1 - c), sibling)
                        for w in range(nw) for j, (px, py) in enumerate(chips)]
        return local, first, arrive, passed, from_sibling

    def start(self, ins, outs, sems):
        local, first, _, _, _ = self._copies(ins, outs, sems)
        for cp in local + first:
            cp.start()

    def finish(self, ins, outs, sems):
        local, first, arrive, passed, from_sibling = self._copies(ins, outs, sems)
        for cp_in, cp_on in zip(arrive, passed):
            cp_in.wait_recv()
            cp_on.start()
        for cp in from_sibling:
            cp.wait_recv()
        for cp in first + passed:
            cp.wait_send()
        for cp in local:
            cp.wait()


class _ExchangeGrads:
    def __init__(self, grads):
        nw = len(grads)
        self.inputs = list(grads)
        self.out_shape = [jax.ShapeDtypeStruct(g.shape, g.dtype) for g in grads]
        self.scratch = [pltpu.SemaphoreType.DMA((nw,)), pltpu.SemaphoreType.DMA((nw * 3,)),
                        pltpu.SemaphoreType.DMA((nw * 3,))]

    def _copies(self, ins, outs, sems):
        nw = len(ins)
        local_sem, send_sem, recv_sem = sems
        x, y, c, chips = _position()
        me = 2 * x + y

        def remote(w, j, src, dst, chip):
            return pltpu.make_async_remote_copy(src_ref=src, dst_ref=dst, send_sem=send_sem.at[w * 3 + j],
                                                recv_sem=recv_sem.at[w * 3 + j], device_id=(*chip, c),
                                                device_id_type=MESH)

        local = [pltpu.make_async_copy(ins[w].at[me], outs[w].at[me], local_sem.at[w]) for w in range(nw)]
        sends = [remote(w, j, ins[w].at[2 * px + py], outs[w].at[me], (px, py))
                 for w in range(nw) for j, (px, py) in enumerate(chips)]
        arrive = [remote(w, j, outs[w].at[2 * px + py], outs[w].at[2 * px + py], (px, py))
                  for w in range(nw) for j, (px, py) in enumerate(chips)]
        return local, sends, arrive

    def start(self, ins, outs, sems):
        local, sends, _ = self._copies(ins, outs, sems)
        for cp in local + sends:
            cp.start()

    def finish(self, ins, outs, sems):
        local, sends, arrive = self._copies(ins, outs, sems)
        for cp in arrive:
            cp.wait_recv()
        for cp in sends:
            cp.wait_send()
        for cp in local:
            cp.wait()


def _hosted(body, comm, *, name, grid, in_specs, out_specs, out_shape, scratch_shapes, compiler_params, args):
    if comm is None:
        res = pl.pallas_call(body, name=name, grid=grid, in_specs=in_specs, out_specs=out_specs, out_shape=out_shape,
                             scratch_shapes=scratch_shapes, compiler_params=compiler_params)(*args)
        return list(res), []
    n_in, n_out, n_scr = len(in_specs), len(out_specs), len(scratch_shapes)
    c_in, c_out = len(comm.inputs), len(comm.out_shape)

    def hosted_body(*refs):
        refs = list(refs)
        cut = lambda k: (refs[:k], refs[k:])
        main_in, refs = cut(n_in)
        comm_in, refs = cut(c_in)
        main_out, refs = cut(n_out)
        comm_out, refs = cut(c_out)
        main_scr, comm_scr = cut(n_scr)
        ids = [pl.program_id(a) for a in range(len(grid))]
        first = functools.reduce(jnp.logical_and, [i == 0 for i in ids])
        last = functools.reduce(jnp.logical_and, [i == g - 1 for i, g in zip(ids, grid)])

        @pl.when(first)
        def _():
            comm.start(comm_in, comm_out, comm_scr)

        body(*main_in, *main_out, *main_scr)

        @pl.when(last)
        def _():
            comm.finish(comm_in, comm_out, comm_scr)

    res = pl.pallas_call(
        hosted_body, name=name, grid=grid, in_specs=[*in_specs, *_any_specs(c_in)],
        out_specs=[*out_specs, *_any_specs(c_out)], out_shape=[*out_shape, *comm.out_shape],
        scratch_shapes=[*scratch_shapes, *comm.scratch], compiler_params=compiler_params,
    )(*args, *comm.inputs)
    return list(res[:n_out]), list(res[n_out:])


def _run_comm(name, comm):
    def body(*refs):
        refs = list(refs)
        c_in, c_out = len(comm.inputs), len(comm.out_shape)
        ins, outs, sems = refs[:c_in], refs[c_in:c_in + c_out], refs[c_in + c_out:]
        comm.start(ins, outs, sems)
        comm.finish(ins, outs, sems)

    return list(pl.pallas_call(
        body, name=name, in_specs=_any_specs(len(comm.inputs)), out_specs=_any_specs(len(comm.out_shape)),
        out_shape=comm.out_shape, scratch_shapes=comm.scratch,
    )(*comm.inputs))


def _sum_slots(name, slots):
    _, rows, cols = slots.shape
    tr = _update_rows(rows)

    def body(s_ref, o_ref):
        acc = s_ref[0].astype(F32)
        for i in range(1, N_CHIPS):
            acc = acc + s_ref[i].astype(F32)
        o_ref[...] = acc

    return pl.pallas_call(
        body, name=name, grid=(rows // tr,),
        in_specs=[pl.BlockSpec((N_CHIPS, tr, cols), lambda i: (0, i, 0))],
        out_specs=pl.BlockSpec((tr, cols), lambda i: (i, 0)),
        out_shape=jax.ShapeDtypeStruct((rows, cols), F32),
        compiler_params=_params(("arbitrary",)),
    )(slots)


def _swap_with_sibling(parts):
    nw = len(parts)

    def body(*refs):
        ins, outs = refs[:nw], refs[nw:2 * nw]
        send_sem, recv_sem = refs[2 * nw:]
        x, y, c, _ = _position()
        copies = [pltpu.make_async_remote_copy(src_ref=ins[w], dst_ref=outs[w], send_sem=send_sem.at[w],
                                               recv_sem=recv_sem.at[w], device_id=(x, y, 1 - c), device_id_type=MESH)
                  for w in range(nw)]
        for cp in copies:
            cp.start()
        for cp in copies:
            cp.wait()

    return pl.pallas_call(
        body, name="swap_with_sibling",
        in_specs=_any_specs(nw), out_specs=_any_specs(nw),
        out_shape=[jax.ShapeDtypeStruct(p.shape, p.dtype) for p in parts],
        scratch_shapes=[pltpu.SemaphoreType.DMA((nw,)), pltpu.SemaphoreType.DMA((nw,))],
    )(*parts)


def _sum_small(part):
    def body(p_ref, o_ref, buf, send_sem, recv_sem):
        x, y, c, _ = _position()
        me = 4 * x + 2 * y + c
        buf[me] = p_ref[...]
        copies = []
        for k in range(1, N_DEV):
            peer = tuple(1 - v if (k >> s) & 1 else v for v, s in ((x, 2), (y, 1), (c, 0)))
            copies.append(pltpu.make_async_remote_copy(src_ref=p_ref, dst_ref=buf.at[me], send_sem=send_sem.at[k - 1],
                                                       recv_sem=recv_sem.at[k - 1], device_id=peer, device_id_type=MESH))
        for cp in copies:
            cp.start()
        for cp in copies:
            cp.wait()
        acc = buf[0]
        for d in range(1, N_DEV):
            acc = acc + buf[d]
        o_ref[...] = acc

    vm = pl.BlockSpec(memory_space=pltpu.VMEM)
    return pl.pallas_call(
        body, name="sum_small", in_specs=[vm], out_specs=vm,
        out_shape=jax.ShapeDtypeStruct((1, SM_LEN), F32),
        scratch_shapes=[pltpu.VMEM((N_DEV, 1, SM_LEN), F32), pltpu.SemaphoreType.DMA((N_DEV - 1,)),
                        pltpu.SemaphoreType.DMA((N_DEV - 1,))],
    )(part)


def _adamw(w, g, m, v):
    m = ADAM_B1 * m + (1.0 - ADAM_B1) * g
    v = ADAM_B2 * v + (1.0 - ADAM_B2) * (g * g)
    m_hat = m / (1.0 - ADAM_B1 ** ADAM_STEP)
    v_hat = v / (1.0 - ADAM_B2 ** ADAM_STEP)
    delta = -ADAM_LR * (m_hat / (jnp.sqrt(v_hat) + ADAM_EPS) + ADAM_WD * w)
    return delta, m, v


def _adam_big(name, p_own, p_sibling, w, m, v):
    rows, cols = w.shape
    tr = _update_rows(rows)

    def body(p_ref, q_ref, w_ref, m_ref, v_ref, g_ref, d_ref, nm_ref, nv_ref):
        g = p_ref[...] + q_ref[...]
        g_ref[...] = g
        d_ref[...], nm_ref[...], nv_ref[...] = _adamw(w_ref[...], g, m_ref[...], v_ref[...])

    spec = pl.BlockSpec((tr, cols), lambda i: (i, 0))
    return pl.pallas_call(
        body, name=name, grid=(rows // tr,), in_specs=[spec] * 5, out_specs=[spec] * 4,
        out_shape=[jax.ShapeDtypeStruct((rows, cols), F32)] * 4,
        compiler_params=_params(("arbitrary",)),
    )(p_own, p_sibling, w, m, v)


def _adam_small(total, wvec, mvec, vvec, lb_of):
    def body(t_ref, w_ref, m_ref, v_ref, g_ref, d_ref, nm_ref, nv_ref):
        g_ref[:, :SM_LB] = t_ref[:, :SM_LB]
        _, vjp = jax.vjp(lb_of, w_ref[:, PK_LB0:PK_LB1], w_ref[:, PK_LB1:PK_LEN])
        dl0, dl1 = vjp(t_ref[:, SM_LB:SM_LOSS])
        g_ref[:, PK_LB0:PK_LB1] = dl0
        g_ref[:, PK_LB1:PK_LEN] = dl1
        d_ref[...], nm_ref[...], nv_ref[...] = _adamw(w_ref[...], g_ref[...], m_ref[...], v_ref[...])

    return pl.pallas_call(body, name="adam_small", out_shape=[jax.ShapeDtypeStruct((1, PK_LEN), F32)] * 4)(
        total, wvec, mvec, vvec)


_BIG = ("ffn1_w1", "ffn1_w3", "ffn1_w2", "w_in", "w_proj_attn", "w_proj_hgrn", "w_out", "ffn2_w1", "ffn2_w3", "ffn2_w2")
_SMALL = ("ln1_g", "ln1_b", "ln2_g", "ln2_b", "ln3_g", "ln3_b", "b_in", "attn_sinks", "hgrn_norm_g", "hgrn_lb_logits")
_ORDER = ("ln1_g", "ln1_b", "ffn1_w1", "ffn1_w3", "ffn1_w2", "ln2_g", "ln2_b", "w_in", "b_in", "attn_sinks",
          "hgrn_lb_logits", "hgrn_norm_g", "w_proj_attn", "w_proj_hgrn", "w_out", "ln3_g", "ln3_b",
          "ffn2_w1", "ffn2_w3", "ffn2_w2")


def _pad_to(a, rows, cols):
    return jnp.pad(a, ((0, rows - a.shape[0]), (0, cols - a.shape[1])))


_TRANSPOSED = ("ffn1_w1", "ffn1_w3", "ffn2_w1", "ffn2_w3")


def _local_view(name, arr):
    return arr[0].T if name in _TRANSPOSED else arr[0]


def _send_form(name, w):
    wb = _bf(w)
    return _pad_to(wb, FF_PAD, D_MODEL) if name.startswith("ffn") else wb


def _pack_small(p):
    pieces = [p[k].reshape(1, -1) for k in ("ln1_g", "ln1_b", "ln2_g", "ln2_b", "ln3_g", "ln3_b", "b_in")]
    pieces.append(_pad_to(p["attn_sinks"].reshape(1, -1), 1, LANES))
    pieces.append(p["hgrn_norm_g"].reshape(1, -1))
    pieces.append(p["hgrn_lb_logits"].reshape(1, -1))
    return jnp.concatenate(pieces, axis=1)


def _unpack_small(vec, like):
    out = {}
    for i, k in enumerate(("ln1_g", "ln1_b", "ln2_g", "ln2_b", "ln3_g", "ln3_b")):
        out[k] = vec[:, i * D_MODEL:(i + 1) * D_MODEL]
    out["b_in"] = vec[:, SM_BIN:SM_SINK]
    out["attn_sinks"] = vec[:, SM_SINK:SM_SINK + N_Q_HEADS]
    out["hgrn_norm_g"] = vec[:, SM_NG:SM_LB]
    out["hgrn_lb_logits"] = vec[:, PK_LB0:PK_LEN].reshape(2, D_MODEL)
    return {k: v.reshape(like[k].shape) for k, v in out.items()}


def kernel(x, ln1_g, ln1_b, ffn1_w1, ffn1_w3, ffn1_w2, ln2_g, ln2_b, w_in, b_in, attn_sinks, hgrn_lb_logits, hgrn_norm_g, w_proj_attn, w_proj_hgrn, w_out, ln3_g, ln3_b, ffn2_w1, ffn2_w3, ffn2_w2, loss_target, m_ln1_g, m_ln1_b, m_ffn1_w1, m_ffn1_w3, m_ffn1_w2, m_ln2_g, m_ln2_b, m_w_in, m_b_in, m_attn_sinks, m_hgrn_lb_logits, m_hgrn_norm_g, m_w_proj_attn, m_w_proj_hgrn, m_w_out, m_ln3_g, m_ln3_b, m_ffn2_w1, m_ffn2_w3, m_ffn2_w2, v_ln1_g, v_ln1_b, v_ffn1_w1, v_ffn1_w3, v_ffn1_w2, v_ln2_g, v_ln2_b, v_w_in, v_b_in, v_attn_sinks, v_hgrn_lb_logits, v_hgrn_norm_g, v_w_proj_attn, v_w_proj_hgrn, v_w_out, v_ln3_g, v_ln3_b, v_ffn2_w1, v_ffn2_w3, v_ffn2_w2):
    w = dict(ln1_g=ln1_g, ln1_b=ln1_b, ffn1_w1=ffn1_w1, ffn1_w3=ffn1_w3, ffn1_w2=ffn1_w2, ln2_g=ln2_g, ln2_b=ln2_b,
             w_in=w_in, b_in=b_in, attn_sinks=attn_sinks, hgrn_lb_logits=hgrn_lb_logits, hgrn_norm_g=hgrn_norm_g,
             w_proj_attn=w_proj_attn, w_proj_hgrn=w_proj_hgrn, w_out=w_out, ln3_g=ln3_g, ln3_b=ln3_b,
             ffn2_w1=ffn2_w1, ffn2_w3=ffn2_w3, ffn2_w2=ffn2_w2)
    mom = dict(ln1_g=m_ln1_g, ln1_b=m_ln1_b, ffn1_w1=m_ffn1_w1, ffn1_w3=m_ffn1_w3, ffn1_w2=m_ffn1_w2, ln2_g=m_ln2_g,
               ln2_b=m_ln2_b, w_in=m_w_in, b_in=m_b_in, attn_sinks=m_attn_sinks, hgrn_lb_logits=m_hgrn_lb_logits,
               hgrn_norm_g=m_hgrn_norm_g, w_proj_attn=m_w_proj_attn, w_proj_hgrn=m_w_proj_hgrn, w_out=m_w_out,
               ln3_g=m_ln3_g, ln3_b=m_ln3_b, ffn2_w1=m_ffn2_w1, ffn2_w3=m_ffn2_w3, ffn2_w2=m_ffn2_w2)
    var = dict(ln1_g=v_ln1_g, ln1_b=v_ln1_b, ffn1_w1=v_ffn1_w1, ffn1_w3=v_ffn1_w3, ffn1_w2=v_ffn1_w2, ln2_g=v_ln2_g,
               ln2_b=v_ln2_b, w_in=v_w_in, b_in=v_b_in, attn_sinks=v_attn_sinks, hgrn_lb_logits=v_hgrn_lb_logits,
               hgrn_norm_g=v_hgrn_norm_g, w_proj_attn=v_w_proj_attn, w_proj_hgrn=v_w_proj_hgrn, w_out=v_w_out,
               ln3_g=v_ln3_g, ln3_b=v_ln3_b, ffn2_w1=v_ffn2_w1, ffn2_w3=v_ffn2_w3, ffn2_w2=v_ffn2_w2)

    n_tok = x.shape[1]
    x0 = x.reshape(n_tok, D_MODEL)
    target = loss_target.reshape(n_tok, D_MODEL)

    shard = {k: _send_form(k, _local_view(k, w[k])) for k in _BIG}
    gather = lambda keys: _GatherWeights([shard[k] for k in keys])
    slots = {}
    exchange = lambda keys: _ExchangeGrads([big[k] for k in keys])
    ffn1_keys = ("ffn1_w1", "ffn1_w3", "ffn1_w2")
    mixer_keys = ("w_in", "w_proj_attn", "w_proj_hgrn", "w_out")
    ffn2_keys = ("ffn2_w1", "ffn2_w3", "ffn2_w2")
    f1 = _run_comm("gather_ffn1", gather(ffn1_keys))

    tabs = _rope_tables(n_tok)
    lb, lb_of = _lb_fwd(hgrn_lb_logits)
    (z1, x1, x0b, h1_1, h3_1, x1b), (w_in_g, w_pa, w_ph, w_o) = _ffn_fwd(
        "ffn1_fwd", x0, *f1, ln1_g, ln1_b, comm=gather(mixer_keys))
    w_pa, w_ph, w_o = (t.reshape(D_MODEL, D_MODEL) for t in (w_pa, w_ph, w_o))
    (proj,), f2 = _in_proj(x1b, w_in_g, b_in, comm=gather(ffn2_keys))
    y_attn = _attn_fwd(proj, tabs, attn_sinks)
    y_hgrn, states = _hgrn_fwd(proj, lb, hgrn_norm_g)
    z2, x2 = _mix_fwd(y_attn, y_hgrn, proj, x1, w_pa, w_ph, w_o, ln2_g, ln2_b)
    (z3, dy, x2b, h1_2, h3_2, loss_part), _ = _ffn_fwd("ffn2_fwd", x2, *f2, ln3_g, ln3_b, target=target)

    big = {}
    small = {}
    (dx2, a2, dh1_2, dh3_2, df2, small["ln3_g"], small["ln3_b"]), _ = _ffn_bwd(
        "ffn2_bwd", h1_2, h3_2, z3, dy, *f2, ln3_g, ln3_b)
    (big["ffn2_w1"],), _ = _grad_matmul("ffn2_dw1", dh1_2, "batch", FF_PAD, x2b, "shared", D_MODEL)
    (big["ffn2_w3"],), _ = _grad_matmul("ffn2_dw3", dh3_2, "batch", FF_PAD, x2b, "shared", D_MODEL)
    (big["ffn2_w2"],), _ = _grad_matmul("ffn2_dw2", a2, "batch", FF_PAD, df2, "shared", D_MODEL)
    (dz2, dz2b, merged, dya, dyh, dy_attn, dy_hgrn, dgates, small["ln2_g"], small["ln2_b"]) = _mix_bwd(
        dx2, z2, y_attn, y_hgrn, proj, w_pa, w_ph, w_o, ln2_g, ln2_b)
    (big["w_out"],), _ = _grad_matmul("dw_out", merged, "cols", PROJ_SHARD, dz2b, "shared", D_MODEL)
    (big["w_proj_attn"],), _ = _grad_matmul("dw_proj_attn", y_attn, "cols", PROJ_SHARD, dya, "shared", D_MODEL)
    (big["w_proj_hgrn"],), _ = _grad_matmul("dw_proj_hgrn", y_hgrn, "cols", PROJ_SHARD, dyh, "shared", D_MODEL)
    dfl, dqh, dih, dog, dlb, small["hgrn_norm_g"] = _hgrn_bwd(proj, lb, hgrn_norm_g, states, dy_hgrn)
    early_keys = ffn2_keys + mixer_keys[1:]
    (dq, dkv, dsinks), early = _attn_bwd(proj, dy_attn, tabs, attn_sinks, comm=exchange(early_keys))
    slots.update(zip(early_keys, early))
    dproj = jnp.concatenate([dq, dkv, dfl, dqh, dih, dog, dgates], axis=1)
    (big["w_in"], small["b_in"]), _ = _grad_matmul(
        "dw_in", x1b, "shared", D_MODEL, dproj, "cols", IN_SHARD, colsum=True)
    (dx1,), (slots["w_in"],) = _in_proj_dx(dproj, w_in_g, dz2, comm=exchange(("w_in",)))
    (grad_x, a1, dh1_1, dh3_1, df1, small["ln1_g"], small["ln1_b"]), _ = _ffn_bwd(
        "ffn1_bwd", h1_1, h3_1, z1, dx1, *f1, ln1_g, ln1_b)
    (big["ffn1_w1"],), _ = _grad_matmul("ffn1_dw1", dh1_1, "batch", FF_PAD, x0b, "shared", D_MODEL)
    (big["ffn1_w3"],), (slots["ffn1_w1"],) = _grad_matmul(
        "ffn1_dw3", dh3_1, "batch", FF_PAD, x0b, "shared", D_MODEL, comm=exchange(("ffn1_w1",)))
    (big["ffn1_w2"],), (slots["ffn1_w3"],) = _grad_matmul(
        "ffn1_dw2", a1, "batch", FF_PAD, df1, "shared", D_MODEL, comm=exchange(("ffn1_w3",)))
    (slots["ffn1_w2"],) = _run_comm("exchange_last", exchange(("ffn1_w2",)))

    local = {k: _local_view(k, w[k]) for k in _BIG}
    partial = []
    for k in _BIG:
        p = _sum_slots("sum_" + k, slots[k])
        partial.append(p[:local[k].shape[0]])
    from_sibling = _swap_with_sibling(partial)

    outs = {"grad": {}, "delta": {}, "m": {}, "v": {}}
    for k, p, q in zip(_BIG, partial, from_sibling):
        res = _adam_big("adam_" + k, p, q, local[k], _local_view(k, mom[k]), _local_view(k, var[k]))
        for kind, r in zip(("grad", "delta", "m", "v"), res):
            outs[kind][k] = (r.T if k in _TRANSPOSED else r).reshape(w[k].shape)

    part = jnp.concatenate(
        [small[k] for k in ("ln1_g", "ln1_b", "ln2_g", "ln2_b", "ln3_g", "ln3_b", "b_in")]
        + [dsinks, small["hgrn_norm_g"], dlb, loss_part], axis=1)
    total = _sum_small(part)
    res = _adam_small(total, _pack_small(w), _pack_small(mom), _pack_small(var), lb_of)
    for kind, r in zip(("grad", "delta", "m", "v"), res):
        outs[kind].update(_unpack_small(r, w))
    loss = total[0, SM_LOSS]

    return (loss, grad_x.reshape(x.shape), *[outs["grad"][k] for k in _ORDER], *[outs["delta"][k] for k in _ORDER],
            *[outs["m"][k] for k in _ORDER], *[outs["v"][k] for k in _ORDER])
```

```python
import functools

import jax
import jax.numpy as jnp
from jax import lax
from jax.experimental import pallas as pl
from jax.experimental.pallas import tpu as pltpu

F32 = jnp.float32
BF16 = jnp.bfloat16

D_MODEL = 1024
N_Q_HEADS = 16
N_KV_HEADS = 4
HEAD_DIM = 64
ATTN_BLOCK = 128
ROPE_THETA = 500000.0
ROPE_DIM = HEAD_DIM // 4
HGRN_HEADS = 8
HGRN_DK = 128
HGRN_CHUNK = 64
D_FF = 2816
D_IN = 7680
DEEPNORM_ALPHA = 2 ** 0.25
LN_EPS = 1e-5
RMS_EPS = 1e-6
NEG_INF = -1e30

ADAM_LR = 0.001
ADAM_B1 = 0.9
ADAM_B2 = 0.999
ADAM_EPS = 1e-08
ADAM_WD = 0.01
ADAM_STEP = 10

N_CHIPS = 4
N_DEV = 8
LANES = 128
FF_GRAD_PARTS = 2
FFN_TILE = 256
IN_SHARD = D_IN // N_CHIPS
PROJ_SHARD = D_MODEL // N_CHIPS
ROW_TILE = 512
GRAD_ROWS = 2048
MIX_TILE = 256
UPDATE_ROWS = 128
HGRN_CHUNKS_PER_STEP = 8
VMEM_LIMIT = 56 * 1024 * 1024

COL_K = 1024 // 256
COL_V = 1280 // 256
COL_F = 1536 // LANES
COL_QH = 2560 // LANES
COL_IH = 3584 // LANES
COL_OG = 4608 // LANES
COL_GA = 5632 // 512
COL_GH = 6656 // 512

SM_LN = 0
SM_BIN = 6 * D_MODEL
SM_SINK = SM_BIN + D_IN
SM_NG = SM_SINK + LANES
SM_LB = SM_NG + LANES
SM_LOSS = SM_LB + D_MODEL
SM_LEN = SM_LOSS + LANES
PK_LB0 = SM_LB
PK_LB1 = SM_LB + D_MODEL
PK_LEN = PK_LB1 + D_MODEL

MESH = pl.DeviceIdType.MESH


def _mm(a, b):
    return lax.dot_general(a, b, (((1,), (0,)), ((), ())), preferred_element_type=F32)


def _mm_nt(a, b):
    return lax.dot_general(a, b, (((1,), (1,)), ((), ())), preferred_element_type=F32)


def _mm_tn(a, b):
    return lax.dot_general(a, b, (((0,), (0,)), ((), ())), preferred_element_type=F32)


def _bf(v):
    return v.astype(BF16)


def _sig(v):
    return jax.nn.sigmoid(v)


def _ln(z, g, b):
    mu = jnp.mean(z, axis=-1, keepdims=True)
    zc = z - mu
    var = jnp.mean(zc * zc, axis=-1, keepdims=True)
    return zc * lax.rsqrt(var + LN_EPS) * g + b


def _swiglu_act(h1, h3):
    return (h1 * _sig(h1)) * h3


def _params(sem=None):
    return pltpu.CompilerParams(dimension_semantics=sem, vmem_limit_bytes=VMEM_LIMIT)


def _full(shape):
    nd = len(shape)
    return pl.BlockSpec(shape, lambda *_: (0,) * nd)


def _update_rows(rows):
    return max(t for t in range(8, UPDATE_ROWS + 1, 8) if rows % t == 0)


def _resident(shape):
    nd = len(shape)
    return pl.BlockSpec(shape, lambda *_: (0,) * nd, pipeline_mode=pl.Buffered(1))


def _ffn_fwd(name, x, w1t, w3t, w2, g, b, target=None, comm=None):
    n = x.shape[0]
    tm = min(FFN_TILE, n)
    final = target is not None

    def body(*refs):
        if final:
            x_ref, w1_ref, w3_ref, w2_ref, g_ref, b_ref, t_ref, z_ref, o_ref, xb_ref, h1_ref, h3_ref, loss_ref = refs
        else:
            x_ref, w1_ref, w3_ref, w2_ref, g_ref, b_ref, z_ref, o_ref, xb_ref, h1_ref, h3_ref, ob_ref = refs
        xb = _bf(x_ref[...])
        xb_ref[...] = xb
        h1 = _mm_nt(xb, w1_ref[...])
        h3 = _mm_nt(xb, w3_ref[...])
        h1_ref[...] = _bf(h1)
        h3_ref[...] = _bf(h3)
        z = DEEPNORM_ALPHA * x_ref[...] + 0.5 * _mm(_bf(_swiglu_act(h1, h3)), w2_ref[...])
        z_ref[...] = z
        y = _ln(z, g_ref[...], b_ref[...])
        if final:
            e = y - t_ref[...]

            @pl.when(pl.program_id(0) == 0)
            def _():
                loss_ref[...] = jnp.zeros_like(loss_ref)

            loss_ref[...] += jnp.sum(e * e) * (0.5 / D_MODEL)
            o_ref[...] = e * (1.0 / D_MODEL)
        else:
            o_ref[...] = y
            ob_ref[...] = _bf(y)

    row = pl.BlockSpec((tm, D_MODEL), lambda i: (i, 0))
    hid = pl.BlockSpec((tm, D_FF), lambda i: (i, 0))
    wres = _resident((D_FF, D_MODEL))
    vec = _full((1, D_MODEL))
    in_specs = [row, wres, wres, wres, vec, vec]
    args = [x, w1t, w3t, w2, g, b]
    hid_shape = jax.ShapeDtypeStruct((n, D_FF), BF16)
    out_specs = [row, row, row, hid, hid]
    out_shape = ([jax.ShapeDtypeStruct((n, D_MODEL), F32)] * 2 + [jax.ShapeDtypeStruct((n, D_MODEL), BF16)]
                 + [hid_shape] * 2)
    if final:
        in_specs.append(row)
        args.append(target)
        out_specs.append(_full((1, LANES)))
        out_shape.append(jax.ShapeDtypeStruct((1, LANES), F32))
    else:
        out_specs.append(row)
        out_shape.append(jax.ShapeDtypeStruct((n, D_MODEL), BF16))
    return _hosted(
        body, comm, name=name, grid=(n // tm,), in_specs=in_specs, out_specs=out_specs, out_shape=out_shape,
        scratch_shapes=[], compiler_params=_params(("arbitrary",)), args=args)


def _ffn_bwd(name, h1s, h3s, z, dout, w1t, w3t, w2, g, b, comm=None):
    n = z.shape[0]
    tm = min(FFN_TILE, n)

    def body(h1_ref, h3_ref, z_ref, do_ref, w1_ref, w3_ref, w2_ref, g_ref, b_ref,
             dx_ref, a_ref, dh1_ref, dh3_ref, df_ref, dg_ref, db_ref):
        _, vjp = jax.vjp(_ln, z_ref[...], g_ref[...], b_ref[...])
        dz, dg, db = vjp(do_ref[...])

        @pl.when(pl.program_id(0) == 0)
        def _():
            dg_ref[...] = jnp.zeros_like(dg_ref)
            db_ref[...] = jnp.zeros_like(db_ref)

        dg_ref[...] += dg
        db_ref[...] += db
        df = _bf(0.5 * dz)
        df_ref[...] = df
        a, act_vjp = jax.vjp(_swiglu_act, h1_ref[...].astype(F32), h3_ref[...].astype(F32))
        dh1, dh3 = act_vjp(_mm_nt(df, w2_ref[...]))
        dh1 = _bf(dh1)
        dh3 = _bf(dh3)
        a_ref[...] = _bf(a)
        dh1_ref[...] = dh1
        dh3_ref[...] = dh3
        dx_ref[...] = DEEPNORM_ALPHA * dz + _mm(dh1, w1_ref[...]) + _mm(dh3, w3_ref[...])

    row = pl.BlockSpec((tm, D_MODEL), lambda i: (i, 0))
    hid = pl.BlockSpec((tm, D_FF), lambda i: (i, 0))
    wres = _resident((D_FF, D_MODEL))
    vec = _full((1, D_MODEL))
    hid_shape = jax.ShapeDtypeStruct((n, D_FF), BF16)
    return _hosted(
        body, comm, name=name, grid=(n // tm,),
        in_specs=[hid, hid, row, row, wres, wres, wres, vec, vec],
        out_specs=[row, hid, hid, hid, row, vec, vec],
        out_shape=[jax.ShapeDtypeStruct((n, D_MODEL), F32), hid_shape, hid_shape, hid_shape,
                   jax.ShapeDtypeStruct((n, D_MODEL), BF16),
                   jax.ShapeDtypeStruct((1, D_MODEL), F32), jax.ShapeDtypeStruct((1, D_MODEL), F32)],
        scratch_shapes=[], compiler_params=_params(("arbitrary",)),
        args=[h1s, h3s, z, dout, w1t, w3t, w2, g, b])


def _operand_spec(arr, mode, tn, width, parts):
    if mode == "shared":
        return pl.BlockSpec((tn, width), lambda s, k: (k, 0))
    assert mode == "cols" and arr.shape[1] == parts * width
    return pl.BlockSpec((tn, width), lambda s, k: (k, s))


def _grad_matmul(name, a, a_mode, ka, b, b_mode, kb, colsum=False, comm=None, parts=N_CHIPS):
    n = a.shape[-2]
    tn = min(GRAD_ROWS, n)
    nk = n // tn

    def body(*refs):
        if colsum:
            a_ref, b_ref, o_ref, cs_ref, acc = refs
        else:
            a_ref, b_ref, o_ref, acc = refs
        k = pl.program_id(1)
        av = a_ref[...]
        bv = b_ref[...]

        @pl.when(k == 0)
        def _():
            acc[...] = jnp.zeros_like(acc)
            if colsum:
                cs_ref[...] = jnp.zeros_like(cs_ref)

        acc[...] += _mm_tn(av, bv)
        if colsum:
            cs_ref[...] += jnp.sum(bv.astype(F32), axis=0, keepdims=True)

        @pl.when(k == nk - 1)
        def _():
            o_ref[0] = _bf(acc[...])

    out_specs = [pl.BlockSpec((1, ka, kb), lambda s, k: (s, 0, 0))]
    out_shape = [jax.ShapeDtypeStruct((parts, ka, kb), BF16)]
    if colsum:
        out_specs.append(pl.BlockSpec((1, kb), lambda s, k: (0, s)))
        out_shape.append(jax.ShapeDtypeStruct((1, parts * kb), F32))
    return _hosted(
        body, comm, name=name, grid=(parts, nk),
        in_specs=[_operand_spec(a, a_mode, tn, ka, parts), _operand_spec(b, b_mode, tn, kb, parts)],
        out_specs=out_specs, out_shape=out_shape,
        scratch_shapes=[pltpu.VMEM((ka, kb), F32)],
        compiler_params=_params(("arbitrary", "arbitrary")), args=[a, b])


def _in_proj(x1, w_in_g, b_in, comm=None):
    n = x1.shape[0]
    tm = min(ROW_TILE, n)

    def body(x_ref, w_ref, b_ref, o_ref):
        o_ref[...] = _mm(x_ref[...], w_ref[0]) + b_ref[...]

    return _hosted(
        body, comm, name="in_proj", grid=(N_CHIPS, n // tm),
        in_specs=[pl.BlockSpec((tm, D_MODEL), lambda j, i: (i, 0)),
                  pl.BlockSpec((1, D_MODEL, IN_SHARD), lambda j, i: (j, 0, 0)),
                  pl.BlockSpec((1, IN_SHARD), lambda j, i: (0, j))],
        out_specs=[pl.BlockSpec((tm, IN_SHARD), lambda j, i: (i, j))],
        out_shape=[jax.ShapeDtypeStruct((n, D_IN), F32)],
        scratch_shapes=[],
        compiler_params=_params(("arbitrary", "arbitrary")), args=[x1, w_in_g, b_in])


def _in_proj_dx(dproj, w_in_g, dz2, comm=None):
    n = dproj.shape[0]
    tm = min(ROW_TILE, n)

    def body(dp_ref, w_ref, dz_ref, o_ref, acc):
        j = pl.program_id(1)

        @pl.when(j == 0)
        def _():
            acc[...] = jnp.zeros_like(acc)

        acc[...] += _mm_nt(dp_ref[...], w_ref[0])

        @pl.when(j == N_CHIPS - 1)
        def _():
            o_ref[...] = DEEPNORM_ALPHA * dz_ref[...] + acc[...]

    return _hosted(
        body, comm, name="in_proj_dx", grid=(n // tm, N_CHIPS),
        in_specs=[pl.BlockSpec((tm, IN_SHARD), lambda i, j: (i, j)),
                  pl.BlockSpec((1, D_MODEL, IN_SHARD), lambda i, j: (j, 0, 0)),
                  pl.BlockSpec((tm, D_MODEL), lambda i, j: (i, 0))],
        out_specs=[pl.BlockSpec((tm, D_MODEL), lambda i, j: (i, 0))],
        out_shape=[jax.ShapeDtypeStruct((n, D_MODEL), F32)],
        scratch_shapes=[pltpu.VMEM((tm, D_MODEL), F32)],
        compiler_params=_params(("arbitrary", "arbitrary")), args=[dproj, w_in_g, dz2])


def _rope_tables(seq_len):
    pos = jnp.arange(seq_len, dtype=F32)
    inv_freq = ROPE_THETA ** (-jnp.arange(0, ROPE_DIM, 2, dtype=F32) / ROPE_DIM)
    ang = pos[:, None] * inv_freq[None, :]
    cos, sin = jnp.cos(ang), jnp.sin(ang)
    half = ROPE_DIM // 2
    rest = HEAD_DIM - ROPE_DIM
    ones = jnp.ones((seq_len, rest), F32)
    zeros = jnp.zeros((seq_len, rest), F32)
    zh = jnp.zeros((seq_len, half), F32)
    c = jnp.concatenate([cos, cos, ones], axis=1)
    sa = jnp.concatenate([-sin, zh, zeros], axis=1)
    sb = jnp.concatenate([zh, sin, zeros], axis=1)
    reps = LANES // HEAD_DIM
    return tuple(jnp.tile(t, (1, reps)) for t in (c, sa, sb))


def _rope(t, c, sa, sb):
    w = t.shape[1]
    reps = w // LANES
    half = ROPE_DIM // 2
    return (t * jnp.tile(c, (1, reps)) + pltpu.roll(t, w - half, 1) * jnp.tile(sa, (1, reps))
            + pltpu.roll(t, half, 1) * jnp.tile(sb, (1, reps)))


def _rope_transposed(g, c, sa, sb):
    w = g.shape[1]
    reps = w // LANES
    half = ROPE_DIM // 2
    return (g * jnp.tile(c, (1, reps)) + pltpu.roll(g * jnp.tile(sa, (1, reps)), half, 1)
            + pltpu.roll(g * jnp.tile(sb, (1, reps)), w - half, 1))


GROUP = N_Q_HEADS // N_KV_HEADS


def _attn_mask(n):
    rows = GROUP * ATTN_BLOCK
    qi = lax.broadcasted_iota(jnp.int32, (rows, 2 * ATTN_BLOCK), 0) % ATTN_BLOCK
    kj = lax.broadcasted_iota(jnp.int32, (rows, 2 * ATTN_BLOCK), 1)
    dist = qi + ATTN_BLOCK - kj
    return (dist >= 0) & (dist < ATTN_BLOCK) & (n * ATTN_BLOCK + kj - ATTN_BLOCK >= 0)


def _both_halves(t_pair, which):
    lo = lax.broadcasted_iota(jnp.int32, t_pair.shape, 1) < HEAD_DIM
    swapped = pltpu.roll(t_pair, HEAD_DIM, 1)
    return _bf(jnp.where(lo, t_pair, swapped) if which == 0 else jnp.where(lo, swapped, t_pair))


def _stack_heads(ref_or_val, kh):
    lo = lax.broadcasted_iota(jnp.int32, (ATTN_BLOCK, LANES), 1) < HEAD_DIM
    rows = []
    for gp in range(GROUP // 2):
        pair = kh * (GROUP // 2) + gp
        t = ref_or_val[:, pair * LANES:(pair + 1) * LANES]
        rows += [jnp.where(lo, t, jnp.zeros_like(t)), jnp.where(lo, jnp.zeros_like(t), t)]
    return jnp.concatenate(rows, axis=0)


def _unstack_pairs(stacked):
    lo = lax.broadcasted_iota(jnp.int32, (ATTN_BLOCK, LANES), 1) < HEAD_DIM
    b = ATTN_BLOCK
    return [jnp.where(lo, stacked[2 * gp * b:(2 * gp + 1) * b], stacked[(2 * gp + 1) * b:(2 * gp + 2) * b])
            for gp in range(GROUP // 2)]


def _sink_column(sink_ref, kh):
    row = lax.broadcasted_iota(jnp.int32, (GROUP * ATTN_BLOCK, 1), 0)
    col = jnp.full((GROUP * ATTN_BLOCK, 1), sink_ref[0, kh * GROUP + GROUP - 1], F32)
    for i in reversed(range(GROUP - 1)):
        col = jnp.where(row < (i + 1) * ATTN_BLOCK, sink_ref[0, kh * GROUP + i], col)
    return col


def _attn_probs(q_masked, k_sel, mask, sink):
    s = _mm_nt(q_masked, k_sel) * (HEAD_DIM ** -0.5)
    s = jnp.where(mask, s, NEG_INF)
    m = jnp.maximum(jnp.max(s, axis=-1, keepdims=True), sink)
    p = jnp.exp(s - m)
    e_sink = jnp.exp(sink - m)
    denom = jnp.sum(p, axis=-1, keepdims=True) + e_sink
    return p / denom, e_sink / denom


def _attn_fwd(proj, tabs, sinks):
    n_tok = proj.shape[0]
    nb = n_tok // ATTN_BLOCK
    group = N_Q_HEADS // N_KV_HEADS

    def body(q_ref, k_ref, v_ref, c_ref, sa_ref, sb_ref, sink_ref, y_ref, kprev, vprev):
        n = pl.program_id(0)

        @pl.when(n == 0)
        def _():
            kprev[...] = jnp.zeros_like(kprev)
            vprev[...] = jnp.zeros_like(vprev)

        c, sa, sb = c_ref[...], sa_ref[...], sb_ref[...]
        qr = _bf(_rope(q_ref[...], c, sa, sb))
        kr = _rope(k_ref[...], c, sa, sb)
        vc = v_ref[...]
        kk = jnp.concatenate([kprev[...], kr], axis=0)
        vv = jnp.concatenate([vprev[...], vc], axis=0)
        kprev[...] = kr
        vprev[...] = vc
        mask = _attn_mask(n)
        for kh in range(N_KV_HEADS):
            r, which = divmod(kh, 2)
            kb = _both_halves(kk[:, r * LANES:(r + 1) * LANES], which)
            vb = _both_halves(vv[:, r * LANES:(r + 1) * LANES], which)
            probs, _ = _attn_probs(_stack_heads(qr, kh), kb, mask, _sink_column(sink_ref, kh))
            for gp, out in enumerate(_unstack_pairs(_mm(_bf(probs), vb))):
                pair = kh * (GROUP // 2) + gp
                y_ref[:, pair * LANES:(pair + 1) * LANES] = _bf(out)

    blk = lambda width, col: pl.BlockSpec((ATTN_BLOCK, width), lambda n: (n, col))
    tab = pl.BlockSpec((ATTN_BLOCK, LANES), lambda n: (n, 0))
    kvw = N_KV_HEADS * HEAD_DIM
    return pl.pallas_call(
        body, name="attn_fwd", grid=(nb,),
        in_specs=[blk(D_MODEL, 0), blk(kvw, COL_K), blk(kvw, COL_V), tab, tab, tab,
                  pl.BlockSpec(memory_space=pltpu.SMEM)],
        out_specs=pl.BlockSpec((ATTN_BLOCK, D_MODEL), lambda n: (n, 0)),
        out_shape=jax.ShapeDtypeStruct((n_tok, D_MODEL), BF16),
        scratch_shapes=[pltpu.VMEM((ATTN_BLOCK, kvw), F32), pltpu.VMEM((ATTN_BLOCK, kvw), F32)],
        compiler_params=_params(("arbitrary",)),
    )(proj, proj, proj, *tabs, sinks)


def _attn_bwd(proj, dy, tabs, sinks, comm=None):
    n_tok = proj.shape[0]
    nb = n_tok // ATTN_BLOCK
    group = N_Q_HEADS // N_KV_HEADS
    kvw = N_KV_HEADS * HEAD_DIM

    def body(q_ref, k_ref, v_ref, do_ref, c_ref, sa_ref, sb_ref, cp_ref, sap_ref, sbp_ref, sink_ref,
             dq_ref, dkv_ref, dsink_ref, kprev, vprev, dkc, dvc):
        n = pl.program_id(0)

        @pl.when(n == 0)
        def _():
            for ref in (kprev, vprev, dkc, dvc, dsink_ref):
                ref[...] = jnp.zeros_like(ref)

        prev_tabs = (cp_ref[...], sap_ref[...], sbp_ref[...])

        @pl.when(n < nb)
        def _():
            c, sa, sb = c_ref[...], sa_ref[...], sb_ref[...]
            qr = _bf(_rope(q_ref[...], c, sa, sb))
            kr = _rope(k_ref[...], c, sa, sb)
            vc = v_ref[...]
            kk = jnp.concatenate([kprev[...], kr], axis=0)
            vv = jnp.concatenate([vprev[...], vc], axis=0)
            kprev[...] = kr
            vprev[...] = vc
            mask = _attn_mask(n)
            lane = lax.broadcasted_iota(jnp.int32, (1, LANES), 1)
            lo2 = lax.broadcasted_iota(jnp.int32, (2 * ATTN_BLOCK, LANES), 1) < HEAD_DIM
            dsink = jnp.zeros((1, LANES), F32)
            dq_pairs = []
            dk_full = []
            dv_full = []
            for kh in range(N_KV_HEADS):
                r, which = divmod(kh, 2)
                kb = _both_halves(kk[:, r * LANES:(r + 1) * LANES], which)
                vb = _both_halves(vv[:, r * LANES:(r + 1) * LANES], which)
                qs = _stack_heads(qr, kh)
                dos = _bf(_stack_heads(do_ref, kh))
                probs, p_sink = _attn_probs(qs, kb, mask, _sink_column(sink_ref, kh))
                dp = _mm_nt(dos, vb)
                delta = jnp.sum(probs * dp, axis=-1, keepdims=True)
                ds = _bf(probs * (dp - delta) * (HEAD_DIM ** -0.5))
                sink_terms = p_sink * delta
                for i in range(GROUP):
                    head_sum = jnp.sum(sink_terms[i * ATTN_BLOCK:(i + 1) * ATTN_BLOCK])
                    dsink = dsink + jnp.where(lane == kh * GROUP + i, -head_sum, 0.0)
                dq_pairs += _unstack_pairs(_mm(ds, kb))
                dk_acc = _mm_tn(ds, qs)
                dv_acc = _mm_tn(_bf(probs), dos)
                dk_full.append(dk_acc + pltpu.roll(dk_acc, HEAD_DIM, 1))
                dv_full.append(dv_acc + pltpu.roll(dv_acc, HEAD_DIM, 1))
            dk_pairs = [jnp.where(lo2, dk_full[2 * r], dk_full[2 * r + 1]) for r in range(N_KV_HEADS // 2)]
            dv_pairs = [jnp.where(lo2, dv_full[2 * r], dv_full[2 * r + 1]) for r in range(N_KV_HEADS // 2)]
            dsink_ref[...] += dsink
            dq_ref[...] = _bf(_rope_transposed(jnp.concatenate(dq_pairs, axis=1), c, sa, sb))
            dk_all = jnp.concatenate(dk_pairs, axis=1)
            dv_all = jnp.concatenate(dv_pairs, axis=1)
            dkv_ref[:, :kvw] = _bf(_rope_transposed(dkc[...] + dk_all[:ATTN_BLOCK], *prev_tabs))
            dkv_ref[:, kvw:] = _bf(dvc[...] + dv_all[:ATTN_BLOCK])
            dkc[...] = dk_all[ATTN_BLOCK:]
            dvc[...] = dv_all[ATTN_BLOCK:]

        @pl.when(n == nb)
        def _():
            dkv_ref[:, :kvw] = _bf(_rope_transposed(dkc[...], *prev_tabs))
            dkv_ref[:, kvw:] = _bf(dvc[...])

    cur = lambda n: jnp.minimum(n, nb - 1)
    prev = lambda n: jnp.maximum(n - 1, 0)
    blk = lambda width, col: pl.BlockSpec((ATTN_BLOCK, width), lambda n: (cur(n), col))
    tab = pl.BlockSpec((ATTN_BLOCK, LANES), lambda n: (cur(n), 0))
    tabp = pl.BlockSpec((ATTN_BLOCK, LANES), lambda n: (prev(n), 0))
    return _hosted(
        body, comm, name="attn_bwd", grid=(nb + 1,),
        in_specs=[blk(D_MODEL, 0), blk(kvw, COL_K), blk(kvw, COL_V), blk(D_MODEL, 0), tab, tab, tab, tabp, tabp, tabp,
                  pl.BlockSpec(memory_space=pltpu.SMEM)],
        out_specs=[pl.BlockSpec((ATTN_BLOCK, D_MODEL), lambda n: (cur(n), 0)),
                   pl.BlockSpec((ATTN_BLOCK, 2 * kvw), lambda n: (prev(n), 0)),
                   pl.BlockSpec((1, LANES), lambda n: (0, 0))],
        out_shape=[jax.ShapeDtypeStruct((n_tok, D_MODEL), BF16), jax.ShapeDtypeStruct((n_tok, 2 * kvw), BF16),
                   jax.ShapeDtypeStruct((1, LANES), F32)],
        scratch_shapes=[pltpu.VMEM((ATTN_BLOCK, kvw), F32)] * 4,
        compiler_params=_params(("arbitrary",)), args=[proj, proj, proj, dy, *tabs, *tabs, sinks])


def _bmm(a, b):
    return lax.dot_general(a, b, (((2,), (1,)), ((0,), (0,))), preferred_element_type=F32)


def _bmm_nt(a, b):
    return lax.dot_general(a, b, (((2,), (2,)), ((0,), (0,))), preferred_element_type=F32)


def _bmm_tn(a, b):
    return lax.dot_general(a, b, (((1,), (1,)), ((0,), (0,))), preferred_element_type=F32)


def _tril(cb, upper=False):
    shape = (cb, HGRN_CHUNK, HGRN_CHUNK)
    r, c = lax.broadcasted_iota(jnp.int32, shape, 1), lax.broadcasted_iota(jnp.int32, shape, 2)
    return (r <= c) if upper else (r >= c)


def _tri_matmul(x, upper):
    return lax.dot_general(_tril(x.shape[0], upper).astype(F32), x, (((2,), (1,)), ((0,), (0,))),
                           precision=lax.Precision.HIGHEST, preferred_element_type=F32)


@jax.custom_vjp
def _chunk_cumsum(x):
    return _tri_matmul(x, False)


_chunk_cumsum.defvjp(lambda x: (_tri_matmul(x, False), None), lambda _, g: (_tri_matmul(g, True),))


def _hg_elem(fl, qh, lb):
    f = lb + (1.0 - lb) * _sig(fl)
    k = 1.0 - f
    gc = _chunk_cumsum(jnp.log(f))
    last = lax.broadcasted_iota(jnp.int32, gc.shape, 1) == HGRN_CHUNK - 1
    g_last = jnp.sum(jnp.where(last, gc, 0.0), axis=1, keepdims=True)
    q = qh * _sig(qh)
    return q * jnp.exp(gc), k * jnp.exp(-gc), k * jnp.exp(g_last - gc), jnp.exp(g_last)


def _hg_out(q_dec, k_inv, v, st):
    sc = jnp.where(_tril(q_dec.shape[0]), _bmm_nt(_bf(q_dec), _bf(k_inv)), 0.0)
    return _bmm(_bf(sc), _bf(v)) + _bmm_nt(_bf(q_dec), _bf(st)), sc


def _hg_post(o, og, ng):
    on = o * lax.rsqrt(jnp.mean(o * o, axis=-1, keepdims=True) + RMS_EPS) * ng
    return on * (og * _sig(og))


def _hgrn_specs(n_tok, rev):
    nc = n_tok // HGRN_CHUNK
    cb = min(HGRN_CHUNKS_PER_STEP, nc)
    nt = nc // cb
    rows = cb * HGRN_CHUNK
    tt = (lambda t: nt - 1 - t) if rev else (lambda t: t)
    col = lambda base: pl.BlockSpec((rows, LANES), lambda h, t: (tt(t), base + h))
    head_vec = pl.BlockSpec((1, LANES), lambda h, t: (0, h))
    one_vec = pl.BlockSpec((1, LANES), lambda h, t: (0, 0))
    state = pl.BlockSpec((1, cb, HGRN_DK, HGRN_DK), lambda h, t: (h, tt(t), 0, 0))
    return nc, cb, nt, col, head_vec, one_vec, state


def _hgrn_fwd(proj, lb, ng):
    n_tok = proj.shape[0]
    nc, cb, nt, col, head_vec, one_vec, state = _hgrn_specs(n_tok, False)

    def body(fl_ref, qh_ref, ih_ref, og_ref, lb_ref, ng_ref, y_ref, st_ref, s_acc):
        @pl.when(pl.program_id(1) == 0)
        def _():
            s_acc[...] = jnp.zeros_like(s_acc)

        chunks = lambda ref: ref[...].reshape(cb, HGRN_CHUNK, LANES)
        q_dec, k_inv, k_end, decay = _hg_elem(chunks(fl_ref), chunks(qh_ref), lb_ref[...])
        v = chunks(ih_ref)
        upd = _bmm_tn(_bf(v), _bf(k_end))
        st = s_acc[...]
        for ci in range(cb):
            st_ref[0, ci] = st
            st = st * decay[ci] + upd[ci]
        s_acc[...] = st
        o, _ = _hg_out(q_dec, k_inv, v, st_ref[0])
        y_ref[...] = _bf(_hg_post(o, chunks(og_ref), ng_ref[...]).reshape(cb * HGRN_CHUNK, LANES))

    return pl.pallas_call(
        body, name="hgrn_fwd", grid=(HGRN_HEADS, nt),
        in_specs=[col(COL_F), col(COL_QH), col(COL_IH), col(COL_OG), head_vec, one_vec],
        out_specs=[col(0), state],
        out_shape=[jax.ShapeDtypeStruct((n_tok, D_MODEL), BF16),
                   jax.ShapeDtypeStruct((HGRN_HEADS, nc, HGRN_DK, HGRN_DK), F32)],
        scratch_shapes=[pltpu.VMEM((HGRN_DK, HGRN_DK), F32)],
        compiler_params=_params(("arbitrary", "arbitrary")),
    )(proj, proj, proj, proj, lb, ng)


def _hgrn_bwd(proj, lb, ng, states, dy):
    n_tok = proj.shape[0]
    nc, cb, nt, col, head_vec, one_vec, state = _hgrn_specs(n_tok, True)

    def body(fl_ref, qh_ref, ih_ref, og_ref, lb_ref, ng_ref, st_ref, dy_ref,
             dfl_ref, dqh_ref, dih_ref, dog_ref, dlb_ref, dng_ref, g_acc, g_all):
        h = pl.program_id(0)
        t = pl.program_id(1)

        @pl.when(t == 0)
        def _():
            g_acc[...] = jnp.zeros_like(g_acc)
            dlb_ref[...] = jnp.zeros_like(dlb_ref)

        @pl.when((t == 0) & (h == 0))
        def _():
            dng_ref[...] = jnp.zeros_like(dng_ref)

        chunks = lambda ref: ref[...].reshape(cb, HGRN_CHUNK, LANES)
        flat = lambda val: _bf(val.reshape(cb * HGRN_CHUNK, LANES))
        (q_dec, k_inv, k_end, decay), elem_vjp = jax.vjp(_hg_elem, chunks(fl_ref), chunks(qh_ref), lb_ref[...])
        v = chunks(ih_ref)
        st = st_ref[0]
        o, sc = _hg_out(q_dec, k_inv, v, st)
        _, post_vjp = jax.vjp(_hg_post, o, chunks(og_ref), ng_ref[...])
        do, dog, dng = post_vjp(chunks(dy_ref))
        dob, vb, qb = _bf(do), _bf(v), _bf(q_dec)
        dsc = _bf(jnp.where(_tril(cb), _bmm_nt(dob, vb), 0.0))
        p = _bmm_tn(dob, qb)
        g = g_acc[...]
        for ci in reversed(range(cb)):
            g_all[ci] = g
            g = g * decay[ci] + p[ci]
        g_acc[...] = g
        g = g_all[...]
        gb = _bf(g)
        dq_dec = _bmm(dsc, _bf(k_inv)) + _bmm(dob, _bf(st))
        dk_inv = _bmm_tn(dsc, qb)
        dv = _bmm_tn(_bf(sc), dob) + _bmm_nt(_bf(k_end), gb)
        dk_end = _bmm(vb, gb)
        ddecay = jnp.sum(st * g, axis=1, keepdims=True)
        dfl, dqh, dlb = elem_vjp((dq_dec, dk_inv, dk_end, ddecay))
        dfl_ref[...] = flat(dfl)
        dqh_ref[...] = flat(dqh)
        dih_ref[...] = flat(dv)
        dog_ref[...] = flat(dog)
        dlb_ref[...] += dlb
        dng_ref[...] += dng

    out_col = jax.ShapeDtypeStruct((n_tok, D_MODEL), BF16)
    return pl.pallas_call(
        body, name="hgrn_bwd", grid=(HGRN_HEADS, nt),
        in_specs=[col(COL_F), col(COL_QH), col(COL_IH), col(COL_OG), head_vec, one_vec, state, col(0)],
        out_specs=[col(0), col(0), col(0), col(0), head_vec, one_vec],
        out_shape=[out_col, out_col, out_col, out_col,
                   jax.ShapeDtypeStruct((1, D_MODEL), F32), jax.ShapeDtypeStruct((1, LANES), F32)],
        scratch_shapes=[pltpu.VMEM((HGRN_DK, HGRN_DK), F32), pltpu.VMEM((cb, HGRN_DK, HGRN_DK), F32)],
        compiler_params=_params(("arbitrary", "arbitrary")),
    )(proj, proj, proj, proj, lb, ng, states, dy)


def _lb_fwd(lb_logits):
    def lb_of(l0, l1):
        m = jnp.maximum(l0, l1)
        e0, e1 = jnp.exp(l0 - m), jnp.exp(l1 - m)
        return e0 / (e0 + e1)

    def body(l_ref, o_ref):
        o_ref[...] = lb_of(l_ref[0:1, :], l_ref[1:2, :])

    lb = pl.pallas_call(body, name="lb_fwd", out_shape=jax.ShapeDtypeStruct((1, D_MODEL), F32))(lb_logits)
    return lb, lb_of


def _gate_specs(tm):
    return [pl.BlockSpec((tm, 512), lambda i, c=c: (i, c)) for c in (COL_GA, COL_GA + 1, COL_GH, COL_GH + 1)]


def _mix_fwd(y_attn, y_hgrn, proj, x1, w_pa, w_ph, w_out, g, b):
    n = x1.shape[0]
    tm = min(MIX_TILE, n)

    def body(ya_ref, yh_ref, ga0, ga1, gh0, gh1, x_ref, wpa, wph, wo, g_ref, b_ref, z_ref, o_ref):
        ya = _mm(ya_ref[...], wpa[...])
        yh = _mm(yh_ref[...], wph[...])
        ga = jnp.concatenate([ga0[...], ga1[...]], axis=1)
        gh = jnp.concatenate([gh0[...], gh1[...]], axis=1)
        merged = _sig(ga) * ya + _sig(gh) * yh
        z = DEEPNORM_ALPHA * x_ref[...] + _mm(_bf(merged), wo[...])
        z_ref[...] = z
        o_ref[...] = _ln(z, g_ref[...], b_ref[...])

    row = pl.BlockSpec((tm, D_MODEL), lambda i: (i, 0))
    sq = _full((D_MODEL, D_MODEL))
    vec = _full((1, D_MODEL))
    return pl.pallas_call(
        body, name="mix_fwd", grid=(n // tm,),
        in_specs=[row, row, *_gate_specs(tm), row, sq, sq, sq, vec, vec],
        out_specs=[row, row], out_shape=[jax.ShapeDtypeStruct((n, D_MODEL), F32)] * 2,
        compiler_params=_params(("arbitrary",)),
    )(y_attn, y_hgrn, proj, proj, proj, proj, x1, w_pa, w_ph, w_out, g, b)


def _mix_bwd(dx2, z2, y_attn, y_hgrn, proj, w_pa, w_ph, w_out, g, b):
    n = z2.shape[0]
    tm = min(MIX_TILE, n)

    def body(do_ref, z_ref, ya_ref, yh_ref, ga0, ga1, gh0, gh1, wpa, wph, wo, g_ref, b_ref,
             dz_ref, dzb_ref, mg_ref, dya_ref, dyh_ref, dyat_ref, dyhg_ref, dgt_ref, dg_ref, db_ref):
        _, vjp = jax.vjp(_ln, z_ref[...], g_ref[...], b_ref[...])
        dz, dg, db = vjp(do_ref[...])

        @pl.when(pl.program_id(0) == 0)
        def _():
            dg_ref[...] = jnp.zeros_like(dg_ref)
            db_ref[...] = jnp.zeros_like(db_ref)

        dg_ref[...] += dg
        db_ref[...] += db
        dz_ref[...] = dz
        ya = _mm(ya_ref[...], wpa[...])
        yh = _mm(yh_ref[...], wph[...])
        ga = jnp.concatenate([ga0[...], ga1[...]], axis=1)
        gh = jnp.concatenate([gh0[...], gh1[...]], axis=1)

        def merge(ga, gh, ya, yh):
            return _sig(ga) * ya + _sig(gh) * yh

        merged, merge_vjp = jax.vjp(merge, ga, gh, ya, yh)
        mg_ref[...] = _bf(merged)
        dzb = _bf(dz)
        dzb_ref[...] = dzb
        dmerged = _mm_nt(dzb, wo[...])
        dga, dgh, dya, dyh = merge_vjp(dmerged)
        dya = _bf(dya)
        dyh = _bf(dyh)
        dya_ref[...] = dya
        dyh_ref[...] = dyh
        dgt_ref[:, :D_MODEL] = _bf(dga)
        dgt_ref[:, D_MODEL:] = _bf(dgh)
        dyat_ref[...] = _mm_nt(dya, wpa[...])
        dyhg_ref[...] = _mm_nt(dyh, wph[...])

    row = pl.BlockSpec((tm, D_MODEL), lambda i: (i, 0))
    row2 = pl.BlockSpec((tm, 2 * D_MODEL), lambda i: (i, 0))
    sq = _full((D_MODEL, D_MODEL))
    vec = _full((1, D_MODEL))
    f32_row = jax.ShapeDtypeStruct((n, D_MODEL), F32)
    bf_row = jax.ShapeDtypeStruct((n, D_MODEL), BF16)
    vec_shape = jax.ShapeDtypeStruct((1, D_MODEL), F32)
    return pl.pallas_call(
        body, name="mix_bwd", grid=(n // tm,),
        in_specs=[row, row, row, row, *_gate_specs(tm), sq, sq, sq, vec, vec],
        out_specs=[row, row, row, row, row, row, row, row2, vec, vec],
        out_shape=[f32_row, bf_row, bf_row, bf_row, bf_row, f32_row, f32_row,
                   jax.ShapeDtypeStruct((n, 2 * D_MODEL), BF16), vec_shape, vec_shape],
        compiler_params=_params(("arbitrary",)),
    )(dx2, z2, y_attn, y_hgrn, proj, proj, proj, proj, w_pa, w_ph, w_out, g, b)


def _position():
    x, y, c = lax.axis_index("x"), lax.axis_index("y"), lax.axis_index("c")
    chips = [(1 - x, y), (x, 1 - y), (1 - x, 1 - y)]
    return x, y, c, chips


def _any_specs(k):
    return [pl.BlockSpec(memory_space=pl.ANY)] * k


class _GatherWeights:
    def __init__(self, shards):
        nw = len(shards)
        self.inputs = list(shards)
        self.out_shape = [jax.ShapeDtypeStruct((N_CHIPS, *s.shape), s.dtype) for s in shards]
        self.scratch = [pltpu.SemaphoreType.DMA((nw,)), pltpu.SemaphoreType.DMA((nw * 6,)),
                        pltpu.SemaphoreType.DMA((nw * 6,))]

    def _copies(self, ins, outs, sems):
        nw = len(ins)
        local_sem, send_sem, recv_sem = sems
        x, y, c, chips = _position()
        me = 2 * x + y
        sibling = (x, y, 1 - c)
        half_rows = [s.shape[0] // 2 for s in self.inputs]

        def half(w, chip_idx, which):
            return outs[w].at[chip_idx, pl.ds(which * half_rows[w], half_rows[w])]

        def remote(w, k, src, dst, to):
            return pltpu.make_async_remote_copy(src_ref=src, dst_ref=dst, send_sem=send_sem.at[w * 6 + k],
                                                recv_sem=recv_sem.at[w * 6 + k], device_id=to, device_id_type=MESH)

        local = [pltpu.make_async_copy(ins[w], outs[w].at[me], local_sem.at[w]) for w in range(nw)]
        first = [remote(w, j, ins[w].at[pl.ds(c * half_rows[w], half_rows[w])], half(w, me, c), (px, py, c))
                 for w in range(nw) for j, (px, py) in enumerate(chips)]
        landed = [half(w, 2 * px + py, c) for w in range(nw) for (px, py) in chips]
        arrive = [remote(w, j, landed[w * 3 + j], landed[w * 3 + j], (px, py, c))
                  for w in range(nw) for j, (px, py) in enumerate(chips)]
        passed = [remote(w, 3 + j, landed[w * 3 + j], landed[w * 3 + j], sibling) for w in range(nw) for j in range(3)]
        from_sibling = [remote(w, 3 + j, half(w, 2 * px + py, 1 - c), half(w, 2 * px + py, 1 - c), sibling)
                        for w in range(nw) for j, (px, py) in enumerate(chips)]
        return local, first, arrive, passed, from_sibling

    def start(self, ins, outs, sems):
        local, first, _, _, _ = self._copies(ins, outs, sems)
        for cp in local + first:
            cp.start()

    def finish(self, ins, outs, sems):
        local, first, arrive, passed, from_sibling = self._copies(ins, outs, sems)
        for cp_in, cp_on in zip(arrive, passed):
            cp_in.wait_recv()
            cp_on.start()
        for cp in from_sibling:
            cp.wait_recv()
        for cp in first + passed:
            cp.wait_send()
        for cp in local:
            cp.wait()


class _ExchangeGrads:
    def __init__(self, grads):
        nw = len(grads)
        self.inputs = list(grads)
        self.out_shape = [jax.ShapeDtypeStruct(g.shape, g.dtype) for g in grads]
        self.scratch = [pltpu.SemaphoreType.DMA((nw,)), pltpu.SemaphoreType.DMA((nw * 3,)),
                        pltpu.SemaphoreType.DMA((nw * 3,))]

    def _copies(self, ins, outs, sems):
        nw = len(ins)
        local_sem, send_sem, recv_sem = sems
        x, y, c, chips = _position()
        me = 2 * x + y

        def remote(w, j, src, dst, chip):
            return pltpu.make_async_remote_copy(src_ref=src, dst_ref=dst, send_sem=send_sem.at[w * 3 + j],
                                                recv_sem=recv_sem.at[w * 3 + j], device_id=(*chip, c),
                                                device_id_type=MESH)

        local = [pltpu.make_async_copy(ins[w].at[me], outs[w].at[me], local_sem.at[w]) for w in range(nw)]
        sends = [remote(w, j, ins[w].at[2 * px + py], outs[w].at[me], (px, py))
                 for w in range(nw) for j, (px, py) in enumerate(chips)]
        arrive = [remote(w, j, outs[w].at[2 * px + py], outs[w].at[2 * px + py], (px, py))
                  for w in range(nw) for j, (px, py) in enumerate(chips)]
        return local, sends, arrive

    def start(self, ins, outs, sems):
        local, sends, _ = self._copies(ins, outs, sems)
        for cp in local + sends:
            cp.start()

    def finish(self, ins, outs, sems):
        local, sends, arrive = self._copies(ins, outs, sems)
        for cp in arrive:
            cp.wait_recv()
        for cp in sends:
            cp.wait_send()
        for cp in local:
            cp.wait()


def _hosted(body, comm, *, name, grid, in_specs, out_specs, out_shape, scratch_shapes, compiler_params, args):
    if comm is None:
        res = pl.pallas_call(body, name=name, grid=grid, in_specs=in_specs, out_specs=out_specs, out_shape=out_shape,
                             scratch_shapes=scratch_shapes, compiler_params=compiler_params)(*args)
        return list(res), []
    n_in, n_out, n_scr = len(in_specs), len(out_specs), len(scratch_shapes)
    c_in, c_out = len(comm.inputs), len(comm.out_shape)

    def hosted_body(*refs):
        refs = list(refs)
        cut = lambda k: (refs[:k], refs[k:])
        main_in, refs = cut(n_in)
        comm_in, refs = cut(c_in)
        main_out, refs = cut(n_out)
        comm_out, refs = cut(c_out)
        main_scr, comm_scr = cut(n_scr)
        ids = [pl.program_id(a) for a in range(len(grid))]
        first = functools.reduce(jnp.logical_and, [i == 0 for i in ids])
        last = functools.reduce(jnp.logical_and, [i == g - 1 for i, g in zip(ids, grid)])

        @pl.when(first)
        def _():
            comm.start(comm_in, comm_out, comm_scr)

        body(*main_in, *main_out, *main_scr)

        @pl.when(last)
        def _():
            comm.finish(comm_in, comm_out, comm_scr)

    res = pl.pallas_call(
        hosted_body, name=name, grid=grid, in_specs=[*in_specs, *_any_specs(c_in)],
        out_specs=[*out_specs, *_any_specs(c_out)], out_shape=[*out_shape, *comm.out_shape],
        scratch_shapes=[*scratch_shapes, *comm.scratch], compiler_params=compiler_params,
    )(*args, *comm.inputs)
    return list(res[:n_out]), list(res[n_out:])


def _run_comm(name, comm):
    def body(*refs):
        refs = list(refs)
        c_in, c_out = len(comm.inputs), len(comm.out_shape)
        ins, outs, sems = refs[:c_in], refs[c_in:c_in + c_out], refs[c_in + c_out:]
        comm.start(ins, outs, sems)
        comm.finish(ins, outs, sems)

    return list(pl.pallas_call(
        body, name=name, in_specs=_any_specs(len(comm.inputs)), out_specs=_any_specs(len(comm.out_shape)),
        out_shape=comm.out_shape, scratch_shapes=comm.scratch,
    )(*comm.inputs))


def _sum_slots(name, slots):
    _, rows, cols = slots.shape
    tr = _update_rows(rows)

    def body(s_ref, o_ref):
        acc = s_ref[0].astype(F32)
        for i in range(1, N_CHIPS):
            acc = acc + s_ref[i].astype(F32)
        o_ref[...] = acc

    return pl.pallas_call(
        body, name=name, grid=(rows // tr,),
        in_specs=[pl.BlockSpec((N_CHIPS, tr, cols), lambda i: (0, i, 0))],
        out_specs=pl.BlockSpec((tr, cols), lambda i: (i, 0)),
        out_shape=jax.ShapeDtypeStruct((rows, cols), F32),
        compiler_params=_params(("arbitrary",)),
    )(slots)


def _swap_with_sibling(parts):
    nw = len(parts)

    def body(*refs):
        ins, outs = refs[:nw], refs[nw:2 * nw]
        send_sem, recv_sem = refs[2 * nw:]
        x, y, c, _ = _position()
        copies = [pltpu.make_async_remote_copy(src_ref=ins[w], dst_ref=outs[w], send_sem=send_sem.at[w],
                                               recv_sem=recv_sem.at[w], device_id=(x, y, 1 - c), device_id_type=MESH)
                  for w in range(nw)]
        for cp in copies:
            cp.start()
        for cp in copies:
            cp.wait()

    return pl.pallas_call(
        body, name="swap_with_sibling",
        in_specs=_any_specs(nw), out_specs=_any_specs(nw),
        out_shape=[jax.ShapeDtypeStruct(p.shape, p.dtype) for p in parts],
        scratch_shapes=[pltpu.SemaphoreType.DMA((nw,)), pltpu.SemaphoreType.DMA((nw,))],
    )(*parts)


def _sum_small(part):
    def body(p_ref, o_ref, buf, send_sem, recv_sem):
        x, y, c, _ = _position()
        me = 4 * x + 2 * y + c
        buf[me] = p_ref[...]
        copies = []
        for k in range(1, N_DEV):
            peer = tuple(1 - v if (k >> s) & 1 else v for v, s in ((x, 2), (y, 1), (c, 0)))
            copies.append(pltpu.make_async_remote_copy(src_ref=p_ref, dst_ref=buf.at[me], send_sem=send_sem.at[k - 1],
                                                       recv_sem=recv_sem.at[k - 1], device_id=peer, device_id_type=MESH))
        for cp in copies:
            cp.start()
        for cp in copies:
            cp.wait()
        acc = buf[0]
        for d in range(1, N_DEV):
            acc = acc + buf[d]
        o_ref[...] = acc

    vm = pl.BlockSpec(memory_space=pltpu.VMEM)
    return pl.pallas_call(
        body, name="sum_small", in_specs=[vm], out_specs=vm,
        out_shape=jax.ShapeDtypeStruct((1, SM_LEN), F32),
        scratch_shapes=[pltpu.VMEM((N_DEV, 1, SM_LEN), F32), pltpu.SemaphoreType.DMA((N_DEV - 1,)),
                        pltpu.SemaphoreType.DMA((N_DEV - 1,))],
    )(part)


def _adamw(w, g, m, v):
    m = ADAM_B1 * m + (1.0 - ADAM_B1) * g
    v = ADAM_B2 * v + (1.0 - ADAM_B2) * (g * g)
    m_hat = m / (1.0 - ADAM_B1 ** ADAM_STEP)
    v_hat = v / (1.0 - ADAM_B2 ** ADAM_STEP)
    delta = -ADAM_LR * (m_hat / (jnp.sqrt(v_hat) + ADAM_EPS) + ADAM_WD * w)
    return delta, m, v


def _adam_big(name, p_own, p_sibling, w, m, v):
    rows, cols = w.shape
    tr = _update_rows(rows)

    def body(p_ref, q_ref, w_ref, m_ref, v_ref, g_ref, d_ref, nm_ref, nv_ref):
        g = p_ref[...] + q_ref[...]
        g_ref[...] = g
        d_ref[...], nm_ref[...], nv_ref[...] = _adamw(w_ref[...], g, m_ref[...], v_ref[...])

    spec = pl.BlockSpec((tr, cols), lambda i: (i, 0))
    return pl.pallas_call(
        body, name=name, grid=(rows // tr,), in_specs=[spec] * 5, out_specs=[spec] * 4,
        out_shape=[jax.ShapeDtypeStruct((rows, cols), F32)] * 4,
        compiler_params=_params(("arbitrary",)),
    )(p_own, p_sibling, w, m, v)


def _adam_small(total, wvec, mvec, vvec, lb_of):
    def body(t_ref, w_ref, m_ref, v_ref, g_ref, d_ref, nm_ref, nv_ref):
        g_ref[:, :SM_LB] = t_ref[:, :SM_LB]
        _, vjp = jax.vjp(lb_of, w_ref[:, PK_LB0:PK_LB1], w_ref[:, PK_LB1:PK_LEN])
        dl0, dl1 = vjp(t_ref[:, SM_LB:SM_LOSS])
        g_ref[:, PK_LB0:PK_LB1] = dl0
        g_ref[:, PK_LB1:PK_LEN] = dl1
        d_ref[...], nm_ref[...], nv_ref[...] = _adamw(w_ref[...], g_ref[...], m_ref[...], v_ref[...])

    return pl.pallas_call(body, name="adam_small", out_shape=[jax.ShapeDtypeStruct((1, PK_LEN), F32)] * 4)(
        total, wvec, mvec, vvec)


_BIG = ("ffn1_w1", "ffn1_w3", "ffn1_w2", "w_in", "w_proj_attn", "w_proj_hgrn", "w_out", "ffn2_w1", "ffn2_w3", "ffn2_w2")
_SMALL = ("ln1_g", "ln1_b", "ln2_g", "ln2_b", "ln3_g", "ln3_b", "b_in", "attn_sinks", "hgrn_norm_g", "hgrn_lb_logits")
_ORDER = ("ln1_g", "ln1_b", "ffn1_w1", "ffn1_w3", "ffn1_w2", "ln2_g", "ln2_b", "w_in", "b_in", "attn_sinks",
          "hgrn_lb_logits", "hgrn_norm_g", "w_proj_attn", "w_proj_hgrn", "w_out", "ln3_g", "ln3_b",
          "ffn2_w1", "ffn2_w3", "ffn2_w2")


def _pad_to(a, rows, cols):
    return jnp.pad(a, ((0, rows - a.shape[0]), (0, cols - a.shape[1])))


_TRANSPOSED = ("ffn1_w1", "ffn1_w3", "ffn2_w1", "ffn2_w3")


def _local_view(name, arr):
    return arr[0].T if name in _TRANSPOSED else arr[0]


def _ffn_grad(name, hidden, other, comm=None):
    (dw,), comm_out = _grad_matmul(name, hidden, "cols", D_FF // FF_GRAD_PARTS, other, "shared", D_MODEL,
                                   comm=comm, parts=FF_GRAD_PARTS)
    return dw.reshape(N_CHIPS, D_FF // N_CHIPS, D_MODEL), comm_out


def _pack_small(p):
    pieces = [p[k].reshape(1, -1) for k in ("ln1_g", "ln1_b", "ln2_g", "ln2_b", "ln3_g", "ln3_b", "b_in")]
    pieces.append(_pad_to(p["attn_sinks"].reshape(1, -1), 1, LANES))
    pieces.append(p["hgrn_norm_g"].reshape(1, -1))
    pieces.append(p["hgrn_lb_logits"].reshape(1, -1))
    return jnp.concatenate(pieces, axis=1)


def _unpack_small(vec, like):
    out = {}
    for i, k in enumerate(("ln1_g", "ln1_b", "ln2_g", "ln2_b", "ln3_g", "ln3_b")):
        out[k] = vec[:, i * D_MODEL:(i + 1) * D_MODEL]
    out["b_in"] = vec[:, SM_BIN:SM_SINK]
    out["attn_sinks"] = vec[:, SM_SINK:SM_SINK + N_Q_HEADS]
    out["hgrn_norm_g"] = vec[:, SM_NG:SM_LB]
    out["hgrn_lb_logits"] = vec[:, PK_LB0:PK_LEN].reshape(2, D_MODEL)
    return {k: v.reshape(like[k].shape) for k, v in out.items()}


def kernel(x, ln1_g, ln1_b, ffn1_w1, ffn1_w3, ffn1_w2, ln2_g, ln2_b, w_in, b_in, attn_sinks, hgrn_lb_logits, hgrn_norm_g, w_proj_attn, w_proj_hgrn, w_out, ln3_g, ln3_b, ffn2_w1, ffn2_w3, ffn2_w2, loss_target, m_ln1_g, m_ln1_b, m_ffn1_w1, m_ffn1_w3, m_ffn1_w2, m_ln2_g, m_ln2_b, m_w_in, m_b_in, m_attn_sinks, m_hgrn_lb_logits, m_hgrn_norm_g, m_w_proj_attn, m_w_proj_hgrn, m_w_out, m_ln3_g, m_ln3_b, m_ffn2_w1, m_ffn2_w3, m_ffn2_w2, v_ln1_g, v_ln1_b, v_ffn1_w1, v_ffn1_w3, v_ffn1_w2, v_ln2_g, v_ln2_b, v_w_in, v_b_in, v_attn_sinks, v_hgrn_lb_logits, v_hgrn_norm_g, v_w_proj_attn, v_w_proj_hgrn, v_w_out, v_ln3_g, v_ln3_b, v_ffn2_w1, v_ffn2_w3, v_ffn2_w2):
    w = dict(ln1_g=ln1_g, ln1_b=ln1_b, ffn1_w1=ffn1_w1, ffn1_w3=ffn1_w3, ffn1_w2=ffn1_w2, ln2_g=ln2_g, ln2_b=ln2_b,
             w_in=w_in, b_in=b_in, attn_sinks=attn_sinks, hgrn_lb_logits=hgrn_lb_logits, hgrn_norm_g=hgrn_norm_g,
             w_proj_attn=w_proj_attn, w_proj_hgrn=w_proj_hgrn, w_out=w_out, ln3_g=ln3_g, ln3_b=ln3_b,
             ffn2_w1=ffn2_w1, ffn2_w3=ffn2_w3, ffn2_w2=ffn2_w2)
    mom = dict(ln1_g=m_ln1_g, ln1_b=m_ln1_b, ffn1_w1=m_ffn1_w1, ffn1_w3=m_ffn1_w3, ffn1_w2=m_ffn1_w2, ln2_g=m_ln2_g,
               ln2_b=m_ln2_b, w_in=m_w_in, b_in=m_b_in, attn_sinks=m_attn_sinks, hgrn_lb_logits=m_hgrn_lb_logits,
               hgrn_norm_g=m_hgrn_norm_g, w_proj_attn=m_w_proj_attn, w_proj_hgrn=m_w_proj_hgrn, w_out=m_w_out,
               ln3_g=m_ln3_g, ln3_b=m_ln3_b, ffn2_w1=m_ffn2_w1, ffn2_w3=m_ffn2_w3, ffn2_w2=m_ffn2_w2)
    var = dict(ln1_g=v_ln1_g, ln1_b=v_ln1_b, ffn1_w1=v_ffn1_w1, ffn1_w3=v_ffn1_w3, ffn1_w2=v_ffn1_w2, ln2_g=v_ln2_g,
               ln2_b=v_ln2_b, w_in=v_w_in, b_in=v_b_in, attn_sinks=v_attn_sinks, hgrn_lb_logits=v_hgrn_lb_logits,
               hgrn_norm_g=v_hgrn_norm_g, w_proj_attn=v_w_proj_attn, w_proj_hgrn=v_w_proj_hgrn, w_out=v_w_out,
               ln3_g=v_ln3_g, ln3_b=v_ln3_b, ffn2_w1=v_ffn2_w1, ffn2_w3=v_ffn2_w3, ffn2_w2=v_ffn2_w2)

    n_tok = x.shape[1]
    x0 = x.reshape(n_tok, D_MODEL)
    target = loss_target.reshape(n_tok, D_MODEL)

    shard = {k: _bf(_local_view(k, w[k])) for k in _BIG}
    gather = lambda keys: _GatherWeights([shard[k] for k in keys])
    slots = {}
    exchange = lambda keys: _ExchangeGrads([big[k] for k in keys])
    ffn1_keys = ("ffn1_w1", "ffn1_w3", "ffn1_w2")
    mixer_keys = ("w_in", "w_proj_attn", "w_proj_hgrn", "w_out")
    ffn2_keys = ("ffn2_w1", "ffn2_w3", "ffn2_w2")
    whole = lambda ts: [t.reshape(D_FF, D_MODEL) for t in ts]
    f1 = whole(_run_comm("gather_ffn1", gather(ffn1_keys)))

    tabs = _rope_tables(n_tok)
    lb, lb_of = _lb_fwd(hgrn_lb_logits)
    (z1, x1, x0b, h1_1, h3_1, x1b), (w_in_g, w_pa, w_ph, w_o) = _ffn_fwd(
        "ffn1_fwd", x0, *f1, ln1_g, ln1_b, comm=gather(mixer_keys))
    w_pa, w_ph, w_o = (t.reshape(D_MODEL, D_MODEL) for t in (w_pa, w_ph, w_o))
    (proj,), f2 = _in_proj(x1b, w_in_g, b_in, comm=gather(ffn2_keys))
    f2 = whole(f2)
    y_attn = _attn_fwd(proj, tabs, attn_sinks)
    y_hgrn, states = _hgrn_fwd(proj, lb, hgrn_norm_g)
    z2, x2 = _mix_fwd(y_attn, y_hgrn, proj, x1, w_pa, w_ph, w_o, ln2_g, ln2_b)
    (z3, dy, x2b, h1_2, h3_2, loss_part), _ = _ffn_fwd("ffn2_fwd", x2, *f2, ln3_g, ln3_b, target=target)

    big = {}
    small = {}
    (dx2, a2, dh1_2, dh3_2, df2, small["ln3_g"], small["ln3_b"]), _ = _ffn_bwd(
        "ffn2_bwd", h1_2, h3_2, z3, dy, *f2, ln3_g, ln3_b)
    big["ffn2_w1"], _ = _ffn_grad("ffn2_dw1", dh1_2, x2b)
    big["ffn2_w3"], _ = _ffn_grad("ffn2_dw3", dh3_2, x2b)
    big["ffn2_w2"], _ = _ffn_grad("ffn2_dw2", a2, df2)
    (dz2, dz2b, merged, dya, dyh, dy_attn, dy_hgrn, dgates, small["ln2_g"], small["ln2_b"]) = _mix_bwd(
        dx2, z2, y_attn, y_hgrn, proj, w_pa, w_ph, w_o, ln2_g, ln2_b)
    (big["w_out"],), _ = _grad_matmul("dw_out", merged, "cols", PROJ_SHARD, dz2b, "shared", D_MODEL)
    (big["w_proj_attn"],), _ = _grad_matmul("dw_proj_attn", y_attn, "cols", PROJ_SHARD, dya, "shared", D_MODEL)
    (big["w_proj_hgrn"],), _ = _grad_matmul("dw_proj_hgrn", y_hgrn, "cols", PROJ_SHARD, dyh, "shared", D_MODEL)
    dfl, dqh, dih, dog, dlb, small["hgrn_norm_g"] = _hgrn_bwd(proj, lb, hgrn_norm_g, states, dy_hgrn)
    early_keys = ffn2_keys + mixer_keys[1:]
    (dq, dkv, dsinks), early = _attn_bwd(proj, dy_attn, tabs, attn_sinks, comm=exchange(early_keys))
    slots.update(zip(early_keys, early))
    dproj = jnp.concatenate([dq, dkv, dfl, dqh, dih, dog, dgates], axis=1)
    (big["w_in"], small["b_in"]), _ = _grad_matmul(
        "dw_in", x1b, "shared", D_MODEL, dproj, "cols", IN_SHARD, colsum=True)
    (dx1,), (slots["w_in"],) = _in_proj_dx(dproj, w_in_g, dz2, comm=exchange(("w_in",)))
    (grad_x, a1, dh1_1, dh3_1, df1, small["ln1_g"], small["ln1_b"]), _ = _ffn_bwd(
        "ffn1_bwd", h1_1, h3_1, z1, dx1, *f1, ln1_g, ln1_b)
    big["ffn1_w1"], _ = _ffn_grad("ffn1_dw1", dh1_1, x0b)
    big["ffn1_w3"], (slots["ffn1_w1"],) = _ffn_grad("ffn1_dw3", dh3_1, x0b, comm=exchange(("ffn1_w1",)))
    big["ffn1_w2"], (slots["ffn1_w3"],) = _ffn_grad("ffn1_dw2", a1, df1, comm=exchange(("ffn1_w3",)))
    (slots["ffn1_w2"],) = _run_comm("exchange_last", exchange(("ffn1_w2",)))

    local = {k: _local_view(k, w[k]) for k in _BIG}
    partial = [_sum_slots("sum_" + k, slots[k]) for k in _BIG]
    from_sibling = _swap_with_sibling(partial)

    outs = {"grad": {}, "delta": {}, "m": {}, "v": {}}
    for k, p, q in zip(_BIG, partial, from_sibling):
        res = _adam_big("adam_" + k, p, q, local[k], _local_view(k, mom[k]), _local_view(k, var[k]))
        for kind, r in zip(("grad", "delta", "m", "v"), res):
            outs[kind][k] = (r.T if k in _TRANSPOSED else r).reshape(w[k].shape)

    part = jnp.concatenate(
        [small[k] for k in ("ln1_g", "ln1_b", "ln2_g", "ln2_b", "ln3_g", "ln3_b", "b_in")]
        + [dsinks, small["hgrn_norm_g"], dlb, loss_part], axis=1)
    total = _sum_small(part)
    res = _adam_small(total, _pack_small(w), _pack_small(mom), _pack_small(var), lb_of)
    for kind, r in zip(("grad", "delta", "m", "v"), res):
        outs[kind].update(_unpack_small(r, w))
    loss = total[0, SM_LOSS]

    return (loss, grad_x.reshape(x.shape), *[outs["grad"][k] for k in _ORDER], *[outs["delta"][k] for k in _ORDER],
            *[outs["m"][k] for k in _ORDER], *[outs["v"][k] for k in _ORDER])
```

```python
import functools

import jax
import jax.numpy as jnp
from jax import lax
from jax.experimental import pallas as pl
from jax.experimental.pallas import tpu as pltpu

F32 = jnp.float32
BF16 = jnp.bfloat16

D_MODEL = 1024
N_Q_HEADS = 16
N_KV_HEADS = 4
HEAD_DIM = 64
ATTN_BLOCK = 128
ROPE_THETA = 500000.0
ROPE_DIM = HEAD_DIM // 4
HGRN_HEADS = 8
HGRN_DK = 128
HGRN_CHUNK = 64
D_FF = 2816
D_IN = 7680
DEEPNORM_ALPHA = 2 ** 0.25
LN_EPS = 1e-5
RMS_EPS = 1e-6
NEG_INF = -1e30

ADAM_LR = 0.001
ADAM_B1 = 0.9
ADAM_B2 = 0.999
ADAM_EPS = 1e-08
ADAM_WD = 0.01
ADAM_STEP = 10

N_CHIPS = 4
N_DEV = 8
LANES = 128
FF_GRAD_PARTS = 2
FFN_TILE = 256
IN_SHARD = D_IN // N_CHIPS
PROJ_SHARD = D_MODEL // N_CHIPS
ROW_TILE = 512
GRAD_ROWS = 2048
MIX_TILE = 256
UPDATE_ROWS = 128
HGRN_CHUNKS_PER_STEP = 8
VMEM_LIMIT = 56 * 1024 * 1024

COL_K = 1024 // 256
COL_V = 1280 // 256
COL_F = 1536 // LANES
COL_QH = 2560 // LANES
COL_IH = 3584 // LANES
COL_OG = 4608 // LANES
COL_GA = 5632 // 512
COL_GH = 6656 // 512

SM_LN = 0
SM_BIN = 6 * D_MODEL
SM_SINK = SM_BIN + D_IN
SM_NG = SM_SINK + LANES
SM_LB = SM_NG + LANES
SM_LOSS = SM_LB + D_MODEL
SM_LEN = SM_LOSS + LANES

MESH = pl.DeviceIdType.MESH


def _mm(a, b):
    return lax.dot_general(a, b, (((1,), (0,)), ((), ())), preferred_element_type=F32)


def _mm_nt(a, b):
    return lax.dot_general(a, b, (((1,), (1,)), ((), ())), preferred_element_type=F32)


def _mm_tn(a, b):
    return lax.dot_general(a, b, (((0,), (0,)), ((), ())), preferred_element_type=F32)


def _bf(v):
    return v.astype(BF16)


def _sig(v):
    return jax.nn.sigmoid(v)


def _ln(z, g, b):
    mu = jnp.mean(z, axis=-1, keepdims=True)
    zc = z - mu
    var = jnp.mean(zc * zc, axis=-1, keepdims=True)
    return zc * lax.rsqrt(var + LN_EPS) * g + b


def _swiglu_act(h1, h3):
    return (h1 * _sig(h1)) * h3


def _params(sem=None):
    return pltpu.CompilerParams(dimension_semantics=sem, vmem_limit_bytes=VMEM_LIMIT)


def _full(shape):
    nd = len(shape)
    return pl.BlockSpec(shape, lambda *_: (0,) * nd)


def _update_rows(rows):
    return max(t for t in range(8, UPDATE_ROWS + 1, 8) if rows % t == 0)


def _resident(shape):
    nd = len(shape)
    return pl.BlockSpec(shape, lambda *_: (0,) * nd, pipeline_mode=pl.Buffered(1))


def _ffn_fwd(name, x, w1t, w3t, w2, g, b, target=None, comm=None):
    n = x.shape[0]
    tm = min(FFN_TILE, n)
    final = target is not None

    def body(*refs):
        if final:
            x_ref, w1_ref, w3_ref, w2_ref, g_ref, b_ref, t_ref, z_ref, o_ref, xb_ref, h1_ref, h3_ref, loss_ref = refs
        else:
            x_ref, w1_ref, w3_ref, w2_ref, g_ref, b_ref, z_ref, o_ref, xb_ref, h1_ref, h3_ref, ob_ref = refs
        xb = _bf(x_ref[...])
        xb_ref[...] = xb
        h1 = _mm_nt(xb, w1_ref[...])
        h3 = _mm_nt(xb, w3_ref[...])
        h1_ref[...] = _bf(h1)
        h3_ref[...] = _bf(h3)
        z = DEEPNORM_ALPHA * x_ref[...] + 0.5 * _mm(_bf(_swiglu_act(h1, h3)), w2_ref[...])
        z_ref[...] = z
        y = _ln(z, g_ref[...], b_ref[...])
        if final:
            e = y - t_ref[...]

            @pl.when(pl.program_id(0) == 0)
            def _():
                loss_ref[...] = jnp.zeros_like(loss_ref)

            loss_ref[...] += jnp.sum(e * e) * (0.5 / D_MODEL)
            o_ref[...] = e * (1.0 / D_MODEL)
        else:
            o_ref[...] = y
            ob_ref[...] = _bf(y)

    row = pl.BlockSpec((tm, D_MODEL), lambda i: (i, 0))
    hid = pl.BlockSpec((tm, D_FF), lambda i: (i, 0))
    wres = _resident((D_FF, D_MODEL))
    vec = _full((1, D_MODEL))
    in_specs = [row, wres, wres, wres, vec, vec]
    args = [x, w1t, w3t, w2, g, b]
    hid_shape = jax.ShapeDtypeStruct((n, D_FF), BF16)
    out_specs = [row, row, row, hid, hid]
    out_shape = ([jax.ShapeDtypeStruct((n, D_MODEL), F32)] * 2 + [jax.ShapeDtypeStruct((n, D_MODEL), BF16)]
                 + [hid_shape] * 2)
    if final:
        in_specs.append(row)
        args.append(target)
        out_specs.append(_full((1, LANES)))
        out_shape.append(jax.ShapeDtypeStruct((1, LANES), F32))
    else:
        out_specs.append(row)
        out_shape.append(jax.ShapeDtypeStruct((n, D_MODEL), BF16))
    return _hosted(
        body, comm, name=name, grid=(n // tm,), in_specs=in_specs, out_specs=out_specs, out_shape=out_shape,
        scratch_shapes=[], compiler_params=_params(("arbitrary",)), args=args)


def _ffn_bwd(name, h1s, h3s, z, dout, w1t, w3t, w2, g, b, comm=None):
    n = z.shape[0]
    tm = min(FFN_TILE, n)

    def body(h1_ref, h3_ref, z_ref, do_ref, w1_ref, w3_ref, w2_ref, g_ref, b_ref,
             dx_ref, a_ref, dh1_ref, dh3_ref, df_ref, dg_ref, db_ref):
        _, vjp = jax.vjp(_ln, z_ref[...], g_ref[...], b_ref[...])
        dz, dg, db = vjp(do_ref[...])

        @pl.when(pl.program_id(0) == 0)
        def _():
            dg_ref[...] = jnp.zeros_like(dg_ref)
            db_ref[...] = jnp.zeros_like(db_ref)

        dg_ref[...] += dg
        db_ref[...] += db
        df = _bf(0.5 * dz)
        df_ref[...] = df
        a, act_vjp = jax.vjp(_swiglu_act, h1_ref[...].astype(F32), h3_ref[...].astype(F32))
        dh1, dh3 = act_vjp(_mm_nt(df, w2_ref[...]))
        dh1 = _bf(dh1)
        dh3 = _bf(dh3)
        a_ref[...] = _bf(a)
        dh1_ref[...] = dh1
        dh3_ref[...] = dh3
        dx_ref[...] = DEEPNORM_ALPHA * dz + _mm(dh1, w1_ref[...]) + _mm(dh3, w3_ref[...])

    row = pl.BlockSpec((tm, D_MODEL), lambda i: (i, 0))
    hid = pl.BlockSpec((tm, D_FF), lambda i: (i, 0))
    wres = _resident((D_FF, D_MODEL))
    vec = _full((1, D_MODEL))
    hid_shape = jax.ShapeDtypeStruct((n, D_FF), BF16)
    return _hosted(
        body, comm, name=name, grid=(n // tm,),
        in_specs=[hid, hid, row, row, wres, wres, wres, vec, vec],
        out_specs=[row, hid, hid, hid, row, vec, vec],
        out_shape=[jax.ShapeDtypeStruct((n, D_MODEL), F32), hid_shape, hid_shape, hid_shape,
                   jax.ShapeDtypeStruct((n, D_MODEL), BF16),
                   jax.ShapeDtypeStruct((1, D_MODEL), F32), jax.ShapeDtypeStruct((1, D_MODEL), F32)],
        scratch_shapes=[], compiler_params=_params(("arbitrary",)),
        args=[h1s, h3s, z, dout, w1t, w3t, w2, g, b])


def _operand_spec(arr, mode, tn, width, parts):
    if mode == "shared":
        return pl.BlockSpec((tn, width), lambda s, k: (k, 0))
    assert mode == "cols" and arr.shape[1] == parts * width
    return pl.BlockSpec((tn, width), lambda s, k: (k, s))


def _grad_matmul(name, a, a_mode, ka, b, b_mode, kb, colsum=False, comm=None, parts=N_CHIPS):
    n = a.shape[-2]
    tn = min(GRAD_ROWS, n)
    nk = n // tn

    def body(*refs):
        if colsum:
            a_ref, b_ref, o_ref, cs_ref, acc = refs
        else:
            a_ref, b_ref, o_ref, acc = refs
        k = pl.program_id(1)
        av = a_ref[...]
        bv = b_ref[...]

        @pl.when(k == 0)
        def _():
            acc[...] = jnp.zeros_like(acc)
            if colsum:
                cs_ref[...] = jnp.zeros_like(cs_ref)

        acc[...] += _mm_tn(av, bv)
        if colsum:
            cs_ref[...] += jnp.sum(bv.astype(F32), axis=0, keepdims=True)

        @pl.when(k == nk - 1)
        def _():
            o_ref[0] = _bf(acc[...])

    out_specs = [pl.BlockSpec((1, ka, kb), lambda s, k: (s, 0, 0))]
    out_shape = [jax.ShapeDtypeStruct((parts, ka, kb), BF16)]
    if colsum:
        out_specs.append(pl.BlockSpec((1, kb), lambda s, k: (0, s)))
        out_shape.append(jax.ShapeDtypeStruct((1, parts * kb), F32))
    return _hosted(
        body, comm, name=name, grid=(parts, nk),
        in_specs=[_operand_spec(a, a_mode, tn, ka, parts), _operand_spec(b, b_mode, tn, kb, parts)],
        out_specs=out_specs, out_shape=out_shape,
        scratch_shapes=[pltpu.VMEM((ka, kb), F32)],
        compiler_params=_params(("arbitrary", "arbitrary")), args=[a, b])


def _in_proj(x1, w_in_g, b_in, comm=None):
    n = x1.shape[0]
    tm = min(FFN_TILE, n)

    def body(x_ref, w_ref, b_ref, o_ref):
        xv = x_ref[...]
        for j in range(N_CHIPS):
            cols = slice(j * IN_SHARD, (j + 1) * IN_SHARD)
            o_ref[:, cols] = _mm(xv, w_ref[j]) + b_ref[:, cols]

    return _hosted(
        body, comm, name="in_proj", grid=(n // tm,),
        in_specs=[pl.BlockSpec((tm, D_MODEL), lambda i: (i, 0)),
                  _resident((N_CHIPS, D_MODEL, IN_SHARD)), _full((1, D_IN))],
        out_specs=[pl.BlockSpec((tm, D_IN), lambda i: (i, 0))],
        out_shape=[jax.ShapeDtypeStruct((n, D_IN), F32)],
        scratch_shapes=[],
        compiler_params=_params(("arbitrary",)), args=[x1, w_in_g, b_in])


def _in_proj_dx(dproj, w_in_g, dz2, comm=None):
    n = dproj.shape[0]
    tm = min(FFN_TILE, n)

    def body(dp_ref, w_ref, dz_ref, o_ref):
        dx = DEEPNORM_ALPHA * dz_ref[...]
        for j in range(N_CHIPS):
            dx = dx + _mm_nt(dp_ref[:, j * IN_SHARD:(j + 1) * IN_SHARD], w_ref[j])
        o_ref[...] = dx

    return _hosted(
        body, comm, name="in_proj_dx", grid=(n // tm,),
        in_specs=[pl.BlockSpec((tm, D_IN), lambda i: (i, 0)),
                  _resident((N_CHIPS, D_MODEL, IN_SHARD)),
                  pl.BlockSpec((tm, D_MODEL), lambda i: (i, 0))],
        out_specs=[pl.BlockSpec((tm, D_MODEL), lambda i: (i, 0))],
        out_shape=[jax.ShapeDtypeStruct((n, D_MODEL), F32)],
        scratch_shapes=[],
        compiler_params=_params(("arbitrary",)), args=[dproj, w_in_g, dz2])


def _rope_tables(seq_len):
    pos = jnp.arange(seq_len, dtype=F32)
    inv_freq = ROPE_THETA ** (-jnp.arange(0, ROPE_DIM, 2, dtype=F32) / ROPE_DIM)
    ang = pos[:, None] * inv_freq[None, :]
    cos, sin = jnp.cos(ang), jnp.sin(ang)
    half = ROPE_DIM // 2
    rest = HEAD_DIM - ROPE_DIM
    ones = jnp.ones((seq_len, rest), F32)
    zeros = jnp.zeros((seq_len, rest), F32)
    zh = jnp.zeros((seq_len, half), F32)
    c = jnp.concatenate([cos, cos, ones], axis=1)
    sa = jnp.concatenate([-sin, zh, zeros], axis=1)
    sb = jnp.concatenate([zh, sin, zeros], axis=1)
    reps = LANES // HEAD_DIM
    return tuple(jnp.tile(t, (1, reps)) for t in (c, sa, sb))


def _rope(t, c, sa, sb):
    w = t.shape[1]
    reps = w // LANES
    half = ROPE_DIM // 2
    return (t * jnp.tile(c, (1, reps)) + pltpu.roll(t, w - half, 1) * jnp.tile(sa, (1, reps))
            + pltpu.roll(t, half, 1) * jnp.tile(sb, (1, reps)))


def _rope_transposed(g, c, sa, sb):
    w = g.shape[1]
    reps = w // LANES
    half = ROPE_DIM // 2
    return (g * jnp.tile(c, (1, reps)) + pltpu.roll(g * jnp.tile(sa, (1, reps)), half, 1)
            + pltpu.roll(g * jnp.tile(sb, (1, reps)), w - half, 1))


GROUP = N_Q_HEADS // N_KV_HEADS


def _both_halves(t_pair, which):
    lo = lax.broadcasted_iota(jnp.int32, t_pair.shape, 1) < HEAD_DIM
    swapped = pltpu.roll(t_pair, HEAD_DIM, 1)
    return _bf(jnp.where(lo, t_pair, swapped) if which == 0 else jnp.where(lo, swapped, t_pair))


def _stack_heads(ref_or_val, kh):
    lo = lax.broadcasted_iota(jnp.int32, (ATTN_BLOCK, LANES), 1) < HEAD_DIM
    rows = []
    for gp in range(GROUP // 2):
        pair = kh * (GROUP // 2) + gp
        t = ref_or_val[:, pair * LANES:(pair + 1) * LANES]
        rows += [jnp.where(lo, t, jnp.zeros_like(t)), jnp.where(lo, jnp.zeros_like(t), t)]
    return jnp.concatenate(rows, axis=0)


def _unstack_pairs(stacked):
    lo = lax.broadcasted_iota(jnp.int32, (ATTN_BLOCK, LANES), 1) < HEAD_DIM
    b = ATTN_BLOCK
    return [jnp.where(lo, stacked[2 * gp * b:(2 * gp + 1) * b], stacked[(2 * gp + 1) * b:(2 * gp + 2) * b])
            for gp in range(GROUP // 2)]


def _attn_mask_t(n):
    cols = GROUP * ATTN_BLOCK
    kj = lax.broadcasted_iota(jnp.int32, (2 * ATTN_BLOCK, cols), 0)
    qi = lax.broadcasted_iota(jnp.int32, (2 * ATTN_BLOCK, cols), 1) % ATTN_BLOCK
    dist = qi + ATTN_BLOCK - kj
    return (dist >= 0) & (dist < ATTN_BLOCK) & (n * ATTN_BLOCK + kj - ATTN_BLOCK >= 0)


def _sink_row(sink_ref, kh):
    col = lax.broadcasted_iota(jnp.int32, (1, GROUP * ATTN_BLOCK), 1)
    row = jnp.full((1, GROUP * ATTN_BLOCK), sink_ref[0, kh * GROUP + GROUP - 1], F32)
    for i in reversed(range(GROUP - 1)):
        row = jnp.where(col < (i + 1) * ATTN_BLOCK, sink_ref[0, kh * GROUP + i], row)
    return row


def _attn_probs_t(q_masked, k_sel, mask_t, sink):
    s = _mm_nt(k_sel, q_masked) * (HEAD_DIM ** -0.5)
    s = jnp.where(mask_t, s, NEG_INF)
    m = jnp.maximum(jnp.max(s, axis=0, keepdims=True), sink)
    p = jnp.exp(s - m)
    e_sink = jnp.exp(sink - m)
    denom = jnp.sum(p, axis=0, keepdims=True) + e_sink
    return p / denom, e_sink / denom


def _attn_fwd(proj, tabs, sinks):
    n_tok = proj.shape[0]
    nb = n_tok // ATTN_BLOCK

    def body(q_ref, k_ref, v_ref, c_ref, sa_ref, sb_ref, sink_ref, y_ref, kprev, vprev):
        n = pl.program_id(0)

        @pl.when(n == 0)
        def _():
            kprev[...] = jnp.zeros_like(kprev)
            vprev[...] = jnp.zeros_like(vprev)

        c, sa, sb = c_ref[...], sa_ref[...], sb_ref[...]
        qr = _bf(_rope(q_ref[...], c, sa, sb))
        kr = _rope(k_ref[...], c, sa, sb)
        vc = v_ref[...]
        kk = jnp.concatenate([kprev[...], kr], axis=0)
        vv = jnp.concatenate([vprev[...], vc], axis=0)
        kprev[...] = kr
        vprev[...] = vc
        mask = _attn_mask_t(n)
        for kh in range(N_KV_HEADS):
            r, which = divmod(kh, 2)
            kb = _both_halves(kk[:, r * LANES:(r + 1) * LANES], which)
            vb = _both_halves(vv[:, r * LANES:(r + 1) * LANES], which)
            probs, _ = _attn_probs_t(_stack_heads(qr, kh), kb, mask, _sink_row(sink_ref, kh))
            for gp, out in enumerate(_unstack_pairs(_mm_tn(_bf(probs), vb))):
                pair = kh * (GROUP // 2) + gp
                y_ref[:, pair * LANES:(pair + 1) * LANES] = _bf(out)

    blk = lambda width, col: pl.BlockSpec((ATTN_BLOCK, width), lambda n: (n, col))
    tab = pl.BlockSpec((ATTN_BLOCK, LANES), lambda n: (n, 0))
    kvw = N_KV_HEADS * HEAD_DIM
    return pl.pallas_call(
        body, name="attn_fwd", grid=(nb,),
        in_specs=[blk(D_MODEL, 0), blk(kvw, COL_K), blk(kvw, COL_V), tab, tab, tab,
                  pl.BlockSpec(memory_space=pltpu.SMEM)],
        out_specs=pl.BlockSpec((ATTN_BLOCK, D_MODEL), lambda n: (n, 0)),
        out_shape=jax.ShapeDtypeStruct((n_tok, D_MODEL), BF16),
        scratch_shapes=[pltpu.VMEM((ATTN_BLOCK, kvw), F32), pltpu.VMEM((ATTN_BLOCK, kvw), F32)],
        compiler_params=_params(("arbitrary",)),
    )(proj, proj, proj, *tabs, sinks)


def _attn_bwd(proj, dy, tabs, sinks, comm=None):
    n_tok = proj.shape[0]
    nb = n_tok // ATTN_BLOCK
    kvw = N_KV_HEADS * HEAD_DIM

    def body(q_ref, k_ref, v_ref, do_ref, c_ref, sa_ref, sb_ref, cp_ref, sap_ref, sbp_ref, sink_ref,
             dq_ref, dkv_ref, dsink_ref, kprev, vprev, dkc, dvc):
        n = pl.program_id(0)

        @pl.when(n == 0)
        def _():
            for ref in (kprev, vprev, dkc, dvc, dsink_ref):
                ref[...] = jnp.zeros_like(ref)

        prev_tabs = (cp_ref[...], sap_ref[...], sbp_ref[...])

        @pl.when(n < nb)
        def _():
            c, sa, sb = c_ref[...], sa_ref[...], sb_ref[...]
            qr = _bf(_rope(q_ref[...], c, sa, sb))
            kr = _rope(k_ref[...], c, sa, sb)
            vc = v_ref[...]
            kk = jnp.concatenate([kprev[...], kr], axis=0)
            vv = jnp.concatenate([vprev[...], vc], axis=0)
            kprev[...] = kr
            vprev[...] = vc
            mask = _attn_mask_t(n)
            lane = lax.broadcasted_iota(jnp.int32, (1, LANES), 1)
            lo2 = lax.broadcasted_iota(jnp.int32, (2 * ATTN_BLOCK, LANES), 1) < HEAD_DIM
            dsink = jnp.zeros((1, LANES), F32)
            dq_pairs = []
            dk_full = []
            dv_full = []
            for kh in range(N_KV_HEADS):
                r, which = divmod(kh, 2)
                kb = _both_halves(kk[:, r * LANES:(r + 1) * LANES], which)
                vb = _both_halves(vv[:, r * LANES:(r + 1) * LANES], which)
                qs = _stack_heads(qr, kh)
                dos = _bf(_stack_heads(do_ref, kh))
                probs, p_sink = _attn_probs_t(qs, kb, mask, _sink_row(sink_ref, kh))
                dp = _mm_nt(vb, dos)
                delta = jnp.sum(probs * dp, axis=0, keepdims=True)
                ds = _bf(probs * (dp - delta) * (HEAD_DIM ** -0.5))
                sink_terms = p_sink * delta
                for i in range(GROUP):
                    head_sum = jnp.sum(sink_terms[:, i * ATTN_BLOCK:(i + 1) * ATTN_BLOCK])
                    dsink = dsink + jnp.where(lane == kh * GROUP + i, -head_sum, 0.0)
                dq_pairs += _unstack_pairs(_mm_tn(ds, kb))
                dk_acc = _mm(ds, qs)
                dv_acc = _mm(_bf(probs), dos)
                dk_full.append(dk_acc + pltpu.roll(dk_acc, HEAD_DIM, 1))
                dv_full.append(dv_acc + pltpu.roll(dv_acc, HEAD_DIM, 1))
            dk_pairs = [jnp.where(lo2, dk_full[2 * r], dk_full[2 * r + 1]) for r in range(N_KV_HEADS // 2)]
            dv_pairs = [jnp.where(lo2, dv_full[2 * r], dv_full[2 * r + 1]) for r in range(N_KV_HEADS // 2)]
            dsink_ref[...] += dsink
            dq_ref[...] = _bf(_rope_transposed(jnp.concatenate(dq_pairs, axis=1), c, sa, sb))
            dk_all = jnp.concatenate(dk_pairs, axis=1)
            dv_all = jnp.concatenate(dv_pairs, axis=1)
            dkv_ref[:, :kvw] = _bf(_rope_transposed(dkc[...] + dk_all[:ATTN_BLOCK], *prev_tabs))
            dkv_ref[:, kvw:] = _bf(dvc[...] + dv_all[:ATTN_BLOCK])
            dkc[...] = dk_all[ATTN_BLOCK:]
            dvc[...] = dv_all[ATTN_BLOCK:]

        @pl.when(n == nb)
        def _():
            dkv_ref[:, :kvw] = _bf(_rope_transposed(dkc[...], *prev_tabs))
            dkv_ref[:, kvw:] = _bf(dvc[...])

    cur = lambda n: jnp.minimum(n, nb - 1)
    prev = lambda n: jnp.maximum(n - 1, 0)
    blk = lambda width, col: pl.BlockSpec((ATTN_BLOCK, width), lambda n: (cur(n), col))
    tab = pl.BlockSpec((ATTN_BLOCK, LANES), lambda n: (cur(n), 0))
    tabp = pl.BlockSpec((ATTN_BLOCK, LANES), lambda n: (prev(n), 0))
    return _hosted(
        body, comm, name="attn_bwd", grid=(nb + 1,),
        in_specs=[blk(D_MODEL, 0), blk(kvw, COL_K), blk(kvw, COL_V), blk(D_MODEL, 0), tab, tab, tab, tabp, tabp, tabp,
                  pl.BlockSpec(memory_space=pltpu.SMEM)],
        out_specs=[pl.BlockSpec((ATTN_BLOCK, D_MODEL), lambda n: (cur(n), 0)),
                   pl.BlockSpec((ATTN_BLOCK, 2 * kvw), lambda n: (prev(n), 0)),
                   pl.BlockSpec((1, LANES), lambda n: (0, 0))],
        out_shape=[jax.ShapeDtypeStruct((n_tok, D_MODEL), BF16), jax.ShapeDtypeStruct((n_tok, 2 * kvw), BF16),
                   jax.ShapeDtypeStruct((1, LANES), F32)],
        scratch_shapes=[pltpu.VMEM((ATTN_BLOCK, kvw), F32)] * 4,
        compiler_params=_params(("arbitrary",)), args=[proj, proj, proj, dy, *tabs, *tabs, sinks])


def _bmm(a, b):
    return lax.dot_general(a, b, (((2,), (1,)), ((0,), (0,))), preferred_element_type=F32)


def _bmm_nt(a, b):
    return lax.dot_general(a, b, (((2,), (2,)), ((0,), (0,))), preferred_element_type=F32)


def _bmm_tn(a, b):
    return lax.dot_general(a, b, (((1,), (1,)), ((0,), (0,))), preferred_element_type=F32)


def _tril(cb, upper=False):
    shape = (cb, HGRN_CHUNK, HGRN_CHUNK)
    r, c = lax.broadcasted_iota(jnp.int32, shape, 1), lax.broadcasted_iota(jnp.int32, shape, 2)
    return (r <= c) if upper else (r >= c)


def _tri_matmul(x, upper):
    return lax.dot_general(_tril(x.shape[0], upper).astype(F32), x, (((2,), (1,)), ((0,), (0,))),
                           precision=lax.Precision.HIGHEST, preferred_element_type=F32)


@jax.custom_vjp
def _chunk_cumsum(x):
    return _tri_matmul(x, False)


_chunk_cumsum.defvjp(lambda x: (_tri_matmul(x, False), None), lambda _, g: (_tri_matmul(g, True),))


def _hg_elem(fl, qh, lb):
    f = lb + (1.0 - lb) * _sig(fl)
    k = 1.0 - f
    gc = _chunk_cumsum(jnp.log(f))
    last = lax.broadcasted_iota(jnp.int32, gc.shape, 1) == HGRN_CHUNK - 1
    g_last = jnp.sum(jnp.where(last, gc, 0.0), axis=1, keepdims=True)
    q = qh * _sig(qh)
    return q * jnp.exp(gc), k * jnp.exp(-gc), k * jnp.exp(g_last - gc), jnp.exp(g_last)


def _hg_out(q_dec, k_inv, v, st):
    sc = jnp.where(_tril(q_dec.shape[0]), _bmm_nt(_bf(q_dec), _bf(k_inv)), 0.0)
    return _bmm(_bf(sc), _bf(v)) + _bmm_nt(_bf(q_dec), _bf(st)), sc


def _hg_post(o, og, ng):
    on = o * lax.rsqrt(jnp.mean(o * o, axis=-1, keepdims=True) + RMS_EPS) * ng
    return on * (og * _sig(og))


def _hgrn_specs(n_tok, rev):
    nc = n_tok // HGRN_CHUNK
    cb = min(HGRN_CHUNKS_PER_STEP, nc)
    nt = nc // cb
    rows = cb * HGRN_CHUNK
    tt = (lambda t: nt - 1 - t) if rev else (lambda t: t)
    col = lambda base: pl.BlockSpec((rows, LANES), lambda h, t: (tt(t), base + h))
    head_vec = pl.BlockSpec((1, LANES), lambda h, t: (0, h))
    one_vec = pl.BlockSpec((1, LANES), lambda h, t: (0, 0))
    state = pl.BlockSpec((1, cb, HGRN_DK, HGRN_DK), lambda h, t: (h, tt(t), 0, 0))
    return nc, cb, nt, col, head_vec, one_vec, state


def _hgrn_fwd(proj, lb, ng):
    n_tok = proj.shape[0]
    nc, cb, nt, col, head_vec, one_vec, state = _hgrn_specs(n_tok, False)

    def body(fl_ref, qh_ref, ih_ref, og_ref, lb_ref, ng_ref, y_ref, st_ref, s_acc):
        @pl.when(pl.program_id(1) == 0)
        def _():
            s_acc[...] = jnp.zeros_like(s_acc)

        chunks = lambda ref: ref[...].reshape(cb, HGRN_CHUNK, LANES)
        q_dec, k_inv, k_end, decay = _hg_elem(chunks(fl_ref), chunks(qh_ref), lb_ref[...])
        v = chunks(ih_ref)
        upd = _bmm_tn(_bf(v), _bf(k_end))
        st = s_acc[...]
        for ci in range(cb):
            st_ref[0, ci] = st
            st = st * decay[ci] + upd[ci]
        s_acc[...] = st
        o, _ = _hg_out(q_dec, k_inv, v, st_ref[0])
        y_ref[...] = _bf(_hg_post(o, chunks(og_ref), ng_ref[...]).reshape(cb * HGRN_CHUNK, LANES))

    return pl.pallas_call(
        body, name="hgrn_fwd", grid=(HGRN_HEADS, nt),
        in_specs=[col(COL_F), col(COL_QH), col(COL_IH), col(COL_OG), head_vec, one_vec],
        out_specs=[col(0), state],
        out_shape=[jax.ShapeDtypeStruct((n_tok, D_MODEL), BF16),
                   jax.ShapeDtypeStruct((HGRN_HEADS, nc, HGRN_DK, HGRN_DK), F32)],
        scratch_shapes=[pltpu.VMEM((HGRN_DK, HGRN_DK), F32)],
        compiler_params=_params(("arbitrary", "arbitrary")),
    )(proj, proj, proj, proj, lb, ng)


def _hgrn_bwd(proj, lb, ng, states, dy):
    n_tok = proj.shape[0]
    nc, cb, nt, col, head_vec, one_vec, state = _hgrn_specs(n_tok, True)

    def body(fl_ref, qh_ref, ih_ref, og_ref, lb_ref, ng_ref, st_ref, dy_ref,
             dfl_ref, dqh_ref, dih_ref, dog_ref, dlb_ref, dng_ref, g_acc, g_all):
        h = pl.program_id(0)
        t = pl.program_id(1)

        @pl.when(t == 0)
        def _():
            g_acc[...] = jnp.zeros_like(g_acc)
            dlb_ref[...] = jnp.zeros_like(dlb_ref)

        @pl.when((t == 0) & (h == 0))
        def _():
            dng_ref[...] = jnp.zeros_like(dng_ref)

        chunks = lambda ref: ref[...].reshape(cb, HGRN_CHUNK, LANES)
        flat = lambda val: _bf(val.reshape(cb * HGRN_CHUNK, LANES))
        (q_dec, k_inv, k_end, decay), elem_vjp = jax.vjp(_hg_elem, chunks(fl_ref), chunks(qh_ref), lb_ref[...])
        v = chunks(ih_ref)
        st = st_ref[0]
        o, sc = _hg_out(q_dec, k_inv, v, st)
        _, post_vjp = jax.vjp(_hg_post, o, chunks(og_ref), ng_ref[...])
        do, dog, dng = post_vjp(chunks(dy_ref))
        dob, vb, qb = _bf(do), _bf(v), _bf(q_dec)
        dsc = _bf(jnp.where(_tril(cb), _bmm_nt(dob, vb), 0.0))
        p = _bmm_tn(dob, qb)
        g = g_acc[...]
        for ci in reversed(range(cb)):
            g_all[ci] = g
            g = g * decay[ci] + p[ci]
        g_acc[...] = g
        g = g_all[...]
        gb = _bf(g)
        dq_dec = _bmm(dsc, _bf(k_inv)) + _bmm(dob, _bf(st))
        dk_inv = _bmm_tn(dsc, qb)
        dv = _bmm_tn(_bf(sc), dob) + _bmm_nt(_bf(k_end), gb)
        dk_end = _bmm(vb, gb)
        ddecay = jnp.sum(st * g, axis=1, keepdims=True)
        dfl, dqh, dlb = elem_vjp((dq_dec, dk_inv, dk_end, ddecay))
        dfl_ref[...] = flat(dfl)
        dqh_ref[...] = flat(dqh)
        dih_ref[...] = flat(dv)
        dog_ref[...] = flat(dog)
        dlb_ref[...] += dlb
        dng_ref[...] += dng

    out_col = jax.ShapeDtypeStruct((n_tok, D_MODEL), BF16)
    return pl.pallas_call(
        body, name="hgrn_bwd", grid=(HGRN_HEADS, nt),
        in_specs=[col(COL_F), col(COL_QH), col(COL_IH), col(COL_OG), head_vec, one_vec, state, col(0)],
        out_specs=[col(0), col(0), col(0), col(0), head_vec, one_vec],
        out_shape=[out_col, out_col, out_col, out_col,
                   jax.ShapeDtypeStruct((1, D_MODEL), F32), jax.ShapeDtypeStruct((1, LANES), F32)],
        scratch_shapes=[pltpu.VMEM((HGRN_DK, HGRN_DK), F32), pltpu.VMEM((cb, HGRN_DK, HGRN_DK), F32)],
        compiler_params=_params(("arbitrary", "arbitrary")),
    )(proj, proj, proj, proj, lb, ng, states, dy)


def _lb_fwd(lb_logits):
    def lb_of(l0, l1):
        m = jnp.maximum(l0, l1)
        e0, e1 = jnp.exp(l0 - m), jnp.exp(l1 - m)
        return e0 / (e0 + e1)

    def body(l_ref, o_ref):
        o_ref[...] = lb_of(l_ref[0:1, :], l_ref[1:2, :])

    lb = pl.pallas_call(body, name="lb_fwd", out_shape=jax.ShapeDtypeStruct((1, D_MODEL), F32))(lb_logits)
    return lb, lb_of


def _gate_specs(tm):
    return [pl.BlockSpec((tm, 512), lambda i, c=c: (i, c)) for c in (COL_GA, COL_GA + 1, COL_GH, COL_GH + 1)]


def _mix_fwd(y_attn, y_hgrn, proj, x1, w_pa, w_ph, w_out, g, b):
    n = x1.shape[0]
    tm = min(MIX_TILE, n)

    def body(ya_ref, yh_ref, ga0, ga1, gh0, gh1, x_ref, wpa, wph, wo, g_ref, b_ref, z_ref, o_ref):
        ya = _mm(ya_ref[...], wpa[...])
        yh = _mm(yh_ref[...], wph[...])
        ga = jnp.concatenate([ga0[...], ga1[...]], axis=1)
        gh = jnp.concatenate([gh0[...], gh1[...]], axis=1)
        merged = _sig(ga) * ya + _sig(gh) * yh
        z = DEEPNORM_ALPHA * x_ref[...] + _mm(_bf(merged), wo[...])
        z_ref[...] = z
        o_ref[...] = _ln(z, g_ref[...], b_ref[...])

    row = pl.BlockSpec((tm, D_MODEL), lambda i: (i, 0))
    sq = _full((D_MODEL, D_MODEL))
    vec = _full((1, D_MODEL))
    return pl.pallas_call(
        body, name="mix_fwd", grid=(n // tm,),
        in_specs=[row, row, *_gate_specs(tm), row, sq, sq, sq, vec, vec],
        out_specs=[row, row], out_shape=[jax.ShapeDtypeStruct((n, D_MODEL), F32)] * 2,
        compiler_params=_params(("arbitrary",)),
    )(y_attn, y_hgrn, proj, proj, proj, proj, x1, w_pa, w_ph, w_out, g, b)


def _mix_bwd(dx2, z2, y_attn, y_hgrn, proj, w_pa, w_ph, w_out, g, b):
    n = z2.shape[0]
    tm = min(MIX_TILE, n)

    def body(do_ref, z_ref, ya_ref, yh_ref, ga0, ga1, gh0, gh1, wpa, wph, wo, g_ref, b_ref,
             dz_ref, dzb_ref, mg_ref, dya_ref, dyh_ref, dyat_ref, dyhg_ref, dgt_ref, dg_ref, db_ref):
        _, vjp = jax.vjp(_ln, z_ref[...], g_ref[...], b_ref[...])
        dz, dg, db = vjp(do_ref[...])

        @pl.when(pl.program_id(0) == 0)
        def _():
            dg_ref[...] = jnp.zeros_like(dg_ref)
            db_ref[...] = jnp.zeros_like(db_ref)

        dg_ref[...] += dg
        db_ref[...] += db
        dz_ref[...] = dz
        ya = _mm(ya_ref[...], wpa[...])
        yh = _mm(yh_ref[...], wph[...])
        ga = jnp.concatenate([ga0[...], ga1[...]], axis=1)
        gh = jnp.concatenate([gh0[...], gh1[...]], axis=1)

        def merge(ga, gh, ya, yh):
            return _sig(ga) * ya + _sig(gh) * yh

        merged, merge_vjp = jax.vjp(merge, ga, gh, ya, yh)
        mg_ref[...] = _bf(merged)
        dzb = _bf(dz)
        dzb_ref[...] = dzb
        dmerged = _mm_nt(dzb, wo[...])
        dga, dgh, dya, dyh = merge_vjp(dmerged)
        dya = _bf(dya)
        dyh = _bf(dyh)
        dya_ref[...] = dya
        dyh_ref[...] = dyh
        dgt_ref[:, :D_MODEL] = _bf(dga)
        dgt_ref[:, D_MODEL:] = _bf(dgh)
        dyat_ref[...] = _mm_nt(dya, wpa[...])
        dyhg_ref[...] = _mm_nt(dyh, wph[...])

    row = pl.BlockSpec((tm, D_MODEL), lambda i: (i, 0))
    row2 = pl.BlockSpec((tm, 2 * D_MODEL), lambda i: (i, 0))
    sq = _full((D_MODEL, D_MODEL))
    vec = _full((1, D_MODEL))
    f32_row = jax.ShapeDtypeStruct((n, D_MODEL), F32)
    bf_row = jax.ShapeDtypeStruct((n, D_MODEL), BF16)
    vec_shape = jax.ShapeDtypeStruct((1, D_MODEL), F32)
    return pl.pallas_call(
        body, name="mix_bwd", grid=(n // tm,),
        in_specs=[row, row, row, row, *_gate_specs(tm), sq, sq, sq, vec, vec],
        out_specs=[row, row, row, row, row, row, row, row2, vec, vec],
        out_shape=[f32_row, bf_row, bf_row, bf_row, bf_row, f32_row, f32_row,
                   jax.ShapeDtypeStruct((n, 2 * D_MODEL), BF16), vec_shape, vec_shape],
        compiler_params=_params(("arbitrary",)),
    )(dx2, z2, y_attn, y_hgrn, proj, proj, proj, proj, w_pa, w_ph, w_out, g, b)


def _position():
    x, y, c = lax.axis_index("x"), lax.axis_index("y"), lax.axis_index("c")
    chips = [(1 - x, y), (x, 1 - y), (1 - x, 1 - y)]
    return x, y, c, chips


def _any_specs(k):
    return [pl.BlockSpec(memory_space=pl.ANY)] * k


class _GatherWeights:
    def __init__(self, shards):
        nw = len(shards)
        self.inputs = list(shards)
        self.out_shape = [jax.ShapeDtypeStruct((N_CHIPS, *s.shape), s.dtype) for s in shards]
        self.scratch = [pltpu.SemaphoreType.DMA((nw,)), pltpu.SemaphoreType.DMA((nw * 6,)),
                        pltpu.SemaphoreType.DMA((nw * 6,))]

    def _copies(self, ins, outs, sems):
        nw = len(ins)
        local_sem, send_sem, recv_sem = sems
        x, y, c, chips = _position()
        me = 2 * x + y
        sibling = (x, y, 1 - c)
        half_rows = [s.shape[0] // 2 for s in self.inputs]

        def half(w, chip_idx, which):
            return outs[w].at[chip_idx, pl.ds(which * half_rows[w], half_rows[w])]

        def remote(w, k, src, dst, to):
            return pltpu.make_async_remote_copy(src_ref=src, dst_ref=dst, send_sem=send_sem.at[w * 6 + k],
                                                recv_sem=recv_sem.at[w * 6 + k], device_id=to, device_id_type=MESH)

        local = [pltpu.make_async_copy(ins[w], outs[w].at[me], local_sem.at[w]) for w in range(nw)]
        first = [remote(w, j, ins[w].at[pl.ds(c * half_rows[w], half_rows[w])], half(w, me, c), (px, py, c))
                 for w in range(nw) for j, (px, py) in enumerate(chips)]
        landed = [half(w, 2 * px + py, c) for w in range(nw) for (px, py) in chips]
        arrive = [remote(w, j, landed[w * 3 + j], landed[w * 3 + j], (px, py, c))
                  for w in range(nw) for j, (px, py) in enumerate(chips)]
        passed = [remote(w, 3 + j, landed[w * 3 + j], landed[w * 3 + j], sibling) for w in range(nw) for j in range(3)]
        from_sibling = [remote(w, 3 + j, half(w, 2 * px + py, 1 - c), half(w, 2 * px + py, 1 - c), sibling)
                        for w in range(nw) for j, (px, py) in enumerate(chips)]
        return local, first, arrive, passed, from_sibling

    def start(self, ins, outs, sems):
        local, first, _, _, _ = self._copies(ins, outs, sems)
        for cp in local + first:
            cp.start()

    def finish(self, ins, outs, sems):
        local, first, arrive, passed, from_sibling = self._copies(ins, outs, sems)
        for cp_in, cp_on in zip(arrive, passed):
            cp_in.wait_recv()
            cp_on.start()
        for cp in from_sibling:
            cp.wait_recv()
        for cp in first + passed:
            cp.wait_send()
        for cp in local:
            cp.wait()


class _ExchangeGrads:
    def __init__(self, grads):
        nw = len(grads)
        self.inputs = list(grads)
        self.out_shape = [jax.ShapeDtypeStruct(g.shape, g.dtype) for g in grads]
        self.scratch = [pltpu.SemaphoreType.DMA((nw,)), pltpu.SemaphoreType.DMA((nw * 3,)),
                        pltpu.SemaphoreType.DMA((nw * 3,))]

    def _copies(self, ins, outs, sems):
        nw = len(ins)
        local_sem, send_sem, recv_sem = sems
        x, y, c, chips = _position()
        me = 2 * x + y

        def remote(w, j, src, dst, chip):
            return pltpu.make_async_remote_copy(src_ref=src, dst_ref=dst, send_sem=send_sem.at[w * 3 + j],
                                                recv_sem=recv_sem.at[w * 3 + j], device_id=(*chip, c),
                                                device_id_type=MESH)

        local = [pltpu.make_async_copy(ins[w].at[me], outs[w].at[me], local_sem.at[w]) for w in range(nw)]
        sends = [remote(w, j, ins[w].at[2 * px + py], outs[w].at[me], (px, py))
                 for w in range(nw) for j, (px, py) in enumerate(chips)]
        arrive = [remote(w, j, outs[w].at[2 * px + py], outs[w].at[2 * px + py], (px, py))
                  for w in range(nw) for j, (px, py) in enumerate(chips)]
        return local, sends, arrive

    def start(self, ins, outs, sems):
        local, sends, _ = self._copies(ins, outs, sems)
        for cp in local + sends:
            cp.start()

    def finish(self, ins, outs, sems):
        local, sends, arrive = self._copies(ins, outs, sems)
        for cp in arrive:
            cp.wait_recv()
        for cp in sends:
            cp.wait_send()
        for cp in local:
            cp.wait()


def _hosted(body, comm, *, name, grid, in_specs, out_specs, out_shape, scratch_shapes, compiler_params, args):
    if comm is None:
        res = pl.pallas_call(body, name=name, grid=grid, in_specs=in_specs, out_specs=out_specs, out_shape=out_shape,
                             scratch_shapes=scratch_shapes, compiler_params=compiler_params)(*args)
        return list(res), []
    n_in, n_out, n_scr = len(in_specs), len(out_specs), len(scratch_shapes)
    c_in, c_out = len(comm.inputs), len(comm.out_shape)

    def hosted_body(*refs):
        refs = list(refs)
        cut = lambda k: (refs[:k], refs[k:])
        main_in, refs = cut(n_in)
        comm_in, refs = cut(c_in)
        main_out, refs = cut(n_out)
        comm_out, refs = cut(c_out)
        main_scr, comm_scr = cut(n_scr)
        ids = [pl.program_id(a) for a in range(len(grid))]
        first = functools.reduce(jnp.logical_and, [i == 0 for i in ids])
        last = functools.reduce(jnp.logical_and, [i == g - 1 for i, g in zip(ids, grid)])

        @pl.when(first)
        def _():
            comm.start(comm_in, comm_out, comm_scr)

        body(*main_in, *main_out, *main_scr)

        @pl.when(last)
        def _():
            comm.finish(comm_in, comm_out, comm_scr)

    res = pl.pallas_call(
        hosted_body, name=name, grid=grid, in_specs=[*in_specs, *_any_specs(c_in)],
        out_specs=[*out_specs, *_any_specs(c_out)], out_shape=[*out_shape, *comm.out_shape],
        scratch_shapes=[*scratch_shapes, *comm.scratch], compiler_params=compiler_params,
    )(*args, *comm.inputs)
    return list(res[:n_out]), list(res[n_out:])


def _run_comm(name, comm):
    def body(*refs):
        refs = list(refs)
        c_in, c_out = len(comm.inputs), len(comm.out_shape)
        ins, outs, sems = refs[:c_in], refs[c_in:c_in + c_out], refs[c_in + c_out:]
        comm.start(ins, outs, sems)
        comm.finish(ins, outs, sems)

    return list(pl.pallas_call(
        body, name=name, in_specs=_any_specs(len(comm.inputs)), out_specs=_any_specs(len(comm.out_shape)),
        out_shape=comm.out_shape, scratch_shapes=comm.scratch,
    )(*comm.inputs))


def _sum_slots(name, slots):
    _, rows, cols = slots.shape
    tr = _update_rows(rows)

    def body(s_ref, o_ref):
        acc = s_ref[0].astype(F32)
        for i in range(1, N_CHIPS):
            acc = acc + s_ref[i].astype(F32)
        o_ref[...] = acc

    return pl.pallas_call(
        body, name=name, grid=(rows // tr,),
        in_specs=[pl.BlockSpec((N_CHIPS, tr, cols), lambda i: (0, i, 0))],
        out_specs=pl.BlockSpec((tr, cols), lambda i: (i, 0)),
        out_shape=jax.ShapeDtypeStruct((rows, cols), F32),
        compiler_params=_params(("arbitrary",)),
    )(slots)


def _swap_with_sibling(parts):
    nw = len(parts)

    def body(*refs):
        ins, outs = refs[:nw], refs[nw:2 * nw]
        send_sem, recv_sem = refs[2 * nw:]
        x, y, c, _ = _position()
        copies = [pltpu.make_async_remote_copy(src_ref=ins[w], dst_ref=outs[w], send_sem=send_sem.at[w],
                                               recv_sem=recv_sem.at[w], device_id=(x, y, 1 - c), device_id_type=MESH)
                  for w in range(nw)]
        for cp in copies:
            cp.start()
        for cp in copies:
            cp.wait()

    return pl.pallas_call(
        body, name="swap_with_sibling",
        in_specs=_any_specs(nw), out_specs=_any_specs(nw),
        out_shape=[jax.ShapeDtypeStruct(p.shape, p.dtype) for p in parts],
        scratch_shapes=[pltpu.SemaphoreType.DMA((nw,)), pltpu.SemaphoreType.DMA((nw,))],
    )(*parts)


def _sum_small(part):
    def body(p_ref, o_ref, buf, send_sem, recv_sem):
        x, y, c, _ = _position()
        me = 4 * x + 2 * y + c
        buf[me] = p_ref[...]
        copies = []
        for k in range(1, N_DEV):
            peer = tuple(1 - v if (k >> s) & 1 else v for v, s in ((x, 2), (y, 1), (c, 0)))
            copies.append(pltpu.make_async_remote_copy(src_ref=p_ref, dst_ref=buf.at[me], send_sem=send_sem.at[k - 1],
                                                       recv_sem=recv_sem.at[k - 1], device_id=peer, device_id_type=MESH))
        for cp in copies:
            cp.start()
        for cp in copies:
            cp.wait()
        acc = buf[0]
        for d in range(1, N_DEV):
            acc = acc + buf[d]
        o_ref[...] = acc

    vm = pl.BlockSpec(memory_space=pltpu.VMEM)
    return pl.pallas_call(
        body, name="sum_small", in_specs=[vm], out_specs=vm,
        out_shape=jax.ShapeDtypeStruct((1, SM_LEN), F32),
        scratch_shapes=[pltpu.VMEM((N_DEV, 1, SM_LEN), F32), pltpu.SemaphoreType.DMA((N_DEV - 1,)),
                        pltpu.SemaphoreType.DMA((N_DEV - 1,))],
    )(part)


def _adamw(w, g, m, v):
    m = ADAM_B1 * m + (1.0 - ADAM_B1) * g
    v = ADAM_B2 * v + (1.0 - ADAM_B2) * (g * g)
    m_hat = m / (1.0 - ADAM_B1 ** ADAM_STEP)
    v_hat = v / (1.0 - ADAM_B2 ** ADAM_STEP)
    delta = -ADAM_LR * (m_hat / (jnp.sqrt(v_hat) + ADAM_EPS) + ADAM_WD * w)
    return delta, m, v


def _adam_big(name, p_own, p_sibling, w, m, v):
    rows, cols = w.shape
    tr = _update_rows(rows)

    def body(p_ref, q_ref, w_ref, m_ref, v_ref, g_ref, d_ref, nm_ref, nv_ref):
        g = p_ref[...] + q_ref[...]
        g_ref[...] = g
        d_ref[...], nm_ref[...], nv_ref[...] = _adamw(w_ref[...], g, m_ref[...], v_ref[...])

    spec = pl.BlockSpec((tr, cols), lambda i: (i, 0))
    return pl.pallas_call(
        body, name=name, grid=(rows // tr,), in_specs=[spec] * 5, out_specs=[spec] * 4,
        out_shape=[jax.ShapeDtypeStruct((rows, cols), F32)] * 4,
        compiler_params=_params(("arbitrary",)),
    )(p_own, p_sibling, w, m, v)


_SMALL_AT = {"ln1_g": 0, "ln1_b": D_MODEL, "ln2_g": 2 * D_MODEL, "ln2_b": 3 * D_MODEL, "ln3_g": 4 * D_MODEL,
             "ln3_b": 5 * D_MODEL, "b_in": SM_BIN, "attn_sinks": SM_SINK, "hgrn_norm_g": SM_NG}


def _adam_small(total, w, m, v, lb_of):
    names = list(_SMALL)
    k = len(names)

    def body(*refs):
        t_ref = refs[0]
        w_refs, m_refs, v_refs = (refs[1 + i * k:1 + (i + 1) * k] for i in range(3))
        g_refs, d_refs, nm_refs, nv_refs = (refs[1 + (3 + i) * k:1 + (4 + i) * k] for i in range(4))
        for i, name in enumerate(names):
            if name == "hgrn_lb_logits":
                _, vjp = jax.vjp(lb_of, w_refs[i][0:1, :], w_refs[i][1:2, :])
                g_refs[i][0:1, :], g_refs[i][1:2, :] = vjp(t_ref[:, SM_LB:SM_LOSS])
            else:
                at = _SMALL_AT[name]
                g_refs[i][...] = t_ref[:, at:at + w_refs[i].shape[1]]
            d_refs[i][...], nm_refs[i][...], nv_refs[i][...] = _adamw(
                w_refs[i][...], g_refs[i][...], m_refs[i][...], v_refs[i][...])

    shapes = [jax.ShapeDtypeStruct(w[name].shape, F32) for name in names]
    res = pl.pallas_call(body, name="adam_small", out_shape=shapes * 4)(
        total, *[w[n] for n in names], *[m[n] for n in names], *[v[n] for n in names])
    return [dict(zip(names, res[i * k:(i + 1) * k])) for i in range(4)]


_BIG = ("ffn1_w1", "ffn1_w3", "ffn1_w2", "w_in", "w_proj_attn", "w_proj_hgrn", "w_out", "ffn2_w1", "ffn2_w3", "ffn2_w2")
_SMALL = ("ln1_g", "ln1_b", "ln2_g", "ln2_b", "ln3_g", "ln3_b", "b_in", "attn_sinks", "hgrn_norm_g", "hgrn_lb_logits")
_ORDER = ("ln1_g", "ln1_b", "ffn1_w1", "ffn1_w3", "ffn1_w2", "ln2_g", "ln2_b", "w_in", "b_in", "attn_sinks",
          "hgrn_lb_logits", "hgrn_norm_g", "w_proj_attn", "w_proj_hgrn", "w_out", "ln3_g", "ln3_b",
          "ffn2_w1", "ffn2_w3", "ffn2_w2")


_TRANSPOSED = ("ffn1_w1", "ffn1_w3", "ffn2_w1", "ffn2_w3")


def _local_view(name, arr):
    return arr[0].T if name in _TRANSPOSED else arr[0]


def _ffn_grad(name, hidden, other, comm=None):
    (dw,), comm_out = _grad_matmul(name, hidden, "cols", D_FF // FF_GRAD_PARTS, other, "shared", D_MODEL,
                                   comm=comm, parts=FF_GRAD_PARTS)
    return dw.reshape(N_CHIPS, D_FF // N_CHIPS, D_MODEL), comm_out


def kernel(x, ln1_g, ln1_b, ffn1_w1, ffn1_w3, ffn1_w2, ln2_g, ln2_b, w_in, b_in, attn_sinks, hgrn_lb_logits, hgrn_norm_g, w_proj_attn, w_proj_hgrn, w_out, ln3_g, ln3_b, ffn2_w1, ffn2_w3, ffn2_w2, loss_target, m_ln1_g, m_ln1_b, m_ffn1_w1, m_ffn1_w3, m_ffn1_w2, m_ln2_g, m_ln2_b, m_w_in, m_b_in, m_attn_sinks, m_hgrn_lb_logits, m_hgrn_norm_g, m_w_proj_attn, m_w_proj_hgrn, m_w_out, m_ln3_g, m_ln3_b, m_ffn2_w1, m_ffn2_w3, m_ffn2_w2, v_ln1_g, v_ln1_b, v_ffn1_w1, v_ffn1_w3, v_ffn1_w2, v_ln2_g, v_ln2_b, v_w_in, v_b_in, v_attn_sinks, v_hgrn_lb_logits, v_hgrn_norm_g, v_w_proj_attn, v_w_proj_hgrn, v_w_out, v_ln3_g, v_ln3_b, v_ffn2_w1, v_ffn2_w3, v_ffn2_w2):
    w = dict(ln1_g=ln1_g, ln1_b=ln1_b, ffn1_w1=ffn1_w1, ffn1_w3=ffn1_w3, ffn1_w2=ffn1_w2, ln2_g=ln2_g, ln2_b=ln2_b,
             w_in=w_in, b_in=b_in, attn_sinks=attn_sinks, hgrn_lb_logits=hgrn_lb_logits, hgrn_norm_g=hgrn_norm_g,
             w_proj_attn=w_proj_attn, w_proj_hgrn=w_proj_hgrn, w_out=w_out, ln3_g=ln3_g, ln3_b=ln3_b,
             ffn2_w1=ffn2_w1, ffn2_w3=ffn2_w3, ffn2_w2=ffn2_w2)
    mom = dict(ln1_g=m_ln1_g, ln1_b=m_ln1_b, ffn1_w1=m_ffn1_w1, ffn1_w3=m_ffn1_w3, ffn1_w2=m_ffn1_w2, ln2_g=m_ln2_g,
               ln2_b=m_ln2_b, w_in=m_w_in, b_in=m_b_in, attn_sinks=m_attn_sinks, hgrn_lb_logits=m_hgrn_lb_logits,
               hgrn_norm_g=m_hgrn_norm_g, w_proj_attn=m_w_proj_attn, w_proj_hgrn=m_w_proj_hgrn, w_out=m_w_out,
               ln3_g=m_ln3_g, ln3_b=m_ln3_b, ffn2_w1=m_ffn2_w1, ffn2_w3=m_ffn2_w3, ffn2_w2=m_ffn2_w2)
    var = dict(ln1_g=v_ln1_g, ln1_b=v_ln1_b, ffn1_w1=v_ffn1_w1, ffn1_w3=v_ffn1_w3, ffn1_w2=v_ffn1_w2, ln2_g=v_ln2_g,
               ln2_b=v_ln2_b, w_in=v_w_in, b_in=v_b_in, attn_sinks=v_attn_sinks, hgrn_lb_logits=v_hgrn_lb_logits,
               hgrn_norm_g=v_hgrn_norm_g, w_proj_attn=v_w_proj_attn, w_proj_hgrn=v_w_proj_hgrn, w_out=v_w_out,
               ln3_g=v_ln3_g, ln3_b=v_ln3_b, ffn2_w1=v_ffn2_w1, ffn2_w3=v_ffn2_w3, ffn2_w2=v_ffn2_w2)

    n_tok = x.shape[1]
    x0 = x.reshape(n_tok, D_MODEL)
    target = loss_target.reshape(n_tok, D_MODEL)

    shard = {k: _bf(_local_view(k, w[k])) for k in _BIG}
    gather = lambda keys: _GatherWeights([shard[k] for k in keys])
    slots = {}
    exchange = lambda keys: _ExchangeGrads([big[k] for k in keys])
    ffn1_keys = ("ffn1_w1", "ffn1_w3", "ffn1_w2")
    mixer_keys = ("w_in", "w_proj_attn", "w_proj_hgrn", "w_out")
    ffn2_keys = ("ffn2_w1", "ffn2_w3", "ffn2_w2")
    whole = lambda ts: [t.reshape(D_FF, D_MODEL) for t in ts]
    f1 = whole(_run_comm("gather_ffn1", gather(ffn1_keys)))

    tabs = _rope_tables(n_tok)
    lb, lb_of = _lb_fwd(hgrn_lb_logits)
    (z1, x1, x0b, h1_1, h3_1, x1b), (w_in_g, w_pa, w_ph, w_o) = _ffn_fwd(
        "ffn1_fwd", x0, *f1, ln1_g, ln1_b, comm=gather(mixer_keys))
    w_pa, w_ph, w_o = (t.reshape(D_MODEL, D_MODEL) for t in (w_pa, w_ph, w_o))
    (proj,), f2 = _in_proj(x1b, w_in_g, b_in, comm=gather(ffn2_keys))
    f2 = whole(f2)
    y_attn = _attn_fwd(proj, tabs, attn_sinks)
    y_hgrn, states = _hgrn_fwd(proj, lb, hgrn_norm_g)
    z2, x2 = _mix_fwd(y_attn, y_hgrn, proj, x1, w_pa, w_ph, w_o, ln2_g, ln2_b)
    (z3, dy, x2b, h1_2, h3_2, loss_part), _ = _ffn_fwd("ffn2_fwd", x2, *f2, ln3_g, ln3_b, target=target)

    big = {}
    small = {}
    (dx2, a2, dh1_2, dh3_2, df2, small["ln3_g"], small["ln3_b"]), _ = _ffn_bwd(
        "ffn2_bwd", h1_2, h3_2, z3, dy, *f2, ln3_g, ln3_b)
    big["ffn2_w1"], _ = _ffn_grad("ffn2_dw1", dh1_2, x2b)
    big["ffn2_w3"], _ = _ffn_grad("ffn2_dw3", dh3_2, x2b)
    big["ffn2_w2"], _ = _ffn_grad("ffn2_dw2", a2, df2)
    (dz2, dz2b, merged, dya, dyh, dy_attn, dy_hgrn, dgates, small["ln2_g"], small["ln2_b"]) = _mix_bwd(
        dx2, z2, y_attn, y_hgrn, proj, w_pa, w_ph, w_o, ln2_g, ln2_b)
    (big["w_out"],), _ = _grad_matmul("dw_out", merged, "cols", PROJ_SHARD, dz2b, "shared", D_MODEL)
    (big["w_proj_attn"],), _ = _grad_matmul("dw_proj_attn", y_attn, "cols", PROJ_SHARD, dya, "shared", D_MODEL)
    (big["w_proj_hgrn"],), _ = _grad_matmul("dw_proj_hgrn", y_hgrn, "cols", PROJ_SHARD, dyh, "shared", D_MODEL)
    dfl, dqh, dih, dog, dlb, small["hgrn_norm_g"] = _hgrn_bwd(proj, lb, hgrn_norm_g, states, dy_hgrn)
    early_keys = ffn2_keys + mixer_keys[1:]
    (dq, dkv, dsinks), early = _attn_bwd(proj, dy_attn, tabs, attn_sinks, comm=exchange(early_keys))
    slots.update(zip(early_keys, early))
    dproj = jnp.concatenate([dq, dkv, dfl, dqh, dih, dog, dgates], axis=1)
    (big["w_in"], small["b_in"]), _ = _grad_matmul(
        "dw_in", x1b, "shared", D_MODEL, dproj, "cols", IN_SHARD, colsum=True)
    (dx1,), (slots["w_in"],) = _in_proj_dx(dproj, w_in_g, dz2, comm=exchange(("w_in",)))
    (grad_x, a1, dh1_1, dh3_1, df1, small["ln1_g"], small["ln1_b"]), _ = _ffn_bwd(
        "ffn1_bwd", h1_1, h3_1, z1, dx1, *f1, ln1_g, ln1_b)
    big["ffn1_w1"], _ = _ffn_grad("ffn1_dw1", dh1_1, x0b)
    big["ffn1_w3"], (slots["ffn1_w1"],) = _ffn_grad("ffn1_dw3", dh3_1, x0b, comm=exchange(("ffn1_w1",)))
    big["ffn1_w2"], (slots["ffn1_w3"],) = _ffn_grad("ffn1_dw2", a1, df1, comm=exchange(("ffn1_w3",)))
    (slots["ffn1_w2"],) = _run_comm("exchange_last", exchange(("ffn1_w2",)))

    local = {k: _local_view(k, w[k]) for k in _BIG}
    partial = [_sum_slots("sum_" + k, slots[k]) for k in _BIG]
    from_sibling = _swap_with_sibling(partial)

    outs = {"grad": {}, "delta": {}, "m": {}, "v": {}}
    for k, p, q in zip(_BIG, partial, from_sibling):
        res = _adam_big("adam_" + k, p, q, local[k], _local_view(k, mom[k]), _local_view(k, var[k]))
        for kind, r in zip(("grad", "delta", "m", "v"), res):
            outs[kind][k] = (r.T if k in _TRANSPOSED else r).reshape(w[k].shape)

    total = _sum_small(jnp.concatenate(
        [small[k] for k in ("ln1_g", "ln1_b", "ln2_g", "ln2_b", "ln3_g", "ln3_b", "b_in")]
        + [dsinks, small["hgrn_norm_g"], dlb, loss_part], axis=1))
    for kind, r in zip(("grad", "delta", "m", "v"), _adam_small(total, w, mom, var, lb_of)):
        outs[kind].update(r)
    loss = total[0, SM_LOSS]

    return (loss, grad_x.reshape(x.shape), *[outs["grad"][k] for k in _ORDER], *[outs["delta"][k] for k in _ORDER],
            *[outs["m"][k] for k in _ORDER], *[outs["v"][k] for k in _ORDER])
```

```python
import functools

import jax
import jax.numpy as jnp
from jax import lax
from jax.experimental import pallas as pl
from jax.experimental.pallas import tpu as pltpu

F32 = jnp.float32
BF16 = jnp.bfloat16

D_MODEL = 1024
N_Q_HEADS = 16
N_KV_HEADS = 4
HEAD_DIM = 64
ATTN_BLOCK = 128
ROPE_THETA = 500000.0
ROPE_DIM = HEAD_DIM // 4
HGRN_HEADS = 8
HGRN_DK = 128
HGRN_CHUNK = 64
D_FF = 2816
D_IN = 7680
DEEPNORM_ALPHA = 2 ** 0.25
LN_EPS = 1e-5
RMS_EPS = 1e-6
NEG_INF = -1e30

ADAM_LR = 0.001
ADAM_B1 = 0.9
ADAM_B2 = 0.999
ADAM_EPS = 1e-08
ADAM_WD = 0.01
ADAM_STEP = 10

N_CHIPS = 4
N_DEV = 8
LANES = 128
FF_GRAD_PARTS = 2
FFN_TILE = 256
IN_SHARD = D_IN // N_CHIPS
PROJ_SHARD = D_MODEL // N_CHIPS
ROW_TILE = 512
GRAD_ROWS = 2048
MIX_TILE = 256
UPDATE_ROWS = 128
HGRN_CHUNKS_PER_STEP = 16
VMEM_LIMIT = 56 * 1024 * 1024

GATES_WIDTH = 2 * D_MODEL
HGRN_HEAD_WIDTH = 4 * HGRN_DK
ATTN_WIDTH = D_MODEL + 2 * N_KV_HEADS * HEAD_DIM
COL_HGRN = GATES_WIDTH // HGRN_HEAD_WIDTH
COL_ATTN = (GATES_WIDTH + HGRN_HEADS * HGRN_HEAD_WIDTH) // ATTN_WIDTH
COL_Q = (GATES_WIDTH + HGRN_HEADS * HGRN_HEAD_WIDTH) // D_MODEL
COL_K = (GATES_WIDTH + HGRN_HEADS * HGRN_HEAD_WIDTH + D_MODEL) // (N_KV_HEADS * HEAD_DIM)
COL_V = COL_K + 1


def _to_kernel_cols(a):
    lead = a.shape[:-1]
    qkv, hg, gates = a[..., :ATTN_WIDTH], a[..., ATTN_WIDTH:D_IN - GATES_WIDTH], a[..., D_IN - GATES_WIDTH:]
    hg = jnp.swapaxes(hg.reshape(*lead, 4, HGRN_HEADS, HGRN_DK), -3, -2).reshape(*lead, -1)
    return jnp.concatenate([gates, hg, qkv], axis=-1)


def _from_kernel_cols(a):
    lead = a.shape[:-1]
    gates, hg, qkv = a[..., :GATES_WIDTH], a[..., GATES_WIDTH:D_IN - ATTN_WIDTH], a[..., D_IN - ATTN_WIDTH:]
    hg = jnp.swapaxes(hg.reshape(*lead, HGRN_HEADS, 4, HGRN_DK), -3, -2).reshape(*lead, -1)
    return jnp.concatenate([qkv, hg, gates], axis=-1)

SM_LN = 0
SM_BIN = 6 * D_MODEL
SM_SINK = SM_BIN + D_IN
SM_NG = SM_SINK + LANES
SM_LB = SM_NG + LANES
SM_LOSS = SM_LB + D_MODEL
SM_LEN = SM_LOSS + LANES

MESH = pl.DeviceIdType.MESH


def _mm(a, b):
    return lax.dot_general(a, b, (((1,), (0,)), ((), ())), preferred_element_type=F32)


def _mm_nt(a, b):
    return lax.dot_general(a, b, (((1,), (1,)), ((), ())), preferred_element_type=F32)


def _mm_tn(a, b):
    return lax.dot_general(a, b, (((0,), (0,)), ((), ())), preferred_element_type=F32)


def _bf(v):
    return v.astype(BF16)


def _sig(v):
    return jax.nn.sigmoid(v)


def _ln(z, g, b):
    mu = jnp.mean(z, axis=-1, keepdims=True)
    zc = z - mu
    var = jnp.mean(zc * zc, axis=-1, keepdims=True)
    return zc * lax.rsqrt(var + LN_EPS) * g + b


def _swiglu_act(h1, h3):
    return (h1 * _sig(h1)) * h3


def _params(sem=None):
    return pltpu.CompilerParams(dimension_semantics=sem, vmem_limit_bytes=VMEM_LIMIT)


def _full(shape):
    nd = len(shape)
    return pl.BlockSpec(shape, lambda *_: (0,) * nd)


def _update_rows(rows):
    return max(t for t in range(8, UPDATE_ROWS + 1, 8) if rows % t == 0)


def _resident(shape):
    nd = len(shape)
    return pl.BlockSpec(shape, lambda *_: (0,) * nd, pipeline_mode=pl.Buffered(1))


def _ffn_fwd(name, x, w1t, w3t, w2, g, b, target=None, comm=None):
    n = x.shape[0]
    tm = min(FFN_TILE, n)
    final = target is not None

    def body(*refs):
        if final:
            x_ref, w1_ref, w3_ref, w2_ref, g_ref, b_ref, t_ref, z_ref, o_ref, xb_ref, h1_ref, h3_ref, loss_ref = refs
        else:
            x_ref, w1_ref, w3_ref, w2_ref, g_ref, b_ref, z_ref, o_ref, xb_ref, h1_ref, h3_ref, ob_ref = refs
        xb = _bf(x_ref[...])
        xb_ref[...] = xb
        h1 = _mm_nt(xb, w1_ref[...])
        h3 = _mm_nt(xb, w3_ref[...])
        h1_ref[...] = _bf(h1)
        h3_ref[...] = _bf(h3)
        z = DEEPNORM_ALPHA * x_ref[...] + 0.5 * _mm(_bf(_swiglu_act(h1, h3)), w2_ref[...])
        z_ref[...] = z
        y = _ln(z, g_ref[...], b_ref[...])
        if final:
            e = y - t_ref[...]

            @pl.when(pl.program_id(0) == 0)
            def _():
                loss_ref[...] = jnp.zeros_like(loss_ref)

            loss_ref[...] += jnp.sum(e * e) * (0.5 / D_MODEL)
            o_ref[...] = e * (1.0 / D_MODEL)
        else:
            o_ref[...] = y
            ob_ref[...] = _bf(y)

    row = pl.BlockSpec((tm, D_MODEL), lambda i: (i, 0))
    hid = pl.BlockSpec((tm, D_FF), lambda i: (i, 0))
    wres = _resident((D_FF, D_MODEL))
    vec = _full((1, D_MODEL))
    in_specs = [row, wres, wres, wres, vec, vec]
    args = [x, w1t, w3t, w2, g, b]
    hid_shape = jax.ShapeDtypeStruct((n, D_FF), BF16)
    out_specs = [row, row, row, hid, hid]
    out_shape = ([jax.ShapeDtypeStruct((n, D_MODEL), F32)] * 2 + [jax.ShapeDtypeStruct((n, D_MODEL), BF16)]
                 + [hid_shape] * 2)
    if final:
        in_specs.append(row)
        args.append(target)
        out_specs.append(_full((1, LANES)))
        out_shape.append(jax.ShapeDtypeStruct((1, LANES), F32))
    else:
        out_specs.append(row)
        out_shape.append(jax.ShapeDtypeStruct((n, D_MODEL), BF16))
    return _hosted(
        body, comm, name=name, grid=(n // tm,), in_specs=in_specs, out_specs=out_specs, out_shape=out_shape,
        scratch_shapes=[], compiler_params=_params(("arbitrary",)), args=args)


def _ffn_bwd(name, h1s, h3s, z, dout, w1t, w3t, w2, g, b, comm=None):
    n = z.shape[0]
    tm = min(FFN_TILE, n)

    def body(h1_ref, h3_ref, z_ref, do_ref, w1_ref, w3_ref, w2_ref, g_ref, b_ref,
             dx_ref, a_ref, dh1_ref, dh3_ref, df_ref, dg_ref, db_ref):
        _, vjp = jax.vjp(_ln, z_ref[...], g_ref[...], b_ref[...])
        dz, dg, db = vjp(do_ref[...])

        @pl.when(pl.program_id(0) == 0)
        def _():
            dg_ref[...] = jnp.zeros_like(dg_ref)
            db_ref[...] = jnp.zeros_like(db_ref)

        dg_ref[...] += dg
        db_ref[...] += db
        df = _bf(0.5 * dz)
        df_ref[...] = df
        a, act_vjp = jax.vjp(_swiglu_act, h1_ref[...].astype(F32), h3_ref[...].astype(F32))
        dh1, dh3 = act_vjp(_mm_nt(df, w2_ref[...]))
        dh1 = _bf(dh1)
        dh3 = _bf(dh3)
        a_ref[...] = _bf(a)
        dh1_ref[...] = dh1
        dh3_ref[...] = dh3
        dx_ref[...] = DEEPNORM_ALPHA * dz + _mm(dh1, w1_ref[...]) + _mm(dh3, w3_ref[...])

    row = pl.BlockSpec((tm, D_MODEL), lambda i: (i, 0))
    hid = pl.BlockSpec((tm, D_FF), lambda i: (i, 0))
    wres = _resident((D_FF, D_MODEL))
    vec = _full((1, D_MODEL))
    hid_shape = jax.ShapeDtypeStruct((n, D_FF), BF16)
    return _hosted(
        body, comm, name=name, grid=(n // tm,),
        in_specs=[hid, hid, row, row, wres, wres, wres, vec, vec],
        out_specs=[row, hid, hid, hid, row, vec, vec],
        out_shape=[jax.ShapeDtypeStruct((n, D_MODEL), F32), hid_shape, hid_shape, hid_shape,
                   jax.ShapeDtypeStruct((n, D_MODEL), BF16),
                   jax.ShapeDtypeStruct((1, D_MODEL), F32), jax.ShapeDtypeStruct((1, D_MODEL), F32)],
        scratch_shapes=[], compiler_params=_params(("arbitrary",)),
        args=[h1s, h3s, z, dout, w1t, w3t, w2, g, b])


def _operand_spec(arr, mode, tn, width, parts):
    if mode == "shared":
        return pl.BlockSpec((tn, width), lambda s, k: (k, 0))
    assert mode == "cols" and arr.shape[1] == parts * width
    return pl.BlockSpec((tn, width), lambda s, k: (k, s))


def _grad_matmul(name, a, a_mode, ka, b, b_mode, kb, colsum=False, comm=None, parts=N_CHIPS):
    n = a.shape[-2]
    tn = min(GRAD_ROWS, n)
    nk = n // tn

    def body(*refs):
        if colsum:
            a_ref, b_ref, o_ref, cs_ref, acc = refs
        else:
            a_ref, b_ref, o_ref, acc = refs
        k = pl.program_id(1)
        av = a_ref[...]
        bv = b_ref[...]

        @pl.when(k == 0)
        def _():
            acc[...] = jnp.zeros_like(acc)
            if colsum:
                cs_ref[...] = jnp.zeros_like(cs_ref)

        acc[...] += _mm_tn(av, bv)
        if colsum:
            cs_ref[...] += jnp.sum(bv.astype(F32), axis=0, keepdims=True)

        @pl.when(k == nk - 1)
        def _():
            o_ref[0] = _bf(acc[...])

    out_specs = [pl.BlockSpec((1, ka, kb), lambda s, k: (s, 0, 0))]
    out_shape = [jax.ShapeDtypeStruct((parts, ka, kb), BF16)]
    if colsum:
        out_specs.append(pl.BlockSpec((1, kb), lambda s, k: (0, s)))
        out_shape.append(jax.ShapeDtypeStruct((1, parts * kb), F32))
    return _hosted(
        body, comm, name=name, grid=(parts, nk),
        in_specs=[_operand_spec(a, a_mode, tn, ka, parts), _operand_spec(b, b_mode, tn, kb, parts)],
        out_specs=out_specs, out_shape=out_shape,
        scratch_shapes=[pltpu.VMEM((ka, kb), F32)],
        compiler_params=_params(("arbitrary", "arbitrary")), args=[a, b])


def _in_proj(x1, w_in_g, b_in, comm=None):
    n = x1.shape[0]
    tm = min(FFN_TILE, n)

    def body(x_ref, w_ref, b_ref, o_ref):
        xv = x_ref[...]
        for j in range(N_CHIPS):
            cols = slice(j * IN_SHARD, (j + 1) * IN_SHARD)
            o_ref[:, cols] = _mm(xv, w_ref[:, cols]) + b_ref[:, cols]

    return _hosted(
        body, comm, name="in_proj", grid=(n // tm,),
        in_specs=[pl.BlockSpec((tm, D_MODEL), lambda i: (i, 0)),
                  _resident((D_MODEL, D_IN)), _full((1, D_IN))],
        out_specs=[pl.BlockSpec((tm, D_IN), lambda i: (i, 0))],
        out_shape=[jax.ShapeDtypeStruct((n, D_IN), F32)],
        scratch_shapes=[],
        compiler_params=_params(("arbitrary",)), args=[x1, w_in_g, b_in])


def _in_proj_dx(dproj, w_in_g, dz2, comm=None):
    n = dproj.shape[0]
    tm = min(FFN_TILE, n)

    def body(dp_ref, w_ref, dz_ref, o_ref):
        dx = DEEPNORM_ALPHA * dz_ref[...]
        for j in range(N_CHIPS):
            cols = slice(j * IN_SHARD, (j + 1) * IN_SHARD)
            dx = dx + _mm_nt(dp_ref[:, cols], w_ref[:, cols])
        o_ref[...] = dx

    return _hosted(
        body, comm, name="in_proj_dx", grid=(n // tm,),
        in_specs=[pl.BlockSpec((tm, D_IN), lambda i: (i, 0)),
                  _resident((D_MODEL, D_IN)),
                  pl.BlockSpec((tm, D_MODEL), lambda i: (i, 0))],
        out_specs=[pl.BlockSpec((tm, D_MODEL), lambda i: (i, 0))],
        out_shape=[jax.ShapeDtypeStruct((n, D_MODEL), F32)],
        scratch_shapes=[],
        compiler_params=_params(("arbitrary",)), args=[dproj, w_in_g, dz2])


def _rope_tables(seq_len):
    pos = jnp.arange(seq_len, dtype=F32)
    inv_freq = ROPE_THETA ** (-jnp.arange(0, ROPE_DIM, 2, dtype=F32) / ROPE_DIM)
    ang = pos[:, None] * inv_freq[None, :]
    cos, sin = jnp.cos(ang), jnp.sin(ang)
    half = ROPE_DIM // 2
    rest = HEAD_DIM - ROPE_DIM
    ones = jnp.ones((seq_len, rest), F32)
    zeros = jnp.zeros((seq_len, rest), F32)
    zh = jnp.zeros((seq_len, half), F32)
    c = jnp.concatenate([cos, cos, ones], axis=1)
    sa = jnp.concatenate([-sin, zh, zeros], axis=1)
    sb = jnp.concatenate([zh, sin, zeros], axis=1)
    reps = LANES // HEAD_DIM
    return tuple(jnp.tile(t, (1, reps)) for t in (c, sa, sb))


def _rope(t, c, sa, sb):
    w = t.shape[1]
    reps = w // LANES
    half = ROPE_DIM // 2
    return (t * jnp.tile(c, (1, reps)) + pltpu.roll(t, w - half, 1) * jnp.tile(sa, (1, reps))
            + pltpu.roll(t, half, 1) * jnp.tile(sb, (1, reps)))


def _rope_transposed(g, c, sa, sb):
    w = g.shape[1]
    reps = w // LANES
    half = ROPE_DIM // 2
    return (g * jnp.tile(c, (1, reps)) + pltpu.roll(g * jnp.tile(sa, (1, reps)), half, 1)
            + pltpu.roll(g * jnp.tile(sb, (1, reps)), w - half, 1))


GROUP = N_Q_HEADS // N_KV_HEADS


def _both_halves(t_pair, which):
    lo = lax.broadcasted_iota(jnp.int32, t_pair.shape, 1) < HEAD_DIM
    swapped = pltpu.roll(t_pair, HEAD_DIM, 1)
    return _bf(jnp.where(lo, t_pair, swapped) if which == 0 else jnp.where(lo, swapped, t_pair))


def _stack_heads(ref_or_val, kh):
    lo = lax.broadcasted_iota(jnp.int32, (ATTN_BLOCK, LANES), 1) < HEAD_DIM
    rows = []
    for gp in range(GROUP // 2):
        pair = kh * (GROUP // 2) + gp
        t = ref_or_val[:, pair * LANES:(pair + 1) * LANES]
        rows += [jnp.where(lo, t, jnp.zeros_like(t)), jnp.where(lo, jnp.zeros_like(t), t)]
    return jnp.concatenate(rows, axis=0)


def _unstack_pairs(stacked):
    lo = lax.broadcasted_iota(jnp.int32, (ATTN_BLOCK, LANES), 1) < HEAD_DIM
    b = ATTN_BLOCK
    return [jnp.where(lo, stacked[2 * gp * b:(2 * gp + 1) * b], stacked[(2 * gp + 1) * b:(2 * gp + 2) * b])
            for gp in range(GROUP // 2)]


def _attn_mask_t(n):
    cols = GROUP * ATTN_BLOCK
    kj = lax.broadcasted_iota(jnp.int32, (2 * ATTN_BLOCK, cols), 0)
    qi = lax.broadcasted_iota(jnp.int32, (2 * ATTN_BLOCK, cols), 1) % ATTN_BLOCK
    dist = qi + ATTN_BLOCK - kj
    return (dist >= 0) & (dist < ATTN_BLOCK) & (n * ATTN_BLOCK + kj - ATTN_BLOCK >= 0)


def _sink_row(sink_ref, kh):
    col = lax.broadcasted_iota(jnp.int32, (1, GROUP * ATTN_BLOCK), 1)
    row = jnp.full((1, GROUP * ATTN_BLOCK), sink_ref[0, kh * GROUP + GROUP - 1], F32)
    for i in reversed(range(GROUP - 1)):
        row = jnp.where(col < (i + 1) * ATTN_BLOCK, sink_ref[0, kh * GROUP + i], row)
    return row


def _attn_probs_t(q_masked, k_sel, mask_t, sink):
    s = _mm_nt(k_sel, q_masked) * (HEAD_DIM ** -0.5)
    s = jnp.where(mask_t, s, NEG_INF)
    m = jnp.maximum(jnp.max(s, axis=0, keepdims=True), sink)
    p = jnp.exp(s - m)
    e_sink = jnp.exp(sink - m)
    denom = jnp.sum(p, axis=0, keepdims=True) + e_sink
    return p / denom, e_sink / denom


def _attn_fwd(proj, tabs, sinks):
    n_tok = proj.shape[0]
    nb = n_tok // ATTN_BLOCK

    def body(q_ref, k_ref, v_ref, c_ref, sa_ref, sb_ref, sink_ref, y_ref, kprev, vprev):
        n = pl.program_id(0)

        @pl.when(n == 0)
        def _():
            kprev[...] = jnp.zeros_like(kprev)
            vprev[...] = jnp.zeros_like(vprev)

        c, sa, sb = c_ref[...], sa_ref[...], sb_ref[...]
        qr = _bf(_rope(q_ref[...], c, sa, sb))
        kr = _rope(k_ref[...], c, sa, sb)
        vc = v_ref[...]
        kk = jnp.concatenate([kprev[...], kr], axis=0)
        vv = jnp.concatenate([vprev[...], vc], axis=0)
        kprev[...] = kr
        vprev[...] = vc
        mask = _attn_mask_t(n)
        for kh in range(N_KV_HEADS):
            r, which = divmod(kh, 2)
            kb = _both_halves(kk[:, r * LANES:(r + 1) * LANES], which)
            vb = _both_halves(vv[:, r * LANES:(r + 1) * LANES], which)
            probs, _ = _attn_probs_t(_stack_heads(qr, kh), kb, mask, _sink_row(sink_ref, kh))
            for gp, out in enumerate(_unstack_pairs(_mm_tn(_bf(probs), vb))):
                pair = kh * (GROUP // 2) + gp
                y_ref[:, pair * LANES:(pair + 1) * LANES] = _bf(out)

    blk = lambda width, col: pl.BlockSpec((ATTN_BLOCK, width), lambda n: (n, col))
    tab = pl.BlockSpec((ATTN_BLOCK, LANES), lambda n: (n, 0))
    kvw = N_KV_HEADS * HEAD_DIM
    return pl.pallas_call(
        body, name="attn_fwd", grid=(nb,),
        in_specs=[blk(D_MODEL, COL_Q), blk(kvw, COL_K), blk(kvw, COL_V), tab, tab, tab,
                  pl.BlockSpec(memory_space=pltpu.SMEM)],
        out_specs=pl.BlockSpec((ATTN_BLOCK, D_MODEL), lambda n: (n, 0)),
        out_shape=jax.ShapeDtypeStruct((n_tok, D_MODEL), BF16),
        scratch_shapes=[pltpu.VMEM((ATTN_BLOCK, kvw), F32), pltpu.VMEM((ATTN_BLOCK, kvw), F32)],
        compiler_params=_params(("arbitrary",)),
    )(proj, proj, proj, *tabs, sinks)


def _attn_bwd(proj, dy, tabs, sinks, dproj, comm=None):
    n_tok = proj.shape[0]
    nb = n_tok // ATTN_BLOCK
    kvw = N_KV_HEADS * HEAD_DIM

    def body(q_ref, k_ref, v_ref, do_ref, c_ref, sa_ref, sb_ref, cp_ref, sap_ref, sbp_ref, sink_ref, _,
             dqkv_ref, dsink_ref, kprev, vprev, dkc, dvc, dqc):
        n = pl.program_id(0)

        @pl.when(n == 0)
        def _():
            for ref in (kprev, vprev, dkc, dvc, dqc, dsink_ref):
                ref[...] = jnp.zeros_like(ref)

        prev_tabs = (cp_ref[...], sap_ref[...], sbp_ref[...])

        @pl.when(n < nb)
        def _():
            c, sa, sb = c_ref[...], sa_ref[...], sb_ref[...]
            qr = _bf(_rope(q_ref[...], c, sa, sb))
            kr = _rope(k_ref[...], c, sa, sb)
            vc = v_ref[...]
            kk = jnp.concatenate([kprev[...], kr], axis=0)
            vv = jnp.concatenate([vprev[...], vc], axis=0)
            kprev[...] = kr
            vprev[...] = vc
            mask = _attn_mask_t(n)
            lane = lax.broadcasted_iota(jnp.int32, (1, LANES), 1)
            lo2 = lax.broadcasted_iota(jnp.int32, (2 * ATTN_BLOCK, LANES), 1) < HEAD_DIM
            dsink = jnp.zeros((1, LANES), F32)
            dq_pairs = []
            dk_full = []
            dv_full = []
            for kh in range(N_KV_HEADS):
                r, which = divmod(kh, 2)
                kb = _both_halves(kk[:, r * LANES:(r + 1) * LANES], which)
                vb = _both_halves(vv[:, r * LANES:(r + 1) * LANES], which)
                qs = _stack_heads(qr, kh)
                dos = _bf(_stack_heads(do_ref, kh))
                probs, p_sink = _attn_probs_t(qs, kb, mask, _sink_row(sink_ref, kh))
                dp = _mm_nt(vb, dos)
                delta = jnp.sum(probs * dp, axis=0, keepdims=True)
                ds = _bf(probs * (dp - delta) * (HEAD_DIM ** -0.5))
                sink_terms = p_sink * delta
                for i in range(GROUP):
                    head_sum = jnp.sum(sink_terms[:, i * ATTN_BLOCK:(i + 1) * ATTN_BLOCK])
                    dsink = dsink + jnp.where(lane == kh * GROUP + i, -head_sum, 0.0)
                dq_pairs += _unstack_pairs(_mm_tn(ds, kb))
                dk_acc = _mm(ds, qs)
                dv_acc = _mm(_bf(probs), dos)
                dk_full.append(dk_acc + pltpu.roll(dk_acc, HEAD_DIM, 1))
                dv_full.append(dv_acc + pltpu.roll(dv_acc, HEAD_DIM, 1))
            dk_pairs = [jnp.where(lo2, dk_full[2 * r], dk_full[2 * r + 1]) for r in range(N_KV_HEADS // 2)]
            dv_pairs = [jnp.where(lo2, dv_full[2 * r], dv_full[2 * r + 1]) for r in range(N_KV_HEADS // 2)]
            dsink_ref[...] += dsink
            dqkv_ref[:, :D_MODEL] = _bf(dqc[...])
            dqc[...] = _rope_transposed(jnp.concatenate(dq_pairs, axis=1), c, sa, sb)
            dk_all = jnp.concatenate(dk_pairs, axis=1)
            dv_all = jnp.concatenate(dv_pairs, axis=1)
            dqkv_ref[:, D_MODEL:D_MODEL + kvw] = _bf(_rope_transposed(dkc[...] + dk_all[:ATTN_BLOCK], *prev_tabs))
            dqkv_ref[:, D_MODEL + kvw:] = _bf(dvc[...] + dv_all[:ATTN_BLOCK])
            dkc[...] = dk_all[ATTN_BLOCK:]
            dvc[...] = dv_all[ATTN_BLOCK:]

        @pl.when(n == nb)
        def _():
            dqkv_ref[:, :D_MODEL] = _bf(dqc[...])
            dqkv_ref[:, D_MODEL:D_MODEL + kvw] = _bf(_rope_transposed(dkc[...], *prev_tabs))
            dqkv_ref[:, D_MODEL + kvw:] = _bf(dvc[...])

    cur = lambda n: jnp.minimum(n, nb - 1)
    prev = lambda n: jnp.maximum(n - 1, 0)
    blk = lambda width, col: pl.BlockSpec((ATTN_BLOCK, width), lambda n: (cur(n), col))
    tab = pl.BlockSpec((ATTN_BLOCK, LANES), lambda n: (cur(n), 0))
    tabp = pl.BlockSpec((ATTN_BLOCK, LANES), lambda n: (prev(n), 0))
    return _hosted(
        body, comm, name="attn_bwd", grid=(nb + 1,),
        in_specs=[blk(D_MODEL, COL_Q), blk(kvw, COL_K), blk(kvw, COL_V), blk(D_MODEL, 0), tab, tab, tab, tabp, tabp, tabp,
                  pl.BlockSpec(memory_space=pltpu.SMEM), pl.BlockSpec(memory_space=pl.ANY)],
        out_specs=[pl.BlockSpec((ATTN_BLOCK, ATTN_WIDTH), lambda n: (prev(n), COL_ATTN)),
                   pl.BlockSpec((1, LANES), lambda n: (0, 0))],
        out_shape=[jax.ShapeDtypeStruct(dproj.shape, dproj.dtype), jax.ShapeDtypeStruct((1, LANES), F32)],
        scratch_shapes=[pltpu.VMEM((ATTN_BLOCK, kvw), F32)] * 4 + [pltpu.VMEM((ATTN_BLOCK, D_MODEL), F32)],
        compiler_params=_params(("arbitrary",)), args=[proj, proj, proj, dy, *tabs, *tabs, sinks, dproj],
        aliases={11: 0})


def _bmm(a, b):
    return lax.dot_general(a, b, (((2,), (1,)), ((0,), (0,))), preferred_element_type=F32)


def _bmm_nt(a, b):
    return lax.dot_general(a, b, (((2,), (2,)), ((0,), (0,))), preferred_element_type=F32)


def _bmm_tn(a, b):
    return lax.dot_general(a, b, (((1,), (1,)), ((0,), (0,))), preferred_element_type=F32)


def _tril(cb, upper=False):
    shape = (cb, HGRN_CHUNK, HGRN_CHUNK)
    r, c = lax.broadcasted_iota(jnp.int32, shape, 1), lax.broadcasted_iota(jnp.int32, shape, 2)
    return (r <= c) if upper else (r >= c)


def _tri_matmul(x, upper):
    return lax.dot_general(_tril(x.shape[0], upper).astype(F32), x, (((2,), (1,)), ((0,), (0,))),
                           precision=lax.Precision.HIGHEST, preferred_element_type=F32)


@jax.custom_vjp
def _chunk_cumsum(x):
    return _tri_matmul(x, False)


_chunk_cumsum.defvjp(lambda x: (_tri_matmul(x, False), None), lambda _, g: (_tri_matmul(g, True),))


def _hg_elem(fl, qh, lb):
    f = lb + (1.0 - lb) * _sig(fl)
    k = 1.0 - f
    gc = _chunk_cumsum(jnp.log(f))
    last = lax.broadcasted_iota(jnp.int32, gc.shape, 1) == HGRN_CHUNK - 1
    g_last = jnp.sum(jnp.where(last, gc, 0.0), axis=1, keepdims=True)
    q = qh * _sig(qh)
    return q * jnp.exp(gc), k * jnp.exp(-gc), k * jnp.exp(g_last - gc), jnp.exp(g_last)


def _hg_out(q_dec, k_inv, v, st):
    sc = jnp.where(_tril(q_dec.shape[0]), _bmm_nt(_bf(q_dec), _bf(k_inv)), 0.0)
    return _bmm(_bf(sc), _bf(v)) + _bmm_nt(_bf(q_dec), _bf(st)), sc


def _hg_post(o, og, ng):
    on = o * lax.rsqrt(jnp.mean(o * o, axis=-1, keepdims=True) + RMS_EPS) * ng
    return on * (og * _sig(og))


def _hgrn_specs(n_tok, rev):
    nc = n_tok // HGRN_CHUNK
    cb = min(HGRN_CHUNKS_PER_STEP, nc)
    nt = nc // cb
    rows = cb * HGRN_CHUNK
    tt = (lambda t: nt - 1 - t) if rev else (lambda t: t)
    col = lambda base: pl.BlockSpec((rows, LANES), lambda h, t: (tt(t), base + h))
    head_cols = pl.BlockSpec((rows, HGRN_HEAD_WIDTH), lambda h, t: (tt(t), COL_HGRN + h))
    head_vec = pl.BlockSpec((1, LANES), lambda h, t: (0, h))
    one_vec = pl.BlockSpec((1, LANES), lambda h, t: (0, 0))
    state = pl.BlockSpec((1, cb, HGRN_DK, HGRN_DK), lambda h, t: (h, tt(t), 0, 0))
    return nc, cb, nt, col, head_cols, head_vec, one_vec, state


def _hgrn_fwd(proj, lb, ng):
    n_tok = proj.shape[0]
    nc, cb, nt, col, head_cols, head_vec, one_vec, state = _hgrn_specs(n_tok, False)

    def body(in_ref, lb_ref, ng_ref, y_ref, st_ref, s_acc):
        @pl.when(pl.program_id(1) == 0)
        def _():
            s_acc[...] = jnp.zeros_like(s_acc)

        fl, qh, v, og = (in_ref[:, i * LANES:(i + 1) * LANES].reshape(cb, HGRN_CHUNK, LANES) for i in range(4))
        q_dec, k_inv, k_end, decay = _hg_elem(fl, qh, lb_ref[...])
        upd = _bmm_tn(_bf(v), _bf(k_end))
        st = s_acc[...]
        for ci in range(cb):
            st_ref[0, ci] = st
            st = st * decay[ci] + upd[ci]
        s_acc[...] = st
        o, _ = _hg_out(q_dec, k_inv, v, st_ref[0])
        y_ref[...] = _bf(_hg_post(o, og, ng_ref[...]).reshape(cb * HGRN_CHUNK, LANES))

    return pl.pallas_call(
        body, name="hgrn_fwd", grid=(HGRN_HEADS, nt),
        in_specs=[head_cols, head_vec, one_vec],
        out_specs=[col(0), state],
        out_shape=[jax.ShapeDtypeStruct((n_tok, D_MODEL), BF16),
                   jax.ShapeDtypeStruct((HGRN_HEADS, nc, HGRN_DK, HGRN_DK), F32)],
        scratch_shapes=[pltpu.VMEM((HGRN_DK, HGRN_DK), F32)],
        compiler_params=_params(("arbitrary", "arbitrary")),
    )(proj, lb, ng)


def _hgrn_bwd(proj, lb, ng, states, dy, dproj):
    n_tok = proj.shape[0]
    nc, cb, nt, col, head_cols, head_vec, one_vec, state = _hgrn_specs(n_tok, True)

    def body(in_ref, lb_ref, ng_ref, st_ref, dy_ref, _, d_ref, dlb_ref, dng_ref, g_acc, g_all):
        h = pl.program_id(0)
        t = pl.program_id(1)

        @pl.when(t == 0)
        def _():
            g_acc[...] = jnp.zeros_like(g_acc)
            dlb_ref[...] = jnp.zeros_like(dlb_ref)

        @pl.when((t == 0) & (h == 0))
        def _():
            dng_ref[...] = jnp.zeros_like(dng_ref)

        fl, qh, v, og = (in_ref[:, i * LANES:(i + 1) * LANES].reshape(cb, HGRN_CHUNK, LANES) for i in range(4))
        (q_dec, k_inv, k_end, decay), elem_vjp = jax.vjp(_hg_elem, fl, qh, lb_ref[...])
        st = st_ref[0]
        o, sc = _hg_out(q_dec, k_inv, v, st)
        _, post_vjp = jax.vjp(_hg_post, o, og, ng_ref[...])
        do, dog, dng = post_vjp(dy_ref[...].reshape(cb, HGRN_CHUNK, LANES))
        dob, vb, qb = _bf(do), _bf(v), _bf(q_dec)
        dsc = _bf(jnp.where(_tril(cb), _bmm_nt(dob, vb), 0.0))
        p = _bmm_tn(dob, qb)
        g = g_acc[...]
        for ci in reversed(range(cb)):
            g_all[ci] = g
            g = g * decay[ci] + p[ci]
        g_acc[...] = g
        g = g_all[...]
        gb = _bf(g)
        dq_dec = _bmm(dsc, _bf(k_inv)) + _bmm(dob, _bf(st))
        dk_inv = _bmm_tn(dsc, qb)
        dv = _bmm_tn(_bf(sc), dob) + _bmm_nt(_bf(k_end), gb)
        dk_end = _bmm(vb, gb)
        ddecay = jnp.sum(st * g, axis=1, keepdims=True)
        dfl, dqh, dlb = elem_vjp((dq_dec, dk_inv, dk_end, ddecay))
        for i, val in enumerate((dfl, dqh, dv, dog)):
            d_ref[:, i * LANES:(i + 1) * LANES] = _bf(val.reshape(cb * HGRN_CHUNK, LANES))
        dlb_ref[...] += dlb
        dng_ref[...] += dng

    return pl.pallas_call(
        body, name="hgrn_bwd", grid=(HGRN_HEADS, nt),
        in_specs=[head_cols, head_vec, one_vec, state, col(0), pl.BlockSpec(memory_space=pl.ANY)],
        out_specs=[head_cols, head_vec, one_vec],
        out_shape=[jax.ShapeDtypeStruct(dproj.shape, dproj.dtype),
                   jax.ShapeDtypeStruct((1, D_MODEL), F32), jax.ShapeDtypeStruct((1, LANES), F32)],
        scratch_shapes=[pltpu.VMEM((HGRN_DK, HGRN_DK), F32), pltpu.VMEM((cb, HGRN_DK, HGRN_DK), F32)],
        compiler_params=_params(("arbitrary", "arbitrary")), input_output_aliases={5: 0},
    )(proj, lb, ng, states, dy, dproj)


def _lb_fwd(lb_logits):
    def lb_of(l0, l1):
        m = jnp.maximum(l0, l1)
        e0, e1 = jnp.exp(l0 - m), jnp.exp(l1 - m)
        return e0 / (e0 + e1)

    def body(l_ref, o_ref):
        o_ref[...] = lb_of(l_ref[0:1, :], l_ref[1:2, :])

    lb = pl.pallas_call(body, name="lb_fwd", out_shape=jax.ShapeDtypeStruct((1, D_MODEL), F32))(lb_logits)
    return lb, lb_of


def _mix_fwd(y_attn, y_hgrn, proj, x1, w_pa, w_ph, w_out, g, b):
    n = x1.shape[0]
    tm = min(MIX_TILE, n)

    def body(ya_ref, yh_ref, gt_ref, x_ref, wpa, wph, wo, g_ref, b_ref, z_ref, o_ref):
        ya = _mm(ya_ref[...], wpa[...])
        yh = _mm(yh_ref[...], wph[...])
        merged = _sig(gt_ref[:, :D_MODEL]) * ya + _sig(gt_ref[:, D_MODEL:]) * yh
        z = DEEPNORM_ALPHA * x_ref[...] + _mm(_bf(merged), wo[...])
        z_ref[...] = z
        o_ref[...] = _ln(z, g_ref[...], b_ref[...])

    row = pl.BlockSpec((tm, D_MODEL), lambda i: (i, 0))
    gates = pl.BlockSpec((tm, GATES_WIDTH), lambda i: (i, 0))
    sq = _resident((D_MODEL, D_MODEL))
    vec = _full((1, D_MODEL))
    return pl.pallas_call(
        body, name="mix_fwd", grid=(n // tm,),
        in_specs=[row, row, gates, row, sq, sq, sq, vec, vec],
        out_specs=[row, row], out_shape=[jax.ShapeDtypeStruct((n, D_MODEL), F32)] * 2,
        compiler_params=_params(("arbitrary",)),
    )(y_attn, y_hgrn, proj, x1, w_pa, w_ph, w_out, g, b)


def _mix_bwd(dx2, z2, y_attn, y_hgrn, proj, w_pa, w_ph, w_out, g, b):
    n = z2.shape[0]
    tm = min(MIX_TILE, n)

    def body(do_ref, z_ref, ya_ref, yh_ref, gt_ref, wpa, wph, wo, g_ref, b_ref,
             dz_ref, dzb_ref, mg_ref, dya_ref, dyh_ref, dyat_ref, dyhg_ref, dgt_ref, dg_ref, db_ref):
        _, vjp = jax.vjp(_ln, z_ref[...], g_ref[...], b_ref[...])
        dz, dg, db = vjp(do_ref[...])

        @pl.when(pl.program_id(0) == 0)
        def _():
            dg_ref[...] = jnp.zeros_like(dg_ref)
            db_ref[...] = jnp.zeros_like(db_ref)

        dg_ref[...] += dg
        db_ref[...] += db
        dz_ref[...] = dz
        ya = _mm(ya_ref[...], wpa[...])
        yh = _mm(yh_ref[...], wph[...])
        def merge(ga, gh, ya, yh):
            return _sig(ga) * ya + _sig(gh) * yh

        merged, merge_vjp = jax.vjp(merge, gt_ref[:, :D_MODEL], gt_ref[:, D_MODEL:], ya, yh)
        mg_ref[...] = _bf(merged)
        dzb = _bf(dz)
        dzb_ref[...] = dzb
        dmerged = _mm_nt(dzb, wo[...])
        dga, dgh, dya, dyh = merge_vjp(dmerged)
        dya = _bf(dya)
        dyh = _bf(dyh)
        dya_ref[...] = dya
        dyh_ref[...] = dyh
        dgt_ref[:, :D_MODEL] = _bf(dga)
        dgt_ref[:, D_MODEL:] = _bf(dgh)
        dyat_ref[...] = _mm_nt(dya, wpa[...])
        dyhg_ref[...] = _mm_nt(dyh, wph[...])

    row = pl.BlockSpec((tm, D_MODEL), lambda i: (i, 0))
    gates = pl.BlockSpec((tm, GATES_WIDTH), lambda i: (i, 0))
    sq = _resident((D_MODEL, D_MODEL))
    vec = _full((1, D_MODEL))
    f32_row = jax.ShapeDtypeStruct((n, D_MODEL), F32)
    bf_row = jax.ShapeDtypeStruct((n, D_MODEL), BF16)
    vec_shape = jax.ShapeDtypeStruct((1, D_MODEL), F32)
    return pl.pallas_call(
        body, name="mix_bwd", grid=(n // tm,),
        in_specs=[row, row, row, row, gates, sq, sq, sq, vec, vec],
        out_specs=[row, row, row, row, row, row, row, gates, vec, vec],
        out_shape=[f32_row, bf_row, bf_row, bf_row, bf_row, f32_row, f32_row,
                   jax.ShapeDtypeStruct((n, D_IN), BF16), vec_shape, vec_shape],
        compiler_params=_params(("arbitrary",)),
    )(dx2, z2, y_attn, y_hgrn, proj, w_pa, w_ph, w_out, g, b)


def _position():
    x, y, c = lax.axis_index("x"), lax.axis_index("y"), lax.axis_index("c")
    chips = [(1 - x, y), (x, 1 - y), (1 - x, 1 - y)]
    return x, y, c, chips


def _any_specs(k):
    return [pl.BlockSpec(memory_space=pl.ANY)] * k


class _GatherWeights:
    def __init__(self, shards):
        nw = len(shards)
        self.inputs = list(shards)
        self.out_shape = [jax.ShapeDtypeStruct((N_CHIPS, *s.shape), s.dtype) for s in shards]
        self.scratch = [pltpu.SemaphoreType.DMA((nw,)), pltpu.SemaphoreType.DMA((nw * 6,)),
                        pltpu.SemaphoreType.DMA((nw * 6,))]

    def _copies(self, ins, outs, sems):
        nw = len(ins)
        local_sem, send_sem, recv_sem = sems
        x, y, c, chips = _position()
        me = 2 * x + y
        sibling = (x, y, 1 - c)
        half_rows = [s.shape[0] // 2 for s in self.inputs]

        def half(w, chip_idx, which):
            return outs[w].at[chip_idx, pl.ds(which * half_rows[w], half_rows[w])]

        def remote(w, k, src, dst, to):
            return pltpu.make_async_remote_copy(src_ref=src, dst_ref=dst, send_sem=send_sem.at[w * 6 + k],
                                                recv_sem=recv_sem.at[w * 6 + k], device_id=to, device_id_type=MESH)

        local = [pltpu.make_async_copy(ins[w], outs[w].at[me], local_sem.at[w]) for w in range(nw)]
        first = [remote(w, j, ins[w].at[pl.ds(c * half_rows[w], half_rows[w])], half(w, me, c), (px, py, c))
                 for w in range(nw) for j, (px, py) in enumerate(chips)]
        landed = [half(w, 2 * px + py, c) for w in range(nw) for (px, py) in chips]
        arrive = [remote(w, j, landed[w * 3 + j], landed[w * 3 + j], (px, py, c))
                  for w in range(nw) for j, (px, py) in enumerate(chips)]
        passed = [remote(w, 3 + j, landed[w * 3 + j], landed[w * 3 + j], sibling) for w in range(nw) for j in range(3)]
        from_sibling = [remote(w, 3 + j, half(w, 2 * px + py, 1 - c), half(w, 2 * px + py, 1 - c), sibling)
                        for w in range(nw) for j, (px, py) in enumerate(chips)]
        return local, first, arrive, passed, from_sibling

    def start(self, ins, outs, sems):
        local, first, _, _, _ = self._copies(ins, outs, sems)
        for cp in local + first:
            cp.start()

    def finish(self, ins, outs, sems):
        local, first, arrive, passed, from_sibling = self._copies(ins, outs, sems)
        for cp_in, cp_on in zip(arrive, passed):
            cp_in.wait_recv()
            cp_on.start()
        for cp in from_sibling:
            cp.wait_recv()
        for cp in first + passed:
            cp.wait_send()
        for cp in local:
            cp.wait()


class _ExchangeGrads:
    def __init__(self, grads):
        nw = len(grads)
        self.inputs = list(grads)
        self.out_shape = [jax.ShapeDtypeStruct(g.shape, g.dtype) for g in grads]
        self.scratch = [pltpu.SemaphoreType.DMA((nw,)), pltpu.SemaphoreType.DMA((nw * 3,)),
                        pltpu.SemaphoreType.DMA((nw * 3,))]

    def _copies(self, ins, outs, sems):
        nw = len(ins)
        local_sem, send_sem, recv_sem = sems
        x, y, c, chips = _position()
        me = 2 * x + y

        def remote(w, j, src, dst, chip):
            return pltpu.make_async_remote_copy(src_ref=src, dst_ref=dst, send_sem=send_sem.at[w * 3 + j],
                                                recv_sem=recv_sem.at[w * 3 + j], device_id=(*chip, c),
                                                device_id_type=MESH)

        local = [pltpu.make_async_copy(ins[w].at[me], outs[w].at[me], local_sem.at[w]) for w in range(nw)]
        sends = [remote(w, j, ins[w].at[2 * px + py], outs[w].at[me], (px, py))
                 for w in range(nw) for j, (px, py) in enumerate(chips)]
        arrive = [remote(w, j, outs[w].at[2 * px + py], outs[w].at[2 * px + py], (px, py))
                  for w in range(nw) for j, (px, py) in enumerate(chips)]
        return local, sends, arrive

    def start(self, ins, outs, sems):
        local, sends, _ = self._copies(ins, outs, sems)
        for cp in local + sends:
            cp.start()

    def finish(self, ins, outs, sems):
        local, sends, arrive = self._copies(ins, outs, sems)
        for cp in arrive:
            cp.wait_recv()
        for cp in sends:
            cp.wait_send()
        for cp in local:
            cp.wait()


def _hosted(body, comm, *, name, grid, in_specs, out_specs, out_shape, scratch_shapes, compiler_params, args,
            aliases=None):
    aliases = aliases or {}
    if comm is None:
        res = pl.pallas_call(body, name=name, grid=grid, in_specs=in_specs, out_specs=out_specs, out_shape=out_shape,
                             scratch_shapes=scratch_shapes, compiler_params=compiler_params,
                             input_output_aliases=aliases)(*args)
        return list(res), []
    n_in, n_out, n_scr = len(in_specs), len(out_specs), len(scratch_shapes)
    c_in, c_out = len(comm.inputs), len(comm.out_shape)

    def hosted_body(*refs):
        refs = list(refs)
        cut = lambda k: (refs[:k], refs[k:])
        main_in, refs = cut(n_in)
        comm_in, refs = cut(c_in)
        main_out, refs = cut(n_out)
        comm_out, refs = cut(c_out)
        main_scr, comm_scr = cut(n_scr)
        ids = [pl.program_id(a) for a in range(len(grid))]
        first = functools.reduce(jnp.logical_and, [i == 0 for i in ids])
        last = functools.reduce(jnp.logical_and, [i == g - 1 for i, g in zip(ids, grid)])

        @pl.when(first)
        def _():
            comm.start(comm_in, comm_out, comm_scr)

        body(*main_in, *main_out, *main_scr)

        @pl.when(last)
        def _():
            comm.finish(comm_in, comm_out, comm_scr)

    res = pl.pallas_call(
        hosted_body, name=name, grid=grid, in_specs=[*in_specs, *_any_specs(c_in)],
        out_specs=[*out_specs, *_any_specs(c_out)], out_shape=[*out_shape, *comm.out_shape],
        scratch_shapes=[*scratch_shapes, *comm.scratch], compiler_params=compiler_params,
        input_output_aliases=aliases,
    )(*args, *comm.inputs)
    return list(res[:n_out]), list(res[n_out:])


def _run_comm(name, comm):
    def body(*refs):
        refs = list(refs)
        c_in, c_out = len(comm.inputs), len(comm.out_shape)
        ins, outs, sems = refs[:c_in], refs[c_in:c_in + c_out], refs[c_in + c_out:]
        comm.start(ins, outs, sems)
        comm.finish(ins, outs, sems)

    return list(pl.pallas_call(
        body, name=name, in_specs=_any_specs(len(comm.inputs)), out_specs=_any_specs(len(comm.out_shape)),
        out_shape=comm.out_shape, scratch_shapes=comm.scratch,
    )(*comm.inputs))


def _sum_slots(name, slots):
    _, rows, cols = slots.shape
    tr = _update_rows(rows)

    def body(s_ref, o_ref):
        acc = s_ref[0].astype(F32)
        for i in range(1, N_CHIPS):
            acc = acc + s_ref[i].astype(F32)
        o_ref[...] = acc

    return pl.pallas_call(
        body, name=name, grid=(rows // tr,),
        in_specs=[pl.BlockSpec((N_CHIPS, tr, cols), lambda i: (0, i, 0))],
        out_specs=pl.BlockSpec((tr, cols), lambda i: (i, 0)),
        out_shape=jax.ShapeDtypeStruct((rows, cols), F32),
        compiler_params=_params(("arbitrary",)),
    )(slots)


def _swap_with_sibling(parts):
    nw = len(parts)

    def body(*refs):
        ins, outs = refs[:nw], refs[nw:2 * nw]
        send_sem, recv_sem = refs[2 * nw:]
        x, y, c, _ = _position()
        copies = [pltpu.make_async_remote_copy(src_ref=ins[w], dst_ref=outs[w], send_sem=send_sem.at[w],
                                               recv_sem=recv_sem.at[w], device_id=(x, y, 1 - c), device_id_type=MESH)
                  for w in range(nw)]
        for cp in copies:
            cp.start()
        for cp in copies:
            cp.wait()

    return pl.pallas_call(
        body, name="swap_with_sibling",
        in_specs=_any_specs(nw), out_specs=_any_specs(nw),
        out_shape=[jax.ShapeDtypeStruct(p.shape, p.dtype) for p in parts],
        scratch_shapes=[pltpu.SemaphoreType.DMA((nw,)), pltpu.SemaphoreType.DMA((nw,))],
    )(*parts)


def _sum_small(part):
    def body(p_ref, o_ref, buf, send_sem, recv_sem):
        x, y, c, _ = _position()
        me = 4 * x + 2 * y + c
        buf[me] = p_ref[...]
        copies = []
        for k in range(1, N_DEV):
            peer = tuple(1 - v if (k >> s) & 1 else v for v, s in ((x, 2), (y, 1), (c, 0)))
            copies.append(pltpu.make_async_remote_copy(src_ref=p_ref, dst_ref=buf.at[me], send_sem=send_sem.at[k - 1],
                                                       recv_sem=recv_sem.at[k - 1], device_id=peer, device_id_type=MESH))
        for cp in copies:
            cp.start()
        for cp in copies:
            cp.wait()
        acc = buf[0]
        for d in range(1, N_DEV):
            acc = acc + buf[d]
        o_ref[...] = acc

    vm = pl.BlockSpec(memory_space=pltpu.VMEM)
    return pl.pallas_call(
        body, name="sum_small", in_specs=[vm], out_specs=vm,
        out_shape=jax.ShapeDtypeStruct((1, SM_LEN), F32),
        scratch_shapes=[pltpu.VMEM((N_DEV, 1, SM_LEN), F32), pltpu.SemaphoreType.DMA((N_DEV - 1,)),
                        pltpu.SemaphoreType.DMA((N_DEV - 1,))],
    )(part)


def _adamw(w, g, m, v):
    m = ADAM_B1 * m + (1.0 - ADAM_B1) * g
    v = ADAM_B2 * v + (1.0 - ADAM_B2) * (g * g)
    m_hat = m / (1.0 - ADAM_B1 ** ADAM_STEP)
    v_hat = v / (1.0 - ADAM_B2 ** ADAM_STEP)
    delta = -ADAM_LR * (m_hat / (jnp.sqrt(v_hat) + ADAM_EPS) + ADAM_WD * w)
    return delta, m, v


def _adam_big(name, p_own, p_sibling, w, m, v):
    rows, cols = w.shape
    tr = _update_rows(rows)

    def body(p_ref, q_ref, w_ref, m_ref, v_ref, g_ref, d_ref, nm_ref, nv_ref):
        g = p_ref[...] + q_ref[...]
        g_ref[...] = g
        d_ref[...], nm_ref[...], nv_ref[...] = _adamw(w_ref[...], g, m_ref[...], v_ref[...])

    spec = pl.BlockSpec((tr, cols), lambda i: (i, 0))
    return pl.pallas_call(
        body, name=name, grid=(rows // tr,), in_specs=[spec] * 5, out_specs=[spec] * 4,
        out_shape=[jax.ShapeDtypeStruct((rows, cols), F32)] * 4,
        compiler_params=_params(("arbitrary",)),
    )(p_own, p_sibling, w, m, v)


_SMALL_AT = {"ln1_g": 0, "ln1_b": D_MODEL, "ln2_g": 2 * D_MODEL, "ln2_b": 3 * D_MODEL, "ln3_g": 4 * D_MODEL,
             "ln3_b": 5 * D_MODEL, "b_in": SM_BIN, "attn_sinks": SM_SINK, "hgrn_norm_g": SM_NG}


def _adam_small(total, w, m, v, lb_of):
    names = list(_SMALL)
    k = len(names)

    def body(*refs):
        t_ref = refs[0]
        w_refs, m_refs, v_refs = (refs[1 + i * k:1 + (i + 1) * k] for i in range(3))
        g_refs, d_refs, nm_refs, nv_refs = (refs[1 + (3 + i) * k:1 + (4 + i) * k] for i in range(4))
        for i, name in enumerate(names):
            if name == "hgrn_lb_logits":
                _, vjp = jax.vjp(lb_of, w_refs[i][0:1, :], w_refs[i][1:2, :])
                g_refs[i][0:1, :], g_refs[i][1:2, :] = vjp(t_ref[:, SM_LB:SM_LOSS])
            else:
                at = _SMALL_AT[name]
                g_refs[i][...] = t_ref[:, at:at + w_refs[i].shape[1]]
            d_refs[i][...], nm_refs[i][...], nv_refs[i][...] = _adamw(
                w_refs[i][...], g_refs[i][...], m_refs[i][...], v_refs[i][...])

    shapes = [jax.ShapeDtypeStruct(w[name].shape, F32) for name in names]
    res = pl.pallas_call(body, name="adam_small", out_shape=shapes * 4)(
        total, *[w[n] for n in names], *[m[n] for n in names], *[v[n] for n in names])
    return [dict(zip(names, res[i * k:(i + 1) * k])) for i in range(4)]


_BIG = ("ffn1_w1", "ffn1_w3", "ffn1_w2", "w_in", "w_proj_attn", "w_proj_hgrn", "w_out", "ffn2_w1", "ffn2_w3", "ffn2_w2")
_SMALL = ("ln1_g", "ln1_b", "ln2_g", "ln2_b", "ln3_g", "ln3_b", "b_in", "attn_sinks", "hgrn_norm_g", "hgrn_lb_logits")
_ORDER = ("ln1_g", "ln1_b", "ffn1_w1", "ffn1_w3", "ffn1_w2", "ln2_g", "ln2_b", "w_in", "b_in", "attn_sinks",
          "hgrn_lb_logits", "hgrn_norm_g", "w_proj_attn", "w_proj_hgrn", "w_out", "ln3_g", "ln3_b",
          "ffn2_w1", "ffn2_w3", "ffn2_w2")


_TRANSPOSED = ("ffn1_w1", "ffn1_w3", "ffn2_w1", "ffn2_w3")


def _local_view(name, arr):
    return arr[0].T if name in _TRANSPOSED else arr[0]


def _ffn_grad(name, hidden, other, comm=None):
    (dw,), comm_out = _grad_matmul(name, hidden, "cols", D_FF // FF_GRAD_PARTS, other, "shared", D_MODEL,
                                   comm=comm, parts=FF_GRAD_PARTS)
    return dw.reshape(N_CHIPS, D_FF // N_CHIPS, D_MODEL), comm_out


def kernel(x, ln1_g, ln1_b, ffn1_w1, ffn1_w3, ffn1_w2, ln2_g, ln2_b, w_in, b_in, attn_sinks, hgrn_lb_logits, hgrn_norm_g, w_proj_attn, w_proj_hgrn, w_out, ln3_g, ln3_b, ffn2_w1, ffn2_w3, ffn2_w2, loss_target, m_ln1_g, m_ln1_b, m_ffn1_w1, m_ffn1_w3, m_ffn1_w2, m_ln2_g, m_ln2_b, m_w_in, m_b_in, m_attn_sinks, m_hgrn_lb_logits, m_hgrn_norm_g, m_w_proj_attn, m_w_proj_hgrn, m_w_out, m_ln3_g, m_ln3_b, m_ffn2_w1, m_ffn2_w3, m_ffn2_w2, v_ln1_g, v_ln1_b, v_ffn1_w1, v_ffn1_w3, v_ffn1_w2, v_ln2_g, v_ln2_b, v_w_in, v_b_in, v_attn_sinks, v_hgrn_lb_logits, v_hgrn_norm_g, v_w_proj_attn, v_w_proj_hgrn, v_w_out, v_ln3_g, v_ln3_b, v_ffn2_w1, v_ffn2_w3, v_ffn2_w2):
    w = dict(ln1_g=ln1_g, ln1_b=ln1_b, ffn1_w1=ffn1_w1, ffn1_w3=ffn1_w3, ffn1_w2=ffn1_w2, ln2_g=ln2_g, ln2_b=ln2_b,
             w_in=w_in, b_in=b_in, attn_sinks=attn_sinks, hgrn_lb_logits=hgrn_lb_logits, hgrn_norm_g=hgrn_norm_g,
             w_proj_attn=w_proj_attn, w_proj_hgrn=w_proj_hgrn, w_out=w_out, ln3_g=ln3_g, ln3_b=ln3_b,
             ffn2_w1=ffn2_w1, ffn2_w3=ffn2_w3, ffn2_w2=ffn2_w2)
    mom = dict(ln1_g=m_ln1_g, ln1_b=m_ln1_b, ffn1_w1=m_ffn1_w1, ffn1_w3=m_ffn1_w3, ffn1_w2=m_ffn1_w2, ln2_g=m_ln2_g,
               ln2_b=m_ln2_b, w_in=m_w_in, b_in=m_b_in, attn_sinks=m_attn_sinks, hgrn_lb_logits=m_hgrn_lb_logits,
               hgrn_norm_g=m_hgrn_norm_g, w_proj_attn=m_w_proj_attn, w_proj_hgrn=m_w_proj_hgrn, w_out=m_w_out,
               ln3_g=m_ln3_g, ln3_b=m_ln3_b, ffn2_w1=m_ffn2_w1, ffn2_w3=m_ffn2_w3, ffn2_w2=m_ffn2_w2)
    var = dict(ln1_g=v_ln1_g, ln1_b=v_ln1_b, ffn1_w1=v_ffn1_w1, ffn1_w3=v_ffn1_w3, ffn1_w2=v_ffn1_w2, ln2_g=v_ln2_g,
               ln2_b=v_ln2_b, w_in=v_w_in, b_in=v_b_in, attn_sinks=v_attn_sinks, hgrn_lb_logits=v_hgrn_lb_logits,
               hgrn_norm_g=v_hgrn_norm_g, w_proj_attn=v_w_proj_attn, w_proj_hgrn=v_w_proj_hgrn, w_out=v_w_out,
               ln3_g=v_ln3_g, ln3_b=v_ln3_b, ffn2_w1=v_ffn2_w1, ffn2_w3=v_ffn2_w3, ffn2_w2=v_ffn2_w2)

    n_tok = x.shape[1]
    x0 = x.reshape(n_tok, D_MODEL)
    target = loss_target.reshape(n_tok, D_MODEL)

    shard = {k: _bf(_local_view(k, w[k])) for k in _BIG}
    gather = lambda keys: _GatherWeights([shard[k] for k in keys])
    slots = {}
    exchange = lambda keys: _ExchangeGrads([big[k] for k in keys])
    ffn1_keys = ("ffn1_w1", "ffn1_w3", "ffn1_w2")
    mixer_keys = ("w_in", "w_proj_attn", "w_proj_hgrn", "w_out")
    ffn2_keys = ("ffn2_w1", "ffn2_w3", "ffn2_w2")
    whole = lambda ts: [t.reshape(D_FF, D_MODEL) for t in ts]
    f1 = whole(_run_comm("gather_ffn1", gather(ffn1_keys)))

    tabs = _rope_tables(n_tok)
    lb, lb_of = _lb_fwd(hgrn_lb_logits)
    (z1, x1, x0b, h1_1, h3_1, x1b), (w_in_g, w_pa, w_ph, w_o) = _ffn_fwd(
        "ffn1_fwd", x0, *f1, ln1_g, ln1_b, comm=gather(mixer_keys))
    w_pa, w_ph, w_o = (t.reshape(D_MODEL, D_MODEL) for t in (w_pa, w_ph, w_o))
    w_in_g = _to_kernel_cols(jnp.swapaxes(w_in_g, 0, 1).reshape(D_MODEL, D_IN))
    (proj,), f2 = _in_proj(x1b, w_in_g, _to_kernel_cols(b_in), comm=gather(ffn2_keys))
    f2 = whole(f2)
    y_attn = _attn_fwd(proj, tabs, attn_sinks)
    y_hgrn, states = _hgrn_fwd(proj, lb, hgrn_norm_g)
    z2, x2 = _mix_fwd(y_attn, y_hgrn, proj, x1, w_pa, w_ph, w_o, ln2_g, ln2_b)
    (z3, dy, x2b, h1_2, h3_2, loss_part), _ = _ffn_fwd("ffn2_fwd", x2, *f2, ln3_g, ln3_b, target=target)

    big = {}
    small = {}
    (dx2, a2, dh1_2, dh3_2, df2, small["ln3_g"], small["ln3_b"]), _ = _ffn_bwd(
        "ffn2_bwd", h1_2, h3_2, z3, dy, *f2, ln3_g, ln3_b)
    big["ffn2_w1"], _ = _ffn_grad("ffn2_dw1", dh1_2, x2b)
    big["ffn2_w3"], _ = _ffn_grad("ffn2_dw3", dh3_2, x2b)
    big["ffn2_w2"], _ = _ffn_grad("ffn2_dw2", a2, df2)
    (dz2, dz2b, merged, dya, dyh, dy_attn, dy_hgrn, dproj, small["ln2_g"], small["ln2_b"]) = _mix_bwd(
        dx2, z2, y_attn, y_hgrn, proj, w_pa, w_ph, w_o, ln2_g, ln2_b)
    for key, name, lhs, rhs in (("w_out", "dw_out", merged, dz2b), ("w_proj_attn", "dw_proj_attn", y_attn, dya),
                                ("w_proj_hgrn", "dw_proj_hgrn", y_hgrn, dyh)):
        (dw,), _ = _grad_matmul(name, lhs, "cols", D_MODEL, rhs, "shared", D_MODEL, parts=1)
        big[key] = dw.reshape(N_CHIPS, PROJ_SHARD, D_MODEL)
    dproj, dlb, small["hgrn_norm_g"] = _hgrn_bwd(proj, lb, hgrn_norm_g, states, dy_hgrn, dproj)
    early_keys = ffn2_keys + mixer_keys[1:]
    (dproj, dsinks), early = _attn_bwd(proj, dy_attn, tabs, attn_sinks, dproj, comm=exchange(early_keys))
    slots.update(zip(early_keys, early))
    (dw_in, db_in), _ = _grad_matmul("dw_in", x1b, "shared", D_MODEL, dproj, "cols", IN_SHARD, colsum=True)
    dw_in = _from_kernel_cols(jnp.swapaxes(dw_in, 0, 1).reshape(D_MODEL, D_IN))
    big["w_in"] = jnp.swapaxes(dw_in.reshape(D_MODEL, N_CHIPS, IN_SHARD), 0, 1)
    small["b_in"] = _from_kernel_cols(db_in)
    (dx1,), (slots["w_in"],) = _in_proj_dx(dproj, w_in_g, dz2, comm=exchange(("w_in",)))
    (grad_x, a1, dh1_1, dh3_1, df1, small["ln1_g"], small["ln1_b"]), _ = _ffn_bwd(
        "ffn1_bwd", h1_1, h3_1, z1, dx1, *f1, ln1_g, ln1_b)
    big["ffn1_w1"], _ = _ffn_grad("ffn1_dw1", dh1_1, x0b)
    big["ffn1_w3"], (slots["ffn1_w1"],) = _ffn_grad("ffn1_dw3", dh3_1, x0b, comm=exchange(("ffn1_w1",)))
    big["ffn1_w2"], (slots["ffn1_w3"],) = _ffn_grad("ffn1_dw2", a1, df1, comm=exchange(("ffn1_w3",)))
    (slots["ffn1_w2"],) = _run_comm("exchange_last", exchange(("ffn1_w2",)))

    local = {k: _local_view(k, w[k]) for k in _BIG}
    partial = [_sum_slots("sum_" + k, slots[k]) for k in _BIG]
    from_sibling = _swap_with_sibling(partial)

    outs = {"grad": {}, "delta": {}, "m": {}, "v": {}}
    for k, p, q in zip(_BIG, partial, from_sibling):
        res = _adam_big("adam_" + k, p, q, local[k], _local_view(k, mom[k]), _local_view(k, var[k]))
        for kind, r in zip(("grad", "delta", "m", "v"), res):
            outs[kind][k] = (r.T if k in _TRANSPOSED else r).reshape(w[k].shape)

    total = _sum_small(jnp.concatenate(
        [small[k] for k in ("ln1_g", "ln1_b", "ln2_g", "ln2_b", "ln3_g", "ln3_b", "b_in")]
        + [dsinks, small["hgrn_norm_g"], dlb, loss_part], axis=1))
    for kind, r in zip(("grad", "delta", "m", "v"), _adam_small(total, w, mom, var, lb_of)):
        outs[kind].update(r)
    loss = total[0, SM_LOSS]

    return (loss, grad_x.reshape(x.shape), *[outs["grad"][k] for k in _ORDER], *[outs["delta"][k] for k in _ORDER],
            *[outs["m"][k] for k in _ORDER], *[outs["v"][k] for k in _ORDER])
```

```python
import functools

import jax
import jax.numpy as jnp
from jax import lax
from jax.experimental import pallas as pl
from jax.experimental.pallas import tpu as pltpu

F32 = jnp.float32
BF16 = jnp.bfloat16

D_MODEL = 1024
N_Q_HEADS = 16
N_KV_HEADS = 4
HEAD_DIM = 64
ATTN_BLOCK = 128
ROPE_THETA = 500000.0
ROPE_DIM = HEAD_DIM // 4
HGRN_HEADS = 8
HGRN_DK = 128
HGRN_CHUNK = 64
D_FF = 2816
D_IN = 7680
DEEPNORM_ALPHA = 2 ** 0.25
LN_EPS = 1e-5
RMS_EPS = 1e-6
NEG_INF = -1e30

ADAM_LR = 0.001
ADAM_B1 = 0.9
ADAM_B2 = 0.999
ADAM_EPS = 1e-08
ADAM_WD = 0.01
ADAM_STEP = 10

N_CHIPS = 4
N_DEV = 8
LANES = 128
FF_GRAD_PARTS = 2
FFN_TILE = 256
IN_SHARD = D_IN // N_CHIPS
PROJ_SHARD = D_MODEL // N_CHIPS
ROW_TILE = 512
GRAD_ROWS = 2048
MIX_TILE = 256
UPDATE_ROWS = 128
HGRN_CHUNKS_PER_STEP = 16
VMEM_LIMIT = 56 * 1024 * 1024

GATES_WIDTH = 2 * D_MODEL
HGRN_HEAD_WIDTH = 4 * HGRN_DK
ATTN_WIDTH = D_MODEL + 2 * N_KV_HEADS * HEAD_DIM
COL_HGRN = GATES_WIDTH // HGRN_HEAD_WIDTH
COL_ATTN = (GATES_WIDTH + HGRN_HEADS * HGRN_HEAD_WIDTH) // ATTN_WIDTH
COL_Q = (GATES_WIDTH + HGRN_HEADS * HGRN_HEAD_WIDTH) // D_MODEL
COL_K = (GATES_WIDTH + HGRN_HEADS * HGRN_HEAD_WIDTH + D_MODEL) // (N_KV_HEADS * HEAD_DIM)
COL_V = COL_K + 1


def _to_kernel_cols(a):
    lead = a.shape[:-1]
    qkv, hg, gates = a[..., :ATTN_WIDTH], a[..., ATTN_WIDTH:D_IN - GATES_WIDTH], a[..., D_IN - GATES_WIDTH:]
    hg = jnp.swapaxes(hg.reshape(*lead, 4, HGRN_HEADS, HGRN_DK), -3, -2).reshape(*lead, -1)
    return jnp.concatenate([gates, hg, qkv], axis=-1)


def _from_kernel_cols(a):
    lead = a.shape[:-1]
    gates, hg, qkv = a[..., :GATES_WIDTH], a[..., GATES_WIDTH:D_IN - ATTN_WIDTH], a[..., D_IN - ATTN_WIDTH:]
    hg = jnp.swapaxes(hg.reshape(*lead, HGRN_HEADS, 4, HGRN_DK), -3, -2).reshape(*lead, -1)
    return jnp.concatenate([qkv, hg, gates], axis=-1)

SM_LN = 0
SM_BIN = 6 * D_MODEL
SM_SINK = SM_BIN + D_IN
SM_NG = SM_SINK + LANES
SM_LB = SM_NG + LANES
SM_LOSS = SM_LB + D_MODEL
SM_LEN = SM_LOSS + LANES

MESH = pl.DeviceIdType.MESH


def _mm(a, b):
    return lax.dot_general(a, b, (((1,), (0,)), ((), ())), preferred_element_type=F32)


def _mm_nt(a, b):
    return lax.dot_general(a, b, (((1,), (1,)), ((), ())), preferred_element_type=F32)


def _mm_tn(a, b):
    return lax.dot_general(a, b, (((0,), (0,)), ((), ())), preferred_element_type=F32)


def _bf(v):
    return v.astype(BF16)


def _sig(v):
    return jax.nn.sigmoid(v)


def _ln(z, g, b):
    mu = jnp.mean(z, axis=-1, keepdims=True)
    zc = z - mu
    var = jnp.mean(zc * zc, axis=-1, keepdims=True)
    return zc * lax.rsqrt(var + LN_EPS) * g + b


def _swiglu_act(h1, h3):
    return (h1 * _sig(h1)) * h3


def _params(sem=None):
    return pltpu.CompilerParams(dimension_semantics=sem, vmem_limit_bytes=VMEM_LIMIT)


def _full(shape):
    nd = len(shape)
    return pl.BlockSpec(shape, lambda *_: (0,) * nd)


def _update_rows(rows):
    return max(t for t in range(8, UPDATE_ROWS + 1, 8) if rows % t == 0)


def _resident(shape):
    nd = len(shape)
    return pl.BlockSpec(shape, lambda *_: (0,) * nd, pipeline_mode=pl.Buffered(1))


def _ffn_fwd(name, x, w1t, w3t, w2, g, b, target=None, comm=None):
    n = x.shape[0]
    tm = min(FFN_TILE, n)
    final = target is not None

    def body(*refs):
        if final:
            x_ref, w1_ref, w3_ref, w2_ref, g_ref, b_ref, t_ref, z_ref, o_ref, xb_ref, h1_ref, h3_ref, loss_ref = refs
        else:
            x_ref, w1_ref, w3_ref, w2_ref, g_ref, b_ref, z_ref, o_ref, xb_ref, h1_ref, h3_ref, ob_ref = refs
        xb = _bf(x_ref[...])
        xb_ref[...] = xb
        h1 = _mm_nt(xb, w1_ref[...])
        h3 = _mm_nt(xb, w3_ref[...])
        h1_ref[...] = _bf(h1)
        h3_ref[...] = _bf(h3)
        z = DEEPNORM_ALPHA * x_ref[...] + 0.5 * _mm(_bf(_swiglu_act(h1, h3)), w2_ref[...])
        z_ref[...] = z
        y = _ln(z, g_ref[...], b_ref[...])
        if final:
            e = y - t_ref[...]

            @pl.when(pl.program_id(0) == 0)
            def _():
                loss_ref[...] = jnp.zeros_like(loss_ref)

            loss_ref[...] += jnp.sum(e * e) * (0.5 / D_MODEL)
            o_ref[...] = e * (1.0 / D_MODEL)
        else:
            o_ref[...] = y
            ob_ref[...] = _bf(y)

    row = pl.BlockSpec((tm, D_MODEL), lambda i: (i, 0))
    hid = pl.BlockSpec((tm, D_FF), lambda i: (i, 0))
    wres = _resident((D_FF, D_MODEL))
    vec = _full((1, D_MODEL))
    in_specs = [row, wres, wres, wres, vec, vec]
    args = [x, w1t, w3t, w2, g, b]
    hid_shape = jax.ShapeDtypeStruct((n, D_FF), BF16)
    out_specs = [row, row, row, hid, hid]
    out_shape = ([jax.ShapeDtypeStruct((n, D_MODEL), F32)] * 2 + [jax.ShapeDtypeStruct((n, D_MODEL), BF16)]
                 + [hid_shape] * 2)
    if final:
        in_specs.append(row)
        args.append(target)
        out_specs.append(_full((1, LANES)))
        out_shape.append(jax.ShapeDtypeStruct((1, LANES), F32))
    else:
        out_specs.append(row)
        out_shape.append(jax.ShapeDtypeStruct((n, D_MODEL), BF16))
    return _hosted(
        body, comm, name=name, grid=(n // tm,), in_specs=in_specs, out_specs=out_specs, out_shape=out_shape,
        scratch_shapes=[], compiler_params=_params(("arbitrary",)), args=args)


def _ffn_bwd(name, h1s, h3s, z, dout, w1t, w3t, w2, g, b, comm=None):
    n = z.shape[0]
    tm = min(FFN_TILE, n)

    def body(h1_ref, h3_ref, z_ref, do_ref, w1_ref, w3_ref, w2_ref, g_ref, b_ref,
             dx_ref, a_ref, dh1_ref, dh3_ref, df_ref, dg_ref, db_ref):
        _, vjp = jax.vjp(_ln, z_ref[...], g_ref[...], b_ref[...])
        dz, dg, db = vjp(do_ref[...])

        @pl.when(pl.program_id(0) == 0)
        def _():
            dg_ref[...] = jnp.zeros_like(dg_ref)
            db_ref[...] = jnp.zeros_like(db_ref)

        dg_ref[...] += dg
        db_ref[...] += db
        df = _bf(0.5 * dz)
        df_ref[...] = df
        a, act_vjp = jax.vjp(_swiglu_act, h1_ref[...].astype(F32), h3_ref[...].astype(F32))
        dh1, dh3 = act_vjp(_mm_nt(df, w2_ref[...]))
        dh1 = _bf(dh1)
        dh3 = _bf(dh3)
        a_ref[...] = _bf(a)
        dh1_ref[...] = dh1
        dh3_ref[...] = dh3
        dx_ref[...] = DEEPNORM_ALPHA * dz + _mm(dh1, w1_ref[...]) + _mm(dh3, w3_ref[...])

    row = pl.BlockSpec((tm, D_MODEL), lambda i: (i, 0))
    hid = pl.BlockSpec((tm, D_FF), lambda i: (i, 0))
    wres = _resident((D_FF, D_MODEL))
    vec = _full((1, D_MODEL))
    hid_shape = jax.ShapeDtypeStruct((n, D_FF), BF16)
    return _hosted(
        body, comm, name=name, grid=(n // tm,),
        in_specs=[hid, hid, row, row, wres, wres, wres, vec, vec],
        out_specs=[row, hid, hid, hid, row, vec, vec],
        out_shape=[jax.ShapeDtypeStruct((n, D_MODEL), F32), hid_shape, hid_shape, hid_shape,
                   jax.ShapeDtypeStruct((n, D_MODEL), BF16),
                   jax.ShapeDtypeStruct((1, D_MODEL), F32), jax.ShapeDtypeStruct((1, D_MODEL), F32)],
        scratch_shapes=[], compiler_params=_params(("arbitrary",)),
        args=[h1s, h3s, z, dout, w1t, w3t, w2, g, b])


def _operand_spec(arr, mode, tn, width, parts):
    if mode == "shared":
        return pl.BlockSpec((tn, width), lambda s, k: (k, 0))
    assert mode == "cols" and arr.shape[1] == parts * width
    return pl.BlockSpec((tn, width), lambda s, k: (k, s))


def _grad_matmul(name, a, a_mode, ka, b, b_mode, kb, colsum=False, comm=None, parts=N_CHIPS):
    n = a.shape[-2]
    tn = min(GRAD_ROWS, n)
    nk = n // tn

    def body(*refs):
        if colsum:
            a_ref, b_ref, o_ref, cs_ref, acc = refs
        else:
            a_ref, b_ref, o_ref, acc = refs
        k = pl.program_id(1)
        av = a_ref[...]
        bv = b_ref[...]

        @pl.when(k == 0)
        def _():
            acc[...] = jnp.zeros_like(acc)
            if colsum:
                cs_ref[...] = jnp.zeros_like(cs_ref)

        acc[...] += _mm_tn(av, bv)
        if colsum:
            cs_ref[...] += jnp.sum(bv.astype(F32), axis=0, keepdims=True)

        @pl.when(k == nk - 1)
        def _():
            o_ref[0] = _bf(acc[...])

    out_specs = [pl.BlockSpec((1, ka, kb), lambda s, k: (s, 0, 0))]
    out_shape = [jax.ShapeDtypeStruct((parts, ka, kb), BF16)]
    if colsum:
        out_specs.append(pl.BlockSpec((1, kb), lambda s, k: (0, s)))
        out_shape.append(jax.ShapeDtypeStruct((1, parts * kb), F32))
    return _hosted(
        body, comm, name=name, grid=(parts, nk),
        in_specs=[_operand_spec(a, a_mode, tn, ka, parts), _operand_spec(b, b_mode, tn, kb, parts)],
        out_specs=out_specs, out_shape=out_shape,
        scratch_shapes=[pltpu.VMEM((ka, kb), F32)],
        compiler_params=_params(("arbitrary", "arbitrary")), args=[a, b])


def _kernel_block_of(ref_block):
    attn_blocks, gate_blocks = ATTN_WIDTH // LANES, GATES_WIDTH // LANES
    hgrn_blocks = HGRN_HEADS * HGRN_HEAD_WIDTH // LANES
    if ref_block < attn_blocks:
        return gate_blocks + hgrn_blocks + ref_block
    if ref_block < attn_blocks + hgrn_blocks:
        kind, head = divmod(ref_block - attn_blocks, HGRN_HEADS)
        return gate_blocks + head * (HGRN_HEAD_WIDTH // LANES) + kind
    return ref_block - attn_blocks - hgrn_blocks


def _reorder_w_in(name, w4, to_kernel_order):
    per = IN_SHARD // LANES
    tr = MIX_TILE

    def body(i_ref, o_ref):
        for g in range(D_IN // LANES):
            s, b = divmod(g, per)
            k = _kernel_block_of(g)
            if to_kernel_order:
                o_ref[:, k * LANES:(k + 1) * LANES] = i_ref[s, :, b * LANES:(b + 1) * LANES]
            else:
                ks, kb = divmod(k, per)
                o_ref[s, :, b * LANES:(b + 1) * LANES] = i_ref[ks, :, kb * LANES:(kb + 1) * LANES]

    in_spec = pl.BlockSpec((N_CHIPS, tr, IN_SHARD), lambda i: (0, i, 0))
    if to_kernel_order:
        out_spec, out_shape = pl.BlockSpec((tr, D_IN), lambda i: (i, 0)), (D_MODEL, D_IN)
    else:
        out_spec, out_shape = in_spec, (N_CHIPS, D_MODEL, IN_SHARD)
    return pl.pallas_call(
        body, name=name, grid=(D_MODEL // tr,), in_specs=[in_spec], out_specs=out_spec,
        out_shape=jax.ShapeDtypeStruct(out_shape, w4.dtype), compiler_params=_params(("arbitrary",)),
    )(w4)


def _in_proj(x1, w_in_g, b_in, comm=None):
    n = x1.shape[0]
    tm = min(FFN_TILE, n)

    def body(x_ref, w_ref, b_ref, o_ref):
        xv = x_ref[...]
        for j in range(N_CHIPS):
            cols = slice(j * IN_SHARD, (j + 1) * IN_SHARD)
            o_ref[:, cols] = _mm(xv, w_ref[:, cols]) + b_ref[:, cols]

    return _hosted(
        body, comm, name="in_proj", grid=(n // tm,),
        in_specs=[pl.BlockSpec((tm, D_MODEL), lambda i: (i, 0)),
                  _resident((D_MODEL, D_IN)), _full((1, D_IN))],
        out_specs=[pl.BlockSpec((tm, D_IN), lambda i: (i, 0))],
        out_shape=[jax.ShapeDtypeStruct((n, D_IN), F32)],
        scratch_shapes=[],
        compiler_params=_params(("arbitrary",)), args=[x1, w_in_g, b_in])


def _in_proj_dx(dproj, w_in_g, dz2, comm=None):
    n = dproj.shape[0]
    tm = min(FFN_TILE, n)

    def body(dp_ref, w_ref, dz_ref, o_ref):
        dx = DEEPNORM_ALPHA * dz_ref[...]
        for j in range(N_CHIPS):
            cols = slice(j * IN_SHARD, (j + 1) * IN_SHARD)
            dx = dx + _mm_nt(dp_ref[:, cols], w_ref[:, cols])
        o_ref[...] = dx

    return _hosted(
        body, comm, name="in_proj_dx", grid=(n // tm,),
        in_specs=[pl.BlockSpec((tm, D_IN), lambda i: (i, 0)),
                  _resident((D_MODEL, D_IN)),
                  pl.BlockSpec((tm, D_MODEL), lambda i: (i, 0))],
        out_specs=[pl.BlockSpec((tm, D_MODEL), lambda i: (i, 0))],
        out_shape=[jax.ShapeDtypeStruct((n, D_MODEL), F32)],
        scratch_shapes=[],
        compiler_params=_params(("arbitrary",)), args=[dproj, w_in_g, dz2])


def _rope_tables(seq_len):
    pos = jnp.arange(seq_len, dtype=F32)
    inv_freq = ROPE_THETA ** (-jnp.arange(0, ROPE_DIM, 2, dtype=F32) / ROPE_DIM)
    ang = pos[:, None] * inv_freq[None, :]
    cos, sin = jnp.cos(ang), jnp.sin(ang)
    half = ROPE_DIM // 2
    rest = HEAD_DIM - ROPE_DIM
    ones = jnp.ones((seq_len, rest), F32)
    zeros = jnp.zeros((seq_len, rest), F32)
    zh = jnp.zeros((seq_len, half), F32)
    c = jnp.concatenate([cos, cos, ones], axis=1)
    sa = jnp.concatenate([-sin, zh, zeros], axis=1)
    sb = jnp.concatenate([zh, sin, zeros], axis=1)
    reps = LANES // HEAD_DIM
    return tuple(jnp.tile(t, (1, reps)) for t in (c, sa, sb))


def _rope(t, c, sa, sb):
    w = t.shape[1]
    reps = w // LANES
    half = ROPE_DIM // 2
    return (t * jnp.tile(c, (1, reps)) + pltpu.roll(t, w - half, 1) * jnp.tile(sa, (1, reps))
            + pltpu.roll(t, half, 1) * jnp.tile(sb, (1, reps)))


def _rope_transposed(g, c, sa, sb):
    w = g.shape[1]
    reps = w // LANES
    half = ROPE_DIM // 2
    return (g * jnp.tile(c, (1, reps)) + pltpu.roll(g * jnp.tile(sa, (1, reps)), half, 1)
            + pltpu.roll(g * jnp.tile(sb, (1, reps)), w - half, 1))


GROUP = N_Q_HEADS // N_KV_HEADS


def _both_halves(t_pair, which):
    lo = lax.broadcasted_iota(jnp.int32, t_pair.shape, 1) < HEAD_DIM
    swapped = pltpu.roll(t_pair, HEAD_DIM, 1)
    return _bf(jnp.where(lo, t_pair, swapped) if which == 0 else jnp.where(lo, swapped, t_pair))


def _stack_heads(ref_or_val, kh):
    lo = lax.broadcasted_iota(jnp.int32, (ATTN_BLOCK, LANES), 1) < HEAD_DIM
    rows = []
    for gp in range(GROUP // 2):
        pair = kh * (GROUP // 2) + gp
        t = ref_or_val[:, pair * LANES:(pair + 1) * LANES]
        rows += [jnp.where(lo, t, jnp.zeros_like(t)), jnp.where(lo, jnp.zeros_like(t), t)]
    return jnp.concatenate(rows, axis=0)


def _unstack_pairs(stacked):
    lo = lax.broadcasted_iota(jnp.int32, (ATTN_BLOCK, LANES), 1) < HEAD_DIM
    b = ATTN_BLOCK
    return [jnp.where(lo, stacked[2 * gp * b:(2 * gp + 1) * b], stacked[(2 * gp + 1) * b:(2 * gp + 2) * b])
            for gp in range(GROUP // 2)]


def _attn_mask_t(n):
    cols = GROUP * ATTN_BLOCK
    kj = lax.broadcasted_iota(jnp.int32, (2 * ATTN_BLOCK, cols), 0)
    qi = lax.broadcasted_iota(jnp.int32, (2 * ATTN_BLOCK, cols), 1) % ATTN_BLOCK
    dist = qi + ATTN_BLOCK - kj
    return (dist >= 0) & (dist < ATTN_BLOCK) & (n * ATTN_BLOCK + kj - ATTN_BLOCK >= 0)


def _sink_row(sink_ref, kh):
    col = lax.broadcasted_iota(jnp.int32, (1, GROUP * ATTN_BLOCK), 1)
    row = jnp.full((1, GROUP * ATTN_BLOCK), sink_ref[0, kh * GROUP + GROUP - 1], F32)
    for i in reversed(range(GROUP - 1)):
        row = jnp.where(col < (i + 1) * ATTN_BLOCK, sink_ref[0, kh * GROUP + i], row)
    return row


def _attn_probs_t(q_masked, k_sel, mask_t, sink):
    s = _mm_nt(k_sel, q_masked) * (HEAD_DIM ** -0.5)
    s = jnp.where(mask_t, s, NEG_INF)
    m = jnp.maximum(jnp.max(s, axis=0, keepdims=True), sink)
    p = jnp.exp(s - m)
    e_sink = jnp.exp(sink - m)
    denom = jnp.sum(p, axis=0, keepdims=True) + e_sink
    return p / denom, e_sink / denom


def _attn_fwd(proj, tabs, sinks):
    n_tok = proj.shape[0]
    nb = n_tok // ATTN_BLOCK

    def body(q_ref, k_ref, v_ref, c_ref, sa_ref, sb_ref, sink_ref, y_ref, kprev, vprev):
        n = pl.program_id(0)

        @pl.when(n == 0)
        def _():
            kprev[...] = jnp.zeros_like(kprev)
            vprev[...] = jnp.zeros_like(vprev)

        c, sa, sb = c_ref[...], sa_ref[...], sb_ref[...]
        qr = _bf(_rope(q_ref[...], c, sa, sb))
        kr = _rope(k_ref[...], c, sa, sb)
        vc = v_ref[...]
        kk = jnp.concatenate([kprev[...], kr], axis=0)
        vv = jnp.concatenate([vprev[...], vc], axis=0)
        kprev[...] = kr
        vprev[...] = vc
        mask = _attn_mask_t(n)
        for kh in range(N_KV_HEADS):
            r, which = divmod(kh, 2)
            kb = _both_halves(kk[:, r * LANES:(r + 1) * LANES], which)
            vb = _both_halves(vv[:, r * LANES:(r + 1) * LANES], which)
            probs, _ = _attn_probs_t(_stack_heads(qr, kh), kb, mask, _sink_row(sink_ref, kh))
            for gp, out in enumerate(_unstack_pairs(_mm_tn(_bf(probs), vb))):
                pair = kh * (GROUP // 2) + gp
                y_ref[:, pair * LANES:(pair + 1) * LANES] = _bf(out)

    blk = lambda width, col: pl.BlockSpec((ATTN_BLOCK, width), lambda n: (n, col))
    tab = pl.BlockSpec((ATTN_BLOCK, LANES), lambda n: (n, 0))
    kvw = N_KV_HEADS * HEAD_DIM
    return pl.pallas_call(
        body, name="attn_fwd", grid=(nb,),
        in_specs=[blk(D_MODEL, COL_Q), blk(kvw, COL_K), blk(kvw, COL_V), tab, tab, tab,
                  pl.BlockSpec(memory_space=pltpu.SMEM)],
        out_specs=pl.BlockSpec((ATTN_BLOCK, D_MODEL), lambda n: (n, 0)),
        out_shape=jax.ShapeDtypeStruct((n_tok, D_MODEL), BF16),
        scratch_shapes=[pltpu.VMEM((ATTN_BLOCK, kvw), F32), pltpu.VMEM((ATTN_BLOCK, kvw), F32)],
        compiler_params=_params(("arbitrary",)),
    )(proj, proj, proj, *tabs, sinks)


def _attn_bwd(proj, dy, tabs, sinks, dproj, comm=None):
    n_tok = proj.shape[0]
    nb = n_tok // ATTN_BLOCK
    kvw = N_KV_HEADS * HEAD_DIM

    def body(q_ref, k_ref, v_ref, do_ref, c_ref, sa_ref, sb_ref, cp_ref, sap_ref, sbp_ref, sink_ref, _,
             dqkv_ref, dsink_ref, kprev, vprev, dkc, dvc, dqc):
        n = pl.program_id(0)

        @pl.when(n == 0)
        def _():
            for ref in (kprev, vprev, dkc, dvc, dqc, dsink_ref):
                ref[...] = jnp.zeros_like(ref)

        prev_tabs = (cp_ref[...], sap_ref[...], sbp_ref[...])

        @pl.when(n < nb)
        def _():
            c, sa, sb = c_ref[...], sa_ref[...], sb_ref[...]
            qr = _bf(_rope(q_ref[...], c, sa, sb))
            kr = _rope(k_ref[...], c, sa, sb)
            vc = v_ref[...]
            kk = jnp.concatenate([kprev[...], kr], axis=0)
            vv = jnp.concatenate([vprev[...], vc], axis=0)
            kprev[...] = kr
            vprev[...] = vc
            mask = _attn_mask_t(n)
            lane = lax.broadcasted_iota(jnp.int32, (1, LANES), 1)
            lo2 = lax.broadcasted_iota(jnp.int32, (2 * ATTN_BLOCK, LANES), 1) < HEAD_DIM
            dsink = jnp.zeros((1, LANES), F32)
            dq_pairs = []
            dk_full = []
            dv_full = []
            for kh in range(N_KV_HEADS):
                r, which = divmod(kh, 2)
                kb = _both_halves(kk[:, r * LANES:(r + 1) * LANES], which)
                vb = _both_halves(vv[:, r * LANES:(r + 1) * LANES], which)
                qs = _stack_heads(qr, kh)
                dos = _bf(_stack_heads(do_ref, kh))
                probs, p_sink = _attn_probs_t(qs, kb, mask, _sink_row(sink_ref, kh))
                dp = _mm_nt(vb, dos)
                delta = jnp.sum(probs * dp, axis=0, keepdims=True)
                ds = _bf(probs * (dp - delta) * (HEAD_DIM ** -0.5))
                sink_terms = p_sink * delta
                for i in range(GROUP):
                    head_sum = jnp.sum(sink_terms[:, i * ATTN_BLOCK:(i + 1) * ATTN_BLOCK])
                    dsink = dsink + jnp.where(lane == kh * GROUP + i, -head_sum, 0.0)
                dq_pairs += _unstack_pairs(_mm_tn(ds, kb))
                dk_acc = _mm(ds, qs)
                dv_acc = _mm(_bf(probs), dos)
                dk_full.append(dk_acc + pltpu.roll(dk_acc, HEAD_DIM, 1))
                dv_full.append(dv_acc + pltpu.roll(dv_acc, HEAD_DIM, 1))
            dk_pairs = [jnp.where(lo2, dk_full[2 * r], dk_full[2 * r + 1]) for r in range(N_KV_HEADS // 2)]
            dv_pairs = [jnp.where(lo2, dv_full[2 * r], dv_full[2 * r + 1]) for r in range(N_KV_HEADS // 2)]
            dsink_ref[...] += dsink
            dqkv_ref[:, :D_MODEL] = _bf(dqc[...])
            dqc[...] = _rope_transposed(jnp.concatenate(dq_pairs, axis=1), c, sa, sb)
            dk_all = jnp.concatenate(dk_pairs, axis=1)
            dv_all = jnp.concatenate(dv_pairs, axis=1)
            dqkv_ref[:, D_MODEL:D_MODEL + kvw] = _bf(_rope_transposed(dkc[...] + dk_all[:ATTN_BLOCK], *prev_tabs))
            dqkv_ref[:, D_MODEL + kvw:] = _bf(dvc[...] + dv_all[:ATTN_BLOCK])
            dkc[...] = dk_all[ATTN_BLOCK:]
            dvc[...] = dv_all[ATTN_BLOCK:]

        @pl.when(n == nb)
        def _():
            dqkv_ref[:, :D_MODEL] = _bf(dqc[...])
            dqkv_ref[:, D_MODEL:D_MODEL + kvw] = _bf(_rope_transposed(dkc[...], *prev_tabs))
            dqkv_ref[:, D_MODEL + kvw:] = _bf(dvc[...])

    cur = lambda n: jnp.minimum(n, nb - 1)
    prev = lambda n: jnp.maximum(n - 1, 0)
    blk = lambda width, col: pl.BlockSpec((ATTN_BLOCK, width), lambda n: (cur(n), col))
    tab = pl.BlockSpec((ATTN_BLOCK, LANES), lambda n: (cur(n), 0))
    tabp = pl.BlockSpec((ATTN_BLOCK, LANES), lambda n: (prev(n), 0))
    return _hosted(
        body, comm, name="attn_bwd", grid=(nb + 1,),
        in_specs=[blk(D_MODEL, COL_Q), blk(kvw, COL_K), blk(kvw, COL_V), blk(D_MODEL, 0), tab, tab, tab, tabp, tabp, tabp,
                  pl.BlockSpec(memory_space=pltpu.SMEM), pl.BlockSpec(memory_space=pl.ANY)],
        out_specs=[pl.BlockSpec((ATTN_BLOCK, ATTN_WIDTH), lambda n: (prev(n), COL_ATTN)),
                   pl.BlockSpec((1, LANES), lambda n: (0, 0))],
        out_shape=[jax.ShapeDtypeStruct(dproj.shape, dproj.dtype), jax.ShapeDtypeStruct((1, LANES), F32)],
        scratch_shapes=[pltpu.VMEM((ATTN_BLOCK, kvw), F32)] * 4 + [pltpu.VMEM((ATTN_BLOCK, D_MODEL), F32)],
        compiler_params=_params(("arbitrary",)), args=[proj, proj, proj, dy, *tabs, *tabs, sinks, dproj],
        aliases={11: 0})


def _bmm(a, b):
    return lax.dot_general(a, b, (((2,), (1,)), ((0,), (0,))), preferred_element_type=F32)


def _bmm_nt(a, b):
    return lax.dot_general(a, b, (((2,), (2,)), ((0,), (0,))), preferred_element_type=F32)


def _bmm_tn(a, b):
    return lax.dot_general(a, b, (((1,), (1,)), ((0,), (0,))), preferred_element_type=F32)


def _tril(cb, upper=False):
    shape = (cb, HGRN_CHUNK, HGRN_CHUNK)
    r, c = lax.broadcasted_iota(jnp.int32, shape, 1), lax.broadcasted_iota(jnp.int32, shape, 2)
    return (r <= c) if upper else (r >= c)


def _tri_matmul(x, upper):
    return lax.dot_general(_tril(x.shape[0], upper).astype(F32), x, (((2,), (1,)), ((0,), (0,))),
                           precision=lax.Precision.HIGHEST, preferred_element_type=F32)


@jax.custom_vjp
def _chunk_cumsum(x):
    return _tri_matmul(x, False)


_chunk_cumsum.defvjp(lambda x: (_tri_matmul(x, False), None), lambda _, g: (_tri_matmul(g, True),))


def _hg_elem(fl, qh, lb):
    f = lb + (1.0 - lb) * _sig(fl)
    k = 1.0 - f
    gc = _chunk_cumsum(jnp.log(f))
    last = lax.broadcasted_iota(jnp.int32, gc.shape, 1) == HGRN_CHUNK - 1
    g_last = jnp.sum(jnp.where(last, gc, 0.0), axis=1, keepdims=True)
    q = qh * _sig(qh)
    return q * jnp.exp(gc), k * jnp.exp(-gc), k * jnp.exp(g_last - gc), jnp.exp(g_last)


def _hg_out(q_dec, k_inv, v, st):
    sc = jnp.where(_tril(q_dec.shape[0]), _bmm_nt(_bf(q_dec), _bf(k_inv)), 0.0)
    return _bmm(_bf(sc), _bf(v)) + _bmm_nt(_bf(q_dec), _bf(st)), sc


def _hg_post(o, og, ng):
    on = o * lax.rsqrt(jnp.mean(o * o, axis=-1, keepdims=True) + RMS_EPS) * ng
    return on * (og * _sig(og))


def _hgrn_specs(n_tok, rev):
    nc = n_tok // HGRN_CHUNK
    cb = min(HGRN_CHUNKS_PER_STEP, nc)
    nt = nc // cb
    rows = cb * HGRN_CHUNK
    tt = (lambda t: nt - 1 - t) if rev else (lambda t: t)
    col = lambda base: pl.BlockSpec((rows, LANES), lambda h, t: (tt(t), base + h))
    head_cols = pl.BlockSpec((rows, HGRN_HEAD_WIDTH), lambda h, t: (tt(t), COL_HGRN + h))
    head_vec = pl.BlockSpec((1, LANES), lambda h, t: (0, h))
    one_vec = pl.BlockSpec((1, LANES), lambda h, t: (0, 0))
    state = pl.BlockSpec((1, cb, HGRN_DK, HGRN_DK), lambda h, t: (h, tt(t), 0, 0))
    return nc, cb, nt, col, head_cols, head_vec, one_vec, state


def _hgrn_fwd(proj, lb, ng):
    n_tok = proj.shape[0]
    nc, cb, nt, col, head_cols, head_vec, one_vec, state = _hgrn_specs(n_tok, False)

    def body(in_ref, lb_ref, ng_ref, y_ref, st_ref, s_acc):
        @pl.when(pl.program_id(1) == 0)
        def _():
            s_acc[...] = jnp.zeros_like(s_acc)

        fl, qh, v, og = (in_ref[:, i * LANES:(i + 1) * LANES].reshape(cb, HGRN_CHUNK, LANES) for i in range(4))
        q_dec, k_inv, k_end, decay = _hg_elem(fl, qh, lb_ref[...])
        upd = _bmm_tn(_bf(v), _bf(k_end))
        st = s_acc[...]
        for ci in range(cb):
            st_ref[0, ci] = st
            st = st * decay[ci] + upd[ci]
        s_acc[...] = st
        o, _ = _hg_out(q_dec, k_inv, v, st_ref[0])
        y_ref[...] = _bf(_hg_post(o, og, ng_ref[...]).reshape(cb * HGRN_CHUNK, LANES))

    return pl.pallas_call(
        body, name="hgrn_fwd", grid=(HGRN_HEADS, nt),
        in_specs=[head_cols, head_vec, one_vec],
        out_specs=[col(0), state],
        out_shape=[jax.ShapeDtypeStruct((n_tok, D_MODEL), BF16),
                   jax.ShapeDtypeStruct((HGRN_HEADS, nc, HGRN_DK, HGRN_DK), F32)],
        scratch_shapes=[pltpu.VMEM((HGRN_DK, HGRN_DK), F32)],
        compiler_params=_params(("arbitrary", "arbitrary")),
    )(proj, lb, ng)


def _hgrn_bwd(proj, lb, ng, states, dy, dproj):
    n_tok = proj.shape[0]
    nc, cb, nt, col, head_cols, head_vec, one_vec, state = _hgrn_specs(n_tok, True)

    def body(in_ref, lb_ref, ng_ref, st_ref, dy_ref, _, d_ref, dlb_ref, dng_ref, g_acc, g_all):
        h = pl.program_id(0)
        t = pl.program_id(1)

        @pl.when(t == 0)
        def _():
            g_acc[...] = jnp.zeros_like(g_acc)
            dlb_ref[...] = jnp.zeros_like(dlb_ref)

        @pl.when((t == 0) & (h == 0))
        def _():
            dng_ref[...] = jnp.zeros_like(dng_ref)

        fl, qh, v, og = (in_ref[:, i * LANES:(i + 1) * LANES].reshape(cb, HGRN_CHUNK, LANES) for i in range(4))
        (q_dec, k_inv, k_end, decay), elem_vjp = jax.vjp(_hg_elem, fl, qh, lb_ref[...])
        st = st_ref[0]
        o, sc = _hg_out(q_dec, k_inv, v, st)
        _, post_vjp = jax.vjp(_hg_post, o, og, ng_ref[...])
        do, dog, dng = post_vjp(dy_ref[...].reshape(cb, HGRN_CHUNK, LANES))
        dob, vb, qb = _bf(do), _bf(v), _bf(q_dec)
        dsc = _bf(jnp.where(_tril(cb), _bmm_nt(dob, vb), 0.0))
        p = _bmm_tn(dob, qb)
        g = g_acc[...]
        for ci in reversed(range(cb)):
            g_all[ci] = g
            g = g * decay[ci] + p[ci]
        g_acc[...] = g
        g = g_all[...]
        gb = _bf(g)
        dq_dec = _bmm(dsc, _bf(k_inv)) + _bmm(dob, _bf(st))
        dk_inv = _bmm_tn(dsc, qb)
        dv = _bmm_tn(_bf(sc), dob) + _bmm_nt(_bf(k_end), gb)
        dk_end = _bmm(vb, gb)
        ddecay = jnp.sum(st * g, axis=1, keepdims=True)
        dfl, dqh, dlb = elem_vjp((dq_dec, dk_inv, dk_end, ddecay))
        for i, val in enumerate((dfl, dqh, dv, dog)):
            d_ref[:, i * LANES:(i + 1) * LANES] = _bf(val.reshape(cb * HGRN_CHUNK, LANES))
        dlb_ref[...] += dlb
        dng_ref[...] += dng

    return pl.pallas_call(
        body, name="hgrn_bwd", grid=(HGRN_HEADS, nt),
        in_specs=[head_cols, head_vec, one_vec, state, col(0), pl.BlockSpec(memory_space=pl.ANY)],
        out_specs=[head_cols, head_vec, one_vec],
        out_shape=[jax.ShapeDtypeStruct(dproj.shape, dproj.dtype),
                   jax.ShapeDtypeStruct((1, D_MODEL), F32), jax.ShapeDtypeStruct((1, LANES), F32)],
        scratch_shapes=[pltpu.VMEM((HGRN_DK, HGRN_DK), F32), pltpu.VMEM((cb, HGRN_DK, HGRN_DK), F32)],
        compiler_params=_params(("arbitrary", "arbitrary")), input_output_aliases={5: 0},
    )(proj, lb, ng, states, dy, dproj)


def _lb_fwd(lb_logits):
    def lb_of(l0, l1):
        m = jnp.maximum(l0, l1)
        e0, e1 = jnp.exp(l0 - m), jnp.exp(l1 - m)
        return e0 / (e0 + e1)

    def body(l_ref, o_ref):
        o_ref[...] = lb_of(l_ref[0:1, :], l_ref[1:2, :])

    lb = pl.pallas_call(body, name="lb_fwd", out_shape=jax.ShapeDtypeStruct((1, D_MODEL), F32))(lb_logits)
    return lb, lb_of


def _mix_fwd(y_attn, y_hgrn, proj, x1, w_pa, w_ph, w_out, g, b):
    n = x1.shape[0]
    tm = min(MIX_TILE, n)

    def body(ya_ref, yh_ref, gt_ref, x_ref, wpa, wph, wo, g_ref, b_ref, z_ref, o_ref, pa_ref, ph_ref):
        ya = _mm(ya_ref[...], wpa[...])
        yh = _mm(yh_ref[...], wph[...])
        pa_ref[...] = _bf(ya)
        ph_ref[...] = _bf(yh)
        merged = _sig(gt_ref[:, :D_MODEL]) * ya + _sig(gt_ref[:, D_MODEL:]) * yh
        z = DEEPNORM_ALPHA * x_ref[...] + _mm(_bf(merged), wo[...])
        z_ref[...] = z
        o_ref[...] = _ln(z, g_ref[...], b_ref[...])

    row = pl.BlockSpec((tm, D_MODEL), lambda i: (i, 0))
    gates = pl.BlockSpec((tm, GATES_WIDTH), lambda i: (i, 0))
    sq = _resident((D_MODEL, D_MODEL))
    vec = _full((1, D_MODEL))
    return pl.pallas_call(
        body, name="mix_fwd", grid=(n // tm,),
        in_specs=[row, row, gates, row, sq, sq, sq, vec, vec],
        out_specs=[row, row, row, row],
        out_shape=[jax.ShapeDtypeStruct((n, D_MODEL), F32)] * 2 + [jax.ShapeDtypeStruct((n, D_MODEL), BF16)] * 2,
        compiler_params=_params(("arbitrary",)),
    )(y_attn, y_hgrn, proj, x1, w_pa, w_ph, w_out, g, b)


def _mix_bwd(dx2, z2, pa, ph, proj, w_pa, w_ph, w_out, g, b):
    n = z2.shape[0]
    tm = min(MIX_TILE, n)

    def body(do_ref, z_ref, ya_ref, yh_ref, gt_ref, wpa, wph, wo, g_ref, b_ref,
             dz_ref, dzb_ref, mg_ref, dya_ref, dyh_ref, dyat_ref, dyhg_ref, dgt_ref, dg_ref, db_ref):
        _, vjp = jax.vjp(_ln, z_ref[...], g_ref[...], b_ref[...])
        dz, dg, db = vjp(do_ref[...])

        @pl.when(pl.program_id(0) == 0)
        def _():
            dg_ref[...] = jnp.zeros_like(dg_ref)
            db_ref[...] = jnp.zeros_like(db_ref)

        dg_ref[...] += dg
        db_ref[...] += db
        dz_ref[...] = dz
        ya = ya_ref[...].astype(F32)
        yh = yh_ref[...].astype(F32)
        def merge(ga, gh, ya, yh):
            return _sig(ga) * ya + _sig(gh) * yh

        merged, merge_vjp = jax.vjp(merge, gt_ref[:, :D_MODEL], gt_ref[:, D_MODEL:], ya, yh)
        mg_ref[...] = _bf(merged)
        dzb = _bf(dz)
        dzb_ref[...] = dzb
        dmerged = _mm_nt(dzb, wo[...])
        dga, dgh, dya, dyh = merge_vjp(dmerged)
        dya = _bf(dya)
        dyh = _bf(dyh)
        dya_ref[...] = dya
        dyh_ref[...] = dyh
        dgt_ref[:, :D_MODEL] = _bf(dga)
        dgt_ref[:, D_MODEL:] = _bf(dgh)
        dyat_ref[...] = _mm_nt(dya, wpa[...])
        dyhg_ref[...] = _mm_nt(dyh, wph[...])

    row = pl.BlockSpec((tm, D_MODEL), lambda i: (i, 0))
    gates = pl.BlockSpec((tm, GATES_WIDTH), lambda i: (i, 0))
    sq = _resident((D_MODEL, D_MODEL))
    vec = _full((1, D_MODEL))
    f32_row = jax.ShapeDtypeStruct((n, D_MODEL), F32)
    bf_row = jax.ShapeDtypeStruct((n, D_MODEL), BF16)
    vec_shape = jax.ShapeDtypeStruct((1, D_MODEL), F32)
    return pl.pallas_call(
        body, name="mix_bwd", grid=(n // tm,),
        in_specs=[row, row, row, row, gates, sq, sq, sq, vec, vec],
        out_specs=[row, row, row, row, row, row, row, gates, vec, vec],
        out_shape=[f32_row, bf_row, bf_row, bf_row, bf_row, f32_row, f32_row,
                   jax.ShapeDtypeStruct((n, D_IN), BF16), vec_shape, vec_shape],
        compiler_params=_params(("arbitrary",)),
    )(dx2, z2, pa, ph, proj, w_pa, w_ph, w_out, g, b)


def _position():
    x, y, c = lax.axis_index("x"), lax.axis_index("y"), lax.axis_index("c")
    chips = [(1 - x, y), (x, 1 - y), (1 - x, 1 - y)]
    return x, y, c, chips


def _any_specs(k):
    return [pl.BlockSpec(memory_space=pl.ANY)] * k


class _GatherWeights:
    def __init__(self, shards):
        nw = len(shards)
        self.inputs = list(shards)
        self.out_shape = [jax.ShapeDtypeStruct((N_CHIPS, *s.shape), s.dtype) for s in shards]
        self.scratch = [pltpu.SemaphoreType.DMA((nw,)), pltpu.SemaphoreType.DMA((nw * 6,)),
                        pltpu.SemaphoreType.DMA((nw * 6,))]

    def _copies(self, ins, outs, sems):
        nw = len(ins)
        local_sem, send_sem, recv_sem = sems
        x, y, c, chips = _position()
        me = 2 * x + y
        sibling = (x, y, 1 - c)
        half_rows = [s.shape[0] // 2 for s in self.inputs]

        def half(w, chip_idx, which):
            return outs[w].at[chip_idx, pl.ds(which * half_rows[w], half_rows[w])]

        def remote(w, k, src, dst, to):
            return pltpu.make_async_remote_copy(src_ref=src, dst_ref=dst, send_sem=send_sem.at[w * 6 + k],
                                                recv_sem=recv_sem.at[w * 6 + k], device_id=to, device_id_type=MESH)

        local = [pltpu.make_async_copy(ins[w], outs[w].at[me], local_sem.at[w]) for w in range(nw)]
        first = [remote(w, j, ins[w].at[pl.ds(c * half_rows[w], half_rows[w])], half(w, me, c), (px, py, c))
                 for w in range(nw) for j, (px, py) in enumerate(chips)]
        landed = [half(w, 2 * px + py, c) for w in range(nw) for (px, py) in chips]
        arrive = [remote(w, j, landed[w * 3 + j], landed[w * 3 + j], (px, py, c))
                  for w in range(nw) for j, (px, py) in enumerate(chips)]
        passed = [remote(w, 3 + j, landed[w * 3 + j], landed[w * 3 + j], sibling) for w in range(nw) for j in range(3)]
        from_sibling = [remote(w, 3 + j, half(w, 2 * px + py, 1 - c), half(w, 2 * px + py, 1 - c), sibling)
                        for w in range(nw) for j, (px, py) in enumerate(chips)]
        return local, first, arrive, passed, from_sibling

    def start(self, ins, outs, sems):
        local, first, _, _, _ = self._copies(ins, outs, sems)
        for cp in local + first:
            cp.start()

    def finish(self, ins, outs, sems):
        local, first, arrive, passed, from_sibling = self._copies(ins, outs, sems)
        for cp_in, cp_on in zip(arrive, passed):
            cp_in.wait_recv()
            cp_on.start()
        for cp in from_sibling:
            cp.wait_recv()
        for cp in first + passed:
            cp.wait_send()
        for cp in local:
            cp.wait()


class _ExchangeGrads:
    def __init__(self, grads):
        nw = len(grads)
        self.inputs = list(grads)
        self.out_shape = [jax.ShapeDtypeStruct(g.shape, g.dtype) for g in grads]
        self.scratch = [pltpu.SemaphoreType.DMA((nw,)), pltpu.SemaphoreType.DMA((nw * 3,)),
                        pltpu.SemaphoreType.DMA((nw * 3,))]

    def _copies(self, ins, outs, sems):
        nw = len(ins)
        local_sem, send_sem, recv_sem = sems
        x, y, c, chips = _position()
        me = 2 * x + y

        def remote(w, j, src, dst, chip):
            return pltpu.make_async_remote_copy(src_ref=src, dst_ref=dst, send_sem=send_sem.at[w * 3 + j],
                                                recv_sem=recv_sem.at[w * 3 + j], device_id=(*chip, c),
                                                device_id_type=MESH)

        local = [pltpu.make_async_copy(ins[w].at[me], outs[w].at[me], local_sem.at[w]) for w in range(nw)]
        sends = [remote(w, j, ins[w].at[2 * px + py], outs[w].at[me], (px, py))
                 for w in range(nw) for j, (px, py) in enumerate(chips)]
        arrive = [remote(w, j, outs[w].at[2 * px + py], outs[w].at[2 * px + py], (px, py))
                  for w in range(nw) for j, (px, py) in enumerate(chips)]
        return local, sends, arrive

    def start(self, ins, outs, sems):
        local, sends, _ = self._copies(ins, outs, sems)
        for cp in local + sends:
            cp.start()

    def finish(self, ins, outs, sems):
        local, sends, arrive = self._copies(ins, outs, sems)
        for cp in arrive:
            cp.wait_recv()
        for cp in sends:
            cp.wait_send()
        for cp in local:
            cp.wait()


def _hosted(body, comm, *, name, grid, in_specs, out_specs, out_shape, scratch_shapes, compiler_params, args,
            aliases=None):
    aliases = aliases or {}
    if comm is None:
        res = pl.pallas_call(body, name=name, grid=grid, in_specs=in_specs, out_specs=out_specs, out_shape=out_shape,
                             scratch_shapes=scratch_shapes, compiler_params=compiler_params,
                             input_output_aliases=aliases)(*args)
        return list(res), []
    n_in, n_out, n_scr = len(in_specs), len(out_specs), len(scratch_shapes)
    c_in, c_out = len(comm.inputs), len(comm.out_shape)

    def hosted_body(*refs):
        refs = list(refs)
        cut = lambda k: (refs[:k], refs[k:])
        main_in, refs = cut(n_in)
        comm_in, refs = cut(c_in)
        main_out, refs = cut(n_out)
        comm_out, refs = cut(c_out)
        main_scr, comm_scr = cut(n_scr)
        ids = [pl.program_id(a) for a in range(len(grid))]
        first = functools.reduce(jnp.logical_and, [i == 0 for i in ids])
        last = functools.reduce(jnp.logical_and, [i == g - 1 for i, g in zip(ids, grid)])

        @pl.when(first)
        def _():
            comm.start(comm_in, comm_out, comm_scr)

        body(*main_in, *main_out, *main_scr)

        @pl.when(last)
        def _():
            comm.finish(comm_in, comm_out, comm_scr)

    res = pl.pallas_call(
        hosted_body, name=name, grid=grid, in_specs=[*in_specs, *_any_specs(c_in)],
        out_specs=[*out_specs, *_any_specs(c_out)], out_shape=[*out_shape, *comm.out_shape],
        scratch_shapes=[*scratch_shapes, *comm.scratch], compiler_params=compiler_params,
        input_output_aliases=aliases,
    )(*args, *comm.inputs)
    return list(res[:n_out]), list(res[n_out:])


def _run_comm(name, comm):
    def body(*refs):
        refs = list(refs)
        c_in, c_out = len(comm.inputs), len(comm.out_shape)
        ins, outs, sems = refs[:c_in], refs[c_in:c_in + c_out], refs[c_in + c_out:]
        comm.start(ins, outs, sems)
        comm.finish(ins, outs, sems)

    return list(pl.pallas_call(
        body, name=name, in_specs=_any_specs(len(comm.inputs)), out_specs=_any_specs(len(comm.out_shape)),
        out_shape=comm.out_shape, scratch_shapes=comm.scratch,
    )(*comm.inputs))


def _sum_slots(name, slots):
    _, rows, cols = slots.shape
    tr = _update_rows(rows)

    def body(s_ref, o_ref):
        acc = s_ref[0].astype(F32)
        for i in range(1, N_CHIPS):
            acc = acc + s_ref[i].astype(F32)
        o_ref[...] = acc

    return pl.pallas_call(
        body, name=name, grid=(rows // tr,),
        in_specs=[pl.BlockSpec((N_CHIPS, tr, cols), lambda i: (0, i, 0))],
        out_specs=pl.BlockSpec((tr, cols), lambda i: (i, 0)),
        out_shape=jax.ShapeDtypeStruct((rows, cols), F32),
        compiler_params=_params(("arbitrary",)),
    )(slots)


def _swap_with_sibling(parts):
    nw = len(parts)

    def body(*refs):
        ins, outs = refs[:nw], refs[nw:2 * nw]
        send_sem, recv_sem = refs[2 * nw:]
        x, y, c, _ = _position()
        copies = [pltpu.make_async_remote_copy(src_ref=ins[w], dst_ref=outs[w], send_sem=send_sem.at[w],
                                               recv_sem=recv_sem.at[w], device_id=(x, y, 1 - c), device_id_type=MESH)
                  for w in range(nw)]
        for cp in copies:
            cp.start()
        for cp in copies:
            cp.wait()

    return pl.pallas_call(
        body, name="swap_with_sibling",
        in_specs=_any_specs(nw), out_specs=_any_specs(nw),
        out_shape=[jax.ShapeDtypeStruct(p.shape, p.dtype) for p in parts],
        scratch_shapes=[pltpu.SemaphoreType.DMA((nw,)), pltpu.SemaphoreType.DMA((nw,))],
    )(*parts)


def _sum_small(part):
    def body(p_ref, o_ref, buf, send_sem, recv_sem):
        x, y, c, _ = _position()
        me = 4 * x + 2 * y + c
        buf[me] = p_ref[...]
        copies = []
        for k in range(1, N_DEV):
            peer = tuple(1 - v if (k >> s) & 1 else v for v, s in ((x, 2), (y, 1), (c, 0)))
            copies.append(pltpu.make_async_remote_copy(src_ref=p_ref, dst_ref=buf.at[me], send_sem=send_sem.at[k - 1],
                                                       recv_sem=recv_sem.at[k - 1], device_id=peer, device_id_type=MESH))
        for cp in copies:
            cp.start()
        for cp in copies:
            cp.wait()
        acc = buf[0]
        for d in range(1, N_DEV):
            acc = acc + buf[d]
        o_ref[...] = acc

    vm = pl.BlockSpec(memory_space=pltpu.VMEM)
    return pl.pallas_call(
        body, name="sum_small", in_specs=[vm], out_specs=vm,
        out_shape=jax.ShapeDtypeStruct((1, SM_LEN), F32),
        scratch_shapes=[pltpu.VMEM((N_DEV, 1, SM_LEN), F32), pltpu.SemaphoreType.DMA((N_DEV - 1,)),
                        pltpu.SemaphoreType.DMA((N_DEV - 1,))],
    )(part)


def _adamw(w, g, m, v):
    m = ADAM_B1 * m + (1.0 - ADAM_B1) * g
    v = ADAM_B2 * v + (1.0 - ADAM_B2) * (g * g)
    m_hat = m / (1.0 - ADAM_B1 ** ADAM_STEP)
    v_hat = v / (1.0 - ADAM_B2 ** ADAM_STEP)
    delta = -ADAM_LR * (m_hat / (jnp.sqrt(v_hat) + ADAM_EPS) + ADAM_WD * w)
    return delta, m, v


def _adam_big(name, p_own, p_sibling, w, m, v):
    rows, cols = w.shape
    tr = _update_rows(rows)

    def body(p_ref, q_ref, w_ref, m_ref, v_ref, g_ref, d_ref, nm_ref, nv_ref):
        g = p_ref[...] + q_ref[...]
        g_ref[...] = g
        d_ref[...], nm_ref[...], nv_ref[...] = _adamw(w_ref[...], g, m_ref[...], v_ref[...])

    spec = pl.BlockSpec((tr, cols), lambda i: (i, 0))
    return pl.pallas_call(
        body, name=name, grid=(rows // tr,), in_specs=[spec] * 5, out_specs=[spec] * 4,
        out_shape=[jax.ShapeDtypeStruct((rows, cols), F32)] * 4,
        compiler_params=_params(("arbitrary",)),
    )(p_own, p_sibling, w, m, v)


_SMALL_AT = {"ln1_g": 0, "ln1_b": D_MODEL, "ln2_g": 2 * D_MODEL, "ln2_b": 3 * D_MODEL, "ln3_g": 4 * D_MODEL,
             "ln3_b": 5 * D_MODEL, "b_in": SM_BIN, "attn_sinks": SM_SINK, "hgrn_norm_g": SM_NG}


def _adam_small(total, w, m, v, lb_of):
    names = list(_SMALL)
    k = len(names)

    def body(*refs):
        t_ref = refs[0]
        w_refs, m_refs, v_refs = (refs[1 + i * k:1 + (i + 1) * k] for i in range(3))
        g_refs, d_refs, nm_refs, nv_refs = (refs[1 + (3 + i) * k:1 + (4 + i) * k] for i in range(4))
        for i, name in enumerate(names):
            if name == "hgrn_lb_logits":
                _, vjp = jax.vjp(lb_of, w_refs[i][0:1, :], w_refs[i][1:2, :])
                g_refs[i][0:1, :], g_refs[i][1:2, :] = vjp(t_ref[:, SM_LB:SM_LOSS])
            else:
                at = _SMALL_AT[name]
                g_refs[i][...] = t_ref[:, at:at + w_refs[i].shape[1]]
            d_refs[i][...], nm_refs[i][...], nv_refs[i][...] = _adamw(
                w_refs[i][...], g_refs[i][...], m_refs[i][...], v_refs[i][...])

    shapes = [jax.ShapeDtypeStruct(w[name].shape, F32) for name in names]
    res = pl.pallas_call(body, name="adam_small", out_shape=shapes * 4)(
        total, *[w[n] for n in names], *[m[n] for n in names], *[v[n] for n in names])
    return [dict(zip(names, res[i * k:(i + 1) * k])) for i in range(4)]


_BIG = ("ffn1_w1", "ffn1_w3", "ffn1_w2", "w_in", "w_proj_attn", "w_proj_hgrn", "w_out", "ffn2_w1", "ffn2_w3", "ffn2_w2")
_SMALL = ("ln1_g", "ln1_b", "ln2_g", "ln2_b", "ln3_g", "ln3_b", "b_in", "attn_sinks", "hgrn_norm_g", "hgrn_lb_logits")
_ORDER = ("ln1_g", "ln1_b", "ffn1_w1", "ffn1_w3", "ffn1_w2", "ln2_g", "ln2_b", "w_in", "b_in", "attn_sinks",
          "hgrn_lb_logits", "hgrn_norm_g", "w_proj_attn", "w_proj_hgrn", "w_out", "ln3_g", "ln3_b",
          "ffn2_w1", "ffn2_w3", "ffn2_w2")


_TRANSPOSED = ("ffn1_w1", "ffn1_w3", "ffn2_w1", "ffn2_w3")


def _local_view(name, arr):
    return arr[0].T if name in _TRANSPOSED else arr[0]


def _ffn_grad(name, hidden, other, comm=None):
    (dw,), comm_out = _grad_matmul(name, hidden, "cols", D_FF // FF_GRAD_PARTS, other, "shared", D_MODEL,
                                   comm=comm, parts=FF_GRAD_PARTS)
    return dw.reshape(N_CHIPS, D_FF // N_CHIPS, D_MODEL), comm_out


def kernel(x, ln1_g, ln1_b, ffn1_w1, ffn1_w3, ffn1_w2, ln2_g, ln2_b, w_in, b_in, attn_sinks, hgrn_lb_logits, hgrn_norm_g, w_proj_attn, w_proj_hgrn, w_out, ln3_g, ln3_b, ffn2_w1, ffn2_w3, ffn2_w2, loss_target, m_ln1_g, m_ln1_b, m_ffn1_w1, m_ffn1_w3, m_ffn1_w2, m_ln2_g, m_ln2_b, m_w_in, m_b_in, m_attn_sinks, m_hgrn_lb_logits, m_hgrn_norm_g, m_w_proj_attn, m_w_proj_hgrn, m_w_out, m_ln3_g, m_ln3_b, m_ffn2_w1, m_ffn2_w3, m_ffn2_w2, v_ln1_g, v_ln1_b, v_ffn1_w1, v_ffn1_w3, v_ffn1_w2, v_ln2_g, v_ln2_b, v_w_in, v_b_in, v_attn_sinks, v_hgrn_lb_logits, v_hgrn_norm_g, v_w_proj_attn, v_w_proj_hgrn, v_w_out, v_ln3_g, v_ln3_b, v_ffn2_w1, v_ffn2_w3, v_ffn2_w2):
    w = dict(ln1_g=ln1_g, ln1_b=ln1_b, ffn1_w1=ffn1_w1, ffn1_w3=ffn1_w3, ffn1_w2=ffn1_w2, ln2_g=ln2_g, ln2_b=ln2_b,
             w_in=w_in, b_in=b_in, attn_sinks=attn_sinks, hgrn_lb_logits=hgrn_lb_logits, hgrn_norm_g=hgrn_norm_g,
             w_proj_attn=w_proj_attn, w_proj_hgrn=w_proj_hgrn, w_out=w_out, ln3_g=ln3_g, ln3_b=ln3_b,
             ffn2_w1=ffn2_w1, ffn2_w3=ffn2_w3, ffn2_w2=ffn2_w2)
    mom = dict(ln1_g=m_ln1_g, ln1_b=m_ln1_b, ffn1_w1=m_ffn1_w1, ffn1_w3=m_ffn1_w3, ffn1_w2=m_ffn1_w2, ln2_g=m_ln2_g,
               ln2_b=m_ln2_b, w_in=m_w_in, b_in=m_b_in, attn_sinks=m_attn_sinks, hgrn_lb_logits=m_hgrn_lb_logits,
               hgrn_norm_g=m_hgrn_norm_g, w_proj_attn=m_w_proj_attn, w_proj_hgrn=m_w_proj_hgrn, w_out=m_w_out,
               ln3_g=m_ln3_g, ln3_b=m_ln3_b, ffn2_w1=m_ffn2_w1, ffn2_w3=m_ffn2_w3, ffn2_w2=m_ffn2_w2)
    var = dict(ln1_g=v_ln1_g, ln1_b=v_ln1_b, ffn1_w1=v_ffn1_w1, ffn1_w3=v_ffn1_w3, ffn1_w2=v_ffn1_w2, ln2_g=v_ln2_g,
               ln2_b=v_ln2_b, w_in=v_w_in, b_in=v_b_in, attn_sinks=v_attn_sinks, hgrn_lb_logits=v_hgrn_lb_logits,
               hgrn_norm_g=v_hgrn_norm_g, w_proj_attn=v_w_proj_attn, w_proj_hgrn=v_w_proj_hgrn, w_out=v_w_out,
               ln3_g=v_ln3_g, ln3_b=v_ln3_b, ffn2_w1=v_ffn2_w1, ffn2_w3=v_ffn2_w3, ffn2_w2=v_ffn2_w2)

    n_tok = x.shape[1]
    x0 = x.reshape(n_tok, D_MODEL)
    target = loss_target.reshape(n_tok, D_MODEL)

    shard = {k: _bf(_local_view(k, w[k])) for k in _BIG}
    gather = lambda keys: _GatherWeights([shard[k] for k in keys])
    slots = {}
    exchange = lambda keys: _ExchangeGrads([big[k] for k in keys])
    ffn1_keys = ("ffn1_w1", "ffn1_w3", "ffn1_w2")
    mixer_keys = ("w_in", "w_proj_attn", "w_proj_hgrn", "w_out")
    ffn2_keys = ("ffn2_w1", "ffn2_w3", "ffn2_w2")
    whole = lambda ts: [t.reshape(D_FF, D_MODEL) for t in ts]
    f1 = whole(_run_comm("gather_ffn1", gather(ffn1_keys)))

    tabs = _rope_tables(n_tok)
    lb, lb_of = _lb_fwd(hgrn_lb_logits)
    (z1, x1, x0b, h1_1, h3_1, x1b), (w_in_g, w_pa, w_ph, w_o) = _ffn_fwd(
        "ffn1_fwd", x0, *f1, ln1_g, ln1_b, comm=gather(mixer_keys))
    w_pa, w_ph, w_o = (t.reshape(D_MODEL, D_MODEL) for t in (w_pa, w_ph, w_o))
    w_in_g = _reorder_w_in("w_in_cols", w_in_g, True)
    (proj,), f2 = _in_proj(x1b, w_in_g, _to_kernel_cols(b_in), comm=gather(ffn2_keys))
    f2 = whole(f2)
    y_attn = _attn_fwd(proj, tabs, attn_sinks)
    y_hgrn, states = _hgrn_fwd(proj, lb, hgrn_norm_g)
    z2, x2, proj_a, proj_h = _mix_fwd(y_attn, y_hgrn, proj, x1, w_pa, w_ph, w_o, ln2_g, ln2_b)
    (z3, dy, x2b, h1_2, h3_2, loss_part), _ = _ffn_fwd("ffn2_fwd", x2, *f2, ln3_g, ln3_b, target=target)

    big = {}
    small = {}
    (dx2, a2, dh1_2, dh3_2, df2, small["ln3_g"], small["ln3_b"]), _ = _ffn_bwd(
        "ffn2_bwd", h1_2, h3_2, z3, dy, *f2, ln3_g, ln3_b)
    big["ffn2_w1"], _ = _ffn_grad("ffn2_dw1", dh1_2, x2b)
    big["ffn2_w3"], _ = _ffn_grad("ffn2_dw3", dh3_2, x2b)
    big["ffn2_w2"], _ = _ffn_grad("ffn2_dw2", a2, df2)
    (dz2, dz2b, merged, dya, dyh, dy_attn, dy_hgrn, dproj, small["ln2_g"], small["ln2_b"]) = _mix_bwd(
        dx2, z2, proj_a, proj_h, proj, w_pa, w_ph, w_o, ln2_g, ln2_b)
    for key, name, lhs, rhs in (("w_out", "dw_out", merged, dz2b), ("w_proj_attn", "dw_proj_attn", y_attn, dya),
                                ("w_proj_hgrn", "dw_proj_hgrn", y_hgrn, dyh)):
        (dw,), _ = _grad_matmul(name, lhs, "cols", D_MODEL, rhs, "shared", D_MODEL, parts=1)
        big[key] = dw.reshape(N_CHIPS, PROJ_SHARD, D_MODEL)
    dproj, dlb, small["hgrn_norm_g"] = _hgrn_bwd(proj, lb, hgrn_norm_g, states, dy_hgrn, dproj)
    early_keys = ffn2_keys + mixer_keys[1:]
    (dproj, dsinks), early = _attn_bwd(proj, dy_attn, tabs, attn_sinks, dproj, comm=exchange(early_keys))
    slots.update(zip(early_keys, early))
    (dw_in, db_in), _ = _grad_matmul("dw_in", x1b, "shared", D_MODEL, dproj, "cols", IN_SHARD, colsum=True)
    big["w_in"] = _reorder_w_in("dw_in_cols", dw_in, False)
    small["b_in"] = _from_kernel_cols(db_in)
    (dx1,), (slots["w_in"],) = _in_proj_dx(dproj, w_in_g, dz2, comm=exchange(("w_in",)))
    (grad_x, a1, dh1_1, dh3_1, df1, small["ln1_g"], small["ln1_b"]), _ = _ffn_bwd(
        "ffn1_bwd", h1_1, h3_1, z1, dx1, *f1, ln1_g, ln1_b)
    big["ffn1_w1"], _ = _ffn_grad("ffn1_dw1", dh1_1, x0b)
    big["ffn1_w3"], (slots["ffn1_w1"],) = _ffn_grad("ffn1_dw3", dh3_1, x0b, comm=exchange(("ffn1_w1",)))
    big["ffn1_w2"], (slots["ffn1_w3"],) = _ffn_grad("ffn1_dw2", a1, df1, comm=exchange(("ffn1_w3",)))
    (slots["ffn1_w2"],) = _run_comm("exchange_last", exchange(("ffn1_w2",)))

    local = {k: _local_view(k, w[k]) for k in _BIG}
    partial = [_sum_slots("sum_" + k, slots[k]) for k in _BIG]
    from_sibling = _swap_with_sibling(partial)

    outs = {"grad": {}, "delta": {}, "m": {}, "v": {}}
    for k, p, q in zip(_BIG, partial, from_sibling):
        res = _adam_big("adam_" + k, p, q, local[k], _local_view(k, mom[k]), _local_view(k, var[k]))
        for kind, r in zip(("grad", "delta", "m", "v"), res):
            outs[kind][k] = (r.T if k in _TRANSPOSED else r).reshape(w[k].shape)

    total = _sum_small(jnp.concatenate(
        [small[k] for k in ("ln1_g", "ln1_b", "ln2_g", "ln2_b", "ln3_g", "ln3_b", "b_in")]
        + [dsinks, small["hgrn_norm_g"], dlb, loss_part], axis=1))
    for kind, r in zip(("grad", "delta", "m", "v"), _adam_small(total, w, mom, var, lb_of)):
        outs[kind].update(r)
    loss = total[0, SM_LOSS]

    return (loss, grad_x.reshape(x.shape), *[outs["grad"][k] for k in _ORDER], *[outs["delta"][k] for k in _ORDER],
            *[outs["m"][k] for k in _ORDER], *[outs["v"][k] for k in _ORDER])
```

```python
import functools

import jax
import jax.numpy as jnp
from jax import lax
from jax.experimental import pallas as pl
from jax.experimental.pallas import tpu as pltpu

F32 = jnp.float32
BF16 = jnp.bfloat16

D_MODEL = 1024
N_Q_HEADS = 16
N_KV_HEADS = 4
HEAD_DIM = 64
ATTN_BLOCK = 128
ROPE_THETA = 500000.0
ROPE_DIM = HEAD_DIM // 4
HGRN_HEADS = 8
HGRN_DK = 128
HGRN_CHUNK = 64
D_FF = 2816
D_IN = 7680
DEEPNORM_ALPHA = 2 ** 0.25
LN_EPS = 1e-5
RMS_EPS = 1e-6
NEG_INF = -1e30

ADAM_LR = 0.001
ADAM_B1 = 0.9
ADAM_B2 = 0.999
ADAM_EPS = 1e-08
ADAM_WD = 0.01
ADAM_STEP = 10

N_CHIPS = 4
N_DEV = 8
LANES = 128
FF_GRAD_PARTS = 2
FFN_TILE = 256
IN_SHARD = D_IN // N_CHIPS
PROJ_SHARD = D_MODEL // N_CHIPS
ROW_TILE = 512
GRAD_ROWS = 2048
MIX_TILE = 256
UPDATE_ROWS = 128
HGRN_CHUNKS_PER_STEP = 16
VMEM_LIMIT = 56 * 1024 * 1024

GATES_WIDTH = 2 * D_MODEL
HGRN_HEAD_WIDTH = 4 * HGRN_DK
ATTN_WIDTH = D_MODEL + 2 * N_KV_HEADS * HEAD_DIM
COL_HGRN = GATES_WIDTH // HGRN_HEAD_WIDTH
COL_ATTN = (GATES_WIDTH + HGRN_HEADS * HGRN_HEAD_WIDTH) // ATTN_WIDTH
COL_Q = (GATES_WIDTH + HGRN_HEADS * HGRN_HEAD_WIDTH) // D_MODEL
COL_K = (GATES_WIDTH + HGRN_HEADS * HGRN_HEAD_WIDTH + D_MODEL) // (N_KV_HEADS * HEAD_DIM)
COL_V = COL_K + 1


def _to_kernel_cols(a):
    lead = a.shape[:-1]
    qkv, hg, gates = a[..., :ATTN_WIDTH], a[..., ATTN_WIDTH:D_IN - GATES_WIDTH], a[..., D_IN - GATES_WIDTH:]
    hg = jnp.swapaxes(hg.reshape(*lead, 4, HGRN_HEADS, HGRN_DK), -3, -2).reshape(*lead, -1)
    return jnp.concatenate([gates, hg, qkv], axis=-1)


def _from_kernel_cols(a):
    lead = a.shape[:-1]
    gates, hg, qkv = a[..., :GATES_WIDTH], a[..., GATES_WIDTH:D_IN - ATTN_WIDTH], a[..., D_IN - ATTN_WIDTH:]
    hg = jnp.swapaxes(hg.reshape(*lead, HGRN_HEADS, 4, HGRN_DK), -3, -2).reshape(*lead, -1)
    return jnp.concatenate([qkv, hg, gates], axis=-1)

SM_LN = 0
SM_BIN = 6 * D_MODEL
SM_SINK = SM_BIN + D_IN
SM_NG = SM_SINK + LANES
SM_LB = SM_NG + LANES
SM_LOSS = SM_LB + D_MODEL
SM_LEN = SM_LOSS + LANES

MESH = pl.DeviceIdType.MESH


def _mm(a, b):
    return lax.dot_general(a, b, (((1,), (0,)), ((), ())), preferred_element_type=F32)


def _mm_nt(a, b):
    return lax.dot_general(a, b, (((1,), (1,)), ((), ())), preferred_element_type=F32)


def _mm_tn(a, b):
    return lax.dot_general(a, b, (((0,), (0,)), ((), ())), preferred_element_type=F32)


def _bf(v):
    return v.astype(BF16)


def _sig(v):
    return jax.nn.sigmoid(v)


def _ln(z, g, b):
    mu = jnp.mean(z, axis=-1, keepdims=True)
    zc = z - mu
    var = jnp.mean(zc * zc, axis=-1, keepdims=True)
    return zc * lax.rsqrt(var + LN_EPS) * g + b


def _swiglu_act(h1, h3):
    return (h1 * _sig(h1)) * h3


def _params(sem=None):
    return pltpu.CompilerParams(dimension_semantics=sem, vmem_limit_bytes=VMEM_LIMIT)


def _full(shape):
    nd = len(shape)
    return pl.BlockSpec(shape, lambda *_: (0,) * nd)


def _update_rows(rows):
    return max(t for t in range(8, UPDATE_ROWS + 1, 8) if rows % t == 0)


def _resident(shape):
    nd = len(shape)
    return pl.BlockSpec(shape, lambda *_: (0,) * nd, pipeline_mode=pl.Buffered(1))


def _ffn_fwd(name, x, w1t, w3t, w2, g, b, target=None, comm=None):
    n = x.shape[0]
    tm = min(FFN_TILE, n)
    final = target is not None

    def body(*refs):
        if final:
            x_ref, w1_ref, w3_ref, w2_ref, g_ref, b_ref, t_ref, z_ref, o_ref, xb_ref, h1_ref, h3_ref, loss_ref = refs
        else:
            x_ref, w1_ref, w3_ref, w2_ref, g_ref, b_ref, z_ref, o_ref, xb_ref, h1_ref, h3_ref, ob_ref = refs
        xb = _bf(x_ref[...])
        xb_ref[...] = xb
        h1 = _mm_nt(xb, w1_ref[...])
        h3 = _mm_nt(xb, w3_ref[...])
        h1_ref[...] = _bf(h1)
        h3_ref[...] = _bf(h3)
        z = DEEPNORM_ALPHA * x_ref[...] + 0.5 * _mm(_bf(_swiglu_act(h1, h3)), w2_ref[...])
        z_ref[...] = z
        y = _ln(z, g_ref[...], b_ref[...])
        if final:
            e = y - t_ref[...]

            @pl.when(pl.program_id(0) == 0)
            def _():
                loss_ref[...] = jnp.zeros_like(loss_ref)

            loss_ref[...] += jnp.sum(e * e) * (0.5 / D_MODEL)
            o_ref[...] = e * (1.0 / D_MODEL)
        else:
            o_ref[...] = y
            ob_ref[...] = _bf(y)

    row = pl.BlockSpec((tm, D_MODEL), lambda i: (i, 0))
    hid = pl.BlockSpec((tm, D_FF), lambda i: (i, 0))
    wres = _resident((D_FF, D_MODEL))
    vec = _full((1, D_MODEL))
    in_specs = [row, wres, wres, wres, vec, vec]
    args = [x, w1t, w3t, w2, g, b]
    hid_shape = jax.ShapeDtypeStruct((n, D_FF), BF16)
    out_specs = [row, row, row, hid, hid]
    out_shape = ([jax.ShapeDtypeStruct((n, D_MODEL), F32)] * 2 + [jax.ShapeDtypeStruct((n, D_MODEL), BF16)]
                 + [hid_shape] * 2)
    if final:
        in_specs.append(row)
        args.append(target)
        out_specs.append(_full((1, LANES)))
        out_shape.append(jax.ShapeDtypeStruct((1, LANES), F32))
    else:
        out_specs.append(row)
        out_shape.append(jax.ShapeDtypeStruct((n, D_MODEL), BF16))
    return _hosted(
        body, comm, name=name, grid=(n // tm,), in_specs=in_specs, out_specs=out_specs, out_shape=out_shape,
        scratch_shapes=[], compiler_params=_params(("arbitrary",)), args=args)


def _ffn_bwd(name, h1s, h3s, z, dout, w1t, w3t, w2, g, b, comm=None):
    n = z.shape[0]
    tm = min(FFN_TILE, n)

    def body(h1_ref, h3_ref, z_ref, do_ref, w1_ref, w3_ref, w2_ref, g_ref, b_ref,
             dx_ref, a_ref, dh1_ref, dh3_ref, df_ref, dg_ref, db_ref):
        _, vjp = jax.vjp(_ln, z_ref[...], g_ref[...], b_ref[...])
        dz, dg, db = vjp(do_ref[...])

        @pl.when(pl.program_id(0) == 0)
        def _():
            dg_ref[...] = jnp.zeros_like(dg_ref)
            db_ref[...] = jnp.zeros_like(db_ref)

        dg_ref[...] += dg
        db_ref[...] += db
        df = _bf(0.5 * dz)
        df_ref[...] = df
        a, act_vjp = jax.vjp(_swiglu_act, h1_ref[...].astype(F32), h3_ref[...].astype(F32))
        dh1, dh3 = act_vjp(_mm_nt(df, w2_ref[...]))
        dh1 = _bf(dh1)
        dh3 = _bf(dh3)
        a_ref[...] = _bf(a)
        dh1_ref[...] = dh1
        dh3_ref[...] = dh3
        dx_ref[...] = DEEPNORM_ALPHA * dz + _mm(dh1, w1_ref[...]) + _mm(dh3, w3_ref[...])

    row = pl.BlockSpec((tm, D_MODEL), lambda i: (i, 0))
    hid = pl.BlockSpec((tm, D_FF), lambda i: (i, 0))
    wres = _resident((D_FF, D_MODEL))
    vec = _full((1, D_MODEL))
    hid_shape = jax.ShapeDtypeStruct((n, D_FF), BF16)
    return _hosted(
        body, comm, name=name, grid=(n // tm,),
        in_specs=[hid, hid, row, row, wres, wres, wres, vec, vec],
        out_specs=[row, hid, hid, hid, row, vec, vec],
        out_shape=[jax.ShapeDtypeStruct((n, D_MODEL), F32), hid_shape, hid_shape, hid_shape,
                   jax.ShapeDtypeStruct((n, D_MODEL), BF16),
                   jax.ShapeDtypeStruct((1, D_MODEL), F32), jax.ShapeDtypeStruct((1, D_MODEL), F32)],
        scratch_shapes=[], compiler_params=_params(("arbitrary",)),
        args=[h1s, h3s, z, dout, w1t, w3t, w2, g, b])


def _operand_spec(arr, mode, tn, width, parts):
    if mode == "shared":
        return pl.BlockSpec((tn, width), lambda s, k: (k, 0))
    assert mode == "cols" and arr.shape[1] == parts * width
    return pl.BlockSpec((tn, width), lambda s, k: (k, s))


def _grad_matmul(name, a, a_mode, ka, b, b_mode, kb, colsum=False, comm=None, parts=N_CHIPS):
    n = a.shape[-2]
    tn = min(GRAD_ROWS, n)
    nk = n // tn

    def body(*refs):
        if colsum:
            a_ref, b_ref, o_ref, cs_ref, acc = refs
        else:
            a_ref, b_ref, o_ref, acc = refs
        k = pl.program_id(1)
        av = a_ref[...]
        bv = b_ref[...]

        @pl.when(k == 0)
        def _():
            acc[...] = jnp.zeros_like(acc)
            if colsum:
                cs_ref[...] = jnp.zeros_like(cs_ref)

        acc[...] += _mm_tn(av, bv)
        if colsum:
            cs_ref[...] += jnp.sum(bv.astype(F32), axis=0, keepdims=True)

        @pl.when(k == nk - 1)
        def _():
            o_ref[0] = _bf(acc[...])

    out_specs = [pl.BlockSpec((1, ka, kb), lambda s, k: (s, 0, 0))]
    out_shape = [jax.ShapeDtypeStruct((parts, ka, kb), BF16)]
    if colsum:
        out_specs.append(pl.BlockSpec((1, kb), lambda s, k: (0, s)))
        out_shape.append(jax.ShapeDtypeStruct((1, parts * kb), F32))
    return _hosted(
        body, comm, name=name, grid=(parts, nk),
        in_specs=[_operand_spec(a, a_mode, tn, ka, parts), _operand_spec(b, b_mode, tn, kb, parts)],
        out_specs=out_specs, out_shape=out_shape,
        scratch_shapes=[pltpu.VMEM((ka, kb), F32)],
        compiler_params=_params(("arbitrary", "arbitrary")), args=[a, b])


def _kernel_block_of(ref_block):
    attn_blocks, gate_blocks = ATTN_WIDTH // LANES, GATES_WIDTH // LANES
    hgrn_blocks = HGRN_HEADS * HGRN_HEAD_WIDTH // LANES
    if ref_block < attn_blocks:
        return gate_blocks + hgrn_blocks + ref_block
    if ref_block < attn_blocks + hgrn_blocks:
        kind, head = divmod(ref_block - attn_blocks, HGRN_HEADS)
        return gate_blocks + head * (HGRN_HEAD_WIDTH // LANES) + kind
    return ref_block - attn_blocks - hgrn_blocks


def _reorder_w_in(name, w4, to_kernel_order):
    per = IN_SHARD // LANES
    tr = MIX_TILE

    def body(i_ref, o_ref):
        for g in range(D_IN // LANES):
            s, b = divmod(g, per)
            k = _kernel_block_of(g)
            if to_kernel_order:
                o_ref[:, k * LANES:(k + 1) * LANES] = i_ref[s, :, b * LANES:(b + 1) * LANES]
            else:
                ks, kb = divmod(k, per)
                o_ref[s, :, b * LANES:(b + 1) * LANES] = i_ref[ks, :, kb * LANES:(kb + 1) * LANES]

    in_spec = pl.BlockSpec((N_CHIPS, tr, IN_SHARD), lambda i: (0, i, 0))
    if to_kernel_order:
        out_spec, out_shape = pl.BlockSpec((tr, D_IN), lambda i: (i, 0)), (D_MODEL, D_IN)
    else:
        out_spec, out_shape = in_spec, (N_CHIPS, D_MODEL, IN_SHARD)
    return pl.pallas_call(
        body, name=name, grid=(D_MODEL // tr,), in_specs=[in_spec], out_specs=out_spec,
        out_shape=jax.ShapeDtypeStruct(out_shape, w4.dtype), compiler_params=_params(("arbitrary",)),
    )(w4)


def _in_proj(x1, w_in_g, b_in, comm=None):
    n = x1.shape[0]
    tm = min(FFN_TILE, n)

    def body(x_ref, w_ref, b_ref, o_ref):
        xv = x_ref[...]
        for j in range(N_CHIPS):
            cols = slice(j * IN_SHARD, (j + 1) * IN_SHARD)
            o_ref[:, cols] = _mm(xv, w_ref[:, cols]) + b_ref[:, cols]

    return _hosted(
        body, comm, name="in_proj", grid=(n // tm,),
        in_specs=[pl.BlockSpec((tm, D_MODEL), lambda i: (i, 0)),
                  _resident((D_MODEL, D_IN)), _full((1, D_IN))],
        out_specs=[pl.BlockSpec((tm, D_IN), lambda i: (i, 0))],
        out_shape=[jax.ShapeDtypeStruct((n, D_IN), F32)],
        scratch_shapes=[],
        compiler_params=_params(("arbitrary",)), args=[x1, w_in_g, b_in])


def _in_proj_dx(dproj, w_in_g, dz2, comm=None):
    n = dproj.shape[0]
    tm = min(FFN_TILE, n)

    def body(dp_ref, w_ref, dz_ref, o_ref):
        dx = DEEPNORM_ALPHA * dz_ref[...]
        for j in range(N_CHIPS):
            cols = slice(j * IN_SHARD, (j + 1) * IN_SHARD)
            dx = dx + _mm_nt(dp_ref[:, cols], w_ref[:, cols])
        o_ref[...] = dx

    return _hosted(
        body, comm, name="in_proj_dx", grid=(n // tm,),
        in_specs=[pl.BlockSpec((tm, D_IN), lambda i: (i, 0)),
                  _resident((D_MODEL, D_IN)),
                  pl.BlockSpec((tm, D_MODEL), lambda i: (i, 0))],
        out_specs=[pl.BlockSpec((tm, D_MODEL), lambda i: (i, 0))],
        out_shape=[jax.ShapeDtypeStruct((n, D_MODEL), F32)],
        scratch_shapes=[],
        compiler_params=_params(("arbitrary",)), args=[dproj, w_in_g, dz2])


def _rope_tables(seq_len):
    pos = jnp.arange(seq_len, dtype=F32)
    inv_freq = ROPE_THETA ** (-jnp.arange(0, ROPE_DIM, 2, dtype=F32) / ROPE_DIM)
    ang = pos[:, None] * inv_freq[None, :]
    cos, sin = jnp.cos(ang), jnp.sin(ang)
    half = ROPE_DIM // 2
    rest = HEAD_DIM - ROPE_DIM
    ones = jnp.ones((seq_len, rest), F32)
    zeros = jnp.zeros((seq_len, rest), F32)
    zh = jnp.zeros((seq_len, half), F32)
    c = jnp.concatenate([cos, cos, ones], axis=1)
    sa = jnp.concatenate([-sin, zh, zeros], axis=1)
    sb = jnp.concatenate([zh, sin, zeros], axis=1)
    reps = LANES // HEAD_DIM
    return tuple(jnp.tile(t, (1, reps)) for t in (c, sa, sb))


def _rope(t, c, sa, sb):
    w = t.shape[1]
    reps = w // LANES
    half = ROPE_DIM // 2
    return (t * jnp.tile(c, (1, reps)) + pltpu.roll(t, w - half, 1) * jnp.tile(sa, (1, reps))
            + pltpu.roll(t, half, 1) * jnp.tile(sb, (1, reps)))


def _rope_transposed(g, c, sa, sb):
    w = g.shape[1]
    reps = w // LANES
    half = ROPE_DIM // 2
    return (g * jnp.tile(c, (1, reps)) + pltpu.roll(g * jnp.tile(sa, (1, reps)), half, 1)
            + pltpu.roll(g * jnp.tile(sb, (1, reps)), w - half, 1))


GROUP = N_Q_HEADS // N_KV_HEADS


def _both_halves(t_pair, which):
    lo = lax.broadcasted_iota(jnp.int32, t_pair.shape, 1) < HEAD_DIM
    swapped = pltpu.roll(t_pair, HEAD_DIM, 1)
    return _bf(jnp.where(lo, t_pair, swapped) if which == 0 else jnp.where(lo, swapped, t_pair))


def _stack_heads(ref_or_val, kh):
    lo = lax.broadcasted_iota(jnp.int32, (ATTN_BLOCK, LANES), 1) < HEAD_DIM
    rows = []
    for gp in range(GROUP // 2):
        pair = kh * (GROUP // 2) + gp
        t = ref_or_val[:, pair * LANES:(pair + 1) * LANES]
        rows += [jnp.where(lo, t, jnp.zeros_like(t)), jnp.where(lo, jnp.zeros_like(t), t)]
    return jnp.concatenate(rows, axis=0)


def _unstack_pairs(stacked):
    lo = lax.broadcasted_iota(jnp.int32, (ATTN_BLOCK, LANES), 1) < HEAD_DIM
    b = ATTN_BLOCK
    return [jnp.where(lo, stacked[2 * gp * b:(2 * gp + 1) * b], stacked[(2 * gp + 1) * b:(2 * gp + 2) * b])
            for gp in range(GROUP // 2)]


def _attn_mask_t(n):
    cols = GROUP * ATTN_BLOCK
    kj = lax.broadcasted_iota(jnp.int32, (2 * ATTN_BLOCK, cols), 0)
    qi = lax.broadcasted_iota(jnp.int32, (2 * ATTN_BLOCK, cols), 1) % ATTN_BLOCK
    dist = qi + ATTN_BLOCK - kj
    return (dist >= 0) & (dist < ATTN_BLOCK) & (n * ATTN_BLOCK + kj - ATTN_BLOCK >= 0)


def _sink_row(sink_ref, kh):
    col = lax.broadcasted_iota(jnp.int32, (1, GROUP * ATTN_BLOCK), 1)
    row = jnp.full((1, GROUP * ATTN_BLOCK), sink_ref[0, kh * GROUP + GROUP - 1], F32)
    for i in reversed(range(GROUP - 1)):
        row = jnp.where(col < (i + 1) * ATTN_BLOCK, sink_ref[0, kh * GROUP + i], row)
    return row


def _attn_probs_t(q_masked, k_sel, mask_t, sink):
    s = _mm_nt(k_sel, q_masked) * (HEAD_DIM ** -0.5)
    s = jnp.where(mask_t, s, NEG_INF)
    m = jnp.maximum(jnp.max(s, axis=0, keepdims=True), sink)
    p = jnp.exp(s - m)
    e_sink = jnp.exp(sink - m)
    denom = jnp.sum(p, axis=0, keepdims=True) + e_sink
    return p / denom, e_sink / denom


def _attn_fwd(proj, tabs, sinks):
    n_tok = proj.shape[0]
    nb = n_tok // ATTN_BLOCK

    def body(q_ref, k_ref, v_ref, c_ref, sa_ref, sb_ref, sink_ref, y_ref, kprev, vprev):
        n = pl.program_id(0)

        @pl.when(n == 0)
        def _():
            kprev[...] = jnp.zeros_like(kprev)
            vprev[...] = jnp.zeros_like(vprev)

        c, sa, sb = c_ref[...], sa_ref[...], sb_ref[...]
        qr = _bf(_rope(q_ref[...], c, sa, sb))
        kr = _rope(k_ref[...], c, sa, sb)
        vc = v_ref[...]
        kk = jnp.concatenate([kprev[...], kr], axis=0)
        vv = jnp.concatenate([vprev[...], vc], axis=0)
        kprev[...] = kr
        vprev[...] = vc
        mask = _attn_mask_t(n)
        for kh in range(N_KV_HEADS):
            r, which = divmod(kh, 2)
            kb = _both_halves(kk[:, r * LANES:(r + 1) * LANES], which)
            vb = _both_halves(vv[:, r * LANES:(r + 1) * LANES], which)
            probs, _ = _attn_probs_t(_stack_heads(qr, kh), kb, mask, _sink_row(sink_ref, kh))
            for gp, out in enumerate(_unstack_pairs(_mm_tn(_bf(probs), vb))):
                pair = kh * (GROUP // 2) + gp
                y_ref[:, pair * LANES:(pair + 1) * LANES] = _bf(out)

    blk = lambda width, col: pl.BlockSpec((ATTN_BLOCK, width), lambda n: (n, col))
    tab = pl.BlockSpec((ATTN_BLOCK, LANES), lambda n: (n, 0))
    kvw = N_KV_HEADS * HEAD_DIM
    return pl.pallas_call(
        body, name="attn_fwd", grid=(nb,),
        in_specs=[blk(D_MODEL, COL_Q), blk(kvw, COL_K), blk(kvw, COL_V), tab, tab, tab,
                  pl.BlockSpec(memory_space=pltpu.SMEM)],
        out_specs=pl.BlockSpec((ATTN_BLOCK, D_MODEL), lambda n: (n, 0)),
        out_shape=jax.ShapeDtypeStruct((n_tok, D_MODEL), BF16),
        scratch_shapes=[pltpu.VMEM((ATTN_BLOCK, kvw), F32), pltpu.VMEM((ATTN_BLOCK, kvw), F32)],
        compiler_params=_params(("arbitrary",)),
    )(proj, proj, proj, *tabs, sinks)


def _attn_bwd(proj, dy, tabs, sinks, dproj, comm=None):
    n_tok = proj.shape[0]
    nb = n_tok // ATTN_BLOCK
    kvw = N_KV_HEADS * HEAD_DIM

    def body(q_ref, k_ref, v_ref, do_ref, c_ref, sa_ref, sb_ref, cp_ref, sap_ref, sbp_ref, sink_ref, _,
             dqkv_ref, dsink_ref, kprev, vprev, dkc, dvc, dqc):
        n = pl.program_id(0)

        @pl.when(n == 0)
        def _():
            for ref in (kprev, vprev, dkc, dvc, dqc, dsink_ref):
                ref[...] = jnp.zeros_like(ref)

        prev_tabs = (cp_ref[...], sap_ref[...], sbp_ref[...])

        @pl.when(n < nb)
        def _():
            c, sa, sb = c_ref[...], sa_ref[...], sb_ref[...]
            qr = _bf(_rope(q_ref[...], c, sa, sb))
            kr = _rope(k_ref[...], c, sa, sb)
            vc = v_ref[...]
            kk = jnp.concatenate([kprev[...], kr], axis=0)
            vv = jnp.concatenate([vprev[...], vc], axis=0)
            kprev[...] = kr
            vprev[...] = vc
            mask = _attn_mask_t(n)
            lane = lax.broadcasted_iota(jnp.int32, (1, LANES), 1)
            lo2 = lax.broadcasted_iota(jnp.int32, (2 * ATTN_BLOCK, LANES), 1) < HEAD_DIM
            dsink = jnp.zeros((1, LANES), F32)
            dq_pairs = []
            dk_full = []
            dv_full = []
            for kh in range(N_KV_HEADS):
                r, which = divmod(kh, 2)
                kb = _both_halves(kk[:, r * LANES:(r + 1) * LANES], which)
                vb = _both_halves(vv[:, r * LANES:(r + 1) * LANES], which)
                qs = _stack_heads(qr, kh)
                dos = _stack_heads(do_ref, kh)
                probs, p_sink = _attn_probs_t(qs, kb, mask, _sink_row(sink_ref, kh))
                dp = _mm_nt(vb, dos)
                delta = jnp.sum(probs * dp, axis=0, keepdims=True)
                ds = _bf(probs * (dp - delta) * (HEAD_DIM ** -0.5))
                sink_terms = p_sink * delta
                for i in range(GROUP):
                    head_sum = jnp.sum(sink_terms[:, i * ATTN_BLOCK:(i + 1) * ATTN_BLOCK])
                    dsink = dsink + jnp.where(lane == kh * GROUP + i, -head_sum, 0.0)
                dq_pairs += _unstack_pairs(_mm_tn(ds, kb))
                dk_acc = _mm(ds, qs)
                dv_acc = _mm(_bf(probs), dos)
                dk_full.append(dk_acc + pltpu.roll(dk_acc, HEAD_DIM, 1))
                dv_full.append(dv_acc + pltpu.roll(dv_acc, HEAD_DIM, 1))
            dk_pairs = [jnp.where(lo2, dk_full[2 * r], dk_full[2 * r + 1]) for r in range(N_KV_HEADS // 2)]
            dv_pairs = [jnp.where(lo2, dv_full[2 * r], dv_full[2 * r + 1]) for r in range(N_KV_HEADS // 2)]
            dsink_ref[...] += dsink
            dqkv_ref[:, :D_MODEL] = _bf(dqc[...])
            dqc[...] = _rope_transposed(jnp.concatenate(dq_pairs, axis=1), c, sa, sb)
            dk_all = jnp.concatenate(dk_pairs, axis=1)
            dv_all = jnp.concatenate(dv_pairs, axis=1)
            dqkv_ref[:, D_MODEL:D_MODEL + kvw] = _bf(_rope_transposed(dkc[...] + dk_all[:ATTN_BLOCK], *prev_tabs))
            dqkv_ref[:, D_MODEL + kvw:] = _bf(dvc[...] + dv_all[:ATTN_BLOCK])
            dkc[...] = dk_all[ATTN_BLOCK:]
            dvc[...] = dv_all[ATTN_BLOCK:]

        @pl.when(n == nb)
        def _():
            dqkv_ref[:, :D_MODEL] = _bf(dqc[...])
            dqkv_ref[:, D_MODEL:D_MODEL + kvw] = _bf(_rope_transposed(dkc[...], *prev_tabs))
            dqkv_ref[:, D_MODEL + kvw:] = _bf(dvc[...])

    cur = lambda n: jnp.minimum(n, nb - 1)
    prev = lambda n: jnp.maximum(n - 1, 0)
    blk = lambda width, col: pl.BlockSpec((ATTN_BLOCK, width), lambda n: (cur(n), col))
    tab = pl.BlockSpec((ATTN_BLOCK, LANES), lambda n: (cur(n), 0))
    tabp = pl.BlockSpec((ATTN_BLOCK, LANES), lambda n: (prev(n), 0))
    return _hosted(
        body, comm, name="attn_bwd", grid=(nb + 1,),
        in_specs=[blk(D_MODEL, COL_Q), blk(kvw, COL_K), blk(kvw, COL_V), blk(D_MODEL, 0), tab, tab, tab, tabp, tabp, tabp,
                  pl.BlockSpec(memory_space=pltpu.SMEM), pl.BlockSpec(memory_space=pl.ANY)],
        out_specs=[pl.BlockSpec((ATTN_BLOCK, ATTN_WIDTH), lambda n: (prev(n), COL_ATTN)),
                   pl.BlockSpec((1, LANES), lambda n: (0, 0))],
        out_shape=[jax.ShapeDtypeStruct(dproj.shape, dproj.dtype), jax.ShapeDtypeStruct((1, LANES), F32)],
        scratch_shapes=[pltpu.VMEM((ATTN_BLOCK, kvw), F32)] * 4 + [pltpu.VMEM((ATTN_BLOCK, D_MODEL), F32)],
        compiler_params=_params(("arbitrary",)), args=[proj, proj, proj, dy, *tabs, *tabs, sinks, dproj],
        aliases={11: 0})


def _bmm(a, b):
    return lax.dot_general(a, b, (((2,), (1,)), ((0,), (0,))), preferred_element_type=F32)


def _bmm_nt(a, b):
    return lax.dot_general(a, b, (((2,), (2,)), ((0,), (0,))), preferred_element_type=F32)


def _bmm_tn(a, b):
    return lax.dot_general(a, b, (((1,), (1,)), ((0,), (0,))), preferred_element_type=F32)


def _tril(cb, upper=False):
    shape = (cb, HGRN_CHUNK, HGRN_CHUNK)
    r, c = lax.broadcasted_iota(jnp.int32, shape, 1), lax.broadcasted_iota(jnp.int32, shape, 2)
    return (r <= c) if upper else (r >= c)


def _tri_matmul(x, upper):
    return lax.dot_general(_tril(x.shape[0], upper).astype(F32), x, (((2,), (1,)), ((0,), (0,))),
                           precision=lax.Precision.HIGHEST, preferred_element_type=F32)


@jax.custom_vjp
def _chunk_cumsum(x):
    return _tri_matmul(x, False)


_chunk_cumsum.defvjp(lambda x: (_tri_matmul(x, False), None), lambda _, g: (_tri_matmul(g, True),))


def _hg_elem(fl, qh, lb):
    f = lb + (1.0 - lb) * _sig(fl)
    k = 1.0 - f
    gc = _chunk_cumsum(jnp.log(f))
    last = lax.broadcasted_iota(jnp.int32, gc.shape, 1) == HGRN_CHUNK - 1
    g_last = jnp.sum(jnp.where(last, gc, 0.0), axis=1, keepdims=True)
    q = qh * _sig(qh)
    return q * jnp.exp(gc), k * jnp.exp(-gc), k * jnp.exp(g_last - gc), jnp.exp(g_last)


def _hg_out(q_dec, k_inv, v, st):
    sc = jnp.where(_tril(q_dec.shape[0]), _bmm_nt(_bf(q_dec), _bf(k_inv)), 0.0)
    return _bmm(_bf(sc), _bf(v)) + _bmm_nt(_bf(q_dec), _bf(st)), sc


def _hg_post(o, og, ng):
    on = o * lax.rsqrt(jnp.mean(o * o, axis=-1, keepdims=True) + RMS_EPS) * ng
    return on * (og * _sig(og))


def _hgrn_specs(n_tok, rev):
    nc = n_tok // HGRN_CHUNK
    cb = min(HGRN_CHUNKS_PER_STEP, nc)
    nt = nc // cb
    rows = cb * HGRN_CHUNK
    tt = (lambda t: nt - 1 - t) if rev else (lambda t: t)
    col = lambda base: pl.BlockSpec((rows, LANES), lambda h, t: (tt(t), base + h))
    head_cols = pl.BlockSpec((rows, HGRN_HEAD_WIDTH), lambda h, t: (tt(t), COL_HGRN + h))
    head_vec = pl.BlockSpec((1, LANES), lambda h, t: (0, h))
    one_vec = pl.BlockSpec((1, LANES), lambda h, t: (0, 0))
    state = pl.BlockSpec((1, cb, HGRN_DK, HGRN_DK), lambda h, t: (h, tt(t), 0, 0))
    return nc, cb, nt, col, head_cols, head_vec, one_vec, state


def _hgrn_fwd(proj, lb, ng):
    n_tok = proj.shape[0]
    nc, cb, nt, col, head_cols, head_vec, one_vec, state = _hgrn_specs(n_tok, False)

    def body(in_ref, lb_ref, ng_ref, y_ref, st_ref, s_acc):
        @pl.when(pl.program_id(1) == 0)
        def _():
            s_acc[...] = jnp.zeros_like(s_acc)

        fl, qh, v, og = (in_ref[:, i * LANES:(i + 1) * LANES].reshape(cb, HGRN_CHUNK, LANES) for i in range(4))
        q_dec, k_inv, k_end, decay = _hg_elem(fl, qh, lb_ref[...])
        upd = _bmm_tn(_bf(v), _bf(k_end))
        st = s_acc[...]
        for ci in range(cb):
            st_ref[0, ci] = st
            st = st * decay[ci] + upd[ci]
        s_acc[...] = st
        o, _ = _hg_out(q_dec, k_inv, v, st_ref[0])
        y_ref[...] = _bf(_hg_post(o, og, ng_ref[...]).reshape(cb * HGRN_CHUNK, LANES))

    return pl.pallas_call(
        body, name="hgrn_fwd", grid=(HGRN_HEADS, nt),
        in_specs=[head_cols, head_vec, one_vec],
        out_specs=[col(0), state],
        out_shape=[jax.ShapeDtypeStruct((n_tok, D_MODEL), BF16),
                   jax.ShapeDtypeStruct((HGRN_HEADS, nc, HGRN_DK, HGRN_DK), F32)],
        scratch_shapes=[pltpu.VMEM((HGRN_DK, HGRN_DK), F32)],
        compiler_params=_params(("arbitrary", "arbitrary")),
    )(proj, lb, ng)


def _hgrn_bwd(proj, lb, ng, states, dy, dproj):
    n_tok = proj.shape[0]
    nc, cb, nt, col, head_cols, head_vec, one_vec, state = _hgrn_specs(n_tok, True)

    def body(in_ref, lb_ref, ng_ref, st_ref, dy_ref, _, d_ref, dlb_ref, dng_ref, g_acc, g_all):
        h = pl.program_id(0)
        t = pl.program_id(1)

        @pl.when(t == 0)
        def _():
            g_acc[...] = jnp.zeros_like(g_acc)
            dlb_ref[...] = jnp.zeros_like(dlb_ref)

        @pl.when((t == 0) & (h == 0))
        def _():
            dng_ref[...] = jnp.zeros_like(dng_ref)

        fl, qh, v, og = (in_ref[:, i * LANES:(i + 1) * LANES].reshape(cb, HGRN_CHUNK, LANES) for i in range(4))
        (q_dec, k_inv, k_end, decay), elem_vjp = jax.vjp(_hg_elem, fl, qh, lb_ref[...])
        st = st_ref[0]
        o, sc = _hg_out(q_dec, k_inv, v, st)
        _, post_vjp = jax.vjp(_hg_post, o, og, ng_ref[...])
        do, dog, dng = post_vjp(dy_ref[...].reshape(cb, HGRN_CHUNK, LANES))
        dob, vb, qb = _bf(do), _bf(v), _bf(q_dec)
        dsc = _bf(jnp.where(_tril(cb), _bmm_nt(dob, vb), 0.0))
        p = _bmm_tn(dob, qb)
        g = g_acc[...]
        for ci in reversed(range(cb)):
            g_all[ci] = g
            g = g * decay[ci] + p[ci]
        g_acc[...] = g
        g = g_all[...]
        gb = _bf(g)
        dq_dec = _bmm(dsc, _bf(k_inv)) + _bmm(dob, _bf(st))
        dk_inv = _bmm_tn(dsc, qb)
        dv = _bmm_tn(_bf(sc), dob) + _bmm_nt(_bf(k_end), gb)
        dk_end = _bmm(vb, gb)
        ddecay = jnp.sum(st * g, axis=1, keepdims=True)
        dfl, dqh, dlb = elem_vjp((dq_dec, dk_inv, dk_end, ddecay))
        for i, val in enumerate((dfl, dqh, dv, dog)):
            d_ref[:, i * LANES:(i + 1) * LANES] = _bf(val.reshape(cb * HGRN_CHUNK, LANES))
        dlb_ref[...] += dlb
        dng_ref[...] += dng

    return pl.pallas_call(
        body, name="hgrn_bwd", grid=(HGRN_HEADS, nt),
        in_specs=[head_cols, head_vec, one_vec, state, col(0), pl.BlockSpec(memory_space=pl.ANY)],
        out_specs=[head_cols, head_vec, one_vec],
        out_shape=[jax.ShapeDtypeStruct(dproj.shape, dproj.dtype),
                   jax.ShapeDtypeStruct((1, D_MODEL), F32), jax.ShapeDtypeStruct((1, LANES), F32)],
        scratch_shapes=[pltpu.VMEM((HGRN_DK, HGRN_DK), F32), pltpu.VMEM((cb, HGRN_DK, HGRN_DK), F32)],
        compiler_params=_params(("arbitrary", "arbitrary")), input_output_aliases={5: 0},
    )(proj, lb, ng, states, dy, dproj)


def _lb_fwd(lb_logits):
    def lb_of(l0, l1):
        m = jnp.maximum(l0, l1)
        e0, e1 = jnp.exp(l0 - m), jnp.exp(l1 - m)
        return e0 / (e0 + e1)

    def body(l_ref, o_ref):
        o_ref[...] = lb_of(l_ref[0:1, :], l_ref[1:2, :])

    lb = pl.pallas_call(body, name="lb_fwd", out_shape=jax.ShapeDtypeStruct((1, D_MODEL), F32))(lb_logits)
    return lb, lb_of


def _mix_fwd(y_attn, y_hgrn, proj, x1, w_pa, w_ph, w_out, g, b):
    n = x1.shape[0]
    tm = min(MIX_TILE, n)

    def body(ya_ref, yh_ref, gt_ref, x_ref, wpa, wph, wo, g_ref, b_ref, z_ref, o_ref, pa_ref, ph_ref):
        ya = _mm(ya_ref[...], wpa[...])
        yh = _mm(yh_ref[...], wph[...])
        pa_ref[...] = _bf(ya)
        ph_ref[...] = _bf(yh)
        merged = _sig(gt_ref[:, :D_MODEL]) * ya + _sig(gt_ref[:, D_MODEL:]) * yh
        z = DEEPNORM_ALPHA * x_ref[...] + _mm(_bf(merged), wo[...])
        z_ref[...] = z
        o_ref[...] = _ln(z, g_ref[...], b_ref[...])

    row = pl.BlockSpec((tm, D_MODEL), lambda i: (i, 0))
    gates = pl.BlockSpec((tm, GATES_WIDTH), lambda i: (i, 0))
    sq = _resident((D_MODEL, D_MODEL))
    vec = _full((1, D_MODEL))
    return pl.pallas_call(
        body, name="mix_fwd", grid=(n // tm,),
        in_specs=[row, row, gates, row, sq, sq, sq, vec, vec],
        out_specs=[row, row, row, row],
        out_shape=[jax.ShapeDtypeStruct((n, D_MODEL), F32)] * 2 + [jax.ShapeDtypeStruct((n, D_MODEL), BF16)] * 2,
        compiler_params=_params(("arbitrary",)),
    )(y_attn, y_hgrn, proj, x1, w_pa, w_ph, w_out, g, b)


def _mix_bwd(dx2, z2, pa, ph, proj, w_pa, w_ph, w_out, g, b):
    n = z2.shape[0]
    tm = min(MIX_TILE, n)

    def body(do_ref, z_ref, ya_ref, yh_ref, gt_ref, wpa, wph, wo, g_ref, b_ref,
             dz_ref, dzb_ref, mg_ref, dya_ref, dyh_ref, dyat_ref, dyhg_ref, dgt_ref, dg_ref, db_ref):
        _, vjp = jax.vjp(_ln, z_ref[...], g_ref[...], b_ref[...])
        dz, dg, db = vjp(do_ref[...])

        @pl.when(pl.program_id(0) == 0)
        def _():
            dg_ref[...] = jnp.zeros_like(dg_ref)
            db_ref[...] = jnp.zeros_like(db_ref)

        dg_ref[...] += dg
        db_ref[...] += db
        dz_ref[...] = dz
        ya = ya_ref[...].astype(F32)
        yh = yh_ref[...].astype(F32)
        def merge(ga, gh, ya, yh):
            return _sig(ga) * ya + _sig(gh) * yh

        merged, merge_vjp = jax.vjp(merge, gt_ref[:, :D_MODEL], gt_ref[:, D_MODEL:], ya, yh)
        mg_ref[...] = _bf(merged)
        dzb = _bf(dz)
        dzb_ref[...] = dzb
        dmerged = _mm_nt(dzb, wo[...])
        dga, dgh, dya, dyh = merge_vjp(dmerged)
        dya = _bf(dya)
        dyh = _bf(dyh)
        dya_ref[...] = dya
        dyh_ref[...] = dyh
        dgt_ref[:, :D_MODEL] = _bf(dga)
        dgt_ref[:, D_MODEL:] = _bf(dgh)
        dyat_ref[...] = _bf(_mm_nt(dya, wpa[...]))
        dyhg_ref[...] = _mm_nt(dyh, wph[...])

    row = pl.BlockSpec((tm, D_MODEL), lambda i: (i, 0))
    gates = pl.BlockSpec((tm, GATES_WIDTH), lambda i: (i, 0))
    sq = _resident((D_MODEL, D_MODEL))
    vec = _full((1, D_MODEL))
    f32_row = jax.ShapeDtypeStruct((n, D_MODEL), F32)
    bf_row = jax.ShapeDtypeStruct((n, D_MODEL), BF16)
    vec_shape = jax.ShapeDtypeStruct((1, D_MODEL), F32)
    return pl.pallas_call(
        body, name="mix_bwd", grid=(n // tm,),
        in_specs=[row, row, row, row, gates, sq, sq, sq, vec, vec],
        out_specs=[row, row, row, row, row, row, row, gates, vec, vec],
        out_shape=[f32_row, bf_row, bf_row, bf_row, bf_row, bf_row, f32_row,
                   jax.ShapeDtypeStruct((n, D_IN), BF16), vec_shape, vec_shape],
        compiler_params=_params(("arbitrary",)),
    )(dx2, z2, pa, ph, proj, w_pa, w_ph, w_out, g, b)


def _position():
    x, y, c = lax.axis_index("x"), lax.axis_index("y"), lax.axis_index("c")
    chips = [(1 - x, y), (x, 1 - y), (1 - x, 1 - y)]
    return x, y, c, chips


def _any_specs(k):
    return [pl.BlockSpec(memory_space=pl.ANY)] * k


class _GatherWeights:
    def __init__(self, shards):
        nw = len(shards)
        self.inputs = list(shards)
        self.out_shape = [jax.ShapeDtypeStruct((N_CHIPS, *s.shape), s.dtype) for s in shards]
        self.scratch = [pltpu.SemaphoreType.DMA((nw,)), pltpu.SemaphoreType.DMA((nw * 6,)),
                        pltpu.SemaphoreType.DMA((nw * 6,))]

    def _copies(self, ins, outs, sems):
        nw = len(ins)
        local_sem, send_sem, recv_sem = sems
        x, y, c, chips = _position()
        me = 2 * x + y
        sibling = (x, y, 1 - c)
        half_rows = [s.shape[0] // 2 for s in self.inputs]

        def half(w, chip_idx, which):
            return outs[w].at[chip_idx, pl.ds(which * half_rows[w], half_rows[w])]

        def remote(w, k, src, dst, to):
            return pltpu.make_async_remote_copy(src_ref=src, dst_ref=dst, send_sem=send_sem.at[w * 6 + k],
                                                recv_sem=recv_sem.at[w * 6 + k], device_id=to, device_id_type=MESH)

        local = [pltpu.make_async_copy(ins[w], outs[w].at[me], local_sem.at[w]) for w in range(nw)]
        first = [remote(w, j, ins[w].at[pl.ds(c * half_rows[w], half_rows[w])], half(w, me, c), (px, py, c))
                 for w in range(nw) for j, (px, py) in enumerate(chips)]
        landed = [half(w, 2 * px + py, c) for w in range(nw) for (px, py) in chips]
        arrive = [remote(w, j, landed[w * 3 + j], landed[w * 3 + j], (px, py, c))
                  for w in range(nw) for j, (px, py) in enumerate(chips)]
        passed = [remote(w, 3 + j, landed[w * 3 + j], landed[w * 3 + j], sibling) for w in range(nw) for j in range(3)]
        from_sibling = [remote(w, 3 + j, half(w, 2 * px + py, 1 - c), half(w, 2 * px + py, 1 - c), sibling)
                        for w in range(nw) for j, (px, py) in enumerate(chips)]
        return local, first, arrive, passed, from_sibling

    def start(self, ins, outs, sems):
        local, first, _, _, _ = self._copies(ins, outs, sems)
        for cp in local + first:
            cp.start()

    def finish(self, ins, outs, sems):
        local, first, arrive, passed, from_sibling = self._copies(ins, outs, sems)
        for cp_in, cp_on in zip(arrive, passed):
            cp_in.wait_recv()
            cp_on.start()
        for cp in from_sibling:
            cp.wait_recv()
        for cp in first + passed:
            cp.wait_send()
        for cp in local:
            cp.wait()


class _ExchangeGrads:
    def __init__(self, grads):
        nw = len(grads)
        self.inputs = list(grads)
        self.out_shape = [jax.ShapeDtypeStruct(g.shape, g.dtype) for g in grads]
        self.scratch = [pltpu.SemaphoreType.DMA((nw,)), pltpu.SemaphoreType.DMA((nw * 3,)),
                        pltpu.SemaphoreType.DMA((nw * 3,))]

    def _copies(self, ins, outs, sems):
        nw = len(ins)
        local_sem, send_sem, recv_sem = sems
        x, y, c, chips = _position()
        me = 2 * x + y

        def remote(w, j, src, dst, chip):
            return pltpu.make_async_remote_copy(src_ref=src, dst_ref=dst, send_sem=send_sem.at[w * 3 + j],
                                                recv_sem=recv_sem.at[w * 3 + j], device_id=(*chip, c),
                                                device_id_type=MESH)

        local = [pltpu.make_async_copy(ins[w].at[me], outs[w].at[me], local_sem.at[w]) for w in range(nw)]
        sends = [remote(w, j, ins[w].at[2 * px + py], outs[w].at[me], (px, py))
                 for w in range(nw) for j, (px, py) in enumerate(chips)]
        arrive = [remote(w, j, outs[w].at[2 * px + py], outs[w].at[2 * px + py], (px, py))
                  for w in range(nw) for j, (px, py) in enumerate(chips)]
        return local, sends, arrive

    def start(self, ins, outs, sems):
        local, sends, _ = self._copies(ins, outs, sems)
        for cp in local + sends:
            cp.start()

    def finish(self, ins, outs, sems):
        local, sends, arrive = self._copies(ins, outs, sems)
        for cp in arrive:
            cp.wait_recv()
        for cp in sends:
            cp.wait_send()
        for cp in local:
            cp.wait()


def _hosted(body, comm, *, name, grid, in_specs, out_specs, out_shape, scratch_shapes, compiler_params, args,
            aliases=None):
    aliases = aliases or {}
    if comm is None:
        res = pl.pallas_call(body, name=name, grid=grid, in_specs=in_specs, out_specs=out_specs, out_shape=out_shape,
                             scratch_shapes=scratch_shapes, compiler_params=compiler_params,
                             input_output_aliases=aliases)(*args)
        return list(res), []
    n_in, n_out, n_scr = len(in_specs), len(out_specs), len(scratch_shapes)
    c_in, c_out = len(comm.inputs), len(comm.out_shape)

    def hosted_body(*refs):
        refs = list(refs)
        cut = lambda k: (refs[:k], refs[k:])
        main_in, refs = cut(n_in)
        comm_in, refs = cut(c_in)
        main_out, refs = cut(n_out)
        comm_out, refs = cut(c_out)
        main_scr, comm_scr = cut(n_scr)
        ids = [pl.program_id(a) for a in range(len(grid))]
        first = functools.reduce(jnp.logical_and, [i == 0 for i in ids])
        last = functools.reduce(jnp.logical_and, [i == g - 1 for i, g in zip(ids, grid)])

        @pl.when(first)
        def _():
            comm.start(comm_in, comm_out, comm_scr)

        body(*main_in, *main_out, *main_scr)

        @pl.when(last)
        def _():
            comm.finish(comm_in, comm_out, comm_scr)

    res = pl.pallas_call(
        hosted_body, name=name, grid=grid, in_specs=[*in_specs, *_any_specs(c_in)],
        out_specs=[*out_specs, *_any_specs(c_out)], out_shape=[*out_shape, *comm.out_shape],
        scratch_shapes=[*scratch_shapes, *comm.scratch], compiler_params=compiler_params,
        input_output_aliases=aliases,
    )(*args, *comm.inputs)
    return list(res[:n_out]), list(res[n_out:])


def _run_comm(name, comm):
    def body(*refs):
        refs = list(refs)
        c_in, c_out = len(comm.inputs), len(comm.out_shape)
        ins, outs, sems = refs[:c_in], refs[c_in:c_in + c_out], refs[c_in + c_out:]
        comm.start(ins, outs, sems)
        comm.finish(ins, outs, sems)

    return list(pl.pallas_call(
        body, name=name, in_specs=_any_specs(len(comm.inputs)), out_specs=_any_specs(len(comm.out_shape)),
        out_shape=comm.out_shape, scratch_shapes=comm.scratch,
    )(*comm.inputs))


def _sum_slots(name, slots):
    k = len(slots)
    _, rows, cols = slots[0].shape
    tr = _update_rows(rows)

    def body(*refs):
        for s_ref, o_ref in zip(refs[:k], refs[k:]):
            acc = s_ref[0].astype(F32)
            for i in range(1, N_CHIPS):
                acc = acc + s_ref[i].astype(F32)
            o_ref[...] = acc

    return pl.pallas_call(
        body, name=name, grid=(rows // tr,),
        in_specs=[pl.BlockSpec((N_CHIPS, tr, cols), lambda i: (0, i, 0))] * k,
        out_specs=[pl.BlockSpec((tr, cols), lambda i: (i, 0))] * k,
        out_shape=[jax.ShapeDtypeStruct((rows, cols), F32)] * k,
        compiler_params=_params(("arbitrary",)),
    )(*slots)


class _SwapWithSibling:
    def __init__(self, parts):
        self.inputs = list(parts)
        self.out_shape = [jax.ShapeDtypeStruct(p.shape, p.dtype) for p in parts]
        self.scratch = [pltpu.SemaphoreType.DMA((len(parts),)), pltpu.SemaphoreType.DMA((len(parts),))]

    def _copies(self, ins, outs, sems):
        send_sem, recv_sem = sems
        x, y, c, _ = _position()
        return [pltpu.make_async_remote_copy(src_ref=ins[w], dst_ref=outs[w], send_sem=send_sem.at[w],
                                             recv_sem=recv_sem.at[w], device_id=(x, y, 1 - c), device_id_type=MESH)
                for w in range(len(ins))]

    def start(self, ins, outs, sems):
        for cp in self._copies(ins, outs, sems):
            cp.start()

    def finish(self, ins, outs, sems):
        for cp in self._copies(ins, outs, sems):
            cp.wait()


class _Together:
    def __init__(self, first, second):
        self.parts = (first, second)
        self.inputs = first.inputs + second.inputs
        self.out_shape = first.out_shape + second.out_shape
        self.scratch = first.scratch + second.scratch

    def _split(self, ins, outs, sems):
        a = self.parts[0]
        ni, no, ns = len(a.inputs), len(a.out_shape), len(a.scratch)
        return ((ins[:ni], outs[:no], sems[:ns]), (ins[ni:], outs[no:], sems[ns:]))

    def start(self, ins, outs, sems):
        for part, args in zip(self.parts, self._split(ins, outs, sems)):
            part.start(*args)

    def finish(self, ins, outs, sems):
        for part, args in zip(self.parts, self._split(ins, outs, sems)):
            part.finish(*args)


def _sum_small(part):
    def body(p_ref, o_ref, buf, send_sem, recv_sem):
        x, y, c, _ = _position()
        me = 4 * x + 2 * y + c
        buf[me] = p_ref[...]
        copies = []
        for k in range(1, N_DEV):
            peer = tuple(1 - v if (k >> s) & 1 else v for v, s in ((x, 2), (y, 1), (c, 0)))
            copies.append(pltpu.make_async_remote_copy(src_ref=p_ref, dst_ref=buf.at[me], send_sem=send_sem.at[k - 1],
                                                       recv_sem=recv_sem.at[k - 1], device_id=peer, device_id_type=MESH))
        for cp in copies:
            cp.start()
        for cp in copies:
            cp.wait()
        acc = buf[0]
        for d in range(1, N_DEV):
            acc = acc + buf[d]
        o_ref[...] = acc

    vm = pl.BlockSpec(memory_space=pltpu.VMEM)
    return pl.pallas_call(
        body, name="sum_small", in_specs=[vm], out_specs=vm,
        out_shape=jax.ShapeDtypeStruct((1, SM_LEN), F32),
        scratch_shapes=[pltpu.VMEM((N_DEV, 1, SM_LEN), F32), pltpu.SemaphoreType.DMA((N_DEV - 1,)),
                        pltpu.SemaphoreType.DMA((N_DEV - 1,))],
    )(part)


def _adamw(w, g, m, v):
    m = ADAM_B1 * m + (1.0 - ADAM_B1) * g
    v = ADAM_B2 * v + (1.0 - ADAM_B2) * (g * g)
    m_hat = m / (1.0 - ADAM_B1 ** ADAM_STEP)
    v_hat = v / (1.0 - ADAM_B2 ** ADAM_STEP)
    delta = -ADAM_LR * (m_hat / (jnp.sqrt(v_hat) + ADAM_EPS) + ADAM_WD * w)
    return delta, m, v


def _adam_big(name, groups):
    k = len(groups)
    rows, cols = groups[0][2].shape
    tr = _update_rows(rows)

    def body(*refs):
        for i in range(k):
            p_ref, q_ref, w_ref, m_ref, v_ref = refs[5 * i:5 * i + 5]
            g_ref, d_ref, nm_ref, nv_ref = refs[5 * k + 4 * i:5 * k + 4 * i + 4]
            g = p_ref[...] + q_ref[...]
            g_ref[...] = g
            d_ref[...], nm_ref[...], nv_ref[...] = _adamw(w_ref[...], g, m_ref[...], v_ref[...])

    spec = pl.BlockSpec((tr, cols), lambda i: (i, 0))
    res = pl.pallas_call(
        body, name=name, grid=(rows // tr,), in_specs=[spec] * (5 * k), out_specs=[spec] * (4 * k),
        out_shape=[jax.ShapeDtypeStruct((rows, cols), F32)] * (4 * k),
        compiler_params=_params(("arbitrary",)),
    )(*[a for grp in groups for a in grp])
    return [res[4 * i:4 * i + 4] for i in range(k)]


_SMALL_AT = {"ln1_g": 0, "ln1_b": D_MODEL, "ln2_g": 2 * D_MODEL, "ln2_b": 3 * D_MODEL, "ln3_g": 4 * D_MODEL,
             "ln3_b": 5 * D_MODEL, "b_in": SM_BIN, "attn_sinks": SM_SINK, "hgrn_norm_g": SM_NG}


def _adam_small(total, w, m, v, lb_of):
    names = list(_SMALL)
    k = len(names)

    def body(*refs):
        t_ref = refs[0]
        w_refs, m_refs, v_refs = (refs[1 + i * k:1 + (i + 1) * k] for i in range(3))
        g_refs, d_refs, nm_refs, nv_refs = (refs[1 + (3 + i) * k:1 + (4 + i) * k] for i in range(4))
        for i, name in enumerate(names):
            if name == "hgrn_lb_logits":
                _, vjp = jax.vjp(lb_of, w_refs[i][0:1, :], w_refs[i][1:2, :])
                g_refs[i][0:1, :], g_refs[i][1:2, :] = vjp(t_ref[:, SM_LB:SM_LOSS])
            else:
                at = _SMALL_AT[name]
                g_refs[i][...] = t_ref[:, at:at + w_refs[i].shape[1]]
            d_refs[i][...], nm_refs[i][...], nv_refs[i][...] = _adamw(
                w_refs[i][...], g_refs[i][...], m_refs[i][...], v_refs[i][...])

    shapes = [jax.ShapeDtypeStruct(w[name].shape, F32) for name in names]
    res = pl.pallas_call(body, name="adam_small", out_shape=shapes * 4)(
        total, *[w[n] for n in names], *[m[n] for n in names], *[v[n] for n in names])
    return [dict(zip(names, res[i * k:(i + 1) * k])) for i in range(4)]


_BIG = ("ffn1_w1", "ffn1_w3", "ffn1_w2", "w_in", "w_proj_attn", "w_proj_hgrn", "w_out", "ffn2_w1", "ffn2_w3", "ffn2_w2")
_SMALL = ("ln1_g", "ln1_b", "ln2_g", "ln2_b", "ln3_g", "ln3_b", "b_in", "attn_sinks", "hgrn_norm_g", "hgrn_lb_logits")
_ORDER = ("ln1_g", "ln1_b", "ffn1_w1", "ffn1_w3", "ffn1_w2", "ln2_g", "ln2_b", "w_in", "b_in", "attn_sinks",
          "hgrn_lb_logits", "hgrn_norm_g", "w_proj_attn", "w_proj_hgrn", "w_out", "ln3_g", "ln3_b",
          "ffn2_w1", "ffn2_w3", "ffn2_w2")


_TRANSPOSED = ("ffn1_w1", "ffn1_w3", "ffn2_w1", "ffn2_w3")


def _local_view(name, arr):
    return arr[0].T if name in _TRANSPOSED else arr[0]


def _ffn_grad(name, hidden, other, comm=None):
    (dw,), comm_out = _grad_matmul(name, hidden, "cols", D_FF // FF_GRAD_PARTS, other, "shared", D_MODEL,
                                   comm=comm, parts=FF_GRAD_PARTS)
    return dw.reshape(N_CHIPS, D_FF // N_CHIPS, D_MODEL), comm_out


def kernel(x, ln1_g, ln1_b, ffn1_w1, ffn1_w3, ffn1_w2, ln2_g, ln2_b, w_in, b_in, attn_sinks, hgrn_lb_logits, hgrn_norm_g, w_proj_attn, w_proj_hgrn, w_out, ln3_g, ln3_b, ffn2_w1, ffn2_w3, ffn2_w2, loss_target, m_ln1_g, m_ln1_b, m_ffn1_w1, m_ffn1_w3, m_ffn1_w2, m_ln2_g, m_ln2_b, m_w_in, m_b_in, m_attn_sinks, m_hgrn_lb_logits, m_hgrn_norm_g, m_w_proj_attn, m_w_proj_hgrn, m_w_out, m_ln3_g, m_ln3_b, m_ffn2_w1, m_ffn2_w3, m_ffn2_w2, v_ln1_g, v_ln1_b, v_ffn1_w1, v_ffn1_w3, v_ffn1_w2, v_ln2_g, v_ln2_b, v_w_in, v_b_in, v_attn_sinks, v_hgrn_lb_logits, v_hgrn_norm_g, v_w_proj_attn, v_w_proj_hgrn, v_w_out, v_ln3_g, v_ln3_b, v_ffn2_w1, v_ffn2_w3, v_ffn2_w2):
    w = dict(ln1_g=ln1_g, ln1_b=ln1_b, ffn1_w1=ffn1_w1, ffn1_w3=ffn1_w3, ffn1_w2=ffn1_w2, ln2_g=ln2_g, ln2_b=ln2_b,
             w_in=w_in, b_in=b_in, attn_sinks=attn_sinks, hgrn_lb_logits=hgrn_lb_logits, hgrn_norm_g=hgrn_norm_g,
             w_proj_attn=w_proj_attn, w_proj_hgrn=w_proj_hgrn, w_out=w_out, ln3_g=ln3_g, ln3_b=ln3_b,
             ffn2_w1=ffn2_w1, ffn2_w3=ffn2_w3, ffn2_w2=ffn2_w2)
    mom = dict(ln1_g=m_ln1_g, ln1_b=m_ln1_b, ffn1_w1=m_ffn1_w1, ffn1_w3=m_ffn1_w3, ffn1_w2=m_ffn1_w2, ln2_g=m_ln2_g,
               ln2_b=m_ln2_b, w_in=m_w_in, b_in=m_b_in, attn_sinks=m_attn_sinks, hgrn_lb_logits=m_hgrn_lb_logits,
               hgrn_norm_g=m_hgrn_norm_g, w_proj_attn=m_w_proj_attn, w_proj_hgrn=m_w_proj_hgrn, w_out=m_w_out,
               ln3_g=m_ln3_g, ln3_b=m_ln3_b, ffn2_w1=m_ffn2_w1, ffn2_w3=m_ffn2_w3, ffn2_w2=m_ffn2_w2)
    var = dict(ln1_g=v_ln1_g, ln1_b=v_ln1_b, ffn1_w1=v_ffn1_w1, ffn1_w3=v_ffn1_w3, ffn1_w2=v_ffn1_w2, ln2_g=v_ln2_g,
               ln2_b=v_ln2_b, w_in=v_w_in, b_in=v_b_in, attn_sinks=v_attn_sinks, hgrn_lb_logits=v_hgrn_lb_logits,
               hgrn_norm_g=v_hgrn_norm_g, w_proj_attn=v_w_proj_attn, w_proj_hgrn=v_w_proj_hgrn, w_out=v_w_out,
               ln3_g=v_ln3_g, ln3_b=v_ln3_b, ffn2_w1=v_ffn2_w1, ffn2_w3=v_ffn2_w3, ffn2_w2=v_ffn2_w2)

    n_tok = x.shape[1]
    x0 = x.reshape(n_tok, D_MODEL)
    target = loss_target.reshape(n_tok, D_MODEL)

    shard = {k: _bf(_local_view(k, w[k])) for k in _BIG}
    gather = lambda keys: _GatherWeights([shard[k] for k in keys])
    slots = {}
    exchange = lambda keys: _ExchangeGrads([big[k] for k in keys])
    ffn1_keys = ("ffn1_w1", "ffn1_w3", "ffn1_w2")
    mixer_keys = ("w_in", "w_proj_attn", "w_proj_hgrn", "w_out")
    ffn2_keys = ("ffn2_w1", "ffn2_w3", "ffn2_w2")
    whole = lambda ts: [t.reshape(D_FF, D_MODEL) for t in ts]
    f1 = whole(_run_comm("gather_ffn1", gather(ffn1_keys)))

    tabs = _rope_tables(n_tok)
    lb, lb_of = _lb_fwd(hgrn_lb_logits)
    (z1, x1, x0b, h1_1, h3_1, x1b), (w_in_g, w_pa, w_ph, w_o) = _ffn_fwd(
        "ffn1_fwd", x0, *f1, ln1_g, ln1_b, comm=gather(mixer_keys))
    w_pa, w_ph, w_o = (t.reshape(D_MODEL, D_MODEL) for t in (w_pa, w_ph, w_o))
    w_in_g = _reorder_w_in("w_in_cols", w_in_g, True)
    (proj,), f2 = _in_proj(x1b, w_in_g, _to_kernel_cols(b_in), comm=gather(ffn2_keys))
    f2 = whole(f2)
    y_attn = _attn_fwd(proj, tabs, attn_sinks)
    y_hgrn, states = _hgrn_fwd(proj, lb, hgrn_norm_g)
    z2, x2, proj_a, proj_h = _mix_fwd(y_attn, y_hgrn, proj, x1, w_pa, w_ph, w_o, ln2_g, ln2_b)
    (z3, dy, x2b, h1_2, h3_2, loss_part), _ = _ffn_fwd("ffn2_fwd", x2, *f2, ln3_g, ln3_b, target=target)

    big = {}
    small = {}
    (dx2, a2, dh1_2, dh3_2, df2, small["ln3_g"], small["ln3_b"]), _ = _ffn_bwd(
        "ffn2_bwd", h1_2, h3_2, z3, dy, *f2, ln3_g, ln3_b)
    big["ffn2_w1"], _ = _ffn_grad("ffn2_dw1", dh1_2, x2b)
    big["ffn2_w3"], _ = _ffn_grad("ffn2_dw3", dh3_2, x2b)
    big["ffn2_w2"], _ = _ffn_grad("ffn2_dw2", a2, df2)
    (dz2, dz2b, merged, dya, dyh, dy_attn, dy_hgrn, dproj, small["ln2_g"], small["ln2_b"]) = _mix_bwd(
        dx2, z2, proj_a, proj_h, proj, w_pa, w_ph, w_o, ln2_g, ln2_b)
    for key, name, lhs, rhs in (("w_out", "dw_out", merged, dz2b), ("w_proj_attn", "dw_proj_attn", y_attn, dya),
                                ("w_proj_hgrn", "dw_proj_hgrn", y_hgrn, dyh)):
        (dw,), _ = _grad_matmul(name, lhs, "cols", D_MODEL, rhs, "shared", D_MODEL, parts=1)
        big[key] = dw.reshape(N_CHIPS, PROJ_SHARD, D_MODEL)
    dproj, dlb, small["hgrn_norm_g"] = _hgrn_bwd(proj, lb, hgrn_norm_g, states, dy_hgrn, dproj)
    early_keys = ffn2_keys + mixer_keys[1:]
    (dproj, dsinks), early = _attn_bwd(proj, dy_attn, tabs, attn_sinks, dproj, comm=exchange(early_keys))
    slots.update(zip(early_keys, early))
    (dw_in, db_in), _ = _grad_matmul("dw_in", x1b, "shared", D_MODEL, dproj, "cols", IN_SHARD, colsum=True)
    big["w_in"] = _reorder_w_in("dw_in_cols", dw_in, False)
    small["b_in"] = _from_kernel_cols(db_in)
    (dx1,), (slots["w_in"],) = _in_proj_dx(dproj, w_in_g, dz2, comm=exchange(("w_in",)))
    (grad_x, a1, dh1_1, dh3_1, df1, small["ln1_g"], small["ln1_b"]), _ = _ffn_bwd(
        "ffn1_bwd", h1_1, h3_1, z1, dx1, *f1, ln1_g, ln1_b)
    big["ffn1_w1"], _ = _ffn_grad("ffn1_dw1", dh1_1, x0b)
    big["ffn1_w3"], (slots["ffn1_w1"],) = _ffn_grad("ffn1_dw3", dh3_1, x0b, comm=exchange(("ffn1_w1",)))
    big["ffn1_w2"], (slots["ffn1_w3"],) = _ffn_grad("ffn1_dw2", a1, df1, comm=exchange(("ffn1_w3",)))

    last = "ffn1_w2"
    groups = [[k for k in ffn1_keys + ffn2_keys if k != last], list(mixer_keys[1:]), ["w_in"]]
    partial = {}
    for keys in groups:
        partial.update(zip(keys, _sum_slots("sum_" + keys[0], [slots[k] for k in keys])))
    swapped_keys = [k for keys in groups for k in keys]
    moved = _run_comm("swap_and_exchange_last",
                      _Together(_SwapWithSibling([partial[k] for k in swapped_keys]), exchange((last,))))
    from_sibling = dict(zip(swapped_keys, moved))
    (partial[last],) = _sum_slots("sum_" + last, [moved[-1]])
    (from_sibling[last],) = _run_comm("swap_last", _SwapWithSibling([partial[last]]))
    groups[0].append(last)

    outs = {"grad": {}, "delta": {}, "m": {}, "v": {}}
    for keys in groups:
        res = _adam_big("adam_" + keys[0], [(partial[k], from_sibling[k], _local_view(k, w[k]), _local_view(k, mom[k]),
                                             _local_view(k, var[k])) for k in keys])
        for k, four in zip(keys, res):
            for kind, r in zip(("grad", "delta", "m", "v"), four):
                outs[kind][k] = (r.T if k in _TRANSPOSED else r).reshape(w[k].shape)

    total = _sum_small(jnp.concatenate(
        [small[k] for k in ("ln1_g", "ln1_b", "ln2_g", "ln2_b", "ln3_g", "ln3_b", "b_in")]
        + [dsinks, small["hgrn_norm_g"], dlb, loss_part], axis=1))
    for kind, r in zip(("grad", "delta", "m", "v"), _adam_small(total, w, mom, var, lb_of)):
        outs[kind].update(r)
    loss = total[0, SM_LOSS]

    return (loss, grad_x.reshape(x.shape), *[outs["grad"][k] for k in _ORDER], *[outs["delta"][k] for k in _ORDER],
            *[outs["m"][k] for k in _ORDER], *[outs["v"][k] for k in _ORDER])
```

```python
import functools

import jax
import jax.numpy as jnp
from jax import lax
from jax.experimental import pallas as pl
from jax.experimental.pallas import tpu as pltpu

F32 = jnp.float32
BF16 = jnp.bfloat16

D_MODEL = 1024
N_Q_HEADS = 16
N_KV_HEADS = 4
HEAD_DIM = 64
ATTN_BLOCK = 128
ROPE_THETA = 500000.0
ROPE_DIM = HEAD_DIM // 4
HGRN_HEADS = 8
HGRN_DK = 128
HGRN_CHUNK = 64
D_FF = 2816
D_IN = 7680
DEEPNORM_ALPHA = 2 ** 0.25
LN_EPS = 1e-5
RMS_EPS = 1e-6
NEG_INF = -1e30

ADAM_LR = 0.001
ADAM_B1 = 0.9
ADAM_B2 = 0.999
ADAM_EPS = 1e-08
ADAM_WD = 0.01
ADAM_STEP = 10

N_CHIPS = 4
N_DEV = 8
LANES = 128
FF_GRAD_PARTS = 2
FFN_TILE = 256
IN_SHARD = D_IN // N_CHIPS
PROJ_SHARD = D_MODEL // N_CHIPS
ROW_TILE = 512
GRAD_ROWS = 2048
MIX_TILE = 256
UPDATE_ROWS = 128
HGRN_CHUNKS_PER_STEP = 16
VMEM_LIMIT = 56 * 1024 * 1024

GATES_WIDTH = 2 * D_MODEL
HGRN_HEAD_WIDTH = 4 * HGRN_DK
ATTN_WIDTH = D_MODEL + 2 * N_KV_HEADS * HEAD_DIM
COL_HGRN = GATES_WIDTH // HGRN_HEAD_WIDTH
COL_ATTN = (GATES_WIDTH + HGRN_HEADS * HGRN_HEAD_WIDTH) // ATTN_WIDTH
COL_Q = (GATES_WIDTH + HGRN_HEADS * HGRN_HEAD_WIDTH) // D_MODEL
COL_K = (GATES_WIDTH + HGRN_HEADS * HGRN_HEAD_WIDTH + D_MODEL) // (N_KV_HEADS * HEAD_DIM)
COL_V = COL_K + 1


def _to_kernel_cols(a):
    lead = a.shape[:-1]
    qkv, hg, gates = a[..., :ATTN_WIDTH], a[..., ATTN_WIDTH:D_IN - GATES_WIDTH], a[..., D_IN - GATES_WIDTH:]
    hg = jnp.swapaxes(hg.reshape(*lead, 4, HGRN_HEADS, HGRN_DK), -3, -2).reshape(*lead, -1)
    return jnp.concatenate([gates, hg, qkv], axis=-1)


def _from_kernel_cols(a):
    lead = a.shape[:-1]
    gates, hg, qkv = a[..., :GATES_WIDTH], a[..., GATES_WIDTH:D_IN - ATTN_WIDTH], a[..., D_IN - ATTN_WIDTH:]
    hg = jnp.swapaxes(hg.reshape(*lead, HGRN_HEADS, 4, HGRN_DK), -3, -2).reshape(*lead, -1)
    return jnp.concatenate([qkv, hg, gates], axis=-1)

SM_LN = 0
SM_BIN = 6 * D_MODEL
SM_SINK = SM_BIN + D_IN
SM_NG = SM_SINK + LANES
SM_LB = SM_NG + LANES
SM_LOSS = SM_LB + D_MODEL
SM_LEN = SM_LOSS + LANES

MESH = pl.DeviceIdType.MESH


def _mm(a, b):
    return lax.dot_general(a, b, (((1,), (0,)), ((), ())), preferred_element_type=F32)


def _mm_nt(a, b):
    return lax.dot_general(a, b, (((1,), (1,)), ((), ())), preferred_element_type=F32)


def _mm_tn(a, b):
    return lax.dot_general(a, b, (((0,), (0,)), ((), ())), preferred_element_type=F32)


def _bf(v):
    return v.astype(BF16)


def _sig(v):
    return jax.nn.sigmoid(v)


def _ln(z, g, b):
    mu = jnp.mean(z, axis=-1, keepdims=True)
    zc = z - mu
    var = jnp.mean(zc * zc, axis=-1, keepdims=True)
    return zc * lax.rsqrt(var + LN_EPS) * g + b


def _swiglu_act(h1, h3):
    return (h1 * _sig(h1)) * h3


def _params(sem=None):
    return pltpu.CompilerParams(dimension_semantics=sem, vmem_limit_bytes=VMEM_LIMIT)


def _full(shape):
    nd = len(shape)
    return pl.BlockSpec(shape, lambda *_: (0,) * nd)


def _update_rows(rows):
    return max(t for t in range(8, UPDATE_ROWS + 1, 8) if rows % t == 0)


def _resident(shape):
    nd = len(shape)
    return pl.BlockSpec(shape, lambda *_: (0,) * nd, pipeline_mode=pl.Buffered(1))


def _ffn_fwd(name, x, w1t, w3t, w2, g, b, target=None, comm=None):
    n = x.shape[0]
    tm = min(FFN_TILE, n)
    final = target is not None

    def body(*refs):
        if final:
            x_ref, w1_ref, w3_ref, w2_ref, g_ref, b_ref, t_ref, z_ref, o_ref, xb_ref, h1_ref, h3_ref, loss_ref = refs
        else:
            x_ref, w1_ref, w3_ref, w2_ref, g_ref, b_ref, z_ref, o_ref, xb_ref, h1_ref, h3_ref, ob_ref = refs
        xb = _bf(x_ref[...])
        xb_ref[...] = xb
        h1 = _mm_nt(xb, w1_ref[...])
        h3 = _mm_nt(xb, w3_ref[...])
        h1_ref[...] = _bf(h1)
        h3_ref[...] = _bf(h3)
        z = DEEPNORM_ALPHA * x_ref[...] + 0.5 * _mm(_bf(_swiglu_act(h1, h3)), w2_ref[...])
        z_ref[...] = z
        y = _ln(z, g_ref[...], b_ref[...])
        if final:
            e = y - t_ref[...]

            @pl.when(pl.program_id(0) == 0)
            def _():
                loss_ref[...] = jnp.zeros_like(loss_ref)

            loss_ref[...] += jnp.sum(e * e) * (0.5 / D_MODEL)
            o_ref[...] = e * (1.0 / D_MODEL)
        else:
            o_ref[...] = y
            ob_ref[...] = _bf(y)

    row = pl.BlockSpec((tm, D_MODEL), lambda i: (i, 0))
    hid = pl.BlockSpec((tm, D_FF), lambda i: (i, 0))
    wres = _resident((D_FF, D_MODEL))
    vec = _full((1, D_MODEL))
    in_specs = [row, wres, wres, wres, vec, vec]
    args = [x, w1t, w3t, w2, g, b]
    hid_shape = jax.ShapeDtypeStruct((n, D_FF), BF16)
    out_specs = [row, row, row, hid, hid]
    out_shape = ([jax.ShapeDtypeStruct((n, D_MODEL), F32)] * 2 + [jax.ShapeDtypeStruct((n, D_MODEL), BF16)]
                 + [hid_shape] * 2)
    if final:
        in_specs.append(row)
        args.append(target)
        out_specs.append(_full((1, LANES)))
        out_shape.append(jax.ShapeDtypeStruct((1, LANES), F32))
    else:
        out_specs.append(row)
        out_shape.append(jax.ShapeDtypeStruct((n, D_MODEL), BF16))
    return _hosted(
        body, comm, name=name, grid=(n // tm,), in_specs=in_specs, out_specs=out_specs, out_shape=out_shape,
        scratch_shapes=[], compiler_params=_params(("arbitrary",)), args=args)


def _ffn_bwd(name, h1s, h3s, z, dout, w1t, w3t, w2, g, b, comm=None):
    n = z.shape[0]
    tm = min(FFN_TILE, n)

    def body(h1_ref, h3_ref, z_ref, do_ref, w1_ref, w3_ref, w2_ref, g_ref, b_ref,
             dx_ref, a_ref, dh1_ref, dh3_ref, df_ref, dg_ref, db_ref):
        _, vjp = jax.vjp(_ln, z_ref[...], g_ref[...], b_ref[...])
        dz, dg, db = vjp(do_ref[...])

        @pl.when(pl.program_id(0) == 0)
        def _():
            dg_ref[...] = jnp.zeros_like(dg_ref)
            db_ref[...] = jnp.zeros_like(db_ref)

        dg_ref[...] += dg
        db_ref[...] += db
        df = _bf(0.5 * dz)
        df_ref[...] = df
        a, act_vjp = jax.vjp(_swiglu_act, h1_ref[...].astype(F32), h3_ref[...].astype(F32))
        dh1, dh3 = act_vjp(_mm_nt(df, w2_ref[...]))
        dh1 = _bf(dh1)
        dh3 = _bf(dh3)
        a_ref[...] = _bf(a)
        dh1_ref[...] = dh1
        dh3_ref[...] = dh3
        dx_ref[...] = DEEPNORM_ALPHA * dz + _mm(dh1, w1_ref[...]) + _mm(dh3, w3_ref[...])

    row = pl.BlockSpec((tm, D_MODEL), lambda i: (i, 0))
    hid = pl.BlockSpec((tm, D_FF), lambda i: (i, 0))
    wres = _resident((D_FF, D_MODEL))
    vec = _full((1, D_MODEL))
    hid_shape = jax.ShapeDtypeStruct((n, D_FF), BF16)
    return _hosted(
        body, comm, name=name, grid=(n // tm,),
        in_specs=[hid, hid, row, row, wres, wres, wres, vec, vec],
        out_specs=[row, hid, hid, hid, row, vec, vec],
        out_shape=[jax.ShapeDtypeStruct((n, D_MODEL), F32), hid_shape, hid_shape, hid_shape,
                   jax.ShapeDtypeStruct((n, D_MODEL), BF16),
                   jax.ShapeDtypeStruct((1, D_MODEL), F32), jax.ShapeDtypeStruct((1, D_MODEL), F32)],
        scratch_shapes=[], compiler_params=_params(("arbitrary",)),
        args=[h1s, h3s, z, dout, w1t, w3t, w2, g, b])


def _operand_spec(arr, mode, tn, width, parts):
    if mode == "shared":
        return pl.BlockSpec((tn, width), lambda s, k: (k, 0))
    assert mode == "cols" and arr.shape[1] == parts * width
    return pl.BlockSpec((tn, width), lambda s, k: (k, s))


def _grad_matmul(name, a, a_mode, ka, b, b_mode, kb, colsum=False, comm=None, parts=N_CHIPS):
    n = a.shape[-2]
    tn = min(GRAD_ROWS, n)
    nk = n // tn

    def body(*refs):
        if colsum:
            a_ref, b_ref, o_ref, cs_ref, acc = refs
        else:
            a_ref, b_ref, o_ref, acc = refs
        k = pl.program_id(1)
        av = a_ref[...]
        bv = b_ref[...]

        @pl.when(k == 0)
        def _():
            acc[...] = jnp.zeros_like(acc)
            if colsum:
                cs_ref[...] = jnp.zeros_like(cs_ref)

        acc[...] += _mm_tn(av, bv)
        if colsum:
            cs_ref[...] += jnp.sum(bv.astype(F32), axis=0, keepdims=True)

        @pl.when(k == nk - 1)
        def _():
            o_ref[0] = _bf(acc[...])

    out_specs = [pl.BlockSpec((1, ka, kb), lambda s, k: (s, 0, 0))]
    out_shape = [jax.ShapeDtypeStruct((parts, ka, kb), BF16)]
    if colsum:
        out_specs.append(pl.BlockSpec((1, kb), lambda s, k: (0, s)))
        out_shape.append(jax.ShapeDtypeStruct((1, parts * kb), F32))
    return _hosted(
        body, comm, name=name, grid=(parts, nk),
        in_specs=[_operand_spec(a, a_mode, tn, ka, parts), _operand_spec(b, b_mode, tn, kb, parts)],
        out_specs=out_specs, out_shape=out_shape,
        scratch_shapes=[pltpu.VMEM((ka, kb), F32)],
        compiler_params=_params(("arbitrary", "arbitrary")), args=[a, b])


def _kernel_block_of(ref_block):
    attn_blocks, gate_blocks = ATTN_WIDTH // LANES, GATES_WIDTH // LANES
    hgrn_blocks = HGRN_HEADS * HGRN_HEAD_WIDTH // LANES
    if ref_block < attn_blocks:
        return gate_blocks + hgrn_blocks + ref_block
    if ref_block < attn_blocks + hgrn_blocks:
        kind, head = divmod(ref_block - attn_blocks, HGRN_HEADS)
        return gate_blocks + head * (HGRN_HEAD_WIDTH // LANES) + kind
    return ref_block - attn_blocks - hgrn_blocks


def _reorder_w_in(name, w4, to_kernel_order):
    per = IN_SHARD // LANES
    tr = MIX_TILE

    def body(i_ref, o_ref):
        for g in range(D_IN // LANES):
            s, b = divmod(g, per)
            k = _kernel_block_of(g)
            if to_kernel_order:
                o_ref[:, k * LANES:(k + 1) * LANES] = i_ref[s, :, b * LANES:(b + 1) * LANES]
            else:
                ks, kb = divmod(k, per)
                o_ref[s, :, b * LANES:(b + 1) * LANES] = i_ref[ks, :, kb * LANES:(kb + 1) * LANES]

    in_spec = pl.BlockSpec((N_CHIPS, tr, IN_SHARD), lambda i: (0, i, 0))
    if to_kernel_order:
        out_spec, out_shape = pl.BlockSpec((tr, D_IN), lambda i: (i, 0)), (D_MODEL, D_IN)
    else:
        out_spec, out_shape = in_spec, (N_CHIPS, D_MODEL, IN_SHARD)
    return pl.pallas_call(
        body, name=name, grid=(D_MODEL // tr,), in_specs=[in_spec], out_specs=out_spec,
        out_shape=jax.ShapeDtypeStruct(out_shape, w4.dtype), compiler_params=_params(("arbitrary",)),
    )(w4)


def _in_proj(x1, w_in_g, b_in, comm=None):
    n = x1.shape[0]
    tm = min(ROW_TILE, n)

    def body(x_ref, w_ref, b_ref, o_ref):
        xv = x_ref[...]
        for j in range(N_CHIPS):
            cols = slice(j * IN_SHARD, (j + 1) * IN_SHARD)
            o_ref[:, cols] = _mm(xv, w_ref[:, cols]) + b_ref[:, cols]

    return _hosted(
        body, comm, name="in_proj", grid=(n // tm,),
        in_specs=[pl.BlockSpec((tm, D_MODEL), lambda i: (i, 0)),
                  _resident((D_MODEL, D_IN)), _full((1, D_IN))],
        out_specs=[pl.BlockSpec((tm, D_IN), lambda i: (i, 0))],
        out_shape=[jax.ShapeDtypeStruct((n, D_IN), F32)],
        scratch_shapes=[],
        compiler_params=_params(("arbitrary",)), args=[x1, w_in_g, b_in])


def _in_proj_dx(dproj, w_in_g, dz2, comm=None):
    n = dproj.shape[0]
    tm = min(ROW_TILE, n)

    def body(dp_ref, w_ref, dz_ref, o_ref):
        dx = DEEPNORM_ALPHA * dz_ref[...]
        for j in range(N_CHIPS):
            cols = slice(j * IN_SHARD, (j + 1) * IN_SHARD)
            dx = dx + _mm_nt(dp_ref[:, cols], w_ref[:, cols])
        o_ref[...] = dx

    return _hosted(
        body, comm, name="in_proj_dx", grid=(n // tm,),
        in_specs=[pl.BlockSpec((tm, D_IN), lambda i: (i, 0)),
                  _resident((D_MODEL, D_IN)),
                  pl.BlockSpec((tm, D_MODEL), lambda i: (i, 0))],
        out_specs=[pl.BlockSpec((tm, D_MODEL), lambda i: (i, 0))],
        out_shape=[jax.ShapeDtypeStruct((n, D_MODEL), F32)],
        scratch_shapes=[],
        compiler_params=_params(("arbitrary",)), args=[dproj, w_in_g, dz2])


def _rope_tables(seq_len):
    pos = jnp.arange(seq_len, dtype=F32)
    inv_freq = ROPE_THETA ** (-jnp.arange(0, ROPE_DIM, 2, dtype=F32) / ROPE_DIM)
    ang = pos[:, None] * inv_freq[None, :]
    cos, sin = jnp.cos(ang), jnp.sin(ang)
    half = ROPE_DIM // 2
    rest = HEAD_DIM - ROPE_DIM
    ones = jnp.ones((seq_len, rest), F32)
    zeros = jnp.zeros((seq_len, rest), F32)
    zh = jnp.zeros((seq_len, half), F32)
    c = jnp.concatenate([cos, cos, ones], axis=1)
    sa = jnp.concatenate([-sin, zh, zeros], axis=1)
    sb = jnp.concatenate([zh, sin, zeros], axis=1)
    reps = LANES // HEAD_DIM
    return tuple(jnp.tile(t, (1, reps)) for t in (c, sa, sb))


def _rope(t, c, sa, sb):
    w = t.shape[1]
    reps = w // LANES
    half = ROPE_DIM // 2
    return (t * jnp.tile(c, (1, reps)) + pltpu.roll(t, w - half, 1) * jnp.tile(sa, (1, reps))
            + pltpu.roll(t, half, 1) * jnp.tile(sb, (1, reps)))


def _rope_transposed(g, c, sa, sb):
    w = g.shape[1]
    reps = w // LANES
    half = ROPE_DIM // 2
    return (g * jnp.tile(c, (1, reps)) + pltpu.roll(g * jnp.tile(sa, (1, reps)), half, 1)
            + pltpu.roll(g * jnp.tile(sb, (1, reps)), w - half, 1))


GROUP = N_Q_HEADS // N_KV_HEADS


def _both_halves(t_pair, which):
    lo = lax.broadcasted_iota(jnp.int32, t_pair.shape, 1) < HEAD_DIM
    swapped = pltpu.roll(t_pair, HEAD_DIM, 1)
    return _bf(jnp.where(lo, t_pair, swapped) if which == 0 else jnp.where(lo, swapped, t_pair))


def _stack_heads(ref_or_val, kh):
    lo = lax.broadcasted_iota(jnp.int32, (ATTN_BLOCK, LANES), 1) < HEAD_DIM
    rows = []
    for gp in range(GROUP // 2):
        pair = kh * (GROUP // 2) + gp
        t = ref_or_val[:, pair * LANES:(pair + 1) * LANES]
        rows += [jnp.where(lo, t, jnp.zeros_like(t)), jnp.where(lo, jnp.zeros_like(t), t)]
    return jnp.concatenate(rows, axis=0)


def _unstack_pairs(stacked):
    lo = lax.broadcasted_iota(jnp.int32, (ATTN_BLOCK, LANES), 1) < HEAD_DIM
    b = ATTN_BLOCK
    return [jnp.where(lo, stacked[2 * gp * b:(2 * gp + 1) * b], stacked[(2 * gp + 1) * b:(2 * gp + 2) * b])
            for gp in range(GROUP // 2)]


def _attn_mask_t(n):
    cols = GROUP * ATTN_BLOCK
    kj = lax.broadcasted_iota(jnp.int32, (2 * ATTN_BLOCK, cols), 0)
    qi = lax.broadcasted_iota(jnp.int32, (2 * ATTN_BLOCK, cols), 1) % ATTN_BLOCK
    dist = qi + ATTN_BLOCK - kj
    return (dist >= 0) & (dist < ATTN_BLOCK) & (n * ATTN_BLOCK + kj - ATTN_BLOCK >= 0)


def _sink_row(sink_ref, kh):
    col = lax.broadcasted_iota(jnp.int32, (1, GROUP * ATTN_BLOCK), 1)
    row = jnp.full((1, GROUP * ATTN_BLOCK), sink_ref[0, kh * GROUP + GROUP - 1], F32)
    for i in reversed(range(GROUP - 1)):
        row = jnp.where(col < (i + 1) * ATTN_BLOCK, sink_ref[0, kh * GROUP + i], row)
    return row


def _attn_probs_t(q_masked, k_sel, mask_t, sink):
    s = _mm_nt(k_sel, q_masked) * (HEAD_DIM ** -0.5)
    s = jnp.where(mask_t, s, NEG_INF)
    m = jnp.maximum(jnp.max(s, axis=0, keepdims=True), sink)
    p = jnp.exp(s - m)
    e_sink = jnp.exp(sink - m)
    denom = jnp.sum(p, axis=0, keepdims=True) + e_sink
    return p / denom, e_sink / denom


def _attn_fwd(proj, tabs, sinks):
    n_tok = proj.shape[0]
    nb = n_tok // ATTN_BLOCK

    def body(q_ref, k_ref, v_ref, c_ref, sa_ref, sb_ref, sink_ref, y_ref, kprev, vprev):
        n = pl.program_id(0)

        @pl.when(n == 0)
        def _():
            kprev[...] = jnp.zeros_like(kprev)
            vprev[...] = jnp.zeros_like(vprev)

        c, sa, sb = c_ref[...], sa_ref[...], sb_ref[...]
        qr = _bf(_rope(q_ref[...], c, sa, sb))
        kr = _rope(k_ref[...], c, sa, sb)
        vc = v_ref[...]
        kk = jnp.concatenate([kprev[...], kr], axis=0)
        vv = jnp.concatenate([vprev[...], vc], axis=0)
        kprev[...] = kr
        vprev[...] = vc
        mask = _attn_mask_t(n)
        for kh in range(N_KV_HEADS):
            r, which = divmod(kh, 2)
            kb = _both_halves(kk[:, r * LANES:(r + 1) * LANES], which)
            vb = _both_halves(vv[:, r * LANES:(r + 1) * LANES], which)
            probs, _ = _attn_probs_t(_stack_heads(qr, kh), kb, mask, _sink_row(sink_ref, kh))
            for gp, out in enumerate(_unstack_pairs(_mm_tn(_bf(probs), vb))):
                pair = kh * (GROUP // 2) + gp
                y_ref[:, pair * LANES:(pair + 1) * LANES] = _bf(out)

    blk = lambda width, col: pl.BlockSpec((ATTN_BLOCK, width), lambda n: (n, col))
    tab = pl.BlockSpec((ATTN_BLOCK, LANES), lambda n: (n, 0))
    kvw = N_KV_HEADS * HEAD_DIM
    return pl.pallas_call(
        body, name="attn_fwd", grid=(nb,),
        in_specs=[blk(D_MODEL, COL_Q), blk(kvw, COL_K), blk(kvw, COL_V), tab, tab, tab,
                  pl.BlockSpec(memory_space=pltpu.SMEM)],
        out_specs=pl.BlockSpec((ATTN_BLOCK, D_MODEL), lambda n: (n, 0)),
        out_shape=jax.ShapeDtypeStruct((n_tok, D_MODEL), BF16),
        scratch_shapes=[pltpu.VMEM((ATTN_BLOCK, kvw), F32), pltpu.VMEM((ATTN_BLOCK, kvw), F32)],
        compiler_params=_params(("arbitrary",)),
    )(proj, proj, proj, *tabs, sinks)


def _attn_bwd(proj, dy, tabs, sinks, dproj, comm=None):
    n_tok = proj.shape[0]
    nb = n_tok // ATTN_BLOCK
    kvw = N_KV_HEADS * HEAD_DIM

    def body(q_ref, k_ref, v_ref, do_ref, c_ref, sa_ref, sb_ref, cp_ref, sap_ref, sbp_ref, sink_ref, _,
             dqkv_ref, dsink_ref, kprev, vprev, dkc, dvc, dqc):
        n = pl.program_id(0)

        @pl.when(n == 0)
        def _():
            for ref in (kprev, vprev, dkc, dvc, dqc, dsink_ref):
                ref[...] = jnp.zeros_like(ref)

        prev_tabs = (cp_ref[...], sap_ref[...], sbp_ref[...])

        @pl.when(n < nb)
        def _():
            c, sa, sb = c_ref[...], sa_ref[...], sb_ref[...]
            qr = _bf(_rope(q_ref[...], c, sa, sb))
            kr = _rope(k_ref[...], c, sa, sb)
            vc = v_ref[...]
            kk = jnp.concatenate([kprev[...], kr], axis=0)
            vv = jnp.concatenate([vprev[...], vc], axis=0)
            kprev[...] = kr
            vprev[...] = vc
            mask = _attn_mask_t(n)
            lane = lax.broadcasted_iota(jnp.int32, (1, LANES), 1)
            lo2 = lax.broadcasted_iota(jnp.int32, (2 * ATTN_BLOCK, LANES), 1) < HEAD_DIM
            dsink = jnp.zeros((1, LANES), F32)
            dq_pairs = []
            dk_full = []
            dv_full = []
            for kh in range(N_KV_HEADS):
                r, which = divmod(kh, 2)
                kb = _both_halves(kk[:, r * LANES:(r + 1) * LANES], which)
                vb = _both_halves(vv[:, r * LANES:(r + 1) * LANES], which)
                qs = _stack_heads(qr, kh)
                dos = _stack_heads(do_ref, kh)
                probs, p_sink = _attn_probs_t(qs, kb, mask, _sink_row(sink_ref, kh))
                dp = _mm_nt(vb, dos)
                delta = jnp.sum(probs * dp, axis=0, keepdims=True)
                ds = _bf(probs * (dp - delta) * (HEAD_DIM ** -0.5))
                sink_terms = p_sink * delta
                for i in range(GROUP):
                    head_sum = jnp.sum(sink_terms[:, i * ATTN_BLOCK:(i + 1) * ATTN_BLOCK])
                    dsink = dsink + jnp.where(lane == kh * GROUP + i, -head_sum, 0.0)
                dq_pairs += _unstack_pairs(_mm_tn(ds, kb))
                dk_acc = _mm(ds, qs)
                dv_acc = _mm(_bf(probs), dos)
                dk_full.append(dk_acc + pltpu.roll(dk_acc, HEAD_DIM, 1))
                dv_full.append(dv_acc + pltpu.roll(dv_acc, HEAD_DIM, 1))
            dk_pairs = [jnp.where(lo2, dk_full[2 * r], dk_full[2 * r + 1]) for r in range(N_KV_HEADS // 2)]
            dv_pairs = [jnp.where(lo2, dv_full[2 * r], dv_full[2 * r + 1]) for r in range(N_KV_HEADS // 2)]
            dsink_ref[...] += dsink
            dqkv_ref[:, :D_MODEL] = _bf(dqc[...])
            dqc[...] = _rope_transposed(jnp.concatenate(dq_pairs, axis=1), c, sa, sb)
            dk_all = jnp.concatenate(dk_pairs, axis=1)
            dv_all = jnp.concatenate(dv_pairs, axis=1)
            dqkv_ref[:, D_MODEL:D_MODEL + kvw] = _bf(_rope_transposed(dkc[...] + dk_all[:ATTN_BLOCK], *prev_tabs))
            dqkv_ref[:, D_MODEL + kvw:] = _bf(dvc[...] + dv_all[:ATTN_BLOCK])
            dkc[...] = dk_all[ATTN_BLOCK:]
            dvc[...] = dv_all[ATTN_BLOCK:]

        @pl.when(n == nb)
        def _():
            dqkv_ref[:, :D_MODEL] = _bf(dqc[...])
            dqkv_ref[:, D_MODEL:D_MODEL + kvw] = _bf(_rope_transposed(dkc[...], *prev_tabs))
            dqkv_ref[:, D_MODEL + kvw:] = _bf(dvc[...])

    cur = lambda n: jnp.minimum(n, nb - 1)
    prev = lambda n: jnp.maximum(n - 1, 0)
    blk = lambda width, col: pl.BlockSpec((ATTN_BLOCK, width), lambda n: (cur(n), col))
    tab = pl.BlockSpec((ATTN_BLOCK, LANES), lambda n: (cur(n), 0))
    tabp = pl.BlockSpec((ATTN_BLOCK, LANES), lambda n: (prev(n), 0))
    return _hosted(
        body, comm, name="attn_bwd", grid=(nb + 1,),
        in_specs=[blk(D_MODEL, COL_Q), blk(kvw, COL_K), blk(kvw, COL_V), blk(D_MODEL, 0), tab, tab, tab, tabp, tabp, tabp,
                  pl.BlockSpec(memory_space=pltpu.SMEM), pl.BlockSpec(memory_space=pl.ANY)],
        out_specs=[pl.BlockSpec((ATTN_BLOCK, ATTN_WIDTH), lambda n: (prev(n), COL_ATTN)),
                   pl.BlockSpec((1, LANES), lambda n: (0, 0))],
        out_shape=[jax.ShapeDtypeStruct(dproj.shape, dproj.dtype), jax.ShapeDtypeStruct((1, LANES), F32)],
        scratch_shapes=[pltpu.VMEM((ATTN_BLOCK, kvw), F32)] * 4 + [pltpu.VMEM((ATTN_BLOCK, D_MODEL), F32)],
        compiler_params=_params(("arbitrary",)), args=[proj, proj, proj, dy, *tabs, *tabs, sinks, dproj],
        aliases={11: 0})


def _bmm(a, b):
    return lax.dot_general(a, b, (((2,), (1,)), ((0,), (0,))), preferred_element_type=F32)


def _bmm_nt(a, b):
    return lax.dot_general(a, b, (((2,), (2,)), ((0,), (0,))), preferred_element_type=F32)


def _bmm_tn(a, b):
    return lax.dot_general(a, b, (((1,), (1,)), ((0,), (0,))), preferred_element_type=F32)


def _tril(cb, upper=False):
    shape = (cb, HGRN_CHUNK, HGRN_CHUNK)
    r, c = lax.broadcasted_iota(jnp.int32, shape, 1), lax.broadcasted_iota(jnp.int32, shape, 2)
    return (r <= c) if upper else (r >= c)


def _tri_matmul(x, upper):
    return lax.dot_general(_tril(x.shape[0], upper).astype(F32), x, (((2,), (1,)), ((0,), (0,))),
                           precision=lax.Precision.HIGHEST, preferred_element_type=F32)


@jax.custom_vjp
def _chunk_cumsum(x):
    return _tri_matmul(x, False)


_chunk_cumsum.defvjp(lambda x: (_tri_matmul(x, False), None), lambda _, g: (_tri_matmul(g, True),))


def _hg_elem(fl, qh, lb):
    f = lb + (1.0 - lb) * _sig(fl)
    k = 1.0 - f
    gc = _chunk_cumsum(jnp.log(f))
    last = lax.broadcasted_iota(jnp.int32, gc.shape, 1) == HGRN_CHUNK - 1
    g_last = jnp.sum(jnp.where(last, gc, 0.0), axis=1, keepdims=True)
    q = qh * _sig(qh)
    return q * jnp.exp(gc), k * jnp.exp(-gc), k * jnp.exp(g_last - gc), jnp.exp(g_last)


def _hg_out(q_dec, k_inv, v, st):
    sc = jnp.where(_tril(q_dec.shape[0]), _bmm_nt(_bf(q_dec), _bf(k_inv)), 0.0)
    return _bmm(_bf(sc), _bf(v)) + _bmm_nt(_bf(q_dec), _bf(st)), sc


def _hg_post(o, og, ng):
    on = o * lax.rsqrt(jnp.mean(o * o, axis=-1, keepdims=True) + RMS_EPS) * ng
    return on * (og * _sig(og))


def _hgrn_specs(n_tok, rev):
    nc = n_tok // HGRN_CHUNK
    cb = min(HGRN_CHUNKS_PER_STEP, nc)
    nt = nc // cb
    rows = cb * HGRN_CHUNK
    tt = (lambda t: nt - 1 - t) if rev else (lambda t: t)
    col = lambda base: pl.BlockSpec((rows, LANES), lambda h, t: (tt(t), base + h))
    head_cols = pl.BlockSpec((rows, HGRN_HEAD_WIDTH), lambda h, t: (tt(t), COL_HGRN + h))
    head_vec = pl.BlockSpec((1, LANES), lambda h, t: (0, h))
    one_vec = pl.BlockSpec((1, LANES), lambda h, t: (0, 0))
    state = pl.BlockSpec((1, cb, HGRN_DK, HGRN_DK), lambda h, t: (h, tt(t), 0, 0))
    return nc, cb, nt, col, head_cols, head_vec, one_vec, state


def _hgrn_fwd(proj, lb, ng):
    n_tok = proj.shape[0]
    nc, cb, nt, col, head_cols, head_vec, one_vec, state = _hgrn_specs(n_tok, False)

    def body(in_ref, lb_ref, ng_ref, y_ref, st_ref, s_acc):
        @pl.when(pl.program_id(1) == 0)
        def _():
            s_acc[...] = jnp.zeros_like(s_acc)

        fl, qh, v, og = (in_ref[:, i * LANES:(i + 1) * LANES].reshape(cb, HGRN_CHUNK, LANES) for i in range(4))
        q_dec, k_inv, k_end, decay = _hg_elem(fl, qh, lb_ref[...])
        upd = _bmm_tn(_bf(v), _bf(k_end))
        st = s_acc[...]
        for ci in range(cb):
            st_ref[0, ci] = st
            st = st * decay[ci] + upd[ci]
        s_acc[...] = st
        o, _ = _hg_out(q_dec, k_inv, v, st_ref[0])
        y_ref[...] = _bf(_hg_post(o, og, ng_ref[...]).reshape(cb * HGRN_CHUNK, LANES))

    return pl.pallas_call(
        body, name="hgrn_fwd", grid=(HGRN_HEADS, nt),
        in_specs=[head_cols, head_vec, one_vec],
        out_specs=[col(0), state],
        out_shape=[jax.ShapeDtypeStruct((n_tok, D_MODEL), BF16),
                   jax.ShapeDtypeStruct((HGRN_HEADS, nc, HGRN_DK, HGRN_DK), F32)],
        scratch_shapes=[pltpu.VMEM((HGRN_DK, HGRN_DK), F32)],
        compiler_params=_params(("arbitrary", "arbitrary")),
    )(proj, lb, ng)


def _hgrn_bwd(proj, lb, ng, states, dy, dproj, comm=None):
    n_tok = proj.shape[0]
    nc, cb, nt, col, head_cols, head_vec, one_vec, state = _hgrn_specs(n_tok, True)

    def body(in_ref, lb_ref, ng_ref, st_ref, dy_ref, _, d_ref, dlb_ref, dng_ref, g_acc, g_all):
        h = pl.program_id(0)
        t = pl.program_id(1)

        @pl.when(t == 0)
        def _():
            g_acc[...] = jnp.zeros_like(g_acc)
            dlb_ref[...] = jnp.zeros_like(dlb_ref)

        @pl.when((t == 0) & (h == 0))
        def _():
            dng_ref[...] = jnp.zeros_like(dng_ref)

        fl, qh, v, og = (in_ref[:, i * LANES:(i + 1) * LANES].reshape(cb, HGRN_CHUNK, LANES) for i in range(4))
        (q_dec, k_inv, k_end, decay), elem_vjp = jax.vjp(_hg_elem, fl, qh, lb_ref[...])
        st = st_ref[0]
        o, sc = _hg_out(q_dec, k_inv, v, st)
        _, post_vjp = jax.vjp(_hg_post, o, og, ng_ref[...])
        do, dog, dng = post_vjp(dy_ref[...].reshape(cb, HGRN_CHUNK, LANES))
        dob, vb, qb = _bf(do), _bf(v), _bf(q_dec)
        dsc = _bf(jnp.where(_tril(cb), _bmm_nt(dob, vb), 0.0))
        p = _bmm_tn(dob, qb)
        g = g_acc[...]
        for ci in reversed(range(cb)):
            g_all[ci] = g
            g = g * decay[ci] + p[ci]
        g_acc[...] = g
        g = g_all[...]
        gb = _bf(g)
        dq_dec = _bmm(dsc, _bf(k_inv)) + _bmm(dob, _bf(st))
        dk_inv = _bmm_tn(dsc, qb)
        dv = _bmm_tn(_bf(sc), dob) + _bmm_nt(_bf(k_end), gb)
        dk_end = _bmm(vb, gb)
        ddecay = jnp.sum(st * g, axis=1, keepdims=True)
        dfl, dqh, dlb = elem_vjp((dq_dec, dk_inv, dk_end, ddecay))
        for i, val in enumerate((dfl, dqh, dv, dog)):
            d_ref[:, i * LANES:(i + 1) * LANES] = _bf(val.reshape(cb * HGRN_CHUNK, LANES))
        dlb_ref[...] += dlb
        dng_ref[...] += dng

    return _hosted(
        body, comm, name="hgrn_bwd", grid=(HGRN_HEADS, nt),
        in_specs=[head_cols, head_vec, one_vec, state, col(0), pl.BlockSpec(memory_space=pl.ANY)],
        out_specs=[head_cols, head_vec, one_vec],
        out_shape=[jax.ShapeDtypeStruct(dproj.shape, dproj.dtype),
                   jax.ShapeDtypeStruct((1, D_MODEL), F32), jax.ShapeDtypeStruct((1, LANES), F32)],
        scratch_shapes=[pltpu.VMEM((HGRN_DK, HGRN_DK), F32), pltpu.VMEM((cb, HGRN_DK, HGRN_DK), F32)],
        compiler_params=_params(("arbitrary", "arbitrary")), args=[proj, lb, ng, states, dy, dproj], aliases={5: 0})


def _lb_fwd(lb_logits):
    def lb_of(l0, l1):
        m = jnp.maximum(l0, l1)
        e0, e1 = jnp.exp(l0 - m), jnp.exp(l1 - m)
        return e0 / (e0 + e1)

    def body(l_ref, o_ref):
        o_ref[...] = lb_of(l_ref[0:1, :], l_ref[1:2, :])

    lb = pl.pallas_call(body, name="lb_fwd", out_shape=jax.ShapeDtypeStruct((1, D_MODEL), F32))(lb_logits)
    return lb, lb_of


def _mix_fwd(y_attn, y_hgrn, proj, x1, w_pa, w_ph, w_out, g, b):
    n = x1.shape[0]
    tm = min(MIX_TILE, n)

    def body(ya_ref, yh_ref, gt_ref, x_ref, wpa, wph, wo, g_ref, b_ref, z_ref, o_ref, pa_ref, ph_ref):
        ya = _mm(ya_ref[...], wpa[...])
        yh = _mm(yh_ref[...], wph[...])
        pa_ref[...] = _bf(ya)
        ph_ref[...] = _bf(yh)
        merged = _sig(gt_ref[:, :D_MODEL]) * ya + _sig(gt_ref[:, D_MODEL:]) * yh
        z = DEEPNORM_ALPHA * x_ref[...] + _mm(_bf(merged), wo[...])
        z_ref[...] = z
        o_ref[...] = _ln(z, g_ref[...], b_ref[...])

    row = pl.BlockSpec((tm, D_MODEL), lambda i: (i, 0))
    gates = pl.BlockSpec((tm, GATES_WIDTH), lambda i: (i, 0))
    sq = _resident((D_MODEL, D_MODEL))
    vec = _full((1, D_MODEL))
    return pl.pallas_call(
        body, name="mix_fwd", grid=(n // tm,),
        in_specs=[row, row, gates, row, sq, sq, sq, vec, vec],
        out_specs=[row, row, row, row],
        out_shape=[jax.ShapeDtypeStruct((n, D_MODEL), F32)] * 2 + [jax.ShapeDtypeStruct((n, D_MODEL), BF16)] * 2,
        compiler_params=_params(("arbitrary",)),
    )(y_attn, y_hgrn, proj, x1, w_pa, w_ph, w_out, g, b)


def _mix_bwd(dx2, z2, pa, ph, proj, w_pa, w_ph, w_out, g, b):
    n = z2.shape[0]
    tm = min(MIX_TILE, n)

    def body(do_ref, z_ref, ya_ref, yh_ref, gt_ref, wpa, wph, wo, g_ref, b_ref,
             dz_ref, dzb_ref, mg_ref, dya_ref, dyh_ref, dyat_ref, dyhg_ref, dgt_ref, dg_ref, db_ref):
        _, vjp = jax.vjp(_ln, z_ref[...], g_ref[...], b_ref[...])
        dz, dg, db = vjp(do_ref[...])

        @pl.when(pl.program_id(0) == 0)
        def _():
            dg_ref[...] = jnp.zeros_like(dg_ref)
            db_ref[...] = jnp.zeros_like(db_ref)

        dg_ref[...] += dg
        db_ref[...] += db
        dz_ref[...] = dz
        ya = ya_ref[...].astype(F32)
        yh = yh_ref[...].astype(F32)
        def merge(ga, gh, ya, yh):
            return _sig(ga) * ya + _sig(gh) * yh

        merged, merge_vjp = jax.vjp(merge, gt_ref[:, :D_MODEL], gt_ref[:, D_MODEL:], ya, yh)
        mg_ref[...] = _bf(merged)
        dzb = _bf(dz)
        dzb_ref[...] = dzb
        dmerged = _mm_nt(dzb, wo[...])
        dga, dgh, dya, dyh = merge_vjp(dmerged)
        dya = _bf(dya)
        dyh = _bf(dyh)
        dya_ref[...] = dya
        dyh_ref[...] = dyh
        dgt_ref[:, :D_MODEL] = _bf(dga)
        dgt_ref[:, D_MODEL:] = _bf(dgh)
        dyat_ref[...] = _bf(_mm_nt(dya, wpa[...]))
        dyhg_ref[...] = _mm_nt(dyh, wph[...])

    row = pl.BlockSpec((tm, D_MODEL), lambda i: (i, 0))
    gates = pl.BlockSpec((tm, GATES_WIDTH), lambda i: (i, 0))
    sq = _resident((D_MODEL, D_MODEL))
    vec = _full((1, D_MODEL))
    f32_row = jax.ShapeDtypeStruct((n, D_MODEL), F32)
    bf_row = jax.ShapeDtypeStruct((n, D_MODEL), BF16)
    vec_shape = jax.ShapeDtypeStruct((1, D_MODEL), F32)
    return pl.pallas_call(
        body, name="mix_bwd", grid=(n // tm,),
        in_specs=[row, row, row, row, gates, sq, sq, sq, vec, vec],
        out_specs=[row, row, row, row, row, row, row, gates, vec, vec],
        out_shape=[f32_row, bf_row, bf_row, bf_row, bf_row, bf_row, f32_row,
                   jax.ShapeDtypeStruct((n, D_IN), BF16), vec_shape, vec_shape],
        compiler_params=_params(("arbitrary",)),
    )(dx2, z2, pa, ph, proj, w_pa, w_ph, w_out, g, b)


def _position():
    x, y, c = lax.axis_index("x"), lax.axis_index("y"), lax.axis_index("c")
    chips = [(1 - x, y), (x, 1 - y), (1 - x, 1 - y)]
    return x, y, c, chips


def _any_specs(k):
    return [pl.BlockSpec(memory_space=pl.ANY)] * k


class _GatherWeights:
    def __init__(self, shards):
        nw = len(shards)
        self.inputs = list(shards)
        self.out_shape = [jax.ShapeDtypeStruct((N_CHIPS, *s.shape), s.dtype) for s in shards]
        self.scratch = [pltpu.SemaphoreType.DMA((nw,)), pltpu.SemaphoreType.DMA((nw * 6,)),
                        pltpu.SemaphoreType.DMA((nw * 6,))]

    def _copies(self, ins, outs, sems):
        nw = len(ins)
        local_sem, send_sem, recv_sem = sems
        x, y, c, chips = _position()
        me = 2 * x + y
        sibling = (x, y, 1 - c)
        half_rows = [s.shape[0] // 2 for s in self.inputs]

        def half(w, chip_idx, which):
            return outs[w].at[chip_idx, pl.ds(which * half_rows[w], half_rows[w])]

        def remote(w, k, src, dst, to):
            return pltpu.make_async_remote_copy(src_ref=src, dst_ref=dst, send_sem=send_sem.at[w * 6 + k],
                                                recv_sem=recv_sem.at[w * 6 + k], device_id=to, device_id_type=MESH)

        local = [pltpu.make_async_copy(ins[w], outs[w].at[me], local_sem.at[w]) for w in range(nw)]
        first = [remote(w, j, ins[w].at[pl.ds(c * half_rows[w], half_rows[w])], half(w, me, c), (px, py, c))
                 for w in range(nw) for j, (px, py) in enumerate(chips)]
        landed = [half(w, 2 * px + py, c) for w in range(nw) for (px, py) in chips]
        arrive = [remote(w, j, landed[w * 3 + j], landed[w * 3 + j], (px, py, c))
                  for w in range(nw) for j, (px, py) in enumerate(chips)]
        passed = [remote(w, 3 + j, landed[w * 3 + j], landed[w * 3 + j], sibling) for w in range(nw) for j in range(3)]
        from_sibling = [remote(w, 3 + j, half(w, 2 * px + py, 1 - c), half(w, 2 * px + py, 1 - c), sibling)
                        for w in range(nw) for j, (px, py) in enumerate(chips)]
        return local, first, arrive, passed, from_sibling

    def start(self, ins, outs, sems):
        local, first, _, _, _ = self._copies(ins, outs, sems)
        for cp in local + first:
            cp.start()

    def finish(self, ins, outs, sems):
        local, first, arrive, passed, from_sibling = self._copies(ins, outs, sems)
        for cp_in, cp_on in zip(arrive, passed):
            cp_in.wait_recv()
            cp_on.start()
        for cp in from_sibling:
            cp.wait_recv()
        for cp in first + passed:
            cp.wait_send()
        for cp in local:
            cp.wait()


class _ExchangeGrads:
    def __init__(self, grads):
        nw = len(grads)
        self.inputs = list(grads)
        self.out_shape = [jax.ShapeDtypeStruct(g.shape, g.dtype) for g in grads]
        self.scratch = [pltpu.SemaphoreType.DMA((nw,)), pltpu.SemaphoreType.DMA((nw * 3,)),
                        pltpu.SemaphoreType.DMA((nw * 3,))]

    def _copies(self, ins, outs, sems):
        nw = len(ins)
        local_sem, send_sem, recv_sem = sems
        x, y, c, chips = _position()
        me = 2 * x + y

        def remote(w, j, src, dst, chip):
            return pltpu.make_async_remote_copy(src_ref=src, dst_ref=dst, send_sem=send_sem.at[w * 3 + j],
                                                recv_sem=recv_sem.at[w * 3 + j], device_id=(*chip, c),
                                                device_id_type=MESH)

        local = [pltpu.make_async_copy(ins[w].at[me], outs[w].at[me], local_sem.at[w]) for w in range(nw)]
        sends = [remote(w, j, ins[w].at[2 * px + py], outs[w].at[me], (px, py))
                 for w in range(nw) for j, (px, py) in enumerate(chips)]
        arrive = [remote(w, j, outs[w].at[2 * px + py], outs[w].at[2 * px + py], (px, py))
                  for w in range(nw) for j, (px, py) in enumerate(chips)]
        return local, sends, arrive

    def start(self, ins, outs, sems):
        local, sends, _ = self._copies(ins, outs, sems)
        for cp in local + sends:
            cp.start()

    def finish(self, ins, outs, sems):
        local, sends, arrive = self._copies(ins, outs, sems)
        for cp in arrive:
            cp.wait_recv()
        for cp in sends:
            cp.wait_send()
        for cp in local:
            cp.wait()


def _hosted(body, comm, *, name, grid, in_specs, out_specs, out_shape, scratch_shapes, compiler_params, args,
            aliases=None):
    aliases = aliases or {}
    if comm is None:
        res = pl.pallas_call(body, name=name, grid=grid, in_specs=in_specs, out_specs=out_specs, out_shape=out_shape,
                             scratch_shapes=scratch_shapes, compiler_params=compiler_params,
                             input_output_aliases=aliases)(*args)
        return list(res), []
    n_in, n_out, n_scr = len(in_specs), len(out_specs), len(scratch_shapes)
    c_in, c_out = len(comm.inputs), len(comm.out_shape)

    def hosted_body(*refs):
        refs = list(refs)
        cut = lambda k: (refs[:k], refs[k:])
        main_in, refs = cut(n_in)
        comm_in, refs = cut(c_in)
        main_out, refs = cut(n_out)
        comm_out, refs = cut(c_out)
        main_scr, comm_scr = cut(n_scr)
        ids = [pl.program_id(a) for a in range(len(grid))]
        first = functools.reduce(jnp.logical_and, [i == 0 for i in ids])
        last = functools.reduce(jnp.logical_and, [i == g - 1 for i, g in zip(ids, grid)])

        @pl.when(first)
        def _():
            comm.start(comm_in, comm_out, comm_scr)

        body(*main_in, *main_out, *main_scr)

        @pl.when(last)
        def _():
            comm.finish(comm_in, comm_out, comm_scr)

    res = pl.pallas_call(
        hosted_body, name=name, grid=grid, in_specs=[*in_specs, *_any_specs(c_in)],
        out_specs=[*out_specs, *_any_specs(c_out)], out_shape=[*out_shape, *comm.out_shape],
        scratch_shapes=[*scratch_shapes, *comm.scratch], compiler_params=compiler_params,
        input_output_aliases=aliases,
    )(*args, *comm.inputs)
    return list(res[:n_out]), list(res[n_out:])


def _run_comm(name, comm):
    def body(*refs):
        refs = list(refs)
        c_in, c_out = len(comm.inputs), len(comm.out_shape)
        ins, outs, sems = refs[:c_in], refs[c_in:c_in + c_out], refs[c_in + c_out:]
        comm.start(ins, outs, sems)
        comm.finish(ins, outs, sems)

    return list(pl.pallas_call(
        body, name=name, in_specs=_any_specs(len(comm.inputs)), out_specs=_any_specs(len(comm.out_shape)),
        out_shape=comm.out_shape, scratch_shapes=comm.scratch,
    )(*comm.inputs))


def _sum_slots(name, slots):
    k = len(slots)
    _, rows, cols = slots[0].shape
    tr = _update_rows(rows)

    def body(*refs):
        for s_ref, o_ref in zip(refs[:k], refs[k:]):
            acc = s_ref[0].astype(F32)
            for i in range(1, N_CHIPS):
                acc = acc + s_ref[i].astype(F32)
            o_ref[...] = acc

    return pl.pallas_call(
        body, name=name, grid=(rows // tr,),
        in_specs=[pl.BlockSpec((N_CHIPS, tr, cols), lambda i: (0, i, 0))] * k,
        out_specs=[pl.BlockSpec((tr, cols), lambda i: (i, 0))] * k,
        out_shape=[jax.ShapeDtypeStruct((rows, cols), F32)] * k,
        compiler_params=_params(("arbitrary",)),
    )(*slots)


class _SwapWithSibling:
    def __init__(self, parts):
        self.inputs = list(parts)
        self.out_shape = [jax.ShapeDtypeStruct(p.shape, p.dtype) for p in parts]
        self.scratch = [pltpu.SemaphoreType.DMA((len(parts),)), pltpu.SemaphoreType.DMA((len(parts),))]

    def _copies(self, ins, outs, sems):
        send_sem, recv_sem = sems
        x, y, c, _ = _position()
        return [pltpu.make_async_remote_copy(src_ref=ins[w], dst_ref=outs[w], send_sem=send_sem.at[w],
                                             recv_sem=recv_sem.at[w], device_id=(x, y, 1 - c), device_id_type=MESH)
                for w in range(len(ins))]

    def start(self, ins, outs, sems):
        for cp in self._copies(ins, outs, sems):
            cp.start()

    def finish(self, ins, outs, sems):
        for cp in self._copies(ins, outs, sems):
            cp.wait()


class _Together:
    def __init__(self, first, second):
        self.parts = (first, second)
        self.inputs = first.inputs + second.inputs
        self.out_shape = first.out_shape + second.out_shape
        self.scratch = first.scratch + second.scratch

    def _split(self, ins, outs, sems):
        a = self.parts[0]
        ni, no, ns = len(a.inputs), len(a.out_shape), len(a.scratch)
        return ((ins[:ni], outs[:no], sems[:ns]), (ins[ni:], outs[no:], sems[ns:]))

    def start(self, ins, outs, sems):
        for part, args in zip(self.parts, self._split(ins, outs, sems)):
            part.start(*args)

    def finish(self, ins, outs, sems):
        for part, args in zip(self.parts, self._split(ins, outs, sems)):
            part.finish(*args)


def _sum_small(part):
    def body(p_ref, o_ref, buf, send_sem, recv_sem):
        x, y, c, _ = _position()
        me = 4 * x + 2 * y + c
        buf[me] = p_ref[...]
        copies = []
        for k in range(1, N_DEV):
            peer = tuple(1 - v if (k >> s) & 1 else v for v, s in ((x, 2), (y, 1), (c, 0)))
            copies.append(pltpu.make_async_remote_copy(src_ref=p_ref, dst_ref=buf.at[me], send_sem=send_sem.at[k - 1],
                                                       recv_sem=recv_sem.at[k - 1], device_id=peer, device_id_type=MESH))
        for cp in copies:
            cp.start()
        for cp in copies:
            cp.wait()
        acc = buf[0]
        for d in range(1, N_DEV):
            acc = acc + buf[d]
        o_ref[...] = acc

    vm = pl.BlockSpec(memory_space=pltpu.VMEM)
    return pl.pallas_call(
        body, name="sum_small", in_specs=[vm], out_specs=vm,
        out_shape=jax.ShapeDtypeStruct((1, SM_LEN), F32),
        scratch_shapes=[pltpu.VMEM((N_DEV, 1, SM_LEN), F32), pltpu.SemaphoreType.DMA((N_DEV - 1,)),
                        pltpu.SemaphoreType.DMA((N_DEV - 1,))],
    )(part)


def _adamw(w, g, m, v):
    m = ADAM_B1 * m + (1.0 - ADAM_B1) * g
    v = ADAM_B2 * v + (1.0 - ADAM_B2) * (g * g)
    m_hat = m / (1.0 - ADAM_B1 ** ADAM_STEP)
    v_hat = v / (1.0 - ADAM_B2 ** ADAM_STEP)
    delta = -ADAM_LR * (m_hat / (jnp.sqrt(v_hat) + ADAM_EPS) + ADAM_WD * w)
    return delta, m, v


def _adam_big(name, groups):
    k = len(groups)
    rows, cols = groups[0][2].shape
    tr = _update_rows(rows)

    def body(*refs):
        for i in range(k):
            p_ref, q_ref, w_ref, m_ref, v_ref = refs[5 * i:5 * i + 5]
            g_ref, d_ref, nm_ref, nv_ref = refs[5 * k + 4 * i:5 * k + 4 * i + 4]
            g = p_ref[...] + q_ref[...]
            g_ref[...] = g
            d_ref[...], nm_ref[...], nv_ref[...] = _adamw(w_ref[...], g, m_ref[...], v_ref[...])

    spec = pl.BlockSpec((tr, cols), lambda i: (i, 0))
    res = pl.pallas_call(
        body, name=name, grid=(rows // tr,), in_specs=[spec] * (5 * k), out_specs=[spec] * (4 * k),
        out_shape=[jax.ShapeDtypeStruct((rows, cols), F32)] * (4 * k),
        compiler_params=_params(("arbitrary",)),
    )(*[a for grp in groups for a in grp])
    return [res[4 * i:4 * i + 4] for i in range(k)]


_SMALL_AT = {"ln1_g": 0, "ln1_b": D_MODEL, "ln2_g": 2 * D_MODEL, "ln2_b": 3 * D_MODEL, "ln3_g": 4 * D_MODEL,
             "ln3_b": 5 * D_MODEL, "b_in": SM_BIN, "attn_sinks": SM_SINK, "hgrn_norm_g": SM_NG}


def _adam_small(total, w, m, v, lb_of):
    names = list(_SMALL)
    k = len(names)

    def body(*refs):
        t_ref = refs[0]
        w_refs, m_refs, v_refs = (refs[1 + i * k:1 + (i + 1) * k] for i in range(3))
        g_refs, d_refs, nm_refs, nv_refs = (refs[1 + (3 + i) * k:1 + (4 + i) * k] for i in range(4))
        for i, name in enumerate(names):
            if name == "hgrn_lb_logits":
                _, vjp = jax.vjp(lb_of, w_refs[i][0:1, :], w_refs[i][1:2, :])
                g_refs[i][0:1, :], g_refs[i][1:2, :] = vjp(t_ref[:, SM_LB:SM_LOSS])
            else:
                at = _SMALL_AT[name]
                g_refs[i][...] = t_ref[:, at:at + w_refs[i].shape[1]]
            d_refs[i][...], nm_refs[i][...], nv_refs[i][...] = _adamw(
                w_refs[i][...], g_refs[i][...], m_refs[i][...], v_refs[i][...])

    shapes = [jax.ShapeDtypeStruct(w[name].shape, F32) for name in names]
    res = pl.pallas_call(body, name="adam_small", out_shape=shapes * 4)(
        total, *[w[n] for n in names], *[m[n] for n in names], *[v[n] for n in names])
    return [dict(zip(names, res[i * k:(i + 1) * k])) for i in range(4)]


_BIG = ("ffn1_w1", "ffn1_w3", "ffn1_w2", "w_in", "w_proj_attn", "w_proj_hgrn", "w_out", "ffn2_w1", "ffn2_w3", "ffn2_w2")
_SMALL = ("ln1_g", "ln1_b", "ln2_g", "ln2_b", "ln3_g", "ln3_b", "b_in", "attn_sinks", "hgrn_norm_g", "hgrn_lb_logits")
_ORDER = ("ln1_g", "ln1_b", "ffn1_w1", "ffn1_w3", "ffn1_w2", "ln2_g", "ln2_b", "w_in", "b_in", "attn_sinks",
          "hgrn_lb_logits", "hgrn_norm_g", "w_proj_attn", "w_proj_hgrn", "w_out", "ln3_g", "ln3_b",
          "ffn2_w1", "ffn2_w3", "ffn2_w2")


_TRANSPOSED = ("ffn1_w1", "ffn1_w3", "ffn2_w1", "ffn2_w3")


def _local_view(name, arr):
    return arr[0].T if name in _TRANSPOSED else arr[0]


def _ffn_grad(name, hidden, other, comm=None):
    (dw,), comm_out = _grad_matmul(name, hidden, "cols", D_FF // FF_GRAD_PARTS, other, "shared", D_MODEL,
                                   comm=comm, parts=FF_GRAD_PARTS)
    return dw.reshape(N_CHIPS, D_FF // N_CHIPS, D_MODEL), comm_out


def kernel(x, ln1_g, ln1_b, ffn1_w1, ffn1_w3, ffn1_w2, ln2_g, ln2_b, w_in, b_in, attn_sinks, hgrn_lb_logits, hgrn_norm_g, w_proj_attn, w_proj_hgrn, w_out, ln3_g, ln3_b, ffn2_w1, ffn2_w3, ffn2_w2, loss_target, m_ln1_g, m_ln1_b, m_ffn1_w1, m_ffn1_w3, m_ffn1_w2, m_ln2_g, m_ln2_b, m_w_in, m_b_in, m_attn_sinks, m_hgrn_lb_logits, m_hgrn_norm_g, m_w_proj_attn, m_w_proj_hgrn, m_w_out, m_ln3_g, m_ln3_b, m_ffn2_w1, m_ffn2_w3, m_ffn2_w2, v_ln1_g, v_ln1_b, v_ffn1_w1, v_ffn1_w3, v_ffn1_w2, v_ln2_g, v_ln2_b, v_w_in, v_b_in, v_attn_sinks, v_hgrn_lb_logits, v_hgrn_norm_g, v_w_proj_attn, v_w_proj_hgrn, v_w_out, v_ln3_g, v_ln3_b, v_ffn2_w1, v_ffn2_w3, v_ffn2_w2):
    w = dict(ln1_g=ln1_g, ln1_b=ln1_b, ffn1_w1=ffn1_w1, ffn1_w3=ffn1_w3, ffn1_w2=ffn1_w2, ln2_g=ln2_g, ln2_b=ln2_b,
             w_in=w_in, b_in=b_in, attn_sinks=attn_sinks, hgrn_lb_logits=hgrn_lb_logits, hgrn_norm_g=hgrn_norm_g,
             w_proj_attn=w_proj_attn, w_proj_hgrn=w_proj_hgrn, w_out=w_out, ln3_g=ln3_g, ln3_b=ln3_b,
             ffn2_w1=ffn2_w1, ffn2_w3=ffn2_w3, ffn2_w2=ffn2_w2)
    mom = dict(ln1_g=m_ln1_g, ln1_b=m_ln1_b, ffn1_w1=m_ffn1_w1, ffn1_w3=m_ffn1_w3, ffn1_w2=m_ffn1_w2, ln2_g=m_ln2_g,
               ln2_b=m_ln2_b, w_in=m_w_in, b_in=m_b_in, attn_sinks=m_attn_sinks, hgrn_lb_logits=m_hgrn_lb_logits,
               hgrn_norm_g=m_hgrn_norm_g, w_proj_attn=m_w_proj_attn, w_proj_hgrn=m_w_proj_hgrn, w_out=m_w_out,
               ln3_g=m_ln3_g, ln3_b=m_ln3_b, ffn2_w1=m_ffn2_w1, ffn2_w3=m_ffn2_w3, ffn2_w2=m_ffn2_w2)
    var = dict(ln1_g=v_ln1_g, ln1_b=v_ln1_b, ffn1_w1=v_ffn1_w1, ffn1_w3=v_ffn1_w3, ffn1_w2=v_ffn1_w2, ln2_g=v_ln2_g,
               ln2_b=v_ln2_b, w_in=v_w_in, b_in=v_b_in, attn_sinks=v_attn_sinks, hgrn_lb_logits=v_hgrn_lb_logits,
               hgrn_norm_g=v_hgrn_norm_g, w_proj_attn=v_w_proj_attn, w_proj_hgrn=v_w_proj_hgrn, w_out=v_w_out,
               ln3_g=v_ln3_g, ln3_b=v_ln3_b, ffn2_w1=v_ffn2_w1, ffn2_w3=v_ffn2_w3, ffn2_w2=v_ffn2_w2)

    n_tok = x.shape[1]
    x0 = x.reshape(n_tok, D_MODEL)
    target = loss_target.reshape(n_tok, D_MODEL)

    shard = {k: _bf(_local_view(k, w[k])) for k in _BIG}
    gather = lambda keys: _GatherWeights([shard[k] for k in keys])
    slots = {}
    exchange = lambda keys: _ExchangeGrads([big[k] for k in keys])
    ffn1_keys = ("ffn1_w1", "ffn1_w3", "ffn1_w2")
    mixer_keys = ("w_in", "w_proj_attn", "w_proj_hgrn", "w_out")
    ffn2_keys = ("ffn2_w1", "ffn2_w3", "ffn2_w2")
    whole = lambda ts: [t.reshape(D_FF, D_MODEL) for t in ts]
    f1 = whole(_run_comm("gather_ffn1", gather(ffn1_keys)))

    tabs = _rope_tables(n_tok)
    lb, lb_of = _lb_fwd(hgrn_lb_logits)
    (z1, x1, x0b, h1_1, h3_1, x1b), (w_in_g, w_pa, w_ph, w_o) = _ffn_fwd(
        "ffn1_fwd", x0, *f1, ln1_g, ln1_b, comm=gather(mixer_keys))
    w_pa, w_ph, w_o = (t.reshape(D_MODEL, D_MODEL) for t in (w_pa, w_ph, w_o))
    w_in_g = _reorder_w_in("w_in_cols", w_in_g, True)
    (proj,), f2 = _in_proj(x1b, w_in_g, _to_kernel_cols(b_in), comm=gather(ffn2_keys))
    f2 = whole(f2)
    y_attn = _attn_fwd(proj, tabs, attn_sinks)
    y_hgrn, states = _hgrn_fwd(proj, lb, hgrn_norm_g)
    z2, x2, proj_a, proj_h = _mix_fwd(y_attn, y_hgrn, proj, x1, w_pa, w_ph, w_o, ln2_g, ln2_b)
    (z3, dy, x2b, h1_2, h3_2, loss_part), _ = _ffn_fwd("ffn2_fwd", x2, *f2, ln3_g, ln3_b, target=target)

    big = {}
    small = {}
    (dx2, a2, dh1_2, dh3_2, df2, small["ln3_g"], small["ln3_b"]), _ = _ffn_bwd(
        "ffn2_bwd", h1_2, h3_2, z3, dy, *f2, ln3_g, ln3_b)
    big["ffn2_w1"], _ = _ffn_grad("ffn2_dw1", dh1_2, x2b)
    big["ffn2_w3"], _ = _ffn_grad("ffn2_dw3", dh3_2, x2b)
    big["ffn2_w2"], _ = _ffn_grad("ffn2_dw2", a2, df2)
    (dz2, dz2b, merged, dya, dyh, dy_attn, dy_hgrn, dproj, small["ln2_g"], small["ln2_b"]) = _mix_bwd(
        dx2, z2, proj_a, proj_h, proj, w_pa, w_ph, w_o, ln2_g, ln2_b)
    for key, name, lhs, rhs in (("w_out", "dw_out", merged, dz2b), ("w_proj_attn", "dw_proj_attn", y_attn, dya),
                                ("w_proj_hgrn", "dw_proj_hgrn", y_hgrn, dyh)):
        (dw,), _ = _grad_matmul(name, lhs, "cols", D_MODEL, rhs, "shared", D_MODEL, parts=1)
        big[key] = dw.reshape(N_CHIPS, PROJ_SHARD, D_MODEL)
    (dproj, dlb, small["hgrn_norm_g"]), early = _hgrn_bwd(
        proj, lb, hgrn_norm_g, states, dy_hgrn, dproj, comm=exchange(ffn2_keys))
    slots.update(zip(ffn2_keys, early))
    (dproj, dsinks), early = _attn_bwd(proj, dy_attn, tabs, attn_sinks, dproj, comm=exchange(mixer_keys[1:]))
    slots.update(zip(mixer_keys[1:], early))
    (dw_in, db_in), _ = _grad_matmul("dw_in", x1b, "shared", D_MODEL, dproj, "cols", IN_SHARD, colsum=True)
    big["w_in"] = _reorder_w_in("dw_in_cols", dw_in, False)
    small["b_in"] = _from_kernel_cols(db_in)
    (dx1,), (slots["w_in"],) = _in_proj_dx(dproj, w_in_g, dz2, comm=exchange(("w_in",)))
    (grad_x, a1, dh1_1, dh3_1, df1, small["ln1_g"], small["ln1_b"]), _ = _ffn_bwd(
        "ffn1_bwd", h1_1, h3_1, z1, dx1, *f1, ln1_g, ln1_b)
    big["ffn1_w1"], _ = _ffn_grad("ffn1_dw1", dh1_1, x0b)
    big["ffn1_w3"], (slots["ffn1_w1"],) = _ffn_grad("ffn1_dw3", dh3_1, x0b, comm=exchange(("ffn1_w1",)))
    big["ffn1_w2"], (slots["ffn1_w3"],) = _ffn_grad("ffn1_dw2", a1, df1, comm=exchange(("ffn1_w3",)))

    last = "ffn1_w2"
    groups = [[k for k in ffn1_keys + ffn2_keys if k != last], list(mixer_keys[1:]), ["w_in"]]
    partial = {}
    for keys in groups:
        partial.update(zip(keys, _sum_slots("sum_" + keys[0], [slots[k] for k in keys])))
    swapped_keys = [k for keys in groups for k in keys]
    moved = _run_comm("swap_and_exchange_last",
                      _Together(_SwapWithSibling([partial[k] for k in swapped_keys]), exchange((last,))))
    from_sibling = dict(zip(swapped_keys, moved))
    (partial[last],) = _sum_slots("sum_" + last, [moved[-1]])
    (from_sibling[last],) = _run_comm("swap_last", _SwapWithSibling([partial[last]]))
    groups[0].append(last)

    outs = {"grad": {}, "delta": {}, "m": {}, "v": {}}
    for keys in groups:
        res = _adam_big("adam_" + keys[0], [(partial[k], from_sibling[k], _local_view(k, w[k]), _local_view(k, mom[k]),
                                             _local_view(k, var[k])) for k in keys])
        for k, four in zip(keys, res):
            for kind, r in zip(("grad", "delta", "m", "v"), four):
                outs[kind][k] = (r.T if k in _TRANSPOSED else r).reshape(w[k].shape)

    total = _sum_small(jnp.concatenate(
        [small[k] for k in ("ln1_g", "ln1_b", "ln2_g", "ln2_b", "ln3_g", "ln3_b", "b_in")]
        + [dsinks, small["hgrn_norm_g"], dlb, loss_part], axis=1))
    for kind, r in zip(("grad", "delta", "m", "v"), _adam_small(total, w, mom, var, lb_of)):
        outs[kind].update(r)
    loss = total[0, SM_LOSS]

    return (loss, grad_x.reshape(x.shape), *[outs["grad"][k] for k in _ORDER], *[outs["delta"][k] for k in _ORDER],
            *[outs["m"][k] for k in _ORDER], *[outs["v"][k] for k in _ORDER])
```

```python
import functools

import jax
import jax.numpy as jnp
from jax import lax
from jax.experimental import pallas as pl
from jax.experimental.pallas import tpu as pltpu

F32 = jnp.float32
BF16 = jnp.bfloat16

D_MODEL = 1024
N_Q_HEADS = 16
N_KV_HEADS = 4
HEAD_DIM = 64
ATTN_BLOCK = 128
ROPE_THETA = 500000.0
ROPE_DIM = HEAD_DIM // 4
HGRN_HEADS = 8
HGRN_DK = 128
HGRN_CHUNK = 64
D_FF = 2816
D_IN = 7680
DEEPNORM_ALPHA = 2 ** 0.25
LN_EPS = 1e-5
RMS_EPS = 1e-6
NEG_INF = -1e30

ADAM_LR = 0.001
ADAM_B1 = 0.9
ADAM_B2 = 0.999
ADAM_EPS = 1e-08
ADAM_WD = 0.01
ADAM_STEP = 10

N_CHIPS = 4
N_DEV = 8
LANES = 128
FF_GRAD_PARTS = 2
FFN_TILE = 256
IN_SHARD = D_IN // N_CHIPS
PROJ_SHARD = D_MODEL // N_CHIPS
ROW_TILE = 512
GRAD_ROWS = 2048
MIX_TILE = 256
UPDATE_ROWS = 128
HGRN_CHUNKS_PER_STEP = 32
W_IN_ROWS_FIRST = 640
VMEM_LIMIT = 56 * 1024 * 1024

GATES_WIDTH = 2 * D_MODEL
HGRN_HEAD_WIDTH = 4 * HGRN_DK
ATTN_WIDTH = D_MODEL + 2 * N_KV_HEADS * HEAD_DIM
COL_HGRN = GATES_WIDTH // HGRN_HEAD_WIDTH
COL_ATTN = (GATES_WIDTH + HGRN_HEADS * HGRN_HEAD_WIDTH) // ATTN_WIDTH
COL_Q = (GATES_WIDTH + HGRN_HEADS * HGRN_HEAD_WIDTH) // D_MODEL
COL_K = (GATES_WIDTH + HGRN_HEADS * HGRN_HEAD_WIDTH + D_MODEL) // (N_KV_HEADS * HEAD_DIM)
COL_V = COL_K + 1


def _to_kernel_cols(a):
    lead = a.shape[:-1]
    qkv, hg, gates = a[..., :ATTN_WIDTH], a[..., ATTN_WIDTH:D_IN - GATES_WIDTH], a[..., D_IN - GATES_WIDTH:]
    hg = jnp.swapaxes(hg.reshape(*lead, 4, HGRN_HEADS, HGRN_DK), -3, -2).reshape(*lead, -1)
    return jnp.concatenate([gates, hg, qkv], axis=-1)


def _from_kernel_cols(a):
    lead = a.shape[:-1]
    gates, hg, qkv = a[..., :GATES_WIDTH], a[..., GATES_WIDTH:D_IN - ATTN_WIDTH], a[..., D_IN - ATTN_WIDTH:]
    hg = jnp.swapaxes(hg.reshape(*lead, HGRN_HEADS, 4, HGRN_DK), -3, -2).reshape(*lead, -1)
    return jnp.concatenate([qkv, hg, gates], axis=-1)

SM_LN = 0
SM_BIN = 6 * D_MODEL
SM_SINK = SM_BIN + D_IN
SM_NG = SM_SINK + LANES
SM_LB = SM_NG + LANES
SM_LOSS = SM_LB + D_MODEL
SM_LEN = SM_LOSS + LANES

MESH = pl.DeviceIdType.MESH


def _mm(a, b):
    return lax.dot_general(a, b, (((1,), (0,)), ((), ())), preferred_element_type=F32)


def _mm_nt(a, b):
    return lax.dot_general(a, b, (((1,), (1,)), ((), ())), preferred_element_type=F32)


def _mm_tn(a, b):
    return lax.dot_general(a, b, (((0,), (0,)), ((), ())), preferred_element_type=F32)


def _bf(v):
    return v.astype(BF16)


def _sig(v):
    return jax.nn.sigmoid(v)


def _ln(z, g, b):
    mu = jnp.mean(z, axis=-1, keepdims=True)
    zc = z - mu
    var = jnp.mean(zc * zc, axis=-1, keepdims=True)
    return zc * lax.rsqrt(var + LN_EPS) * g + b


def _swiglu_act(h1, h3):
    return (h1 * _sig(h1)) * h3


def _params(sem=None):
    return pltpu.CompilerParams(dimension_semantics=sem, vmem_limit_bytes=VMEM_LIMIT)


def _full(shape):
    nd = len(shape)
    return pl.BlockSpec(shape, lambda *_: (0,) * nd)


def _update_rows(rows):
    return max(t for t in range(8, UPDATE_ROWS + 1, 8) if rows % t == 0)


def _resident(shape):
    nd = len(shape)
    return pl.BlockSpec(shape, lambda *_: (0,) * nd, pipeline_mode=pl.Buffered(1))


def _ffn_fwd(name, x, w1t, w3t, w2, g, b, target=None, comm=None):
    n = x.shape[0]
    tm = min(FFN_TILE, n)
    final = target is not None

    def body(*refs):
        if final:
            x_ref, w1_ref, w3_ref, w2_ref, g_ref, b_ref, t_ref, z_ref, o_ref, xb_ref, h1_ref, h3_ref, loss_ref = refs
        else:
            x_ref, w1_ref, w3_ref, w2_ref, g_ref, b_ref, z_ref, o_ref, xb_ref, h1_ref, h3_ref, ob_ref = refs
        xb = _bf(x_ref[...])
        xb_ref[...] = xb
        h1 = _mm_nt(xb, w1_ref[...])
        h3 = _mm_nt(xb, w3_ref[...])
        h1_ref[...] = _bf(h1)
        h3_ref[...] = _bf(h3)
        z = DEEPNORM_ALPHA * x_ref[...] + 0.5 * _mm(_bf(_swiglu_act(h1, h3)), w2_ref[...])
        z_ref[...] = z
        y = _ln(z, g_ref[...], b_ref[...])
        if final:
            e = y - t_ref[...]

            @pl.when(pl.program_id(0) == 0)
            def _():
                loss_ref[...] = jnp.zeros_like(loss_ref)

            loss_ref[...] += jnp.sum(e * e) * (0.5 / D_MODEL)
            o_ref[...] = e * (1.0 / D_MODEL)
        else:
            o_ref[...] = y
            ob_ref[...] = _bf(y)

    row = pl.BlockSpec((tm, D_MODEL), lambda i: (i, 0))
    hid = pl.BlockSpec((tm, D_FF), lambda i: (i, 0))
    wres = _resident((D_FF, D_MODEL))
    vec = _full((1, D_MODEL))
    in_specs = [row, wres, wres, wres, vec, vec]
    args = [x, w1t, w3t, w2, g, b]
    hid_shape = jax.ShapeDtypeStruct((n, D_FF), BF16)
    out_specs = [row, row, row, hid, hid]
    out_shape = ([jax.ShapeDtypeStruct((n, D_MODEL), F32)] * 2 + [jax.ShapeDtypeStruct((n, D_MODEL), BF16)]
                 + [hid_shape] * 2)
    if final:
        in_specs.append(row)
        args.append(target)
        out_specs.append(_full((1, LANES)))
        out_shape.append(jax.ShapeDtypeStruct((1, LANES), F32))
    else:
        out_specs.append(row)
        out_shape.append(jax.ShapeDtypeStruct((n, D_MODEL), BF16))
    return _hosted(
        body, comm, name=name, grid=(n // tm,), in_specs=in_specs, out_specs=out_specs, out_shape=out_shape,
        scratch_shapes=[], compiler_params=_params(("arbitrary",)), args=args)


def _ffn_bwd(name, h1s, h3s, z, dout, w1t, w3t, w2, g, b, comm=None):
    n = z.shape[0]
    tm = min(FFN_TILE, n)

    def body(h1_ref, h3_ref, z_ref, do_ref, w1_ref, w3_ref, w2_ref, g_ref, b_ref,
             dx_ref, a_ref, dh1_ref, dh3_ref, df_ref, dg_ref, db_ref):
        _, vjp = jax.vjp(_ln, z_ref[...], g_ref[...], b_ref[...])
        dz, dg, db = vjp(do_ref[...])

        @pl.when(pl.program_id(0) == 0)
        def _():
            dg_ref[...] = jnp.zeros_like(dg_ref)
            db_ref[...] = jnp.zeros_like(db_ref)

        dg_ref[...] += dg
        db_ref[...] += db
        df = _bf(0.5 * dz)
        df_ref[...] = df
        a, act_vjp = jax.vjp(_swiglu_act, h1_ref[...].astype(F32), h3_ref[...].astype(F32))
        dh1, dh3 = act_vjp(_mm_nt(df, w2_ref[...]))
        dh1 = _bf(dh1)
        dh3 = _bf(dh3)
        a_ref[...] = _bf(a)
        dh1_ref[...] = dh1
        dh3_ref[...] = dh3
        dx_ref[...] = DEEPNORM_ALPHA * dz + _mm(dh1, w1_ref[...]) + _mm(dh3, w3_ref[...])

    row = pl.BlockSpec((tm, D_MODEL), lambda i: (i, 0))
    hid = pl.BlockSpec((tm, D_FF), lambda i: (i, 0))
    wres = _resident((D_FF, D_MODEL))
    vec = _full((1, D_MODEL))
    hid_shape = jax.ShapeDtypeStruct((n, D_FF), BF16)
    return _hosted(
        body, comm, name=name, grid=(n // tm,),
        in_specs=[hid, hid, row, row, wres, wres, wres, vec, vec],
        out_specs=[row, hid, hid, hid, row, vec, vec],
        out_shape=[jax.ShapeDtypeStruct((n, D_MODEL), F32), hid_shape, hid_shape, hid_shape,
                   jax.ShapeDtypeStruct((n, D_MODEL), BF16),
                   jax.ShapeDtypeStruct((1, D_MODEL), F32), jax.ShapeDtypeStruct((1, D_MODEL), F32)],
        scratch_shapes=[], compiler_params=_params(("arbitrary",)),
        args=[h1s, h3s, z, dout, w1t, w3t, w2, g, b])


def _operand_spec(arr, mode, tn, width, parts):
    if mode == "shared":
        return pl.BlockSpec((tn, width), lambda s, k: (k, 0))
    assert mode == "cols" and arr.shape[1] == parts * width
    return pl.BlockSpec((tn, width), lambda s, k: (k, s))


def _grad_matmul(name, a, a_mode, ka, b, b_mode, kb, colsum=False, comm=None, parts=N_CHIPS):
    n = a.shape[-2]
    tn = min(GRAD_ROWS, n)
    nk = n // tn

    def body(*refs):
        if colsum:
            a_ref, b_ref, o_ref, cs_ref, acc = refs
        else:
            a_ref, b_ref, o_ref, acc = refs
        k = pl.program_id(1)
        av = a_ref[...]
        bv = b_ref[...]

        @pl.when(k == 0)
        def _():
            acc[...] = jnp.zeros_like(acc)
            if colsum:
                cs_ref[...] = jnp.zeros_like(cs_ref)

        acc[...] += _mm_tn(av, bv)
        if colsum:
            cs_ref[...] += jnp.sum(bv.astype(F32), axis=0, keepdims=True)

        @pl.when(k == nk - 1)
        def _():
            o_ref[0] = _bf(acc[...])

    out_specs = [pl.BlockSpec((1, ka, kb), lambda s, k: (s, 0, 0))]
    out_shape = [jax.ShapeDtypeStruct((parts, ka, kb), BF16)]
    if colsum:
        out_specs.append(pl.BlockSpec((1, kb), lambda s, k: (0, s)))
        out_shape.append(jax.ShapeDtypeStruct((1, parts * kb), F32))
    return _hosted(
        body, comm, name=name, grid=(parts, nk),
        in_specs=[_operand_spec(a, a_mode, tn, ka, parts), _operand_spec(b, b_mode, tn, kb, parts)],
        out_specs=out_specs, out_shape=out_shape,
        scratch_shapes=[pltpu.VMEM((ka, kb), F32)],
        compiler_params=_params(("arbitrary", "arbitrary")), args=[a, b])


def _kernel_block_of(ref_block):
    attn_blocks, gate_blocks = ATTN_WIDTH // LANES, GATES_WIDTH // LANES
    hgrn_blocks = HGRN_HEADS * HGRN_HEAD_WIDTH // LANES
    if ref_block < attn_blocks:
        return gate_blocks + hgrn_blocks + ref_block
    if ref_block < attn_blocks + hgrn_blocks:
        kind, head = divmod(ref_block - attn_blocks, HGRN_HEADS)
        return gate_blocks + head * (HGRN_HEAD_WIDTH // LANES) + kind
    return ref_block - attn_blocks - hgrn_blocks


def _reorder_w_in(name, w4, to_kernel_order):
    per = IN_SHARD // LANES
    tr = MIX_TILE

    def body(i_ref, o_ref):
        for g in range(D_IN // LANES):
            s, b = divmod(g, per)
            k = _kernel_block_of(g)
            if to_kernel_order:
                o_ref[:, k * LANES:(k + 1) * LANES] = i_ref[s, :, b * LANES:(b + 1) * LANES]
            else:
                ks, kb = divmod(k, per)
                o_ref[s, :, b * LANES:(b + 1) * LANES] = i_ref[ks, :, kb * LANES:(kb + 1) * LANES]

    in_spec = pl.BlockSpec((N_CHIPS, tr, IN_SHARD), lambda i: (0, i, 0))
    if to_kernel_order:
        out_spec, out_shape = pl.BlockSpec((tr, D_IN), lambda i: (i, 0)), (D_MODEL, D_IN)
    else:
        out_spec, out_shape = in_spec, (N_CHIPS, D_MODEL, IN_SHARD)
    return pl.pallas_call(
        body, name=name, grid=(D_MODEL // tr,), in_specs=[in_spec], out_specs=out_spec,
        out_shape=jax.ShapeDtypeStruct(out_shape, w4.dtype), compiler_params=_params(("arbitrary",)),
    )(w4)


def _in_proj(x1, w_in_g, b_in, comm=None):
    n = x1.shape[0]
    tm = min(ROW_TILE, n)

    def body(x_ref, w_ref, b_ref, o_ref):
        xv = x_ref[...]
        for j in range(N_CHIPS):
            cols = slice(j * IN_SHARD, (j + 1) * IN_SHARD)
            o_ref[:, cols] = _mm(xv, w_ref[:, cols]) + b_ref[:, cols]

    return _hosted(
        body, comm, name="in_proj", grid=(n // tm,),
        in_specs=[pl.BlockSpec((tm, D_MODEL), lambda i: (i, 0)),
                  _resident((D_MODEL, D_IN)), _full((1, D_IN))],
        out_specs=[pl.BlockSpec((tm, D_IN), lambda i: (i, 0))],
        out_shape=[jax.ShapeDtypeStruct((n, D_IN), F32)],
        scratch_shapes=[],
        compiler_params=_params(("arbitrary",)), args=[x1, w_in_g, b_in])


def _in_proj_dx(dproj, w_in_g, dz2, comm=None):
    n = dproj.shape[0]
    tm = min(ROW_TILE, n)

    def body(dp_ref, w_ref, dz_ref, o_ref):
        dx = DEEPNORM_ALPHA * dz_ref[...]
        for j in range(N_CHIPS):
            cols = slice(j * IN_SHARD, (j + 1) * IN_SHARD)
            dx = dx + _mm_nt(dp_ref[:, cols], w_ref[:, cols])
        o_ref[...] = dx

    return _hosted(
        body, comm, name="in_proj_dx", grid=(n // tm,),
        in_specs=[pl.BlockSpec((tm, D_IN), lambda i: (i, 0)),
                  _resident((D_MODEL, D_IN)),
                  pl.BlockSpec((tm, D_MODEL), lambda i: (i, 0))],
        out_specs=[pl.BlockSpec((tm, D_MODEL), lambda i: (i, 0))],
        out_shape=[jax.ShapeDtypeStruct((n, D_MODEL), F32)],
        scratch_shapes=[],
        compiler_params=_params(("arbitrary",)), args=[dproj, w_in_g, dz2])


def _rope_tables(seq_len):
    pos = jnp.arange(seq_len, dtype=F32)
    inv_freq = ROPE_THETA ** (-jnp.arange(0, ROPE_DIM, 2, dtype=F32) / ROPE_DIM)
    ang = pos[:, None] * inv_freq[None, :]
    cos, sin = jnp.cos(ang), jnp.sin(ang)
    half = ROPE_DIM // 2
    rest = HEAD_DIM - ROPE_DIM
    ones = jnp.ones((seq_len, rest), F32)
    zeros = jnp.zeros((seq_len, rest), F32)
    zh = jnp.zeros((seq_len, half), F32)
    c = jnp.concatenate([cos, cos, ones], axis=1)
    sa = jnp.concatenate([-sin, zh, zeros], axis=1)
    sb = jnp.concatenate([zh, sin, zeros], axis=1)
    reps = LANES // HEAD_DIM
    return tuple(jnp.tile(t, (1, reps)) for t in (c, sa, sb))


def _rope(t, c, sa, sb):
    w = t.shape[1]
    reps = w // LANES
    half = ROPE_DIM // 2
    return (t * jnp.tile(c, (1, reps)) + pltpu.roll(t, w - half, 1) * jnp.tile(sa, (1, reps))
            + pltpu.roll(t, half, 1) * jnp.tile(sb, (1, reps)))


def _rope_transposed(g, c, sa, sb):
    w = g.shape[1]
    reps = w // LANES
    half = ROPE_DIM // 2
    return (g * jnp.tile(c, (1, reps)) + pltpu.roll(g * jnp.tile(sa, (1, reps)), half, 1)
            + pltpu.roll(g * jnp.tile(sb, (1, reps)), w - half, 1))


GROUP = N_Q_HEADS // N_KV_HEADS


def _both_halves(t_pair, which):
    lo = lax.broadcasted_iota(jnp.int32, t_pair.shape, 1) < HEAD_DIM
    swapped = pltpu.roll(t_pair, HEAD_DIM, 1)
    return _bf(jnp.where(lo, t_pair, swapped) if which == 0 else jnp.where(lo, swapped, t_pair))


def _stack_heads(ref_or_val, kh):
    lo = lax.broadcasted_iota(jnp.int32, (ATTN_BLOCK, LANES), 1) < HEAD_DIM
    rows = []
    for gp in range(GROUP // 2):
        pair = kh * (GROUP // 2) + gp
        t = ref_or_val[:, pair * LANES:(pair + 1) * LANES]
        rows += [jnp.where(lo, t, jnp.zeros_like(t)), jnp.where(lo, jnp.zeros_like(t), t)]
    return jnp.concatenate(rows, axis=0)


def _unstack_pairs(stacked):
    lo = lax.broadcasted_iota(jnp.int32, (ATTN_BLOCK, LANES), 1) < HEAD_DIM
    b = ATTN_BLOCK
    return [jnp.where(lo, stacked[2 * gp * b:(2 * gp + 1) * b], stacked[(2 * gp + 1) * b:(2 * gp + 2) * b])
            for gp in range(GROUP // 2)]


def _attn_mask_t(n):
    cols = GROUP * ATTN_BLOCK
    kj = lax.broadcasted_iota(jnp.int32, (2 * ATTN_BLOCK, cols), 0)
    qi = lax.broadcasted_iota(jnp.int32, (2 * ATTN_BLOCK, cols), 1) % ATTN_BLOCK
    dist = qi + ATTN_BLOCK - kj
    return (dist >= 0) & (dist < ATTN_BLOCK) & (n * ATTN_BLOCK + kj - ATTN_BLOCK >= 0)


def _sink_row(sink_ref, kh):
    col = lax.broadcasted_iota(jnp.int32, (1, GROUP * ATTN_BLOCK), 1)
    row = jnp.full((1, GROUP * ATTN_BLOCK), sink_ref[0, kh * GROUP + GROUP - 1], F32)
    for i in reversed(range(GROUP - 1)):
        row = jnp.where(col < (i + 1) * ATTN_BLOCK, sink_ref[0, kh * GROUP + i], row)
    return row


def _attn_probs_t(q_masked, k_sel, mask_t, sink):
    s = _mm_nt(k_sel, q_masked) * (HEAD_DIM ** -0.5)
    s = jnp.where(mask_t, s, NEG_INF)
    m = jnp.maximum(jnp.max(s, axis=0, keepdims=True), sink)
    p = jnp.exp(s - m)
    e_sink = jnp.exp(sink - m)
    denom = jnp.sum(p, axis=0, keepdims=True) + e_sink
    return p / denom, e_sink / denom


def _attn_fwd(proj, tabs, sinks):
    n_tok = proj.shape[0]
    nb = n_tok // ATTN_BLOCK

    def body(q_ref, k_ref, v_ref, c_ref, sa_ref, sb_ref, sink_ref, y_ref, kprev, vprev):
        n = pl.program_id(0)

        @pl.when(n == 0)
        def _():
            kprev[...] = jnp.zeros_like(kprev)
            vprev[...] = jnp.zeros_like(vprev)

        c, sa, sb = c_ref[...], sa_ref[...], sb_ref[...]
        qr = _bf(_rope(q_ref[...], c, sa, sb))
        kr = _rope(k_ref[...], c, sa, sb)
        vc = v_ref[...]
        kk = jnp.concatenate([kprev[...], kr], axis=0)
        vv = jnp.concatenate([vprev[...], vc], axis=0)
        kprev[...] = kr
        vprev[...] = vc
        mask = _attn_mask_t(n)
        for kh in range(N_KV_HEADS):
            r, which = divmod(kh, 2)
            kb = _both_halves(kk[:, r * LANES:(r + 1) * LANES], which)
            vb = _both_halves(vv[:, r * LANES:(r + 1) * LANES], which)
            probs, _ = _attn_probs_t(_stack_heads(qr, kh), kb, mask, _sink_row(sink_ref, kh))
            for gp, out in enumerate(_unstack_pairs(_mm_tn(_bf(probs), vb))):
                pair = kh * (GROUP // 2) + gp
                y_ref[:, pair * LANES:(pair + 1) * LANES] = _bf(out)

    blk = lambda width, col: pl.BlockSpec((ATTN_BLOCK, width), lambda n: (n, col))
    tab = pl.BlockSpec((ATTN_BLOCK, LANES), lambda n: (n, 0))
    kvw = N_KV_HEADS * HEAD_DIM
    return pl.pallas_call(
        body, name="attn_fwd", grid=(nb,),
        in_specs=[blk(D_MODEL, COL_Q), blk(kvw, COL_K), blk(kvw, COL_V), tab, tab, tab,
                  pl.BlockSpec(memory_space=pltpu.SMEM)],
        out_specs=pl.BlockSpec((ATTN_BLOCK, D_MODEL), lambda n: (n, 0)),
        out_shape=jax.ShapeDtypeStruct((n_tok, D_MODEL), BF16),
        scratch_shapes=[pltpu.VMEM((ATTN_BLOCK, kvw), F32), pltpu.VMEM((ATTN_BLOCK, kvw), F32)],
        compiler_params=_params(("arbitrary",)),
    )(proj, proj, proj, *tabs, sinks)


def _attn_bwd(proj, dy, tabs, sinks, dproj, comm=None):
    n_tok = proj.shape[0]
    nb = n_tok // ATTN_BLOCK
    kvw = N_KV_HEADS * HEAD_DIM

    def body(q_ref, k_ref, v_ref, do_ref, c_ref, sa_ref, sb_ref, cp_ref, sap_ref, sbp_ref, sink_ref, _,
             dqkv_ref, dsink_ref, kprev, vprev, dkc, dvc, dqc):
        n = pl.program_id(0)

        @pl.when(n == 0)
        def _():
            for ref in (kprev, vprev, dkc, dvc, dqc, dsink_ref):
                ref[...] = jnp.zeros_like(ref)

        prev_tabs = (cp_ref[...], sap_ref[...], sbp_ref[...])

        @pl.when(n < nb)
        def _():
            c, sa, sb = c_ref[...], sa_ref[...], sb_ref[...]
            qr = _bf(_rope(q_ref[...], c, sa, sb))
            kr = _rope(k_ref[...], c, sa, sb)
            vc = v_ref[...]
            kk = jnp.concatenate([kprev[...], kr], axis=0)
            vv = jnp.concatenate([vprev[...], vc], axis=0)
            kprev[...] = kr
            vprev[...] = vc
            mask = _attn_mask_t(n)
            lane = lax.broadcasted_iota(jnp.int32, (1, LANES), 1)
            lo2 = lax.broadcasted_iota(jnp.int32, (2 * ATTN_BLOCK, LANES), 1) < HEAD_DIM
            dsink = jnp.zeros((1, LANES), F32)
            dq_pairs = []
            dk_full = []
            dv_full = []
            for kh in range(N_KV_HEADS):
                r, which = divmod(kh, 2)
                kb = _both_halves(kk[:, r * LANES:(r + 1) * LANES], which)
                vb = _both_halves(vv[:, r * LANES:(r + 1) * LANES], which)
                qs = _stack_heads(qr, kh)
                dos = _stack_heads(do_ref, kh)
                probs, p_sink = _attn_probs_t(qs, kb, mask, _sink_row(sink_ref, kh))
                dp = _mm_nt(vb, dos)
                delta = jnp.sum(probs * dp, axis=0, keepdims=True)
                ds = _bf(probs * (dp - delta) * (HEAD_DIM ** -0.5))
                sink_terms = p_sink * delta
                for i in range(GROUP):
                    head_sum = jnp.sum(sink_terms[:, i * ATTN_BLOCK:(i + 1) * ATTN_BLOCK])
                    dsink = dsink + jnp.where(lane == kh * GROUP + i, -head_sum, 0.0)
                dq_pairs += _unstack_pairs(_mm_tn(ds, kb))
                dk_acc = _mm(ds, qs)
                dv_acc = _mm(_bf(probs), dos)
                dk_full.append(dk_acc + pltpu.roll(dk_acc, HEAD_DIM, 1))
                dv_full.append(dv_acc + pltpu.roll(dv_acc, HEAD_DIM, 1))
            dk_pairs = [jnp.where(lo2, dk_full[2 * r], dk_full[2 * r + 1]) for r in range(N_KV_HEADS // 2)]
            dv_pairs = [jnp.where(lo2, dv_full[2 * r], dv_full[2 * r + 1]) for r in range(N_KV_HEADS // 2)]
            dsink_ref[...] += dsink
            dqkv_ref[:, :D_MODEL] = _bf(dqc[...])
            dqc[...] = _rope_transposed(jnp.concatenate(dq_pairs, axis=1), c, sa, sb)
            dk_all = jnp.concatenate(dk_pairs, axis=1)
            dv_all = jnp.concatenate(dv_pairs, axis=1)
            dqkv_ref[:, D_MODEL:D_MODEL + kvw] = _bf(_rope_transposed(dkc[...] + dk_all[:ATTN_BLOCK], *prev_tabs))
            dqkv_ref[:, D_MODEL + kvw:] = _bf(dvc[...] + dv_all[:ATTN_BLOCK])
            dkc[...] = dk_all[ATTN_BLOCK:]
            dvc[...] = dv_all[ATTN_BLOCK:]

        @pl.when(n == nb)
        def _():
            dqkv_ref[:, :D_MODEL] = _bf(dqc[...])
            dqkv_ref[:, D_MODEL:D_MODEL + kvw] = _bf(_rope_transposed(dkc[...], *prev_tabs))
            dqkv_ref[:, D_MODEL + kvw:] = _bf(dvc[...])

    cur = lambda n: jnp.minimum(n, nb - 1)
    prev = lambda n: jnp.maximum(n - 1, 0)
    blk = lambda width, col: pl.BlockSpec((ATTN_BLOCK, width), lambda n: (cur(n), col))
    tab = pl.BlockSpec((ATTN_BLOCK, LANES), lambda n: (cur(n), 0))
    tabp = pl.BlockSpec((ATTN_BLOCK, LANES), lambda n: (prev(n), 0))
    return _hosted(
        body, comm, name="attn_bwd", grid=(nb + 1,),
        in_specs=[blk(D_MODEL, COL_Q), blk(kvw, COL_K), blk(kvw, COL_V), blk(D_MODEL, 0), tab, tab, tab, tabp, tabp, tabp,
                  pl.BlockSpec(memory_space=pltpu.SMEM), pl.BlockSpec(memory_space=pl.ANY)],
        out_specs=[pl.BlockSpec((ATTN_BLOCK, ATTN_WIDTH), lambda n: (prev(n), COL_ATTN)),
                   pl.BlockSpec((1, LANES), lambda n: (0, 0))],
        out_shape=[jax.ShapeDtypeStruct(dproj.shape, dproj.dtype), jax.ShapeDtypeStruct((1, LANES), F32)],
        scratch_shapes=[pltpu.VMEM((ATTN_BLOCK, kvw), F32)] * 4 + [pltpu.VMEM((ATTN_BLOCK, D_MODEL), F32)],
        compiler_params=_params(("arbitrary",)), args=[proj, proj, proj, dy, *tabs, *tabs, sinks, dproj],
        aliases={11: 0})


def _bmm(a, b):
    return lax.dot_general(a, b, (((2,), (1,)), ((0,), (0,))), preferred_element_type=F32)


def _bmm_nt(a, b):
    return lax.dot_general(a, b, (((2,), (2,)), ((0,), (0,))), preferred_element_type=F32)


def _bmm_tn(a, b):
    return lax.dot_general(a, b, (((1,), (1,)), ((0,), (0,))), preferred_element_type=F32)


def _tril(cb, upper=False):
    shape = (cb, HGRN_CHUNK, HGRN_CHUNK)
    r, c = lax.broadcasted_iota(jnp.int32, shape, 1), lax.broadcasted_iota(jnp.int32, shape, 2)
    return (r <= c) if upper else (r >= c)


def _tri_matmul(x, upper):
    return lax.dot_general(_tril(x.shape[0], upper).astype(F32), x, (((2,), (1,)), ((0,), (0,))),
                           precision=lax.Precision.HIGHEST, preferred_element_type=F32)


@jax.custom_vjp
def _chunk_cumsum(x):
    return _tri_matmul(x, False)


_chunk_cumsum.defvjp(lambda x: (_tri_matmul(x, False), None), lambda _, g: (_tri_matmul(g, True),))


def _hg_elem(fl, qh, lb):
    f = lb + (1.0 - lb) * _sig(fl)
    k = 1.0 - f
    gc = _chunk_cumsum(jnp.log(f))
    last = lax.broadcasted_iota(jnp.int32, gc.shape, 1) == HGRN_CHUNK - 1
    g_last = jnp.sum(jnp.where(last, gc, 0.0), axis=1, keepdims=True)
    q = qh * _sig(qh)
    return q * jnp.exp(gc), k * jnp.exp(-gc), k * jnp.exp(g_last - gc), jnp.exp(g_last)


def _hg_out(q_dec, k_inv, v, st):
    sc = jnp.where(_tril(q_dec.shape[0]), _bmm_nt(_bf(q_dec), _bf(k_inv)), 0.0)
    return _bmm(_bf(sc), _bf(v)) + _bmm_nt(_bf(q_dec), _bf(st)), sc


def _hg_post(o, og, ng):
    on = o * lax.rsqrt(jnp.mean(o * o, axis=-1, keepdims=True) + RMS_EPS) * ng
    return on * (og * _sig(og))


def _hgrn_specs(n_tok, rev):
    nc = n_tok // HGRN_CHUNK
    cb = min(HGRN_CHUNKS_PER_STEP, nc)
    nt = nc // cb
    rows = cb * HGRN_CHUNK
    tt = (lambda t: nt - 1 - t) if rev else (lambda t: t)
    col = lambda base: pl.BlockSpec((rows, LANES), lambda h, t: (tt(t), base + h))
    head_cols = pl.BlockSpec((rows, HGRN_HEAD_WIDTH), lambda h, t: (tt(t), COL_HGRN + h))
    head_vec = pl.BlockSpec((1, LANES), lambda h, t: (0, h))
    one_vec = pl.BlockSpec((1, LANES), lambda h, t: (0, 0))
    state = pl.BlockSpec((1, cb, HGRN_DK, HGRN_DK), lambda h, t: (h, tt(t), 0, 0))
    return nc, cb, nt, col, head_cols, head_vec, one_vec, state


def _hgrn_fwd(proj, lb, ng):
    n_tok = proj.shape[0]
    nc, cb, nt, col, head_cols, head_vec, one_vec, state = _hgrn_specs(n_tok, False)

    def body(in_ref, lb_ref, ng_ref, y_ref, st_ref, s_acc):
        @pl.when(pl.program_id(1) == 0)
        def _():
            s_acc[...] = jnp.zeros_like(s_acc)

        fl, qh, v, og = (in_ref[:, i * LANES:(i + 1) * LANES].reshape(cb, HGRN_CHUNK, LANES) for i in range(4))
        q_dec, k_inv, k_end, decay = _hg_elem(fl, qh, lb_ref[...])
        upd = _bmm_tn(_bf(v), _bf(k_end))
        st = s_acc[...]
        for ci in range(cb):
            st_ref[0, ci] = st
            st = st * decay[ci] + upd[ci]
        s_acc[...] = st
        o, _ = _hg_out(q_dec, k_inv, v, st_ref[0])
        y_ref[...] = _bf(_hg_post(o, og, ng_ref[...]).reshape(cb * HGRN_CHUNK, LANES))

    return pl.pallas_call(
        body, name="hgrn_fwd", grid=(HGRN_HEADS, nt),
        in_specs=[head_cols, head_vec, one_vec],
        out_specs=[col(0), state],
        out_shape=[jax.ShapeDtypeStruct((n_tok, D_MODEL), BF16),
                   jax.ShapeDtypeStruct((HGRN_HEADS, nc, HGRN_DK, HGRN_DK), F32)],
        scratch_shapes=[pltpu.VMEM((HGRN_DK, HGRN_DK), F32)],
        compiler_params=_params(("arbitrary", "arbitrary")),
    )(proj, lb, ng)


def _hgrn_bwd(proj, lb, ng, states, dy, dproj, comm=None):
    n_tok = proj.shape[0]
    nc, cb, nt, col, head_cols, head_vec, one_vec, state = _hgrn_specs(n_tok, True)

    def body(in_ref, lb_ref, ng_ref, st_ref, dy_ref, _, d_ref, dlb_ref, dng_ref, g_acc, g_all):
        h = pl.program_id(0)
        t = pl.program_id(1)

        @pl.when(t == 0)
        def _():
            g_acc[...] = jnp.zeros_like(g_acc)
            dlb_ref[...] = jnp.zeros_like(dlb_ref)

        @pl.when((t == 0) & (h == 0))
        def _():
            dng_ref[...] = jnp.zeros_like(dng_ref)

        fl, qh, v, og = (in_ref[:, i * LANES:(i + 1) * LANES].reshape(cb, HGRN_CHUNK, LANES) for i in range(4))
        (q_dec, k_inv, k_end, decay), elem_vjp = jax.vjp(_hg_elem, fl, qh, lb_ref[...])
        st = st_ref[0]
        o, sc = _hg_out(q_dec, k_inv, v, st)
        _, post_vjp = jax.vjp(_hg_post, o, og, ng_ref[...])
        do, dog, dng = post_vjp(dy_ref[...].reshape(cb, HGRN_CHUNK, LANES))
        dob, vb, qb = _bf(do), _bf(v), _bf(q_dec)
        dsc = _bf(jnp.where(_tril(cb), _bmm_nt(dob, vb), 0.0))
        p = _bmm_tn(dob, qb)
        g = g_acc[...]
        for ci in reversed(range(cb)):
            g_all[ci] = g
            g = g * decay[ci] + p[ci]
        g_acc[...] = g
        g = g_all[...]
        gb = _bf(g)
        dq_dec = _bmm(dsc, _bf(k_inv)) + _bmm(dob, _bf(st))
        dk_inv = _bmm_tn(dsc, qb)
        dv = _bmm_tn(_bf(sc), dob) + _bmm_nt(_bf(k_end), gb)
        dk_end = _bmm(vb, gb)
        ddecay = jnp.sum(st * g, axis=1, keepdims=True)
        dfl, dqh, dlb = elem_vjp((dq_dec, dk_inv, dk_end, ddecay))
        for i, val in enumerate((dfl, dqh, dv, dog)):
            d_ref[:, i * LANES:(i + 1) * LANES] = _bf(val.reshape(cb * HGRN_CHUNK, LANES))
        dlb_ref[...] += dlb
        dng_ref[...] += dng

    return _hosted(
        body, comm, name="hgrn_bwd", grid=(HGRN_HEADS, nt),
        in_specs=[head_cols, head_vec, one_vec, state, col(0), pl.BlockSpec(memory_space=pl.ANY)],
        out_specs=[head_cols, head_vec, one_vec],
        out_shape=[jax.ShapeDtypeStruct(dproj.shape, dproj.dtype),
                   jax.ShapeDtypeStruct((1, D_MODEL), F32), jax.ShapeDtypeStruct((1, LANES), F32)],
        scratch_shapes=[pltpu.VMEM((HGRN_DK, HGRN_DK), F32), pltpu.VMEM((cb, HGRN_DK, HGRN_DK), F32)],
        compiler_params=_params(("arbitrary", "arbitrary")), args=[proj, lb, ng, states, dy, dproj], aliases={5: 0})


def _lb_fwd(lb_logits):
    def lb_of(l0, l1):
        m = jnp.maximum(l0, l1)
        e0, e1 = jnp.exp(l0 - m), jnp.exp(l1 - m)
        return e0 / (e0 + e1)

    def body(l_ref, o_ref):
        o_ref[...] = lb_of(l_ref[0:1, :], l_ref[1:2, :])

    lb = pl.pallas_call(body, name="lb_fwd", out_shape=jax.ShapeDtypeStruct((1, D_MODEL), F32))(lb_logits)
    return lb, lb_of


def _mix_fwd(y_attn, y_hgrn, proj, x1, w_pa, w_ph, w_out, g, b):
    n = x1.shape[0]
    tm = min(ROW_TILE, n)

    def body(ya_ref, yh_ref, gt_ref, x_ref, wpa, wph, wo, g_ref, b_ref, z_ref, o_ref, pa_ref, ph_ref):
        ya = _mm(ya_ref[...], wpa[...])
        yh = _mm(yh_ref[...], wph[...])
        pa_ref[...] = _bf(ya)
        ph_ref[...] = _bf(yh)
        merged = _sig(gt_ref[:, :D_MODEL]) * ya + _sig(gt_ref[:, D_MODEL:]) * yh
        z = DEEPNORM_ALPHA * x_ref[...] + _mm(_bf(merged), wo[...])
        z_ref[...] = z
        o_ref[...] = _ln(z, g_ref[...], b_ref[...])

    row = pl.BlockSpec((tm, D_MODEL), lambda i: (i, 0))
    gates = pl.BlockSpec((tm, GATES_WIDTH), lambda i: (i, 0))
    sq = _resident((D_MODEL, D_MODEL))
    vec = _full((1, D_MODEL))
    return pl.pallas_call(
        body, name="mix_fwd", grid=(n // tm,),
        in_specs=[row, row, gates, row, sq, sq, sq, vec, vec],
        out_specs=[row, row, row, row],
        out_shape=[jax.ShapeDtypeStruct((n, D_MODEL), F32)] * 2 + [jax.ShapeDtypeStruct((n, D_MODEL), BF16)] * 2,
        compiler_params=_params(("arbitrary",)),
    )(y_attn, y_hgrn, proj, x1, w_pa, w_ph, w_out, g, b)


def _mix_bwd(dx2, z2, pa, ph, proj, w_pa, w_ph, w_out, g, b):
    n = z2.shape[0]
    tm = min(MIX_TILE, n)

    def body(do_ref, z_ref, ya_ref, yh_ref, gt_ref, wpa, wph, wo, g_ref, b_ref,
             dz_ref, dzb_ref, mg_ref, dya_ref, dyh_ref, dyat_ref, dyhg_ref, dgt_ref, dg_ref, db_ref):
        _, vjp = jax.vjp(_ln, z_ref[...], g_ref[...], b_ref[...])
        dz, dg, db = vjp(do_ref[...])

        @pl.when(pl.program_id(0) == 0)
        def _():
            dg_ref[...] = jnp.zeros_like(dg_ref)
            db_ref[...] = jnp.zeros_like(db_ref)

        dg_ref[...] += dg
        db_ref[...] += db
        dz_ref[...] = dz
        ya = ya_ref[...].astype(F32)
        yh = yh_ref[...].astype(F32)
        def merge(ga, gh, ya, yh):
            return _sig(ga) * ya + _sig(gh) * yh

        merged, merge_vjp = jax.vjp(merge, gt_ref[:, :D_MODEL], gt_ref[:, D_MODEL:], ya, yh)
        mg_ref[...] = _bf(merged)
        dzb = _bf(dz)
        dzb_ref[...] = dzb
        dmerged = _mm_nt(dzb, wo[...])
        dga, dgh, dya, dyh = merge_vjp(dmerged)
        dya = _bf(dya)
        dyh = _bf(dyh)
        dya_ref[...] = dya
        dyh_ref[...] = dyh
        dgt_ref[:, :D_MODEL] = _bf(dga)
        dgt_ref[:, D_MODEL:] = _bf(dgh)
        dyat_ref[...] = _bf(_mm_nt(dya, wpa[...]))
        dyhg_ref[...] = _mm_nt(dyh, wph[...])

    row = pl.BlockSpec((tm, D_MODEL), lambda i: (i, 0))
    gates = pl.BlockSpec((tm, GATES_WIDTH), lambda i: (i, 0))
    sq = _resident((D_MODEL, D_MODEL))
    vec = _full((1, D_MODEL))
    f32_row = jax.ShapeDtypeStruct((n, D_MODEL), F32)
    bf_row = jax.ShapeDtypeStruct((n, D_MODEL), BF16)
    vec_shape = jax.ShapeDtypeStruct((1, D_MODEL), F32)
    return pl.pallas_call(
        body, name="mix_bwd", grid=(n // tm,),
        in_specs=[row, row, row, row, gates, sq, sq, sq, vec, vec],
        out_specs=[row, row, row, row, row, row, row, gates, vec, vec],
        out_shape=[f32_row, bf_row, bf_row, bf_row, bf_row, bf_row, f32_row,
                   jax.ShapeDtypeStruct((n, D_IN), BF16), vec_shape, vec_shape],
        compiler_params=_params(("arbitrary",)),
    )(dx2, z2, pa, ph, proj, w_pa, w_ph, w_out, g, b)


def _position():
    x, y, c = lax.axis_index("x"), lax.axis_index("y"), lax.axis_index("c")
    chips = [(1 - x, y), (x, 1 - y), (1 - x, 1 - y)]
    return x, y, c, chips


def _any_specs(k):
    return [pl.BlockSpec(memory_space=pl.ANY)] * k


class _GatherWeights:
    def __init__(self, shards):
        nw = len(shards)
        self.inputs = list(shards)
        self.out_shape = [jax.ShapeDtypeStruct((N_CHIPS, *s.shape), s.dtype) for s in shards]
        self.scratch = [pltpu.SemaphoreType.DMA((nw,)), pltpu.SemaphoreType.DMA((nw * 6,)),
                        pltpu.SemaphoreType.DMA((nw * 6,))]

    def _copies(self, ins, outs, sems):
        nw = len(ins)
        local_sem, send_sem, recv_sem = sems
        x, y, c, chips = _position()
        me = 2 * x + y
        sibling = (x, y, 1 - c)
        half_rows = [s.shape[0] // 2 for s in self.inputs]

        def half(w, chip_idx, which):
            return outs[w].at[chip_idx, pl.ds(which * half_rows[w], half_rows[w])]

        def remote(w, k, src, dst, to):
            return pltpu.make_async_remote_copy(src_ref=src, dst_ref=dst, send_sem=send_sem.at[w * 6 + k],
                                                recv_sem=recv_sem.at[w * 6 + k], device_id=to, device_id_type=MESH)

        local = [pltpu.make_async_copy(ins[w], outs[w].at[me], local_sem.at[w]) for w in range(nw)]
        first = [remote(w, j, ins[w].at[pl.ds(c * half_rows[w], half_rows[w])], half(w, me, c), (px, py, c))
                 for w in range(nw) for j, (px, py) in enumerate(chips)]
        landed = [half(w, 2 * px + py, c) for w in range(nw) for (px, py) in chips]
        arrive = [remote(w, j, landed[w * 3 + j], landed[w * 3 + j], (px, py, c))
                  for w in range(nw) for j, (px, py) in enumerate(chips)]
        passed = [remote(w, 3 + j, landed[w * 3 + j], landed[w * 3 + j], sibling) for w in range(nw) for j in range(3)]
        from_sibling = [remote(w, 3 + j, half(w, 2 * px + py, 1 - c), half(w, 2 * px + py, 1 - c), sibling)
                        for w in range(nw) for j, (px, py) in enumerate(chips)]
        return local, first, arrive, passed, from_sibling

    def start(self, ins, outs, sems):
        local, first, _, _, _ = self._copies(ins, outs, sems)
        for cp in local + first:
            cp.start()

    def finish(self, ins, outs, sems):
        local, first, arrive, passed, from_sibling = self._copies(ins, outs, sems)
        for cp_in, cp_on in zip(arrive, passed):
            cp_in.wait_recv()
            cp_on.start()
        for cp in from_sibling:
            cp.wait_recv()
        for cp in first + passed:
            cp.wait_send()
        for cp in local:
            cp.wait()


class _ExchangeGrads:
    def __init__(self, grads, rows=None, into=None):
        nw = len(grads)
        self.rows = rows or (0, grads[0].shape[1])
        self.inputs = list(grads) + list(into or [])
        self.aliases = {nw + i: i for i in range(nw)} if into else {}
        self.out_shape = [jax.ShapeDtypeStruct(g.shape, g.dtype) for g in grads]
        self.scratch = [pltpu.SemaphoreType.DMA((nw,)), pltpu.SemaphoreType.DMA((nw * 3,)),
                        pltpu.SemaphoreType.DMA((nw * 3,))]

    def _copies(self, ins, outs, sems):
        nw = len(outs)
        local_sem, send_sem, recv_sem = sems
        x, y, c, chips = _position()
        me = 2 * x + y
        rows = pl.ds(*self.rows)

        def remote(w, j, src, dst, chip):
            return pltpu.make_async_remote_copy(src_ref=src, dst_ref=dst, send_sem=send_sem.at[w * 3 + j],
                                                recv_sem=recv_sem.at[w * 3 + j], device_id=(*chip, c),
                                                device_id_type=MESH)

        local = [pltpu.make_async_copy(ins[w].at[me, rows], outs[w].at[me, rows], local_sem.at[w]) for w in range(nw)]
        sends = [remote(w, j, ins[w].at[2 * px + py, rows], outs[w].at[me, rows], (px, py))
                 for w in range(nw) for j, (px, py) in enumerate(chips)]
        arrive = [remote(w, j, outs[w].at[2 * px + py, rows], outs[w].at[2 * px + py, rows], (px, py))
                  for w in range(nw) for j, (px, py) in enumerate(chips)]
        return local, sends, arrive

    def start(self, ins, outs, sems):
        local, sends, _ = self._copies(ins, outs, sems)
        for cp in local + sends:
            cp.start()

    def finish(self, ins, outs, sems):
        local, sends, arrive = self._copies(ins, outs, sems)
        for cp in arrive:
            cp.wait_recv()
        for cp in sends:
            cp.wait_send()
        for cp in local:
            cp.wait()


def _hosted(body, comm, *, name, grid, in_specs, out_specs, out_shape, scratch_shapes, compiler_params, args,
            aliases=None):
    aliases = aliases or {}
    if comm is None:
        res = pl.pallas_call(body, name=name, grid=grid, in_specs=in_specs, out_specs=out_specs, out_shape=out_shape,
                             scratch_shapes=scratch_shapes, compiler_params=compiler_params,
                             input_output_aliases=aliases)(*args)
        return list(res), []
    n_in, n_out, n_scr = len(in_specs), len(out_specs), len(scratch_shapes)
    c_in, c_out = len(comm.inputs), len(comm.out_shape)
    aliases = {**aliases, **{n_in + i: n_out + o for i, o in getattr(comm, "aliases", {}).items()}}

    def hosted_body(*refs):
        refs = list(refs)
        cut = lambda k: (refs[:k], refs[k:])
        main_in, refs = cut(n_in)
        comm_in, refs = cut(c_in)
        main_out, refs = cut(n_out)
        comm_out, refs = cut(c_out)
        main_scr, comm_scr = cut(n_scr)
        ids = [pl.program_id(a) for a in range(len(grid))]
        first = functools.reduce(jnp.logical_and, [i == 0 for i in ids])
        last = functools.reduce(jnp.logical_and, [i == g - 1 for i, g in zip(ids, grid)])

        @pl.when(first)
        def _():
            comm.start(comm_in, comm_out, comm_scr)

        body(*main_in, *main_out, *main_scr)

        @pl.when(last)
        def _():
            comm.finish(comm_in, comm_out, comm_scr)

    res = pl.pallas_call(
        hosted_body, name=name, grid=grid, in_specs=[*in_specs, *_any_specs(c_in)],
        out_specs=[*out_specs, *_any_specs(c_out)], out_shape=[*out_shape, *comm.out_shape],
        scratch_shapes=[*scratch_shapes, *comm.scratch], compiler_params=compiler_params,
        input_output_aliases=aliases,
    )(*args, *comm.inputs)
    return list(res[:n_out]), list(res[n_out:])


def _run_comm(name, comm):
    def body(*refs):
        refs = list(refs)
        c_in, c_out = len(comm.inputs), len(comm.out_shape)
        ins, outs, sems = refs[:c_in], refs[c_in:c_in + c_out], refs[c_in + c_out:]
        comm.start(ins, outs, sems)
        comm.finish(ins, outs, sems)

    return list(pl.pallas_call(
        body, name=name, in_specs=_any_specs(len(comm.inputs)), out_specs=_any_specs(len(comm.out_shape)),
        out_shape=comm.out_shape, scratch_shapes=comm.scratch,
    )(*comm.inputs))


def _sum_slots(name, slots):
    k = len(slots)
    _, rows, cols = slots[0].shape
    tr = _update_rows(rows)

    def body(*refs):
        for s_ref, o_ref in zip(refs[:k], refs[k:]):
            acc = s_ref[0].astype(F32)
            for i in range(1, N_CHIPS):
                acc = acc + s_ref[i].astype(F32)
            o_ref[...] = acc

    return pl.pallas_call(
        body, name=name, grid=(rows // tr,),
        in_specs=[pl.BlockSpec((N_CHIPS, tr, cols), lambda i: (0, i, 0))] * k,
        out_specs=[pl.BlockSpec((tr, cols), lambda i: (i, 0))] * k,
        out_shape=[jax.ShapeDtypeStruct((rows, cols), F32)] * k,
        compiler_params=_params(("arbitrary",)),
    )(*slots)


class _SwapWithSibling:
    def __init__(self, parts):
        self.inputs = list(parts)
        self.out_shape = [jax.ShapeDtypeStruct(p.shape, p.dtype) for p in parts]
        self.scratch = [pltpu.SemaphoreType.DMA((len(parts),)), pltpu.SemaphoreType.DMA((len(parts),))]

    def _copies(self, ins, outs, sems):
        send_sem, recv_sem = sems
        x, y, c, _ = _position()
        return [pltpu.make_async_remote_copy(src_ref=ins[w], dst_ref=outs[w], send_sem=send_sem.at[w],
                                             recv_sem=recv_sem.at[w], device_id=(x, y, 1 - c), device_id_type=MESH)
                for w in range(len(ins))]

    def start(self, ins, outs, sems):
        for cp in self._copies(ins, outs, sems):
            cp.start()

    def finish(self, ins, outs, sems):
        for cp in self._copies(ins, outs, sems):
            cp.wait()


class _Together:
    def __init__(self, first, second):
        self.parts = (first, second)
        self.inputs = first.inputs + second.inputs
        self.out_shape = first.out_shape + second.out_shape
        self.scratch = first.scratch + second.scratch

    def _split(self, ins, outs, sems):
        a = self.parts[0]
        ni, no, ns = len(a.inputs), len(a.out_shape), len(a.scratch)
        return ((ins[:ni], outs[:no], sems[:ns]), (ins[ni:], outs[no:], sems[ns:]))

    def start(self, ins, outs, sems):
        for part, args in zip(self.parts, self._split(ins, outs, sems)):
            part.start(*args)

    def finish(self, ins, outs, sems):
        for part, args in zip(self.parts, self._split(ins, outs, sems)):
            part.finish(*args)


def _sum_small(part):
    def body(p_ref, o_ref, buf, send_sem, recv_sem):
        x, y, c, _ = _position()
        me = 4 * x + 2 * y + c
        buf[me] = p_ref[...]
        copies = []
        for k in range(1, N_DEV):
            peer = tuple(1 - v if (k >> s) & 1 else v for v, s in ((x, 2), (y, 1), (c, 0)))
            copies.append(pltpu.make_async_remote_copy(src_ref=p_ref, dst_ref=buf.at[me], send_sem=send_sem.at[k - 1],
                                                       recv_sem=recv_sem.at[k - 1], device_id=peer, device_id_type=MESH))
        for cp in copies:
            cp.start()
        for cp in copies:
            cp.wait()
        acc = buf[0]
        for d in range(1, N_DEV):
            acc = acc + buf[d]
        o_ref[...] = acc

    vm = pl.BlockSpec(memory_space=pltpu.VMEM)
    return pl.pallas_call(
        body, name="sum_small", in_specs=[vm], out_specs=vm,
        out_shape=jax.ShapeDtypeStruct((1, SM_LEN), F32),
        scratch_shapes=[pltpu.VMEM((N_DEV, 1, SM_LEN), F32), pltpu.SemaphoreType.DMA((N_DEV - 1,)),
                        pltpu.SemaphoreType.DMA((N_DEV - 1,))],
    )(part)


def _adamw(w, g, m, v):
    m = ADAM_B1 * m + (1.0 - ADAM_B1) * g
    v = ADAM_B2 * v + (1.0 - ADAM_B2) * (g * g)
    m_hat = m / (1.0 - ADAM_B1 ** ADAM_STEP)
    v_hat = v / (1.0 - ADAM_B2 ** ADAM_STEP)
    delta = -ADAM_LR * (m_hat / (jnp.sqrt(v_hat) + ADAM_EPS) + ADAM_WD * w)
    return delta, m, v


def _adam_big(name, groups):
    k = len(groups)
    rows, cols = groups[0][2].shape
    tr = _update_rows(rows)

    def body(*refs):
        for i in range(k):
            p_ref, q_ref, w_ref, m_ref, v_ref = refs[5 * i:5 * i + 5]
            g_ref, d_ref, nm_ref, nv_ref = refs[5 * k + 4 * i:5 * k + 4 * i + 4]
            g = p_ref[...] + q_ref[...]
            g_ref[...] = g
            d_ref[...], nm_ref[...], nv_ref[...] = _adamw(w_ref[...], g, m_ref[...], v_ref[...])

    spec = pl.BlockSpec((tr, cols), lambda i: (i, 0))
    res = pl.pallas_call(
        body, name=name, grid=(rows // tr,), in_specs=[spec] * (5 * k), out_specs=[spec] * (4 * k),
        out_shape=[jax.ShapeDtypeStruct((rows, cols), F32)] * (4 * k),
        compiler_params=_params(("arbitrary",)),
    )(*[a for grp in groups for a in grp])
    return [res[4 * i:4 * i + 4] for i in range(k)]


_SMALL_AT = {"ln1_g": 0, "ln1_b": D_MODEL, "ln2_g": 2 * D_MODEL, "ln2_b": 3 * D_MODEL, "ln3_g": 4 * D_MODEL,
             "ln3_b": 5 * D_MODEL, "b_in": SM_BIN, "attn_sinks": SM_SINK, "hgrn_norm_g": SM_NG}


def _adam_small(total, w, m, v, lb_of):
    names = list(_SMALL)
    k = len(names)

    def body(*refs):
        t_ref = refs[0]
        w_refs, m_refs, v_refs = (refs[1 + i * k:1 + (i + 1) * k] for i in range(3))
        g_refs, d_refs, nm_refs, nv_refs = (refs[1 + (3 + i) * k:1 + (4 + i) * k] for i in range(4))
        for i, name in enumerate(names):
            if name == "hgrn_lb_logits":
                _, vjp = jax.vjp(lb_of, w_refs[i][0:1, :], w_refs[i][1:2, :])
                g_refs[i][0:1, :], g_refs[i][1:2, :] = vjp(t_ref[:, SM_LB:SM_LOSS])
            else:
                at = _SMALL_AT[name]
                g_refs[i][...] = t_ref[:, at:at + w_refs[i].shape[1]]
            d_refs[i][...], nm_refs[i][...], nv_refs[i][...] = _adamw(
                w_refs[i][...], g_refs[i][...], m_refs[i][...], v_refs[i][...])

    shapes = [jax.ShapeDtypeStruct(w[name].shape, F32) for name in names]
    res = pl.pallas_call(body, name="adam_small", out_shape=shapes * 4)(
        total, *[w[n] for n in names], *[m[n] for n in names], *[v[n] for n in names])
    return [dict(zip(names, res[i * k:(i + 1) * k])) for i in range(4)]


_BIG = ("ffn1_w1", "ffn1_w3", "ffn1_w2", "w_in", "w_proj_attn", "w_proj_hgrn", "w_out", "ffn2_w1", "ffn2_w3", "ffn2_w2")
_SMALL = ("ln1_g", "ln1_b", "ln2_g", "ln2_b", "ln3_g", "ln3_b", "b_in", "attn_sinks", "hgrn_norm_g", "hgrn_lb_logits")
_ORDER = ("ln1_g", "ln1_b", "ffn1_w1", "ffn1_w3", "ffn1_w2", "ln2_g", "ln2_b", "w_in", "b_in", "attn_sinks",
          "hgrn_lb_logits", "hgrn_norm_g", "w_proj_attn", "w_proj_hgrn", "w_out", "ln3_g", "ln3_b",
          "ffn2_w1", "ffn2_w3", "ffn2_w2")


_TRANSPOSED = ("ffn1_w1", "ffn1_w3", "ffn2_w1", "ffn2_w3")


def _local_view(name, arr):
    return arr[0].T if name in _TRANSPOSED else arr[0]


def _ffn_grad(name, hidden, other, comm=None):
    (dw,), comm_out = _grad_matmul(name, hidden, "cols", D_FF // FF_GRAD_PARTS, other, "shared", D_MODEL,
                                   comm=comm, parts=FF_GRAD_PARTS)
    return dw.reshape(N_CHIPS, D_FF // N_CHIPS, D_MODEL), comm_out


def kernel(x, ln1_g, ln1_b, ffn1_w1, ffn1_w3, ffn1_w2, ln2_g, ln2_b, w_in, b_in, attn_sinks, hgrn_lb_logits, hgrn_norm_g, w_proj_attn, w_proj_hgrn, w_out, ln3_g, ln3_b, ffn2_w1, ffn2_w3, ffn2_w2, loss_target, m_ln1_g, m_ln1_b, m_ffn1_w1, m_ffn1_w3, m_ffn1_w2, m_ln2_g, m_ln2_b, m_w_in, m_b_in, m_attn_sinks, m_hgrn_lb_logits, m_hgrn_norm_g, m_w_proj_attn, m_w_proj_hgrn, m_w_out, m_ln3_g, m_ln3_b, m_ffn2_w1, m_ffn2_w3, m_ffn2_w2, v_ln1_g, v_ln1_b, v_ffn1_w1, v_ffn1_w3, v_ffn1_w2, v_ln2_g, v_ln2_b, v_w_in, v_b_in, v_attn_sinks, v_hgrn_lb_logits, v_hgrn_norm_g, v_w_proj_attn, v_w_proj_hgrn, v_w_out, v_ln3_g, v_ln3_b, v_ffn2_w1, v_ffn2_w3, v_ffn2_w2):
    w = dict(ln1_g=ln1_g, ln1_b=ln1_b, ffn1_w1=ffn1_w1, ffn1_w3=ffn1_w3, ffn1_w2=ffn1_w2, ln2_g=ln2_g, ln2_b=ln2_b,
             w_in=w_in, b_in=b_in, attn_sinks=attn_sinks, hgrn_lb_logits=hgrn_lb_logits, hgrn_norm_g=hgrn_norm_g,
             w_proj_attn=w_proj_attn, w_proj_hgrn=w_proj_hgrn, w_out=w_out, ln3_g=ln3_g, ln3_b=ln3_b,
             ffn2_w1=ffn2_w1, ffn2_w3=ffn2_w3, ffn2_w2=ffn2_w2)
    mom = dict(ln1_g=m_ln1_g, ln1_b=m_ln1_b, ffn1_w1=m_ffn1_w1, ffn1_w3=m_ffn1_w3, ffn1_w2=m_ffn1_w2, ln2_g=m_ln2_g,
               ln2_b=m_ln2_b, w_in=m_w_in, b_in=m_b_in, attn_sinks=m_attn_sinks, hgrn_lb_logits=m_hgrn_lb_logits,
               hgrn_norm_g=m_hgrn_norm_g, w_proj_attn=m_w_proj_attn, w_proj_hgrn=m_w_proj_hgrn, w_out=m_w_out,
               ln3_g=m_ln3_g, ln3_b=m_ln3_b, ffn2_w1=m_ffn2_w1, ffn2_w3=m_ffn2_w3, ffn2_w2=m_ffn2_w2)
    var = dict(ln1_g=v_ln1_g, ln1_b=v_ln1_b, ffn1_w1=v_ffn1_w1, ffn1_w3=v_ffn1_w3, ffn1_w2=v_ffn1_w2, ln2_g=v_ln2_g,
               ln2_b=v_ln2_b, w_in=v_w_in, b_in=v_b_in, attn_sinks=v_attn_sinks, hgrn_lb_logits=v_hgrn_lb_logits,
               hgrn_norm_g=v_hgrn_norm_g, w_proj_attn=v_w_proj_attn, w_proj_hgrn=v_w_proj_hgrn, w_out=v_w_out,
               ln3_g=v_ln3_g, ln3_b=v_ln3_b, ffn2_w1=v_ffn2_w1, ffn2_w3=v_ffn2_w3, ffn2_w2=v_ffn2_w2)

    n_tok = x.shape[1]
    x0 = x.reshape(n_tok, D_MODEL)
    target = loss_target.reshape(n_tok, D_MODEL)

    shard = {k: _bf(_local_view(k, w[k])) for k in _BIG}
    gather = lambda keys: _GatherWeights([shard[k] for k in keys])
    slots = {}
    exchange = lambda keys: _ExchangeGrads([big[k] for k in keys])
    ffn1_keys = ("ffn1_w1", "ffn1_w3", "ffn1_w2")
    mixer_keys = ("w_in", "w_proj_attn", "w_proj_hgrn", "w_out")
    ffn2_keys = ("ffn2_w1", "ffn2_w3", "ffn2_w2")
    whole = lambda ts: [t.reshape(D_FF, D_MODEL) for t in ts]
    f1 = whole(_run_comm("gather_ffn1", gather(ffn1_keys)))

    tabs = _rope_tables(n_tok)
    lb, lb_of = _lb_fwd(hgrn_lb_logits)
    (z1, x1, x0b, h1_1, h3_1, x1b), (w_in_g, w_pa, w_ph, w_o) = _ffn_fwd(
        "ffn1_fwd", x0, *f1, ln1_g, ln1_b, comm=gather(mixer_keys))
    w_pa, w_ph, w_o = (t.reshape(D_MODEL, D_MODEL) for t in (w_pa, w_ph, w_o))
    w_in_g = _reorder_w_in("w_in_cols", w_in_g, True)
    (proj,), f2 = _in_proj(x1b, w_in_g, _to_kernel_cols(b_in), comm=gather(ffn2_keys))
    f2 = whole(f2)
    y_attn = _attn_fwd(proj, tabs, attn_sinks)
    y_hgrn, states = _hgrn_fwd(proj, lb, hgrn_norm_g)
    z2, x2, proj_a, proj_h = _mix_fwd(y_attn, y_hgrn, proj, x1, w_pa, w_ph, w_o, ln2_g, ln2_b)
    (z3, dy, x2b, h1_2, h3_2, loss_part), _ = _ffn_fwd("ffn2_fwd", x2, *f2, ln3_g, ln3_b, target=target)

    big = {}
    small = {}
    (dx2, a2, dh1_2, dh3_2, df2, small["ln3_g"], small["ln3_b"]), _ = _ffn_bwd(
        "ffn2_bwd", h1_2, h3_2, z3, dy, *f2, ln3_g, ln3_b)
    big["ffn2_w1"], _ = _ffn_grad("ffn2_dw1", dh1_2, x2b)
    big["ffn2_w3"], _ = _ffn_grad("ffn2_dw3", dh3_2, x2b)
    big["ffn2_w2"], _ = _ffn_grad("ffn2_dw2", a2, df2)
    (dz2, dz2b, merged, dya, dyh, dy_attn, dy_hgrn, dproj, small["ln2_g"], small["ln2_b"]) = _mix_bwd(
        dx2, z2, proj_a, proj_h, proj, w_pa, w_ph, w_o, ln2_g, ln2_b)
    for key, name, lhs, rhs in (("w_out", "dw_out", merged, dz2b), ("w_proj_attn", "dw_proj_attn", y_attn, dya),
                                ("w_proj_hgrn", "dw_proj_hgrn", y_hgrn, dyh)):
        (dw,), _ = _grad_matmul(name, lhs, "cols", D_MODEL, rhs, "shared", D_MODEL, parts=1)
        big[key] = dw.reshape(N_CHIPS, PROJ_SHARD, D_MODEL)
    (dproj, dlb, small["hgrn_norm_g"]), early = _hgrn_bwd(
        proj, lb, hgrn_norm_g, states, dy_hgrn, dproj, comm=exchange(ffn2_keys))
    slots.update(zip(ffn2_keys, early))
    (dproj, dsinks), early = _attn_bwd(proj, dy_attn, tabs, attn_sinks, dproj, comm=exchange(mixer_keys[1:]))
    slots.update(zip(mixer_keys[1:], early))
    (dw_in, db_in), _ = _grad_matmul("dw_in", x1b, "shared", D_MODEL, dproj, "cols", IN_SHARD, colsum=True)
    big["w_in"] = _reorder_w_in("dw_in_cols", dw_in, False)
    small["b_in"] = _from_kernel_cols(db_in)
    (dx1,), (w_in_slots,) = _in_proj_dx(
        dproj, w_in_g, dz2, comm=_ExchangeGrads([big["w_in"]], rows=(0, W_IN_ROWS_FIRST)))
    (grad_x, a1, dh1_1, dh3_1, df1, small["ln1_g"], small["ln1_b"]), _ = _ffn_bwd(
        "ffn1_bwd", h1_1, h3_1, z1, dx1, *f1, ln1_g, ln1_b)
    big["ffn1_w1"], (slots["w_in"],) = _ffn_grad("ffn1_dw1", dh1_1, x0b, comm=_ExchangeGrads(
        [big["w_in"]], rows=(W_IN_ROWS_FIRST, D_MODEL - W_IN_ROWS_FIRST), into=[w_in_slots]))
    big["ffn1_w3"], (slots["ffn1_w1"],) = _ffn_grad("ffn1_dw3", dh3_1, x0b, comm=exchange(("ffn1_w1",)))
    big["ffn1_w2"], (slots["ffn1_w3"],) = _ffn_grad("ffn1_dw2", a1, df1, comm=exchange(("ffn1_w3",)))

    last = "ffn1_w2"
    groups = [[k for k in ffn1_keys + ffn2_keys if k != last], list(mixer_keys[1:]), ["w_in"]]
    partial = {}
    for keys in groups:
        partial.update(zip(keys, _sum_slots("sum_" + keys[0], [slots[k] for k in keys])))
    swapped_keys = [k for keys in groups for k in keys]
    moved = _run_comm("swap_and_exchange_last",
                      _Together(_SwapWithSibling([partial[k] for k in swapped_keys]), exchange((last,))))
    from_sibling = dict(zip(swapped_keys, moved))
    (partial[last],) = _sum_slots("sum_" + last, [moved[-1]])
    (from_sibling[last],) = _run_comm("swap_last", _SwapWithSibling([partial[last]]))
    groups[0].append(last)

    outs = {"grad": {}, "delta": {}, "m": {}, "v": {}}
    for keys in groups:
        res = _adam_big("adam_" + keys[0], [(partial[k], from_sibling[k], _local_view(k, w[k]), _local_view(k, mom[k]),
                                             _local_view(k, var[k])) for k in keys])
        for k, four in zip(keys, res):
            for kind, r in zip(("grad", "delta", "m", "v"), four):
                outs[kind][k] = (r.T if k in _TRANSPOSED else r).reshape(w[k].shape)

    total = _sum_small(jnp.concatenate(
        [small[k] for k in ("ln1_g", "ln1_b", "ln2_g", "ln2_b", "ln3_g", "ln3_b", "b_in")]
        + [dsinks, small["hgrn_norm_g"], dlb, loss_part], axis=1))
    for kind, r in zip(("grad", "delta", "m", "v"), _adam_small(total, w, mom, var, lb_of)):
        outs[kind].update(r)
    loss = total[0, SM_LOSS]

    return (loss, grad_x.reshape(x.shape), *[outs["grad"][k] for k in _ORDER], *[outs["delta"][k] for k in _ORDER],
            *[outs["m"][k] for k in _ORDER], *[outs["v"][k] for k in _ORDER])
```

```python
import functools

import jax
import jax.numpy as jnp
from jax import lax
from jax.experimental import pallas as pl
from jax.experimental.pallas import tpu as pltpu

F32 = jnp.float32
BF16 = jnp.bfloat16

D_MODEL = 1024
N_Q_HEADS = 16
N_KV_HEADS = 4
HEAD_DIM = 64
ATTN_BLOCK = 128
ROPE_THETA = 500000.0
ROPE_DIM = HEAD_DIM // 4
HGRN_HEADS = 8
HGRN_DK = 128
HGRN_CHUNK = 64
D_FF = 2816
D_IN = 7680
DEEPNORM_ALPHA = 2 ** 0.25
LN_EPS = 1e-5
RMS_EPS = 1e-6
NEG_INF = -1e30

ADAM_LR = 0.001
ADAM_B1 = 0.9
ADAM_B2 = 0.999
ADAM_EPS = 1e-08
ADAM_WD = 0.01
ADAM_STEP = 10

N_CHIPS = 4
N_DEV = 8
LANES = 128
FF_GRAD_PARTS = 2
FFN_TILE = 256
IN_SHARD = D_IN // N_CHIPS
PROJ_SHARD = D_MODEL // N_CHIPS
ROW_TILE = 512
GRAD_ROWS = 2048
MIX_TILE = 256
UPDATE_ROWS = 128
HGRN_CHUNKS_PER_STEP = 32
W_IN_ROWS_FIRST = 640
VMEM_LIMIT = 56 * 1024 * 1024

GATES_WIDTH = 2 * D_MODEL
HGRN_HEAD_WIDTH = 4 * HGRN_DK
ATTN_WIDTH = D_MODEL + 2 * N_KV_HEADS * HEAD_DIM
COL_HGRN = GATES_WIDTH // HGRN_HEAD_WIDTH
COL_ATTN = (GATES_WIDTH + HGRN_HEADS * HGRN_HEAD_WIDTH) // ATTN_WIDTH
COL_Q = (GATES_WIDTH + HGRN_HEADS * HGRN_HEAD_WIDTH) // D_MODEL
COL_K = (GATES_WIDTH + HGRN_HEADS * HGRN_HEAD_WIDTH + D_MODEL) // (N_KV_HEADS * HEAD_DIM)
COL_V = COL_K + 1


def _to_kernel_cols(a):
    lead = a.shape[:-1]
    qkv, hg, gates = a[..., :ATTN_WIDTH], a[..., ATTN_WIDTH:D_IN - GATES_WIDTH], a[..., D_IN - GATES_WIDTH:]
    hg = jnp.swapaxes(hg.reshape(*lead, 4, HGRN_HEADS, HGRN_DK), -3, -2).reshape(*lead, -1)
    return jnp.concatenate([gates, hg, qkv], axis=-1)


def _from_kernel_cols(a):
    lead = a.shape[:-1]
    gates, hg, qkv = a[..., :GATES_WIDTH], a[..., GATES_WIDTH:D_IN - ATTN_WIDTH], a[..., D_IN - ATTN_WIDTH:]
    hg = jnp.swapaxes(hg.reshape(*lead, HGRN_HEADS, 4, HGRN_DK), -3, -2).reshape(*lead, -1)
    return jnp.concatenate([qkv, hg, gates], axis=-1)

SM_LN = 0
SM_BIN = 6 * D_MODEL
SM_SINK = SM_BIN + D_IN
SM_NG = SM_SINK + LANES
SM_LB = SM_NG + LANES
SM_LOSS = SM_LB + D_MODEL
SM_LEN = SM_LOSS + LANES

MESH = pl.DeviceIdType.MESH


def _mm(a, b):
    return lax.dot_general(a, b, (((1,), (0,)), ((), ())), preferred_element_type=F32)


def _mm_nt(a, b):
    return lax.dot_general(a, b, (((1,), (1,)), ((), ())), preferred_element_type=F32)


def _mm_tn(a, b):
    return lax.dot_general(a, b, (((0,), (0,)), ((), ())), preferred_element_type=F32)


def _bf(v):
    return v.astype(BF16)


def _sig(v):
    return jax.nn.sigmoid(v)


def _ln(z, g, b):
    mu = jnp.mean(z, axis=-1, keepdims=True)
    zc = z - mu
    var = jnp.mean(zc * zc, axis=-1, keepdims=True)
    return zc * lax.rsqrt(var + LN_EPS) * g + b


def _swiglu_act(h1, h3):
    return (h1 * _sig(h1)) * h3


def _params(sem=None):
    return pltpu.CompilerParams(dimension_semantics=sem, vmem_limit_bytes=VMEM_LIMIT)


def _full(shape):
    nd = len(shape)
    return pl.BlockSpec(shape, lambda *_: (0,) * nd)


def _update_rows(rows):
    return max(t for t in range(8, UPDATE_ROWS + 1, 8) if rows % t == 0)


def _resident(shape):
    nd = len(shape)
    return pl.BlockSpec(shape, lambda *_: (0,) * nd, pipeline_mode=pl.Buffered(1))


def _ffn_fwd(name, x, w1t, w3t, w2, g, b, target=None, comm=None):
    n = x.shape[0]
    tm = min(ROW_TILE, n)
    final = target is not None

    def body(*refs):
        if final:
            x_ref, w1_ref, w3_ref, w2_ref, g_ref, b_ref, t_ref, z_ref, o_ref, xb_ref, h1_ref, h3_ref, loss_ref = refs
        else:
            x_ref, w1_ref, w3_ref, w2_ref, g_ref, b_ref, z_ref, o_ref, xb_ref, h1_ref, h3_ref, ob_ref = refs
        xb = _bf(x_ref[...])
        xb_ref[...] = xb
        h1 = _mm_nt(xb, w1_ref[...])
        h3 = _mm_nt(xb, w3_ref[...])
        h1_ref[...] = _bf(h1)
        h3_ref[...] = _bf(h3)
        z = DEEPNORM_ALPHA * x_ref[...] + 0.5 * _mm(_bf(_swiglu_act(h1, h3)), w2_ref[...])
        z_ref[...] = z
        y = _ln(z, g_ref[...], b_ref[...])
        if final:
            e = y - t_ref[...]

            @pl.when(pl.program_id(0) == 0)
            def _():
                loss_ref[...] = jnp.zeros_like(loss_ref)

            loss_ref[...] += jnp.sum(e * e) * (0.5 / D_MODEL)
            o_ref[...] = e * (1.0 / D_MODEL)
        else:
            o_ref[...] = y
            ob_ref[...] = _bf(y)

    row = pl.BlockSpec((tm, D_MODEL), lambda i: (i, 0))
    hid = pl.BlockSpec((tm, D_FF), lambda i: (i, 0))
    wres = _resident((D_FF, D_MODEL))
    vec = _full((1, D_MODEL))
    in_specs = [row, wres, wres, wres, vec, vec]
    args = [x, w1t, w3t, w2, g, b]
    hid_shape = jax.ShapeDtypeStruct((n, D_FF), BF16)
    out_specs = [row, row, row, hid, hid]
    out_shape = ([jax.ShapeDtypeStruct((n, D_MODEL), F32)] * 2 + [jax.ShapeDtypeStruct((n, D_MODEL), BF16)]
                 + [hid_shape] * 2)
    if final:
        in_specs.append(row)
        args.append(target)
        out_specs.append(_full((1, LANES)))
        out_shape.append(jax.ShapeDtypeStruct((1, LANES), F32))
    else:
        out_specs.append(row)
        out_shape.append(jax.ShapeDtypeStruct((n, D_MODEL), BF16))
    return _hosted(
        body, comm, name=name, grid=(n // tm,), in_specs=in_specs, out_specs=out_specs, out_shape=out_shape,
        scratch_shapes=[], compiler_params=_params(("arbitrary",)), args=args)


def _ffn_bwd(name, h1s, h3s, z, dout, w1t, w3t, w2, g, b, comm=None):
    n = z.shape[0]
    tm = min(FFN_TILE, n)

    def body(h1_ref, h3_ref, z_ref, do_ref, w1_ref, w3_ref, w2_ref, g_ref, b_ref,
             dx_ref, a_ref, dh1_ref, dh3_ref, df_ref, dg_ref, db_ref):
        _, vjp = jax.vjp(_ln, z_ref[...], g_ref[...], b_ref[...])
        dz, dg, db = vjp(do_ref[...])

        @pl.when(pl.program_id(0) == 0)
        def _():
            dg_ref[...] = jnp.zeros_like(dg_ref)
            db_ref[...] = jnp.zeros_like(db_ref)

        dg_ref[...] += dg
        db_ref[...] += db
        df = _bf(0.5 * dz)
        df_ref[...] = df
        a, act_vjp = jax.vjp(_swiglu_act, h1_ref[...].astype(F32), h3_ref[...].astype(F32))
        dh1, dh3 = act_vjp(_mm_nt(df, w2_ref[...]))
        dh1 = _bf(dh1)
        dh3 = _bf(dh3)
        a_ref[...] = _bf(a)
        dh1_ref[...] = dh1
        dh3_ref[...] = dh3
        dx_ref[...] = DEEPNORM_ALPHA * dz + _mm(dh1, w1_ref[...]) + _mm(dh3, w3_ref[...])

    row = pl.BlockSpec((tm, D_MODEL), lambda i: (i, 0))
    hid = pl.BlockSpec((tm, D_FF), lambda i: (i, 0))
    wres = _resident((D_FF, D_MODEL))
    vec = _full((1, D_MODEL))
    hid_shape = jax.ShapeDtypeStruct((n, D_FF), BF16)
    return _hosted(
        body, comm, name=name, grid=(n // tm,),
        in_specs=[hid, hid, row, row, wres, wres, wres, vec, vec],
        out_specs=[row, hid, hid, hid, row, vec, vec],
        out_shape=[jax.ShapeDtypeStruct((n, D_MODEL), F32), hid_shape, hid_shape, hid_shape,
                   jax.ShapeDtypeStruct((n, D_MODEL), BF16),
                   jax.ShapeDtypeStruct((1, D_MODEL), F32), jax.ShapeDtypeStruct((1, D_MODEL), F32)],
        scratch_shapes=[], compiler_params=_params(("arbitrary",)),
        args=[h1s, h3s, z, dout, w1t, w3t, w2, g, b])


def _operand_spec(arr, mode, tn, width, parts):
    if mode == "shared":
        return pl.BlockSpec((tn, width), lambda s, k: (k, 0))
    assert mode == "cols" and arr.shape[1] == parts * width
    return pl.BlockSpec((tn, width), lambda s, k: (k, s))


def _grad_matmul(name, a, a_mode, ka, b, b_mode, kb, colsum=False, comm=None, parts=N_CHIPS):
    n = a.shape[-2]
    tn = min(GRAD_ROWS, n)
    nk = n // tn

    def body(*refs):
        if colsum:
            a_ref, b_ref, o_ref, cs_ref, acc = refs
        else:
            a_ref, b_ref, o_ref, acc = refs
        k = pl.program_id(1)
        av = a_ref[...]
        bv = b_ref[...]

        @pl.when(k == 0)
        def _():
            acc[...] = jnp.zeros_like(acc)
            if colsum:
                cs_ref[...] = jnp.zeros_like(cs_ref)

        acc[...] += _mm_tn(av, bv)
        if colsum:
            cs_ref[...] += jnp.sum(bv.astype(F32), axis=0, keepdims=True)

        @pl.when(k == nk - 1)
        def _():
            o_ref[0] = _bf(acc[...])

    out_specs = [pl.BlockSpec((1, ka, kb), lambda s, k: (s, 0, 0))]
    out_shape = [jax.ShapeDtypeStruct((parts, ka, kb), BF16)]
    if colsum:
        out_specs.append(pl.BlockSpec((1, kb), lambda s, k: (0, s)))
        out_shape.append(jax.ShapeDtypeStruct((1, parts * kb), F32))
    return _hosted(
        body, comm, name=name, grid=(parts, nk),
        in_specs=[_operand_spec(a, a_mode, tn, ka, parts), _operand_spec(b, b_mode, tn, kb, parts)],
        out_specs=out_specs, out_shape=out_shape,
        scratch_shapes=[pltpu.VMEM((ka, kb), F32)],
        compiler_params=_params(("arbitrary", "arbitrary")), args=[a, b])


def _kernel_block_of(ref_block):
    attn_blocks, gate_blocks = ATTN_WIDTH // LANES, GATES_WIDTH // LANES
    hgrn_blocks = HGRN_HEADS * HGRN_HEAD_WIDTH // LANES
    if ref_block < attn_blocks:
        return gate_blocks + hgrn_blocks + ref_block
    if ref_block < attn_blocks + hgrn_blocks:
        kind, head = divmod(ref_block - attn_blocks, HGRN_HEADS)
        return gate_blocks + head * (HGRN_HEAD_WIDTH // LANES) + kind
    return ref_block - attn_blocks - hgrn_blocks


def _reorder_w_in(name, w4, to_kernel_order):
    per = IN_SHARD // LANES
    tr = MIX_TILE

    def body(i_ref, o_ref):
        for g in range(D_IN // LANES):
            s, b = divmod(g, per)
            k = _kernel_block_of(g)
            if to_kernel_order:
                o_ref[:, k * LANES:(k + 1) * LANES] = i_ref[s, :, b * LANES:(b + 1) * LANES]
            else:
                ks, kb = divmod(k, per)
                o_ref[s, :, b * LANES:(b + 1) * LANES] = i_ref[ks, :, kb * LANES:(kb + 1) * LANES]

    in_spec = pl.BlockSpec((N_CHIPS, tr, IN_SHARD), lambda i: (0, i, 0))
    if to_kernel_order:
        out_spec, out_shape = pl.BlockSpec((tr, D_IN), lambda i: (i, 0)), (D_MODEL, D_IN)
    else:
        out_spec, out_shape = in_spec, (N_CHIPS, D_MODEL, IN_SHARD)
    return pl.pallas_call(
        body, name=name, grid=(D_MODEL // tr,), in_specs=[in_spec], out_specs=out_spec,
        out_shape=jax.ShapeDtypeStruct(out_shape, w4.dtype), compiler_params=_params(("arbitrary",)),
    )(w4)


def _in_proj(x1, w_in_g, b_in, comm=None):
    n = x1.shape[0]
    tm = min(ROW_TILE, n)

    def body(x_ref, w_ref, b_ref, o_ref):
        xv = x_ref[...]
        for j in range(N_CHIPS):
            cols = slice(j * IN_SHARD, (j + 1) * IN_SHARD)
            o_ref[:, cols] = _mm(xv, w_ref[:, cols]) + b_ref[:, cols]

    return _hosted(
        body, comm, name="in_proj", grid=(n // tm,),
        in_specs=[pl.BlockSpec((tm, D_MODEL), lambda i: (i, 0)),
                  _resident((D_MODEL, D_IN)), _full((1, D_IN))],
        out_specs=[pl.BlockSpec((tm, D_IN), lambda i: (i, 0))],
        out_shape=[jax.ShapeDtypeStruct((n, D_IN), F32)],
        scratch_shapes=[],
        compiler_params=_params(("arbitrary",)), args=[x1, w_in_g, b_in])


def _in_proj_dx(dproj, w_in_g, dz2, comm=None):
    n = dproj.shape[0]
    tm = min(ROW_TILE, n)

    def body(dp_ref, w_ref, dz_ref, o_ref):
        dx = DEEPNORM_ALPHA * dz_ref[...]
        for j in range(N_CHIPS):
            cols = slice(j * IN_SHARD, (j + 1) * IN_SHARD)
            dx = dx + _mm_nt(dp_ref[:, cols], w_ref[:, cols])
        o_ref[...] = dx

    return _hosted(
        body, comm, name="in_proj_dx", grid=(n // tm,),
        in_specs=[pl.BlockSpec((tm, D_IN), lambda i: (i, 0)),
                  _resident((D_MODEL, D_IN)),
                  pl.BlockSpec((tm, D_MODEL), lambda i: (i, 0))],
        out_specs=[pl.BlockSpec((tm, D_MODEL), lambda i: (i, 0))],
        out_shape=[jax.ShapeDtypeStruct((n, D_MODEL), F32)],
        scratch_shapes=[],
        compiler_params=_params(("arbitrary",)), args=[dproj, w_in_g, dz2])


def _rope_tables(seq_len):
    pos = jnp.arange(seq_len, dtype=F32)
    inv_freq = ROPE_THETA ** (-jnp.arange(0, ROPE_DIM, 2, dtype=F32) / ROPE_DIM)
    ang = pos[:, None] * inv_freq[None, :]
    cos, sin = jnp.cos(ang), jnp.sin(ang)
    half = ROPE_DIM // 2
    rest = HEAD_DIM - ROPE_DIM
    ones = jnp.ones((seq_len, rest), F32)
    zeros = jnp.zeros((seq_len, rest), F32)
    zh = jnp.zeros((seq_len, half), F32)
    c = jnp.concatenate([cos, cos, ones], axis=1)
    sa = jnp.concatenate([-sin, zh, zeros], axis=1)
    sb = jnp.concatenate([zh, sin, zeros], axis=1)
    reps = LANES // HEAD_DIM
    return tuple(jnp.tile(t, (1, reps)) for t in (c, sa, sb))


def _rope(t, c, sa, sb):
    w = t.shape[1]
    reps = w // LANES
    half = ROPE_DIM // 2
    return (t * jnp.tile(c, (1, reps)) + pltpu.roll(t, w - half, 1) * jnp.tile(sa, (1, reps))
            + pltpu.roll(t, half, 1) * jnp.tile(sb, (1, reps)))


def _rope_transposed(g, c, sa, sb):
    w = g.shape[1]
    reps = w // LANES
    half = ROPE_DIM // 2
    return (g * jnp.tile(c, (1, reps)) + pltpu.roll(g * jnp.tile(sa, (1, reps)), half, 1)
            + pltpu.roll(g * jnp.tile(sb, (1, reps)), w - half, 1))


GROUP = N_Q_HEADS // N_KV_HEADS


def _both_halves(t_pair, which):
    lo = lax.broadcasted_iota(jnp.int32, t_pair.shape, 1) < HEAD_DIM
    swapped = pltpu.roll(t_pair, HEAD_DIM, 1)
    return _bf(jnp.where(lo, t_pair, swapped) if which == 0 else jnp.where(lo, swapped, t_pair))


def _stack_heads(ref_or_val, kh):
    lo = lax.broadcasted_iota(jnp.int32, (ATTN_BLOCK, LANES), 1) < HEAD_DIM
    rows = []
    for gp in range(GROUP // 2):
        pair = kh * (GROUP // 2) + gp
        t = ref_or_val[:, pair * LANES:(pair + 1) * LANES]
        rows += [jnp.where(lo, t, jnp.zeros_like(t)), jnp.where(lo, jnp.zeros_like(t), t)]
    return jnp.concatenate(rows, axis=0)


def _unstack_pairs(stacked):
    lo = lax.broadcasted_iota(jnp.int32, (ATTN_BLOCK, LANES), 1) < HEAD_DIM
    b = ATTN_BLOCK
    return [jnp.where(lo, stacked[2 * gp * b:(2 * gp + 1) * b], stacked[(2 * gp + 1) * b:(2 * gp + 2) * b])
            for gp in range(GROUP // 2)]


def _attn_mask_t(n):
    cols = GROUP * ATTN_BLOCK
    kj = lax.broadcasted_iota(jnp.int32, (2 * ATTN_BLOCK, cols), 0)
    qi = lax.broadcasted_iota(jnp.int32, (2 * ATTN_BLOCK, cols), 1) % ATTN_BLOCK
    dist = qi + ATTN_BLOCK - kj
    return (dist >= 0) & (dist < ATTN_BLOCK) & (n * ATTN_BLOCK + kj - ATTN_BLOCK >= 0)


def _sink_row(sink_ref, kh):
    col = lax.broadcasted_iota(jnp.int32, (1, GROUP * ATTN_BLOCK), 1)
    row = jnp.full((1, GROUP * ATTN_BLOCK), sink_ref[0, kh * GROUP + GROUP - 1], F32)
    for i in reversed(range(GROUP - 1)):
        row = jnp.where(col < (i + 1) * ATTN_BLOCK, sink_ref[0, kh * GROUP + i], row)
    return row


def _attn_probs_t(q_masked, k_sel, mask_t, sink):
    s = _mm_nt(k_sel, q_masked) * (HEAD_DIM ** -0.5)
    s = jnp.where(mask_t, s, NEG_INF)
    m = jnp.maximum(jnp.max(s, axis=0, keepdims=True), sink)
    p = jnp.exp(s - m)
    e_sink = jnp.exp(sink - m)
    denom = jnp.sum(p, axis=0, keepdims=True) + e_sink
    return p / denom, e_sink / denom


def _attn_fwd(proj, tabs, sinks):
    n_tok = proj.shape[0]
    nb = n_tok // ATTN_BLOCK

    def body(q_ref, k_ref, v_ref, c_ref, sa_ref, sb_ref, sink_ref, y_ref, kprev, vprev):
        n = pl.program_id(0)

        @pl.when(n == 0)
        def _():
            kprev[...] = jnp.zeros_like(kprev)
            vprev[...] = jnp.zeros_like(vprev)

        c, sa, sb = c_ref[...], sa_ref[...], sb_ref[...]
        qr = _bf(_rope(q_ref[...], c, sa, sb))
        kr = _rope(k_ref[...], c, sa, sb)
        vc = v_ref[...]
        kk = jnp.concatenate([kprev[...], kr], axis=0)
        vv = jnp.concatenate([vprev[...], vc], axis=0)
        kprev[...] = kr
        vprev[...] = vc
        mask = _attn_mask_t(n)
        for kh in range(N_KV_HEADS):
            r, which = divmod(kh, 2)
            kb = _both_halves(kk[:, r * LANES:(r + 1) * LANES], which)
            vb = _both_halves(vv[:, r * LANES:(r + 1) * LANES], which)
            probs, _ = _attn_probs_t(_stack_heads(qr, kh), kb, mask, _sink_row(sink_ref, kh))
            for gp, out in enumerate(_unstack_pairs(_mm_tn(_bf(probs), vb))):
                pair = kh * (GROUP // 2) + gp
                y_ref[:, pair * LANES:(pair + 1) * LANES] = _bf(out)

    blk = lambda width, col: pl.BlockSpec((ATTN_BLOCK, width), lambda n: (n, col))
    tab = pl.BlockSpec((ATTN_BLOCK, LANES), lambda n: (n, 0))
    kvw = N_KV_HEADS * HEAD_DIM
    return pl.pallas_call(
        body, name="attn_fwd", grid=(nb,),
        in_specs=[blk(D_MODEL, COL_Q), blk(kvw, COL_K), blk(kvw, COL_V), tab, tab, tab,
                  pl.BlockSpec(memory_space=pltpu.SMEM)],
        out_specs=pl.BlockSpec((ATTN_BLOCK, D_MODEL), lambda n: (n, 0)),
        out_shape=jax.ShapeDtypeStruct((n_tok, D_MODEL), BF16),
        scratch_shapes=[pltpu.VMEM((ATTN_BLOCK, kvw), F32), pltpu.VMEM((ATTN_BLOCK, kvw), F32)],
        compiler_params=_params(("arbitrary",)),
    )(proj, proj, proj, *tabs, sinks)


def _attn_bwd(proj, dy, tabs, sinks, dproj, comm=None):
    n_tok = proj.shape[0]
    nb = n_tok // ATTN_BLOCK
    kvw = N_KV_HEADS * HEAD_DIM

    def body(q_ref, k_ref, v_ref, do_ref, c_ref, sa_ref, sb_ref, cp_ref, sap_ref, sbp_ref, sink_ref, _,
             dqkv_ref, dsink_ref, kprev, vprev, dkc, dvc, dqc):
        n = pl.program_id(0)

        @pl.when(n == 0)
        def _():
            for ref in (kprev, vprev, dkc, dvc, dqc, dsink_ref):
                ref[...] = jnp.zeros_like(ref)

        prev_tabs = (cp_ref[...], sap_ref[...], sbp_ref[...])

        @pl.when(n < nb)
        def _():
            c, sa, sb = c_ref[...], sa_ref[...], sb_ref[...]
            qr = _bf(_rope(q_ref[...], c, sa, sb))
            kr = _rope(k_ref[...], c, sa, sb)
            vc = v_ref[...]
            kk = jnp.concatenate([kprev[...], kr], axis=0)
            vv = jnp.concatenate([vprev[...], vc], axis=0)
            kprev[...] = kr
            vprev[...] = vc
            mask = _attn_mask_t(n)
            lane = lax.broadcasted_iota(jnp.int32, (1, LANES), 1)
            lo2 = lax.broadcasted_iota(jnp.int32, (2 * ATTN_BLOCK, LANES), 1) < HEAD_DIM
            dsink = jnp.zeros((1, LANES), F32)
            dq_pairs = []
            dk_full = []
            dv_full = []
            for kh in range(N_KV_HEADS):
                r, which = divmod(kh, 2)
                kb = _both_halves(kk[:, r * LANES:(r + 1) * LANES], which)
                vb = _both_halves(vv[:, r * LANES:(r + 1) * LANES], which)
                qs = _stack_heads(qr, kh)
                dos = _stack_heads(do_ref, kh)
                probs, p_sink = _attn_probs_t(qs, kb, mask, _sink_row(sink_ref, kh))
                dp = _mm_nt(vb, dos)
                delta = jnp.sum(probs * dp, axis=0, keepdims=True)
                ds = _bf(probs * (dp - delta) * (HEAD_DIM ** -0.5))
                sink_terms = p_sink * delta
                for i in range(GROUP):
                    head_sum = jnp.sum(sink_terms[:, i * ATTN_BLOCK:(i + 1) * ATTN_BLOCK])
                    dsink = dsink + jnp.where(lane == kh * GROUP + i, -head_sum, 0.0)
                dq_pairs += _unstack_pairs(_mm_tn(ds, kb))
                dk_acc = _mm(ds, qs)
                dv_acc = _mm(_bf(probs), dos)
                dk_full.append(dk_acc + pltpu.roll(dk_acc, HEAD_DIM, 1))
                dv_full.append(dv_acc + pltpu.roll(dv_acc, HEAD_DIM, 1))
            dk_pairs = [jnp.where(lo2, dk_full[2 * r], dk_full[2 * r + 1]) for r in range(N_KV_HEADS // 2)]
            dv_pairs = [jnp.where(lo2, dv_full[2 * r], dv_full[2 * r + 1]) for r in range(N_KV_HEADS // 2)]
            dsink_ref[...] += dsink
            dqkv_ref[:, :D_MODEL] = _bf(dqc[...])
            dqc[...] = _rope_transposed(jnp.concatenate(dq_pairs, axis=1), c, sa, sb)
            dk_all = jnp.concatenate(dk_pairs, axis=1)
            dv_all = jnp.concatenate(dv_pairs, axis=1)
            dqkv_ref[:, D_MODEL:D_MODEL + kvw] = _bf(_rope_transposed(dkc[...] + dk_all[:ATTN_BLOCK], *prev_tabs))
            dqkv_ref[:, D_MODEL + kvw:] = _bf(dvc[...] + dv_all[:ATTN_BLOCK])
            dkc[...] = dk_all[ATTN_BLOCK:]
            dvc[...] = dv_all[ATTN_BLOCK:]

        @pl.when(n == nb)
        def _():
            dqkv_ref[:, :D_MODEL] = _bf(dqc[...])
            dqkv_ref[:, D_MODEL:D_MODEL + kvw] = _bf(_rope_transposed(dkc[...], *prev_tabs))
            dqkv_ref[:, D_MODEL + kvw:] = _bf(dvc[...])

    cur = lambda n: jnp.minimum(n, nb - 1)
    prev = lambda n: jnp.maximum(n - 1, 0)
    blk = lambda width, col: pl.BlockSpec((ATTN_BLOCK, width), lambda n: (cur(n), col))
    tab = pl.BlockSpec((ATTN_BLOCK, LANES), lambda n: (cur(n), 0))
    tabp = pl.BlockSpec((ATTN_BLOCK, LANES), lambda n: (prev(n), 0))
    return _hosted(
        body, comm, name="attn_bwd", grid=(nb + 1,),
        in_specs=[blk(D_MODEL, COL_Q), blk(kvw, COL_K), blk(kvw, COL_V), blk(D_MODEL, 0), tab, tab, tab, tabp, tabp, tabp,
                  pl.BlockSpec(memory_space=pltpu.SMEM), pl.BlockSpec(memory_space=pl.ANY)],
        out_specs=[pl.BlockSpec((ATTN_BLOCK, ATTN_WIDTH), lambda n: (prev(n), COL_ATTN)),
                   pl.BlockSpec((1, LANES), lambda n: (0, 0))],
        out_shape=[jax.ShapeDtypeStruct(dproj.shape, dproj.dtype), jax.ShapeDtypeStruct((1, LANES), F32)],
        scratch_shapes=[pltpu.VMEM((ATTN_BLOCK, kvw), F32)] * 4 + [pltpu.VMEM((ATTN_BLOCK, D_MODEL), F32)],
        compiler_params=_params(("arbitrary",)), args=[proj, proj, proj, dy, *tabs, *tabs, sinks, dproj],
        aliases={11: 0})


def _bmm(a, b):
    return lax.dot_general(a, b, (((2,), (1,)), ((0,), (0,))), preferred_element_type=F32)


def _bmm_nt(a, b):
    return lax.dot_general(a, b, (((2,), (2,)), ((0,), (0,))), preferred_element_type=F32)


def _bmm_tn(a, b):
    return lax.dot_general(a, b, (((1,), (1,)), ((0,), (0,))), preferred_element_type=F32)


def _tril(cb, upper=False):
    shape = (cb, HGRN_CHUNK, HGRN_CHUNK)
    r, c = lax.broadcasted_iota(jnp.int32, shape, 1), lax.broadcasted_iota(jnp.int32, shape, 2)
    return (r <= c) if upper else (r >= c)


def _tri_matmul(x, upper):
    return lax.dot_general(_tril(x.shape[0], upper).astype(F32), x, (((2,), (1,)), ((0,), (0,))),
                           precision=lax.Precision.HIGHEST, preferred_element_type=F32)


@jax.custom_vjp
def _chunk_cumsum(x):
    return _tri_matmul(x, False)


_chunk_cumsum.defvjp(lambda x: (_tri_matmul(x, False), None), lambda _, g: (_tri_matmul(g, True),))


def _hg_elem(fl, qh, lb):
    f = lb + (1.0 - lb) * _sig(fl)
    k = 1.0 - f
    gc = _chunk_cumsum(jnp.log(f))
    last = lax.broadcasted_iota(jnp.int32, gc.shape, 1) == HGRN_CHUNK - 1
    g_last = jnp.sum(jnp.where(last, gc, 0.0), axis=1, keepdims=True)
    q = qh * _sig(qh)
    return q * jnp.exp(gc), k * jnp.exp(-gc), k * jnp.exp(g_last - gc), jnp.exp(g_last)


def _hg_out(q_dec, k_inv, v, st):
    sc = jnp.where(_tril(q_dec.shape[0]), _bmm_nt(_bf(q_dec), _bf(k_inv)), 0.0)
    return _bmm(_bf(sc), _bf(v)) + _bmm_nt(_bf(q_dec), _bf(st)), sc


def _hg_post(o, og, ng):
    on = o * lax.rsqrt(jnp.mean(o * o, axis=-1, keepdims=True) + RMS_EPS) * ng
    return on * (og * _sig(og))


def _hgrn_specs(n_tok, rev):
    nc = n_tok // HGRN_CHUNK
    cb = min(HGRN_CHUNKS_PER_STEP, nc)
    nt = nc // cb
    rows = cb * HGRN_CHUNK
    tt = (lambda t: nt - 1 - t) if rev else (lambda t: t)
    col = lambda base: pl.BlockSpec((rows, LANES), lambda h, t: (tt(t), base + h))
    head_cols = pl.BlockSpec((rows, HGRN_HEAD_WIDTH), lambda h, t: (tt(t), COL_HGRN + h))
    head_vec = pl.BlockSpec((1, LANES), lambda h, t: (0, h))
    one_vec = pl.BlockSpec((1, LANES), lambda h, t: (0, 0))
    state = pl.BlockSpec((1, cb, HGRN_DK, HGRN_DK), lambda h, t: (h, tt(t), 0, 0))
    return nc, cb, nt, col, head_cols, head_vec, one_vec, state


def _hgrn_fwd(proj, lb, ng):
    n_tok = proj.shape[0]
    nc, cb, nt, col, head_cols, head_vec, one_vec, state = _hgrn_specs(n_tok, False)

    def body(in_ref, lb_ref, ng_ref, y_ref, st_ref, s_acc):
        @pl.when(pl.program_id(1) == 0)
        def _():
            s_acc[...] = jnp.zeros_like(s_acc)

        fl, qh, v, og = (in_ref[:, i * LANES:(i + 1) * LANES].reshape(cb, HGRN_CHUNK, LANES) for i in range(4))
        q_dec, k_inv, k_end, decay = _hg_elem(fl, qh, lb_ref[...])
        upd = _bmm_tn(_bf(v), _bf(k_end))
        st = s_acc[...]
        for ci in range(cb):
            st_ref[0, ci] = st
            st = st * decay[ci] + upd[ci]
        s_acc[...] = st
        o, _ = _hg_out(q_dec, k_inv, v, st_ref[0])
        y_ref[...] = _bf(_hg_post(o, og, ng_ref[...]).reshape(cb * HGRN_CHUNK, LANES))

    return pl.pallas_call(
        body, name="hgrn_fwd", grid=(HGRN_HEADS, nt),
        in_specs=[head_cols, head_vec, one_vec],
        out_specs=[col(0), state],
        out_shape=[jax.ShapeDtypeStruct((n_tok, D_MODEL), BF16),
                   jax.ShapeDtypeStruct((HGRN_HEADS, nc, HGRN_DK, HGRN_DK), F32)],
        scratch_shapes=[pltpu.VMEM((HGRN_DK, HGRN_DK), F32)],
        compiler_params=_params(("arbitrary", "arbitrary")),
    )(proj, lb, ng)


def _hgrn_bwd(proj, lb, ng, states, dy, dproj, comm=None):
    n_tok = proj.shape[0]
    nc, cb, nt, col, head_cols, head_vec, one_vec, state = _hgrn_specs(n_tok, True)

    def body(in_ref, lb_ref, ng_ref, st_ref, dy_ref, _, d_ref, dlb_ref, dng_ref, g_acc, g_all):
        h = pl.program_id(0)
        t = pl.program_id(1)

        @pl.when(t == 0)
        def _():
            g_acc[...] = jnp.zeros_like(g_acc)
            dlb_ref[...] = jnp.zeros_like(dlb_ref)

        @pl.when((t == 0) & (h == 0))
        def _():
            dng_ref[...] = jnp.zeros_like(dng_ref)

        fl, qh, v, og = (in_ref[:, i * LANES:(i + 1) * LANES].reshape(cb, HGRN_CHUNK, LANES) for i in range(4))
        (q_dec, k_inv, k_end, decay), elem_vjp = jax.vjp(_hg_elem, fl, qh, lb_ref[...])
        st = st_ref[0]
        o, sc = _hg_out(q_dec, k_inv, v, st)
        _, post_vjp = jax.vjp(_hg_post, o, og, ng_ref[...])
        do, dog, dng = post_vjp(dy_ref[...].reshape(cb, HGRN_CHUNK, LANES))
        dob, vb, qb = _bf(do), _bf(v), _bf(q_dec)
        dsc = _bf(jnp.where(_tril(cb), _bmm_nt(dob, vb), 0.0))
        p = _bmm_tn(dob, qb)
        g = g_acc[...]
        for ci in reversed(range(cb)):
            g_all[ci] = g
            g = g * decay[ci] + p[ci]
        g_acc[...] = g
        g = g_all[...]
        gb = _bf(g)
        dq_dec = _bmm(dsc, _bf(k_inv)) + _bmm(dob, _bf(st))
        dk_inv = _bmm_tn(dsc, qb)
        dv = _bmm_tn(_bf(sc), dob) + _bmm_nt(_bf(k_end), gb)
        dk_end = _bmm(vb, gb)
        ddecay = jnp.sum(st * g, axis=1, keepdims=True)
        dfl, dqh, dlb = elem_vjp((dq_dec, dk_inv, dk_end, ddecay))
        for i, val in enumerate((dfl, dqh, dv, dog)):
            d_ref[:, i * LANES:(i + 1) * LANES] = _bf(val.reshape(cb * HGRN_CHUNK, LANES))
        dlb_ref[...] += dlb
        dng_ref[...] += dng

    return _hosted(
        body, comm, name="hgrn_bwd", grid=(HGRN_HEADS, nt),
        in_specs=[head_cols, head_vec, one_vec, state, col(0), pl.BlockSpec(memory_space=pl.ANY)],
        out_specs=[head_cols, head_vec, one_vec],
        out_shape=[jax.ShapeDtypeStruct(dproj.shape, dproj.dtype),
                   jax.ShapeDtypeStruct((1, D_MODEL), F32), jax.ShapeDtypeStruct((1, LANES), F32)],
        scratch_shapes=[pltpu.VMEM((HGRN_DK, HGRN_DK), F32), pltpu.VMEM((cb, HGRN_DK, HGRN_DK), F32)],
        compiler_params=_params(("arbitrary", "arbitrary")), args=[proj, lb, ng, states, dy, dproj], aliases={5: 0})


def _lb_fwd(lb_logits):
    def lb_of(l0, l1):
        m = jnp.maximum(l0, l1)
        e0, e1 = jnp.exp(l0 - m), jnp.exp(l1 - m)
        return e0 / (e0 + e1)

    def body(l_ref, o_ref):
        o_ref[...] = lb_of(l_ref[0:1, :], l_ref[1:2, :])

    lb = pl.pallas_call(body, name="lb_fwd", out_shape=jax.ShapeDtypeStruct((1, D_MODEL), F32))(lb_logits)
    return lb, lb_of


def _mix_fwd(y_attn, y_hgrn, proj, x1, w_pa, w_ph, w_out, g, b):
    n = x1.shape[0]
    tm = min(ROW_TILE, n)

    def body(ya_ref, yh_ref, gt_ref, x_ref, wpa, wph, wo, g_ref, b_ref, z_ref, o_ref, pa_ref, ph_ref):
        ya = _mm(ya_ref[...], wpa[...])
        yh = _mm(yh_ref[...], wph[...])
        pa_ref[...] = _bf(ya)
        ph_ref[...] = _bf(yh)
        merged = _sig(gt_ref[:, :D_MODEL]) * ya + _sig(gt_ref[:, D_MODEL:]) * yh
        z = DEEPNORM_ALPHA * x_ref[...] + _mm(_bf(merged), wo[...])
        z_ref[...] = z
        o_ref[...] = _ln(z, g_ref[...], b_ref[...])

    row = pl.BlockSpec((tm, D_MODEL), lambda i: (i, 0))
    gates = pl.BlockSpec((tm, GATES_WIDTH), lambda i: (i, 0))
    sq = _resident((D_MODEL, D_MODEL))
    vec = _full((1, D_MODEL))
    return pl.pallas_call(
        body, name="mix_fwd", grid=(n // tm,),
        in_specs=[row, row, gates, row, sq, sq, sq, vec, vec],
        out_specs=[row, row, row, row],
        out_shape=[jax.ShapeDtypeStruct((n, D_MODEL), F32)] * 2 + [jax.ShapeDtypeStruct((n, D_MODEL), BF16)] * 2,
        compiler_params=_params(("arbitrary",)),
    )(y_attn, y_hgrn, proj, x1, w_pa, w_ph, w_out, g, b)


def _mix_bwd(dx2, z2, pa, ph, proj, w_pa, w_ph, w_out, g, b):
    n = z2.shape[0]
    tm = min(MIX_TILE, n)

    def body(do_ref, z_ref, ya_ref, yh_ref, gt_ref, wpa, wph, wo, g_ref, b_ref,
             dz_ref, dzb_ref, mg_ref, dya_ref, dyh_ref, dyat_ref, dyhg_ref, dgt_ref, dg_ref, db_ref):
        _, vjp = jax.vjp(_ln, z_ref[...], g_ref[...], b_ref[...])
        dz, dg, db = vjp(do_ref[...])

        @pl.when(pl.program_id(0) == 0)
        def _():
            dg_ref[...] = jnp.zeros_like(dg_ref)
            db_ref[...] = jnp.zeros_like(db_ref)

        dg_ref[...] += dg
        db_ref[...] += db
        dz_ref[...] = dz
        ya = ya_ref[...].astype(F32)
        yh = yh_ref[...].astype(F32)
        def merge(ga, gh, ya, yh):
            return _sig(ga) * ya + _sig(gh) * yh

        merged, merge_vjp = jax.vjp(merge, gt_ref[:, :D_MODEL], gt_ref[:, D_MODEL:], ya, yh)
        mg_ref[...] = _bf(merged)
        dzb = _bf(dz)
        dzb_ref[...] = dzb
        dmerged = _mm_nt(dzb, wo[...])
        dga, dgh, dya, dyh = merge_vjp(dmerged)
        dya = _bf(dya)
        dyh = _bf(dyh)
        dya_ref[...] = dya
        dyh_ref[...] = dyh
        dgt_ref[:, :D_MODEL] = _bf(dga)
        dgt_ref[:, D_MODEL:] = _bf(dgh)
        dyat_ref[...] = _bf(_mm_nt(dya, wpa[...]))
        dyhg_ref[...] = _mm_nt(dyh, wph[...])

    row = pl.BlockSpec((tm, D_MODEL), lambda i: (i, 0))
    gates = pl.BlockSpec((tm, GATES_WIDTH), lambda i: (i, 0))
    sq = _resident((D_MODEL, D_MODEL))
    vec = _full((1, D_MODEL))
    f32_row = jax.ShapeDtypeStruct((n, D_MODEL), F32)
    bf_row = jax.ShapeDtypeStruct((n, D_MODEL), BF16)
    vec_shape = jax.ShapeDtypeStruct((1, D_MODEL), F32)
    return pl.pallas_call(
        body, name="mix_bwd", grid=(n // tm,),
        in_specs=[row, row, row, row, gates, sq, sq, sq, vec, vec],
        out_specs=[row, row, row, row, row, row, row, gates, vec, vec],
        out_shape=[f32_row, bf_row, bf_row, bf_row, bf_row, bf_row, f32_row,
                   jax.ShapeDtypeStruct((n, D_IN), BF16), vec_shape, vec_shape],
        compiler_params=_params(("arbitrary",)),
    )(dx2, z2, pa, ph, proj, w_pa, w_ph, w_out, g, b)


def _position():
    x, y, c = lax.axis_index("x"), lax.axis_index("y"), lax.axis_index("c")
    chips = [(1 - x, y), (x, 1 - y), (1 - x, 1 - y)]
    return x, y, c, chips


def _any_specs(k):
    return [pl.BlockSpec(memory_space=pl.ANY)] * k


class _GatherWeights:
    def __init__(self, shards):
        nw = len(shards)
        self.inputs = list(shards)
        self.out_shape = [jax.ShapeDtypeStruct((N_CHIPS, *s.shape), s.dtype) for s in shards]
        self.scratch = [pltpu.SemaphoreType.DMA((nw,)), pltpu.SemaphoreType.DMA((nw * 6,)),
                        pltpu.SemaphoreType.DMA((nw * 6,))]

    def _copies(self, ins, outs, sems):
        nw = len(ins)
        local_sem, send_sem, recv_sem = sems
        x, y, c, chips = _position()
        me = 2 * x + y
        sibling = (x, y, 1 - c)
        half_rows = [s.shape[0] // 2 for s in self.inputs]

        def half(w, chip_idx, which):
            return outs[w].at[chip_idx, pl.ds(which * half_rows[w], half_rows[w])]

        def remote(w, k, src, dst, to):
            return pltpu.make_async_remote_copy(src_ref=src, dst_ref=dst, send_sem=send_sem.at[w * 6 + k],
                                                recv_sem=recv_sem.at[w * 6 + k], device_id=to, device_id_type=MESH)

        local = [pltpu.make_async_copy(ins[w], outs[w].at[me], local_sem.at[w]) for w in range(nw)]
        first = [remote(w, j, ins[w].at[pl.ds(c * half_rows[w], half_rows[w])], half(w, me, c), (px, py, c))
                 for w in range(nw) for j, (px, py) in enumerate(chips)]
        landed = [half(w, 2 * px + py, c) for w in range(nw) for (px, py) in chips]
        arrive = [remote(w, j, landed[w * 3 + j], landed[w * 3 + j], (px, py, c))
                  for w in range(nw) for j, (px, py) in enumerate(chips)]
        passed = [remote(w, 3 + j, landed[w * 3 + j], landed[w * 3 + j], sibling) for w in range(nw) for j in range(3)]
        from_sibling = [remote(w, 3 + j, half(w, 2 * px + py, 1 - c), half(w, 2 * px + py, 1 - c), sibling)
                        for w in range(nw) for j, (px, py) in enumerate(chips)]
        return local, first, arrive, passed, from_sibling

    def start(self, ins, outs, sems):
        local, first, _, _, _ = self._copies(ins, outs, sems)
        for cp in local + first:
            cp.start()

    def finish(self, ins, outs, sems):
        local, first, arrive, passed, from_sibling = self._copies(ins, outs, sems)
        for cp_in, cp_on in zip(arrive, passed):
            cp_in.wait_recv()
            cp_on.start()
        for cp in from_sibling:
            cp.wait_recv()
        for cp in first + passed:
            cp.wait_send()
        for cp in local:
            cp.wait()


class _ExchangeGrads:
    def __init__(self, grads, rows=None, into=None):
        nw = len(grads)
        self.rows = rows or (0, grads[0].shape[1])
        self.inputs = list(grads) + list(into or [])
        self.aliases = {nw + i: i for i in range(nw)} if into else {}
        self.out_shape = [jax.ShapeDtypeStruct(g.shape, g.dtype) for g in grads]
        self.scratch = [pltpu.SemaphoreType.DMA((nw,)), pltpu.SemaphoreType.DMA((nw * 3,)),
                        pltpu.SemaphoreType.DMA((nw * 3,))]

    def _copies(self, ins, outs, sems):
        nw = len(outs)
        local_sem, send_sem, recv_sem = sems
        x, y, c, chips = _position()
        me = 2 * x + y
        rows = pl.ds(*self.rows)

        def remote(w, j, src, dst, chip):
            return pltpu.make_async_remote_copy(src_ref=src, dst_ref=dst, send_sem=send_sem.at[w * 3 + j],
                                                recv_sem=recv_sem.at[w * 3 + j], device_id=(*chip, c),
                                                device_id_type=MESH)

        local = [pltpu.make_async_copy(ins[w].at[me, rows], outs[w].at[me, rows], local_sem.at[w]) for w in range(nw)]
        sends = [remote(w, j, ins[w].at[2 * px + py, rows], outs[w].at[me, rows], (px, py))
                 for w in range(nw) for j, (px, py) in enumerate(chips)]
        arrive = [remote(w, j, outs[w].at[2 * px + py, rows], outs[w].at[2 * px + py, rows], (px, py))
                  for w in range(nw) for j, (px, py) in enumerate(chips)]
        return local, sends, arrive

    def start(self, ins, outs, sems):
        local, sends, _ = self._copies(ins, outs, sems)
        for cp in local + sends:
            cp.start()

    def finish(self, ins, outs, sems):
        local, sends, arrive = self._copies(ins, outs, sems)
        for cp in arrive:
            cp.wait_recv()
        for cp in sends:
            cp.wait_send()
        for cp in local:
            cp.wait()


def _hosted(body, comm, *, name, grid, in_specs, out_specs, out_shape, scratch_shapes, compiler_params, args,
            aliases=None):
    aliases = aliases or {}
    if comm is None:
        res = pl.pallas_call(body, name=name, grid=grid, in_specs=in_specs, out_specs=out_specs, out_shape=out_shape,
                             scratch_shapes=scratch_shapes, compiler_params=compiler_params,
                             input_output_aliases=aliases)(*args)
        return list(res), []
    n_in, n_out, n_scr = len(in_specs), len(out_specs), len(scratch_shapes)
    c_in, c_out = len(comm.inputs), len(comm.out_shape)
    aliases = {**aliases, **{n_in + i: n_out + o for i, o in getattr(comm, "aliases", {}).items()}}

    def hosted_body(*refs):
        refs = list(refs)
        cut = lambda k: (refs[:k], refs[k:])
        main_in, refs = cut(n_in)
        comm_in, refs = cut(c_in)
        main_out, refs = cut(n_out)
        comm_out, refs = cut(c_out)
        main_scr, comm_scr = cut(n_scr)
        ids = [pl.program_id(a) for a in range(len(grid))]
        first = functools.reduce(jnp.logical_and, [i == 0 for i in ids])
        last = functools.reduce(jnp.logical_and, [i == g - 1 for i, g in zip(ids, grid)])

        @pl.when(first)
        def _():
            comm.start(comm_in, comm_out, comm_scr)

        body(*main_in, *main_out, *main_scr)

        @pl.when(last)
        def _():
            comm.finish(comm_in, comm_out, comm_scr)

    res = pl.pallas_call(
        hosted_body, name=name, grid=grid, in_specs=[*in_specs, *_any_specs(c_in)],
        out_specs=[*out_specs, *_any_specs(c_out)], out_shape=[*out_shape, *comm.out_shape],
        scratch_shapes=[*scratch_shapes, *comm.scratch], compiler_params=compiler_params,
        input_output_aliases=aliases,
    )(*args, *comm.inputs)
    return list(res[:n_out]), list(res[n_out:])


def _run_comm(name, comm):
    def body(*refs):
        refs = list(refs)
        c_in, c_out = len(comm.inputs), len(comm.out_shape)
        ins, outs, sems = refs[:c_in], refs[c_in:c_in + c_out], refs[c_in + c_out:]
        comm.start(ins, outs, sems)
        comm.finish(ins, outs, sems)

    return list(pl.pallas_call(
        body, name=name, in_specs=_any_specs(len(comm.inputs)), out_specs=_any_specs(len(comm.out_shape)),
        out_shape=comm.out_shape, scratch_shapes=comm.scratch,
    )(*comm.inputs))


def _sum_slots(name, slots):
    k = len(slots)
    _, rows, cols = slots[0].shape
    tr = _update_rows(rows)

    def body(*refs):
        for s_ref, o_ref in zip(refs[:k], refs[k:]):
            acc = s_ref[0].astype(F32)
            for i in range(1, N_CHIPS):
                acc = acc + s_ref[i].astype(F32)
            o_ref[...] = acc

    return pl.pallas_call(
        body, name=name, grid=(rows // tr,),
        in_specs=[pl.BlockSpec((N_CHIPS, tr, cols), lambda i: (0, i, 0))] * k,
        out_specs=[pl.BlockSpec((tr, cols), lambda i: (i, 0))] * k,
        out_shape=[jax.ShapeDtypeStruct((rows, cols), F32)] * k,
        compiler_params=_params(("arbitrary",)),
    )(*slots)


class _SwapWithSibling:
    def __init__(self, parts):
        self.inputs = list(parts)
        self.out_shape = [jax.ShapeDtypeStruct(p.shape, p.dtype) for p in parts]
        self.scratch = [pltpu.SemaphoreType.DMA((len(parts),)), pltpu.SemaphoreType.DMA((len(parts),))]

    def _copies(self, ins, outs, sems):
        send_sem, recv_sem = sems
        x, y, c, _ = _position()
        return [pltpu.make_async_remote_copy(src_ref=ins[w], dst_ref=outs[w], send_sem=send_sem.at[w],
                                             recv_sem=recv_sem.at[w], device_id=(x, y, 1 - c), device_id_type=MESH)
                for w in range(len(ins))]

    def start(self, ins, outs, sems):
        for cp in self._copies(ins, outs, sems):
            cp.start()

    def finish(self, ins, outs, sems):
        for cp in self._copies(ins, outs, sems):
            cp.wait()


class _Together:
    def __init__(self, first, second):
        self.parts = (first, second)
        self.inputs = first.inputs + second.inputs
        self.out_shape = first.out_shape + second.out_shape
        self.scratch = first.scratch + second.scratch

    def _split(self, ins, outs, sems):
        a = self.parts[0]
        ni, no, ns = len(a.inputs), len(a.out_shape), len(a.scratch)
        return ((ins[:ni], outs[:no], sems[:ns]), (ins[ni:], outs[no:], sems[ns:]))

    def start(self, ins, outs, sems):
        for part, args in zip(self.parts, self._split(ins, outs, sems)):
            part.start(*args)

    def finish(self, ins, outs, sems):
        for part, args in zip(self.parts, self._split(ins, outs, sems)):
            part.finish(*args)


def _sum_small(part):
    def body(p_ref, o_ref, buf, send_sem, recv_sem):
        x, y, c, _ = _position()
        me = 4 * x + 2 * y + c
        buf[me] = p_ref[...]
        copies = []
        for k in range(1, N_DEV):
            peer = tuple(1 - v if (k >> s) & 1 else v for v, s in ((x, 2), (y, 1), (c, 0)))
            copies.append(pltpu.make_async_remote_copy(src_ref=p_ref, dst_ref=buf.at[me], send_sem=send_sem.at[k - 1],
                                                       recv_sem=recv_sem.at[k - 1], device_id=peer, device_id_type=MESH))
        for cp in copies:
            cp.start()
        for cp in copies:
            cp.wait()
        acc = buf[0]
        for d in range(1, N_DEV):
            acc = acc + buf[d]
        o_ref[...] = acc

    vm = pl.BlockSpec(memory_space=pltpu.VMEM)
    return pl.pallas_call(
        body, name="sum_small", in_specs=[vm], out_specs=vm,
        out_shape=jax.ShapeDtypeStruct((1, SM_LEN), F32),
        scratch_shapes=[pltpu.VMEM((N_DEV, 1, SM_LEN), F32), pltpu.SemaphoreType.DMA((N_DEV - 1,)),
                        pltpu.SemaphoreType.DMA((N_DEV - 1,))],
    )(part)


def _adamw(w, g, m, v):
    m = ADAM_B1 * m + (1.0 - ADAM_B1) * g
    v = ADAM_B2 * v + (1.0 - ADAM_B2) * (g * g)
    m_hat = m / (1.0 - ADAM_B1 ** ADAM_STEP)
    v_hat = v / (1.0 - ADAM_B2 ** ADAM_STEP)
    delta = -ADAM_LR * (m_hat / (jnp.sqrt(v_hat) + ADAM_EPS) + ADAM_WD * w)
    return delta, m, v


def _adam_big(name, groups):
    k = len(groups)
    rows, cols = groups[0][2].shape
    tr = _update_rows(rows)

    def body(*refs):
        for i in range(k):
            p_ref, q_ref, w_ref, m_ref, v_ref = refs[5 * i:5 * i + 5]
            g_ref, d_ref, nm_ref, nv_ref = refs[5 * k + 4 * i:5 * k + 4 * i + 4]
            g = p_ref[...] + q_ref[...]
            g_ref[...] = g
            d_ref[...], nm_ref[...], nv_ref[...] = _adamw(w_ref[...], g, m_ref[...], v_ref[...])

    spec = pl.BlockSpec((tr, cols), lambda i: (i, 0))
    res = pl.pallas_call(
        body, name=name, grid=(rows // tr,), in_specs=[spec] * (5 * k), out_specs=[spec] * (4 * k),
        out_shape=[jax.ShapeDtypeStruct((rows, cols), F32)] * (4 * k),
        compiler_params=_params(("arbitrary",)),
    )(*[a for grp in groups for a in grp])
    return [res[4 * i:4 * i + 4] for i in range(k)]


_SMALL_AT = {"ln1_g": 0, "ln1_b": D_MODEL, "ln2_g": 2 * D_MODEL, "ln2_b": 3 * D_MODEL, "ln3_g": 4 * D_MODEL,
             "ln3_b": 5 * D_MODEL, "b_in": SM_BIN, "attn_sinks": SM_SINK, "hgrn_norm_g": SM_NG}


def _adam_small(total, w, m, v, lb_of):
    names = list(_SMALL)
    k = len(names)

    def body(*refs):
        t_ref = refs[0]
        w_refs, m_refs, v_refs = (refs[1 + i * k:1 + (i + 1) * k] for i in range(3))
        g_refs, d_refs, nm_refs, nv_refs = (refs[1 + (3 + i) * k:1 + (4 + i) * k] for i in range(4))
        for i, name in enumerate(names):
            if name == "hgrn_lb_logits":
                _, vjp = jax.vjp(lb_of, w_refs[i][0:1, :], w_refs[i][1:2, :])
                g_refs[i][0:1, :], g_refs[i][1:2, :] = vjp(t_ref[:, SM_LB:SM_LOSS])
            else:
                at = _SMALL_AT[name]
                g_refs[i][...] = t_ref[:, at:at + w_refs[i].shape[1]]
            d_refs[i][...], nm_refs[i][...], nv_refs[i][...] = _adamw(
                w_refs[i][...], g_refs[i][...], m_refs[i][...], v_refs[i][...])

    shapes = [jax.ShapeDtypeStruct(w[name].shape, F32) for name in names]
    res = pl.pallas_call(body, name="adam_small", out_shape=shapes * 4)(
        total, *[w[n] for n in names], *[m[n] for n in names], *[v[n] for n in names])
    return [dict(zip(names, res[i * k:(i + 1) * k])) for i in range(4)]


_BIG = ("ffn1_w1", "ffn1_w3", "ffn1_w2", "w_in", "w_proj_attn", "w_proj_hgrn", "w_out", "ffn2_w1", "ffn2_w3", "ffn2_w2")
_SMALL = ("ln1_g", "ln1_b", "ln2_g", "ln2_b", "ln3_g", "ln3_b", "b_in", "attn_sinks", "hgrn_norm_g", "hgrn_lb_logits")
_ORDER = ("ln1_g", "ln1_b", "ffn1_w1", "ffn1_w3", "ffn1_w2", "ln2_g", "ln2_b", "w_in", "b_in", "attn_sinks",
          "hgrn_lb_logits", "hgrn_norm_g", "w_proj_attn", "w_proj_hgrn", "w_out", "ln3_g", "ln3_b",
          "ffn2_w1", "ffn2_w3", "ffn2_w2")


_TRANSPOSED = ("ffn1_w1", "ffn1_w3", "ffn2_w1", "ffn2_w3")


def _local_view(name, arr):
    return arr[0].T if name in _TRANSPOSED else arr[0]


def _ffn_grad(name, hidden, other, comm=None):
    (dw,), comm_out = _grad_matmul(name, hidden, "cols", D_FF // FF_GRAD_PARTS, other, "shared", D_MODEL,
                                   comm=comm, parts=FF_GRAD_PARTS)
    return dw.reshape(N_CHIPS, D_FF // N_CHIPS, D_MODEL), comm_out


def kernel(x, ln1_g, ln1_b, ffn1_w1, ffn1_w3, ffn1_w2, ln2_g, ln2_b, w_in, b_in, attn_sinks, hgrn_lb_logits, hgrn_norm_g, w_proj_attn, w_proj_hgrn, w_out, ln3_g, ln3_b, ffn2_w1, ffn2_w3, ffn2_w2, loss_target, m_ln1_g, m_ln1_b, m_ffn1_w1, m_ffn1_w3, m_ffn1_w2, m_ln2_g, m_ln2_b, m_w_in, m_b_in, m_attn_sinks, m_hgrn_lb_logits, m_hgrn_norm_g, m_w_proj_attn, m_w_proj_hgrn, m_w_out, m_ln3_g, m_ln3_b, m_ffn2_w1, m_ffn2_w3, m_ffn2_w2, v_ln1_g, v_ln1_b, v_ffn1_w1, v_ffn1_w3, v_ffn1_w2, v_ln2_g, v_ln2_b, v_w_in, v_b_in, v_attn_sinks, v_hgrn_lb_logits, v_hgrn_norm_g, v_w_proj_attn, v_w_proj_hgrn, v_w_out, v_ln3_g, v_ln3_b, v_ffn2_w1, v_ffn2_w3, v_ffn2_w2):
    w = dict(ln1_g=ln1_g, ln1_b=ln1_b, ffn1_w1=ffn1_w1, ffn1_w3=ffn1_w3, ffn1_w2=ffn1_w2, ln2_g=ln2_g, ln2_b=ln2_b,
             w_in=w_in, b_in=b_in, attn_sinks=attn_sinks, hgrn_lb_logits=hgrn_lb_logits, hgrn_norm_g=hgrn_norm_g,
             w_proj_attn=w_proj_attn, w_proj_hgrn=w_proj_hgrn, w_out=w_out, ln3_g=ln3_g, ln3_b=ln3_b,
             ffn2_w1=ffn2_w1, ffn2_w3=ffn2_w3, ffn2_w2=ffn2_w2)
    mom = dict(ln1_g=m_ln1_g, ln1_b=m_ln1_b, ffn1_w1=m_ffn1_w1, ffn1_w3=m_ffn1_w3, ffn1_w2=m_ffn1_w2, ln2_g=m_ln2_g,
               ln2_b=m_ln2_b, w_in=m_w_in, b_in=m_b_in, attn_sinks=m_attn_sinks, hgrn_lb_logits=m_hgrn_lb_logits,
               hgrn_norm_g=m_hgrn_norm_g, w_proj_attn=m_w_proj_attn, w_proj_hgrn=m_w_proj_hgrn, w_out=m_w_out,
               ln3_g=m_ln3_g, ln3_b=m_ln3_b, ffn2_w1=m_ffn2_w1, ffn2_w3=m_ffn2_w3, ffn2_w2=m_ffn2_w2)
    var = dict(ln1_g=v_ln1_g, ln1_b=v_ln1_b, ffn1_w1=v_ffn1_w1, ffn1_w3=v_ffn1_w3, ffn1_w2=v_ffn1_w2, ln2_g=v_ln2_g,
               ln2_b=v_ln2_b, w_in=v_w_in, b_in=v_b_in, attn_sinks=v_attn_sinks, hgrn_lb_logits=v_hgrn_lb_logits,
               hgrn_norm_g=v_hgrn_norm_g, w_proj_attn=v_w_proj_attn, w_proj_hgrn=v_w_proj_hgrn, w_out=v_w_out,
               ln3_g=v_ln3_g, ln3_b=v_ln3_b, ffn2_w1=v_ffn2_w1, ffn2_w3=v_ffn2_w3, ffn2_w2=v_ffn2_w2)

    n_tok = x.shape[1]
    x0 = x.reshape(n_tok, D_MODEL)
    target = loss_target.reshape(n_tok, D_MODEL)

    shard = {k: _bf(_local_view(k, w[k])) for k in _BIG}
    gather = lambda keys: _GatherWeights([shard[k] for k in keys])
    slots = {}
    exchange = lambda keys: _ExchangeGrads([big[k] for k in keys])
    ffn1_keys = ("ffn1_w1", "ffn1_w3", "ffn1_w2")
    mixer_keys = ("w_in", "w_proj_attn", "w_proj_hgrn", "w_out")
    ffn2_keys = ("ffn2_w1", "ffn2_w3", "ffn2_w2")
    whole = lambda ts: [t.reshape(D_FF, D_MODEL) for t in ts]
    f1 = whole(_run_comm("gather_ffn1", gather(ffn1_keys)))

    tabs = _rope_tables(n_tok)
    lb, lb_of = _lb_fwd(hgrn_lb_logits)
    (z1, x1, x0b, h1_1, h3_1, x1b), (w_in_g, w_pa, w_ph, w_o) = _ffn_fwd(
        "ffn1_fwd", x0, *f1, ln1_g, ln1_b, comm=gather(mixer_keys))
    w_pa, w_ph, w_o = (t.reshape(D_MODEL, D_MODEL) for t in (w_pa, w_ph, w_o))
    w_in_g = _reorder_w_in("w_in_cols", w_in_g, True)
    (proj,), f2 = _in_proj(x1b, w_in_g, _to_kernel_cols(b_in), comm=gather(ffn2_keys))
    f2 = whole(f2)
    y_attn = _attn_fwd(proj, tabs, attn_sinks)
    y_hgrn, states = _hgrn_fwd(proj, lb, hgrn_norm_g)
    z2, x2, proj_a, proj_h = _mix_fwd(y_attn, y_hgrn, proj, x1, w_pa, w_ph, w_o, ln2_g, ln2_b)
    (z3, dy, x2b, h1_2, h3_2, loss_part), _ = _ffn_fwd("ffn2_fwd", x2, *f2, ln3_g, ln3_b, target=target)

    big = {}
    small = {}
    (dx2, a2, dh1_2, dh3_2, df2, small["ln3_g"], small["ln3_b"]), _ = _ffn_bwd(
        "ffn2_bwd", h1_2, h3_2, z3, dy, *f2, ln3_g, ln3_b)
    big["ffn2_w1"], _ = _ffn_grad("ffn2_dw1", dh1_2, x2b)
    big["ffn2_w3"], _ = _ffn_grad("ffn2_dw3", dh3_2, x2b)
    big["ffn2_w2"], _ = _ffn_grad("ffn2_dw2", a2, df2)
    (dz2, dz2b, merged, dya, dyh, dy_attn, dy_hgrn, dproj, small["ln2_g"], small["ln2_b"]) = _mix_bwd(
        dx2, z2, proj_a, proj_h, proj, w_pa, w_ph, w_o, ln2_g, ln2_b)
    for key, name, lhs, rhs in (("w_out", "dw_out", merged, dz2b), ("w_proj_attn", "dw_proj_attn", y_attn, dya),
                                ("w_proj_hgrn", "dw_proj_hgrn", y_hgrn, dyh)):
        (dw,), _ = _grad_matmul(name, lhs, "cols", D_MODEL, rhs, "shared", D_MODEL, parts=1)
        big[key] = dw.reshape(N_CHIPS, PROJ_SHARD, D_MODEL)
    (dproj, dlb, small["hgrn_norm_g"]), early = _hgrn_bwd(
        proj, lb, hgrn_norm_g, states, dy_hgrn, dproj, comm=exchange(ffn2_keys))
    slots.update(zip(ffn2_keys, early))
    (dproj, dsinks), early = _attn_bwd(proj, dy_attn, tabs, attn_sinks, dproj, comm=exchange(mixer_keys[1:]))
    slots.update(zip(mixer_keys[1:], early))
    (dw_in, db_in), _ = _grad_matmul("dw_in", x1b, "shared", D_MODEL, dproj, "cols", IN_SHARD, colsum=True)
    big["w_in"] = _reorder_w_in("dw_in_cols", dw_in, False)
    small["b_in"] = _from_kernel_cols(db_in)
    (dx1,), (w_in_slots,) = _in_proj_dx(
        dproj, w_in_g, dz2, comm=_ExchangeGrads([big["w_in"]], rows=(0, W_IN_ROWS_FIRST)))
    (grad_x, a1, dh1_1, dh3_1, df1, small["ln1_g"], small["ln1_b"]), _ = _ffn_bwd(
        "ffn1_bwd", h1_1, h3_1, z1, dx1, *f1, ln1_g, ln1_b)
    big["ffn1_w1"], (slots["w_in"],) = _ffn_grad("ffn1_dw1", dh1_1, x0b, comm=_ExchangeGrads(
        [big["w_in"]], rows=(W_IN_ROWS_FIRST, D_MODEL - W_IN_ROWS_FIRST), into=[w_in_slots]))
    big["ffn1_w3"], (slots["ffn1_w1"],) = _ffn_grad("ffn1_dw3", dh3_1, x0b, comm=exchange(("ffn1_w1",)))
    big["ffn1_w2"], (slots["ffn1_w3"],) = _ffn_grad("ffn1_dw2", a1, df1, comm=exchange(("ffn1_w3",)))

    last = "ffn1_w2"
    groups = [[k for k in ffn1_keys + ffn2_keys if k != last], list(mixer_keys[1:]), ["w_in"]]
    partial = {}
    for keys in groups:
        partial.update(zip(keys, _sum_slots("sum_" + keys[0], [slots[k] for k in keys])))
    swapped_keys = [k for keys in groups for k in keys]
    moved = _run_comm("swap_and_exchange_last",
                      _Together(_SwapWithSibling([partial[k] for k in swapped_keys]), exchange((last,))))
    from_sibling = dict(zip(swapped_keys, moved))
    (partial[last],) = _sum_slots("sum_" + last, [moved[-1]])
    (from_sibling[last],) = _run_comm("swap_last", _SwapWithSibling([partial[last]]))
    groups[0].append(last)

    outs = {"grad": {}, "delta": {}, "m": {}, "v": {}}
    for keys in groups:
        res = _adam_big("adam_" + keys[0], [(partial[k], from_sibling[k], _local_view(k, w[k]), _local_view(k, mom[k]),
                                             _local_view(k, var[k])) for k in keys])
        for k, four in zip(keys, res):
            for kind, r in zip(("grad", "delta", "m", "v"), four):
                outs[kind][k] = (r.T if k in _TRANSPOSED else r).reshape(w[k].shape)

    total = _sum_small(jnp.concatenate(
        [small[k] for k in ("ln1_g", "ln1_b", "ln2_g", "ln2_b", "ln3_g", "ln3_b", "b_in")]
        + [dsinks, small["hgrn_norm_g"], dlb, loss_part], axis=1))
    for kind, r in zip(("grad", "delta", "m", "v"), _adam_small(total, w, mom, var, lb_of)):
        outs[kind].update(r)
    loss = total[0, SM_LOSS]

    return (loss, grad_x.reshape(x.shape), *[outs["grad"][k] for k in _ORDER], *[outs["delta"][k] for k in _ORDER],
            *[outs["m"][k] for k in _ORDER], *[outs["v"][k] for k in _ORDER])
```

```python
import functools

import jax
import jax.numpy as jnp
from jax import lax
from jax.experimental import pallas as pl
from jax.experimental.pallas import tpu as pltpu

F32 = jnp.float32
BF16 = jnp.bfloat16

D_MODEL = 1024
N_Q_HEADS = 16
N_KV_HEADS = 4
HEAD_DIM = 64
ATTN_BLOCK = 128
ROPE_THETA = 500000.0
ROPE_DIM = HEAD_DIM // 4
HGRN_HEADS = 8
HGRN_DK = 128
HGRN_CHUNK = 64
D_FF = 2816
D_IN = 7680
DEEPNORM_ALPHA = 2 ** 0.25
LN_EPS = 1e-5
RMS_EPS = 1e-6
NEG_INF = -1e30

ADAM_LR = 0.001
ADAM_B1 = 0.9
ADAM_B2 = 0.999
ADAM_EPS = 1e-08
ADAM_WD = 0.01
ADAM_STEP = 10

N_CHIPS = 4
N_DEV = 8
LANES = 128
FF_GRAD_PARTS = 2
FFN_TILE = 256
IN_SHARD = D_IN // N_CHIPS
PROJ_SHARD = D_MODEL // N_CHIPS
ROW_TILE = 512
GRAD_ROWS = 2048
MIX_TILE = 256
UPDATE_ROWS = 128
HGRN_CHUNKS_PER_STEP = 32
W_IN_ROWS_FIRST = 640
VMEM_LIMIT = 56 * 1024 * 1024

GATES_WIDTH = 2 * D_MODEL
HGRN_HEAD_WIDTH = 4 * HGRN_DK
ATTN_WIDTH = D_MODEL + 2 * N_KV_HEADS * HEAD_DIM
COL_HGRN = GATES_WIDTH // HGRN_HEAD_WIDTH
COL_ATTN = (GATES_WIDTH + HGRN_HEADS * HGRN_HEAD_WIDTH) // ATTN_WIDTH
COL_Q = (GATES_WIDTH + HGRN_HEADS * HGRN_HEAD_WIDTH) // D_MODEL
COL_K = (GATES_WIDTH + HGRN_HEADS * HGRN_HEAD_WIDTH + D_MODEL) // (N_KV_HEADS * HEAD_DIM)
COL_V = COL_K + 1


def _to_kernel_cols(a):
    lead = a.shape[:-1]
    qkv, hg, gates = a[..., :ATTN_WIDTH], a[..., ATTN_WIDTH:D_IN - GATES_WIDTH], a[..., D_IN - GATES_WIDTH:]
    hg = jnp.swapaxes(hg.reshape(*lead, 4, HGRN_HEADS, HGRN_DK), -3, -2).reshape(*lead, -1)
    return jnp.concatenate([gates, hg, qkv], axis=-1)


def _from_kernel_cols(a):
    lead = a.shape[:-1]
    gates, hg, qkv = a[..., :GATES_WIDTH], a[..., GATES_WIDTH:D_IN - ATTN_WIDTH], a[..., D_IN - ATTN_WIDTH:]
    hg = jnp.swapaxes(hg.reshape(*lead, HGRN_HEADS, 4, HGRN_DK), -3, -2).reshape(*lead, -1)
    return jnp.concatenate([qkv, hg, gates], axis=-1)

SM_LN = 0
SM_BIN = 6 * D_MODEL
SM_SINK = SM_BIN + D_IN
SM_NG = SM_SINK + LANES
SM_LB = SM_NG + LANES
SM_LOSS = SM_LB + D_MODEL
SM_LEN = SM_LOSS + LANES

MESH = pl.DeviceIdType.MESH


def _mm(a, b):
    return lax.dot_general(a, b, (((1,), (0,)), ((), ())), preferred_element_type=F32)


def _mm_nt(a, b):
    return lax.dot_general(a, b, (((1,), (1,)), ((), ())), preferred_element_type=F32)


def _mm_tn(a, b):
    return lax.dot_general(a, b, (((0,), (0,)), ((), ())), preferred_element_type=F32)


def _bf(v):
    return v.astype(BF16)


def _sig(v):
    return jax.nn.sigmoid(v)


def _ln(z, g, b):
    mu = jnp.mean(z, axis=-1, keepdims=True)
    zc = z - mu
    var = jnp.mean(zc * zc, axis=-1, keepdims=True)
    return zc * lax.rsqrt(var + LN_EPS) * g + b


def _swiglu_act(h1, h3):
    return (h1 * _sig(h1)) * h3


def _params(sem=None):
    return pltpu.CompilerParams(dimension_semantics=sem, vmem_limit_bytes=VMEM_LIMIT)


def _full(shape):
    nd = len(shape)
    return pl.BlockSpec(shape, lambda *_: (0,) * nd)


def _update_rows(rows):
    return max(t for t in range(8, UPDATE_ROWS + 1, 8) if rows % t == 0)


def _resident(shape):
    nd = len(shape)
    return pl.BlockSpec(shape, lambda *_: (0,) * nd, pipeline_mode=pl.Buffered(1))


def _ffn_fwd(name, x, w1t, w3t, w2, g, b, target=None, comm=None):
    n = x.shape[0]
    tm = min(ROW_TILE, n)
    final = target is not None

    def body(*refs):
        if final:
            x_ref, w1_ref, w3_ref, w2_ref, g_ref, b_ref, t_ref, z_ref, o_ref, xb_ref, h1_ref, h3_ref, loss_ref = refs
        else:
            x_ref, w1_ref, w3_ref, w2_ref, g_ref, b_ref, z_ref, o_ref, xb_ref, h1_ref, h3_ref, ob_ref = refs
        xb = _bf(x_ref[...])
        xb_ref[...] = xb
        h1 = _mm_nt(xb, w1_ref[...])
        h3 = _mm_nt(xb, w3_ref[...])
        h1_ref[...] = _bf(h1)
        h3_ref[...] = _bf(h3)
        z = DEEPNORM_ALPHA * x_ref[...] + 0.5 * _mm(_bf(_swiglu_act(h1, h3)), w2_ref[...])
        z_ref[...] = z
        y = _ln(z, g_ref[...], b_ref[...])
        if final:
            e = y - t_ref[...]

            @pl.when(pl.program_id(0) == 0)
            def _():
                loss_ref[...] = jnp.zeros_like(loss_ref)

            loss_ref[...] += jnp.sum(e * e) * (0.5 / D_MODEL)
            o_ref[...] = e * (1.0 / D_MODEL)
        else:
            o_ref[...] = y
            ob_ref[...] = _bf(y)

    row = pl.BlockSpec((tm, D_MODEL), lambda i: (i, 0))
    hid = pl.BlockSpec((tm, D_FF), lambda i: (i, 0))
    wres = _resident((D_FF, D_MODEL))
    vec = _full((1, D_MODEL))
    in_specs = [row, wres, wres, wres, vec, vec]
    args = [x, w1t, w3t, w2, g, b]
    hid_shape = jax.ShapeDtypeStruct((n, D_FF), BF16)
    out_specs = [row, row, row, hid, hid]
    out_shape = ([jax.ShapeDtypeStruct((n, D_MODEL), F32)] * 2 + [jax.ShapeDtypeStruct((n, D_MODEL), BF16)]
                 + [hid_shape] * 2)
    if final:
        in_specs.append(row)
        args.append(target)
        out_specs.append(_full((1, LANES)))
        out_shape.append(jax.ShapeDtypeStruct((1, LANES), F32))
    else:
        out_specs.append(row)
        out_shape.append(jax.ShapeDtypeStruct((n, D_MODEL), BF16))
    return _hosted(
        body, comm, name=name, grid=(n // tm,), in_specs=in_specs, out_specs=out_specs, out_shape=out_shape,
        scratch_shapes=[], compiler_params=_params(("arbitrary",)), args=args)


def _ffn_bwd(name, h1s, h3s, z, dout, w1t, w3t, w2, g, b, comm=None):
    n = z.shape[0]
    tm = min(FFN_TILE, n)

    def body(h1_ref, h3_ref, z_ref, do_ref, w1_ref, w3_ref, w2_ref, g_ref, b_ref,
             dx_ref, a_ref, dh1_ref, dh3_ref, df_ref, dg_ref, db_ref):
        _, vjp = jax.vjp(_ln, z_ref[...], g_ref[...], b_ref[...])
        dz, dg, db = vjp(do_ref[...])

        @pl.when(pl.program_id(0) == 0)
        def _():
            dg_ref[...] = jnp.zeros_like(dg_ref)
            db_ref[...] = jnp.zeros_like(db_ref)

        dg_ref[...] += dg
        db_ref[...] += db
        df = _bf(0.5 * dz)
        df_ref[...] = df
        a, act_vjp = jax.vjp(_swiglu_act, h1_ref[...].astype(F32), h3_ref[...].astype(F32))
        dh1, dh3 = act_vjp(_mm_nt(df, w2_ref[...]))
        dh1 = _bf(dh1)
        dh3 = _bf(dh3)
        a_ref[...] = _bf(a)
        dh1_ref[...] = dh1
        dh3_ref[...] = dh3
        dx_ref[...] = DEEPNORM_ALPHA * dz + _mm(dh1, w1_ref[...]) + _mm(dh3, w3_ref[...])

    row = pl.BlockSpec((tm, D_MODEL), lambda i: (i, 0))
    hid = pl.BlockSpec((tm, D_FF), lambda i: (i, 0))
    wres = _resident((D_FF, D_MODEL))
    vec = _full((1, D_MODEL))
    hid_shape = jax.ShapeDtypeStruct((n, D_FF), BF16)
    return _hosted(
        body, comm, name=name, grid=(n // tm,),
        in_specs=[hid, hid, row, row, wres, wres, wres, vec, vec],
        out_specs=[row, hid, hid, hid, row, vec, vec],
        out_shape=[jax.ShapeDtypeStruct((n, D_MODEL), F32), hid_shape, hid_shape, hid_shape,
                   jax.ShapeDtypeStruct((n, D_MODEL), BF16),
                   jax.ShapeDtypeStruct((1, D_MODEL), F32), jax.ShapeDtypeStruct((1, D_MODEL), F32)],
        scratch_shapes=[], compiler_params=_params(("arbitrary",)),
        args=[h1s, h3s, z, dout, w1t, w3t, w2, g, b])


def _operand_spec(arr, mode, tn, width, parts):
    if mode == "shared":
        return pl.BlockSpec((tn, width), lambda s, k: (k, 0))
    assert mode == "cols" and arr.shape[1] == parts * width
    return pl.BlockSpec((tn, width), lambda s, k: (k, s))


def _grad_matmul(name, a, a_mode, ka, b, b_mode, kb, colsum=False, comm=None, parts=N_CHIPS):
    n = a.shape[-2]
    tn = min(GRAD_ROWS, n)
    nk = n // tn

    def body(*refs):
        if colsum:
            a_ref, b_ref, o_ref, cs_ref, acc = refs
        else:
            a_ref, b_ref, o_ref, acc = refs
        k = pl.program_id(1)
        av = a_ref[...]
        bv = b_ref[...]

        @pl.when(k == 0)
        def _():
            acc[...] = jnp.zeros_like(acc)
            if colsum:
                cs_ref[...] = jnp.zeros_like(cs_ref)

        acc[...] += _mm_tn(av, bv)
        if colsum:
            cs_ref[...] += jnp.sum(bv.astype(F32), axis=0, keepdims=True)

        @pl.when(k == nk - 1)
        def _():
            o_ref[0] = _bf(acc[...])

    out_specs = [pl.BlockSpec((1, ka, kb), lambda s, k: (s, 0, 0))]
    out_shape = [jax.ShapeDtypeStruct((parts, ka, kb), BF16)]
    if colsum:
        out_specs.append(pl.BlockSpec((1, kb), lambda s, k: (0, s)))
        out_shape.append(jax.ShapeDtypeStruct((1, parts * kb), F32))
    return _hosted(
        body, comm, name=name, grid=(parts, nk),
        in_specs=[_operand_spec(a, a_mode, tn, ka, parts), _operand_spec(b, b_mode, tn, kb, parts)],
        out_specs=out_specs, out_shape=out_shape,
        scratch_shapes=[pltpu.VMEM((ka, kb), F32)],
        compiler_params=_params(("arbitrary", "arbitrary")), args=[a, b])


def _kernel_block_of(ref_block):
    attn_blocks, gate_blocks = ATTN_WIDTH // LANES, GATES_WIDTH // LANES
    hgrn_blocks = HGRN_HEADS * HGRN_HEAD_WIDTH // LANES
    if ref_block < attn_blocks:
        return gate_blocks + hgrn_blocks + ref_block
    if ref_block < attn_blocks + hgrn_blocks:
        kind, head = divmod(ref_block - attn_blocks, HGRN_HEADS)
        return gate_blocks + head * (HGRN_HEAD_WIDTH // LANES) + kind
    return ref_block - attn_blocks - hgrn_blocks


def _reorder_w_in(name, w4, to_kernel_order):
    per = IN_SHARD // LANES
    tr = MIX_TILE

    def body(i_ref, o_ref):
        for g in range(D_IN // LANES):
            s, b = divmod(g, per)
            k = _kernel_block_of(g)
            if to_kernel_order:
                o_ref[:, k * LANES:(k + 1) * LANES] = i_ref[s, :, b * LANES:(b + 1) * LANES]
            else:
                ks, kb = divmod(k, per)
                o_ref[s, :, b * LANES:(b + 1) * LANES] = i_ref[ks, :, kb * LANES:(kb + 1) * LANES]

    in_spec = pl.BlockSpec((N_CHIPS, tr, IN_SHARD), lambda i: (0, i, 0))
    if to_kernel_order:
        out_spec, out_shape = pl.BlockSpec((tr, D_IN), lambda i: (i, 0)), (D_MODEL, D_IN)
    else:
        out_spec, out_shape = in_spec, (N_CHIPS, D_MODEL, IN_SHARD)
    return pl.pallas_call(
        body, name=name, grid=(D_MODEL // tr,), in_specs=[in_spec], out_specs=out_spec,
        out_shape=jax.ShapeDtypeStruct(out_shape, w4.dtype), compiler_params=_params(("arbitrary",)),
    )(w4)


def _in_proj(x1, w_in_g, b_in, comm=None):
    n = x1.shape[0]
    tm = min(ROW_TILE, n)

    def body(x_ref, w_ref, b_ref, o_ref):
        xv = x_ref[...]
        for j in range(N_CHIPS):
            cols = slice(j * IN_SHARD, (j + 1) * IN_SHARD)
            o_ref[:, cols] = _mm(xv, w_ref[:, cols]) + b_ref[:, cols]

    return _hosted(
        body, comm, name="in_proj", grid=(n // tm,),
        in_specs=[pl.BlockSpec((tm, D_MODEL), lambda i: (i, 0)),
                  _resident((D_MODEL, D_IN)), _full((1, D_IN))],
        out_specs=[pl.BlockSpec((tm, D_IN), lambda i: (i, 0))],
        out_shape=[jax.ShapeDtypeStruct((n, D_IN), F32)],
        scratch_shapes=[],
        compiler_params=_params(("arbitrary",)), args=[x1, w_in_g, b_in])


def _in_proj_dx(dproj, w_in_g, dz2, comm=None):
    n = dproj.shape[0]
    tm = min(ROW_TILE, n)

    def body(dp_ref, w_ref, dz_ref, o_ref):
        dx = DEEPNORM_ALPHA * dz_ref[...]
        for j in range(N_CHIPS):
            cols = slice(j * IN_SHARD, (j + 1) * IN_SHARD)
            dx = dx + _mm_nt(dp_ref[:, cols], w_ref[:, cols])
        o_ref[...] = dx

    return _hosted(
        body, comm, name="in_proj_dx", grid=(n // tm,),
        in_specs=[pl.BlockSpec((tm, D_IN), lambda i: (i, 0)),
                  _resident((D_MODEL, D_IN)),
                  pl.BlockSpec((tm, D_MODEL), lambda i: (i, 0))],
        out_specs=[pl.BlockSpec((tm, D_MODEL), lambda i: (i, 0))],
        out_shape=[jax.ShapeDtypeStruct((n, D_MODEL), F32)],
        scratch_shapes=[],
        compiler_params=_params(("arbitrary",)), args=[dproj, w_in_g, dz2])


def _rope_tables(seq_len):
    pos = jnp.arange(seq_len, dtype=F32)
    inv_freq = ROPE_THETA ** (-jnp.arange(0, ROPE_DIM, 2, dtype=F32) / ROPE_DIM)
    ang = pos[:, None] * inv_freq[None, :]
    cos, sin = jnp.cos(ang), jnp.sin(ang)
    half = ROPE_DIM // 2
    rest = HEAD_DIM - ROPE_DIM
    ones = jnp.ones((seq_len, rest), F32)
    zeros = jnp.zeros((seq_len, rest), F32)
    zh = jnp.zeros((seq_len, half), F32)
    c = jnp.concatenate([cos, cos, ones], axis=1)
    sa = jnp.concatenate([-sin, zh, zeros], axis=1)
    sb = jnp.concatenate([zh, sin, zeros], axis=1)
    reps = LANES // HEAD_DIM
    return tuple(jnp.tile(t, (1, reps)) for t in (c, sa, sb))


def _rope(t, c, sa, sb):
    w = t.shape[1]
    reps = w // LANES
    half = ROPE_DIM // 2
    return (t * jnp.tile(c, (1, reps)) + pltpu.roll(t, w - half, 1) * jnp.tile(sa, (1, reps))
            + pltpu.roll(t, half, 1) * jnp.tile(sb, (1, reps)))


def _rope_transposed(g, c, sa, sb):
    w = g.shape[1]
    reps = w // LANES
    half = ROPE_DIM // 2
    return (g * jnp.tile(c, (1, reps)) + pltpu.roll(g * jnp.tile(sa, (1, reps)), half, 1)
            + pltpu.roll(g * jnp.tile(sb, (1, reps)), w - half, 1))


GROUP = N_Q_HEADS // N_KV_HEADS


def _both_halves(t_pair, which):
    lo = lax.broadcasted_iota(jnp.int32, t_pair.shape, 1) < HEAD_DIM
    swapped = pltpu.roll(t_pair, HEAD_DIM, 1)
    return _bf(jnp.where(lo, t_pair, swapped) if which == 0 else jnp.where(lo, swapped, t_pair))


def _stack_heads(ref_or_val, kh):
    lo = lax.broadcasted_iota(jnp.int32, (ATTN_BLOCK, LANES), 1) < HEAD_DIM
    rows = []
    for gp in range(GROUP // 2):
        pair = kh * (GROUP // 2) + gp
        t = ref_or_val[:, pair * LANES:(pair + 1) * LANES]
        rows += [jnp.where(lo, t, jnp.zeros_like(t)), jnp.where(lo, jnp.zeros_like(t), t)]
    return jnp.concatenate(rows, axis=0)


def _unstack_pairs(stacked):
    lo = lax.broadcasted_iota(jnp.int32, (ATTN_BLOCK, LANES), 1) < HEAD_DIM
    b = ATTN_BLOCK
    return [jnp.where(lo, stacked[2 * gp * b:(2 * gp + 1) * b], stacked[(2 * gp + 1) * b:(2 * gp + 2) * b])
            for gp in range(GROUP // 2)]


def _attn_mask_t(n):
    cols = GROUP * ATTN_BLOCK
    kj = lax.broadcasted_iota(jnp.int32, (2 * ATTN_BLOCK, cols), 0)
    qi = lax.broadcasted_iota(jnp.int32, (2 * ATTN_BLOCK, cols), 1) % ATTN_BLOCK
    dist = qi + ATTN_BLOCK - kj
    return (dist >= 0) & (dist < ATTN_BLOCK) & (n * ATTN_BLOCK + kj - ATTN_BLOCK >= 0)


def _sink_row(sink_ref, kh):
    col = lax.broadcasted_iota(jnp.int32, (1, GROUP * ATTN_BLOCK), 1)
    row = jnp.full((1, GROUP * ATTN_BLOCK), sink_ref[0, kh * GROUP + GROUP - 1], F32)
    for i in reversed(range(GROUP - 1)):
        row = jnp.where(col < (i + 1) * ATTN_BLOCK, sink_ref[0, kh * GROUP + i], row)
    return row


def _attn_probs_t(q_masked, k_sel, mask_t, sink):
    s = _mm_nt(k_sel, q_masked) * (HEAD_DIM ** -0.5)
    s = jnp.where(mask_t, s, NEG_INF)
    m = jnp.maximum(jnp.max(s, axis=0, keepdims=True), sink)
    p = jnp.exp(s - m)
    e_sink = jnp.exp(sink - m)
    denom = jnp.sum(p, axis=0, keepdims=True) + e_sink
    return p / denom, e_sink / denom


def _attn_fwd(proj, tabs, sinks):
    n_tok = proj.shape[0]
    nb = n_tok // ATTN_BLOCK

    def body(q_ref, k_ref, v_ref, c_ref, sa_ref, sb_ref, sink_ref, y_ref, kprev, vprev):
        n = pl.program_id(0)

        @pl.when(n == 0)
        def _():
            kprev[...] = jnp.zeros_like(kprev)
            vprev[...] = jnp.zeros_like(vprev)

        c, sa, sb = c_ref[...], sa_ref[...], sb_ref[...]
        qr = _bf(_rope(q_ref[...], c, sa, sb))
        kr = _rope(k_ref[...], c, sa, sb)
        vc = v_ref[...]
        kk = jnp.concatenate([kprev[...], kr], axis=0)
        vv = jnp.concatenate([vprev[...], vc], axis=0)
        kprev[...] = kr
        vprev[...] = vc
        mask = _attn_mask_t(n)
        for kh in range(N_KV_HEADS):
            r, which = divmod(kh, 2)
            kb = _both_halves(kk[:, r * LANES:(r + 1) * LANES], which)
            vb = _both_halves(vv[:, r * LANES:(r + 1) * LANES], which)
            probs, _ = _attn_probs_t(_stack_heads(qr, kh), kb, mask, _sink_row(sink_ref, kh))
            for gp, out in enumerate(_unstack_pairs(_mm_tn(_bf(probs), vb))):
                pair = kh * (GROUP // 2) + gp
                y_ref[:, pair * LANES:(pair + 1) * LANES] = _bf(out)

    blk = lambda width, col: pl.BlockSpec((ATTN_BLOCK, width), lambda n: (n, col))
    tab = pl.BlockSpec((ATTN_BLOCK, LANES), lambda n: (n, 0))
    kvw = N_KV_HEADS * HEAD_DIM
    return pl.pallas_call(
        body, name="attn_fwd", grid=(nb,),
        in_specs=[blk(D_MODEL, COL_Q), blk(kvw, COL_K), blk(kvw, COL_V), tab, tab, tab,
                  pl.BlockSpec(memory_space=pltpu.SMEM)],
        out_specs=pl.BlockSpec((ATTN_BLOCK, D_MODEL), lambda n: (n, 0)),
        out_shape=jax.ShapeDtypeStruct((n_tok, D_MODEL), BF16),
        scratch_shapes=[pltpu.VMEM((ATTN_BLOCK, kvw), F32), pltpu.VMEM((ATTN_BLOCK, kvw), F32)],
        compiler_params=_params(("arbitrary",)),
    )(proj, proj, proj, *tabs, sinks)


def _attn_bwd(proj, dy, tabs, sinks, dproj, comm=None):
    n_tok = proj.shape[0]
    nb = n_tok // ATTN_BLOCK
    kvw = N_KV_HEADS * HEAD_DIM

    def body(q_ref, k_ref, v_ref, do_ref, c_ref, sa_ref, sb_ref, cp_ref, sap_ref, sbp_ref, sink_ref, _,
             dqkv_ref, dsink_ref, kprev, vprev, dkc, dvc, dqc):
        n = pl.program_id(0)

        @pl.when(n == 0)
        def _():
            for ref in (kprev, vprev, dkc, dvc, dqc, dsink_ref):
                ref[...] = jnp.zeros_like(ref)

        prev_tabs = (cp_ref[...], sap_ref[...], sbp_ref[...])

        @pl.when(n < nb)
        def _():
            c, sa, sb = c_ref[...], sa_ref[...], sb_ref[...]
            qr = _bf(_rope(q_ref[...], c, sa, sb))
            kr = _rope(k_ref[...], c, sa, sb)
            vc = v_ref[...]
            kk = jnp.concatenate([kprev[...], kr], axis=0)
            vv = jnp.concatenate([vprev[...], vc], axis=0)
            kprev[...] = kr
            vprev[...] = vc
            mask = _attn_mask_t(n)
            lane = lax.broadcasted_iota(jnp.int32, (1, LANES), 1)
            lo2 = lax.broadcasted_iota(jnp.int32, (2 * ATTN_BLOCK, LANES), 1) < HEAD_DIM
            dsink = jnp.zeros((1, LANES), F32)
            dq_pairs = []
            dk_full = []
            dv_full = []
            for kh in range(N_KV_HEADS):
                r, which = divmod(kh, 2)
                kb = _both_halves(kk[:, r * LANES:(r + 1) * LANES], which)
                vb = _both_halves(vv[:, r * LANES:(r + 1) * LANES], which)
                qs = _stack_heads(qr, kh)
                dos = _stack_heads(do_ref, kh)
                probs, p_sink = _attn_probs_t(qs, kb, mask, _sink_row(sink_ref, kh))
                dp = _mm_nt(vb, dos)
                delta = jnp.sum(probs * dp, axis=0, keepdims=True)
                ds = _bf(probs * (dp - delta) * (HEAD_DIM ** -0.5))
                sink_terms = p_sink * delta
                for i in range(GROUP):
                    head_sum = jnp.sum(sink_terms[:, i * ATTN_BLOCK:(i + 1) * ATTN_BLOCK])
                    dsink = dsink + jnp.where(lane == kh * GROUP + i, -head_sum, 0.0)
                dq_pairs += _unstack_pairs(_mm_tn(ds, kb))
                dk_acc = _mm(ds, qs)
                dv_acc = _mm(_bf(probs), dos)
                dk_full.append(dk_acc + pltpu.roll(dk_acc, HEAD_DIM, 1))
                dv_full.append(dv_acc + pltpu.roll(dv_acc, HEAD_DIM, 1))
            dk_pairs = [jnp.where(lo2, dk_full[2 * r], dk_full[2 * r + 1]) for r in range(N_KV_HEADS // 2)]
            dv_pairs = [jnp.where(lo2, dv_full[2 * r], dv_full[2 * r + 1]) for r in range(N_KV_HEADS // 2)]
            dsink_ref[...] += dsink
            dqkv_ref[:, :D_MODEL] = _bf(dqc[...])
            dqc[...] = _rope_transposed(jnp.concatenate(dq_pairs, axis=1), c, sa, sb)
            dk_all = jnp.concatenate(dk_pairs, axis=1)
            dv_all = jnp.concatenate(dv_pairs, axis=1)
            dqkv_ref[:, D_MODEL:D_MODEL + kvw] = _bf(_rope_transposed(dkc[...] + dk_all[:ATTN_BLOCK], *prev_tabs))
            dqkv_ref[:, D_MODEL + kvw:] = _bf(dvc[...] + dv_all[:ATTN_BLOCK])
            dkc[...] = dk_all[ATTN_BLOCK:]
            dvc[...] = dv_all[ATTN_BLOCK:]

        @pl.when(n == nb)
        def _():
            dqkv_ref[:, :D_MODEL] = _bf(dqc[...])
            dqkv_ref[:, D_MODEL:D_MODEL + kvw] = _bf(_rope_transposed(dkc[...], *prev_tabs))
            dqkv_ref[:, D_MODEL + kvw:] = _bf(dvc[...])

    cur = lambda n: jnp.minimum(n, nb - 1)
    prev = lambda n: jnp.maximum(n - 1, 0)
    blk = lambda width, col: pl.BlockSpec((ATTN_BLOCK, width), lambda n: (cur(n), col))
    tab = pl.BlockSpec((ATTN_BLOCK, LANES), lambda n: (cur(n), 0))
    tabp = pl.BlockSpec((ATTN_BLOCK, LANES), lambda n: (prev(n), 0))
    return _hosted(
        body, comm, name="attn_bwd", grid=(nb + 1,),
        in_specs=[blk(D_MODEL, COL_Q), blk(kvw, COL_K), blk(kvw, COL_V), blk(D_MODEL, 0), tab, tab, tab, tabp, tabp, tabp,
                  pl.BlockSpec(memory_space=pltpu.SMEM), pl.BlockSpec(memory_space=pl.ANY)],
        out_specs=[pl.BlockSpec((ATTN_BLOCK, ATTN_WIDTH), lambda n: (prev(n), COL_ATTN)),
                   pl.BlockSpec((1, LANES), lambda n: (0, 0))],
        out_shape=[jax.ShapeDtypeStruct(dproj.shape, dproj.dtype), jax.ShapeDtypeStruct((1, LANES), F32)],
        scratch_shapes=[pltpu.VMEM((ATTN_BLOCK, kvw), F32)] * 4 + [pltpu.VMEM((ATTN_BLOCK, D_MODEL), F32)],
        compiler_params=_params(("arbitrary",)), args=[proj, proj, proj, dy, *tabs, *tabs, sinks, dproj],
        aliases={11: 0})


def _bmm(a, b):
    return lax.dot_general(a, b, (((2,), (1,)), ((0,), (0,))), preferred_element_type=F32)


def _bmm_nt(a, b):
    return lax.dot_general(a, b, (((2,), (2,)), ((0,), (0,))), preferred_element_type=F32)


def _bmm_tn(a, b):
    return lax.dot_general(a, b, (((1,), (1,)), ((0,), (0,))), preferred_element_type=F32)


def _tril(cb, upper=False):
    shape = (cb, HGRN_CHUNK, HGRN_CHUNK)
    r, c = lax.broadcasted_iota(jnp.int32, shape, 1), lax.broadcasted_iota(jnp.int32, shape, 2)
    return (r <= c) if upper else (r >= c)


def _tri_matmul(x, upper):
    return lax.dot_general(_tril(x.shape[0], upper).astype(F32), x, (((2,), (1,)), ((0,), (0,))),
                           precision=lax.Precision.HIGHEST, preferred_element_type=F32)


@jax.custom_vjp
def _chunk_cumsum(x):
    return _tri_matmul(x, False)


_chunk_cumsum.defvjp(lambda x: (_tri_matmul(x, False), None), lambda _, g: (_tri_matmul(g, True),))


def _hg_elem(fl, qh, lb):
    f = lb + (1.0 - lb) * _sig(fl)
    k = 1.0 - f
    gc = _chunk_cumsum(jnp.log(f))
    last = lax.broadcasted_iota(jnp.int32, gc.shape, 1) == HGRN_CHUNK - 1
    g_last = jnp.sum(jnp.where(last, gc, 0.0), axis=1, keepdims=True)
    q = qh * _sig(qh)
    return q * jnp.exp(gc), k * jnp.exp(-gc), k * jnp.exp(g_last - gc), jnp.exp(g_last)


def _hg_out(q_dec, k_inv, v, st):
    sc = jnp.where(_tril(q_dec.shape[0]), _bmm_nt(_bf(q_dec), _bf(k_inv)), 0.0)
    return _bmm(_bf(sc), _bf(v)) + _bmm_nt(_bf(q_dec), _bf(st)), sc


def _hg_post(o, og, ng):
    on = o * lax.rsqrt(jnp.mean(o * o, axis=-1, keepdims=True) + RMS_EPS) * ng
    return on * (og * _sig(og))


def _hgrn_specs(n_tok, rev):
    nc = n_tok // HGRN_CHUNK
    cb = min(HGRN_CHUNKS_PER_STEP, nc)
    nt = nc // cb
    rows = cb * HGRN_CHUNK
    tt = (lambda t: nt - 1 - t) if rev else (lambda t: t)
    col = lambda base: pl.BlockSpec((rows, LANES), lambda h, t: (tt(t), base + h))
    head_cols = pl.BlockSpec((rows, HGRN_HEAD_WIDTH), lambda h, t: (tt(t), COL_HGRN + h))
    head_vec = pl.BlockSpec((1, LANES), lambda h, t: (0, h))
    one_vec = pl.BlockSpec((1, LANES), lambda h, t: (0, 0))
    state = pl.BlockSpec((1, cb, HGRN_DK, HGRN_DK), lambda h, t: (h, tt(t), 0, 0))
    return nc, cb, nt, col, head_cols, head_vec, one_vec, state


def _hgrn_fwd(proj, lb, ng):
    n_tok = proj.shape[0]
    nc, cb, nt, col, head_cols, head_vec, one_vec, state = _hgrn_specs(n_tok, False)

    def body(in_ref, lb_ref, ng_ref, y_ref, st_ref, s_acc):
        @pl.when(pl.program_id(1) == 0)
        def _():
            s_acc[...] = jnp.zeros_like(s_acc)

        fl, qh, v, og = (in_ref[:, i * LANES:(i + 1) * LANES].reshape(cb, HGRN_CHUNK, LANES) for i in range(4))
        q_dec, k_inv, k_end, decay = _hg_elem(fl, qh, lb_ref[...])
        upd = _bmm_tn(_bf(v), _bf(k_end))
        st = s_acc[...]
        for ci in range(cb):
            st_ref[0, ci] = st
            st = st * decay[ci] + upd[ci]
        s_acc[...] = st
        o, _ = _hg_out(q_dec, k_inv, v, st_ref[0])
        y_ref[...] = _bf(_hg_post(o, og, ng_ref[...]).reshape(cb * HGRN_CHUNK, LANES))

    return pl.pallas_call(
        body, name="hgrn_fwd", grid=(HGRN_HEADS, nt),
        in_specs=[head_cols, head_vec, one_vec],
        out_specs=[col(0), state],
        out_shape=[jax.ShapeDtypeStruct((n_tok, D_MODEL), BF16),
                   jax.ShapeDtypeStruct((HGRN_HEADS, nc, HGRN_DK, HGRN_DK), F32)],
        scratch_shapes=[pltpu.VMEM((HGRN_DK, HGRN_DK), F32)],
        compiler_params=_params(("arbitrary", "arbitrary")),
    )(proj, lb, ng)


def _hgrn_bwd(proj, lb, ng, states, dy, dproj, comm=None):
    n_tok = proj.shape[0]
    nc, cb, nt, col, head_cols, head_vec, one_vec, state = _hgrn_specs(n_tok, True)

    def body(in_ref, lb_ref, ng_ref, st_ref, dy_ref, _, d_ref, dlb_ref, dng_ref, g_acc, g_all):
        h = pl.program_id(0)
        t = pl.program_id(1)

        @pl.when(t == 0)
        def _():
            g_acc[...] = jnp.zeros_like(g_acc)
            dlb_ref[...] = jnp.zeros_like(dlb_ref)

        @pl.when((t == 0) & (h == 0))
        def _():
            dng_ref[...] = jnp.zeros_like(dng_ref)

        fl, qh, v, og = (in_ref[:, i * LANES:(i + 1) * LANES].reshape(cb, HGRN_CHUNK, LANES) for i in range(4))
        (q_dec, k_inv, k_end, decay), elem_vjp = jax.vjp(_hg_elem, fl, qh, lb_ref[...])
        st = st_ref[0]
        o, sc = _hg_out(q_dec, k_inv, v, st)
        _, post_vjp = jax.vjp(_hg_post, o, og, ng_ref[...])
        do, dog, dng = post_vjp(dy_ref[...].reshape(cb, HGRN_CHUNK, LANES))
        dob, vb, qb = _bf(do), _bf(v), _bf(q_dec)
        dsc = _bf(jnp.where(_tril(cb), _bmm_nt(dob, vb), 0.0))
        p = _bmm_tn(dob, qb)
        g = g_acc[...]
        for ci in reversed(range(cb)):
            g_all[ci] = g
            g = g * decay[ci] + p[ci]
        g_acc[...] = g
        g = g_all[...]
        gb = _bf(g)
        dq_dec = _bmm(dsc, _bf(k_inv)) + _bmm(dob, _bf(st))
        dk_inv = _bmm_tn(dsc, qb)
        dv = _bmm_tn(_bf(sc), dob) + _bmm_nt(_bf(k_end), gb)
        dk_end = _bmm(vb, gb)
        ddecay = jnp.sum(st * g, axis=1, keepdims=True)
        dfl, dqh, dlb = elem_vjp((dq_dec, dk_inv, dk_end, ddecay))
        for i, val in enumerate((dfl, dqh, dv, dog)):
            d_ref[:, i * LANES:(i + 1) * LANES] = _bf(val.reshape(cb * HGRN_CHUNK, LANES))
        dlb_ref[...] += dlb
        dng_ref[...] += dng

    return _hosted(
        body, comm, name="hgrn_bwd", grid=(HGRN_HEADS, nt),
        in_specs=[head_cols, head_vec, one_vec, state, col(0), pl.BlockSpec(memory_space=pl.ANY)],
        out_specs=[head_cols, head_vec, one_vec],
        out_shape=[jax.ShapeDtypeStruct(dproj.shape, dproj.dtype),
                   jax.ShapeDtypeStruct((1, D_MODEL), F32), jax.ShapeDtypeStruct((1, LANES), F32)],
        scratch_shapes=[pltpu.VMEM((HGRN_DK, HGRN_DK), F32), pltpu.VMEM((cb, HGRN_DK, HGRN_DK), F32)],
        compiler_params=_params(("arbitrary", "arbitrary")), args=[proj, lb, ng, states, dy, dproj], aliases={5: 0})


def _lb_fwd(lb_logits):
    def lb_of(l0, l1):
        m = jnp.maximum(l0, l1)
        e0, e1 = jnp.exp(l0 - m), jnp.exp(l1 - m)
        return e0 / (e0 + e1)

    def body(l_ref, o_ref):
        o_ref[...] = lb_of(l_ref[0:1, :], l_ref[1:2, :])

    lb = pl.pallas_call(body, name="lb_fwd", out_shape=jax.ShapeDtypeStruct((1, D_MODEL), F32))(lb_logits)
    return lb, lb_of


def _mix_fwd(y_attn, y_hgrn, proj, x1, w_pa, w_ph, w_out, g, b):
    n = x1.shape[0]
    tm = min(ROW_TILE, n)

    def body(ya_ref, yh_ref, gt_ref, x_ref, wpa, wph, wo, g_ref, b_ref, z_ref, o_ref, pa_ref, ph_ref):
        ya = _mm(ya_ref[...], wpa[...])
        yh = _mm(yh_ref[...], wph[...])
        pa_ref[...] = _bf(ya)
        ph_ref[...] = _bf(yh)
        merged = _sig(gt_ref[:, :D_MODEL]) * ya + _sig(gt_ref[:, D_MODEL:]) * yh
        z = DEEPNORM_ALPHA * x_ref[...] + _mm(_bf(merged), wo[...])
        z_ref[...] = z
        o_ref[...] = _ln(z, g_ref[...], b_ref[...])

    row = pl.BlockSpec((tm, D_MODEL), lambda i: (i, 0))
    gates = pl.BlockSpec((tm, GATES_WIDTH), lambda i: (i, 0))
    sq = _resident((D_MODEL, D_MODEL))
    vec = _full((1, D_MODEL))
    return pl.pallas_call(
        body, name="mix_fwd", grid=(n // tm,),
        in_specs=[row, row, gates, row, sq, sq, sq, vec, vec],
        out_specs=[row, row, row, row],
        out_shape=[jax.ShapeDtypeStruct((n, D_MODEL), F32)] * 2 + [jax.ShapeDtypeStruct((n, D_MODEL), BF16)] * 2,
        compiler_params=_params(("arbitrary",)),
    )(y_attn, y_hgrn, proj, x1, w_pa, w_ph, w_out, g, b)


def _mix_bwd(dx2, z2, pa, ph, proj, w_pa, w_ph, w_out, g, b):
    n = z2.shape[0]
    tm = min(MIX_TILE, n)

    def body(do_ref, z_ref, ya_ref, yh_ref, gt_ref, wpa, wph, wo, g_ref, b_ref,
             dz_ref, dzb_ref, mg_ref, dya_ref, dyh_ref, dyat_ref, dyhg_ref, dgt_ref, dg_ref, db_ref):
        _, vjp = jax.vjp(_ln, z_ref[...], g_ref[...], b_ref[...])
        dz, dg, db = vjp(do_ref[...])

        @pl.when(pl.program_id(0) == 0)
        def _():
            dg_ref[...] = jnp.zeros_like(dg_ref)
            db_ref[...] = jnp.zeros_like(db_ref)

        dg_ref[...] += dg
        db_ref[...] += db
        dz_ref[...] = dz
        ya = ya_ref[...].astype(F32)
        yh = yh_ref[...].astype(F32)
        def merge(ga, gh, ya, yh):
            return _sig(ga) * ya + _sig(gh) * yh

        merged, merge_vjp = jax.vjp(merge, gt_ref[:, :D_MODEL], gt_ref[:, D_MODEL:], ya, yh)
        mg_ref[...] = _bf(merged)
        dzb = _bf(dz)
        dzb_ref[...] = dzb
        dmerged = _mm_nt(dzb, wo[...])
        dga, dgh, dya, dyh = merge_vjp(dmerged)
        dya = _bf(dya)
        dyh = _bf(dyh)
        dya_ref[...] = dya
        dyh_ref[...] = dyh
        dgt_ref[:, :D_MODEL] = _bf(dga)
        dgt_ref[:, D_MODEL:] = _bf(dgh)
        dyat_ref[...] = _bf(_mm_nt(dya, wpa[...]))
        dyhg_ref[...] = _mm_nt(dyh, wph[...])

    row = pl.BlockSpec((tm, D_MODEL), lambda i: (i, 0))
    gates = pl.BlockSpec((tm, GATES_WIDTH), lambda i: (i, 0))
    sq = _resident((D_MODEL, D_MODEL))
    vec = _full((1, D_MODEL))
    f32_row = jax.ShapeDtypeStruct((n, D_MODEL), F32)
    bf_row = jax.ShapeDtypeStruct((n, D_MODEL), BF16)
    vec_shape = jax.ShapeDtypeStruct((1, D_MODEL), F32)
    return pl.pallas_call(
        body, name="mix_bwd", grid=(n // tm,),
        in_specs=[row, row, row, row, gates, sq, sq, sq, vec, vec],
        out_specs=[row, row, row, row, row, row, row, gates, vec, vec],
        out_shape=[f32_row, bf_row, bf_row, bf_row, bf_row, bf_row, f32_row,
                   jax.ShapeDtypeStruct((n, D_IN), BF16), vec_shape, vec_shape],
        compiler_params=_params(("arbitrary",)),
    )(dx2, z2, pa, ph, proj, w_pa, w_ph, w_out, g, b)


def _position():
    x, y, c = lax.axis_index("x"), lax.axis_index("y"), lax.axis_index("c")
    chips = [(1 - x, y), (x, 1 - y), (1 - x, 1 - y)]
    return x, y, c, chips


def _any_specs(k):
    return [pl.BlockSpec(memory_space=pl.ANY)] * k


class _GatherWeights:
    def __init__(self, shards):
        nw = len(shards)
        self.inputs = list(shards)
        self.out_shape = [jax.ShapeDtypeStruct((N_CHIPS, *s.shape), s.dtype) for s in shards]
        self.scratch = [pltpu.SemaphoreType.DMA((nw,)), pltpu.SemaphoreType.DMA((nw * 6,)),
                        pltpu.SemaphoreType.DMA((nw * 6,))]

    def _copies(self, ins, outs, sems):
        nw = len(ins)
        local_sem, send_sem, recv_sem = sems
        x, y, c, chips = _position()
        me = 2 * x + y
        sibling = (x, y, 1 - c)
        half_rows = [s.shape[0] // 2 for s in self.inputs]

        def half(w, chip_idx, which):
            return outs[w].at[chip_idx, pl.ds(which * half_rows[w], half_rows[w])]

        def remote(w, k, src, dst, to):
            return pltpu.make_async_remote_copy(src_ref=src, dst_ref=dst, send_sem=send_sem.at[w * 6 + k],
                                                recv_sem=recv_sem.at[w * 6 + k], device_id=to, device_id_type=MESH)

        local = [pltpu.make_async_copy(ins[w], outs[w].at[me], local_sem.at[w]) for w in range(nw)]
        first = [remote(w, j, ins[w].at[pl.ds(c * half_rows[w], half_rows[w])], half(w, me, c), (px, py, c))
                 for w in range(nw) for j, (px, py) in enumerate(chips)]
        landed = [half(w, 2 * px + py, c) for w in range(nw) for (px, py) in chips]
        arrive = [remote(w, j, landed[w * 3 + j], landed[w * 3 + j], (px, py, c))
                  for w in range(nw) for j, (px, py) in enumerate(chips)]
        passed = [remote(w, 3 + j, landed[w * 3 + j], landed[w * 3 + j], sibling) for w in range(nw) for j in range(3)]
        from_sibling = [remote(w, 3 + j, half(w, 2 * px + py, 1 - c), half(w, 2 * px + py, 1 - c), sibling)
                        for w in range(nw) for j, (px, py) in enumerate(chips)]
        return local, first, arrive, passed, from_sibling

    def start(self, ins, outs, sems):
        local, first, _, _, _ = self._copies(ins, outs, sems)
        for cp in local + first:
            cp.start()

    def finish(self, ins, outs, sems):
        local, first, arrive, passed, from_sibling = self._copies(ins, outs, sems)
        for cp_in, cp_on in zip(arrive, passed):
            cp_in.wait_recv()
            cp_on.start()
        for cp in from_sibling:
            cp.wait_recv()
        for cp in first + passed:
            cp.wait_send()
        for cp in local:
            cp.wait()


class _ExchangeGrads:
    def __init__(self, grads, rows=None, into=None):
        nw = len(grads)
        self.rows = [rows or (0, g.shape[1]) for g in grads]
        self.inputs = list(grads) + list(into or [])
        self.aliases = {nw + i: i for i in range(nw)} if into else {}
        self.out_shape = [jax.ShapeDtypeStruct(g.shape, g.dtype) for g in grads]
        self.scratch = [pltpu.SemaphoreType.DMA((nw,)), pltpu.SemaphoreType.DMA((nw * 3,)),
                        pltpu.SemaphoreType.DMA((nw * 3,))]

    def _copies(self, ins, outs, sems):
        nw = len(outs)
        local_sem, send_sem, recv_sem = sems
        x, y, c, chips = _position()
        me = 2 * x + y
        rows = [pl.ds(*r) for r in self.rows]

        def remote(w, j, src, dst, chip):
            return pltpu.make_async_remote_copy(src_ref=src, dst_ref=dst, send_sem=send_sem.at[w * 3 + j],
                                                recv_sem=recv_sem.at[w * 3 + j], device_id=(*chip, c),
                                                device_id_type=MESH)

        local = [pltpu.make_async_copy(ins[w].at[me, rows[w]], outs[w].at[me, rows[w]], local_sem.at[w])
                 for w in range(nw)]
        sends = [remote(w, j, ins[w].at[2 * px + py, rows[w]], outs[w].at[me, rows[w]], (px, py))
                 for w in range(nw) for j, (px, py) in enumerate(chips)]
        arrive = [remote(w, j, outs[w].at[2 * px + py, rows[w]], outs[w].at[2 * px + py, rows[w]], (px, py))
                  for w in range(nw) for j, (px, py) in enumerate(chips)]
        return local, sends, arrive

    def start(self, ins, outs, sems):
        local, sends, _ = self._copies(ins, outs, sems)
        for cp in local + sends:
            cp.start()

    def finish(self, ins, outs, sems):
        local, sends, arrive = self._copies(ins, outs, sems)
        for cp in arrive:
            cp.wait_recv()
        for cp in sends:
            cp.wait_send()
        for cp in local:
            cp.wait()


def _hosted(body, comm, *, name, grid, in_specs, out_specs, out_shape, scratch_shapes, compiler_params, args,
            aliases=None):
    aliases = aliases or {}
    if comm is None:
        res = pl.pallas_call(body, name=name, grid=grid, in_specs=in_specs, out_specs=out_specs, out_shape=out_shape,
                             scratch_shapes=scratch_shapes, compiler_params=compiler_params,
                             input_output_aliases=aliases)(*args)
        return list(res), []
    n_in, n_out, n_scr = len(in_specs), len(out_specs), len(scratch_shapes)
    c_in, c_out = len(comm.inputs), len(comm.out_shape)
    aliases = {**aliases, **{n_in + i: n_out + o for i, o in getattr(comm, "aliases", {}).items()}}

    def hosted_body(*refs):
        refs = list(refs)
        cut = lambda k: (refs[:k], refs[k:])
        main_in, refs = cut(n_in)
        comm_in, refs = cut(c_in)
        main_out, refs = cut(n_out)
        comm_out, refs = cut(c_out)
        main_scr, comm_scr = cut(n_scr)
        ids = [pl.program_id(a) for a in range(len(grid))]
        first = functools.reduce(jnp.logical_and, [i == 0 for i in ids])
        last = functools.reduce(jnp.logical_and, [i == g - 1 for i, g in zip(ids, grid)])

        @pl.when(first)
        def _():
            comm.start(comm_in, comm_out, comm_scr)

        body(*main_in, *main_out, *main_scr)

        @pl.when(last)
        def _():
            comm.finish(comm_in, comm_out, comm_scr)

    res = pl.pallas_call(
        hosted_body, name=name, grid=grid, in_specs=[*in_specs, *_any_specs(c_in)],
        out_specs=[*out_specs, *_any_specs(c_out)], out_shape=[*out_shape, *comm.out_shape],
        scratch_shapes=[*scratch_shapes, *comm.scratch], compiler_params=compiler_params,
        input_output_aliases=aliases,
    )(*args, *comm.inputs)
    return list(res[:n_out]), list(res[n_out:])


def _run_comm(name, comm):
    def body(*refs):
        refs = list(refs)
        c_in, c_out = len(comm.inputs), len(comm.out_shape)
        ins, outs, sems = refs[:c_in], refs[c_in:c_in + c_out], refs[c_in + c_out:]
        comm.start(ins, outs, sems)
        comm.finish(ins, outs, sems)

    return list(pl.pallas_call(
        body, name=name, in_specs=_any_specs(len(comm.inputs)), out_specs=_any_specs(len(comm.out_shape)),
        out_shape=comm.out_shape, scratch_shapes=comm.scratch,
    )(*comm.inputs))


def _sum_slots(name, slots):
    k = len(slots)
    _, rows, cols = slots[0].shape
    tr = _update_rows(rows)

    def body(*refs):
        for s_ref, o_ref in zip(refs[:k], refs[k:]):
            acc = s_ref[0].astype(F32)
            for i in range(1, N_CHIPS):
                acc = acc + s_ref[i].astype(F32)
            o_ref[...] = acc

    return pl.pallas_call(
        body, name=name, grid=(rows // tr,),
        in_specs=[pl.BlockSpec((N_CHIPS, tr, cols), lambda i: (0, i, 0))] * k,
        out_specs=[pl.BlockSpec((tr, cols), lambda i: (i, 0))] * k,
        out_shape=[jax.ShapeDtypeStruct((rows, cols), F32)] * k,
        compiler_params=_params(("arbitrary",)),
    )(*slots)


class _SwapWithSibling:
    def __init__(self, parts):
        self.inputs = list(parts)
        self.out_shape = [jax.ShapeDtypeStruct(p.shape, p.dtype) for p in parts]
        self.scratch = [pltpu.SemaphoreType.DMA((len(parts),)), pltpu.SemaphoreType.DMA((len(parts),))]

    def _copies(self, ins, outs, sems):
        send_sem, recv_sem = sems
        x, y, c, _ = _position()
        return [pltpu.make_async_remote_copy(src_ref=ins[w], dst_ref=outs[w], send_sem=send_sem.at[w],
                                             recv_sem=recv_sem.at[w], device_id=(x, y, 1 - c), device_id_type=MESH)
                for w in range(len(ins))]

    def start(self, ins, outs, sems):
        for cp in self._copies(ins, outs, sems):
            cp.start()

    def finish(self, ins, outs, sems):
        for cp in self._copies(ins, outs, sems):
            cp.wait()


class _Together:
    def __init__(self, first, second):
        self.parts = (first, second)
        self.inputs = first.inputs + second.inputs
        self.out_shape = first.out_shape + second.out_shape
        self.scratch = first.scratch + second.scratch

    def _split(self, ins, outs, sems):
        a = self.parts[0]
        ni, no, ns = len(a.inputs), len(a.out_shape), len(a.scratch)
        return ((ins[:ni], outs[:no], sems[:ns]), (ins[ni:], outs[no:], sems[ns:]))

    def start(self, ins, outs, sems):
        for part, args in zip(self.parts, self._split(ins, outs, sems)):
            part.start(*args)

    def finish(self, ins, outs, sems):
        for part, args in zip(self.parts, self._split(ins, outs, sems)):
            part.finish(*args)


def _sum_small(part):
    def body(p_ref, o_ref, buf, send_sem, recv_sem):
        x, y, c, _ = _position()
        me = 4 * x + 2 * y + c
        buf[me] = p_ref[...]
        copies = []
        for k in range(1, N_DEV):
            peer = tuple(1 - v if (k >> s) & 1 else v for v, s in ((x, 2), (y, 1), (c, 0)))
            copies.append(pltpu.make_async_remote_copy(src_ref=p_ref, dst_ref=buf.at[me], send_sem=send_sem.at[k - 1],
                                                       recv_sem=recv_sem.at[k - 1], device_id=peer, device_id_type=MESH))
        for cp in copies:
            cp.start()
        for cp in copies:
            cp.wait()
        acc = buf[0]
        for d in range(1, N_DEV):
            acc = acc + buf[d]
        o_ref[...] = acc

    vm = pl.BlockSpec(memory_space=pltpu.VMEM)
    return pl.pallas_call(
        body, name="sum_small", in_specs=[vm], out_specs=vm,
        out_shape=jax.ShapeDtypeStruct((1, SM_LEN), F32),
        scratch_shapes=[pltpu.VMEM((N_DEV, 1, SM_LEN), F32), pltpu.SemaphoreType.DMA((N_DEV - 1,)),
                        pltpu.SemaphoreType.DMA((N_DEV - 1,))],
    )(part)


def _adamw(w, g, m, v):
    m = ADAM_B1 * m + (1.0 - ADAM_B1) * g
    v = ADAM_B2 * v + (1.0 - ADAM_B2) * (g * g)
    m_hat = m / (1.0 - ADAM_B1 ** ADAM_STEP)
    v_hat = v / (1.0 - ADAM_B2 ** ADAM_STEP)
    delta = -ADAM_LR * (m_hat / (jnp.sqrt(v_hat) + ADAM_EPS) + ADAM_WD * w)
    return delta, m, v


def _adam_big(name, groups):
    k = len(groups)
    rows, cols = groups[0][2].shape
    tr = _update_rows(rows)

    def body(*refs):
        for i in range(k):
            p_ref, q_ref, w_ref, m_ref, v_ref = refs[5 * i:5 * i + 5]
            g_ref, d_ref, nm_ref, nv_ref = refs[5 * k + 4 * i:5 * k + 4 * i + 4]
            g = p_ref[...] + q_ref[...]
            g_ref[...] = g
            d_ref[...], nm_ref[...], nv_ref[...] = _adamw(w_ref[...], g, m_ref[...], v_ref[...])

    spec = pl.BlockSpec((tr, cols), lambda i: (i, 0))
    res = pl.pallas_call(
        body, name=name, grid=(rows // tr,), in_specs=[spec] * (5 * k), out_specs=[spec] * (4 * k),
        out_shape=[jax.ShapeDtypeStruct((rows, cols), F32)] * (4 * k),
        compiler_params=_params(("arbitrary",)),
    )(*[a for grp in groups for a in grp])
    return [res[4 * i:4 * i + 4] for i in range(k)]


_SMALL_AT = {"ln1_g": 0, "ln1_b": D_MODEL, "ln2_g": 2 * D_MODEL, "ln2_b": 3 * D_MODEL, "ln3_g": 4 * D_MODEL,
             "ln3_b": 5 * D_MODEL, "b_in": SM_BIN, "attn_sinks": SM_SINK, "hgrn_norm_g": SM_NG}


def _adam_small(total, w, m, v, lb_of):
    names = list(_SMALL)
    k = len(names)

    def body(*refs):
        t_ref = refs[0]
        w_refs, m_refs, v_refs = (refs[1 + i * k:1 + (i + 1) * k] for i in range(3))
        g_refs, d_refs, nm_refs, nv_refs = (refs[1 + (3 + i) * k:1 + (4 + i) * k] for i in range(4))
        for i, name in enumerate(names):
            if name == "hgrn_lb_logits":
                _, vjp = jax.vjp(lb_of, w_refs[i][0:1, :], w_refs[i][1:2, :])
                g_refs[i][0:1, :], g_refs[i][1:2, :] = vjp(t_ref[:, SM_LB:SM_LOSS])
            else:
                at = _SMALL_AT[name]
                g_refs[i][...] = t_ref[:, at:at + w_refs[i].shape[1]]
            d_refs[i][...], nm_refs[i][...], nv_refs[i][...] = _adamw(
                w_refs[i][...], g_refs[i][...], m_refs[i][...], v_refs[i][...])

    shapes = [jax.ShapeDtypeStruct(w[name].shape, F32) for name in names]
    res = pl.pallas_call(body, name="adam_small", out_shape=shapes * 4)(
        total, *[w[n] for n in names], *[m[n] for n in names], *[v[n] for n in names])
    return [dict(zip(names, res[i * k:(i + 1) * k])) for i in range(4)]


_BIG = ("ffn1_w1", "ffn1_w3", "ffn1_w2", "w_in", "w_proj_attn", "w_proj_hgrn", "w_out", "ffn2_w1", "ffn2_w3", "ffn2_w2")
_SMALL = ("ln1_g", "ln1_b", "ln2_g", "ln2_b", "ln3_g", "ln3_b", "b_in", "attn_sinks", "hgrn_norm_g", "hgrn_lb_logits")
_ORDER = ("ln1_g", "ln1_b", "ffn1_w1", "ffn1_w3", "ffn1_w2", "ln2_g", "ln2_b", "w_in", "b_in", "attn_sinks",
          "hgrn_lb_logits", "hgrn_norm_g", "w_proj_attn", "w_proj_hgrn", "w_out", "ln3_g", "ln3_b",
          "ffn2_w1", "ffn2_w3", "ffn2_w2")


_TRANSPOSED = ("ffn1_w1", "ffn1_w3", "ffn2_w1", "ffn2_w3")


def _local_view(name, arr):
    return arr[0].T if name in _TRANSPOSED else arr[0]


def _ffn_grad(name, hidden, other, comm=None):
    (dw,), comm_out = _grad_matmul(name, hidden, "cols", D_FF // FF_GRAD_PARTS, other, "shared", D_MODEL,
                                   comm=comm, parts=FF_GRAD_PARTS)
    return dw.reshape(N_CHIPS, D_FF // N_CHIPS, D_MODEL), comm_out


def kernel(x, ln1_g, ln1_b, ffn1_w1, ffn1_w3, ffn1_w2, ln2_g, ln2_b, w_in, b_in, attn_sinks, hgrn_lb_logits, hgrn_norm_g, w_proj_attn, w_proj_hgrn, w_out, ln3_g, ln3_b, ffn2_w1, ffn2_w3, ffn2_w2, loss_target, m_ln1_g, m_ln1_b, m_ffn1_w1, m_ffn1_w3, m_ffn1_w2, m_ln2_g, m_ln2_b, m_w_in, m_b_in, m_attn_sinks, m_hgrn_lb_logits, m_hgrn_norm_g, m_w_proj_attn, m_w_proj_hgrn, m_w_out, m_ln3_g, m_ln3_b, m_ffn2_w1, m_ffn2_w3, m_ffn2_w2, v_ln1_g, v_ln1_b, v_ffn1_w1, v_ffn1_w3, v_ffn1_w2, v_ln2_g, v_ln2_b, v_w_in, v_b_in, v_attn_sinks, v_hgrn_lb_logits, v_hgrn_norm_g, v_w_proj_attn, v_w_proj_hgrn, v_w_out, v_ln3_g, v_ln3_b, v_ffn2_w1, v_ffn2_w3, v_ffn2_w2):
    w = dict(ln1_g=ln1_g, ln1_b=ln1_b, ffn1_w1=ffn1_w1, ffn1_w3=ffn1_w3, ffn1_w2=ffn1_w2, ln2_g=ln2_g, ln2_b=ln2_b,
             w_in=w_in, b_in=b_in, attn_sinks=attn_sinks, hgrn_lb_logits=hgrn_lb_logits, hgrn_norm_g=hgrn_norm_g,
             w_proj_attn=w_proj_attn, w_proj_hgrn=w_proj_hgrn, w_out=w_out, ln3_g=ln3_g, ln3_b=ln3_b,
             ffn2_w1=ffn2_w1, ffn2_w3=ffn2_w3, ffn2_w2=ffn2_w2)
    mom = dict(ln1_g=m_ln1_g, ln1_b=m_ln1_b, ffn1_w1=m_ffn1_w1, ffn1_w3=m_ffn1_w3, ffn1_w2=m_ffn1_w2, ln2_g=m_ln2_g,
               ln2_b=m_ln2_b, w_in=m_w_in, b_in=m_b_in, attn_sinks=m_attn_sinks, hgrn_lb_logits=m_hgrn_lb_logits,
               hgrn_norm_g=m_hgrn_norm_g, w_proj_attn=m_w_proj_attn, w_proj_hgrn=m_w_proj_hgrn, w_out=m_w_out,
               ln3_g=m_ln3_g, ln3_b=m_ln3_b, ffn2_w1=m_ffn2_w1, ffn2_w3=m_ffn2_w3, ffn2_w2=m_ffn2_w2)
    var = dict(ln1_g=v_ln1_g, ln1_b=v_ln1_b, ffn1_w1=v_ffn1_w1, ffn1_w3=v_ffn1_w3, ffn1_w2=v_ffn1_w2, ln2_g=v_ln2_g,
               ln2_b=v_ln2_b, w_in=v_w_in, b_in=v_b_in, attn_sinks=v_attn_sinks, hgrn_lb_logits=v_hgrn_lb_logits,
               hgrn_norm_g=v_hgrn_norm_g, w_proj_attn=v_w_proj_attn, w_proj_hgrn=v_w_proj_hgrn, w_out=v_w_out,
               ln3_g=v_ln3_g, ln3_b=v_ln3_b, ffn2_w1=v_ffn2_w1, ffn2_w3=v_ffn2_w3, ffn2_w2=v_ffn2_w2)

    n_tok = x.shape[1]
    x0 = x.reshape(n_tok, D_MODEL)
    target = loss_target.reshape(n_tok, D_MODEL)

    shard = {k: _bf(_local_view(k, w[k])) for k in _BIG}
    gather = lambda keys: _GatherWeights([shard[k] for k in keys])
    slots = {}
    exchange = lambda keys: _ExchangeGrads([big[k] for k in keys])
    ffn1_keys = ("ffn1_w1", "ffn1_w3", "ffn1_w2")
    mixer_keys = ("w_in", "w_proj_attn", "w_proj_hgrn", "w_out")
    ffn2_keys = ("ffn2_w1", "ffn2_w3", "ffn2_w2")
    whole = lambda ts: [t.reshape(D_FF, D_MODEL) for t in ts]
    f1 = whole(_run_comm("gather_ffn1", gather(ffn1_keys)))

    tabs = _rope_tables(n_tok)
    lb, lb_of = _lb_fwd(hgrn_lb_logits)
    (z1, x1, x0b, h1_1, h3_1, x1b), (w_in_g, w_pa, w_ph, w_o) = _ffn_fwd(
        "ffn1_fwd", x0, *f1, ln1_g, ln1_b, comm=gather(mixer_keys))
    w_pa, w_ph, w_o = (t.reshape(D_MODEL, D_MODEL) for t in (w_pa, w_ph, w_o))
    w_in_g = _reorder_w_in("w_in_cols", w_in_g, True)
    (proj,), f2 = _in_proj(x1b, w_in_g, _to_kernel_cols(b_in), comm=gather(ffn2_keys))
    f2 = whole(f2)
    y_attn = _attn_fwd(proj, tabs, attn_sinks)
    y_hgrn, states = _hgrn_fwd(proj, lb, hgrn_norm_g)
    z2, x2, proj_a, proj_h = _mix_fwd(y_attn, y_hgrn, proj, x1, w_pa, w_ph, w_o, ln2_g, ln2_b)
    (z3, dy, x2b, h1_2, h3_2, loss_part), _ = _ffn_fwd("ffn2_fwd", x2, *f2, ln3_g, ln3_b, target=target)

    big = {}
    small = {}
    (dx2, a2, dh1_2, dh3_2, df2, small["ln3_g"], small["ln3_b"]), _ = _ffn_bwd(
        "ffn2_bwd", h1_2, h3_2, z3, dy, *f2, ln3_g, ln3_b)
    big["ffn2_w1"], _ = _ffn_grad("ffn2_dw1", dh1_2, x2b)
    big["ffn2_w3"], _ = _ffn_grad("ffn2_dw3", dh3_2, x2b)
    big["ffn2_w2"], _ = _ffn_grad("ffn2_dw2", a2, df2)
    (dz2, dz2b, merged, dya, dyh, dy_attn, dy_hgrn, dproj, small["ln2_g"], small["ln2_b"]) = _mix_bwd(
        dx2, z2, proj_a, proj_h, proj, w_pa, w_ph, w_o, ln2_g, ln2_b)
    for key, name, lhs, rhs in (("w_out", "dw_out", merged, dz2b), ("w_proj_attn", "dw_proj_attn", y_attn, dya),
                                ("w_proj_hgrn", "dw_proj_hgrn", y_hgrn, dyh)):
        (dw,), _ = _grad_matmul(name, lhs, "cols", D_MODEL, rhs, "shared", D_MODEL, parts=1)
        big[key] = dw.reshape(N_CHIPS, PROJ_SHARD, D_MODEL)
    (dproj, dlb, small["hgrn_norm_g"]), early = _hgrn_bwd(
        proj, lb, hgrn_norm_g, states, dy_hgrn, dproj, comm=exchange(ffn2_keys[:2]))
    slots.update(zip(ffn2_keys[:2], early))
    attn_hosted = ffn2_keys[2:] + mixer_keys[1:]
    (dproj, dsinks), early = _attn_bwd(proj, dy_attn, tabs, attn_sinks, dproj, comm=exchange(attn_hosted))
    slots.update(zip(attn_hosted, early))
    (dw_in, db_in), _ = _grad_matmul("dw_in", x1b, "shared", D_MODEL, dproj, "cols", IN_SHARD, colsum=True)
    big["w_in"] = _reorder_w_in("dw_in_cols", dw_in, False)
    small["b_in"] = _from_kernel_cols(db_in)
    (dx1,), (w_in_slots,) = _in_proj_dx(
        dproj, w_in_g, dz2, comm=_ExchangeGrads([big["w_in"]], rows=(0, W_IN_ROWS_FIRST)))
    (grad_x, a1, dh1_1, dh3_1, df1, small["ln1_g"], small["ln1_b"]), _ = _ffn_bwd(
        "ffn1_bwd", h1_1, h3_1, z1, dx1, *f1, ln1_g, ln1_b)
    big["ffn1_w1"], (slots["w_in"],) = _ffn_grad("ffn1_dw1", dh1_1, x0b, comm=_ExchangeGrads(
        [big["w_in"]], rows=(W_IN_ROWS_FIRST, D_MODEL - W_IN_ROWS_FIRST), into=[w_in_slots]))
    big["ffn1_w3"], (slots["ffn1_w1"],) = _ffn_grad("ffn1_dw3", dh3_1, x0b, comm=exchange(("ffn1_w1",)))
    big["ffn1_w2"], (slots["ffn1_w3"],) = _ffn_grad("ffn1_dw2", a1, df1, comm=exchange(("ffn1_w3",)))

    last = "ffn1_w2"
    groups = [[k for k in ffn1_keys + ffn2_keys if k != last], list(mixer_keys[1:]), ["w_in"]]
    partial = {}
    for keys in groups:
        partial.update(zip(keys, _sum_slots("sum_" + keys[0], [slots[k] for k in keys])))
    swapped_keys = [k for keys in groups for k in keys]
    moved = _run_comm("swap_and_exchange_last",
                      _Together(_SwapWithSibling([partial[k] for k in swapped_keys]), exchange((last,))))
    from_sibling = dict(zip(swapped_keys, moved))
    (partial[last],) = _sum_slots("sum_" + last, [moved[-1]])
    (from_sibling[last],) = _run_comm("swap_last", _SwapWithSibling([partial[last]]))
    groups[0].append(last)

    outs = {"grad": {}, "delta": {}, "m": {}, "v": {}}
    for keys in groups:
        res = _adam_big("adam_" + keys[0], [(partial[k], from_sibling[k], _local_view(k, w[k]), _local_view(k, mom[k]),
                                             _local_view(k, var[k])) for k in keys])
        for k, four in zip(keys, res):
            for kind, r in zip(("grad", "delta", "m", "v"), four):
                outs[kind][k] = (r.T if k in _TRANSPOSED else r).reshape(w[k].shape)

    total = _sum_small(jnp.concatenate(
        [small[k] for k in ("ln1_g", "ln1_b", "ln2_g", "ln2_b", "ln3_g", "ln3_b", "b_in")]
        + [dsinks, small["hgrn_norm_g"], dlb, loss_part], axis=1))
    for kind, r in zip(("grad", "delta", "m", "v"), _adam_small(total, w, mom, var, lb_of)):
        outs[kind].update(r)
    loss = total[0, SM_LOSS]

    return (loss, grad_x.reshape(x.shape), *[outs["grad"][k] for k in _ORDER], *[outs["delta"][k] for k in _ORDER],
            *[outs["m"][k] for k in _ORDER], *[outs["v"][k] for k in _ORDER])
```

```python
import functools

import jax
import jax.numpy as jnp
from jax import lax
from jax.experimental import pallas as pl
from jax.experimental.pallas import tpu as pltpu

F32 = jnp.float32
BF16 = jnp.bfloat16

D_MODEL = 1024
N_Q_HEADS = 16
N_KV_HEADS = 4
HEAD_DIM = 64
ATTN_BLOCK = 128
ROPE_THETA = 500000.0
ROPE_DIM = HEAD_DIM // 4
HGRN_HEADS = 8
HGRN_DK = 128
HGRN_CHUNK = 64
D_FF = 2816
D_IN = 7680
DEEPNORM_ALPHA = 2 ** 0.25
LN_EPS = 1e-5
RMS_EPS = 1e-6
NEG_INF = -1e30

ADAM_LR = 0.001
ADAM_B1 = 0.9
ADAM_B2 = 0.999
ADAM_EPS = 1e-08
ADAM_WD = 0.01
ADAM_STEP = 10

N_CHIPS = 4
N_DEV = 8
LANES = 128
FF_GRAD_PARTS = 2
FFN_TILE = 256
IN_SHARD = D_IN // N_CHIPS
PROJ_SHARD = D_MODEL // N_CHIPS
ROW_TILE = 512
GRAD_ROWS = 2048
MIX_TILE = 256
UPDATE_ROWS = 128
HGRN_CHUNKS_PER_STEP = 32
W_IN_ROWS_FIRST = 640
VMEM_LIMIT = 56 * 1024 * 1024

GATES_WIDTH = 2 * D_MODEL
HGRN_HEAD_WIDTH = 4 * HGRN_DK
ATTN_WIDTH = D_MODEL + 2 * N_KV_HEADS * HEAD_DIM
COL_HGRN = GATES_WIDTH // HGRN_HEAD_WIDTH
COL_ATTN = (GATES_WIDTH + HGRN_HEADS * HGRN_HEAD_WIDTH) // ATTN_WIDTH
COL_Q = (GATES_WIDTH + HGRN_HEADS * HGRN_HEAD_WIDTH) // D_MODEL
COL_K = (GATES_WIDTH + HGRN_HEADS * HGRN_HEAD_WIDTH + D_MODEL) // (N_KV_HEADS * HEAD_DIM)
COL_V = COL_K + 1


def _to_kernel_cols(a):
    lead = a.shape[:-1]
    qkv, hg, gates = a[..., :ATTN_WIDTH], a[..., ATTN_WIDTH:D_IN - GATES_WIDTH], a[..., D_IN - GATES_WIDTH:]
    hg = jnp.swapaxes(hg.reshape(*lead, 4, HGRN_HEADS, HGRN_DK), -3, -2).reshape(*lead, -1)
    return jnp.concatenate([gates, hg, qkv], axis=-1)


def _from_kernel_cols(a):
    lead = a.shape[:-1]
    gates, hg, qkv = a[..., :GATES_WIDTH], a[..., GATES_WIDTH:D_IN - ATTN_WIDTH], a[..., D_IN - ATTN_WIDTH:]
    hg = jnp.swapaxes(hg.reshape(*lead, HGRN_HEADS, 4, HGRN_DK), -3, -2).reshape(*lead, -1)
    return jnp.concatenate([qkv, hg, gates], axis=-1)

SM_LN = 0
SM_BIN = 6 * D_MODEL
SM_SINK = SM_BIN + D_IN
SM_NG = SM_SINK + LANES
SM_LB = SM_NG + LANES
SM_LOSS = SM_LB + D_MODEL
SM_LEN = SM_LOSS + LANES

MESH = pl.DeviceIdType.MESH


def _mm(a, b):
    return lax.dot_general(a, b, (((1,), (0,)), ((), ())), preferred_element_type=F32)


def _mm_nt(a, b):
    return lax.dot_general(a, b, (((1,), (1,)), ((), ())), preferred_element_type=F32)


def _mm_tn(a, b):
    return lax.dot_general(a, b, (((0,), (0,)), ((), ())), preferred_element_type=F32)


def _bf(v):
    return v.astype(BF16)


def _sig(v):
    return jax.nn.sigmoid(v)


def _ln(z, g, b):
    mu = jnp.mean(z, axis=-1, keepdims=True)
    zc = z - mu
    var = jnp.mean(zc * zc, axis=-1, keepdims=True)
    return zc * lax.rsqrt(var + LN_EPS) * g + b


def _swiglu_act(h1, h3):
    return (h1 * _sig(h1)) * h3


def _params(sem=None):
    return pltpu.CompilerParams(dimension_semantics=sem, vmem_limit_bytes=VMEM_LIMIT)


def _full(shape):
    nd = len(shape)
    return pl.BlockSpec(shape, lambda *_: (0,) * nd)


def _update_rows(rows):
    return max(t for t in range(8, UPDATE_ROWS + 1, 8) if rows % t == 0)


def _resident(shape):
    nd = len(shape)
    return pl.BlockSpec(shape, lambda *_: (0,) * nd, pipeline_mode=pl.Buffered(1))


def _ffn_fwd(name, x, w1t, w3t, w2, g, b, target=None, comm=None):
    n = x.shape[0]
    tm = min(ROW_TILE, n)
    final = target is not None

    def body(*refs):
        if final:
            x_ref, w1_ref, w3_ref, w2_ref, g_ref, b_ref, t_ref, z_ref, o_ref, xb_ref, h1_ref, h3_ref, loss_ref = refs
        else:
            x_ref, w1_ref, w3_ref, w2_ref, g_ref, b_ref, z_ref, o_ref, xb_ref, h1_ref, h3_ref, ob_ref = refs
        xb = _bf(x_ref[...])
        xb_ref[...] = xb
        h1 = _mm_nt(xb, w1_ref[...])
        h3 = _mm_nt(xb, w3_ref[...])
        h1_ref[...] = _bf(h1)
        h3_ref[...] = _bf(h3)
        z = DEEPNORM_ALPHA * x_ref[...] + 0.5 * _mm(_bf(_swiglu_act(h1, h3)), w2_ref[...])
        z_ref[...] = z
        y = _ln(z, g_ref[...], b_ref[...])
        if final:
            e = y - t_ref[...]

            @pl.when(pl.program_id(0) == 0)
            def _():
                loss_ref[...] = jnp.zeros_like(loss_ref)

            loss_ref[...] += jnp.sum(e * e) * (0.5 / D_MODEL)
            o_ref[...] = e * (1.0 / D_MODEL)
        else:
            o_ref[...] = y
            ob_ref[...] = _bf(y)

    row = pl.BlockSpec((tm, D_MODEL), lambda i: (i, 0))
    hid = pl.BlockSpec((tm, D_FF), lambda i: (i, 0))
    wres = _resident((D_FF, D_MODEL))
    vec = _full((1, D_MODEL))
    in_specs = [row, wres, wres, wres, vec, vec]
    args = [x, w1t, w3t, w2, g, b]
    hid_shape = jax.ShapeDtypeStruct((n, D_FF), BF16)
    out_specs = [row, row, row, hid, hid]
    out_shape = ([jax.ShapeDtypeStruct((n, D_MODEL), F32)] * 2 + [jax.ShapeDtypeStruct((n, D_MODEL), BF16)]
                 + [hid_shape] * 2)
    if final:
        in_specs.append(row)
        args.append(target)
        out_specs.append(_full((1, LANES)))
        out_shape.append(jax.ShapeDtypeStruct((1, LANES), F32))
    else:
        out_specs.append(row)
        out_shape.append(jax.ShapeDtypeStruct((n, D_MODEL), BF16))
    return _hosted(
        body, comm, name=name, grid=(n // tm,), in_specs=in_specs, out_specs=out_specs, out_shape=out_shape,
        scratch_shapes=[], compiler_params=_params(("arbitrary",)), args=args)


def _ffn_bwd(name, h1s, h3s, z, dout, w1t, w3t, w2, g, b, comm=None):
    n = z.shape[0]
    tm = min(FFN_TILE, n)

    def body(h1_ref, h3_ref, z_ref, do_ref, w1_ref, w3_ref, w2_ref, g_ref, b_ref,
             dx_ref, a_ref, dh1_ref, dh3_ref, df_ref, dg_ref, db_ref):
        _, vjp = jax.vjp(_ln, z_ref[...], g_ref[...], b_ref[...])
        dz, dg, db = vjp(do_ref[...])

        @pl.when(pl.program_id(0) == 0)
        def _():
            dg_ref[...] = jnp.zeros_like(dg_ref)
            db_ref[...] = jnp.zeros_like(db_ref)

        dg_ref[...] += dg
        db_ref[...] += db
        df = _bf(0.5 * dz)
        df_ref[...] = df
        a, act_vjp = jax.vjp(_swiglu_act, h1_ref[...].astype(F32), h3_ref[...].astype(F32))
        dh1, dh3 = act_vjp(_mm_nt(df, w2_ref[...]))
        dh1 = _bf(dh1)
        dh3 = _bf(dh3)
        a_ref[...] = _bf(a)
        dh1_ref[...] = dh1
        dh3_ref[...] = dh3
        dx_ref[...] = DEEPNORM_ALPHA * dz + _mm(dh1, w1_ref[...]) + _mm(dh3, w3_ref[...])

    row = pl.BlockSpec((tm, D_MODEL), lambda i: (i, 0))
    hid = pl.BlockSpec((tm, D_FF), lambda i: (i, 0))
    wres = _resident((D_FF, D_MODEL))
    vec = _full((1, D_MODEL))
    hid_shape = jax.ShapeDtypeStruct((n, D_FF), BF16)
    return _hosted(
        body, comm, name=name, grid=(n // tm,),
        in_specs=[hid, hid, row, row, wres, wres, wres, vec, vec],
        out_specs=[row, hid, hid, hid, row, vec, vec],
        out_shape=[jax.ShapeDtypeStruct((n, D_MODEL), F32), hid_shape, hid_shape, hid_shape,
                   jax.ShapeDtypeStruct((n, D_MODEL), BF16),
                   jax.ShapeDtypeStruct((1, D_MODEL), F32), jax.ShapeDtypeStruct((1, D_MODEL), F32)],
        scratch_shapes=[], compiler_params=_params(("arbitrary",)),
        args=[h1s, h3s, z, dout, w1t, w3t, w2, g, b])


def _operand_spec(arr, mode, tn, width, parts):
    if mode == "shared":
        return pl.BlockSpec((tn, width), lambda s, k: (k, 0))
    assert mode == "cols" and arr.shape[1] == parts * width
    return pl.BlockSpec((tn, width), lambda s, k: (k, s))


def _grad_matmul(name, a, a_mode, ka, b, b_mode, kb, colsum=False, comm=None, parts=N_CHIPS, rows=GRAD_ROWS):
    n = a.shape[-2]
    tn = min(rows, n)
    nk = n // tn

    def body(*refs):
        if colsum:
            a_ref, b_ref, o_ref, cs_ref, acc = refs
        else:
            a_ref, b_ref, o_ref, acc = refs
        k = pl.program_id(1)
        av = a_ref[...]
        bv = b_ref[...]

        @pl.when(k == 0)
        def _():
            acc[...] = jnp.zeros_like(acc)
            if colsum:
                cs_ref[...] = jnp.zeros_like(cs_ref)

        acc[...] += _mm_tn(av, bv)
        if colsum:
            cs_ref[...] += jnp.sum(bv.astype(F32), axis=0, keepdims=True)

        @pl.when(k == nk - 1)
        def _():
            o_ref[0] = _bf(acc[...])

    out_specs = [pl.BlockSpec((1, ka, kb), lambda s, k: (s, 0, 0))]
    out_shape = [jax.ShapeDtypeStruct((parts, ka, kb), BF16)]
    if colsum:
        out_specs.append(pl.BlockSpec((1, kb), lambda s, k: (0, s)))
        out_shape.append(jax.ShapeDtypeStruct((1, parts * kb), F32))
    return _hosted(
        body, comm, name=name, grid=(parts, nk),
        in_specs=[_operand_spec(a, a_mode, tn, ka, parts), _operand_spec(b, b_mode, tn, kb, parts)],
        out_specs=out_specs, out_shape=out_shape,
        scratch_shapes=[pltpu.VMEM((ka, kb), F32)],
        compiler_params=_params(("arbitrary", "arbitrary")), args=[a, b])


def _kernel_block_of(ref_block):
    attn_blocks, gate_blocks = ATTN_WIDTH // LANES, GATES_WIDTH // LANES
    hgrn_blocks = HGRN_HEADS * HGRN_HEAD_WIDTH // LANES
    if ref_block < attn_blocks:
        return gate_blocks + hgrn_blocks + ref_block
    if ref_block < attn_blocks + hgrn_blocks:
        kind, head = divmod(ref_block - attn_blocks, HGRN_HEADS)
        return gate_blocks + head * (HGRN_HEAD_WIDTH // LANES) + kind
    return ref_block - attn_blocks - hgrn_blocks


def _reorder_w_in(name, w4, to_kernel_order):
    per = IN_SHARD // LANES
    tr = MIX_TILE

    def body(i_ref, o_ref):
        for g in range(D_IN // LANES):
            s, b = divmod(g, per)
            k = _kernel_block_of(g)
            if to_kernel_order:
                o_ref[:, k * LANES:(k + 1) * LANES] = i_ref[s, :, b * LANES:(b + 1) * LANES]
            else:
                ks, kb = divmod(k, per)
                o_ref[s, :, b * LANES:(b + 1) * LANES] = i_ref[ks, :, kb * LANES:(kb + 1) * LANES]

    in_spec = pl.BlockSpec((N_CHIPS, tr, IN_SHARD), lambda i: (0, i, 0))
    if to_kernel_order:
        out_spec, out_shape = pl.BlockSpec((tr, D_IN), lambda i: (i, 0)), (D_MODEL, D_IN)
    else:
        out_spec, out_shape = in_spec, (N_CHIPS, D_MODEL, IN_SHARD)
    return pl.pallas_call(
        body, name=name, grid=(D_MODEL // tr,), in_specs=[in_spec], out_specs=out_spec,
        out_shape=jax.ShapeDtypeStruct(out_shape, w4.dtype), compiler_params=_params(("arbitrary",)),
    )(w4)


def _in_proj(x1, w_in_g, b_in, comm=None):
    n = x1.shape[0]
    tm = min(ROW_TILE, n)

    def body(x_ref, w_ref, b_ref, o_ref):
        xv = x_ref[...]
        for j in range(N_CHIPS):
            cols = slice(j * IN_SHARD, (j + 1) * IN_SHARD)
            o_ref[:, cols] = _mm(xv, w_ref[:, cols]) + b_ref[:, cols]

    return _hosted(
        body, comm, name="in_proj", grid=(n // tm,),
        in_specs=[pl.BlockSpec((tm, D_MODEL), lambda i: (i, 0)),
                  _resident((D_MODEL, D_IN)), _full((1, D_IN))],
        out_specs=[pl.BlockSpec((tm, D_IN), lambda i: (i, 0))],
        out_shape=[jax.ShapeDtypeStruct((n, D_IN), F32)],
        scratch_shapes=[],
        compiler_params=_params(("arbitrary",)), args=[x1, w_in_g, b_in])


def _in_proj_dx(dproj, w_in_g, dz2, comm=None):
    n = dproj.shape[0]
    tm = min(ROW_TILE, n)

    def body(dp_ref, w_ref, dz_ref, o_ref):
        dx = DEEPNORM_ALPHA * dz_ref[...]
        for j in range(N_CHIPS):
            cols = slice(j * IN_SHARD, (j + 1) * IN_SHARD)
            dx = dx + _mm_nt(dp_ref[:, cols], w_ref[:, cols])
        o_ref[...] = dx

    return _hosted(
        body, comm, name="in_proj_dx", grid=(n // tm,),
        in_specs=[pl.BlockSpec((tm, D_IN), lambda i: (i, 0)),
                  _resident((D_MODEL, D_IN)),
                  pl.BlockSpec((tm, D_MODEL), lambda i: (i, 0))],
        out_specs=[pl.BlockSpec((tm, D_MODEL), lambda i: (i, 0))],
        out_shape=[jax.ShapeDtypeStruct((n, D_MODEL), F32)],
        scratch_shapes=[],
        compiler_params=_params(("arbitrary",)), args=[dproj, w_in_g, dz2])


def _rope_tables(seq_len):
    pos = jnp.arange(seq_len, dtype=F32)
    inv_freq = ROPE_THETA ** (-jnp.arange(0, ROPE_DIM, 2, dtype=F32) / ROPE_DIM)
    ang = pos[:, None] * inv_freq[None, :]
    cos, sin = jnp.cos(ang), jnp.sin(ang)
    half = ROPE_DIM // 2
    rest = HEAD_DIM - ROPE_DIM
    ones = jnp.ones((seq_len, rest), F32)
    zeros = jnp.zeros((seq_len, rest), F32)
    zh = jnp.zeros((seq_len, half), F32)
    c = jnp.concatenate([cos, cos, ones], axis=1)
    sa = jnp.concatenate([-sin, zh, zeros], axis=1)
    sb = jnp.concatenate([zh, sin, zeros], axis=1)
    reps = LANES // HEAD_DIM
    return tuple(jnp.tile(t, (1, reps)) for t in (c, sa, sb))


def _rope(t, c, sa, sb):
    w = t.shape[1]
    reps = w // LANES
    half = ROPE_DIM // 2
    return (t * jnp.tile(c, (1, reps)) + pltpu.roll(t, w - half, 1) * jnp.tile(sa, (1, reps))
            + pltpu.roll(t, half, 1) * jnp.tile(sb, (1, reps)))


def _rope_transposed(g, c, sa, sb):
    w = g.shape[1]
    reps = w // LANES
    half = ROPE_DIM // 2
    return (g * jnp.tile(c, (1, reps)) + pltpu.roll(g * jnp.tile(sa, (1, reps)), half, 1)
            + pltpu.roll(g * jnp.tile(sb, (1, reps)), w - half, 1))


GROUP = N_Q_HEADS // N_KV_HEADS


def _both_halves(t_pair, which):
    lo = lax.broadcasted_iota(jnp.int32, t_pair.shape, 1) < HEAD_DIM
    swapped = pltpu.roll(t_pair, HEAD_DIM, 1)
    return _bf(jnp.where(lo, t_pair, swapped) if which == 0 else jnp.where(lo, swapped, t_pair))


def _stack_heads(ref_or_val, kh):
    lo = lax.broadcasted_iota(jnp.int32, (ATTN_BLOCK, LANES), 1) < HEAD_DIM
    rows = []
    for gp in range(GROUP // 2):
        pair = kh * (GROUP // 2) + gp
        t = ref_or_val[:, pair * LANES:(pair + 1) * LANES]
        rows += [jnp.where(lo, t, jnp.zeros_like(t)), jnp.where(lo, jnp.zeros_like(t), t)]
    return jnp.concatenate(rows, axis=0)


def _unstack_pairs(stacked):
    lo = lax.broadcasted_iota(jnp.int32, (ATTN_BLOCK, LANES), 1) < HEAD_DIM
    b = ATTN_BLOCK
    return [jnp.where(lo, stacked[2 * gp * b:(2 * gp + 1) * b], stacked[(2 * gp + 1) * b:(2 * gp + 2) * b])
            for gp in range(GROUP // 2)]


def _attn_mask_t(n):
    cols = GROUP * ATTN_BLOCK
    kj = lax.broadcasted_iota(jnp.int32, (2 * ATTN_BLOCK, cols), 0)
    qi = lax.broadcasted_iota(jnp.int32, (2 * ATTN_BLOCK, cols), 1) % ATTN_BLOCK
    dist = qi + ATTN_BLOCK - kj
    return (dist >= 0) & (dist < ATTN_BLOCK) & (n * ATTN_BLOCK + kj - ATTN_BLOCK >= 0)


def _sink_row(sink_ref, kh):
    col = lax.broadcasted_iota(jnp.int32, (1, GROUP * ATTN_BLOCK), 1)
    row = jnp.full((1, GROUP * ATTN_BLOCK), sink_ref[0, kh * GROUP + GROUP - 1], F32)
    for i in reversed(range(GROUP - 1)):
        row = jnp.where(col < (i + 1) * ATTN_BLOCK, sink_ref[0, kh * GROUP + i], row)
    return row


def _attn_probs_t(q_masked, k_sel, mask_t, sink):
    s = _mm_nt(k_sel, q_masked) * (HEAD_DIM ** -0.5)
    s = jnp.where(mask_t, s, NEG_INF)
    m = jnp.maximum(jnp.max(s, axis=0, keepdims=True), sink)
    p = jnp.exp(s - m)
    e_sink = jnp.exp(sink - m)
    denom = jnp.sum(p, axis=0, keepdims=True) + e_sink
    return p / denom, e_sink / denom


def _attn_fwd(proj, tabs, sinks):
    n_tok = proj.shape[0]
    nb = n_tok // ATTN_BLOCK

    def body(q_ref, k_ref, v_ref, c_ref, sa_ref, sb_ref, sink_ref, y_ref, kprev, vprev):
        n = pl.program_id(0)

        @pl.when(n == 0)
        def _():
            kprev[...] = jnp.zeros_like(kprev)
            vprev[...] = jnp.zeros_like(vprev)

        c, sa, sb = c_ref[...], sa_ref[...], sb_ref[...]
        qr = _bf(_rope(q_ref[...], c, sa, sb))
        kr = _rope(k_ref[...], c, sa, sb)
        vc = v_ref[...]
        kk = jnp.concatenate([kprev[...], kr], axis=0)
        vv = jnp.concatenate([vprev[...], vc], axis=0)
        kprev[...] = kr
        vprev[...] = vc
        mask = _attn_mask_t(n)
        for kh in range(N_KV_HEADS):
            r, which = divmod(kh, 2)
            kb = _both_halves(kk[:, r * LANES:(r + 1) * LANES], which)
            vb = _both_halves(vv[:, r * LANES:(r + 1) * LANES], which)
            probs, _ = _attn_probs_t(_stack_heads(qr, kh), kb, mask, _sink_row(sink_ref, kh))
            for gp, out in enumerate(_unstack_pairs(_mm_tn(_bf(probs), vb))):
                pair = kh * (GROUP // 2) + gp
                y_ref[:, pair * LANES:(pair + 1) * LANES] = _bf(out)

    blk = lambda width, col: pl.BlockSpec((ATTN_BLOCK, width), lambda n: (n, col))
    tab = pl.BlockSpec((ATTN_BLOCK, LANES), lambda n: (n, 0))
    kvw = N_KV_HEADS * HEAD_DIM
    return pl.pallas_call(
        body, name="attn_fwd", grid=(nb,),
        in_specs=[blk(D_MODEL, COL_Q), blk(kvw, COL_K), blk(kvw, COL_V), tab, tab, tab,
                  pl.BlockSpec(memory_space=pltpu.SMEM)],
        out_specs=pl.BlockSpec((ATTN_BLOCK, D_MODEL), lambda n: (n, 0)),
        out_shape=jax.ShapeDtypeStruct((n_tok, D_MODEL), BF16),
        scratch_shapes=[pltpu.VMEM((ATTN_BLOCK, kvw), F32), pltpu.VMEM((ATTN_BLOCK, kvw), F32)],
        compiler_params=_params(("arbitrary",)),
    )(proj, proj, proj, *tabs, sinks)


def _attn_bwd(proj, dy, tabs, sinks, dproj, comm=None):
    n_tok = proj.shape[0]
    nb = n_tok // ATTN_BLOCK
    kvw = N_KV_HEADS * HEAD_DIM

    def body(q_ref, k_ref, v_ref, do_ref, c_ref, sa_ref, sb_ref, cp_ref, sap_ref, sbp_ref, sink_ref, _,
             dqkv_ref, dsink_ref, kprev, vprev, dkc, dvc, dqc):
        n = pl.program_id(0)

        @pl.when(n == 0)
        def _():
            for ref in (kprev, vprev, dkc, dvc, dqc, dsink_ref):
                ref[...] = jnp.zeros_like(ref)

        prev_tabs = (cp_ref[...], sap_ref[...], sbp_ref[...])

        @pl.when(n < nb)
        def _():
            c, sa, sb = c_ref[...], sa_ref[...], sb_ref[...]
            qr = _bf(_rope(q_ref[...], c, sa, sb))
            kr = _rope(k_ref[...], c, sa, sb)
            vc = v_ref[...]
            kk = jnp.concatenate([kprev[...], kr], axis=0)
            vv = jnp.concatenate([vprev[...], vc], axis=0)
            kprev[...] = kr
            vprev[...] = vc
            mask = _attn_mask_t(n)
            lane = lax.broadcasted_iota(jnp.int32, (1, LANES), 1)
            lo2 = lax.broadcasted_iota(jnp.int32, (2 * ATTN_BLOCK, LANES), 1) < HEAD_DIM
            dsink = jnp.zeros((1, LANES), F32)
            dq_pairs = []
            dk_full = []
            dv_full = []
            for kh in range(N_KV_HEADS):
                r, which = divmod(kh, 2)
                kb = _both_halves(kk[:, r * LANES:(r + 1) * LANES], which)
                vb = _both_halves(vv[:, r * LANES:(r + 1) * LANES], which)
                qs = _stack_heads(qr, kh)
                dos = _stack_heads(do_ref, kh)
                probs, p_sink = _attn_probs_t(qs, kb, mask, _sink_row(sink_ref, kh))
                dp = _mm_nt(vb, dos)
                delta = jnp.sum(probs * dp, axis=0, keepdims=True)
                ds = _bf(probs * (dp - delta) * (HEAD_DIM ** -0.5))
                sink_terms = p_sink * delta
                for i in range(GROUP):
                    head_sum = jnp.sum(sink_terms[:, i * ATTN_BLOCK:(i + 1) * ATTN_BLOCK])
                    dsink = dsink + jnp.where(lane == kh * GROUP + i, -head_sum, 0.0)
                dq_pairs += _unstack_pairs(_mm_tn(ds, kb))
                dk_acc = _mm(ds, qs)
                dv_acc = _mm(_bf(probs), dos)
                dk_full.append(dk_acc + pltpu.roll(dk_acc, HEAD_DIM, 1))
                dv_full.append(dv_acc + pltpu.roll(dv_acc, HEAD_DIM, 1))
            dk_pairs = [jnp.where(lo2, dk_full[2 * r], dk_full[2 * r + 1]) for r in range(N_KV_HEADS // 2)]
            dv_pairs = [jnp.where(lo2, dv_full[2 * r], dv_full[2 * r + 1]) for r in range(N_KV_HEADS // 2)]
            dsink_ref[...] += dsink
            dqkv_ref[:, :D_MODEL] = _bf(dqc[...])
            dqc[...] = _rope_transposed(jnp.concatenate(dq_pairs, axis=1), c, sa, sb)
            dk_all = jnp.concatenate(dk_pairs, axis=1)
            dv_all = jnp.concatenate(dv_pairs, axis=1)
            dqkv_ref[:, D_MODEL:D_MODEL + kvw] = _bf(_rope_transposed(dkc[...] + dk_all[:ATTN_BLOCK], *prev_tabs))
            dqkv_ref[:, D_MODEL + kvw:] = _bf(dvc[...] + dv_all[:ATTN_BLOCK])
            dkc[...] = dk_all[ATTN_BLOCK:]
            dvc[...] = dv_all[ATTN_BLOCK:]

        @pl.when(n == nb)
        def _():
            dqkv_ref[:, :D_MODEL] = _bf(dqc[...])
            dqkv_ref[:, D_MODEL:D_MODEL + kvw] = _bf(_rope_transposed(dkc[...], *prev_tabs))
            dqkv_ref[:, D_MODEL + kvw:] = _bf(dvc[...])

    cur = lambda n: jnp.minimum(n, nb - 1)
    prev = lambda n: jnp.maximum(n - 1, 0)
    blk = lambda width, col: pl.BlockSpec((ATTN_BLOCK, width), lambda n: (cur(n), col))
    tab = pl.BlockSpec((ATTN_BLOCK, LANES), lambda n: (cur(n), 0))
    tabp = pl.BlockSpec((ATTN_BLOCK, LANES), lambda n: (prev(n), 0))
    return _hosted(
        body, comm, name="attn_bwd", grid=(nb + 1,),
        in_specs=[blk(D_MODEL, COL_Q), blk(kvw, COL_K), blk(kvw, COL_V), blk(D_MODEL, 0), tab, tab, tab, tabp, tabp, tabp,
                  pl.BlockSpec(memory_space=pltpu.SMEM), pl.BlockSpec(memory_space=pl.ANY)],
        out_specs=[pl.BlockSpec((ATTN_BLOCK, ATTN_WIDTH), lambda n: (prev(n), COL_ATTN)),
                   pl.BlockSpec((1, LANES), lambda n: (0, 0))],
        out_shape=[jax.ShapeDtypeStruct(dproj.shape, dproj.dtype), jax.ShapeDtypeStruct((1, LANES), F32)],
        scratch_shapes=[pltpu.VMEM((ATTN_BLOCK, kvw), F32)] * 4 + [pltpu.VMEM((ATTN_BLOCK, D_MODEL), F32)],
        compiler_params=_params(("arbitrary",)), args=[proj, proj, proj, dy, *tabs, *tabs, sinks, dproj],
        aliases={11: 0})


def _bmm(a, b):
    return lax.dot_general(a, b, (((2,), (1,)), ((0,), (0,))), preferred_element_type=F32)


def _bmm_nt(a, b):
    return lax.dot_general(a, b, (((2,), (2,)), ((0,), (0,))), preferred_element_type=F32)


def _bmm_tn(a, b):
    return lax.dot_general(a, b, (((1,), (1,)), ((0,), (0,))), preferred_element_type=F32)


def _tril(cb, upper=False):
    shape = (cb, HGRN_CHUNK, HGRN_CHUNK)
    r, c = lax.broadcasted_iota(jnp.int32, shape, 1), lax.broadcasted_iota(jnp.int32, shape, 2)
    return (r <= c) if upper else (r >= c)


def _tri_matmul(x, upper):
    return lax.dot_general(_tril(x.shape[0], upper).astype(F32), x, (((2,), (1,)), ((0,), (0,))),
                           precision=lax.Precision.HIGHEST, preferred_element_type=F32)


@jax.custom_vjp
def _chunk_cumsum(x):
    return _tri_matmul(x, False)


_chunk_cumsum.defvjp(lambda x: (_tri_matmul(x, False), None), lambda _, g: (_tri_matmul(g, True),))


def _hg_elem(fl, qh, lb):
    f = lb + (1.0 - lb) * _sig(fl)
    k = 1.0 - f
    gc = _chunk_cumsum(jnp.log(f))
    last = lax.broadcasted_iota(jnp.int32, gc.shape, 1) == HGRN_CHUNK - 1
    g_last = jnp.sum(jnp.where(last, gc, 0.0), axis=1, keepdims=True)
    q = qh * _sig(qh)
    return q * jnp.exp(gc), k * jnp.exp(-gc), k * jnp.exp(g_last - gc), jnp.exp(g_last)


def _hg_out(q_dec, k_inv, v, st):
    sc = jnp.where(_tril(q_dec.shape[0]), _bmm_nt(_bf(q_dec), _bf(k_inv)), 0.0)
    return _bmm(_bf(sc), _bf(v)) + _bmm_nt(_bf(q_dec), _bf(st)), sc


def _hg_post(o, og, ng):
    on = o * lax.rsqrt(jnp.mean(o * o, axis=-1, keepdims=True) + RMS_EPS) * ng
    return on * (og * _sig(og))


def _hgrn_specs(n_tok, rev):
    nc = n_tok // HGRN_CHUNK
    cb = min(HGRN_CHUNKS_PER_STEP * (1 if rev else 2), nc)
    nt = nc // cb
    rows = cb * HGRN_CHUNK
    tt = (lambda t: nt - 1 - t) if rev else (lambda t: t)
    col = lambda base: pl.BlockSpec((rows, LANES), lambda h, t: (tt(t), base + h))
    head_cols = pl.BlockSpec((rows, HGRN_HEAD_WIDTH), lambda h, t: (tt(t), COL_HGRN + h))
    head_vec = pl.BlockSpec((1, LANES), lambda h, t: (0, h))
    one_vec = pl.BlockSpec((1, LANES), lambda h, t: (0, 0))
    state = pl.BlockSpec((1, cb, HGRN_DK, HGRN_DK), lambda h, t: (h, tt(t), 0, 0))
    return nc, cb, nt, col, head_cols, head_vec, one_vec, state


def _hgrn_fwd(proj, lb, ng):
    n_tok = proj.shape[0]
    nc, cb, nt, col, head_cols, head_vec, one_vec, state = _hgrn_specs(n_tok, False)

    def body(in_ref, lb_ref, ng_ref, y_ref, st_ref, s_acc):
        @pl.when(pl.program_id(1) == 0)
        def _():
            s_acc[...] = jnp.zeros_like(s_acc)

        fl, qh, v, og = (in_ref[:, i * LANES:(i + 1) * LANES].reshape(cb, HGRN_CHUNK, LANES) for i in range(4))
        q_dec, k_inv, k_end, decay = _hg_elem(fl, qh, lb_ref[...])
        upd = _bmm_tn(_bf(v), _bf(k_end))
        st = s_acc[...]
        for ci in range(cb):
            st_ref[0, ci] = st
            st = st * decay[ci] + upd[ci]
        s_acc[...] = st
        o, _ = _hg_out(q_dec, k_inv, v, st_ref[0])
        y_ref[...] = _bf(_hg_post(o, og, ng_ref[...]).reshape(cb * HGRN_CHUNK, LANES))

    return pl.pallas_call(
        body, name="hgrn_fwd", grid=(HGRN_HEADS, nt),
        in_specs=[head_cols, head_vec, one_vec],
        out_specs=[col(0), state],
        out_shape=[jax.ShapeDtypeStruct((n_tok, D_MODEL), BF16),
                   jax.ShapeDtypeStruct((HGRN_HEADS, nc, HGRN_DK, HGRN_DK), F32)],
        scratch_shapes=[pltpu.VMEM((HGRN_DK, HGRN_DK), F32)],
        compiler_params=_params(("arbitrary", "arbitrary")),
    )(proj, lb, ng)


def _hgrn_bwd(proj, lb, ng, states, dy, dproj, comm=None):
    n_tok = proj.shape[0]
    nc, cb, nt, col, head_cols, head_vec, one_vec, state = _hgrn_specs(n_tok, True)

    def body(in_ref, lb_ref, ng_ref, st_ref, dy_ref, _, d_ref, dlb_ref, dng_ref, g_acc, g_all):
        h = pl.program_id(0)
        t = pl.program_id(1)

        @pl.when(t == 0)
        def _():
            g_acc[...] = jnp.zeros_like(g_acc)
            dlb_ref[...] = jnp.zeros_like(dlb_ref)

        @pl.when((t == 0) & (h == 0))
        def _():
            dng_ref[...] = jnp.zeros_like(dng_ref)

        fl, qh, v, og = (in_ref[:, i * LANES:(i + 1) * LANES].reshape(cb, HGRN_CHUNK, LANES) for i in range(4))
        (q_dec, k_inv, k_end, decay), elem_vjp = jax.vjp(_hg_elem, fl, qh, lb_ref[...])
        st = st_ref[0]
        o, sc = _hg_out(q_dec, k_inv, v, st)
        _, post_vjp = jax.vjp(_hg_post, o, og, ng_ref[...])
        do, dog, dng = post_vjp(dy_ref[...].reshape(cb, HGRN_CHUNK, LANES))
        dob, vb, qb = _bf(do), _bf(v), _bf(q_dec)
        dsc = _bf(jnp.where(_tril(cb), _bmm_nt(dob, vb), 0.0))
        p = _bmm_tn(dob, qb)
        g = g_acc[...]
        for ci in reversed(range(cb)):
            g_all[ci] = g
            g = g * decay[ci] + p[ci]
        g_acc[...] = g
        g = g_all[...]
        gb = _bf(g)
        dq_dec = _bmm(dsc, _bf(k_inv)) + _bmm(dob, _bf(st))
        dk_inv = _bmm_tn(dsc, qb)
        dv = _bmm_tn(_bf(sc), dob) + _bmm_nt(_bf(k_end), gb)
        dk_end = _bmm(vb, gb)
        ddecay = jnp.sum(st * g, axis=1, keepdims=True)
        dfl, dqh, dlb = elem_vjp((dq_dec, dk_inv, dk_end, ddecay))
        for i, val in enumerate((dfl, dqh, dv, dog)):
            d_ref[:, i * LANES:(i + 1) * LANES] = _bf(val.reshape(cb * HGRN_CHUNK, LANES))
        dlb_ref[...] += dlb
        dng_ref[...] += dng

    return _hosted(
        body, comm, name="hgrn_bwd", grid=(HGRN_HEADS, nt),
        in_specs=[head_cols, head_vec, one_vec, state, col(0), pl.BlockSpec(memory_space=pl.ANY)],
        out_specs=[head_cols, head_vec, one_vec],
        out_shape=[jax.ShapeDtypeStruct(dproj.shape, dproj.dtype),
                   jax.ShapeDtypeStruct((1, D_MODEL), F32), jax.ShapeDtypeStruct((1, LANES), F32)],
        scratch_shapes=[pltpu.VMEM((HGRN_DK, HGRN_DK), F32), pltpu.VMEM((cb, HGRN_DK, HGRN_DK), F32)],
        compiler_params=_params(("arbitrary", "arbitrary")), args=[proj, lb, ng, states, dy, dproj], aliases={5: 0})


def _lb_fwd(lb_logits):
    def lb_of(l0, l1):
        m = jnp.maximum(l0, l1)
        e0, e1 = jnp.exp(l0 - m), jnp.exp(l1 - m)
        return e0 / (e0 + e1)

    def body(l_ref, o_ref):
        o_ref[...] = lb_of(l_ref[0:1, :], l_ref[1:2, :])

    lb = pl.pallas_call(body, name="lb_fwd", out_shape=jax.ShapeDtypeStruct((1, D_MODEL), F32))(lb_logits)
    return lb, lb_of


def _mix_fwd(y_attn, y_hgrn, proj, x1, w_pa, w_ph, w_out, g, b):
    n = x1.shape[0]
    tm = min(ROW_TILE, n)

    def body(ya_ref, yh_ref, gt_ref, x_ref, wpa, wph, wo, g_ref, b_ref, z_ref, o_ref, pa_ref, ph_ref):
        ya = _mm(ya_ref[...], wpa[...])
        yh = _mm(yh_ref[...], wph[...])
        pa_ref[...] = _bf(ya)
        ph_ref[...] = _bf(yh)
        merged = _sig(gt_ref[:, :D_MODEL]) * ya + _sig(gt_ref[:, D_MODEL:]) * yh
        z = DEEPNORM_ALPHA * x_ref[...] + _mm(_bf(merged), wo[...])
        z_ref[...] = z
        o_ref[...] = _ln(z, g_ref[...], b_ref[...])

    row = pl.BlockSpec((tm, D_MODEL), lambda i: (i, 0))
    gates = pl.BlockSpec((tm, GATES_WIDTH), lambda i: (i, 0))
    sq = _resident((D_MODEL, D_MODEL))
    vec = _full((1, D_MODEL))
    return pl.pallas_call(
        body, name="mix_fwd", grid=(n // tm,),
        in_specs=[row, row, gates, row, sq, sq, sq, vec, vec],
        out_specs=[row, row, row, row],
        out_shape=[jax.ShapeDtypeStruct((n, D_MODEL), F32)] * 2 + [jax.ShapeDtypeStruct((n, D_MODEL), BF16)] * 2,
        compiler_params=_params(("arbitrary",)),
    )(y_attn, y_hgrn, proj, x1, w_pa, w_ph, w_out, g, b)


def _mix_bwd(dx2, z2, pa, ph, proj, w_pa, w_ph, w_out, g, b):
    n = z2.shape[0]
    tm = min(MIX_TILE, n)

    def body(do_ref, z_ref, ya_ref, yh_ref, gt_ref, wpa, wph, wo, g_ref, b_ref,
             dz_ref, dzb_ref, mg_ref, dya_ref, dyh_ref, dyat_ref, dyhg_ref, dgt_ref, dg_ref, db_ref):
        _, vjp = jax.vjp(_ln, z_ref[...], g_ref[...], b_ref[...])
        dz, dg, db = vjp(do_ref[...])

        @pl.when(pl.program_id(0) == 0)
        def _():
            dg_ref[...] = jnp.zeros_like(dg_ref)
            db_ref[...] = jnp.zeros_like(db_ref)

        dg_ref[...] += dg
        db_ref[...] += db
        dz_ref[...] = dz
        ya = ya_ref[...].astype(F32)
        yh = yh_ref[...].astype(F32)
        def merge(ga, gh, ya, yh):
            return _sig(ga) * ya + _sig(gh) * yh

        merged, merge_vjp = jax.vjp(merge, gt_ref[:, :D_MODEL], gt_ref[:, D_MODEL:], ya, yh)
        mg_ref[...] = _bf(merged)
        dzb = _bf(dz)
        dzb_ref[...] = dzb
        dmerged = _mm_nt(dzb, wo[...])
        dga, dgh, dya, dyh = merge_vjp(dmerged)
        dya = _bf(dya)
        dyh = _bf(dyh)
        dya_ref[...] = dya
        dyh_ref[...] = dyh
        dgt_ref[:, :D_MODEL] = _bf(dga)
        dgt_ref[:, D_MODEL:] = _bf(dgh)
        dyat_ref[...] = _bf(_mm_nt(dya, wpa[...]))
        dyhg_ref[...] = _mm_nt(dyh, wph[...])

    row = pl.BlockSpec((tm, D_MODEL), lambda i: (i, 0))
    gates = pl.BlockSpec((tm, GATES_WIDTH), lambda i: (i, 0))
    sq = _resident((D_MODEL, D_MODEL))
    vec = _full((1, D_MODEL))
    f32_row = jax.ShapeDtypeStruct((n, D_MODEL), F32)
    bf_row = jax.ShapeDtypeStruct((n, D_MODEL), BF16)
    vec_shape = jax.ShapeDtypeStruct((1, D_MODEL), F32)
    return pl.pallas_call(
        body, name="mix_bwd", grid=(n // tm,),
        in_specs=[row, row, row, row, gates, sq, sq, sq, vec, vec],
        out_specs=[row, row, row, row, row, row, row, gates, vec, vec],
        out_shape=[f32_row, bf_row, bf_row, bf_row, bf_row, bf_row, f32_row,
                   jax.ShapeDtypeStruct((n, D_IN), BF16), vec_shape, vec_shape],
        compiler_params=_params(("arbitrary",)),
    )(dx2, z2, pa, ph, proj, w_pa, w_ph, w_out, g, b)


def _position():
    x, y, c = lax.axis_index("x"), lax.axis_index("y"), lax.axis_index("c")
    chips = [(1 - x, y), (x, 1 - y), (1 - x, 1 - y)]
    return x, y, c, chips


def _any_specs(k):
    return [pl.BlockSpec(memory_space=pl.ANY)] * k


class _GatherWeights:
    def __init__(self, shards):
        nw = len(shards)
        self.inputs = list(shards)
        self.out_shape = [jax.ShapeDtypeStruct((N_CHIPS, *s.shape), s.dtype) for s in shards]
        self.scratch = [pltpu.SemaphoreType.DMA((nw,)), pltpu.SemaphoreType.DMA((nw * 6,)),
                        pltpu.SemaphoreType.DMA((nw * 6,))]

    def _copies(self, ins, outs, sems):
        nw = len(ins)
        local_sem, send_sem, recv_sem = sems
        x, y, c, chips = _position()
        me = 2 * x + y
        sibling = (x, y, 1 - c)
        half_rows = [s.shape[0] // 2 for s in self.inputs]

        def half(w, chip_idx, which):
            return outs[w].at[chip_idx, pl.ds(which * half_rows[w], half_rows[w])]

        def remote(w, k, src, dst, to):
            return pltpu.make_async_remote_copy(src_ref=src, dst_ref=dst, send_sem=send_sem.at[w * 6 + k],
                                                recv_sem=recv_sem.at[w * 6 + k], device_id=to, device_id_type=MESH)

        local = [pltpu.make_async_copy(ins[w], outs[w].at[me], local_sem.at[w]) for w in range(nw)]
        first = [remote(w, j, ins[w].at[pl.ds(c * half_rows[w], half_rows[w])], half(w, me, c), (px, py, c))
                 for w in range(nw) for j, (px, py) in enumerate(chips)]
        landed = [half(w, 2 * px + py, c) for w in range(nw) for (px, py) in chips]
        arrive = [remote(w, j, landed[w * 3 + j], landed[w * 3 + j], (px, py, c))
                  for w in range(nw) for j, (px, py) in enumerate(chips)]
        passed = [remote(w, 3 + j, landed[w * 3 + j], landed[w * 3 + j], sibling) for w in range(nw) for j in range(3)]
        from_sibling = [remote(w, 3 + j, half(w, 2 * px + py, 1 - c), half(w, 2 * px + py, 1 - c), sibling)
                        for w in range(nw) for j, (px, py) in enumerate(chips)]
        return local, first, arrive, passed, from_sibling

    def start(self, ins, outs, sems):
        local, first, _, _, _ = self._copies(ins, outs, sems)
        for cp in local + first:
            cp.start()

    def finish(self, ins, outs, sems):
        local, first, arrive, passed, from_sibling = self._copies(ins, outs, sems)
        for cp_in, cp_on in zip(arrive, passed):
            cp_in.wait_recv()
            cp_on.start()
        for cp in from_sibling:
            cp.wait_recv()
        for cp in first + passed:
            cp.wait_send()
        for cp in local:
            cp.wait()


class _ExchangeGrads:
    def __init__(self, grads, rows=None, into=None):
        nw = len(grads)
        self.rows = [rows or (0, g.shape[1]) for g in grads]
        self.inputs = list(grads) + list(into or [])
        self.aliases = {nw + i: i for i in range(nw)} if into else {}
        self.out_shape = [jax.ShapeDtypeStruct(g.shape, g.dtype) for g in grads]
        self.scratch = [pltpu.SemaphoreType.DMA((nw,)), pltpu.SemaphoreType.DMA((nw * 3,)),
                        pltpu.SemaphoreType.DMA((nw * 3,))]

    def _copies(self, ins, outs, sems):
        nw = len(outs)
        local_sem, send_sem, recv_sem = sems
        x, y, c, chips = _position()
        me = 2 * x + y
        rows = [pl.ds(*r) for r in self.rows]

        def remote(w, j, src, dst, chip):
            return pltpu.make_async_remote_copy(src_ref=src, dst_ref=dst, send_sem=send_sem.at[w * 3 + j],
                                                recv_sem=recv_sem.at[w * 3 + j], device_id=(*chip, c),
                                                device_id_type=MESH)

        local = [pltpu.make_async_copy(ins[w].at[me, rows[w]], outs[w].at[me, rows[w]], local_sem.at[w])
                 for w in range(nw)]
        sends = [remote(w, j, ins[w].at[2 * px + py, rows[w]], outs[w].at[me, rows[w]], (px, py))
                 for w in range(nw) for j, (px, py) in enumerate(chips)]
        arrive = [remote(w, j, outs[w].at[2 * px + py, rows[w]], outs[w].at[2 * px + py, rows[w]], (px, py))
                  for w in range(nw) for j, (px, py) in enumerate(chips)]
        return local, sends, arrive

    def start(self, ins, outs, sems):
        local, sends, _ = self._copies(ins, outs, sems)
        for cp in local + sends:
            cp.start()

    def finish(self, ins, outs, sems):
        local, sends, arrive = self._copies(ins, outs, sems)
        for cp in arrive:
            cp.wait_recv()
        for cp in sends:
            cp.wait_send()
        for cp in local:
            cp.wait()


def _hosted(body, comm, *, name, grid, in_specs, out_specs, out_shape, scratch_shapes, compiler_params, args,
            aliases=None):
    aliases = aliases or {}
    if comm is None:
        res = pl.pallas_call(body, name=name, grid=grid, in_specs=in_specs, out_specs=out_specs, out_shape=out_shape,
                             scratch_shapes=scratch_shapes, compiler_params=compiler_params,
                             input_output_aliases=aliases)(*args)
        return list(res), []
    n_in, n_out, n_scr = len(in_specs), len(out_specs), len(scratch_shapes)
    c_in, c_out = len(comm.inputs), len(comm.out_shape)
    aliases = {**aliases, **{n_in + i: n_out + o for i, o in getattr(comm, "aliases", {}).items()}}

    def hosted_body(*refs):
        refs = list(refs)
        cut = lambda k: (refs[:k], refs[k:])
        main_in, refs = cut(n_in)
        comm_in, refs = cut(c_in)
        main_out, refs = cut(n_out)
        comm_out, refs = cut(c_out)
        main_scr, comm_scr = cut(n_scr)
        ids = [pl.program_id(a) for a in range(len(grid))]
        first = functools.reduce(jnp.logical_and, [i == 0 for i in ids])
        last = functools.reduce(jnp.logical_and, [i == g - 1 for i, g in zip(ids, grid)])

        @pl.when(first)
        def _():
            comm.start(comm_in, comm_out, comm_scr)

        body(*main_in, *main_out, *main_scr)

        @pl.when(last)
        def _():
            comm.finish(comm_in, comm_out, comm_scr)

    res = pl.pallas_call(
        hosted_body, name=name, grid=grid, in_specs=[*in_specs, *_any_specs(c_in)],
        out_specs=[*out_specs, *_any_specs(c_out)], out_shape=[*out_shape, *comm.out_shape],
        scratch_shapes=[*scratch_shapes, *comm.scratch], compiler_params=compiler_params,
        input_output_aliases=aliases,
    )(*args, *comm.inputs)
    return list(res[:n_out]), list(res[n_out:])


def _run_comm(name, comm):
    def body(*refs):
        refs = list(refs)
        c_in, c_out = len(comm.inputs), len(comm.out_shape)
        ins, outs, sems = refs[:c_in], refs[c_in:c_in + c_out], refs[c_in + c_out:]
        comm.start(ins, outs, sems)
        comm.finish(ins, outs, sems)

    return list(pl.pallas_call(
        body, name=name, in_specs=_any_specs(len(comm.inputs)), out_specs=_any_specs(len(comm.out_shape)),
        out_shape=comm.out_shape, scratch_shapes=comm.scratch,
    )(*comm.inputs))


def _sum_slots(name, slots):
    k = len(slots)
    _, rows, cols = slots[0].shape
    tr = _update_rows(rows)

    def body(*refs):
        for s_ref, o_ref in zip(refs[:k], refs[k:]):
            acc = s_ref[0].astype(F32)
            for i in range(1, N_CHIPS):
                acc = acc + s_ref[i].astype(F32)
            o_ref[...] = acc

    return pl.pallas_call(
        body, name=name, grid=(rows // tr,),
        in_specs=[pl.BlockSpec((N_CHIPS, tr, cols), lambda i: (0, i, 0))] * k,
        out_specs=[pl.BlockSpec((tr, cols), lambda i: (i, 0))] * k,
        out_shape=[jax.ShapeDtypeStruct((rows, cols), F32)] * k,
        compiler_params=_params(("arbitrary",)),
    )(*slots)


class _SwapWithSibling:
    def __init__(self, parts):
        self.inputs = list(parts)
        self.out_shape = [jax.ShapeDtypeStruct(p.shape, p.dtype) for p in parts]
        self.scratch = [pltpu.SemaphoreType.DMA((len(parts),)), pltpu.SemaphoreType.DMA((len(parts),))]

    def _copies(self, ins, outs, sems):
        send_sem, recv_sem = sems
        x, y, c, _ = _position()
        return [pltpu.make_async_remote_copy(src_ref=ins[w], dst_ref=outs[w], send_sem=send_sem.at[w],
                                             recv_sem=recv_sem.at[w], device_id=(x, y, 1 - c), device_id_type=MESH)
                for w in range(len(ins))]

    def start(self, ins, outs, sems):
        for cp in self._copies(ins, outs, sems):
            cp.start()

    def finish(self, ins, outs, sems):
        for cp in self._copies(ins, outs, sems):
            cp.wait()


class _Together:
    def __init__(self, first, second):
        self.parts = (first, second)
        self.inputs = first.inputs + second.inputs
        self.out_shape = first.out_shape + second.out_shape
        self.scratch = first.scratch + second.scratch

    def _split(self, ins, outs, sems):
        a = self.parts[0]
        ni, no, ns = len(a.inputs), len(a.out_shape), len(a.scratch)
        return ((ins[:ni], outs[:no], sems[:ns]), (ins[ni:], outs[no:], sems[ns:]))

    def start(self, ins, outs, sems):
        for part, args in zip(self.parts, self._split(ins, outs, sems)):
            part.start(*args)

    def finish(self, ins, outs, sems):
        for part, args in zip(self.parts, self._split(ins, outs, sems)):
            part.finish(*args)


def _sum_small(part):
    def body(p_ref, o_ref, buf, send_sem, recv_sem):
        x, y, c, _ = _position()
        me = 4 * x + 2 * y + c
        buf[me] = p_ref[...]
        copies = []
        for k in range(1, N_DEV):
            peer = tuple(1 - v if (k >> s) & 1 else v for v, s in ((x, 2), (y, 1), (c, 0)))
            copies.append(pltpu.make_async_remote_copy(src_ref=p_ref, dst_ref=buf.at[me], send_sem=send_sem.at[k - 1],
                                                       recv_sem=recv_sem.at[k - 1], device_id=peer, device_id_type=MESH))
        for cp in copies:
            cp.start()
        for cp in copies:
            cp.wait()
        acc = buf[0]
        for d in range(1, N_DEV):
            acc = acc + buf[d]
        o_ref[...] = acc

    vm = pl.BlockSpec(memory_space=pltpu.VMEM)
    return pl.pallas_call(
        body, name="sum_small", in_specs=[vm], out_specs=vm,
        out_shape=jax.ShapeDtypeStruct((1, SM_LEN), F32),
        scratch_shapes=[pltpu.VMEM((N_DEV, 1, SM_LEN), F32), pltpu.SemaphoreType.DMA((N_DEV - 1,)),
                        pltpu.SemaphoreType.DMA((N_DEV - 1,))],
    )(part)


def _adamw(w, g, m, v):
    m = ADAM_B1 * m + (1.0 - ADAM_B1) * g
    v = ADAM_B2 * v + (1.0 - ADAM_B2) * (g * g)
    m_hat = m / (1.0 - ADAM_B1 ** ADAM_STEP)
    v_hat = v / (1.0 - ADAM_B2 ** ADAM_STEP)
    delta = -ADAM_LR * (m_hat / (jnp.sqrt(v_hat) + ADAM_EPS) + ADAM_WD * w)
    return delta, m, v


def _adam_big(name, groups):
    k = len(groups)
    rows, cols = groups[0][2].shape
    tr = _update_rows(rows)

    def body(*refs):
        for i in range(k):
            p_ref, q_ref, w_ref, m_ref, v_ref = refs[5 * i:5 * i + 5]
            g_ref, d_ref, nm_ref, nv_ref = refs[5 * k + 4 * i:5 * k + 4 * i + 4]
            g = p_ref[...] + q_ref[...]
            g_ref[...] = g
            d_ref[...], nm_ref[...], nv_ref[...] = _adamw(w_ref[...], g, m_ref[...], v_ref[...])

    spec = pl.BlockSpec((tr, cols), lambda i: (i, 0))
    res = pl.pallas_call(
        body, name=name, grid=(rows // tr,), in_specs=[spec] * (5 * k), out_specs=[spec] * (4 * k),
        out_shape=[jax.ShapeDtypeStruct((rows, cols), F32)] * (4 * k),
        compiler_params=_params(("arbitrary",)),
    )(*[a for grp in groups for a in grp])
    return [res[4 * i:4 * i + 4] for i in range(k)]


_SMALL_AT = {"ln1_g": 0, "ln1_b": D_MODEL, "ln2_g": 2 * D_MODEL, "ln2_b": 3 * D_MODEL, "ln3_g": 4 * D_MODEL,
             "ln3_b": 5 * D_MODEL, "b_in": SM_BIN, "attn_sinks": SM_SINK, "hgrn_norm_g": SM_NG}


def _adam_small(total, w, m, v, lb_of):
    names = list(_SMALL)
    k = len(names)

    def body(*refs):
        t_ref = refs[0]
        w_refs, m_refs, v_refs = (refs[1 + i * k:1 + (i + 1) * k] for i in range(3))
        g_refs, d_refs, nm_refs, nv_refs = (refs[1 + (3 + i) * k:1 + (4 + i) * k] for i in range(4))
        for i, name in enumerate(names):
            if name == "hgrn_lb_logits":
                _, vjp = jax.vjp(lb_of, w_refs[i][0:1, :], w_refs[i][1:2, :])
                g_refs[i][0:1, :], g_refs[i][1:2, :] = vjp(t_ref[:, SM_LB:SM_LOSS])
            else:
                at = _SMALL_AT[name]
                g_refs[i][...] = t_ref[:, at:at + w_refs[i].shape[1]]
            d_refs[i][...], nm_refs[i][...], nv_refs[i][...] = _adamw(
                w_refs[i][...], g_refs[i][...], m_refs[i][...], v_refs[i][...])

    shapes = [jax.ShapeDtypeStruct(w[name].shape, F32) for name in names]
    res = pl.pallas_call(body, name="adam_small", out_shape=shapes * 4)(
        total, *[w[n] for n in names], *[m[n] for n in names], *[v[n] for n in names])
    return [dict(zip(names, res[i * k:(i + 1) * k])) for i in range(4)]


_BIG = ("ffn1_w1", "ffn1_w3", "ffn1_w2", "w_in", "w_proj_attn", "w_proj_hgrn", "w_out", "ffn2_w1", "ffn2_w3", "ffn2_w2")
_SMALL = ("ln1_g", "ln1_b", "ln2_g", "ln2_b", "ln3_g", "ln3_b", "b_in", "attn_sinks", "hgrn_norm_g", "hgrn_lb_logits")
_ORDER = ("ln1_g", "ln1_b", "ffn1_w1", "ffn1_w3", "ffn1_w2", "ln2_g", "ln2_b", "w_in", "b_in", "attn_sinks",
          "hgrn_lb_logits", "hgrn_norm_g", "w_proj_attn", "w_proj_hgrn", "w_out", "ln3_g", "ln3_b",
          "ffn2_w1", "ffn2_w3", "ffn2_w2")


_TRANSPOSED = ("ffn1_w1", "ffn1_w3", "ffn2_w1", "ffn2_w3")


def _local_view(name, arr):
    return arr[0].T if name in _TRANSPOSED else arr[0]


def _ffn_grad(name, hidden, other, comm=None):
    (dw,), comm_out = _grad_matmul(name, hidden, "cols", D_FF // FF_GRAD_PARTS, other, "shared", D_MODEL,
                                   comm=comm, parts=FF_GRAD_PARTS)
    return dw.reshape(N_CHIPS, D_FF // N_CHIPS, D_MODEL), comm_out


def kernel(x, ln1_g, ln1_b, ffn1_w1, ffn1_w3, ffn1_w2, ln2_g, ln2_b, w_in, b_in, attn_sinks, hgrn_lb_logits, hgrn_norm_g, w_proj_attn, w_proj_hgrn, w_out, ln3_g, ln3_b, ffn2_w1, ffn2_w3, ffn2_w2, loss_target, m_ln1_g, m_ln1_b, m_ffn1_w1, m_ffn1_w3, m_ffn1_w2, m_ln2_g, m_ln2_b, m_w_in, m_b_in, m_attn_sinks, m_hgrn_lb_logits, m_hgrn_norm_g, m_w_proj_attn, m_w_proj_hgrn, m_w_out, m_ln3_g, m_ln3_b, m_ffn2_w1, m_ffn2_w3, m_ffn2_w2, v_ln1_g, v_ln1_b, v_ffn1_w1, v_ffn1_w3, v_ffn1_w2, v_ln2_g, v_ln2_b, v_w_in, v_b_in, v_attn_sinks, v_hgrn_lb_logits, v_hgrn_norm_g, v_w_proj_attn, v_w_proj_hgrn, v_w_out, v_ln3_g, v_ln3_b, v_ffn2_w1, v_ffn2_w3, v_ffn2_w2):
    w = dict(ln1_g=ln1_g, ln1_b=ln1_b, ffn1_w1=ffn1_w1, ffn1_w3=ffn1_w3, ffn1_w2=ffn1_w2, ln2_g=ln2_g, ln2_b=ln2_b,
             w_in=w_in, b_in=b_in, attn_sinks=attn_sinks, hgrn_lb_logits=hgrn_lb_logits, hgrn_norm_g=hgrn_norm_g,
             w_proj_attn=w_proj_attn, w_proj_hgrn=w_proj_hgrn, w_out=w_out, ln3_g=ln3_g, ln3_b=ln3_b,
             ffn2_w1=ffn2_w1, ffn2_w3=ffn2_w3, ffn2_w2=ffn2_w2)
    mom = dict(ln1_g=m_ln1_g, ln1_b=m_ln1_b, ffn1_w1=m_ffn1_w1, ffn1_w3=m_ffn1_w3, ffn1_w2=m_ffn1_w2, ln2_g=m_ln2_g,
               ln2_b=m_ln2_b, w_in=m_w_in, b_in=m_b_in, attn_sinks=m_attn_sinks, hgrn_lb_logits=m_hgrn_lb_logits,
               hgrn_norm_g=m_hgrn_norm_g, w_proj_attn=m_w_proj_attn, w_proj_hgrn=m_w_proj_hgrn, w_out=m_w_out,
               ln3_g=m_ln3_g, ln3_b=m_ln3_b, ffn2_w1=m_ffn2_w1, ffn2_w3=m_ffn2_w3, ffn2_w2=m_ffn2_w2)
    var = dict(ln1_g=v_ln1_g, ln1_b=v_ln1_b, ffn1_w1=v_ffn1_w1, ffn1_w3=v_ffn1_w3, ffn1_w2=v_ffn1_w2, ln2_g=v_ln2_g,
               ln2_b=v_ln2_b, w_in=v_w_in, b_in=v_b_in, attn_sinks=v_attn_sinks, hgrn_lb_logits=v_hgrn_lb_logits,
               hgrn_norm_g=v_hgrn_norm_g, w_proj_attn=v_w_proj_attn, w_proj_hgrn=v_w_proj_hgrn, w_out=v_w_out,
               ln3_g=v_ln3_g, ln3_b=v_ln3_b, ffn2_w1=v_ffn2_w1, ffn2_w3=v_ffn2_w3, ffn2_w2=v_ffn2_w2)

    n_tok = x.shape[1]
    x0 = x.reshape(n_tok, D_MODEL)
    target = loss_target.reshape(n_tok, D_MODEL)

    shard = {k: _bf(_local_view(k, w[k])) for k in _BIG}
    gather = lambda keys: _GatherWeights([shard[k] for k in keys])
    slots = {}
    exchange = lambda keys: _ExchangeGrads([big[k] for k in keys])
    ffn1_keys = ("ffn1_w1", "ffn1_w3", "ffn1_w2")
    mixer_keys = ("w_in", "w_proj_attn", "w_proj_hgrn", "w_out")
    ffn2_keys = ("ffn2_w1", "ffn2_w3", "ffn2_w2")
    whole = lambda ts: [t.reshape(D_FF, D_MODEL) for t in ts]
    f1 = whole(_run_comm("gather_ffn1", gather(ffn1_keys)))

    tabs = _rope_tables(n_tok)
    lb, lb_of = _lb_fwd(hgrn_lb_logits)
    (z1, x1, x0b, h1_1, h3_1, x1b), (w_in_g, w_pa, w_ph, w_o) = _ffn_fwd(
        "ffn1_fwd", x0, *f1, ln1_g, ln1_b, comm=gather(mixer_keys))
    w_pa, w_ph, w_o = (t.reshape(D_MODEL, D_MODEL) for t in (w_pa, w_ph, w_o))
    w_in_g = _reorder_w_in("w_in_cols", w_in_g, True)
    (proj,), f2 = _in_proj(x1b, w_in_g, _to_kernel_cols(b_in), comm=gather(ffn2_keys))
    f2 = whole(f2)
    y_attn = _attn_fwd(proj, tabs, attn_sinks)
    y_hgrn, states = _hgrn_fwd(proj, lb, hgrn_norm_g)
    z2, x2, proj_a, proj_h = _mix_fwd(y_attn, y_hgrn, proj, x1, w_pa, w_ph, w_o, ln2_g, ln2_b)
    (z3, dy, x2b, h1_2, h3_2, loss_part), _ = _ffn_fwd("ffn2_fwd", x2, *f2, ln3_g, ln3_b, target=target)

    big = {}
    small = {}
    (dx2, a2, dh1_2, dh3_2, df2, small["ln3_g"], small["ln3_b"]), _ = _ffn_bwd(
        "ffn2_bwd", h1_2, h3_2, z3, dy, *f2, ln3_g, ln3_b)
    big["ffn2_w1"], _ = _ffn_grad("ffn2_dw1", dh1_2, x2b)
    big["ffn2_w3"], _ = _ffn_grad("ffn2_dw3", dh3_2, x2b)
    big["ffn2_w2"], _ = _ffn_grad("ffn2_dw2", a2, df2)
    (dz2, dz2b, merged, dya, dyh, dy_attn, dy_hgrn, dproj, small["ln2_g"], small["ln2_b"]) = _mix_bwd(
        dx2, z2, proj_a, proj_h, proj, w_pa, w_ph, w_o, ln2_g, ln2_b)
    for key, name, lhs, rhs in (("w_out", "dw_out", merged, dz2b), ("w_proj_attn", "dw_proj_attn", y_attn, dya),
                                ("w_proj_hgrn", "dw_proj_hgrn", y_hgrn, dyh)):
        (dw,), _ = _grad_matmul(name, lhs, "cols", D_MODEL, rhs, "shared", D_MODEL, parts=1, rows=2 * GRAD_ROWS)
        big[key] = dw.reshape(N_CHIPS, PROJ_SHARD, D_MODEL)
    (dproj, dlb, small["hgrn_norm_g"]), early = _hgrn_bwd(
        proj, lb, hgrn_norm_g, states, dy_hgrn, dproj, comm=exchange(ffn2_keys[:2]))
    slots.update(zip(ffn2_keys[:2], early))
    attn_hosted = ffn2_keys[2:] + mixer_keys[1:]
    (dproj, dsinks), early = _attn_bwd(proj, dy_attn, tabs, attn_sinks, dproj, comm=exchange(attn_hosted))
    slots.update(zip(attn_hosted, early))
    (dw_in, db_in), _ = _grad_matmul("dw_in", x1b, "shared", D_MODEL, dproj, "cols", IN_SHARD, colsum=True)
    big["w_in"] = _reorder_w_in("dw_in_cols", dw_in, False)
    small["b_in"] = _from_kernel_cols(db_in)
    (dx1,), (w_in_slots,) = _in_proj_dx(
        dproj, w_in_g, dz2, comm=_ExchangeGrads([big["w_in"]], rows=(0, W_IN_ROWS_FIRST)))
    (grad_x, a1, dh1_1, dh3_1, df1, small["ln1_g"], small["ln1_b"]), _ = _ffn_bwd(
        "ffn1_bwd", h1_1, h3_1, z1, dx1, *f1, ln1_g, ln1_b)
    big["ffn1_w1"], (slots["w_in"],) = _ffn_grad("ffn1_dw1", dh1_1, x0b, comm=_ExchangeGrads(
        [big["w_in"]], rows=(W_IN_ROWS_FIRST, D_MODEL - W_IN_ROWS_FIRST), into=[w_in_slots]))
    big["ffn1_w3"], (slots["ffn1_w1"],) = _ffn_grad("ffn1_dw3", dh3_1, x0b, comm=exchange(("ffn1_w1",)))
    big["ffn1_w2"], (slots["ffn1_w3"],) = _ffn_grad("ffn1_dw2", a1, df1, comm=exchange(("ffn1_w3",)))

    last = "ffn1_w2"
    groups = [[k for k in ffn1_keys + ffn2_keys if k != last], list(mixer_keys[1:]), ["w_in"]]
    partial = {}
    for keys in groups:
        partial.update(zip(keys, _sum_slots("sum_" + keys[0], [slots[k] for k in keys])))
    swapped_keys = [k for keys in groups for k in keys]
    moved = _run_comm("swap_and_exchange_last",
                      _Together(_SwapWithSibling([partial[k] for k in swapped_keys]), exchange((last,))))
    from_sibling = dict(zip(swapped_keys, moved))
    (partial[last],) = _sum_slots("sum_" + last, [moved[-1]])
    (from_sibling[last],) = _run_comm("swap_last", _SwapWithSibling([partial[last]]))
    groups[0].append(last)

    outs = {"grad": {}, "delta": {}, "m": {}, "v": {}}
    for keys in groups:
        res = _adam_big("adam_" + keys[0], [(partial[k], from_sibling[k], _local_view(k, w[k]), _local_view(k, mom[k]),
                                             _local_view(k, var[k])) for k in keys])
        for k, four in zip(keys, res):
            for kind, r in zip(("grad", "delta", "m", "v"), four):
                outs[kind][k] = (r.T if k in _TRANSPOSED else r).reshape(w[k].shape)

    total = _sum_small(jnp.concatenate(
        [small[k] for k in ("ln1_g", "ln1_b", "ln2_g", "ln2_b", "ln3_g", "ln3_b", "b_in")]
        + [dsinks, small["hgrn_norm_g"], dlb, loss_part], axis=1))
    for kind, r in zip(("grad", "delta", "m", "v"), _adam_small(total, w, mom, var, lb_of)):
        outs[kind].update(r)
    loss = total[0, SM_LOSS]

    return (loss, grad_x.reshape(x.shape), *[outs["grad"][k] for k in _ORDER], *[outs["delta"][k] for k in _ORDER],
            *[outs["m"][k] for k in _ORDER], *[outs["v"][k] for k in _ORDER])
```

```python
import functools

import jax
import jax.numpy as jnp
from jax import lax
from jax.experimental import pallas as pl
from jax.experimental.pallas import tpu as pltpu

F32 = jnp.float32
BF16 = jnp.bfloat16

D_MODEL = 1024
N_Q_HEADS = 16
N_KV_HEADS = 4
HEAD_DIM = 64
ATTN_BLOCK = 128
ROPE_THETA = 500000.0
ROPE_DIM = HEAD_DIM // 4
HGRN_HEADS = 8
HGRN_DK = 128
HGRN_CHUNK = 64
D_FF = 2816
D_IN = 7680
DEEPNORM_ALPHA = 2 ** 0.25
LN_EPS = 1e-5
RMS_EPS = 1e-6
NEG_INF = -1e30

ADAM_LR = 0.001
ADAM_B1 = 0.9
ADAM_B2 = 0.999
ADAM_EPS = 1e-08
ADAM_WD = 0.01
ADAM_STEP = 10

N_CHIPS = 4
N_DEV = 8
LANES = 128
FF_GRAD_PARTS = 2
FFN_TILE = 256
IN_SHARD = D_IN // N_CHIPS
PROJ_SHARD = D_MODEL // N_CHIPS
ROW_TILE = 512
GRAD_ROWS = 2048
MIX_TILE = 256
UPDATE_ROWS = 128
HGRN_CHUNKS_PER_STEP = 32
W_IN_ROWS_FIRST = 704
VMEM_LIMIT = 56 * 1024 * 1024

GATES_WIDTH = 2 * D_MODEL
HGRN_HEAD_WIDTH = 4 * HGRN_DK
ATTN_WIDTH = D_MODEL + 2 * N_KV_HEADS * HEAD_DIM
COL_HGRN = GATES_WIDTH // HGRN_HEAD_WIDTH
COL_ATTN = (GATES_WIDTH + HGRN_HEADS * HGRN_HEAD_WIDTH) // ATTN_WIDTH
COL_Q = (GATES_WIDTH + HGRN_HEADS * HGRN_HEAD_WIDTH) // D_MODEL
COL_K = (GATES_WIDTH + HGRN_HEADS * HGRN_HEAD_WIDTH + D_MODEL) // (N_KV_HEADS * HEAD_DIM)
COL_V = COL_K + 1


def _to_kernel_cols(a):
    lead = a.shape[:-1]
    qkv, hg, gates = a[..., :ATTN_WIDTH], a[..., ATTN_WIDTH:D_IN - GATES_WIDTH], a[..., D_IN - GATES_WIDTH:]
    hg = jnp.swapaxes(hg.reshape(*lead, 4, HGRN_HEADS, HGRN_DK), -3, -2).reshape(*lead, -1)
    return jnp.concatenate([gates, hg, qkv], axis=-1)


def _from_kernel_cols(a):
    lead = a.shape[:-1]
    gates, hg, qkv = a[..., :GATES_WIDTH], a[..., GATES_WIDTH:D_IN - ATTN_WIDTH], a[..., D_IN - ATTN_WIDTH:]
    hg = jnp.swapaxes(hg.reshape(*lead, HGRN_HEADS, 4, HGRN_DK), -3, -2).reshape(*lead, -1)
    return jnp.concatenate([qkv, hg, gates], axis=-1)

SM_LN = 0
SM_BIN = 6 * D_MODEL
SM_SINK = SM_BIN + D_IN
SM_NG = SM_SINK + LANES
SM_LB = SM_NG + LANES
SM_LOSS = SM_LB + D_MODEL
SM_LEN = SM_LOSS + LANES

MESH = pl.DeviceIdType.MESH


def _mm(a, b):
    return lax.dot_general(a, b, (((1,), (0,)), ((), ())), preferred_element_type=F32)


def _mm_nt(a, b):
    return lax.dot_general(a, b, (((1,), (1,)), ((), ())), preferred_element_type=F32)


def _mm_tn(a, b):
    return lax.dot_general(a, b, (((0,), (0,)), ((), ())), preferred_element_type=F32)


def _bf(v):
    return v.astype(BF16)


def _sig(v):
    return jax.nn.sigmoid(v)


def _ln(z, g, b):
    mu = jnp.mean(z, axis=-1, keepdims=True)
    zc = z - mu
    var = jnp.mean(zc * zc, axis=-1, keepdims=True)
    return zc * lax.rsqrt(var + LN_EPS) * g + b


def _swiglu_act(h1, h3):
    return (h1 * _sig(h1)) * h3


def _params(sem=None):
    return pltpu.CompilerParams(dimension_semantics=sem, vmem_limit_bytes=VMEM_LIMIT)


def _full(shape):
    nd = len(shape)
    return pl.BlockSpec(shape, lambda *_: (0,) * nd)


def _update_rows(rows):
    return max(t for t in range(8, UPDATE_ROWS + 1, 8) if rows % t == 0)


def _resident(shape):
    nd = len(shape)
    return pl.BlockSpec(shape, lambda *_: (0,) * nd, pipeline_mode=pl.Buffered(1))


def _ffn_fwd(name, x, w1t, w3t, w2, g, b, target=None, comm=None):
    n = x.shape[0]
    tm = min(ROW_TILE, n)
    final = target is not None

    def body(*refs):
        if final:
            x_ref, w1_ref, w3_ref, w2_ref, g_ref, b_ref, t_ref, z_ref, o_ref, xb_ref, h1_ref, h3_ref, loss_ref = refs
        else:
            x_ref, w1_ref, w3_ref, w2_ref, g_ref, b_ref, z_ref, o_ref, xb_ref, h1_ref, h3_ref, ob_ref = refs
        xb = _bf(x_ref[...])
        xb_ref[...] = xb
        h1 = _mm_nt(xb, w1_ref[...])
        h3 = _mm_nt(xb, w3_ref[...])
        h1_ref[...] = _bf(h1)
        h3_ref[...] = _bf(h3)
        z = DEEPNORM_ALPHA * x_ref[...] + 0.5 * _mm(_bf(_swiglu_act(h1, h3)), w2_ref[...])
        z_ref[...] = z
        y = _ln(z, g_ref[...], b_ref[...])
        if final:
            e = y - t_ref[...]

            @pl.when(pl.program_id(0) == 0)
            def _():
                loss_ref[...] = jnp.zeros_like(loss_ref)

            loss_ref[...] += jnp.sum(e * e) * (0.5 / D_MODEL)
            o_ref[...] = e * (1.0 / D_MODEL)
        else:
            o_ref[...] = y
            ob_ref[...] = _bf(y)

    row = pl.BlockSpec((tm, D_MODEL), lambda i: (i, 0))
    hid = pl.BlockSpec((tm, D_FF), lambda i: (i, 0))
    wres = _resident((D_FF, D_MODEL))
    vec = _full((1, D_MODEL))
    in_specs = [row, wres, wres, wres, vec, vec]
    args = [x, w1t, w3t, w2, g, b]
    hid_shape = jax.ShapeDtypeStruct((n, D_FF), BF16)
    out_specs = [row, row, row, hid, hid]
    out_shape = ([jax.ShapeDtypeStruct((n, D_MODEL), F32)] * 2 + [jax.ShapeDtypeStruct((n, D_MODEL), BF16)]
                 + [hid_shape] * 2)
    if final:
        in_specs.append(row)
        args.append(target)
        out_specs.append(_full((1, LANES)))
        out_shape.append(jax.ShapeDtypeStruct((1, LANES), F32))
    else:
        out_specs.append(row)
        out_shape.append(jax.ShapeDtypeStruct((n, D_MODEL), BF16))
    return _hosted(
        body, comm, name=name, grid=(n // tm,), in_specs=in_specs, out_specs=out_specs, out_shape=out_shape,
        scratch_shapes=[], compiler_params=_params(("arbitrary",)), args=args)


def _ffn_bwd(name, h1s, h3s, z, dout, w1t, w3t, w2, g, b, comm=None):
    n = z.shape[0]
    tm = min(FFN_TILE, n)

    def body(h1_ref, h3_ref, z_ref, do_ref, w1_ref, w3_ref, w2_ref, g_ref, b_ref,
             dx_ref, a_ref, dh1_ref, dh3_ref, df_ref, dg_ref, db_ref):
        _, vjp = jax.vjp(_ln, z_ref[...], g_ref[...], b_ref[...])
        dz, dg, db = vjp(do_ref[...])

        @pl.when(pl.program_id(0) == 0)
        def _():
            dg_ref[...] = jnp.zeros_like(dg_ref)
            db_ref[...] = jnp.zeros_like(db_ref)

        dg_ref[...] += dg
        db_ref[...] += db
        df = _bf(0.5 * dz)
        df_ref[...] = df
        a, act_vjp = jax.vjp(_swiglu_act, h1_ref[...].astype(F32), h3_ref[...].astype(F32))
        dh1, dh3 = act_vjp(_mm_nt(df, w2_ref[...]))
        dh1 = _bf(dh1)
        dh3 = _bf(dh3)
        a_ref[...] = _bf(a)
        dh1_ref[...] = dh1
        dh3_ref[...] = dh3
        dx_ref[...] = DEEPNORM_ALPHA * dz + _mm(dh1, w1_ref[...]) + _mm(dh3, w3_ref[...])

    row = pl.BlockSpec((tm, D_MODEL), lambda i: (i, 0))
    hid = pl.BlockSpec((tm, D_FF), lambda i: (i, 0))
    wres = _resident((D_FF, D_MODEL))
    vec = _full((1, D_MODEL))
    hid_shape = jax.ShapeDtypeStruct((n, D_FF), BF16)
    return _hosted(
        body, comm, name=name, grid=(n // tm,),
        in_specs=[hid, hid, row, row, wres, wres, wres, vec, vec],
        out_specs=[row, hid, hid, hid, row, vec, vec],
        out_shape=[jax.ShapeDtypeStruct((n, D_MODEL), F32), hid_shape, hid_shape, hid_shape,
                   jax.ShapeDtypeStruct((n, D_MODEL), BF16),
                   jax.ShapeDtypeStruct((1, D_MODEL), F32), jax.ShapeDtypeStruct((1, D_MODEL), F32)],
        scratch_shapes=[], compiler_params=_params(("arbitrary",)),
        args=[h1s, h3s, z, dout, w1t, w3t, w2, g, b])


def _operand_spec(arr, mode, tn, width, parts):
    if mode == "shared":
        return pl.BlockSpec((tn, width), lambda s, k: (k, 0))
    assert mode == "cols" and arr.shape[1] == parts * width
    return pl.BlockSpec((tn, width), lambda s, k: (k, s))


def _grad_matmul(name, a, a_mode, ka, b, b_mode, kb, colsum=False, comm=None, parts=N_CHIPS):
    n = a.shape[-2]
    tn = min(GRAD_ROWS, n)
    nk = n // tn

    def body(*refs):
        if colsum:
            a_ref, b_ref, o_ref, cs_ref, acc = refs
        else:
            a_ref, b_ref, o_ref, acc = refs
        k = pl.program_id(1)
        av = a_ref[...]
        bv = b_ref[...]

        @pl.when(k == 0)
        def _():
            acc[...] = jnp.zeros_like(acc)
            if colsum:
                cs_ref[...] = jnp.zeros_like(cs_ref)

        acc[...] += _mm_tn(av, bv)
        if colsum:
            cs_ref[...] += jnp.sum(bv.astype(F32), axis=0, keepdims=True)

        @pl.when(k == nk - 1)
        def _():
            o_ref[0] = _bf(acc[...])

    out_specs = [pl.BlockSpec((1, ka, kb), lambda s, k: (s, 0, 0))]
    out_shape = [jax.ShapeDtypeStruct((parts, ka, kb), BF16)]
    if colsum:
        out_specs.append(pl.BlockSpec((1, kb), lambda s, k: (0, s)))
        out_shape.append(jax.ShapeDtypeStruct((1, parts * kb), F32))
    return _hosted(
        body, comm, name=name, grid=(parts, nk),
        in_specs=[_operand_spec(a, a_mode, tn, ka, parts), _operand_spec(b, b_mode, tn, kb, parts)],
        out_specs=out_specs, out_shape=out_shape,
        scratch_shapes=[pltpu.VMEM((ka, kb), F32)],
        compiler_params=_params(("arbitrary", "arbitrary")), args=[a, b])


def _kernel_block_of(ref_block):
    attn_blocks, gate_blocks = ATTN_WIDTH // LANES, GATES_WIDTH // LANES
    hgrn_blocks = HGRN_HEADS * HGRN_HEAD_WIDTH // LANES
    if ref_block < attn_blocks:
        return gate_blocks + hgrn_blocks + ref_block
    if ref_block < attn_blocks + hgrn_blocks:
        kind, head = divmod(ref_block - attn_blocks, HGRN_HEADS)
        return gate_blocks + head * (HGRN_HEAD_WIDTH // LANES) + kind
    return ref_block - attn_blocks - hgrn_blocks


def _reorder_w_in(name, w4, to_kernel_order):
    per = IN_SHARD // LANES
    tr = MIX_TILE

    def body(i_ref, o_ref):
        for g in range(D_IN // LANES):
            s, b = divmod(g, per)
            k = _kernel_block_of(g)
            if to_kernel_order:
                o_ref[:, k * LANES:(k + 1) * LANES] = i_ref[s, :, b * LANES:(b + 1) * LANES]
            else:
                ks, kb = divmod(k, per)
                o_ref[s, :, b * LANES:(b + 1) * LANES] = i_ref[ks, :, kb * LANES:(kb + 1) * LANES]

    in_spec = pl.BlockSpec((N_CHIPS, tr, IN_SHARD), lambda i: (0, i, 0))
    if to_kernel_order:
        out_spec, out_shape = pl.BlockSpec((tr, D_IN), lambda i: (i, 0)), (D_MODEL, D_IN)
    else:
        out_spec, out_shape = in_spec, (N_CHIPS, D_MODEL, IN_SHARD)
    return pl.pallas_call(
        body, name=name, grid=(D_MODEL // tr,), in_specs=[in_spec], out_specs=out_spec,
        out_shape=jax.ShapeDtypeStruct(out_shape, w4.dtype), compiler_params=_params(("arbitrary",)),
    )(w4)


def _in_proj(x1, w_in_g, b_in, comm=None):
    n = x1.shape[0]
    tm = min(ROW_TILE, n)

    def body(x_ref, w_ref, b_ref, o_ref):
        xv = x_ref[...]
        for j in range(N_CHIPS):
            cols = slice(j * IN_SHARD, (j + 1) * IN_SHARD)
            o_ref[:, cols] = _mm(xv, w_ref[:, cols]) + b_ref[:, cols]

    return _hosted(
        body, comm, name="in_proj", grid=(n // tm,),
        in_specs=[pl.BlockSpec((tm, D_MODEL), lambda i: (i, 0)),
                  _resident((D_MODEL, D_IN)), _full((1, D_IN))],
        out_specs=[pl.BlockSpec((tm, D_IN), lambda i: (i, 0))],
        out_shape=[jax.ShapeDtypeStruct((n, D_IN), F32)],
        scratch_shapes=[],
        compiler_params=_params(("arbitrary",)), args=[x1, w_in_g, b_in])


def _in_proj_dx(dproj, w_in_g, dz2, comm=None):
    n = dproj.shape[0]
    tm = min(ROW_TILE, n)

    def body(dp_ref, w_ref, dz_ref, o_ref):
        dx = DEEPNORM_ALPHA * dz_ref[...]
        for j in range(N_CHIPS):
            cols = slice(j * IN_SHARD, (j + 1) * IN_SHARD)
            dx = dx + _mm_nt(dp_ref[:, cols], w_ref[:, cols])
        o_ref[...] = dx

    return _hosted(
        body, comm, name="in_proj_dx", grid=(n // tm,),
        in_specs=[pl.BlockSpec((tm, D_IN), lambda i: (i, 0)),
                  _resident((D_MODEL, D_IN)),
                  pl.BlockSpec((tm, D_MODEL), lambda i: (i, 0))],
        out_specs=[pl.BlockSpec((tm, D_MODEL), lambda i: (i, 0))],
        out_shape=[jax.ShapeDtypeStruct((n, D_MODEL), F32)],
        scratch_shapes=[],
        compiler_params=_params(("arbitrary",)), args=[dproj, w_in_g, dz2])


def _rope_tables(seq_len):
    pos = jnp.arange(seq_len, dtype=F32)
    inv_freq = ROPE_THETA ** (-jnp.arange(0, ROPE_DIM, 2, dtype=F32) / ROPE_DIM)
    ang = pos[:, None] * inv_freq[None, :]
    cos, sin = jnp.cos(ang), jnp.sin(ang)
    half = ROPE_DIM // 2
    rest = HEAD_DIM - ROPE_DIM
    ones = jnp.ones((seq_len, rest), F32)
    zeros = jnp.zeros((seq_len, rest), F32)
    zh = jnp.zeros((seq_len, half), F32)
    c = jnp.concatenate([cos, cos, ones], axis=1)
    sa = jnp.concatenate([-sin, zh, zeros], axis=1)
    sb = jnp.concatenate([zh, sin, zeros], axis=1)
    reps = LANES // HEAD_DIM
    return tuple(jnp.tile(t, (1, reps)) for t in (c, sa, sb))


def _rope(t, c, sa, sb):
    w = t.shape[1]
    reps = w // LANES
    half = ROPE_DIM // 2
    return (t * jnp.tile(c, (1, reps)) + pltpu.roll(t, w - half, 1) * jnp.tile(sa, (1, reps))
            + pltpu.roll(t, half, 1) * jnp.tile(sb, (1, reps)))


def _rope_transposed(g, c, sa, sb):
    w = g.shape[1]
    reps = w // LANES
    half = ROPE_DIM // 2
    return (g * jnp.tile(c, (1, reps)) + pltpu.roll(g * jnp.tile(sa, (1, reps)), half, 1)
            + pltpu.roll(g * jnp.tile(sb, (1, reps)), w - half, 1))


GROUP = N_Q_HEADS // N_KV_HEADS


def _both_halves(t_pair, which):
    lo = lax.broadcasted_iota(jnp.int32, t_pair.shape, 1) < HEAD_DIM
    swapped = pltpu.roll(t_pair, HEAD_DIM, 1)
    return _bf(jnp.where(lo, t_pair, swapped) if which == 0 else jnp.where(lo, swapped, t_pair))


def _stack_heads(ref_or_val, kh):
    lo = lax.broadcasted_iota(jnp.int32, (ATTN_BLOCK, LANES), 1) < HEAD_DIM
    rows = []
    for gp in range(GROUP // 2):
        pair = kh * (GROUP // 2) + gp
        t = ref_or_val[:, pair * LANES:(pair + 1) * LANES]
        rows += [jnp.where(lo, t, jnp.zeros_like(t)), jnp.where(lo, jnp.zeros_like(t), t)]
    return jnp.concatenate(rows, axis=0)


def _unstack_pairs(stacked):
    lo = lax.broadcasted_iota(jnp.int32, (ATTN_BLOCK, LANES), 1) < HEAD_DIM
    b = ATTN_BLOCK
    return [jnp.where(lo, stacked[2 * gp * b:(2 * gp + 1) * b], stacked[(2 * gp + 1) * b:(2 * gp + 2) * b])
            for gp in range(GROUP // 2)]


def _attn_mask_t(n):
    cols = GROUP * ATTN_BLOCK
    kj = lax.broadcasted_iota(jnp.int32, (2 * ATTN_BLOCK, cols), 0)
    qi = lax.broadcasted_iota(jnp.int32, (2 * ATTN_BLOCK, cols), 1) % ATTN_BLOCK
    dist = qi + ATTN_BLOCK - kj
    return (dist >= 0) & (dist < ATTN_BLOCK) & (n * ATTN_BLOCK + kj - ATTN_BLOCK >= 0)


def _sink_row(sink_ref, kh):
    col = lax.broadcasted_iota(jnp.int32, (1, GROUP * ATTN_BLOCK), 1)
    row = jnp.full((1, GROUP * ATTN_BLOCK), sink_ref[0, kh * GROUP + GROUP - 1], F32)
    for i in reversed(range(GROUP - 1)):
        row = jnp.where(col < (i + 1) * ATTN_BLOCK, sink_ref[0, kh * GROUP + i], row)
    return row


def _attn_probs_t(q_masked, k_sel, mask_t, sink):
    s = _mm_nt(k_sel, q_masked) * (HEAD_DIM ** -0.5)
    s = jnp.where(mask_t, s, NEG_INF)
    m = jnp.maximum(jnp.max(s, axis=0, keepdims=True), sink)
    p = jnp.exp(s - m)
    e_sink = jnp.exp(sink - m)
    denom = jnp.sum(p, axis=0, keepdims=True) + e_sink
    return p / denom, e_sink / denom


def _attn_fwd(proj, tabs, sinks):
    n_tok = proj.shape[0]
    nb = n_tok // ATTN_BLOCK

    def body(q_ref, k_ref, v_ref, c_ref, sa_ref, sb_ref, sink_ref, y_ref, kprev, vprev):
        n = pl.program_id(0)

        @pl.when(n == 0)
        def _():
            kprev[...] = jnp.zeros_like(kprev)
            vprev[...] = jnp.zeros_like(vprev)

        c, sa, sb = c_ref[...], sa_ref[...], sb_ref[...]
        qr = _bf(_rope(q_ref[...], c, sa, sb))
        kr = _rope(k_ref[...], c, sa, sb)
        vc = v_ref[...]
        kk = jnp.concatenate([kprev[...], kr], axis=0)
        vv = jnp.concatenate([vprev[...], vc], axis=0)
        kprev[...] = kr
        vprev[...] = vc
        mask = _attn_mask_t(n)
        for kh in range(N_KV_HEADS):
            r, which = divmod(kh, 2)
            kb = _both_halves(kk[:, r * LANES:(r + 1) * LANES], which)
            vb = _both_halves(vv[:, r * LANES:(r + 1) * LANES], which)
            probs, _ = _attn_probs_t(_stack_heads(qr, kh), kb, mask, _sink_row(sink_ref, kh))
            for gp, out in enumerate(_unstack_pairs(_mm_tn(_bf(probs), vb))):
                pair = kh * (GROUP // 2) + gp
                y_ref[:, pair * LANES:(pair + 1) * LANES] = _bf(out)

    blk = lambda width, col: pl.BlockSpec((ATTN_BLOCK, width), lambda n: (n, col))
    tab = pl.BlockSpec((ATTN_BLOCK, LANES), lambda n: (n, 0))
    kvw = N_KV_HEADS * HEAD_DIM
    return pl.pallas_call(
        body, name="attn_fwd", grid=(nb,),
        in_specs=[blk(D_MODEL, COL_Q), blk(kvw, COL_K), blk(kvw, COL_V), tab, tab, tab,
                  pl.BlockSpec(memory_space=pltpu.SMEM)],
        out_specs=pl.BlockSpec((ATTN_BLOCK, D_MODEL), lambda n: (n, 0)),
        out_shape=jax.ShapeDtypeStruct((n_tok, D_MODEL), BF16),
        scratch_shapes=[pltpu.VMEM((ATTN_BLOCK, kvw), F32), pltpu.VMEM((ATTN_BLOCK, kvw), F32)],
        compiler_params=_params(("arbitrary",)),
    )(proj, proj, proj, *tabs, sinks)


def _attn_bwd(proj, dy, tabs, sinks, dproj, comm=None):
    n_tok = proj.shape[0]
    nb = n_tok // ATTN_BLOCK
    kvw = N_KV_HEADS * HEAD_DIM

    def body(q_ref, k_ref, v_ref, do_ref, c_ref, sa_ref, sb_ref, cp_ref, sap_ref, sbp_ref, sink_ref, _,
             dqkv_ref, dsink_ref, kprev, vprev, dkc, dvc, dqc):
        n = pl.program_id(0)

        @pl.when(n == 0)
        def _():
            for ref in (kprev, vprev, dkc, dvc, dqc, dsink_ref):
                ref[...] = jnp.zeros_like(ref)

        prev_tabs = (cp_ref[...], sap_ref[...], sbp_ref[...])

        @pl.when(n < nb)
        def _():
            c, sa, sb = c_ref[...], sa_ref[...], sb_ref[...]
            qr = _bf(_rope(q_ref[...], c, sa, sb))
            kr = _rope(k_ref[...], c, sa, sb)
            vc = v_ref[...]
            kk = jnp.concatenate([kprev[...], kr], axis=0)
            vv = jnp.concatenate([vprev[...], vc], axis=0)
            kprev[...] = kr
            vprev[...] = vc
            mask = _attn_mask_t(n)
            lane = lax.broadcasted_iota(jnp.int32, (1, LANES), 1)
            lo2 = lax.broadcasted_iota(jnp.int32, (2 * ATTN_BLOCK, LANES), 1) < HEAD_DIM
            dsink = jnp.zeros((1, LANES), F32)
            dq_pairs = []
            dk_full = []
            dv_full = []
            for kh in range(N_KV_HEADS):
                r, which = divmod(kh, 2)
                kb = _both_halves(kk[:, r * LANES:(r + 1) * LANES], which)
                vb = _both_halves(vv[:, r * LANES:(r + 1) * LANES], which)
                qs = _stack_heads(qr, kh)
                dos = _stack_heads(do_ref, kh)
                probs, p_sink = _attn_probs_t(qs, kb, mask, _sink_row(sink_ref, kh))
                dp = _mm_nt(vb, dos)
                delta = jnp.sum(probs * dp, axis=0, keepdims=True)
                ds = _bf(probs * (dp - delta) * (HEAD_DIM ** -0.5))
                sink_terms = p_sink * delta
                for i in range(GROUP):
                    head_sum = jnp.sum(sink_terms[:, i * ATTN_BLOCK:(i + 1) * ATTN_BLOCK])
                    dsink = dsink + jnp.where(lane == kh * GROUP + i, -head_sum, 0.0)
                dq_pairs += _unstack_pairs(_mm_tn(ds, kb))
                dk_acc = _mm(ds, qs)
                dv_acc = _mm(_bf(probs), dos)
                dk_full.append(dk_acc + pltpu.roll(dk_acc, HEAD_DIM, 1))
                dv_full.append(dv_acc + pltpu.roll(dv_acc, HEAD_DIM, 1))
            dk_pairs = [jnp.where(lo2, dk_full[2 * r], dk_full[2 * r + 1]) for r in range(N_KV_HEADS // 2)]
            dv_pairs = [jnp.where(lo2, dv_full[2 * r], dv_full[2 * r + 1]) for r in range(N_KV_HEADS // 2)]
            dsink_ref[...] += dsink
            dqkv_ref[:, :D_MODEL] = _bf(dqc[...])
            dqc[...] = _rope_transposed(jnp.concatenate(dq_pairs, axis=1), c, sa, sb)
            dk_all = jnp.concatenate(dk_pairs, axis=1)
            dv_all = jnp.concatenate(dv_pairs, axis=1)
            dqkv_ref[:, D_MODEL:D_MODEL + kvw] = _bf(_rope_transposed(dkc[...] + dk_all[:ATTN_BLOCK], *prev_tabs))
            dqkv_ref[:, D_MODEL + kvw:] = _bf(dvc[...] + dv_all[:ATTN_BLOCK])
            dkc[...] = dk_all[ATTN_BLOCK:]
            dvc[...] = dv_all[ATTN_BLOCK:]

        @pl.when(n == nb)
        def _():
            dqkv_ref[:, :D_MODEL] = _bf(dqc[...])
            dqkv_ref[:, D_MODEL:D_MODEL + kvw] = _bf(_rope_transposed(dkc[...], *prev_tabs))
            dqkv_ref[:, D_MODEL + kvw:] = _bf(dvc[...])

    cur = lambda n: jnp.minimum(n, nb - 1)
    prev = lambda n: jnp.maximum(n - 1, 0)
    blk = lambda width, col: pl.BlockSpec((ATTN_BLOCK, width), lambda n: (cur(n), col))
    tab = pl.BlockSpec((ATTN_BLOCK, LANES), lambda n: (cur(n), 0))
    tabp = pl.BlockSpec((ATTN_BLOCK, LANES), lambda n: (prev(n), 0))
    return _hosted(
        body, comm, name="attn_bwd", grid=(nb + 1,),
        in_specs=[blk(D_MODEL, COL_Q), blk(kvw, COL_K), blk(kvw, COL_V), blk(D_MODEL, 0), tab, tab, tab, tabp, tabp, tabp,
                  pl.BlockSpec(memory_space=pltpu.SMEM), pl.BlockSpec(memory_space=pl.ANY)],
        out_specs=[pl.BlockSpec((ATTN_BLOCK, ATTN_WIDTH), lambda n: (prev(n), COL_ATTN)),
                   pl.BlockSpec((1, LANES), lambda n: (0, 0))],
        out_shape=[jax.ShapeDtypeStruct(dproj.shape, dproj.dtype), jax.ShapeDtypeStruct((1, LANES), F32)],
        scratch_shapes=[pltpu.VMEM((ATTN_BLOCK, kvw), F32)] * 4 + [pltpu.VMEM((ATTN_BLOCK, D_MODEL), F32)],
        compiler_params=_params(("arbitrary",)), args=[proj, proj, proj, dy, *tabs, *tabs, sinks, dproj],
        aliases={11: 0})


def _bmm(a, b):
    return lax.dot_general(a, b, (((2,), (1,)), ((0,), (0,))), preferred_element_type=F32)


def _bmm_nt(a, b):
    return lax.dot_general(a, b, (((2,), (2,)), ((0,), (0,))), preferred_element_type=F32)


def _bmm_tn(a, b):
    return lax.dot_general(a, b, (((1,), (1,)), ((0,), (0,))), preferred_element_type=F32)


def _tril(cb, upper=False):
    shape = (cb, HGRN_CHUNK, HGRN_CHUNK)
    r, c = lax.broadcasted_iota(jnp.int32, shape, 1), lax.broadcasted_iota(jnp.int32, shape, 2)
    return (r <= c) if upper else (r >= c)


def _tri_matmul(x, upper):
    return lax.dot_general(_tril(x.shape[0], upper).astype(F32), x, (((2,), (1,)), ((0,), (0,))),
                           precision=lax.Precision.HIGHEST, preferred_element_type=F32)


@jax.custom_vjp
def _chunk_cumsum(x):
    return _tri_matmul(x, False)


_chunk_cumsum.defvjp(lambda x: (_tri_matmul(x, False), None), lambda _, g: (_tri_matmul(g, True),))


def _hg_elem(fl, qh, lb):
    f = lb + (1.0 - lb) * _sig(fl)
    k = 1.0 - f
    gc = _chunk_cumsum(jnp.log(f))
    last = lax.broadcasted_iota(jnp.int32, gc.shape, 1) == HGRN_CHUNK - 1
    g_last = jnp.sum(jnp.where(last, gc, 0.0), axis=1, keepdims=True)
    q = qh * _sig(qh)
    return q * jnp.exp(gc), k * jnp.exp(-gc), k * jnp.exp(g_last - gc), jnp.exp(g_last)


def _hg_out(q_dec, k_inv, v, st):
    sc = jnp.where(_tril(q_dec.shape[0]), _bmm_nt(_bf(q_dec), _bf(k_inv)), 0.0)
    return _bmm(_bf(sc), _bf(v)) + _bmm_nt(_bf(q_dec), _bf(st)), sc


def _hg_post(o, og, ng):
    on = o * lax.rsqrt(jnp.mean(o * o, axis=-1, keepdims=True) + RMS_EPS) * ng
    return on * (og * _sig(og))


def _hgrn_specs(n_tok, rev):
    nc = n_tok // HGRN_CHUNK
    cb = min(HGRN_CHUNKS_PER_STEP, nc)
    nt = nc // cb
    rows = cb * HGRN_CHUNK
    tt = (lambda t: nt - 1 - t) if rev else (lambda t: t)
    col = lambda base: pl.BlockSpec((rows, LANES), lambda h, t: (tt(t), base + h))
    head_cols = pl.BlockSpec((rows, HGRN_HEAD_WIDTH), lambda h, t: (tt(t), COL_HGRN + h))
    head_vec = pl.BlockSpec((1, LANES), lambda h, t: (0, h))
    one_vec = pl.BlockSpec((1, LANES), lambda h, t: (0, 0))
    state = pl.BlockSpec((1, cb, HGRN_DK, HGRN_DK), lambda h, t: (h, tt(t), 0, 0))
    return nc, cb, nt, col, head_cols, head_vec, one_vec, state


def _hgrn_fwd(proj, lb, ng):
    n_tok = proj.shape[0]
    nc, cb, nt, col, head_cols, head_vec, one_vec, state = _hgrn_specs(n_tok, False)

    def body(in_ref, lb_ref, ng_ref, y_ref, st_ref, s_acc):
        @pl.when(pl.program_id(1) == 0)
        def _():
            s_acc[...] = jnp.zeros_like(s_acc)

        fl, qh, v, og = (in_ref[:, i * LANES:(i + 1) * LANES].reshape(cb, HGRN_CHUNK, LANES) for i in range(4))
        q_dec, k_inv, k_end, decay = _hg_elem(fl, qh, lb_ref[...])
        upd = _bmm_tn(_bf(v), _bf(k_end))
        st = s_acc[...]
        for ci in range(cb):
            st_ref[0, ci] = st
            st = st * decay[ci] + upd[ci]
        s_acc[...] = st
        o, _ = _hg_out(q_dec, k_inv, v, st_ref[0])
        y_ref[...] = _bf(_hg_post(o, og, ng_ref[...]).reshape(cb * HGRN_CHUNK, LANES))

    return pl.pallas_call(
        body, name="hgrn_fwd", grid=(HGRN_HEADS, nt),
        in_specs=[head_cols, head_vec, one_vec],
        out_specs=[col(0), state],
        out_shape=[jax.ShapeDtypeStruct((n_tok, D_MODEL), BF16),
                   jax.ShapeDtypeStruct((HGRN_HEADS, nc, HGRN_DK, HGRN_DK), F32)],
        scratch_shapes=[pltpu.VMEM((HGRN_DK, HGRN_DK), F32)],
        compiler_params=_params(("arbitrary", "arbitrary")),
    )(proj, lb, ng)


def _hgrn_bwd(proj, lb, ng, states, dy, dproj, comm=None):
    n_tok = proj.shape[0]
    nc, cb, nt, col, head_cols, head_vec, one_vec, state = _hgrn_specs(n_tok, True)

    def body(in_ref, lb_ref, ng_ref, st_ref, dy_ref, _, d_ref, dlb_ref, dng_ref, g_acc, g_all):
        h = pl.program_id(0)
        t = pl.program_id(1)

        @pl.when(t == 0)
        def _():
            g_acc[...] = jnp.zeros_like(g_acc)
            dlb_ref[...] = jnp.zeros_like(dlb_ref)

        @pl.when((t == 0) & (h == 0))
        def _():
            dng_ref[...] = jnp.zeros_like(dng_ref)

        fl, qh, v, og = (in_ref[:, i * LANES:(i + 1) * LANES].reshape(cb, HGRN_CHUNK, LANES) for i in range(4))
        (q_dec, k_inv, k_end, decay), elem_vjp = jax.vjp(_hg_elem, fl, qh, lb_ref[...])
        st = st_ref[0]
        o, sc = _hg_out(q_dec, k_inv, v, st)
        _, post_vjp = jax.vjp(_hg_post, o, og, ng_ref[...])
        do, dog, dng = post_vjp(dy_ref[...].reshape(cb, HGRN_CHUNK, LANES))
        dob, vb, qb = _bf(do), _bf(v), _bf(q_dec)
        dsc = _bf(jnp.where(_tril(cb), _bmm_nt(dob, vb), 0.0))
        p = _bmm_tn(dob, qb)
        g = g_acc[...]
        for ci in reversed(range(cb)):
            g_all[ci] = g
            g = g * decay[ci] + p[ci]
        g_acc[...] = g
        g = g_all[...]
        gb = _bf(g)
        dq_dec = _bmm(dsc, _bf(k_inv)) + _bmm(dob, _bf(st))
        dk_inv = _bmm_tn(dsc, qb)
        dv = _bmm_tn(_bf(sc), dob) + _bmm_nt(_bf(k_end), gb)
        dk_end = _bmm(vb, gb)
        ddecay = jnp.sum(st * g, axis=1, keepdims=True)
        dfl, dqh, dlb = elem_vjp((dq_dec, dk_inv, dk_end, ddecay))
        for i, val in enumerate((dfl, dqh, dv, dog)):
            d_ref[:, i * LANES:(i + 1) * LANES] = _bf(val.reshape(cb * HGRN_CHUNK, LANES))
        dlb_ref[...] += dlb
        dng_ref[...] += dng

    return _hosted(
        body, comm, name="hgrn_bwd", grid=(HGRN_HEADS, nt),
        in_specs=[head_cols, head_vec, one_vec, state, col(0), pl.BlockSpec(memory_space=pl.ANY)],
        out_specs=[head_cols, head_vec, one_vec],
        out_shape=[jax.ShapeDtypeStruct(dproj.shape, dproj.dtype),
                   jax.ShapeDtypeStruct((1, D_MODEL), F32), jax.ShapeDtypeStruct((1, LANES), F32)],
        scratch_shapes=[pltpu.VMEM((HGRN_DK, HGRN_DK), F32), pltpu.VMEM((cb, HGRN_DK, HGRN_DK), F32)],
        compiler_params=_params(("arbitrary", "arbitrary")), args=[proj, lb, ng, states, dy, dproj], aliases={5: 0})


def _lb_fwd(lb_logits):
    def lb_of(l0, l1):
        m = jnp.maximum(l0, l1)
        e0, e1 = jnp.exp(l0 - m), jnp.exp(l1 - m)
        return e0 / (e0 + e1)

    def body(l_ref, o_ref):
        o_ref[...] = lb_of(l_ref[0:1, :], l_ref[1:2, :])

    lb = pl.pallas_call(body, name="lb_fwd", out_shape=jax.ShapeDtypeStruct((1, D_MODEL), F32))(lb_logits)
    return lb, lb_of


def _mix_fwd(y_attn, y_hgrn, proj, x1, w_pa, w_ph, w_out, g, b):
    n = x1.shape[0]
    tm = min(ROW_TILE, n)

    def body(ya_ref, yh_ref, gt_ref, x_ref, wpa, wph, wo, g_ref, b_ref, z_ref, o_ref, pa_ref, ph_ref):
        ya = _mm(ya_ref[...], wpa[...])
        yh = _mm(yh_ref[...], wph[...])
        pa_ref[...] = _bf(ya)
        ph_ref[...] = _bf(yh)
        merged = _sig(gt_ref[:, :D_MODEL]) * ya + _sig(gt_ref[:, D_MODEL:]) * yh
        z = DEEPNORM_ALPHA * x_ref[...] + _mm(_bf(merged), wo[...])
        z_ref[...] = z
        o_ref[...] = _ln(z, g_ref[...], b_ref[...])

    row = pl.BlockSpec((tm, D_MODEL), lambda i: (i, 0))
    gates = pl.BlockSpec((tm, GATES_WIDTH), lambda i: (i, 0))
    sq = _resident((D_MODEL, D_MODEL))
    vec = _full((1, D_MODEL))
    return pl.pallas_call(
        body, name="mix_fwd", grid=(n // tm,),
        in_specs=[row, row, gates, row, sq, sq, sq, vec, vec],
        out_specs=[row, row, row, row],
        out_shape=[jax.ShapeDtypeStruct((n, D_MODEL), F32)] * 2 + [jax.ShapeDtypeStruct((n, D_MODEL), BF16)] * 2,
        compiler_params=_params(("arbitrary",)),
    )(y_attn, y_hgrn, proj, x1, w_pa, w_ph, w_out, g, b)


def _mix_bwd(dx2, z2, pa, ph, proj, w_pa, w_ph, w_out, g, b):
    n = z2.shape[0]
    tm = min(MIX_TILE, n)

    def body(do_ref, z_ref, ya_ref, yh_ref, gt_ref, wpa, wph, wo, g_ref, b_ref,
             dz_ref, dzb_ref, mg_ref, dya_ref, dyh_ref, dyat_ref, dyhg_ref, dgt_ref, dg_ref, db_ref):
        _, vjp = jax.vjp(_ln, z_ref[...], g_ref[...], b_ref[...])
        dz, dg, db = vjp(do_ref[...])

        @pl.when(pl.program_id(0) == 0)
        def _():
            dg_ref[...] = jnp.zeros_like(dg_ref)
            db_ref[...] = jnp.zeros_like(db_ref)

        dg_ref[...] += dg
        db_ref[...] += db
        dz_ref[...] = dz
        ya = ya_ref[...].astype(F32)
        yh = yh_ref[...].astype(F32)
        def merge(ga, gh, ya, yh):
            return _sig(ga) * ya + _sig(gh) * yh

        merged, merge_vjp = jax.vjp(merge, gt_ref[:, :D_MODEL], gt_ref[:, D_MODEL:], ya, yh)
        mg_ref[...] = _bf(merged)
        dzb = _bf(dz)
        dzb_ref[...] = dzb
        dmerged = _mm_nt(dzb, wo[...])
        dga, dgh, dya, dyh = merge_vjp(dmerged)
        dya = _bf(dya)
        dyh = _bf(dyh)
        dya_ref[...] = dya
        dyh_ref[...] = dyh
        dgt_ref[:, :D_MODEL] = _bf(dga)
        dgt_ref[:, D_MODEL:] = _bf(dgh)
        dyat_ref[...] = _bf(_mm_nt(dya, wpa[...]))
        dyhg_ref[...] = _mm_nt(dyh, wph[...])

    row = pl.BlockSpec((tm, D_MODEL), lambda i: (i, 0))
    gates = pl.BlockSpec((tm, GATES_WIDTH), lambda i: (i, 0))
    sq = _resident((D_MODEL, D_MODEL))
    vec = _full((1, D_MODEL))
    f32_row = jax.ShapeDtypeStruct((n, D_MODEL), F32)
    bf_row = jax.ShapeDtypeStruct((n, D_MODEL), BF16)
    vec_shape = jax.ShapeDtypeStruct((1, D_MODEL), F32)
    return pl.pallas_call(
        body, name="mix_bwd", grid=(n // tm,),
        in_specs=[row, row, row, row, gates, sq, sq, sq, vec, vec],
        out_specs=[row, row, row, row, row, row, row, gates, vec, vec],
        out_shape=[f32_row, bf_row, bf_row, bf_row, bf_row, bf_row, f32_row,
                   jax.ShapeDtypeStruct((n, D_IN), BF16), vec_shape, vec_shape],
        compiler_params=_params(("arbitrary",)),
    )(dx2, z2, pa, ph, proj, w_pa, w_ph, w_out, g, b)


def _position():
    x, y, c = lax.axis_index("x"), lax.axis_index("y"), lax.axis_index("c")
    chips = [(1 - x, y), (x, 1 - y), (1 - x, 1 - y)]
    return x, y, c, chips


def _any_specs(k):
    return [pl.BlockSpec(memory_space=pl.ANY)] * k


class _GatherWeights:
    def __init__(self, shards):
        nw = len(shards)
        self.inputs = list(shards)
        self.out_shape = [jax.ShapeDtypeStruct((N_CHIPS, *s.shape), s.dtype) for s in shards]
        self.scratch = [pltpu.SemaphoreType.DMA((nw,)), pltpu.SemaphoreType.DMA((nw * 6,)),
                        pltpu.SemaphoreType.DMA((nw * 6,))]

    def _copies(self, ins, outs, sems):
        nw = len(ins)
        local_sem, send_sem, recv_sem = sems
        x, y, c, chips = _position()
        me = 2 * x + y
        sibling = (x, y, 1 - c)
        half_rows = [s.shape[0] // 2 for s in self.inputs]

        def half(w, chip_idx, which):
            return outs[w].at[chip_idx, pl.ds(which * half_rows[w], half_rows[w])]

        def remote(w, k, src, dst, to):
            return pltpu.make_async_remote_copy(src_ref=src, dst_ref=dst, send_sem=send_sem.at[w * 6 + k],
                                                recv_sem=recv_sem.at[w * 6 + k], device_id=to, device_id_type=MESH)

        local = [pltpu.make_async_copy(ins[w], outs[w].at[me], local_sem.at[w]) for w in range(nw)]
        first = [remote(w, j, ins[w].at[pl.ds(c * half_rows[w], half_rows[w])], half(w, me, c), (px, py, c))
                 for w in range(nw) for j, (px, py) in enumerate(chips)]
        landed = [half(w, 2 * px + py, c) for w in range(nw) for (px, py) in chips]
        arrive = [remote(w, j, landed[w * 3 + j], landed[w * 3 + j], (px, py, c))
                  for w in range(nw) for j, (px, py) in enumerate(chips)]
        passed = [remote(w, 3 + j, landed[w * 3 + j], landed[w * 3 + j], sibling) for w in range(nw) for j in range(3)]
        from_sibling = [remote(w, 3 + j, half(w, 2 * px + py, 1 - c), half(w, 2 * px + py, 1 - c), sibling)
                        for w in range(nw) for j, (px, py) in enumerate(chips)]
        return local, first, arrive, passed, from_sibling

    def start(self, ins, outs, sems):
        local, first, _, _, _ = self._copies(ins, outs, sems)
        for cp in local + first:
            cp.start()

    def finish(self, ins, outs, sems):
        local, first, arrive, passed, from_sibling = self._copies(ins, outs, sems)
        for cp_in, cp_on in zip(arrive, passed):
            cp_in.wait_recv()
            cp_on.start()
        for cp in from_sibling:
            cp.wait_recv()
        for cp in first + passed:
            cp.wait_send()
        for cp in local:
            cp.wait()


class _ExchangeGrads:
    def __init__(self, grads, rows=None, into=None):
        nw = len(grads)
        self.rows = [rows or (0, g.shape[1]) for g in grads]
        self.inputs = list(grads) + list(into or [])
        self.aliases = {nw + i: i for i in range(nw)} if into else {}
        self.out_shape = [jax.ShapeDtypeStruct(g.shape, g.dtype) for g in grads]
        self.scratch = [pltpu.SemaphoreType.DMA((nw,)), pltpu.SemaphoreType.DMA((nw * 3,)),
                        pltpu.SemaphoreType.DMA((nw * 3,))]

    def _copies(self, ins, outs, sems):
        nw = len(outs)
        local_sem, send_sem, recv_sem = sems
        x, y, c, chips = _position()
        me = 2 * x + y
        rows = [pl.ds(*r) for r in self.rows]

        def remote(w, j, src, dst, chip):
            return pltpu.make_async_remote_copy(src_ref=src, dst_ref=dst, send_sem=send_sem.at[w * 3 + j],
                                                recv_sem=recv_sem.at[w * 3 + j], device_id=(*chip, c),
                                                device_id_type=MESH)

        local = [pltpu.make_async_copy(ins[w].at[me, rows[w]], outs[w].at[me, rows[w]], local_sem.at[w])
                 for w in range(nw)]
        sends = [remote(w, j, ins[w].at[2 * px + py, rows[w]], outs[w].at[me, rows[w]], (px, py))
                 for w in range(nw) for j, (px, py) in enumerate(chips)]
        arrive = [remote(w, j, outs[w].at[2 * px + py, rows[w]], outs[w].at[2 * px + py, rows[w]], (px, py))
                  for w in range(nw) for j, (px, py) in enumerate(chips)]
        return local, sends, arrive

    def start(self, ins, outs, sems):
        local, sends, _ = self._copies(ins, outs, sems)
        for cp in local + sends:
            cp.start()

    def finish(self, ins, outs, sems):
        local, sends, arrive = self._copies(ins, outs, sems)
        for cp in arrive:
            cp.wait_recv()
        for cp in sends:
            cp.wait_send()
        for cp in local:
            cp.wait()


def _hosted(body, comm, *, name, grid, in_specs, out_specs, out_shape, scratch_shapes, compiler_params, args,
            aliases=None):
    aliases = aliases or {}
    if comm is None:
        res = pl.pallas_call(body, name=name, grid=grid, in_specs=in_specs, out_specs=out_specs, out_shape=out_shape,
                             scratch_shapes=scratch_shapes, compiler_params=compiler_params,
                             input_output_aliases=aliases)(*args)
        return list(res), []
    n_in, n_out, n_scr = len(in_specs), len(out_specs), len(scratch_shapes)
    c_in, c_out = len(comm.inputs), len(comm.out_shape)
    aliases = {**aliases, **{n_in + i: n_out + o for i, o in getattr(comm, "aliases", {}).items()}}

    def hosted_body(*refs):
        refs = list(refs)
        cut = lambda k: (refs[:k], refs[k:])
        main_in, refs = cut(n_in)
        comm_in, refs = cut(c_in)
        main_out, refs = cut(n_out)
        comm_out, refs = cut(c_out)
        main_scr, comm_scr = cut(n_scr)
        ids = [pl.program_id(a) for a in range(len(grid))]
        first = functools.reduce(jnp.logical_and, [i == 0 for i in ids])
        last = functools.reduce(jnp.logical_and, [i == g - 1 for i, g in zip(ids, grid)])

        @pl.when(first)
        def _():
            comm.start(comm_in, comm_out, comm_scr)

        body(*main_in, *main_out, *main_scr)

        @pl.when(last)
        def _():
            comm.finish(comm_in, comm_out, comm_scr)

    res = pl.pallas_call(
        hosted_body, name=name, grid=grid, in_specs=[*in_specs, *_any_specs(c_in)],
        out_specs=[*out_specs, *_any_specs(c_out)], out_shape=[*out_shape, *comm.out_shape],
        scratch_shapes=[*scratch_shapes, *comm.scratch], compiler_params=compiler_params,
        input_output_aliases=aliases,
    )(*args, *comm.inputs)
    return list(res[:n_out]), list(res[n_out:])


def _run_comm(name, comm):
    def body(*refs):
        refs = list(refs)
        c_in, c_out = len(comm.inputs), len(comm.out_shape)
        ins, outs, sems = refs[:c_in], refs[c_in:c_in + c_out], refs[c_in + c_out:]
        comm.start(ins, outs, sems)
        comm.finish(ins, outs, sems)

    return list(pl.pallas_call(
        body, name=name, in_specs=_any_specs(len(comm.inputs)), out_specs=_any_specs(len(comm.out_shape)),
        out_shape=comm.out_shape, scratch_shapes=comm.scratch,
    )(*comm.inputs))


def _sum_slots(name, slots):
    k = len(slots)
    _, rows, cols = slots[0].shape
    tr = _update_rows(rows)

    def body(*refs):
        for s_ref, o_ref in zip(refs[:k], refs[k:]):
            acc = s_ref[0].astype(F32)
            for i in range(1, N_CHIPS):
                acc = acc + s_ref[i].astype(F32)
            o_ref[...] = acc

    return pl.pallas_call(
        body, name=name, grid=(rows // tr,),
        in_specs=[pl.BlockSpec((N_CHIPS, tr, cols), lambda i: (0, i, 0))] * k,
        out_specs=[pl.BlockSpec((tr, cols), lambda i: (i, 0))] * k,
        out_shape=[jax.ShapeDtypeStruct((rows, cols), F32)] * k,
        compiler_params=_params(("arbitrary",)),
    )(*slots)


class _SwapWithSibling:
    def __init__(self, parts):
        self.inputs = list(parts)
        self.out_shape = [jax.ShapeDtypeStruct(p.shape, p.dtype) for p in parts]
        self.scratch = [pltpu.SemaphoreType.DMA((len(parts),)), pltpu.SemaphoreType.DMA((len(parts),))]

    def _copies(self, ins, outs, sems):
        send_sem, recv_sem = sems
        x, y, c, _ = _position()
        return [pltpu.make_async_remote_copy(src_ref=ins[w], dst_ref=outs[w], send_sem=send_sem.at[w],
                                             recv_sem=recv_sem.at[w], device_id=(x, y, 1 - c), device_id_type=MESH)
                for w in range(len(ins))]

    def start(self, ins, outs, sems):
        for cp in self._copies(ins, outs, sems):
            cp.start()

    def finish(self, ins, outs, sems):
        for cp in self._copies(ins, outs, sems):
            cp.wait()


class _Together:
    def __init__(self, first, second):
        self.parts = (first, second)
        self.inputs = first.inputs + second.inputs
        self.out_shape = first.out_shape + second.out_shape
        self.scratch = first.scratch + second.scratch

    def _split(self, ins, outs, sems):
        a = self.parts[0]
        ni, no, ns = len(a.inputs), len(a.out_shape), len(a.scratch)
        return ((ins[:ni], outs[:no], sems[:ns]), (ins[ni:], outs[no:], sems[ns:]))

    def start(self, ins, outs, sems):
        for part, args in zip(self.parts, self._split(ins, outs, sems)):
            part.start(*args)

    def finish(self, ins, outs, sems):
        for part, args in zip(self.parts, self._split(ins, outs, sems)):
            part.finish(*args)


def _sum_small(part):
    def body(p_ref, o_ref, buf, send_sem, recv_sem):
        x, y, c, _ = _position()
        me = 4 * x + 2 * y + c
        buf[me] = p_ref[...]
        copies = []
        for k in range(1, N_DEV):
            peer = tuple(1 - v if (k >> s) & 1 else v for v, s in ((x, 2), (y, 1), (c, 0)))
            copies.append(pltpu.make_async_remote_copy(src_ref=p_ref, dst_ref=buf.at[me], send_sem=send_sem.at[k - 1],
                                                       recv_sem=recv_sem.at[k - 1], device_id=peer, device_id_type=MESH))
        for cp in copies:
            cp.start()
        for cp in copies:
            cp.wait()
        acc = buf[0]
        for d in range(1, N_DEV):
            acc = acc + buf[d]
        o_ref[...] = acc

    vm = pl.BlockSpec(memory_space=pltpu.VMEM)
    return pl.pallas_call(
        body, name="sum_small", in_specs=[vm], out_specs=vm,
        out_shape=jax.ShapeDtypeStruct((1, SM_LEN), F32),
        scratch_shapes=[pltpu.VMEM((N_DEV, 1, SM_LEN), F32), pltpu.SemaphoreType.DMA((N_DEV - 1,)),
                        pltpu.SemaphoreType.DMA((N_DEV - 1,))],
    )(part)


def _adamw(w, g, m, v):
    m = ADAM_B1 * m + (1.0 - ADAM_B1) * g
    v = ADAM_B2 * v + (1.0 - ADAM_B2) * (g * g)
    m_hat = m / (1.0 - ADAM_B1 ** ADAM_STEP)
    v_hat = v / (1.0 - ADAM_B2 ** ADAM_STEP)
    delta = -ADAM_LR * (m_hat / (jnp.sqrt(v_hat) + ADAM_EPS) + ADAM_WD * w)
    return delta, m, v


def _adam_big(name, groups):
    k = len(groups)
    rows, cols = groups[0][2].shape
    tr = _update_rows(rows)

    def body(*refs):
        for i in range(k):
            p_ref, q_ref, w_ref, m_ref, v_ref = refs[5 * i:5 * i + 5]
            g_ref, d_ref, nm_ref, nv_ref = refs[5 * k + 4 * i:5 * k + 4 * i + 4]
            g = p_ref[...] + q_ref[...]
            g_ref[...] = g
            d_ref[...], nm_ref[...], nv_ref[...] = _adamw(w_ref[...], g, m_ref[...], v_ref[...])

    spec = pl.BlockSpec((tr, cols), lambda i: (i, 0))
    res = pl.pallas_call(
        body, name=name, grid=(rows // tr,), in_specs=[spec] * (5 * k), out_specs=[spec] * (4 * k),
        out_shape=[jax.ShapeDtypeStruct((rows, cols), F32)] * (4 * k),
        compiler_params=_params(("arbitrary",)),
    )(*[a for grp in groups for a in grp])
    return [res[4 * i:4 * i + 4] for i in range(k)]


_SMALL_AT = {"ln1_g": 0, "ln1_b": D_MODEL, "ln2_g": 2 * D_MODEL, "ln2_b": 3 * D_MODEL, "ln3_g": 4 * D_MODEL,
             "ln3_b": 5 * D_MODEL, "b_in": SM_BIN, "attn_sinks": SM_SINK, "hgrn_norm_g": SM_NG}


def _adam_small(total, w, m, v, lb_of):
    names = list(_SMALL)
    k = len(names)

    def body(*refs):
        t_ref = refs[0]
        w_refs, m_refs, v_refs = (refs[1 + i * k:1 + (i + 1) * k] for i in range(3))
        g_refs, d_refs, nm_refs, nv_refs = (refs[1 + (3 + i) * k:1 + (4 + i) * k] for i in range(4))
        for i, name in enumerate(names):
            if name == "hgrn_lb_logits":
                _, vjp = jax.vjp(lb_of, w_refs[i][0:1, :], w_refs[i][1:2, :])
                g_refs[i][0:1, :], g_refs[i][1:2, :] = vjp(t_ref[:, SM_LB:SM_LOSS])
            else:
                at = _SMALL_AT[name]
                g_refs[i][...] = t_ref[:, at:at + w_refs[i].shape[1]]
            d_refs[i][...], nm_refs[i][...], nv_refs[i][...] = _adamw(
                w_refs[i][...], g_refs[i][...], m_refs[i][...], v_refs[i][...])

    shapes = [jax.ShapeDtypeStruct(w[name].shape, F32) for name in names]
    res = pl.pallas_call(body, name="adam_small", out_shape=shapes * 4)(
        total, *[w[n] for n in names], *[m[n] for n in names], *[v[n] for n in names])
    return [dict(zip(names, res[i * k:(i + 1) * k])) for i in range(4)]


_BIG = ("ffn1_w1", "ffn1_w3", "ffn1_w2", "w_in", "w_proj_attn", "w_proj_hgrn", "w_out", "ffn2_w1", "ffn2_w3", "ffn2_w2")
_SMALL = ("ln1_g", "ln1_b", "ln2_g", "ln2_b", "ln3_g", "ln3_b", "b_in", "attn_sinks", "hgrn_norm_g", "hgrn_lb_logits")
_ORDER = ("ln1_g", "ln1_b", "ffn1_w1", "ffn1_w3", "ffn1_w2", "ln2_g", "ln2_b", "w_in", "b_in", "attn_sinks",
          "hgrn_lb_logits", "hgrn_norm_g", "w_proj_attn", "w_proj_hgrn", "w_out", "ln3_g", "ln3_b",
          "ffn2_w1", "ffn2_w3", "ffn2_w2")


_TRANSPOSED = ("ffn1_w1", "ffn1_w3", "ffn2_w1", "ffn2_w3")


def _local_view(name, arr):
    return arr[0].T if name in _TRANSPOSED else arr[0]


def _ffn_grad(name, hidden, other, comm=None):
    (dw,), comm_out = _grad_matmul(name, hidden, "cols", D_FF // FF_GRAD_PARTS, other, "shared", D_MODEL,
                                   comm=comm, parts=FF_GRAD_PARTS)
    return dw.reshape(N_CHIPS, D_FF // N_CHIPS, D_MODEL), comm_out


def kernel(x, ln1_g, ln1_b, ffn1_w1, ffn1_w3, ffn1_w2, ln2_g, ln2_b, w_in, b_in, attn_sinks, hgrn_lb_logits, hgrn_norm_g, w_proj_attn, w_proj_hgrn, w_out, ln3_g, ln3_b, ffn2_w1, ffn2_w3, ffn2_w2, loss_target, m_ln1_g, m_ln1_b, m_ffn1_w1, m_ffn1_w3, m_ffn1_w2, m_ln2_g, m_ln2_b, m_w_in, m_b_in, m_attn_sinks, m_hgrn_lb_logits, m_hgrn_norm_g, m_w_proj_attn, m_w_proj_hgrn, m_w_out, m_ln3_g, m_ln3_b, m_ffn2_w1, m_ffn2_w3, m_ffn2_w2, v_ln1_g, v_ln1_b, v_ffn1_w1, v_ffn1_w3, v_ffn1_w2, v_ln2_g, v_ln2_b, v_w_in, v_b_in, v_attn_sinks, v_hgrn_lb_logits, v_hgrn_norm_g, v_w_proj_attn, v_w_proj_hgrn, v_w_out, v_ln3_g, v_ln3_b, v_ffn2_w1, v_ffn2_w3, v_ffn2_w2):
    w = dict(ln1_g=ln1_g, ln1_b=ln1_b, ffn1_w1=ffn1_w1, ffn1_w3=ffn1_w3, ffn1_w2=ffn1_w2, ln2_g=ln2_g, ln2_b=ln2_b,
             w_in=w_in, b_in=b_in, attn_sinks=attn_sinks, hgrn_lb_logits=hgrn_lb_logits, hgrn_norm_g=hgrn_norm_g,
             w_proj_attn=w_proj_attn, w_proj_hgrn=w_proj_hgrn, w_out=w_out, ln3_g=ln3_g, ln3_b=ln3_b,
             ffn2_w1=ffn2_w1, ffn2_w3=ffn2_w3, ffn2_w2=ffn2_w2)
    mom = dict(ln1_g=m_ln1_g, ln1_b=m_ln1_b, ffn1_w1=m_ffn1_w1, ffn1_w3=m_ffn1_w3, ffn1_w2=m_ffn1_w2, ln2_g=m_ln2_g,
               ln2_b=m_ln2_b, w_in=m_w_in, b_in=m_b_in, attn_sinks=m_attn_sinks, hgrn_lb_logits=m_hgrn_lb_logits,
               hgrn_norm_g=m_hgrn_norm_g, w_proj_attn=m_w_proj_attn, w_proj_hgrn=m_w_proj_hgrn, w_out=m_w_out,
               ln3_g=m_ln3_g, ln3_b=m_ln3_b, ffn2_w1=m_ffn2_w1, ffn2_w3=m_ffn2_w3, ffn2_w2=m_ffn2_w2)
    var = dict(ln1_g=v_ln1_g, ln1_b=v_ln1_b, ffn1_w1=v_ffn1_w1, ffn1_w3=v_ffn1_w3, ffn1_w2=v_ffn1_w2, ln2_g=v_ln2_g,
               ln2_b=v_ln2_b, w_in=v_w_in, b_in=v_b_in, attn_sinks=v_attn_sinks, hgrn_lb_logits=v_hgrn_lb_logits,
               hgrn_norm_g=v_hgrn_norm_g, w_proj_attn=v_w_proj_attn, w_proj_hgrn=v_w_proj_hgrn, w_out=v_w_out,
               ln3_g=v_ln3_g, ln3_b=v_ln3_b, ffn2_w1=v_ffn2_w1, ffn2_w3=v_ffn2_w3, ffn2_w2=v_ffn2_w2)

    n_tok = x.shape[1]
    x0 = x.reshape(n_tok, D_MODEL)
    target = loss_target.reshape(n_tok, D_MODEL)

    shard = {k: _bf(_local_view(k, w[k])) for k in _BIG}
    gather = lambda keys: _GatherWeights([shard[k] for k in keys])
    slots = {}
    exchange = lambda keys: _ExchangeGrads([big[k] for k in keys])
    ffn1_keys = ("ffn1_w1", "ffn1_w3", "ffn1_w2")
    mixer_keys = ("w_in", "w_proj_attn", "w_proj_hgrn", "w_out")
    ffn2_keys = ("ffn2_w1", "ffn2_w3", "ffn2_w2")
    whole = lambda ts: [t.reshape(D_FF, D_MODEL) for t in ts]
    f1 = whole(_run_comm("gather_ffn1", gather(ffn1_keys)))

    tabs = _rope_tables(n_tok)
    lb, lb_of = _lb_fwd(hgrn_lb_logits)
    (z1, x1, x0b, h1_1, h3_1, x1b), (w_in_g, w_pa, w_ph, w_o) = _ffn_fwd(
        "ffn1_fwd", x0, *f1, ln1_g, ln1_b, comm=gather(mixer_keys))
    w_pa, w_ph, w_o = (t.reshape(D_MODEL, D_MODEL) for t in (w_pa, w_ph, w_o))
    w_in_g = _reorder_w_in("w_in_cols", w_in_g, True)
    (proj,), f2 = _in_proj(x1b, w_in_g, _to_kernel_cols(b_in), comm=gather(ffn2_keys))
    f2 = whole(f2)
    y_attn = _attn_fwd(proj, tabs, attn_sinks)
    y_hgrn, states = _hgrn_fwd(proj, lb, hgrn_norm_g)
    z2, x2, proj_a, proj_h = _mix_fwd(y_attn, y_hgrn, proj, x1, w_pa, w_ph, w_o, ln2_g, ln2_b)
    (z3, dy, x2b, h1_2, h3_2, loss_part), _ = _ffn_fwd("ffn2_fwd", x2, *f2, ln3_g, ln3_b, target=target)

    big = {}
    small = {}
    (dx2, a2, dh1_2, dh3_2, df2, small["ln3_g"], small["ln3_b"]), _ = _ffn_bwd(
        "ffn2_bwd", h1_2, h3_2, z3, dy, *f2, ln3_g, ln3_b)
    big["ffn2_w1"], _ = _ffn_grad("ffn2_dw1", dh1_2, x2b)
    big["ffn2_w3"], _ = _ffn_grad("ffn2_dw3", dh3_2, x2b)
    big["ffn2_w2"], _ = _ffn_grad("ffn2_dw2", a2, df2)
    (dz2, dz2b, merged, dya, dyh, dy_attn, dy_hgrn, dproj, small["ln2_g"], small["ln2_b"]) = _mix_bwd(
        dx2, z2, proj_a, proj_h, proj, w_pa, w_ph, w_o, ln2_g, ln2_b)
    for key, name, lhs, rhs in (("w_out", "dw_out", merged, dz2b), ("w_proj_attn", "dw_proj_attn", y_attn, dya),
                                ("w_proj_hgrn", "dw_proj_hgrn", y_hgrn, dyh)):
        (dw,), _ = _grad_matmul(name, lhs, "cols", D_MODEL, rhs, "shared", D_MODEL, parts=1)
        big[key] = dw.reshape(N_CHIPS, PROJ_SHARD, D_MODEL)
    (dproj, dlb, small["hgrn_norm_g"]), early = _hgrn_bwd(
        proj, lb, hgrn_norm_g, states, dy_hgrn, dproj, comm=exchange(ffn2_keys[:2]))
    slots.update(zip(ffn2_keys[:2], early))
    attn_hosted = ffn2_keys[2:] + mixer_keys[1:]
    (dproj, dsinks), early = _attn_bwd(proj, dy_attn, tabs, attn_sinks, dproj, comm=exchange(attn_hosted))
    slots.update(zip(attn_hosted, early))
    (dw_in, db_in), _ = _grad_matmul("dw_in", x1b, "shared", D_MODEL, dproj, "cols", IN_SHARD, colsum=True)
    big["w_in"] = _reorder_w_in("dw_in_cols", dw_in, False)
    small["b_in"] = _from_kernel_cols(db_in)
    (dx1,), (w_in_slots,) = _in_proj_dx(
        dproj, w_in_g, dz2, comm=_ExchangeGrads([big["w_in"]], rows=(0, W_IN_ROWS_FIRST)))
    (grad_x, a1, dh1_1, dh3_1, df1, small["ln1_g"], small["ln1_b"]), _ = _ffn_bwd(
        "ffn1_bwd", h1_1, h3_1, z1, dx1, *f1, ln1_g, ln1_b)
    big["ffn1_w1"], (slots["w_in"],) = _ffn_grad("ffn1_dw1", dh1_1, x0b, comm=_ExchangeGrads(
        [big["w_in"]], rows=(W_IN_ROWS_FIRST, D_MODEL - W_IN_ROWS_FIRST), into=[w_in_slots]))
    big["ffn1_w3"], (slots["ffn1_w1"],) = _ffn_grad("ffn1_dw3", dh3_1, x0b, comm=exchange(("ffn1_w1",)))
    big["ffn1_w2"], (slots["ffn1_w3"],) = _ffn_grad("ffn1_dw2", a1, df1, comm=exchange(("ffn1_w3",)))

    last = "ffn1_w2"
    groups = [[k for k in ffn1_keys + ffn2_keys if k != last], list(mixer_keys[1:]), ["w_in"]]
    partial = {}
    for keys in groups:
        partial.update(zip(keys, _sum_slots("sum_" + keys[0], [slots[k] for k in keys])))
    swapped_keys = [k for keys in groups for k in keys]
    moved = _run_comm("swap_and_exchange_last",
                      _Together(_SwapWithSibling([partial[k] for k in swapped_keys]), exchange((last,))))
    from_sibling = dict(zip(swapped_keys, moved))
    (partial[last],) = _sum_slots("sum_" + last, [moved[-1]])
    (from_sibling[last],) = _run_comm("swap_last", _SwapWithSibling([partial[last]]))
    groups[0].append(last)

    outs = {"grad": {}, "delta": {}, "m": {}, "v": {}}
    for keys in groups:
        res = _adam_big("adam_" + keys[0], [(partial[k], from_sibling[k], _local_view(k, w[k]), _local_view(k, mom[k]),
                                             _local_view(k, var[k])) for k in keys])
        for k, four in zip(keys, res):
            for kind, r in zip(("grad", "delta", "m", "v"), four):
                outs[kind][k] = (r.T if k in _TRANSPOSED else r).reshape(w[k].shape)

    total = _sum_small(jnp.concatenate(
        [small[k] for k in ("ln1_g", "ln1_b", "ln2_g", "ln2_b", "ln3_g", "ln3_b", "b_in")]
        + [dsinks, small["hgrn_norm_g"], dlb, loss_part], axis=1))
    for kind, r in zip(("grad", "delta", "m", "v"), _adam_small(total, w, mom, var, lb_of)):
        outs[kind].update(r)
    loss = total[0, SM_LOSS]

    return (loss, grad_x.reshape(x.shape), *[outs["grad"][k] for k in _ORDER], *[outs["delta"][k] for k in _ORDER],
            *[outs["m"][k] for k in _ORDER], *[outs["v"][k] for k in _ORDER])
```

```python
import functools

import jax
import jax.numpy as jnp
from jax import lax
from jax.experimental import pallas as pl
from jax.experimental.pallas import tpu as pltpu

F32 = jnp.float32
BF16 = jnp.bfloat16

D_MODEL = 1024
N_Q_HEADS = 16
N_KV_HEADS = 4
HEAD_DIM = 64
ATTN_BLOCK = 128
ROPE_THETA = 500000.0
ROPE_DIM = HEAD_DIM // 4
HGRN_HEADS = 8
HGRN_DK = 128
HGRN_CHUNK = 64
D_FF = 2816
D_IN = 7680
DEEPNORM_ALPHA = 2 ** 0.25
LN_EPS = 1e-5
RMS_EPS = 1e-6
NEG_INF = -1e30

ADAM_LR = 0.001
ADAM_B1 = 0.9
ADAM_B2 = 0.999
ADAM_EPS = 1e-08
ADAM_WD = 0.01
ADAM_STEP = 10

N_CHIPS = 4
N_DEV = 8
LANES = 128
FF_GRAD_PARTS = 2
FFN_TILE = 256
IN_SHARD = D_IN // N_CHIPS
PROJ_SHARD = D_MODEL // N_CHIPS
ROW_TILE = 512
GRAD_ROWS = 2048
MIX_TILE = 256
UPDATE_ROWS = 128
HGRN_CHUNKS_PER_STEP = 32
W_IN_ROWS_FIRST = 832
VMEM_LIMIT = 56 * 1024 * 1024

GATES_WIDTH = 2 * D_MODEL
HGRN_HEAD_WIDTH = 4 * HGRN_DK
ATTN_WIDTH = D_MODEL + 2 * N_KV_HEADS * HEAD_DIM
COL_HGRN = GATES_WIDTH // HGRN_HEAD_WIDTH
COL_ATTN = (GATES_WIDTH + HGRN_HEADS * HGRN_HEAD_WIDTH) // ATTN_WIDTH
COL_Q = (GATES_WIDTH + HGRN_HEADS * HGRN_HEAD_WIDTH) // D_MODEL
COL_K = (GATES_WIDTH + HGRN_HEADS * HGRN_HEAD_WIDTH + D_MODEL) // (N_KV_HEADS * HEAD_DIM)
COL_V = COL_K + 1


def _to_kernel_cols(a):
    lead = a.shape[:-1]
    qkv, hg, gates = a[..., :ATTN_WIDTH], a[..., ATTN_WIDTH:D_IN - GATES_WIDTH], a[..., D_IN - GATES_WIDTH:]
    hg = jnp.swapaxes(hg.reshape(*lead, 4, HGRN_HEADS, HGRN_DK), -3, -2).reshape(*lead, -1)
    return jnp.concatenate([gates, hg, qkv], axis=-1)


def _from_kernel_cols(a):
    lead = a.shape[:-1]
    gates, hg, qkv = a[..., :GATES_WIDTH], a[..., GATES_WIDTH:D_IN - ATTN_WIDTH], a[..., D_IN - ATTN_WIDTH:]
    hg = jnp.swapaxes(hg.reshape(*lead, HGRN_HEADS, 4, HGRN_DK), -3, -2).reshape(*lead, -1)
    return jnp.concatenate([qkv, hg, gates], axis=-1)

SM_LN = 0
SM_BIN = 6 * D_MODEL
SM_SINK = SM_BIN + D_IN
SM_NG = SM_SINK + LANES
SM_LB = SM_NG + LANES
SM_LOSS = SM_LB + D_MODEL
SM_LEN = SM_LOSS + LANES

MESH = pl.DeviceIdType.MESH


def _mm(a, b):
    return lax.dot_general(a, b, (((1,), (0,)), ((), ())), preferred_element_type=F32)


def _mm_nt(a, b):
    return lax.dot_general(a, b, (((1,), (1,)), ((), ())), preferred_element_type=F32)


def _mm_tn(a, b):
    return lax.dot_general(a, b, (((0,), (0,)), ((), ())), preferred_element_type=F32)


def _bf(v):
    return v.astype(BF16)


def _sig(v):
    return jax.nn.sigmoid(v)


def _ln(z, g, b):
    mu = jnp.mean(z, axis=-1, keepdims=True)
    zc = z - mu
    var = jnp.mean(zc * zc, axis=-1, keepdims=True)
    return zc * lax.rsqrt(var + LN_EPS) * g + b


def _swiglu_act(h1, h3):
    return (h1 * _sig(h1)) * h3


def _params(sem=None):
    return pltpu.CompilerParams(dimension_semantics=sem, vmem_limit_bytes=VMEM_LIMIT)


def _full(shape):
    nd = len(shape)
    return pl.BlockSpec(shape, lambda *_: (0,) * nd)


def _update_rows(rows):
    return max(t for t in range(8, UPDATE_ROWS + 1, 8) if rows % t == 0)


def _resident(shape):
    nd = len(shape)
    return pl.BlockSpec(shape, lambda *_: (0,) * nd, pipeline_mode=pl.Buffered(1))


def _ffn_fwd(name, x, w1t, w3t, w2, g, b, target=None, comm=None):
    n = x.shape[0]
    tm = min(ROW_TILE, n)
    final = target is not None

    def body(*refs):
        if final:
            x_ref, w1_ref, w3_ref, w2_ref, g_ref, b_ref, t_ref, z_ref, o_ref, xb_ref, h1_ref, h3_ref, loss_ref = refs
        else:
            x_ref, w1_ref, w3_ref, w2_ref, g_ref, b_ref, z_ref, o_ref, xb_ref, h1_ref, h3_ref, ob_ref = refs
        xb = _bf(x_ref[...])
        xb_ref[...] = xb
        h1 = _mm_nt(xb, w1_ref[...])
        h3 = _mm_nt(xb, w3_ref[...])
        h1_ref[...] = _bf(h1)
        h3_ref[...] = _bf(h3)
        z = DEEPNORM_ALPHA * x_ref[...] + 0.5 * _mm(_bf(_swiglu_act(h1, h3)), w2_ref[...])
        z_ref[...] = z
        y = _ln(z, g_ref[...], b_ref[...])
        if final:
            e = y - t_ref[...]

            @pl.when(pl.program_id(0) == 0)
            def _():
                loss_ref[...] = jnp.zeros_like(loss_ref)

            loss_ref[...] += jnp.sum(e * e) * (0.5 / D_MODEL)
            o_ref[...] = e * (1.0 / D_MODEL)
        else:
            o_ref[...] = y
            ob_ref[...] = _bf(y)

    row = pl.BlockSpec((tm, D_MODEL), lambda i: (i, 0))
    hid = pl.BlockSpec((tm, D_FF), lambda i: (i, 0))
    wres = _resident((D_FF, D_MODEL))
    vec = _full((1, D_MODEL))
    in_specs = [row, wres, wres, wres, vec, vec]
    args = [x, w1t, w3t, w2, g, b]
    hid_shape = jax.ShapeDtypeStruct((n, D_FF), BF16)
    out_specs = [row, row, row, hid, hid]
    out_shape = ([jax.ShapeDtypeStruct((n, D_MODEL), F32)] * 2 + [jax.ShapeDtypeStruct((n, D_MODEL), BF16)]
                 + [hid_shape] * 2)
    if final:
        in_specs.append(row)
        args.append(target)
        out_specs.append(_full((1, LANES)))
        out_shape.append(jax.ShapeDtypeStruct((1, LANES), F32))
    else:
        out_specs.append(row)
        out_shape.append(jax.ShapeDtypeStruct((n, D_MODEL), BF16))
    return _hosted(
        body, comm, name=name, grid=(n // tm,), in_specs=in_specs, out_specs=out_specs, out_shape=out_shape,
        scratch_shapes=[], compiler_params=_params(("arbitrary",)), args=args)


def _ffn_bwd(name, h1s, h3s, z, dout, w1t, w3t, w2, g, b, comm=None):
    n = z.shape[0]
    tm = min(FFN_TILE, n)

    def body(h1_ref, h3_ref, z_ref, do_ref, w1_ref, w3_ref, w2_ref, g_ref, b_ref,
             dx_ref, a_ref, dh1_ref, dh3_ref, df_ref, dg_ref, db_ref):
        _, vjp = jax.vjp(_ln, z_ref[...], g_ref[...], b_ref[...])
        dz, dg, db = vjp(do_ref[...])

        @pl.when(pl.program_id(0) == 0)
        def _():
            dg_ref[...] = jnp.zeros_like(dg_ref)
            db_ref[...] = jnp.zeros_like(db_ref)

        dg_ref[...] += dg
        db_ref[...] += db
        df = _bf(0.5 * dz)
        df_ref[...] = df
        a, act_vjp = jax.vjp(_swiglu_act, h1_ref[...].astype(F32), h3_ref[...].astype(F32))
        dh1, dh3 = act_vjp(_mm_nt(df, w2_ref[...]))
        dh1 = _bf(dh1)
        dh3 = _bf(dh3)
        a_ref[...] = _bf(a)
        dh1_ref[...] = dh1
        dh3_ref[...] = dh3
        dx_ref[...] = DEEPNORM_ALPHA * dz + _mm(dh1, w1_ref[...]) + _mm(dh3, w3_ref[...])

    row = pl.BlockSpec((tm, D_MODEL), lambda i: (i, 0))
    hid = pl.BlockSpec((tm, D_FF), lambda i: (i, 0))
    wres = _resident((D_FF, D_MODEL))
    vec = _full((1, D_MODEL))
    hid_shape = jax.ShapeDtypeStruct((n, D_FF), BF16)
    return _hosted(
        body, comm, name=name, grid=(n // tm,),
        in_specs=[hid, hid, row, row, wres, wres, wres, vec, vec],
        out_specs=[row, hid, hid, hid, row, vec, vec],
        out_shape=[jax.ShapeDtypeStruct((n, D_MODEL), F32), hid_shape, hid_shape, hid_shape,
                   jax.ShapeDtypeStruct((n, D_MODEL), BF16),
                   jax.ShapeDtypeStruct((1, D_MODEL), F32), jax.ShapeDtypeStruct((1, D_MODEL), F32)],
        scratch_shapes=[], compiler_params=_params(("arbitrary",)),
        args=[h1s, h3s, z, dout, w1t, w3t, w2, g, b])


def _operand_spec(arr, mode, tn, width, parts):
    if mode == "shared":
        return pl.BlockSpec((tn, width), lambda s, k: (k, 0))
    assert mode == "cols" and arr.shape[1] == parts * width
    return pl.BlockSpec((tn, width), lambda s, k: (k, s))


def _grad_matmul(name, a, a_mode, ka, b, b_mode, kb, colsum=False, comm=None, parts=N_CHIPS):
    n = a.shape[-2]
    tn = min(GRAD_ROWS, n)
    nk = n // tn

    def body(*refs):
        if colsum:
            a_ref, b_ref, o_ref, cs_ref, acc = refs
        else:
            a_ref, b_ref, o_ref, acc = refs
        k = pl.program_id(1)
        av = a_ref[...]
        bv = b_ref[...]

        @pl.when(k == 0)
        def _():
            acc[...] = jnp.zeros_like(acc)
            if colsum:
                cs_ref[...] = jnp.zeros_like(cs_ref)

        acc[...] += _mm_tn(av, bv)
        if colsum:
            cs_ref[...] += jnp.sum(bv.astype(F32), axis=0, keepdims=True)

        @pl.when(k == nk - 1)
        def _():
            o_ref[0] = _bf(acc[...])

    out_specs = [pl.BlockSpec((1, ka, kb), lambda s, k: (s, 0, 0))]
    out_shape = [jax.ShapeDtypeStruct((parts, ka, kb), BF16)]
    if colsum:
        out_specs.append(pl.BlockSpec((1, kb), lambda s, k: (0, s)))
        out_shape.append(jax.ShapeDtypeStruct((1, parts * kb), F32))
    return _hosted(
        body, comm, name=name, grid=(parts, nk),
        in_specs=[_operand_spec(a, a_mode, tn, ka, parts), _operand_spec(b, b_mode, tn, kb, parts)],
        out_specs=out_specs, out_shape=out_shape,
        scratch_shapes=[pltpu.VMEM((ka, kb), F32)],
        compiler_params=_params(("arbitrary", "arbitrary")), args=[a, b])


def _kernel_block_of(ref_block):
    attn_blocks, gate_blocks = ATTN_WIDTH // LANES, GATES_WIDTH // LANES
    hgrn_blocks = HGRN_HEADS * HGRN_HEAD_WIDTH // LANES
    if ref_block < attn_blocks:
        return gate_blocks + hgrn_blocks + ref_block
    if ref_block < attn_blocks + hgrn_blocks:
        kind, head = divmod(ref_block - attn_blocks, HGRN_HEADS)
        return gate_blocks + head * (HGRN_HEAD_WIDTH // LANES) + kind
    return ref_block - attn_blocks - hgrn_blocks


def _reorder_w_in(name, w4, to_kernel_order):
    per = IN_SHARD // LANES
    tr = MIX_TILE

    def body(i_ref, o_ref):
        for g in range(D_IN // LANES):
            s, b = divmod(g, per)
            k = _kernel_block_of(g)
            if to_kernel_order:
                o_ref[:, k * LANES:(k + 1) * LANES] = i_ref[s, :, b * LANES:(b + 1) * LANES]
            else:
                ks, kb = divmod(k, per)
                o_ref[s, :, b * LANES:(b + 1) * LANES] = i_ref[ks, :, kb * LANES:(kb + 1) * LANES]

    in_spec = pl.BlockSpec((N_CHIPS, tr, IN_SHARD), lambda i: (0, i, 0))
    if to_kernel_order:
        out_spec, out_shape = pl.BlockSpec((tr, D_IN), lambda i: (i, 0)), (D_MODEL, D_IN)
    else:
        out_spec, out_shape = in_spec, (N_CHIPS, D_MODEL, IN_SHARD)
    return pl.pallas_call(
        body, name=name, grid=(D_MODEL // tr,), in_specs=[in_spec], out_specs=out_spec,
        out_shape=jax.ShapeDtypeStruct(out_shape, w4.dtype), compiler_params=_params(("arbitrary",)),
    )(w4)


def _in_proj(x1, w_in_g, b_in, comm=None):
    n = x1.shape[0]
    tm = min(ROW_TILE, n)

    def body(x_ref, w_ref, b_ref, o_ref):
        xv = x_ref[...]
        for j in range(N_CHIPS):
            cols = slice(j * IN_SHARD, (j + 1) * IN_SHARD)
            o_ref[:, cols] = _mm(xv, w_ref[:, cols]) + b_ref[:, cols]

    return _hosted(
        body, comm, name="in_proj", grid=(n // tm,),
        in_specs=[pl.BlockSpec((tm, D_MODEL), lambda i: (i, 0)),
                  _resident((D_MODEL, D_IN)), _full((1, D_IN))],
        out_specs=[pl.BlockSpec((tm, D_IN), lambda i: (i, 0))],
        out_shape=[jax.ShapeDtypeStruct((n, D_IN), F32)],
        scratch_shapes=[],
        compiler_params=_params(("arbitrary",)), args=[x1, w_in_g, b_in])


def _in_proj_dx(dproj, w_in_g, dz2, comm=None):
    n = dproj.shape[0]
    tm = min(ROW_TILE, n)

    def body(dp_ref, w_ref, dz_ref, o_ref):
        dx = DEEPNORM_ALPHA * dz_ref[...]
        for j in range(N_CHIPS):
            cols = slice(j * IN_SHARD, (j + 1) * IN_SHARD)
            dx = dx + _mm_nt(dp_ref[:, cols], w_ref[:, cols])
        o_ref[...] = dx

    return _hosted(
        body, comm, name="in_proj_dx", grid=(n // tm,),
        in_specs=[pl.BlockSpec((tm, D_IN), lambda i: (i, 0)),
                  _resident((D_MODEL, D_IN)),
                  pl.BlockSpec((tm, D_MODEL), lambda i: (i, 0))],
        out_specs=[pl.BlockSpec((tm, D_MODEL), lambda i: (i, 0))],
        out_shape=[jax.ShapeDtypeStruct((n, D_MODEL), F32)],
        scratch_shapes=[],
        compiler_params=_params(("arbitrary",)), args=[dproj, w_in_g, dz2])


def _rope_tables(seq_len):
    pos = jnp.arange(seq_len, dtype=F32)
    inv_freq = ROPE_THETA ** (-jnp.arange(0, ROPE_DIM, 2, dtype=F32) / ROPE_DIM)
    ang = pos[:, None] * inv_freq[None, :]
    cos, sin = jnp.cos(ang), jnp.sin(ang)
    half = ROPE_DIM // 2
    rest = HEAD_DIM - ROPE_DIM
    ones = jnp.ones((seq_len, rest), F32)
    zeros = jnp.zeros((seq_len, rest), F32)
    zh = jnp.zeros((seq_len, half), F32)
    c = jnp.concatenate([cos, cos, ones], axis=1)
    sa = jnp.concatenate([-sin, zh, zeros], axis=1)
    sb = jnp.concatenate([zh, sin, zeros], axis=1)
    reps = LANES // HEAD_DIM
    return tuple(jnp.tile(t, (1, reps)) for t in (c, sa, sb))


def _rope(t, c, sa, sb):
    w = t.shape[1]
    reps = w // LANES
    half = ROPE_DIM // 2
    return (t * jnp.tile(c, (1, reps)) + pltpu.roll(t, w - half, 1) * jnp.tile(sa, (1, reps))
            + pltpu.roll(t, half, 1) * jnp.tile(sb, (1, reps)))


def _rope_transposed(g, c, sa, sb):
    w = g.shape[1]
    reps = w // LANES
    half = ROPE_DIM // 2
    return (g * jnp.tile(c, (1, reps)) + pltpu.roll(g * jnp.tile(sa, (1, reps)), half, 1)
            + pltpu.roll(g * jnp.tile(sb, (1, reps)), w - half, 1))


GROUP = N_Q_HEADS // N_KV_HEADS


def _both_halves(t_pair, which):
    lo = lax.broadcasted_iota(jnp.int32, t_pair.shape, 1) < HEAD_DIM
    swapped = pltpu.roll(t_pair, HEAD_DIM, 1)
    return _bf(jnp.where(lo, t_pair, swapped) if which == 0 else jnp.where(lo, swapped, t_pair))


def _stack_heads(ref_or_val, kh):
    lo = lax.broadcasted_iota(jnp.int32, (ATTN_BLOCK, LANES), 1) < HEAD_DIM
    rows = []
    for gp in range(GROUP // 2):
        pair = kh * (GROUP // 2) + gp
        t = ref_or_val[:, pair * LANES:(pair + 1) * LANES]
        rows += [jnp.where(lo, t, jnp.zeros_like(t)), jnp.where(lo, jnp.zeros_like(t), t)]
    return jnp.concatenate(rows, axis=0)


def _unstack_pairs(stacked):
    lo = lax.broadcasted_iota(jnp.int32, (ATTN_BLOCK, LANES), 1) < HEAD_DIM
    b = ATTN_BLOCK
    return [jnp.where(lo, stacked[2 * gp * b:(2 * gp + 1) * b], stacked[(2 * gp + 1) * b:(2 * gp + 2) * b])
            for gp in range(GROUP // 2)]


def _attn_mask_t(n):
    cols = GROUP * ATTN_BLOCK
    kj = lax.broadcasted_iota(jnp.int32, (2 * ATTN_BLOCK, cols), 0)
    qi = lax.broadcasted_iota(jnp.int32, (2 * ATTN_BLOCK, cols), 1) % ATTN_BLOCK
    dist = qi + ATTN_BLOCK - kj
    return (dist >= 0) & (dist < ATTN_BLOCK) & (n * ATTN_BLOCK + kj - ATTN_BLOCK >= 0)


def _sink_row(sink_ref, kh):
    col = lax.broadcasted_iota(jnp.int32, (1, GROUP * ATTN_BLOCK), 1)
    row = jnp.full((1, GROUP * ATTN_BLOCK), sink_ref[0, kh * GROUP + GROUP - 1], F32)
    for i in reversed(range(GROUP - 1)):
        row = jnp.where(col < (i + 1) * ATTN_BLOCK, sink_ref[0, kh * GROUP + i], row)
    return row


def _attn_probs_t(q_masked, k_sel, mask_t, sink):
    s = _mm_nt(k_sel, q_masked) * (HEAD_DIM ** -0.5)
    s = jnp.where(mask_t, s, NEG_INF)
    m = jnp.maximum(jnp.max(s, axis=0, keepdims=True), sink)
    p = jnp.exp(s - m)
    e_sink = jnp.exp(sink - m)
    denom = jnp.sum(p, axis=0, keepdims=True) + e_sink
    return p / denom, e_sink / denom


def _attn_fwd(proj, tabs, sinks):
    n_tok = proj.shape[0]
    nb = n_tok // ATTN_BLOCK

    def body(q_ref, k_ref, v_ref, c_ref, sa_ref, sb_ref, sink_ref, y_ref, kprev, vprev):
        n = pl.program_id(0)

        @pl.when(n == 0)
        def _():
            kprev[...] = jnp.zeros_like(kprev)
            vprev[...] = jnp.zeros_like(vprev)

        c, sa, sb = c_ref[...], sa_ref[...], sb_ref[...]
        qr = _bf(_rope(q_ref[...], c, sa, sb))
        kr = _rope(k_ref[...], c, sa, sb)
        vc = v_ref[...]
        kk = jnp.concatenate([kprev[...], kr], axis=0)
        vv = jnp.concatenate([vprev[...], vc], axis=0)
        kprev[...] = kr
        vprev[...] = vc
        mask = _attn_mask_t(n)
        for kh in range(N_KV_HEADS):
            r, which = divmod(kh, 2)
            kb = _both_halves(kk[:, r * LANES:(r + 1) * LANES], which)
            vb = _both_halves(vv[:, r * LANES:(r + 1) * LANES], which)
            probs, _ = _attn_probs_t(_stack_heads(qr, kh), kb, mask, _sink_row(sink_ref, kh))
            for gp, out in enumerate(_unstack_pairs(_mm_tn(_bf(probs), vb))):
                pair = kh * (GROUP // 2) + gp
                y_ref[:, pair * LANES:(pair + 1) * LANES] = _bf(out)

    blk = lambda width, col: pl.BlockSpec((ATTN_BLOCK, width), lambda n: (n, col))
    tab = pl.BlockSpec((ATTN_BLOCK, LANES), lambda n: (n, 0))
    kvw = N_KV_HEADS * HEAD_DIM
    return pl.pallas_call(
        body, name="attn_fwd", grid=(nb,),
        in_specs=[blk(D_MODEL, COL_Q), blk(kvw, COL_K), blk(kvw, COL_V), tab, tab, tab,
                  pl.BlockSpec(memory_space=pltpu.SMEM)],
        out_specs=pl.BlockSpec((ATTN_BLOCK, D_MODEL), lambda n: (n, 0)),
        out_shape=jax.ShapeDtypeStruct((n_tok, D_MODEL), BF16),
        scratch_shapes=[pltpu.VMEM((ATTN_BLOCK, kvw), F32), pltpu.VMEM((ATTN_BLOCK, kvw), F32)],
        compiler_params=_params(("arbitrary",)),
    )(proj, proj, proj, *tabs, sinks)


def _attn_bwd(proj, dy, tabs, sinks, dproj, comm=None):
    n_tok = proj.shape[0]
    nb = n_tok // ATTN_BLOCK
    kvw = N_KV_HEADS * HEAD_DIM

    def body(q_ref, k_ref, v_ref, do_ref, c_ref, sa_ref, sb_ref, cp_ref, sap_ref, sbp_ref, sink_ref, _,
             dqkv_ref, dsink_ref, kprev, vprev, dkc, dvc, dqc):
        n = pl.program_id(0)

        @pl.when(n == 0)
        def _():
            for ref in (kprev, vprev, dkc, dvc, dqc, dsink_ref):
                ref[...] = jnp.zeros_like(ref)

        prev_tabs = (cp_ref[...], sap_ref[...], sbp_ref[...])

        @pl.when(n < nb)
        def _():
            c, sa, sb = c_ref[...], sa_ref[...], sb_ref[...]
            qr = _bf(_rope(q_ref[...], c, sa, sb))
            kr = _rope(k_ref[...], c, sa, sb)
            vc = v_ref[...]
            kk = jnp.concatenate([kprev[...], kr], axis=0)
            vv = jnp.concatenate([vprev[...], vc], axis=0)
            kprev[...] = kr
            vprev[...] = vc
            mask = _attn_mask_t(n)
            lane = lax.broadcasted_iota(jnp.int32, (1, LANES), 1)
            lo2 = lax.broadcasted_iota(jnp.int32, (2 * ATTN_BLOCK, LANES), 1) < HEAD_DIM
            dsink = jnp.zeros((1, LANES), F32)
            dq_pairs = []
            dk_full = []
            dv_full = []
            for kh in range(N_KV_HEADS):
                r, which = divmod(kh, 2)
                kb = _both_halves(kk[:, r * LANES:(r + 1) * LANES], which)
                vb = _both_halves(vv[:, r * LANES:(r + 1) * LANES], which)
                qs = _stack_heads(qr, kh)
                dos = _stack_heads(do_ref, kh)
                probs, p_sink = _attn_probs_t(qs, kb, mask, _sink_row(sink_ref, kh))
                dp = _mm_nt(vb, dos)
                delta = jnp.sum(probs * dp, axis=0, keepdims=True)
                ds = _bf(probs * (dp - delta) * (HEAD_DIM ** -0.5))
                sink_terms = p_sink * delta
                for i in range(GROUP):
                    head_sum = jnp.sum(sink_terms[:, i * ATTN_BLOCK:(i + 1) * ATTN_BLOCK])
                    dsink = dsink + jnp.where(lane == kh * GROUP + i, -head_sum, 0.0)
                dq_pairs += _unstack_pairs(_mm_tn(ds, kb))
                dk_acc = _mm(ds, qs)
                dv_acc = _mm(_bf(probs), dos)
                dk_full.append(dk_acc + pltpu.roll(dk_acc, HEAD_DIM, 1))
                dv_full.append(dv_acc + pltpu.roll(dv_acc, HEAD_DIM, 1))
            dk_pairs = [jnp.where(lo2, dk_full[2 * r], dk_full[2 * r + 1]) for r in range(N_KV_HEADS // 2)]
            dv_pairs = [jnp.where(lo2, dv_full[2 * r], dv_full[2 * r + 1]) for r in range(N_KV_HEADS // 2)]
            dsink_ref[...] += dsink
            dqkv_ref[:, :D_MODEL] = _bf(dqc[...])
            dqc[...] = _rope_transposed(jnp.concatenate(dq_pairs, axis=1), c, sa, sb)
            dk_all = jnp.concatenate(dk_pairs, axis=1)
            dv_all = jnp.concatenate(dv_pairs, axis=1)
            dqkv_ref[:, D_MODEL:D_MODEL + kvw] = _bf(_rope_transposed(dkc[...] + dk_all[:ATTN_BLOCK], *prev_tabs))
            dqkv_ref[:, D_MODEL + kvw:] = _bf(dvc[...] + dv_all[:ATTN_BLOCK])
            dkc[...] = dk_all[ATTN_BLOCK:]
            dvc[...] = dv_all[ATTN_BLOCK:]

        @pl.when(n == nb)
        def _():
            dqkv_ref[:, :D_MODEL] = _bf(dqc[...])
            dqkv_ref[:, D_MODEL:D_MODEL + kvw] = _bf(_rope_transposed(dkc[...], *prev_tabs))
            dqkv_ref[:, D_MODEL + kvw:] = _bf(dvc[...])

    cur = lambda n: jnp.minimum(n, nb - 1)
    prev = lambda n: jnp.maximum(n - 1, 0)
    blk = lambda width, col: pl.BlockSpec((ATTN_BLOCK, width), lambda n: (cur(n), col))
    tab = pl.BlockSpec((ATTN_BLOCK, LANES), lambda n: (cur(n), 0))
    tabp = pl.BlockSpec((ATTN_BLOCK, LANES), lambda n: (prev(n), 0))
    return _hosted(
        body, comm, name="attn_bwd", grid=(nb + 1,),
        in_specs=[blk(D_MODEL, COL_Q), blk(kvw, COL_K), blk(kvw, COL_V), blk(D_MODEL, 0), tab, tab, tab, tabp, tabp, tabp,
                  pl.BlockSpec(memory_space=pltpu.SMEM), pl.BlockSpec(memory_space=pl.ANY)],
        out_specs=[pl.BlockSpec((ATTN_BLOCK, ATTN_WIDTH), lambda n: (prev(n), COL_ATTN)),
                   pl.BlockSpec((1, LANES), lambda n: (0, 0))],
        out_shape=[jax.ShapeDtypeStruct(dproj.shape, dproj.dtype), jax.ShapeDtypeStruct((1, LANES), F32)],
        scratch_shapes=[pltpu.VMEM((ATTN_BLOCK, kvw), F32)] * 4 + [pltpu.VMEM((ATTN_BLOCK, D_MODEL), F32)],
        compiler_params=_params(("arbitrary",)), args=[proj, proj, proj, dy, *tabs, *tabs, sinks, dproj],
        aliases={11: 0})


def _bmm(a, b):
    return lax.dot_general(a, b, (((2,), (1,)), ((0,), (0,))), preferred_element_type=F32)


def _bmm_nt(a, b):
    return lax.dot_general(a, b, (((2,), (2,)), ((0,), (0,))), preferred_element_type=F32)


def _bmm_tn(a, b):
    return lax.dot_general(a, b, (((1,), (1,)), ((0,), (0,))), preferred_element_type=F32)


def _tril(cb, upper=False):
    shape = (cb, HGRN_CHUNK, HGRN_CHUNK)
    r, c = lax.broadcasted_iota(jnp.int32, shape, 1), lax.broadcasted_iota(jnp.int32, shape, 2)
    return (r <= c) if upper else (r >= c)


def _tri_matmul(x, upper):
    return lax.dot_general(_tril(x.shape[0], upper).astype(F32), x, (((2,), (1,)), ((0,), (0,))),
                           precision=lax.Precision.HIGHEST, preferred_element_type=F32)


@jax.custom_vjp
def _chunk_cumsum(x):
    return _tri_matmul(x, False)


_chunk_cumsum.defvjp(lambda x: (_tri_matmul(x, False), None), lambda _, g: (_tri_matmul(g, True),))


def _hg_elem(fl, qh, lb):
    f = lb + (1.0 - lb) * _sig(fl)
    k = 1.0 - f
    gc = _chunk_cumsum(jnp.log(f))
    last = lax.broadcasted_iota(jnp.int32, gc.shape, 1) == HGRN_CHUNK - 1
    g_last = jnp.sum(jnp.where(last, gc, 0.0), axis=1, keepdims=True)
    q = qh * _sig(qh)
    return q * jnp.exp(gc), k * jnp.exp(-gc), k * jnp.exp(g_last - gc), jnp.exp(g_last)


def _hg_out(q_dec, k_inv, v, st):
    sc = jnp.where(_tril(q_dec.shape[0]), _bmm_nt(_bf(q_dec), _bf(k_inv)), 0.0)
    return _bmm(_bf(sc), _bf(v)) + _bmm_nt(_bf(q_dec), _bf(st)), sc


def _hg_post(o, og, ng):
    on = o * lax.rsqrt(jnp.mean(o * o, axis=-1, keepdims=True) + RMS_EPS) * ng
    return on * (og * _sig(og))


def _hgrn_specs(n_tok, rev):
    nc = n_tok // HGRN_CHUNK
    cb = min(HGRN_CHUNKS_PER_STEP, nc)
    nt = nc // cb
    rows = cb * HGRN_CHUNK
    tt = (lambda t: nt - 1 - t) if rev else (lambda t: t)
    col = lambda base: pl.BlockSpec((rows, LANES), lambda h, t: (tt(t), base + h))
    head_cols = pl.BlockSpec((rows, HGRN_HEAD_WIDTH), lambda h, t: (tt(t), COL_HGRN + h))
    head_vec = pl.BlockSpec((1, LANES), lambda h, t: (0, h))
    one_vec = pl.BlockSpec((1, LANES), lambda h, t: (0, 0))
    state = pl.BlockSpec((1, cb, HGRN_DK, HGRN_DK), lambda h, t: (h, tt(t), 0, 0))
    return nc, cb, nt, col, head_cols, head_vec, one_vec, state


def _hgrn_fwd(proj, lb, ng):
    n_tok = proj.shape[0]
    nc, cb, nt, col, head_cols, head_vec, one_vec, state = _hgrn_specs(n_tok, False)

    def body(in_ref, lb_ref, ng_ref, y_ref, st_ref, s_acc):
        @pl.when(pl.program_id(1) == 0)
        def _():
            s_acc[...] = jnp.zeros_like(s_acc)

        fl, qh, v, og = (in_ref[:, i * LANES:(i + 1) * LANES].reshape(cb, HGRN_CHUNK, LANES) for i in range(4))
        q_dec, k_inv, k_end, decay = _hg_elem(fl, qh, lb_ref[...])
        upd = _bmm_tn(_bf(v), _bf(k_end))
        st = s_acc[...]
        for ci in range(cb):
            st_ref[0, ci] = st
            st = st * decay[ci] + upd[ci]
        s_acc[...] = st
        o, _ = _hg_out(q_dec, k_inv, v, st_ref[0])
        y_ref[...] = _bf(_hg_post(o, og, ng_ref[...]).reshape(cb * HGRN_CHUNK, LANES))

    return pl.pallas_call(
        body, name="hgrn_fwd", grid=(HGRN_HEADS, nt),
        in_specs=[head_cols, head_vec, one_vec],
        out_specs=[col(0), state],
        out_shape=[jax.ShapeDtypeStruct((n_tok, D_MODEL), BF16),
                   jax.ShapeDtypeStruct((HGRN_HEADS, nc, HGRN_DK, HGRN_DK), F32)],
        scratch_shapes=[pltpu.VMEM((HGRN_DK, HGRN_DK), F32)],
        compiler_params=_params(("arbitrary", "arbitrary")),
    )(proj, lb, ng)


def _hgrn_bwd(proj, lb, ng, states, dy, dproj, comm=None):
    n_tok = proj.shape[0]
    nc, cb, nt, col, head_cols, head_vec, one_vec, state = _hgrn_specs(n_tok, True)

    def body(in_ref, lb_ref, ng_ref, st_ref, dy_ref, _, d_ref, dlb_ref, dng_ref, g_acc, g_all):
        h = pl.program_id(0)
        t = pl.program_id(1)

        @pl.when(t == 0)
        def _():
            g_acc[...] = jnp.zeros_like(g_acc)
            dlb_ref[...] = jnp.zeros_like(dlb_ref)

        @pl.when((t == 0) & (h == 0))
        def _():
            dng_ref[...] = jnp.zeros_like(dng_ref)

        fl, qh, v, og = (in_ref[:, i * LANES:(i + 1) * LANES].reshape(cb, HGRN_CHUNK, LANES) for i in range(4))
        (q_dec, k_inv, k_end, decay), elem_vjp = jax.vjp(_hg_elem, fl, qh, lb_ref[...])
        st = st_ref[0]
        o, sc = _hg_out(q_dec, k_inv, v, st)
        _, post_vjp = jax.vjp(_hg_post, o, og, ng_ref[...])
        do, dog, dng = post_vjp(dy_ref[...].reshape(cb, HGRN_CHUNK, LANES))
        dob, vb, qb = _bf(do), _bf(v), _bf(q_dec)
        dsc = _bf(jnp.where(_tril(cb), _bmm_nt(dob, vb), 0.0))
        p = _bmm_tn(dob, qb)
        g = g_acc[...]
        for ci in reversed(range(cb)):
            g_all[ci] = g
            g = g * decay[ci] + p[ci]
        g_acc[...] = g
        g = g_all[...]
        gb = _bf(g)
        dq_dec = _bmm(dsc, _bf(k_inv)) + _bmm(dob, _bf(st))
        dk_inv = _bmm_tn(dsc, qb)
        dv = _bmm_tn(_bf(sc), dob) + _bmm_nt(_bf(k_end), gb)
        dk_end = _bmm(vb, gb)
        ddecay = jnp.sum(st * g, axis=1, keepdims=True)
        dfl, dqh, dlb = elem_vjp((dq_dec, dk_inv, dk_end, ddecay))
        for i, val in enumerate((dfl, dqh, dv, dog)):
            d_ref[:, i * LANES:(i + 1) * LANES] = _bf(val.reshape(cb * HGRN_CHUNK, LANES))
        dlb_ref[...] += dlb
        dng_ref[...] += dng

    return _hosted(
        body, comm, name="hgrn_bwd", grid=(HGRN_HEADS, nt),
        in_specs=[head_cols, head_vec, one_vec, state, col(0), pl.BlockSpec(memory_space=pl.ANY)],
        out_specs=[head_cols, head_vec, one_vec],
        out_shape=[jax.ShapeDtypeStruct(dproj.shape, dproj.dtype),
                   jax.ShapeDtypeStruct((1, D_MODEL), F32), jax.ShapeDtypeStruct((1, LANES), F32)],
        scratch_shapes=[pltpu.VMEM((HGRN_DK, HGRN_DK), F32), pltpu.VMEM((cb, HGRN_DK, HGRN_DK), F32)],
        compiler_params=_params(("arbitrary", "arbitrary")), args=[proj, lb, ng, states, dy, dproj], aliases={5: 0})


def _lb_fwd(lb_logits):
    def lb_of(l0, l1):
        m = jnp.maximum(l0, l1)
        e0, e1 = jnp.exp(l0 - m), jnp.exp(l1 - m)
        return e0 / (e0 + e1)

    def body(l_ref, o_ref):
        o_ref[...] = lb_of(l_ref[0:1, :], l_ref[1:2, :])

    lb = pl.pallas_call(body, name="lb_fwd", out_shape=jax.ShapeDtypeStruct((1, D_MODEL), F32))(lb_logits)
    return lb, lb_of


def _mix_fwd(y_attn, y_hgrn, proj, x1, w_pa, w_ph, w_out, g, b):
    n = x1.shape[0]
    tm = min(ROW_TILE, n)

    def body(ya_ref, yh_ref, gt_ref, x_ref, wpa, wph, wo, g_ref, b_ref, z_ref, o_ref, pa_ref, ph_ref):
        ya = _mm(ya_ref[...], wpa[...])
        yh = _mm(yh_ref[...], wph[...])
        pa_ref[...] = _bf(ya)
        ph_ref[...] = _bf(yh)
        merged = _sig(gt_ref[:, :D_MODEL]) * ya + _sig(gt_ref[:, D_MODEL:]) * yh
        z = DEEPNORM_ALPHA * x_ref[...] + _mm(_bf(merged), wo[...])
        z_ref[...] = z
        o_ref[...] = _ln(z, g_ref[...], b_ref[...])

    row = pl.BlockSpec((tm, D_MODEL), lambda i: (i, 0))
    gates = pl.BlockSpec((tm, GATES_WIDTH), lambda i: (i, 0))
    sq = _resident((D_MODEL, D_MODEL))
    vec = _full((1, D_MODEL))
    return pl.pallas_call(
        body, name="mix_fwd", grid=(n // tm,),
        in_specs=[row, row, gates, row, sq, sq, sq, vec, vec],
        out_specs=[row, row, row, row],
        out_shape=[jax.ShapeDtypeStruct((n, D_MODEL), F32)] * 2 + [jax.ShapeDtypeStruct((n, D_MODEL), BF16)] * 2,
        compiler_params=_params(("arbitrary",)),
    )(y_attn, y_hgrn, proj, x1, w_pa, w_ph, w_out, g, b)


def _mix_bwd(dx2, z2, pa, ph, proj, w_pa, w_ph, w_out, g, b):
    n = z2.shape[0]
    tm = min(MIX_TILE, n)

    def body(do_ref, z_ref, ya_ref, yh_ref, gt_ref, wpa, wph, wo, g_ref, b_ref,
             dz_ref, dzb_ref, mg_ref, dya_ref, dyh_ref, dyat_ref, dyhg_ref, dgt_ref, dg_ref, db_ref):
        _, vjp = jax.vjp(_ln, z_ref[...], g_ref[...], b_ref[...])
        dz, dg, db = vjp(do_ref[...])

        @pl.when(pl.program_id(0) == 0)
        def _():
            dg_ref[...] = jnp.zeros_like(dg_ref)
            db_ref[...] = jnp.zeros_like(db_ref)

        dg_ref[...] += dg
        db_ref[...] += db
        dz_ref[...] = dz
        ya = ya_ref[...].astype(F32)
        yh = yh_ref[...].astype(F32)
        def merge(ga, gh, ya, yh):
            return _sig(ga) * ya + _sig(gh) * yh

        merged, merge_vjp = jax.vjp(merge, gt_ref[:, :D_MODEL], gt_ref[:, D_MODEL:], ya, yh)
        mg_ref[...] = _bf(merged)
        dzb = _bf(dz)
        dzb_ref[...] = dzb
        dmerged = _mm_nt(dzb, wo[...])
        dga, dgh, dya, dyh = merge_vjp(dmerged)
        dya = _bf(dya)
        dyh = _bf(dyh)
        dya_ref[...] = dya
        dyh_ref[...] = dyh
        dgt_ref[:, :D_MODEL] = _bf(dga)
        dgt_ref[:, D_MODEL:] = _bf(dgh)
        dyat_ref[...] = _bf(_mm_nt(dya, wpa[...]))
        dyhg_ref[...] = _mm_nt(dyh, wph[...])

    row = pl.BlockSpec((tm, D_MODEL), lambda i: (i, 0))
    gates = pl.BlockSpec((tm, GATES_WIDTH), lambda i: (i, 0))
    sq = _resident((D_MODEL, D_MODEL))
    vec = _full((1, D_MODEL))
    f32_row = jax.ShapeDtypeStruct((n, D_MODEL), F32)
    bf_row = jax.ShapeDtypeStruct((n, D_MODEL), BF16)
    vec_shape = jax.ShapeDtypeStruct((1, D_MODEL), F32)
    return pl.pallas_call(
        body, name="mix_bwd", grid=(n // tm,),
        in_specs=[row, row, row, row, gates, sq, sq, sq, vec, vec],
        out_specs=[row, row, row, row, row, row, row, gates, vec, vec],
        out_shape=[f32_row, bf_row, bf_row, bf_row, bf_row, bf_row, f32_row,
                   jax.ShapeDtypeStruct((n, D_IN), BF16), vec_shape, vec_shape],
        compiler_params=_params(("arbitrary",)),
    )(dx2, z2, pa, ph, proj, w_pa, w_ph, w_out, g, b)


def _position():
    x, y, c = lax.axis_index("x"), lax.axis_index("y"), lax.axis_index("c")
    chips = [(1 - x, y), (x, 1 - y), (1 - x, 1 - y)]
    return x, y, c, chips


def _any_specs(k):
    return [pl.BlockSpec(memory_space=pl.ANY)] * k


class _GatherWeights:
    def __init__(self, shards):
        nw = len(shards)
        self.inputs = list(shards)
        self.out_shape = [jax.ShapeDtypeStruct((N_CHIPS, *s.shape), s.dtype) for s in shards]
        self.scratch = [pltpu.SemaphoreType.DMA((nw,)), pltpu.SemaphoreType.DMA((nw * 6,)),
                        pltpu.SemaphoreType.DMA((nw * 6,))]

    def _copies(self, ins, outs, sems):
        nw = len(ins)
        local_sem, send_sem, recv_sem = sems
        x, y, c, chips = _position()
        me = 2 * x + y
        sibling = (x, y, 1 - c)
        half_rows = [s.shape[0] // 2 for s in self.inputs]

        def half(w, chip_idx, which):
            return outs[w].at[chip_idx, pl.ds(which * half_rows[w], half_rows[w])]

        def remote(w, k, src, dst, to):
            return pltpu.make_async_remote_copy(src_ref=src, dst_ref=dst, send_sem=send_sem.at[w * 6 + k],
                                                recv_sem=recv_sem.at[w * 6 + k], device_id=to, device_id_type=MESH)

        local = [pltpu.make_async_copy(ins[w], outs[w].at[me], local_sem.at[w]) for w in range(nw)]
        first = [remote(w, j, ins[w].at[pl.ds(c * half_rows[w], half_rows[w])], half(w, me, c), (px, py, c))
                 for w in range(nw) for j, (px, py) in enumerate(chips)]
        landed = [half(w, 2 * px + py, c) for w in range(nw) for (px, py) in chips]
        arrive = [remote(w, j, landed[w * 3 + j], landed[w * 3 + j], (px, py, c))
                  for w in range(nw) for j, (px, py) in enumerate(chips)]
        passed = [remote(w, 3 + j, landed[w * 3 + j], landed[w * 3 + j], sibling) for w in range(nw) for j in range(3)]
        from_sibling = [remote(w, 3 + j, half(w, 2 * px + py, 1 - c), half(w, 2 * px + py, 1 - c), sibling)
                        for w in range(nw) for j, (px, py) in enumerate(chips)]
        return local, first, arrive, passed, from_sibling

    def start(self, ins, outs, sems):
        local, first, _, _, _ = self._copies(ins, outs, sems)
        for cp in local + first:
            cp.start()

    def finish(self, ins, outs, sems):
        local, first, arrive, passed, from_sibling = self._copies(ins, outs, sems)
        for cp_in, cp_on in zip(arrive, passed):
            cp_in.wait_recv()
            cp_on.start()
        for cp in from_sibling:
            cp.wait_recv()
        for cp in first + passed:
            cp.wait_send()
        for cp in local:
            cp.wait()


class _ExchangeGrads:
    def __init__(self, grads, rows=None, into=None):
        nw = len(grads)
        self.rows = [rows or (0, g.shape[1]) for g in grads]
        self.inputs = list(grads) + list(into or [])
        self.aliases = {nw + i: i for i in range(nw)} if into else {}
        self.out_shape = [jax.ShapeDtypeStruct(g.shape, g.dtype) for g in grads]
        self.scratch = [pltpu.SemaphoreType.DMA((nw,)), pltpu.SemaphoreType.DMA((nw * 3,)),
                        pltpu.SemaphoreType.DMA((nw * 3,))]

    def _copies(self, ins, outs, sems):
        nw = len(outs)
        local_sem, send_sem, recv_sem = sems
        x, y, c, chips = _position()
        me = 2 * x + y
        rows = [pl.ds(*r) for r in self.rows]

        def remote(w, j, src, dst, chip):
            return pltpu.make_async_remote_copy(src_ref=src, dst_ref=dst, send_sem=send_sem.at[w * 3 + j],
                                                recv_sem=recv_sem.at[w * 3 + j], device_id=(*chip, c),
                                                device_id_type=MESH)

        local = [pltpu.make_async_copy(ins[w].at[me, rows[w]], outs[w].at[me, rows[w]], local_sem.at[w])
                 for w in range(nw)]
        sends = [remote(w, j, ins[w].at[2 * px + py, rows[w]], outs[w].at[me, rows[w]], (px, py))
                 for w in range(nw) for j, (px, py) in enumerate(chips)]
        arrive = [remote(w, j, outs[w].at[2 * px + py, rows[w]], outs[w].at[2 * px + py, rows[w]], (px, py))
                  for w in range(nw) for j, (px, py) in enumerate(chips)]
        return local, sends, arrive

    def start(self, ins, outs, sems):
        local, sends, _ = self._copies(ins, outs, sems)
        for cp in local + sends:
            cp.start()

    def finish(self, ins, outs, sems):
        local, sends, arrive = self._copies(ins, outs, sems)
        for cp in arrive:
            cp.wait_recv()
        for cp in sends:
            cp.wait_send()
        for cp in local:
            cp.wait()


def _hosted(body, comm, *, name, grid, in_specs, out_specs, out_shape, scratch_shapes, compiler_params, args,
            aliases=None):
    aliases = aliases or {}
    if comm is None:
        res = pl.pallas_call(body, name=name, grid=grid, in_specs=in_specs, out_specs=out_specs, out_shape=out_shape,
                             scratch_shapes=scratch_shapes, compiler_params=compiler_params,
                             input_output_aliases=aliases)(*args)
        return list(res), []
    n_in, n_out, n_scr = len(in_specs), len(out_specs), len(scratch_shapes)
    c_in, c_out = len(comm.inputs), len(comm.out_shape)
    aliases = {**aliases, **{n_in + i: n_out + o for i, o in getattr(comm, "aliases", {}).items()}}

    def hosted_body(*refs):
        refs = list(refs)
        cut = lambda k: (refs[:k], refs[k:])
        main_in, refs = cut(n_in)
        comm_in, refs = cut(c_in)
        main_out, refs = cut(n_out)
        comm_out, refs = cut(c_out)
        main_scr, comm_scr = cut(n_scr)
        ids = [pl.program_id(a) for a in range(len(grid))]
        first = functools.reduce(jnp.logical_and, [i == 0 for i in ids])
        last = functools.reduce(jnp.logical_and, [i == g - 1 for i, g in zip(ids, grid)])

        @pl.when(first)
        def _():
            comm.start(comm_in, comm_out, comm_scr)

        body(*main_in, *main_out, *main_scr)

        @pl.when(last)
        def _():
            comm.finish(comm_in, comm_out, comm_scr)

    res = pl.pallas_call(
        hosted_body, name=name, grid=grid, in_specs=[*in_specs, *_any_specs(c_in)],
        out_specs=[*out_specs, *_any_specs(c_out)], out_shape=[*out_shape, *comm.out_shape],
        scratch_shapes=[*scratch_shapes, *comm.scratch], compiler_params=compiler_params,
        input_output_aliases=aliases,
    )(*args, *comm.inputs)
    return list(res[:n_out]), list(res[n_out:])


def _run_comm(name, comm):
    def body(*refs):
        refs = list(refs)
        c_in, c_out = len(comm.inputs), len(comm.out_shape)
        ins, outs, sems = refs[:c_in], refs[c_in:c_in + c_out], refs[c_in + c_out:]
        comm.start(ins, outs, sems)
        comm.finish(ins, outs, sems)

    return list(pl.pallas_call(
        body, name=name, in_specs=_any_specs(len(comm.inputs)), out_specs=_any_specs(len(comm.out_shape)),
        out_shape=comm.out_shape, scratch_shapes=comm.scratch,
    )(*comm.inputs))


def _sum_slots(name, slots):
    k = len(slots)
    _, rows, cols = slots[0].shape
    tr = _update_rows(rows)

    def body(*refs):
        for s_ref, o_ref in zip(refs[:k], refs[k:]):
            acc = s_ref[0].astype(F32)
            for i in range(1, N_CHIPS):
                acc = acc + s_ref[i].astype(F32)
            o_ref[...] = acc

    return pl.pallas_call(
        body, name=name, grid=(rows // tr,),
        in_specs=[pl.BlockSpec((N_CHIPS, tr, cols), lambda i: (0, i, 0))] * k,
        out_specs=[pl.BlockSpec((tr, cols), lambda i: (i, 0))] * k,
        out_shape=[jax.ShapeDtypeStruct((rows, cols), F32)] * k,
        compiler_params=_params(("arbitrary",)),
    )(*slots)


class _SwapWithSibling:
    def __init__(self, parts):
        self.inputs = list(parts)
        self.out_shape = [jax.ShapeDtypeStruct(p.shape, p.dtype) for p in parts]
        self.scratch = [pltpu.SemaphoreType.DMA((len(parts),)), pltpu.SemaphoreType.DMA((len(parts),))]

    def _copies(self, ins, outs, sems):
        send_sem, recv_sem = sems
        x, y, c, _ = _position()
        return [pltpu.make_async_remote_copy(src_ref=ins[w], dst_ref=outs[w], send_sem=send_sem.at[w],
                                             recv_sem=recv_sem.at[w], device_id=(x, y, 1 - c), device_id_type=MESH)
                for w in range(len(ins))]

    def start(self, ins, outs, sems):
        for cp in self._copies(ins, outs, sems):
            cp.start()

    def finish(self, ins, outs, sems):
        for cp in self._copies(ins, outs, sems):
            cp.wait()


class _Together:
    def __init__(self, first, second):
        self.parts = (first, second)
        self.inputs = first.inputs + second.inputs
        self.out_shape = first.out_shape + second.out_shape
        self.scratch = first.scratch + second.scratch

    def _split(self, ins, outs, sems):
        a = self.parts[0]
        ni, no, ns = len(a.inputs), len(a.out_shape), len(a.scratch)
        return ((ins[:ni], outs[:no], sems[:ns]), (ins[ni:], outs[no:], sems[ns:]))

    def start(self, ins, outs, sems):
        for part, args in zip(self.parts, self._split(ins, outs, sems)):
            part.start(*args)

    def finish(self, ins, outs, sems):
        for part, args in zip(self.parts, self._split(ins, outs, sems)):
            part.finish(*args)


def _sum_small(part):
    def body(p_ref, o_ref, buf, send_sem, recv_sem):
        x, y, c, _ = _position()
        me = 4 * x + 2 * y + c
        buf[me] = p_ref[...]
        copies = []
        for k in range(1, N_DEV):
            peer = tuple(1 - v if (k >> s) & 1 else v for v, s in ((x, 2), (y, 1), (c, 0)))
            copies.append(pltpu.make_async_remote_copy(src_ref=p_ref, dst_ref=buf.at[me], send_sem=send_sem.at[k - 1],
                                                       recv_sem=recv_sem.at[k - 1], device_id=peer, device_id_type=MESH))
        for cp in copies:
            cp.start()
        for cp in copies:
            cp.wait()
        acc = buf[0]
        for d in range(1, N_DEV):
            acc = acc + buf[d]
        o_ref[...] = acc

    vm = pl.BlockSpec(memory_space=pltpu.VMEM)
    return pl.pallas_call(
        body, name="sum_small", in_specs=[vm], out_specs=vm,
        out_shape=jax.ShapeDtypeStruct((1, SM_LEN), F32),
        scratch_shapes=[pltpu.VMEM((N_DEV, 1, SM_LEN), F32), pltpu.SemaphoreType.DMA((N_DEV - 1,)),
                        pltpu.SemaphoreType.DMA((N_DEV - 1,))],
    )(part)


def _adamw(w, g, m, v):
    m = ADAM_B1 * m + (1.0 - ADAM_B1) * g
    v = ADAM_B2 * v + (1.0 - ADAM_B2) * (g * g)
    m_hat = m / (1.0 - ADAM_B1 ** ADAM_STEP)
    v_hat = v / (1.0 - ADAM_B2 ** ADAM_STEP)
    delta = -ADAM_LR * (m_hat / (jnp.sqrt(v_hat) + ADAM_EPS) + ADAM_WD * w)
    return delta, m, v


def _adam_big(name, groups):
    k = len(groups)
    rows, cols = groups[0][2].shape
    tr = _update_rows(rows)

    def body(*refs):
        for i in range(k):
            p_ref, q_ref, w_ref, m_ref, v_ref = refs[5 * i:5 * i + 5]
            g_ref, d_ref, nm_ref, nv_ref = refs[5 * k + 4 * i:5 * k + 4 * i + 4]
            g = p_ref[...] + q_ref[...]
            g_ref[...] = g
            d_ref[...], nm_ref[...], nv_ref[...] = _adamw(w_ref[...], g, m_ref[...], v_ref[...])

    spec = pl.BlockSpec((tr, cols), lambda i: (i, 0))
    res = pl.pallas_call(
        body, name=name, grid=(rows // tr,), in_specs=[spec] * (5 * k), out_specs=[spec] * (4 * k),
        out_shape=[jax.ShapeDtypeStruct((rows, cols), F32)] * (4 * k),
        compiler_params=_params(("arbitrary",)),
    )(*[a for grp in groups for a in grp])
    return [res[4 * i:4 * i + 4] for i in range(k)]


_SMALL_AT = {"ln1_g": 0, "ln1_b": D_MODEL, "ln2_g": 2 * D_MODEL, "ln2_b": 3 * D_MODEL, "ln3_g": 4 * D_MODEL,
             "ln3_b": 5 * D_MODEL, "b_in": SM_BIN, "attn_sinks": SM_SINK, "hgrn_norm_g": SM_NG}


def _adam_small(total, w, m, v, lb_of):
    names = list(_SMALL)
    k = len(names)

    def body(*refs):
        t_ref = refs[0]
        w_refs, m_refs, v_refs = (refs[1 + i * k:1 + (i + 1) * k] for i in range(3))
        g_refs, d_refs, nm_refs, nv_refs = (refs[1 + (3 + i) * k:1 + (4 + i) * k] for i in range(4))
        for i, name in enumerate(names):
            if name == "hgrn_lb_logits":
                _, vjp = jax.vjp(lb_of, w_refs[i][0:1, :], w_refs[i][1:2, :])
                g_refs[i][0:1, :], g_refs[i][1:2, :] = vjp(t_ref[:, SM_LB:SM_LOSS])
            else:
                at = _SMALL_AT[name]
                g_refs[i][...] = t_ref[:, at:at + w_refs[i].shape[1]]
            d_refs[i][...], nm_refs[i][...], nv_refs[i][...] = _adamw(
                w_refs[i][...], g_refs[i][...], m_refs[i][...], v_refs[i][...])

    shapes = [jax.ShapeDtypeStruct(w[name].shape, F32) for name in names]
    res = pl.pallas_call(body, name="adam_small", out_shape=shapes * 4)(
        total, *[w[n] for n in names], *[m[n] for n in names], *[v[n] for n in names])
    return [dict(zip(names, res[i * k:(i + 1) * k])) for i in range(4)]


_BIG = ("ffn1_w1", "ffn1_w3", "ffn1_w2", "w_in", "w_proj_attn", "w_proj_hgrn", "w_out", "ffn2_w1", "ffn2_w3", "ffn2_w2")
_SMALL = ("ln1_g", "ln1_b", "ln2_g", "ln2_b", "ln3_g", "ln3_b", "b_in", "attn_sinks", "hgrn_norm_g", "hgrn_lb_logits")
_ORDER = ("ln1_g", "ln1_b", "ffn1_w1", "ffn1_w3", "ffn1_w2", "ln2_g", "ln2_b", "w_in", "b_in", "attn_sinks",
          "hgrn_lb_logits", "hgrn_norm_g", "w_proj_attn", "w_proj_hgrn", "w_out", "ln3_g", "ln3_b",
          "ffn2_w1", "ffn2_w3", "ffn2_w2")


_TRANSPOSED = ("ffn1_w1", "ffn1_w3", "ffn2_w1", "ffn2_w3")


def _local_view(name, arr):
    return arr[0].T if name in _TRANSPOSED else arr[0]


def _ffn_grad(name, hidden, other, comm=None):
    (dw,), comm_out = _grad_matmul(name, hidden, "cols", D_FF // FF_GRAD_PARTS, other, "shared", D_MODEL,
                                   comm=comm, parts=FF_GRAD_PARTS)
    return dw.reshape(N_CHIPS, D_FF // N_CHIPS, D_MODEL), comm_out


def kernel(x, ln1_g, ln1_b, ffn1_w1, ffn1_w3, ffn1_w2, ln2_g, ln2_b, w_in, b_in, attn_sinks, hgrn_lb_logits, hgrn_norm_g, w_proj_attn, w_proj_hgrn, w_out, ln3_g, ln3_b, ffn2_w1, ffn2_w3, ffn2_w2, loss_target, m_ln1_g, m_ln1_b, m_ffn1_w1, m_ffn1_w3, m_ffn1_w2, m_ln2_g, m_ln2_b, m_w_in, m_b_in, m_attn_sinks, m_hgrn_lb_logits, m_hgrn_norm_g, m_w_proj_attn, m_w_proj_hgrn, m_w_out, m_ln3_g, m_ln3_b, m_ffn2_w1, m_ffn2_w3, m_ffn2_w2, v_ln1_g, v_ln1_b, v_ffn1_w1, v_ffn1_w3, v_ffn1_w2, v_ln2_g, v_ln2_b, v_w_in, v_b_in, v_attn_sinks, v_hgrn_lb_logits, v_hgrn_norm_g, v_w_proj_attn, v_w_proj_hgrn, v_w_out, v_ln3_g, v_ln3_b, v_ffn2_w1, v_ffn2_w3, v_ffn2_w2):
    w = dict(ln1_g=ln1_g, ln1_b=ln1_b, ffn1_w1=ffn1_w1, ffn1_w3=ffn1_w3, ffn1_w2=ffn1_w2, ln2_g=ln2_g, ln2_b=ln2_b,
             w_in=w_in, b_in=b_in, attn_sinks=attn_sinks, hgrn_lb_logits=hgrn_lb_logits, hgrn_norm_g=hgrn_norm_g,
             w_proj_attn=w_proj_attn, w_proj_hgrn=w_proj_hgrn, w_out=w_out, ln3_g=ln3_g, ln3_b=ln3_b,
             ffn2_w1=ffn2_w1, ffn2_w3=ffn2_w3, ffn2_w2=ffn2_w2)
    mom = dict(ln1_g=m_ln1_g, ln1_b=m_ln1_b, ffn1_w1=m_ffn1_w1, ffn1_w3=m_ffn1_w3, ffn1_w2=m_ffn1_w2, ln2_g=m_ln2_g,
               ln2_b=m_ln2_b, w_in=m_w_in, b_in=m_b_in, attn_sinks=m_attn_sinks, hgrn_lb_logits=m_hgrn_lb_logits,
               hgrn_norm_g=m_hgrn_norm_g, w_proj_attn=m_w_proj_attn, w_proj_hgrn=m_w_proj_hgrn, w_out=m_w_out,
               ln3_g=m_ln3_g, ln3_b=m_ln3_b, ffn2_w1=m_ffn2_w1, ffn2_w3=m_ffn2_w3, ffn2_w2=m_ffn2_w2)
    var = dict(ln1_g=v_ln1_g, ln1_b=v_ln1_b, ffn1_w1=v_ffn1_w1, ffn1_w3=v_ffn1_w3, ffn1_w2=v_ffn1_w2, ln2_g=v_ln2_g,
               ln2_b=v_ln2_b, w_in=v_w_in, b_in=v_b_in, attn_sinks=v_attn_sinks, hgrn_lb_logits=v_hgrn_lb_logits,
               hgrn_norm_g=v_hgrn_norm_g, w_proj_attn=v_w_proj_attn, w_proj_hgrn=v_w_proj_hgrn, w_out=v_w_out,
               ln3_g=v_ln3_g, ln3_b=v_ln3_b, ffn2_w1=v_ffn2_w1, ffn2_w3=v_ffn2_w3, ffn2_w2=v_ffn2_w2)

    n_tok = x.shape[1]
    x0 = x.reshape(n_tok, D_MODEL)
    target = loss_target.reshape(n_tok, D_MODEL)

    shard = {k: _bf(_local_view(k, w[k])) for k in _BIG}
    gather = lambda keys: _GatherWeights([shard[k] for k in keys])
    slots = {}
    exchange = lambda keys: _ExchangeGrads([big[k] for k in keys])
    ffn1_keys = ("ffn1_w1", "ffn1_w3", "ffn1_w2")
    mixer_keys = ("w_in", "w_proj_attn", "w_proj_hgrn", "w_out")
    ffn2_keys = ("ffn2_w1", "ffn2_w3", "ffn2_w2")
    whole = lambda ts: [t.reshape(D_FF, D_MODEL) for t in ts]
    f1 = whole(_run_comm("gather_ffn1", gather(ffn1_keys)))

    tabs = _rope_tables(n_tok)
    lb, lb_of = _lb_fwd(hgrn_lb_logits)
    (z1, x1, x0b, h1_1, h3_1, x1b), (w_in_g, w_pa, w_ph, w_o) = _ffn_fwd(
        "ffn1_fwd", x0, *f1, ln1_g, ln1_b, comm=gather(mixer_keys))
    w_pa, w_ph, w_o = (t.reshape(D_MODEL, D_MODEL) for t in (w_pa, w_ph, w_o))
    w_in_g = _reorder_w_in("w_in_cols", w_in_g, True)
    (proj,), f2 = _in_proj(x1b, w_in_g, _to_kernel_cols(b_in), comm=gather(ffn2_keys))
    f2 = whole(f2)
    y_attn = _attn_fwd(proj, tabs, attn_sinks)
    y_hgrn, states = _hgrn_fwd(proj, lb, hgrn_norm_g)
    z2, x2, proj_a, proj_h = _mix_fwd(y_attn, y_hgrn, proj, x1, w_pa, w_ph, w_o, ln2_g, ln2_b)
    (z3, dy, x2b, h1_2, h3_2, loss_part), _ = _ffn_fwd("ffn2_fwd", x2, *f2, ln3_g, ln3_b, target=target)

    big = {}
    small = {}
    (dx2, a2, dh1_2, dh3_2, df2, small["ln3_g"], small["ln3_b"]), _ = _ffn_bwd(
        "ffn2_bwd", h1_2, h3_2, z3, dy, *f2, ln3_g, ln3_b)
    big["ffn2_w1"], _ = _ffn_grad("ffn2_dw1", dh1_2, x2b)
    big["ffn2_w3"], _ = _ffn_grad("ffn2_dw3", dh3_2, x2b)
    big["ffn2_w2"], _ = _ffn_grad("ffn2_dw2", a2, df2)
    (dz2, dz2b, merged, dya, dyh, dy_attn, dy_hgrn, dproj, small["ln2_g"], small["ln2_b"]) = _mix_bwd(
        dx2, z2, proj_a, proj_h, proj, w_pa, w_ph, w_o, ln2_g, ln2_b)
    for key, name, lhs, rhs in (("w_out", "dw_out", merged, dz2b), ("w_proj_attn", "dw_proj_attn", y_attn, dya),
                                ("w_proj_hgrn", "dw_proj_hgrn", y_hgrn, dyh)):
        (dw,), _ = _grad_matmul(name, lhs, "cols", D_MODEL, rhs, "shared", D_MODEL, parts=1)
        big[key] = dw.reshape(N_CHIPS, PROJ_SHARD, D_MODEL)
    (dproj, dlb, small["hgrn_norm_g"]), early = _hgrn_bwd(
        proj, lb, hgrn_norm_g, states, dy_hgrn, dproj, comm=exchange(ffn2_keys[:2]))
    slots.update(zip(ffn2_keys[:2], early))
    attn_hosted = ffn2_keys[2:] + mixer_keys[1:]
    (dproj, dsinks), early = _attn_bwd(proj, dy_attn, tabs, attn_sinks, dproj, comm=exchange(attn_hosted))
    slots.update(zip(attn_hosted, early))
    (dw_in, db_in), _ = _grad_matmul("dw_in", x1b, "shared", D_MODEL, dproj, "cols", IN_SHARD, colsum=True)
    big["w_in"] = _reorder_w_in("dw_in_cols", dw_in, False)
    small["b_in"] = _from_kernel_cols(db_in)
    (dx1,), (w_in_slots,) = _in_proj_dx(
        dproj, w_in_g, dz2, comm=_ExchangeGrads([big["w_in"]], rows=(0, W_IN_ROWS_FIRST)))
    (grad_x, a1, dh1_1, dh3_1, df1, small["ln1_g"], small["ln1_b"]), _ = _ffn_bwd(
        "ffn1_bwd", h1_1, h3_1, z1, dx1, *f1, ln1_g, ln1_b)
    big["ffn1_w1"], (slots["w_in"],) = _ffn_grad("ffn1_dw1", dh1_1, x0b, comm=_ExchangeGrads(
        [big["w_in"]], rows=(W_IN_ROWS_FIRST, D_MODEL - W_IN_ROWS_FIRST), into=[w_in_slots]))
    big["ffn1_w3"], (slots["ffn1_w1"],) = _ffn_grad("ffn1_dw3", dh3_1, x0b, comm=exchange(("ffn1_w1",)))
    big["ffn1_w2"], (slots["ffn1_w3"],) = _ffn_grad("ffn1_dw2", a1, df1, comm=exchange(("ffn1_w3",)))

    last = "ffn1_w2"
    groups = [[k for k in ffn1_keys + ffn2_keys if k != last], list(mixer_keys[1:]), ["w_in"]]
    partial = {}
    for keys in groups:
        partial.update(zip(keys, _sum_slots("sum_" + keys[0], [slots[k] for k in keys])))
    swapped_keys = [k for keys in groups for k in keys]
    moved = _run_comm("swap_and_exchange_last",
                      _Together(_SwapWithSibling([partial[k] for k in swapped_keys]), exchange((last,))))
    from_sibling = dict(zip(swapped_keys, moved))
    (partial[last],) = _sum_slots("sum_" + last, [moved[-1]])
    (from_sibling[last],) = _run_comm("swap_last", _SwapWithSibling([partial[last]]))
    groups[0].append(last)

    outs = {"grad": {}, "delta": {}, "m": {}, "v": {}}
    for keys in groups:
        res = _adam_big("adam_" + keys[0], [(partial[k], from_sibling[k], _local_view(k, w[k]), _local_view(k, mom[k]),
                                             _local_view(k, var[k])) for k in keys])
        for k, four in zip(keys, res):
            for kind, r in zip(("grad", "delta", "m", "v"), four):
                outs[kind][k] = (r.T if k in _TRANSPOSED else r).reshape(w[k].shape)

    total = _sum_small(jnp.concatenate(
        [small[k] for k in ("ln1_g", "ln1_b", "ln2_g", "ln2_b", "ln3_g", "ln3_b", "b_in")]
        + [dsinks, small["hgrn_norm_g"], dlb, loss_part], axis=1))
    for kind, r in zip(("grad", "delta", "m", "v"), _adam_small(total, w, mom, var, lb_of)):
        outs[kind].update(r)
    loss = total[0, SM_LOSS]

    return (loss, grad_x.reshape(x.shape), *[outs["grad"][k] for k in _ORDER], *[outs["delta"][k] for k in _ORDER],
            *[outs["m"][k] for k in _ORDER], *[outs["v"][k] for k in _ORDER])
```

```python
import functools

import jax
import jax.numpy as jnp
from jax import lax
from jax.experimental import pallas as pl
from jax.experimental.pallas import tpu as pltpu

F32 = jnp.float32
BF16 = jnp.bfloat16

D_MODEL = 1024
N_Q_HEADS = 16
N_KV_HEADS = 4
HEAD_DIM = 64
ATTN_BLOCK = 128
ROPE_THETA = 500000.0
ROPE_DIM = HEAD_DIM // 4
HGRN_HEADS = 8
HGRN_DK = 128
HGRN_CHUNK = 64
D_FF = 2816
D_IN = 7680
DEEPNORM_ALPHA = 2 ** 0.25
LN_EPS = 1e-5
RMS_EPS = 1e-6
NEG_INF = -1e30

ADAM_LR = 0.001
ADAM_B1 = 0.9
ADAM_B2 = 0.999
ADAM_EPS = 1e-08
ADAM_WD = 0.01
ADAM_STEP = 10

N_CHIPS = 4
N_DEV = 8
LANES = 128
FF_GRAD_PARTS = 2
FFN_TILE = 256
IN_SHARD = D_IN // N_CHIPS
PROJ_SHARD = D_MODEL // N_CHIPS
ROW_TILE = 512
GRAD_ROWS = 2048
MIX_TILE = 256
UPDATE_ROWS = 128
HGRN_CHUNKS_PER_STEP = 32
W_IN_ROWS_FIRST = 832
VMEM_LIMIT = 56 * 1024 * 1024

GATES_WIDTH = 2 * D_MODEL
HGRN_HEAD_WIDTH = 4 * HGRN_DK
ATTN_WIDTH = D_MODEL + 2 * N_KV_HEADS * HEAD_DIM
COL_HGRN = GATES_WIDTH // HGRN_HEAD_WIDTH
COL_ATTN = (GATES_WIDTH + HGRN_HEADS * HGRN_HEAD_WIDTH) // ATTN_WIDTH
COL_Q = (GATES_WIDTH + HGRN_HEADS * HGRN_HEAD_WIDTH) // D_MODEL
COL_K = (GATES_WIDTH + HGRN_HEADS * HGRN_HEAD_WIDTH + D_MODEL) // (N_KV_HEADS * HEAD_DIM)
COL_V = COL_K + 1


def _to_kernel_cols(a):
    lead = a.shape[:-1]
    qkv, hg, gates = a[..., :ATTN_WIDTH], a[..., ATTN_WIDTH:D_IN - GATES_WIDTH], a[..., D_IN - GATES_WIDTH:]
    hg = jnp.swapaxes(hg.reshape(*lead, 4, HGRN_HEADS, HGRN_DK), -3, -2).reshape(*lead, -1)
    return jnp.concatenate([gates, hg, qkv], axis=-1)


def _from_kernel_cols(a):
    lead = a.shape[:-1]
    gates, hg, qkv = a[..., :GATES_WIDTH], a[..., GATES_WIDTH:D_IN - ATTN_WIDTH], a[..., D_IN - ATTN_WIDTH:]
    hg = jnp.swapaxes(hg.reshape(*lead, HGRN_HEADS, 4, HGRN_DK), -3, -2).reshape(*lead, -1)
    return jnp.concatenate([qkv, hg, gates], axis=-1)

SM_LN = 0
SM_BIN = 6 * D_MODEL
SM_SINK = SM_BIN + D_IN
SM_NG = SM_SINK + LANES
SM_LB = SM_NG + LANES
SM_LOSS = SM_LB + D_MODEL
SM_LEN = SM_LOSS + LANES

MESH = pl.DeviceIdType.MESH


def _mm(a, b):
    return lax.dot_general(a, b, (((1,), (0,)), ((), ())), preferred_element_type=F32)


def _mm_nt(a, b):
    return lax.dot_general(a, b, (((1,), (1,)), ((), ())), preferred_element_type=F32)


def _mm_tn(a, b):
    return lax.dot_general(a, b, (((0,), (0,)), ((), ())), preferred_element_type=F32)


def _bf(v):
    return v.astype(BF16)


def _sig(v):
    return jax.nn.sigmoid(v)


def _ln(z, g, b):
    mu = jnp.mean(z, axis=-1, keepdims=True)
    zc = z - mu
    var = jnp.mean(zc * zc, axis=-1, keepdims=True)
    return zc * lax.rsqrt(var + LN_EPS) * g + b


def _swiglu_act(h1, h3):
    return (h1 * _sig(h1)) * h3


def _params(sem=None):
    return pltpu.CompilerParams(dimension_semantics=sem, vmem_limit_bytes=VMEM_LIMIT)


def _full(shape):
    nd = len(shape)
    return pl.BlockSpec(shape, lambda *_: (0,) * nd)


def _update_rows(rows):
    return max(t for t in range(8, UPDATE_ROWS + 1, 8) if rows % t == 0)


def _resident(shape):
    nd = len(shape)
    return pl.BlockSpec(shape, lambda *_: (0,) * nd, pipeline_mode=pl.Buffered(1))


def _ffn_fwd(name, x, w1t, w3t, w2, g, b, target=None, comm=None):
    n = x.shape[0]
    tm = min(ROW_TILE, n)
    final = target is not None

    def body(*refs):
        if final:
            x_ref, w1_ref, w3_ref, w2_ref, g_ref, b_ref, t_ref, z_ref, o_ref, xb_ref, h1_ref, h3_ref, loss_ref = refs
        else:
            x_ref, w1_ref, w3_ref, w2_ref, g_ref, b_ref, z_ref, o_ref, xb_ref, h1_ref, h3_ref, ob_ref = refs
        xb = _bf(x_ref[...])
        xb_ref[...] = xb
        h1 = _mm_nt(xb, w1_ref[...])
        h3 = _mm_nt(xb, w3_ref[...])
        h1_ref[...] = _bf(h1)
        h3_ref[...] = _bf(h3)
        z = DEEPNORM_ALPHA * x_ref[...] + 0.5 * _mm(_bf(_swiglu_act(h1, h3)), w2_ref[...])
        z_ref[...] = z
        y = _ln(z, g_ref[...], b_ref[...])
        if final:
            e = y - t_ref[...]

            @pl.when(pl.program_id(0) == 0)
            def _():
                loss_ref[...] = jnp.zeros_like(loss_ref)

            loss_ref[...] += jnp.sum(e * e) * (0.5 / D_MODEL)
            o_ref[...] = e * (1.0 / D_MODEL)
        else:
            o_ref[...] = y
            ob_ref[...] = _bf(y)

    row = pl.BlockSpec((tm, D_MODEL), lambda i: (i, 0))
    hid = pl.BlockSpec((tm, D_FF), lambda i: (i, 0))
    wres = _resident((D_FF, D_MODEL))
    vec = _full((1, D_MODEL))
    in_specs = [row, wres, wres, wres, vec, vec]
    args = [x, w1t, w3t, w2, g, b]
    hid_shape = jax.ShapeDtypeStruct((n, D_FF), BF16)
    out_specs = [row, row, row, hid, hid]
    out_shape = ([jax.ShapeDtypeStruct((n, D_MODEL), F32)] * 2 + [jax.ShapeDtypeStruct((n, D_MODEL), BF16)]
                 + [hid_shape] * 2)
    if final:
        in_specs.append(row)
        args.append(target)
        out_specs.append(_full((1, LANES)))
        out_shape.append(jax.ShapeDtypeStruct((1, LANES), F32))
    else:
        out_specs.append(row)
        out_shape.append(jax.ShapeDtypeStruct((n, D_MODEL), BF16))
    return _hosted(
        body, comm, name=name, grid=(n // tm,), in_specs=in_specs, out_specs=out_specs, out_shape=out_shape,
        scratch_shapes=[], compiler_params=_params(("arbitrary",)), args=args)


def _ffn_bwd(name, h1s, h3s, z, dout, w1t, w3t, w2, g, b, comm=None, with_dx=True):
    n = z.shape[0]
    tm = min(FFN_TILE, n)

    def body(h1_ref, h3_ref, z_ref, do_ref, *rest):
        if with_dx:
            w1_ref, w3_ref, w2_ref, g_ref, b_ref, dx_ref, a_ref, dh1_ref, dh3_ref, df_ref, dg_ref, db_ref = rest
        else:
            w2_ref, g_ref, b_ref, dx_ref, a_ref, dh1_ref, dh3_ref, df_ref, dg_ref, db_ref = rest
        _, vjp = jax.vjp(_ln, z_ref[...], g_ref[...], b_ref[...])
        dz, dg, db = vjp(do_ref[...])

        @pl.when(pl.program_id(0) == 0)
        def _():
            dg_ref[...] = jnp.zeros_like(dg_ref)
            db_ref[...] = jnp.zeros_like(db_ref)

        dg_ref[...] += dg
        db_ref[...] += db
        df = _bf(0.5 * dz)
        df_ref[...] = df
        a, act_vjp = jax.vjp(_swiglu_act, h1_ref[...].astype(F32), h3_ref[...].astype(F32))
        dh1, dh3 = act_vjp(_mm_nt(df, w2_ref[...]))
        dh1 = _bf(dh1)
        dh3 = _bf(dh3)
        a_ref[...] = _bf(a)
        dh1_ref[...] = dh1
        dh3_ref[...] = dh3
        if with_dx:
            dx_ref[...] = DEEPNORM_ALPHA * dz + _mm(dh1, w1_ref[...]) + _mm(dh3, w3_ref[...])
        else:
            dx_ref[...] = dz

    row = pl.BlockSpec((tm, D_MODEL), lambda i: (i, 0))
    hid = pl.BlockSpec((tm, D_FF), lambda i: (i, 0))
    wres = _resident((D_FF, D_MODEL))
    vec = _full((1, D_MODEL))
    hid_shape = jax.ShapeDtypeStruct((n, D_FF), BF16)
    weights = [w1t, w3t, w2] if with_dx else [w2]
    return _hosted(
        body, comm, name=name, grid=(n // tm,),
        in_specs=[hid, hid, row, row, *[wres] * len(weights), vec, vec],
        out_specs=[row, hid, hid, hid, row, vec, vec],
        out_shape=[jax.ShapeDtypeStruct((n, D_MODEL), F32), hid_shape, hid_shape, hid_shape,
                   jax.ShapeDtypeStruct((n, D_MODEL), BF16),
                   jax.ShapeDtypeStruct((1, D_MODEL), F32), jax.ShapeDtypeStruct((1, D_MODEL), F32)],
        scratch_shapes=[], compiler_params=_params(("arbitrary",)),
        args=[h1s, h3s, z, dout, *weights, g, b])


def _ffn_dx(name, dz, dh1, dh3, w1t, w3t, comm=None):
    n = dz.shape[0]
    tm = min(ROW_TILE, n)

    def body(dz_ref, dh1_ref, dh3_ref, w1_ref, w3_ref, dx_ref):
        dx_ref[...] = DEEPNORM_ALPHA * dz_ref[...] + _mm(dh1_ref[...], w1_ref[...]) + _mm(dh3_ref[...], w3_ref[...])

    row = pl.BlockSpec((tm, D_MODEL), lambda i: (i, 0))
    hid = pl.BlockSpec((tm, D_FF), lambda i: (i, 0))
    wres = _resident((D_FF, D_MODEL))
    return _hosted(
        body, comm, name=name, grid=(n // tm,), in_specs=[row, hid, hid, wres, wres], out_specs=[row],
        out_shape=[jax.ShapeDtypeStruct((n, D_MODEL), F32)], scratch_shapes=[],
        compiler_params=_params(("arbitrary",)), args=[dz, dh1, dh3, w1t, w3t])


def _operand_spec(arr, mode, tn, width, parts):
    if mode == "shared":
        return pl.BlockSpec((tn, width), lambda s, k: (k, 0))
    assert mode == "cols" and arr.shape[1] == parts * width
    return pl.BlockSpec((tn, width), lambda s, k: (k, s))


def _grad_matmul(name, a, a_mode, ka, b, b_mode, kb, colsum=False, comm=None, parts=N_CHIPS):
    n = a.shape[-2]
    tn = min(GRAD_ROWS, n)
    nk = n // tn

    def body(*refs):
        if colsum:
            a_ref, b_ref, o_ref, cs_ref, acc = refs
        else:
            a_ref, b_ref, o_ref, acc = refs
        k = pl.program_id(1)
        av = a_ref[...]
        bv = b_ref[...]

        @pl.when(k == 0)
        def _():
            acc[...] = jnp.zeros_like(acc)
            if colsum:
                cs_ref[...] = jnp.zeros_like(cs_ref)

        acc[...] += _mm_tn(av, bv)
        if colsum:
            cs_ref[...] += jnp.sum(bv.astype(F32), axis=0, keepdims=True)

        @pl.when(k == nk - 1)
        def _():
            o_ref[0] = _bf(acc[...])

    out_specs = [pl.BlockSpec((1, ka, kb), lambda s, k: (s, 0, 0))]
    out_shape = [jax.ShapeDtypeStruct((parts, ka, kb), BF16)]
    if colsum:
        out_specs.append(pl.BlockSpec((1, kb), lambda s, k: (0, s)))
        out_shape.append(jax.ShapeDtypeStruct((1, parts * kb), F32))
    return _hosted(
        body, comm, name=name, grid=(parts, nk),
        in_specs=[_operand_spec(a, a_mode, tn, ka, parts), _operand_spec(b, b_mode, tn, kb, parts)],
        out_specs=out_specs, out_shape=out_shape,
        scratch_shapes=[pltpu.VMEM((ka, kb), F32)],
        compiler_params=_params(("arbitrary", "arbitrary")), args=[a, b])


def _kernel_block_of(ref_block):
    attn_blocks, gate_blocks = ATTN_WIDTH // LANES, GATES_WIDTH // LANES
    hgrn_blocks = HGRN_HEADS * HGRN_HEAD_WIDTH // LANES
    if ref_block < attn_blocks:
        return gate_blocks + hgrn_blocks + ref_block
    if ref_block < attn_blocks + hgrn_blocks:
        kind, head = divmod(ref_block - attn_blocks, HGRN_HEADS)
        return gate_blocks + head * (HGRN_HEAD_WIDTH // LANES) + kind
    return ref_block - attn_blocks - hgrn_blocks


def _reorder_w_in(name, w4, to_kernel_order):
    per = IN_SHARD // LANES
    tr = MIX_TILE

    def body(i_ref, o_ref):
        for g in range(D_IN // LANES):
            s, b = divmod(g, per)
            k = _kernel_block_of(g)
            if to_kernel_order:
                o_ref[:, k * LANES:(k + 1) * LANES] = i_ref[s, :, b * LANES:(b + 1) * LANES]
            else:
                ks, kb = divmod(k, per)
                o_ref[s, :, b * LANES:(b + 1) * LANES] = i_ref[ks, :, kb * LANES:(kb + 1) * LANES]

    in_spec = pl.BlockSpec((N_CHIPS, tr, IN_SHARD), lambda i: (0, i, 0))
    if to_kernel_order:
        out_spec, out_shape = pl.BlockSpec((tr, D_IN), lambda i: (i, 0)), (D_MODEL, D_IN)
    else:
        out_spec, out_shape = in_spec, (N_CHIPS, D_MODEL, IN_SHARD)
    return pl.pallas_call(
        body, name=name, grid=(D_MODEL // tr,), in_specs=[in_spec], out_specs=out_spec,
        out_shape=jax.ShapeDtypeStruct(out_shape, w4.dtype), compiler_params=_params(("arbitrary",)),
    )(w4)


def _in_proj(x1, w_in_g, b_in, comm=None):
    n = x1.shape[0]
    tm = min(ROW_TILE, n)

    def body(x_ref, w_ref, b_ref, o_ref):
        xv = x_ref[...]
        for j in range(N_CHIPS):
            cols = slice(j * IN_SHARD, (j + 1) * IN_SHARD)
            o_ref[:, cols] = _mm(xv, w_ref[:, cols]) + b_ref[:, cols]

    return _hosted(
        body, comm, name="in_proj", grid=(n // tm,),
        in_specs=[pl.BlockSpec((tm, D_MODEL), lambda i: (i, 0)),
                  _resident((D_MODEL, D_IN)), _full((1, D_IN))],
        out_specs=[pl.BlockSpec((tm, D_IN), lambda i: (i, 0))],
        out_shape=[jax.ShapeDtypeStruct((n, D_IN), F32)],
        scratch_shapes=[],
        compiler_params=_params(("arbitrary",)), args=[x1, w_in_g, b_in])


def _in_proj_dx(dproj, w_in_g, dz2, comm=None):
    n = dproj.shape[0]
    tm = min(ROW_TILE, n)

    def body(dp_ref, w_ref, dz_ref, o_ref):
        dx = DEEPNORM_ALPHA * dz_ref[...]
        for j in range(N_CHIPS):
            cols = slice(j * IN_SHARD, (j + 1) * IN_SHARD)
            dx = dx + _mm_nt(dp_ref[:, cols], w_ref[:, cols])
        o_ref[...] = dx

    return _hosted(
        body, comm, name="in_proj_dx", grid=(n // tm,),
        in_specs=[pl.BlockSpec((tm, D_IN), lambda i: (i, 0)),
                  _resident((D_MODEL, D_IN)),
                  pl.BlockSpec((tm, D_MODEL), lambda i: (i, 0))],
        out_specs=[pl.BlockSpec((tm, D_MODEL), lambda i: (i, 0))],
        out_shape=[jax.ShapeDtypeStruct((n, D_MODEL), F32)],
        scratch_shapes=[],
        compiler_params=_params(("arbitrary",)), args=[dproj, w_in_g, dz2])


def _rope_tables(seq_len):
    pos = jnp.arange(seq_len, dtype=F32)
    inv_freq = ROPE_THETA ** (-jnp.arange(0, ROPE_DIM, 2, dtype=F32) / ROPE_DIM)
    ang = pos[:, None] * inv_freq[None, :]
    cos, sin = jnp.cos(ang), jnp.sin(ang)
    half = ROPE_DIM // 2
    rest = HEAD_DIM - ROPE_DIM
    ones = jnp.ones((seq_len, rest), F32)
    zeros = jnp.zeros((seq_len, rest), F32)
    zh = jnp.zeros((seq_len, half), F32)
    c = jnp.concatenate([cos, cos, ones], axis=1)
    sa = jnp.concatenate([-sin, zh, zeros], axis=1)
    sb = jnp.concatenate([zh, sin, zeros], axis=1)
    reps = LANES // HEAD_DIM
    return tuple(jnp.tile(t, (1, reps)) for t in (c, sa, sb))


def _rope(t, c, sa, sb):
    w = t.shape[1]
    reps = w // LANES
    half = ROPE_DIM // 2
    return (t * jnp.tile(c, (1, reps)) + pltpu.roll(t, w - half, 1) * jnp.tile(sa, (1, reps))
            + pltpu.roll(t, half, 1) * jnp.tile(sb, (1, reps)))


def _rope_transposed(g, c, sa, sb):
    w = g.shape[1]
    reps = w // LANES
    half = ROPE_DIM // 2
    return (g * jnp.tile(c, (1, reps)) + pltpu.roll(g * jnp.tile(sa, (1, reps)), half, 1)
            + pltpu.roll(g * jnp.tile(sb, (1, reps)), w - half, 1))


GROUP = N_Q_HEADS // N_KV_HEADS


def _both_halves(t_pair, which):
    lo = lax.broadcasted_iota(jnp.int32, t_pair.shape, 1) < HEAD_DIM
    swapped = pltpu.roll(t_pair, HEAD_DIM, 1)
    return _bf(jnp.where(lo, t_pair, swapped) if which == 0 else jnp.where(lo, swapped, t_pair))


def _stack_heads(ref_or_val, kh):
    lo = lax.broadcasted_iota(jnp.int32, (ATTN_BLOCK, LANES), 1) < HEAD_DIM
    rows = []
    for gp in range(GROUP // 2):
        pair = kh * (GROUP // 2) + gp
        t = ref_or_val[:, pair * LANES:(pair + 1) * LANES]
        rows += [jnp.where(lo, t, jnp.zeros_like(t)), jnp.where(lo, jnp.zeros_like(t), t)]
    return jnp.concatenate(rows, axis=0)


def _unstack_pairs(stacked):
    lo = lax.broadcasted_iota(jnp.int32, (ATTN_BLOCK, LANES), 1) < HEAD_DIM
    b = ATTN_BLOCK
    return [jnp.where(lo, stacked[2 * gp * b:(2 * gp + 1) * b], stacked[(2 * gp + 1) * b:(2 * gp + 2) * b])
            for gp in range(GROUP // 2)]


def _attn_mask_t(n):
    cols = GROUP * ATTN_BLOCK
    kj = lax.broadcasted_iota(jnp.int32, (2 * ATTN_BLOCK, cols), 0)
    qi = lax.broadcasted_iota(jnp.int32, (2 * ATTN_BLOCK, cols), 1) % ATTN_BLOCK
    dist = qi + ATTN_BLOCK - kj
    return (dist >= 0) & (dist < ATTN_BLOCK) & (n * ATTN_BLOCK + kj - ATTN_BLOCK >= 0)


def _sink_row(sink_ref, kh):
    col = lax.broadcasted_iota(jnp.int32, (1, GROUP * ATTN_BLOCK), 1)
    row = jnp.full((1, GROUP * ATTN_BLOCK), sink_ref[0, kh * GROUP + GROUP - 1], F32)
    for i in reversed(range(GROUP - 1)):
        row = jnp.where(col < (i + 1) * ATTN_BLOCK, sink_ref[0, kh * GROUP + i], row)
    return row


def _attn_probs_t(q_masked, k_sel, mask_t, sink):
    s = _mm_nt(k_sel, q_masked) * (HEAD_DIM ** -0.5)
    s = jnp.where(mask_t, s, NEG_INF)
    m = jnp.maximum(jnp.max(s, axis=0, keepdims=True), sink)
    p = jnp.exp(s - m)
    e_sink = jnp.exp(sink - m)
    denom = jnp.sum(p, axis=0, keepdims=True) + e_sink
    return p / denom, e_sink / denom


def _attn_fwd(proj, tabs, sinks):
    n_tok = proj.shape[0]
    nb = n_tok // ATTN_BLOCK

    def body(q_ref, k_ref, v_ref, c_ref, sa_ref, sb_ref, sink_ref, y_ref, kprev, vprev):
        n = pl.program_id(0)

        @pl.when(n == 0)
        def _():
            kprev[...] = jnp.zeros_like(kprev)
            vprev[...] = jnp.zeros_like(vprev)

        c, sa, sb = c_ref[...], sa_ref[...], sb_ref[...]
        qr = _bf(_rope(q_ref[...], c, sa, sb))
        kr = _rope(k_ref[...], c, sa, sb)
        vc = v_ref[...]
        kk = jnp.concatenate([kprev[...], kr], axis=0)
        vv = jnp.concatenate([vprev[...], vc], axis=0)
        kprev[...] = kr
        vprev[...] = vc
        mask = _attn_mask_t(n)
        for kh in range(N_KV_HEADS):
            r, which = divmod(kh, 2)
            kb = _both_halves(kk[:, r * LANES:(r + 1) * LANES], which)
            vb = _both_halves(vv[:, r * LANES:(r + 1) * LANES], which)
            probs, _ = _attn_probs_t(_stack_heads(qr, kh), kb, mask, _sink_row(sink_ref, kh))
            for gp, out in enumerate(_unstack_pairs(_mm_tn(_bf(probs), vb))):
                pair = kh * (GROUP // 2) + gp
                y_ref[:, pair * LANES:(pair + 1) * LANES] = _bf(out)

    blk = lambda width, col: pl.BlockSpec((ATTN_BLOCK, width), lambda n: (n, col))
    tab = pl.BlockSpec((ATTN_BLOCK, LANES), lambda n: (n, 0))
    kvw = N_KV_HEADS * HEAD_DIM
    return pl.pallas_call(
        body, name="attn_fwd", grid=(nb,),
        in_specs=[blk(D_MODEL, COL_Q), blk(kvw, COL_K), blk(kvw, COL_V), tab, tab, tab,
                  pl.BlockSpec(memory_space=pltpu.SMEM)],
        out_specs=pl.BlockSpec((ATTN_BLOCK, D_MODEL), lambda n: (n, 0)),
        out_shape=jax.ShapeDtypeStruct((n_tok, D_MODEL), BF16),
        scratch_shapes=[pltpu.VMEM((ATTN_BLOCK, kvw), F32), pltpu.VMEM((ATTN_BLOCK, kvw), F32)],
        compiler_params=_params(("arbitrary",)),
    )(proj, proj, proj, *tabs, sinks)


def _attn_bwd(proj, dy, tabs, sinks, dproj, comm=None):
    n_tok = proj.shape[0]
    nb = n_tok // ATTN_BLOCK
    kvw = N_KV_HEADS * HEAD_DIM

    def body(q_ref, k_ref, v_ref, do_ref, c_ref, sa_ref, sb_ref, cp_ref, sap_ref, sbp_ref, sink_ref, _,
             dqkv_ref, dsink_ref, kprev, vprev, dkc, dvc, dqc):
        n = pl.program_id(0)

        @pl.when(n == 0)
        def _():
            for ref in (kprev, vprev, dkc, dvc, dqc, dsink_ref):
                ref[...] = jnp.zeros_like(ref)

        prev_tabs = (cp_ref[...], sap_ref[...], sbp_ref[...])

        @pl.when(n < nb)
        def _():
            c, sa, sb = c_ref[...], sa_ref[...], sb_ref[...]
            qr = _bf(_rope(q_ref[...], c, sa, sb))
            kr = _rope(k_ref[...], c, sa, sb)
            vc = v_ref[...]
            kk = jnp.concatenate([kprev[...], kr], axis=0)
            vv = jnp.concatenate([vprev[...], vc], axis=0)
            kprev[...] = kr
            vprev[...] = vc
            mask = _attn_mask_t(n)
            lane = lax.broadcasted_iota(jnp.int32, (1, LANES), 1)
            lo2 = lax.broadcasted_iota(jnp.int32, (2 * ATTN_BLOCK, LANES), 1) < HEAD_DIM
            dsink = jnp.zeros((1, LANES), F32)
            dq_pairs = []
            dk_full = []
            dv_full = []
            for kh in range(N_KV_HEADS):
                r, which = divmod(kh, 2)
                kb = _both_halves(kk[:, r * LANES:(r + 1) * LANES], which)
                vb = _both_halves(vv[:, r * LANES:(r + 1) * LANES], which)
                qs = _stack_heads(qr, kh)
                dos = _stack_heads(do_ref, kh)
                probs, p_sink = _attn_probs_t(qs, kb, mask, _sink_row(sink_ref, kh))
                dp = _mm_nt(vb, dos)
                delta = jnp.sum(probs * dp, axis=0, keepdims=True)
                ds = _bf(probs * (dp - delta) * (HEAD_DIM ** -0.5))
                sink_terms = p_sink * delta
                for i in range(GROUP):
                    head_sum = jnp.sum(sink_terms[:, i * ATTN_BLOCK:(i + 1) * ATTN_BLOCK])
                    dsink = dsink + jnp.where(lane == kh * GROUP + i, -head_sum, 0.0)
                dq_pairs += _unstack_pairs(_mm_tn(ds, kb))
                dk_acc = _mm(ds, qs)
                dv_acc = _mm(_bf(probs), dos)
                dk_full.append(dk_acc + pltpu.roll(dk_acc, HEAD_DIM, 1))
                dv_full.append(dv_acc + pltpu.roll(dv_acc, HEAD_DIM, 1))
            dk_pairs = [jnp.where(lo2, dk_full[2 * r], dk_full[2 * r + 1]) for r in range(N_KV_HEADS // 2)]
            dv_pairs = [jnp.where(lo2, dv_full[2 * r], dv_full[2 * r + 1]) for r in range(N_KV_HEADS // 2)]
            dsink_ref[...] += dsink
            dqkv_ref[:, :D_MODEL] = _bf(dqc[...])
            dqc[...] = _rope_transposed(jnp.concatenate(dq_pairs, axis=1), c, sa, sb)
            dk_all = jnp.concatenate(dk_pairs, axis=1)
            dv_all = jnp.concatenate(dv_pairs, axis=1)
            dqkv_ref[:, D_MODEL:D_MODEL + kvw] = _bf(_rope_transposed(dkc[...] + dk_all[:ATTN_BLOCK], *prev_tabs))
            dqkv_ref[:, D_MODEL + kvw:] = _bf(dvc[...] + dv_all[:ATTN_BLOCK])
            dkc[...] = dk_all[ATTN_BLOCK:]
            dvc[...] = dv_all[ATTN_BLOCK:]

        @pl.when(n == nb)
        def _():
            dqkv_ref[:, :D_MODEL] = _bf(dqc[...])
            dqkv_ref[:, D_MODEL:D_MODEL + kvw] = _bf(_rope_transposed(dkc[...], *prev_tabs))
            dqkv_ref[:, D_MODEL + kvw:] = _bf(dvc[...])

    cur = lambda n: jnp.minimum(n, nb - 1)
    prev = lambda n: jnp.maximum(n - 1, 0)
    blk = lambda width, col: pl.BlockSpec((ATTN_BLOCK, width), lambda n: (cur(n), col))
    tab = pl.BlockSpec((ATTN_BLOCK, LANES), lambda n: (cur(n), 0))
    tabp = pl.BlockSpec((ATTN_BLOCK, LANES), lambda n: (prev(n), 0))
    return _hosted(
        body, comm, name="attn_bwd", grid=(nb + 1,),
        in_specs=[blk(D_MODEL, COL_Q), blk(kvw, COL_K), blk(kvw, COL_V), blk(D_MODEL, 0), tab, tab, tab, tabp, tabp, tabp,
                  pl.BlockSpec(memory_space=pltpu.SMEM), pl.BlockSpec(memory_space=pl.ANY)],
        out_specs=[pl.BlockSpec((ATTN_BLOCK, ATTN_WIDTH), lambda n: (prev(n), COL_ATTN)),
                   pl.BlockSpec((1, LANES), lambda n: (0, 0))],
        out_shape=[jax.ShapeDtypeStruct(dproj.shape, dproj.dtype), jax.ShapeDtypeStruct((1, LANES), F32)],
        scratch_shapes=[pltpu.VMEM((ATTN_BLOCK, kvw), F32)] * 4 + [pltpu.VMEM((ATTN_BLOCK, D_MODEL), F32)],
        compiler_params=_params(("arbitrary",)), args=[proj, proj, proj, dy, *tabs, *tabs, sinks, dproj],
        aliases={11: 0})


def _bmm(a, b):
    return lax.dot_general(a, b, (((2,), (1,)), ((0,), (0,))), preferred_element_type=F32)


def _bmm_nt(a, b):
    return lax.dot_general(a, b, (((2,), (2,)), ((0,), (0,))), preferred_element_type=F32)


def _bmm_tn(a, b):
    return lax.dot_general(a, b, (((1,), (1,)), ((0,), (0,))), preferred_element_type=F32)


def _tril(cb, upper=False):
    shape = (cb, HGRN_CHUNK, HGRN_CHUNK)
    r, c = lax.broadcasted_iota(jnp.int32, shape, 1), lax.broadcasted_iota(jnp.int32, shape, 2)
    return (r <= c) if upper else (r >= c)


def _tri_matmul(x, upper):
    return lax.dot_general(_tril(x.shape[0], upper).astype(F32), x, (((2,), (1,)), ((0,), (0,))),
                           precision=lax.Precision.HIGHEST, preferred_element_type=F32)


@jax.custom_vjp
def _chunk_cumsum(x):
    return _tri_matmul(x, False)


_chunk_cumsum.defvjp(lambda x: (_tri_matmul(x, False), None), lambda _, g: (_tri_matmul(g, True),))


def _hg_elem(fl, qh, lb):
    f = lb + (1.0 - lb) * _sig(fl)
    k = 1.0 - f
    gc = _chunk_cumsum(jnp.log(f))
    last = lax.broadcasted_iota(jnp.int32, gc.shape, 1) == HGRN_CHUNK - 1
    g_last = jnp.sum(jnp.where(last, gc, 0.0), axis=1, keepdims=True)
    q = qh * _sig(qh)
    return q * jnp.exp(gc), k * jnp.exp(-gc), k * jnp.exp(g_last - gc), jnp.exp(g_last)


def _hg_out(q_dec, k_inv, v, st):
    sc = jnp.where(_tril(q_dec.shape[0]), _bmm_nt(_bf(q_dec), _bf(k_inv)), 0.0)
    return _bmm(_bf(sc), _bf(v)) + _bmm_nt(_bf(q_dec), _bf(st)), sc


def _hg_post(o, og, ng):
    on = o * lax.rsqrt(jnp.mean(o * o, axis=-1, keepdims=True) + RMS_EPS) * ng
    return on * (og * _sig(og))


def _hgrn_specs(n_tok, rev):
    nc = n_tok // HGRN_CHUNK
    cb = min(HGRN_CHUNKS_PER_STEP, nc)
    nt = nc // cb
    rows = cb * HGRN_CHUNK
    tt = (lambda t: nt - 1 - t) if rev else (lambda t: t)
    col = lambda base: pl.BlockSpec((rows, LANES), lambda h, t: (tt(t), base + h))
    head_cols = pl.BlockSpec((rows, HGRN_HEAD_WIDTH), lambda h, t: (tt(t), COL_HGRN + h))
    head_vec = pl.BlockSpec((1, LANES), lambda h, t: (0, h))
    one_vec = pl.BlockSpec((1, LANES), lambda h, t: (0, 0))
    state = pl.BlockSpec((1, cb, HGRN_DK, HGRN_DK), lambda h, t: (h, tt(t), 0, 0))
    return nc, cb, nt, col, head_cols, head_vec, one_vec, state


def _hgrn_fwd(proj, lb, ng):
    n_tok = proj.shape[0]
    nc, cb, nt, col, head_cols, head_vec, one_vec, state = _hgrn_specs(n_tok, False)

    def body(in_ref, lb_ref, ng_ref, y_ref, st_ref, s_acc):
        @pl.when(pl.program_id(1) == 0)
        def _():
            s_acc[...] = jnp.zeros_like(s_acc)

        fl, qh, v, og = (in_ref[:, i * LANES:(i + 1) * LANES].reshape(cb, HGRN_CHUNK, LANES) for i in range(4))
        q_dec, k_inv, k_end, decay = _hg_elem(fl, qh, lb_ref[...])
        upd = _bmm_tn(_bf(v), _bf(k_end))
        st = s_acc[...]
        for ci in range(cb):
            st_ref[0, ci] = st
            st = st * decay[ci] + upd[ci]
        s_acc[...] = st
        o, _ = _hg_out(q_dec, k_inv, v, st_ref[0])
        y_ref[...] = _bf(_hg_post(o, og, ng_ref[...]).reshape(cb * HGRN_CHUNK, LANES))

    return pl.pallas_call(
        body, name="hgrn_fwd", grid=(HGRN_HEADS, nt),
        in_specs=[head_cols, head_vec, one_vec],
        out_specs=[col(0), state],
        out_shape=[jax.ShapeDtypeStruct((n_tok, D_MODEL), BF16),
                   jax.ShapeDtypeStruct((HGRN_HEADS, nc, HGRN_DK, HGRN_DK), F32)],
        scratch_shapes=[pltpu.VMEM((HGRN_DK, HGRN_DK), F32)],
        compiler_params=_params(("arbitrary", "arbitrary")),
    )(proj, lb, ng)


def _hgrn_bwd(proj, lb, ng, states, dy, dproj, comm=None):
    n_tok = proj.shape[0]
    nc, cb, nt, col, head_cols, head_vec, one_vec, state = _hgrn_specs(n_tok, True)

    def body(in_ref, lb_ref, ng_ref, st_ref, dy_ref, _, d_ref, dlb_ref, dng_ref, g_acc, g_all):
        h = pl.program_id(0)
        t = pl.program_id(1)

        @pl.when(t == 0)
        def _():
            g_acc[...] = jnp.zeros_like(g_acc)
            dlb_ref[...] = jnp.zeros_like(dlb_ref)

        @pl.when((t == 0) & (h == 0))
        def _():
            dng_ref[...] = jnp.zeros_like(dng_ref)

        fl, qh, v, og = (in_ref[:, i * LANES:(i + 1) * LANES].reshape(cb, HGRN_CHUNK, LANES) for i in range(4))
        (q_dec, k_inv, k_end, decay), elem_vjp = jax.vjp(_hg_elem, fl, qh, lb_ref[...])
        st = st_ref[0]
        o, sc = _hg_out(q_dec, k_inv, v, st)
        _, post_vjp = jax.vjp(_hg_post, o, og, ng_ref[...])
        do, dog, dng = post_vjp(dy_ref[...].reshape(cb, HGRN_CHUNK, LANES))
        dob, vb, qb = _bf(do), _bf(v), _bf(q_dec)
        dsc = _bf(jnp.where(_tril(cb), _bmm_nt(dob, vb), 0.0))
        p = _bmm_tn(dob, qb)
        g = g_acc[...]
        for ci in reversed(range(cb)):
            g_all[ci] = g
            g = g * decay[ci] + p[ci]
        g_acc[...] = g
        g = g_all[...]
        gb = _bf(g)
        dq_dec = _bmm(dsc, _bf(k_inv)) + _bmm(dob, _bf(st))
        dk_inv = _bmm_tn(dsc, qb)
        dv = _bmm_tn(_bf(sc), dob) + _bmm_nt(_bf(k_end), gb)
        dk_end = _bmm(vb, gb)
        ddecay = jnp.sum(st * g, axis=1, keepdims=True)
        dfl, dqh, dlb = elem_vjp((dq_dec, dk_inv, dk_end, ddecay))
        for i, val in enumerate((dfl, dqh, dv, dog)):
            d_ref[:, i * LANES:(i + 1) * LANES] = _bf(val.reshape(cb * HGRN_CHUNK, LANES))
        dlb_ref[...] += dlb
        dng_ref[...] += dng

    return _hosted(
        body, comm, name="hgrn_bwd", grid=(HGRN_HEADS, nt),
        in_specs=[head_cols, head_vec, one_vec, state, col(0), pl.BlockSpec(memory_space=pl.ANY)],
        out_specs=[head_cols, head_vec, one_vec],
        out_shape=[jax.ShapeDtypeStruct(dproj.shape, dproj.dtype),
                   jax.ShapeDtypeStruct((1, D_MODEL), F32), jax.ShapeDtypeStruct((1, LANES), F32)],
        scratch_shapes=[pltpu.VMEM((HGRN_DK, HGRN_DK), F32), pltpu.VMEM((cb, HGRN_DK, HGRN_DK), F32)],
        compiler_params=_params(("arbitrary", "arbitrary")), args=[proj, lb, ng, states, dy, dproj], aliases={5: 0})


def _lb_fwd(lb_logits):
    def lb_of(l0, l1):
        m = jnp.maximum(l0, l1)
        e0, e1 = jnp.exp(l0 - m), jnp.exp(l1 - m)
        return e0 / (e0 + e1)

    def body(l_ref, o_ref):
        o_ref[...] = lb_of(l_ref[0:1, :], l_ref[1:2, :])

    lb = pl.pallas_call(body, name="lb_fwd", out_shape=jax.ShapeDtypeStruct((1, D_MODEL), F32))(lb_logits)
    return lb, lb_of


def _mix_fwd(y_attn, y_hgrn, proj, x1, w_pa, w_ph, w_out, g, b):
    n = x1.shape[0]
    tm = min(ROW_TILE, n)

    def body(ya_ref, yh_ref, gt_ref, x_ref, wpa, wph, wo, g_ref, b_ref, z_ref, o_ref, pa_ref, ph_ref):
        ya = _mm(ya_ref[...], wpa[...])
        yh = _mm(yh_ref[...], wph[...])
        pa_ref[...] = _bf(ya)
        ph_ref[...] = _bf(yh)
        merged = _sig(gt_ref[:, :D_MODEL]) * ya + _sig(gt_ref[:, D_MODEL:]) * yh
        z = DEEPNORM_ALPHA * x_ref[...] + _mm(_bf(merged), wo[...])
        z_ref[...] = z
        o_ref[...] = _ln(z, g_ref[...], b_ref[...])

    row = pl.BlockSpec((tm, D_MODEL), lambda i: (i, 0))
    gates = pl.BlockSpec((tm, GATES_WIDTH), lambda i: (i, 0))
    sq = _resident((D_MODEL, D_MODEL))
    vec = _full((1, D_MODEL))
    return pl.pallas_call(
        body, name="mix_fwd", grid=(n // tm,),
        in_specs=[row, row, gates, row, sq, sq, sq, vec, vec],
        out_specs=[row, row, row, row],
        out_shape=[jax.ShapeDtypeStruct((n, D_MODEL), F32)] * 2 + [jax.ShapeDtypeStruct((n, D_MODEL), BF16)] * 2,
        compiler_params=_params(("arbitrary",)),
    )(y_attn, y_hgrn, proj, x1, w_pa, w_ph, w_out, g, b)


def _mix_bwd(dx2, z2, pa, ph, proj, w_pa, w_ph, w_out, g, b):
    n = z2.shape[0]
    tm = min(MIX_TILE, n)

    def body(do_ref, z_ref, ya_ref, yh_ref, gt_ref, wpa, wph, wo, g_ref, b_ref,
             dz_ref, dzb_ref, mg_ref, dya_ref, dyh_ref, dyat_ref, dyhg_ref, dgt_ref, dg_ref, db_ref):
        _, vjp = jax.vjp(_ln, z_ref[...], g_ref[...], b_ref[...])
        dz, dg, db = vjp(do_ref[...])

        @pl.when(pl.program_id(0) == 0)
        def _():
            dg_ref[...] = jnp.zeros_like(dg_ref)
            db_ref[...] = jnp.zeros_like(db_ref)

        dg_ref[...] += dg
        db_ref[...] += db
        dz_ref[...] = dz
        ya = ya_ref[...].astype(F32)
        yh = yh_ref[...].astype(F32)
        def merge(ga, gh, ya, yh):
            return _sig(ga) * ya + _sig(gh) * yh

        merged, merge_vjp = jax.vjp(merge, gt_ref[:, :D_MODEL], gt_ref[:, D_MODEL:], ya, yh)
        mg_ref[...] = _bf(merged)
        dzb = _bf(dz)
        dzb_ref[...] = dzb
        dmerged = _mm_nt(dzb, wo[...])
        dga, dgh, dya, dyh = merge_vjp(dmerged)
        dya = _bf(dya)
        dyh = _bf(dyh)
        dya_ref[...] = dya
        dyh_ref[...] = dyh
        dgt_ref[:, :D_MODEL] = _bf(dga)
        dgt_ref[:, D_MODEL:] = _bf(dgh)
        dyat_ref[...] = _bf(_mm_nt(dya, wpa[...]))
        dyhg_ref[...] = _mm_nt(dyh, wph[...])

    row = pl.BlockSpec((tm, D_MODEL), lambda i: (i, 0))
    gates = pl.BlockSpec((tm, GATES_WIDTH), lambda i: (i, 0))
    sq = _resident((D_MODEL, D_MODEL))
    vec = _full((1, D_MODEL))
    f32_row = jax.ShapeDtypeStruct((n, D_MODEL), F32)
    bf_row = jax.ShapeDtypeStruct((n, D_MODEL), BF16)
    vec_shape = jax.ShapeDtypeStruct((1, D_MODEL), F32)
    return pl.pallas_call(
        body, name="mix_bwd", grid=(n // tm,),
        in_specs=[row, row, row, row, gates, sq, sq, sq, vec, vec],
        out_specs=[row, row, row, row, row, row, row, gates, vec, vec],
        out_shape=[f32_row, bf_row, bf_row, bf_row, bf_row, bf_row, f32_row,
                   jax.ShapeDtypeStruct((n, D_IN), BF16), vec_shape, vec_shape],
        compiler_params=_params(("arbitrary",)),
    )(dx2, z2, pa, ph, proj, w_pa, w_ph, w_out, g, b)


def _position():
    x, y, c = lax.axis_index("x"), lax.axis_index("y"), lax.axis_index("c")
    chips = [(1 - x, y), (x, 1 - y), (1 - x, 1 - y)]
    return x, y, c, chips


def _any_specs(k):
    return [pl.BlockSpec(memory_space=pl.ANY)] * k


class _GatherWeights:
    def __init__(self, shards):
        nw = len(shards)
        self.inputs = list(shards)
        self.out_shape = [jax.ShapeDtypeStruct((N_CHIPS, *s.shape), s.dtype) for s in shards]
        self.scratch = [pltpu.SemaphoreType.DMA((nw,)), pltpu.SemaphoreType.DMA((nw * 6,)),
                        pltpu.SemaphoreType.DMA((nw * 6,))]

    def _copies(self, ins, outs, sems):
        nw = len(ins)
        local_sem, send_sem, recv_sem = sems
        x, y, c, chips = _position()
        me = 2 * x + y
        sibling = (x, y, 1 - c)
        half_rows = [s.shape[0] // 2 for s in self.inputs]

        def half(w, chip_idx, which):
            return outs[w].at[chip_idx, pl.ds(which * half_rows[w], half_rows[w])]

        def remote(w, k, src, dst, to):
            return pltpu.make_async_remote_copy(src_ref=src, dst_ref=dst, send_sem=send_sem.at[w * 6 + k],
                                                recv_sem=recv_sem.at[w * 6 + k], device_id=to, device_id_type=MESH)

        local = [pltpu.make_async_copy(ins[w], outs[w].at[me], local_sem.at[w]) for w in range(nw)]
        first = [remote(w, j, ins[w].at[pl.ds(c * half_rows[w], half_rows[w])], half(w, me, c), (px, py, c))
                 for w in range(nw) for j, (px, py) in enumerate(chips)]
        landed = [half(w, 2 * px + py, c) for w in range(nw) for (px, py) in chips]
        arrive = [remote(w, j, landed[w * 3 + j], landed[w * 3 + j], (px, py, c))
                  for w in range(nw) for j, (px, py) in enumerate(chips)]
        passed = [remote(w, 3 + j, landed[w * 3 + j], landed[w * 3 + j], sibling) for w in range(nw) for j in range(3)]
        from_sibling = [remote(w, 3 + j, half(w, 2 * px + py, 1 - c), half(w, 2 * px + py, 1 - c), sibling)
                        for w in range(nw) for j, (px, py) in enumerate(chips)]
        return local, first, arrive, passed, from_sibling

    def start(self, ins, outs, sems):
        local, first, _, _, _ = self._copies(ins, outs, sems)
        for cp in local + first:
            cp.start()

    def finish(self, ins, outs, sems):
        local, first, arrive, passed, from_sibling = self._copies(ins, outs, sems)
        for cp_in, cp_on in zip(arrive, passed):
            cp_in.wait_recv()
            cp_on.start()
        for cp in from_sibling:
            cp.wait_recv()
        for cp in first + passed:
            cp.wait_send()
        for cp in local:
            cp.wait()


class _ExchangeGrads:
    def __init__(self, grads, rows=None, into=None):
        nw = len(grads)
        self.rows = [rows or (0, g.shape[1]) for g in grads]
        self.inputs = list(grads) + list(into or [])
        self.aliases = {nw + i: i for i in range(nw)} if into else {}
        self.out_shape = [jax.ShapeDtypeStruct(g.shape, g.dtype) for g in grads]
        self.scratch = [pltpu.SemaphoreType.DMA((nw,)), pltpu.SemaphoreType.DMA((nw * 3,)),
                        pltpu.SemaphoreType.DMA((nw * 3,))]

    def _copies(self, ins, outs, sems):
        nw = len(outs)
        local_sem, send_sem, recv_sem = sems
        x, y, c, chips = _position()
        me = 2 * x + y
        rows = [pl.ds(*r) for r in self.rows]

        def remote(w, j, src, dst, chip):
            return pltpu.make_async_remote_copy(src_ref=src, dst_ref=dst, send_sem=send_sem.at[w * 3 + j],
                                                recv_sem=recv_sem.at[w * 3 + j], device_id=(*chip, c),
                                                device_id_type=MESH)

        local = [pltpu.make_async_copy(ins[w].at[me, rows[w]], outs[w].at[me, rows[w]], local_sem.at[w])
                 for w in range(nw)]
        sends = [remote(w, j, ins[w].at[2 * px + py, rows[w]], outs[w].at[me, rows[w]], (px, py))
                 for w in range(nw) for j, (px, py) in enumerate(chips)]
        arrive = [remote(w, j, outs[w].at[2 * px + py, rows[w]], outs[w].at[2 * px + py, rows[w]], (px, py))
                  for w in range(nw) for j, (px, py) in enumerate(chips)]
        return local, sends, arrive

    def start(self, ins, outs, sems):
        local, sends, _ = self._copies(ins, outs, sems)
        for cp in local + sends:
            cp.start()

    def finish(self, ins, outs, sems):
        local, sends, arrive = self._copies(ins, outs, sems)
        for cp in arrive:
            cp.wait_recv()
        for cp in sends:
            cp.wait_send()
        for cp in local:
            cp.wait()


def _hosted(body, comm, *, name, grid, in_specs, out_specs, out_shape, scratch_shapes, compiler_params, args,
            aliases=None):
    aliases = aliases or {}
    if comm is None:
        res = pl.pallas_call(body, name=name, grid=grid, in_specs=in_specs, out_specs=out_specs, out_shape=out_shape,
                             scratch_shapes=scratch_shapes, compiler_params=compiler_params,
                             input_output_aliases=aliases)(*args)
        return list(res), []
    n_in, n_out, n_scr = len(in_specs), len(out_specs), len(scratch_shapes)
    c_in, c_out = len(comm.inputs), len(comm.out_shape)
    aliases = {**aliases, **{n_in + i: n_out + o for i, o in getattr(comm, "aliases", {}).items()}}

    def hosted_body(*refs):
        refs = list(refs)
        cut = lambda k: (refs[:k], refs[k:])
        main_in, refs = cut(n_in)
        comm_in, refs = cut(c_in)
        main_out, refs = cut(n_out)
        comm_out, refs = cut(c_out)
        main_scr, comm_scr = cut(n_scr)
        ids = [pl.program_id(a) for a in range(len(grid))]
        first = functools.reduce(jnp.logical_and, [i == 0 for i in ids])
        last = functools.reduce(jnp.logical_and, [i == g - 1 for i, g in zip(ids, grid)])

        @pl.when(first)
        def _():
            comm.start(comm_in, comm_out, comm_scr)

        body(*main_in, *main_out, *main_scr)

        @pl.when(last)
        def _():
            comm.finish(comm_in, comm_out, comm_scr)

    res = pl.pallas_call(
        hosted_body, name=name, grid=grid, in_specs=[*in_specs, *_any_specs(c_in)],
        out_specs=[*out_specs, *_any_specs(c_out)], out_shape=[*out_shape, *comm.out_shape],
        scratch_shapes=[*scratch_shapes, *comm.scratch], compiler_params=compiler_params,
        input_output_aliases=aliases,
    )(*args, *comm.inputs)
    return list(res[:n_out]), list(res[n_out:])


def _run_comm(name, comm):
    def body(*refs):
        refs = list(refs)
        c_in, c_out = len(comm.inputs), len(comm.out_shape)
        ins, outs, sems = refs[:c_in], refs[c_in:c_in + c_out], refs[c_in + c_out:]
        comm.start(ins, outs, sems)
        comm.finish(ins, outs, sems)

    return list(pl.pallas_call(
        body, name=name, in_specs=_any_specs(len(comm.inputs)), out_specs=_any_specs(len(comm.out_shape)),
        out_shape=comm.out_shape, scratch_shapes=comm.scratch,
    )(*comm.inputs))


def _sum_slots(name, slots):
    k = len(slots)
    _, rows, cols = slots[0].shape
    tr = _update_rows(rows)

    def body(*refs):
        for s_ref, o_ref in zip(refs[:k], refs[k:]):
            acc = s_ref[0].astype(F32)
            for i in range(1, N_CHIPS):
                acc = acc + s_ref[i].astype(F32)
            o_ref[...] = acc

    return pl.pallas_call(
        body, name=name, grid=(rows // tr,),
        in_specs=[pl.BlockSpec((N_CHIPS, tr, cols), lambda i: (0, i, 0))] * k,
        out_specs=[pl.BlockSpec((tr, cols), lambda i: (i, 0))] * k,
        out_shape=[jax.ShapeDtypeStruct((rows, cols), F32)] * k,
        compiler_params=_params(("arbitrary",)),
    )(*slots)


class _SwapWithSibling:
    def __init__(self, parts):
        self.inputs = list(parts)
        self.out_shape = [jax.ShapeDtypeStruct(p.shape, p.dtype) for p in parts]
        self.scratch = [pltpu.SemaphoreType.DMA((len(parts),)), pltpu.SemaphoreType.DMA((len(parts),))]

    def _copies(self, ins, outs, sems):
        send_sem, recv_sem = sems
        x, y, c, _ = _position()
        return [pltpu.make_async_remote_copy(src_ref=ins[w], dst_ref=outs[w], send_sem=send_sem.at[w],
                                             recv_sem=recv_sem.at[w], device_id=(x, y, 1 - c), device_id_type=MESH)
                for w in range(len(ins))]

    def start(self, ins, outs, sems):
        for cp in self._copies(ins, outs, sems):
            cp.start()

    def finish(self, ins, outs, sems):
        for cp in self._copies(ins, outs, sems):
            cp.wait()


class _Together:
    def __init__(self, first, second):
        self.parts = (first, second)
        self.inputs = first.inputs + second.inputs
        self.out_shape = first.out_shape + second.out_shape
        self.scratch = first.scratch + second.scratch

    def _split(self, ins, outs, sems):
        a = self.parts[0]
        ni, no, ns = len(a.inputs), len(a.out_shape), len(a.scratch)
        return ((ins[:ni], outs[:no], sems[:ns]), (ins[ni:], outs[no:], sems[ns:]))

    def start(self, ins, outs, sems):
        for part, args in zip(self.parts, self._split(ins, outs, sems)):
            part.start(*args)

    def finish(self, ins, outs, sems):
        for part, args in zip(self.parts, self._split(ins, outs, sems)):
            part.finish(*args)


def _sum_small(part):
    def body(p_ref, o_ref, buf, send_sem, recv_sem):
        x, y, c, _ = _position()
        me = 4 * x + 2 * y + c
        buf[me] = p_ref[...]
        copies = []
        for k in range(1, N_DEV):
            peer = tuple(1 - v if (k >> s) & 1 else v for v, s in ((x, 2), (y, 1), (c, 0)))
            copies.append(pltpu.make_async_remote_copy(src_ref=p_ref, dst_ref=buf.at[me], send_sem=send_sem.at[k - 1],
                                                       recv_sem=recv_sem.at[k - 1], device_id=peer, device_id_type=MESH))
        for cp in copies:
            cp.start()
        for cp in copies:
            cp.wait()
        acc = buf[0]
        for d in range(1, N_DEV):
            acc = acc + buf[d]
        o_ref[...] = acc

    vm = pl.BlockSpec(memory_space=pltpu.VMEM)
    return pl.pallas_call(
        body, name="sum_small", in_specs=[vm], out_specs=vm,
        out_shape=jax.ShapeDtypeStruct((1, SM_LEN), F32),
        scratch_shapes=[pltpu.VMEM((N_DEV, 1, SM_LEN), F32), pltpu.SemaphoreType.DMA((N_DEV - 1,)),
                        pltpu.SemaphoreType.DMA((N_DEV - 1,))],
    )(part)


def _adamw(w, g, m, v):
    m = ADAM_B1 * m + (1.0 - ADAM_B1) * g
    v = ADAM_B2 * v + (1.0 - ADAM_B2) * (g * g)
    m_hat = m / (1.0 - ADAM_B1 ** ADAM_STEP)
    v_hat = v / (1.0 - ADAM_B2 ** ADAM_STEP)
    delta = -ADAM_LR * (m_hat / (jnp.sqrt(v_hat) + ADAM_EPS) + ADAM_WD * w)
    return delta, m, v


def _adam_big(name, groups):
    k = len(groups)
    rows, cols = groups[0][2].shape
    tr = _update_rows(rows)

    def body(*refs):
        for i in range(k):
            p_ref, q_ref, w_ref, m_ref, v_ref = refs[5 * i:5 * i + 5]
            g_ref, d_ref, nm_ref, nv_ref = refs[5 * k + 4 * i:5 * k + 4 * i + 4]
            g = p_ref[...] + q_ref[...]
            g_ref[...] = g
            d_ref[...], nm_ref[...], nv_ref[...] = _adamw(w_ref[...], g, m_ref[...], v_ref[...])

    spec = pl.BlockSpec((tr, cols), lambda i: (i, 0))
    res = pl.pallas_call(
        body, name=name, grid=(rows // tr,), in_specs=[spec] * (5 * k), out_specs=[spec] * (4 * k),
        out_shape=[jax.ShapeDtypeStruct((rows, cols), F32)] * (4 * k),
        compiler_params=_params(("arbitrary",)),
    )(*[a for grp in groups for a in grp])
    return [res[4 * i:4 * i + 4] for i in range(k)]


_SMALL_AT = {"ln1_g": 0, "ln1_b": D_MODEL, "ln2_g": 2 * D_MODEL, "ln2_b": 3 * D_MODEL, "ln3_g": 4 * D_MODEL,
             "ln3_b": 5 * D_MODEL, "b_in": SM_BIN, "attn_sinks": SM_SINK, "hgrn_norm_g": SM_NG}


def _adam_small(total, w, m, v, lb_of):
    names = list(_SMALL)
    k = len(names)

    def body(*refs):
        t_ref = refs[0]
        w_refs, m_refs, v_refs = (refs[1 + i * k:1 + (i + 1) * k] for i in range(3))
        g_refs, d_refs, nm_refs, nv_refs = (refs[1 + (3 + i) * k:1 + (4 + i) * k] for i in range(4))
        for i, name in enumerate(names):
            if name == "hgrn_lb_logits":
                _, vjp = jax.vjp(lb_of, w_refs[i][0:1, :], w_refs[i][1:2, :])
                g_refs[i][0:1, :], g_refs[i][1:2, :] = vjp(t_ref[:, SM_LB:SM_LOSS])
            else:
                at = _SMALL_AT[name]
                g_refs[i][...] = t_ref[:, at:at + w_refs[i].shape[1]]
            d_refs[i][...], nm_refs[i][...], nv_refs[i][...] = _adamw(
                w_refs[i][...], g_refs[i][...], m_refs[i][...], v_refs[i][...])

    shapes = [jax.ShapeDtypeStruct(w[name].shape, F32) for name in names]
    res = pl.pallas_call(body, name="adam_small", out_shape=shapes * 4)(
        total, *[w[n] for n in names], *[m[n] for n in names], *[v[n] for n in names])
    return [dict(zip(names, res[i * k:(i + 1) * k])) for i in range(4)]


_BIG = ("ffn1_w1", "ffn1_w3", "ffn1_w2", "w_in", "w_proj_attn", "w_proj_hgrn", "w_out", "ffn2_w1", "ffn2_w3", "ffn2_w2")
_SMALL = ("ln1_g", "ln1_b", "ln2_g", "ln2_b", "ln3_g", "ln3_b", "b_in", "attn_sinks", "hgrn_norm_g", "hgrn_lb_logits")
_ORDER = ("ln1_g", "ln1_b", "ffn1_w1", "ffn1_w3", "ffn1_w2", "ln2_g", "ln2_b", "w_in", "b_in", "attn_sinks",
          "hgrn_lb_logits", "hgrn_norm_g", "w_proj_attn", "w_proj_hgrn", "w_out", "ln3_g", "ln3_b",
          "ffn2_w1", "ffn2_w3", "ffn2_w2")


_TRANSPOSED = ("ffn1_w1", "ffn1_w3", "ffn2_w1", "ffn2_w3")


def _local_view(name, arr):
    return arr[0].T if name in _TRANSPOSED else arr[0]


def _ffn_grad(name, hidden, other, comm=None):
    (dw,), comm_out = _grad_matmul(name, hidden, "cols", D_FF // FF_GRAD_PARTS, other, "shared", D_MODEL,
                                   comm=comm, parts=FF_GRAD_PARTS)
    return dw.reshape(N_CHIPS, D_FF // N_CHIPS, D_MODEL), comm_out


def kernel(x, ln1_g, ln1_b, ffn1_w1, ffn1_w3, ffn1_w2, ln2_g, ln2_b, w_in, b_in, attn_sinks, hgrn_lb_logits, hgrn_norm_g, w_proj_attn, w_proj_hgrn, w_out, ln3_g, ln3_b, ffn2_w1, ffn2_w3, ffn2_w2, loss_target, m_ln1_g, m_ln1_b, m_ffn1_w1, m_ffn1_w3, m_ffn1_w2, m_ln2_g, m_ln2_b, m_w_in, m_b_in, m_attn_sinks, m_hgrn_lb_logits, m_hgrn_norm_g, m_w_proj_attn, m_w_proj_hgrn, m_w_out, m_ln3_g, m_ln3_b, m_ffn2_w1, m_ffn2_w3, m_ffn2_w2, v_ln1_g, v_ln1_b, v_ffn1_w1, v_ffn1_w3, v_ffn1_w2, v_ln2_g, v_ln2_b, v_w_in, v_b_in, v_attn_sinks, v_hgrn_lb_logits, v_hgrn_norm_g, v_w_proj_attn, v_w_proj_hgrn, v_w_out, v_ln3_g, v_ln3_b, v_ffn2_w1, v_ffn2_w3, v_ffn2_w2):
    w = dict(ln1_g=ln1_g, ln1_b=ln1_b, ffn1_w1=ffn1_w1, ffn1_w3=ffn1_w3, ffn1_w2=ffn1_w2, ln2_g=ln2_g, ln2_b=ln2_b,
             w_in=w_in, b_in=b_in, attn_sinks=attn_sinks, hgrn_lb_logits=hgrn_lb_logits, hgrn_norm_g=hgrn_norm_g,
             w_proj_attn=w_proj_attn, w_proj_hgrn=w_proj_hgrn, w_out=w_out, ln3_g=ln3_g, ln3_b=ln3_b,
             ffn2_w1=ffn2_w1, ffn2_w3=ffn2_w3, ffn2_w2=ffn2_w2)
    mom = dict(ln1_g=m_ln1_g, ln1_b=m_ln1_b, ffn1_w1=m_ffn1_w1, ffn1_w3=m_ffn1_w3, ffn1_w2=m_ffn1_w2, ln2_g=m_ln2_g,
               ln2_b=m_ln2_b, w_in=m_w_in, b_in=m_b_in, attn_sinks=m_attn_sinks, hgrn_lb_logits=m_hgrn_lb_logits,
               hgrn_norm_g=m_hgrn_norm_g, w_proj_attn=m_w_proj_attn, w_proj_hgrn=m_w_proj_hgrn, w_out=m_w_out,
               ln3_g=m_ln3_g, ln3_b=m_ln3_b, ffn2_w1=m_ffn2_w1, ffn2_w3=m_ffn2_w3, ffn2_w2=m_ffn2_w2)
    var = dict(ln1_g=v_ln1_g, ln1_b=v_ln1_b, ffn1_w1=v_ffn1_w1, ffn1_w3=v_ffn1_w3, ffn1_w2=v_ffn1_w2, ln2_g=v_ln2_g,
               ln2_b=v_ln2_b, w_in=v_w_in, b_in=v_b_in, attn_sinks=v_attn_sinks, hgrn_lb_logits=v_hgrn_lb_logits,
               hgrn_norm_g=v_hgrn_norm_g, w_proj_attn=v_w_proj_attn, w_proj_hgrn=v_w_proj_hgrn, w_out=v_w_out,
               ln3_g=v_ln3_g, ln3_b=v_ln3_b, ffn2_w1=v_ffn2_w1, ffn2_w3=v_ffn2_w3, ffn2_w2=v_ffn2_w2)

    n_tok = x.shape[1]
    x0 = x.reshape(n_tok, D_MODEL)
    target = loss_target.reshape(n_tok, D_MODEL)

    shard = {k: _bf(_local_view(k, w[k])) for k in _BIG}
    gather = lambda keys: _GatherWeights([shard[k] for k in keys])
    slots = {}
    exchange = lambda keys: _ExchangeGrads([big[k] for k in keys])
    ffn1_keys = ("ffn1_w1", "ffn1_w3", "ffn1_w2")
    mixer_keys = ("w_in", "w_proj_attn", "w_proj_hgrn", "w_out")
    ffn2_keys = ("ffn2_w1", "ffn2_w3", "ffn2_w2")
    whole = lambda ts: [t.reshape(D_FF, D_MODEL) for t in ts]
    f1 = whole(_run_comm("gather_ffn1", gather(ffn1_keys)))

    tabs = _rope_tables(n_tok)
    lb, lb_of = _lb_fwd(hgrn_lb_logits)
    (z1, x1, x0b, h1_1, h3_1, x1b), (w_in_g, w_pa, w_ph, w_o) = _ffn_fwd(
        "ffn1_fwd", x0, *f1, ln1_g, ln1_b, comm=gather(mixer_keys))
    w_pa, w_ph, w_o = (t.reshape(D_MODEL, D_MODEL) for t in (w_pa, w_ph, w_o))
    w_in_g = _reorder_w_in("w_in_cols", w_in_g, True)
    (proj,), f2 = _in_proj(x1b, w_in_g, _to_kernel_cols(b_in), comm=gather(ffn2_keys))
    f2 = whole(f2)
    y_attn = _attn_fwd(proj, tabs, attn_sinks)
    y_hgrn, states = _hgrn_fwd(proj, lb, hgrn_norm_g)
    z2, x2, proj_a, proj_h = _mix_fwd(y_attn, y_hgrn, proj, x1, w_pa, w_ph, w_o, ln2_g, ln2_b)
    (z3, dy, x2b, h1_2, h3_2, loss_part), _ = _ffn_fwd("ffn2_fwd", x2, *f2, ln3_g, ln3_b, target=target)

    big = {}
    small = {}
    (dx2, a2, dh1_2, dh3_2, df2, small["ln3_g"], small["ln3_b"]), _ = _ffn_bwd(
        "ffn2_bwd", h1_2, h3_2, z3, dy, *f2, ln3_g, ln3_b)
    big["ffn2_w1"], _ = _ffn_grad("ffn2_dw1", dh1_2, x2b)
    big["ffn2_w3"], _ = _ffn_grad("ffn2_dw3", dh3_2, x2b)
    big["ffn2_w2"], _ = _ffn_grad("ffn2_dw2", a2, df2)
    (dz2, dz2b, merged, dya, dyh, dy_attn, dy_hgrn, dproj, small["ln2_g"], small["ln2_b"]) = _mix_bwd(
        dx2, z2, proj_a, proj_h, proj, w_pa, w_ph, w_o, ln2_g, ln2_b)
    for key, name, lhs, rhs in (("w_out", "dw_out", merged, dz2b), ("w_proj_attn", "dw_proj_attn", y_attn, dya),
                                ("w_proj_hgrn", "dw_proj_hgrn", y_hgrn, dyh)):
        (dw,), _ = _grad_matmul(name, lhs, "cols", D_MODEL, rhs, "shared", D_MODEL, parts=1)
        big[key] = dw.reshape(N_CHIPS, PROJ_SHARD, D_MODEL)
    (dproj, dlb, small["hgrn_norm_g"]), early = _hgrn_bwd(
        proj, lb, hgrn_norm_g, states, dy_hgrn, dproj, comm=exchange(ffn2_keys[:2]))
    slots.update(zip(ffn2_keys[:2], early))
    attn_hosted = ffn2_keys[2:] + mixer_keys[1:]
    (dproj, dsinks), early = _attn_bwd(proj, dy_attn, tabs, attn_sinks, dproj, comm=exchange(attn_hosted))
    slots.update(zip(attn_hosted, early))
    (dw_in, db_in), _ = _grad_matmul("dw_in", x1b, "shared", D_MODEL, dproj, "cols", IN_SHARD, colsum=True)
    big["w_in"] = _reorder_w_in("dw_in_cols", dw_in, False)
    small["b_in"] = _from_kernel_cols(db_in)
    (dx1,), (w_in_slots,) = _in_proj_dx(
        dproj, w_in_g, dz2, comm=_ExchangeGrads([big["w_in"]], rows=(0, W_IN_ROWS_FIRST)))
    (dz1, a1, dh1_1, dh3_1, df1, small["ln1_g"], small["ln1_b"]), _ = _ffn_bwd(
        "ffn1_bwd", h1_1, h3_1, z1, dx1, *f1, ln1_g, ln1_b, with_dx=False)
    big["ffn1_w1"], (slots["w_in"],) = _ffn_grad("ffn1_dw1", dh1_1, x0b, comm=_ExchangeGrads(
        [big["w_in"]], rows=(W_IN_ROWS_FIRST, D_MODEL - W_IN_ROWS_FIRST), into=[w_in_slots]))
    big["ffn1_w3"], (slots["ffn1_w1"],) = _ffn_grad("ffn1_dw3", dh3_1, x0b, comm=exchange(("ffn1_w1",)))
    big["ffn1_w2"], (slots["ffn1_w3"],) = _ffn_grad("ffn1_dw2", a1, df1, comm=exchange(("ffn1_w3",)))

    last = "ffn1_w2"
    groups = [[k for k in ffn1_keys + ffn2_keys if k != last], list(mixer_keys[1:]), ["w_in"]]
    partial = {}
    for keys in groups:
        partial.update(zip(keys, _sum_slots("sum_" + keys[0], [slots[k] for k in keys])))
    swapped_keys = [k for keys in groups for k in keys]
    (grad_x,), moved = _ffn_dx("ffn1_dx", dz1, dh1_1, dh3_1, f1[0], f1[1], comm=_Together(
        _SwapWithSibling([partial[k] for k in swapped_keys]), exchange((last,))))
    from_sibling = dict(zip(swapped_keys, moved))
    (partial[last],) = _sum_slots("sum_" + last, [moved[-1]])
    (from_sibling[last],) = _run_comm("swap_last", _SwapWithSibling([partial[last]]))
    groups[0].append(last)

    outs = {"grad": {}, "delta": {}, "m": {}, "v": {}}
    for keys in groups:
        res = _adam_big("adam_" + keys[0], [(partial[k], from_sibling[k], _local_view(k, w[k]), _local_view(k, mom[k]),
                                             _local_view(k, var[k])) for k in keys])
        for k, four in zip(keys, res):
            for kind, r in zip(("grad", "delta", "m", "v"), four):
                outs[kind][k] = (r.T if k in _TRANSPOSED else r).reshape(w[k].shape)

    total = _sum_small(jnp.concatenate(
        [small[k] for k in ("ln1_g", "ln1_b", "ln2_g", "ln2_b", "ln3_g", "ln3_b", "b_in")]
        + [dsinks, small["hgrn_norm_g"], dlb, loss_part], axis=1))
    for kind, r in zip(("grad", "delta", "m", "v"), _adam_small(total, w, mom, var, lb_of)):
        outs[kind].update(r)
    loss = total[0, SM_LOSS]

    return (loss, grad_x.reshape(x.shape), *[outs["grad"][k] for k in _ORDER], *[outs["delta"][k] for k in _ORDER],
            *[outs["m"][k] for k in _ORDER], *[outs["v"][k] for k in _ORDER])
```

```python
import functools

import jax
import jax.numpy as jnp
from jax import lax
from jax.experimental import pallas as pl
from jax.experimental.pallas import tpu as pltpu

F32 = jnp.float32
BF16 = jnp.bfloat16

D_MODEL = 1024
N_Q_HEADS = 16
N_KV_HEADS = 4
HEAD_DIM = 64
ATTN_BLOCK = 128
ROPE_THETA = 500000.0
ROPE_DIM = HEAD_DIM // 4
HGRN_HEADS = 8
HGRN_DK = 128
HGRN_CHUNK = 64
D_FF = 2816
D_IN = 7680
DEEPNORM_ALPHA = 2 ** 0.25
LN_EPS = 1e-5
RMS_EPS = 1e-6
NEG_INF = -1e30

ADAM_LR = 0.001
ADAM_B1 = 0.9
ADAM_B2 = 0.999
ADAM_EPS = 1e-08
ADAM_WD = 0.01
ADAM_STEP = 10

N_CHIPS = 4
N_DEV = 8
LANES = 128
FF_GRAD_PARTS = 2
FFN_TILE = 256
IN_SHARD = D_IN // N_CHIPS
PROJ_SHARD = D_MODEL // N_CHIPS
ROW_TILE = 512
GRAD_ROWS = 2048
MIX_TILE = 256
UPDATE_ROWS = 128
HGRN_CHUNKS_PER_STEP = 32
W_IN_ROWS_FIRST = 832
VMEM_LIMIT = 56 * 1024 * 1024

GATES_WIDTH = 2 * D_MODEL
HGRN_HEAD_WIDTH = 4 * HGRN_DK
ATTN_WIDTH = D_MODEL + 2 * N_KV_HEADS * HEAD_DIM
COL_HGRN = GATES_WIDTH // HGRN_HEAD_WIDTH
COL_ATTN = (GATES_WIDTH + HGRN_HEADS * HGRN_HEAD_WIDTH) // ATTN_WIDTH
COL_Q = (GATES_WIDTH + HGRN_HEADS * HGRN_HEAD_WIDTH) // D_MODEL
COL_K = (GATES_WIDTH + HGRN_HEADS * HGRN_HEAD_WIDTH + D_MODEL) // (N_KV_HEADS * HEAD_DIM)
COL_V = COL_K + 1


def _to_kernel_cols(a):
    lead = a.shape[:-1]
    qkv, hg, gates = a[..., :ATTN_WIDTH], a[..., ATTN_WIDTH:D_IN - GATES_WIDTH], a[..., D_IN - GATES_WIDTH:]
    hg = jnp.swapaxes(hg.reshape(*lead, 4, HGRN_HEADS, HGRN_DK), -3, -2).reshape(*lead, -1)
    return jnp.concatenate([gates, hg, qkv], axis=-1)


def _from_kernel_cols(a):
    lead = a.shape[:-1]
    gates, hg, qkv = a[..., :GATES_WIDTH], a[..., GATES_WIDTH:D_IN - ATTN_WIDTH], a[..., D_IN - ATTN_WIDTH:]
    hg = jnp.swapaxes(hg.reshape(*lead, HGRN_HEADS, 4, HGRN_DK), -3, -2).reshape(*lead, -1)
    return jnp.concatenate([qkv, hg, gates], axis=-1)

SM_LN = 0
SM_BIN = 6 * D_MODEL
SM_SINK = SM_BIN + D_IN
SM_NG = SM_SINK + LANES
SM_LB = SM_NG + LANES
SM_LOSS = SM_LB + D_MODEL
SM_LEN = SM_LOSS + LANES

MESH = pl.DeviceIdType.MESH


def _mm(a, b):
    return lax.dot_general(a, b, (((1,), (0,)), ((), ())), preferred_element_type=F32)


def _mm_nt(a, b):
    return lax.dot_general(a, b, (((1,), (1,)), ((), ())), preferred_element_type=F32)


def _mm_tn(a, b):
    return lax.dot_general(a, b, (((0,), (0,)), ((), ())), preferred_element_type=F32)


def _bf(v):
    return v.astype(BF16)


def _sig(v):
    return jax.nn.sigmoid(v)


def _ln(z, g, b):
    mu = jnp.mean(z, axis=-1, keepdims=True)
    zc = z - mu
    var = jnp.mean(zc * zc, axis=-1, keepdims=True)
    return zc * lax.rsqrt(var + LN_EPS) * g + b


def _swiglu_act(h1, h3):
    return (h1 * _sig(h1)) * h3


def _params(sem=None):
    return pltpu.CompilerParams(dimension_semantics=sem, vmem_limit_bytes=VMEM_LIMIT)


def _full(shape):
    nd = len(shape)
    return pl.BlockSpec(shape, lambda *_: (0,) * nd)


def _update_rows(rows):
    return max(t for t in range(8, UPDATE_ROWS + 1, 8) if rows % t == 0)


def _resident(shape):
    nd = len(shape)
    return pl.BlockSpec(shape, lambda *_: (0,) * nd, pipeline_mode=pl.Buffered(1))


def _ffn_fwd(name, x, w1t, w3t, w2, g, b, target=None, comm=None):
    n = x.shape[0]
    tm = min(ROW_TILE, n)
    final = target is not None

    def body(*refs):
        if final:
            x_ref, w1_ref, w3_ref, w2_ref, g_ref, b_ref, t_ref, z_ref, o_ref, xb_ref, h1_ref, h3_ref, loss_ref = refs
        else:
            x_ref, w1_ref, w3_ref, w2_ref, g_ref, b_ref, z_ref, o_ref, xb_ref, h1_ref, h3_ref, ob_ref = refs
        xb = _bf(x_ref[...])
        xb_ref[...] = xb
        h1 = _mm_nt(xb, w1_ref[...])
        h3 = _mm_nt(xb, w3_ref[...])
        h1_ref[...] = _bf(h1)
        h3_ref[...] = _bf(h3)
        z = DEEPNORM_ALPHA * x_ref[...] + 0.5 * _mm(_bf(_swiglu_act(h1, h3)), w2_ref[...])
        z_ref[...] = z
        y = _ln(z, g_ref[...], b_ref[...])
        if final:
            e = y - t_ref[...]

            @pl.when(pl.program_id(0) == 0)
            def _():
                loss_ref[...] = jnp.zeros_like(loss_ref)

            loss_ref[...] += jnp.sum(e * e) * (0.5 / D_MODEL)
            o_ref[...] = e * (1.0 / D_MODEL)
        else:
            o_ref[...] = y
            ob_ref[...] = _bf(y)

    row = pl.BlockSpec((tm, D_MODEL), lambda i: (i, 0))
    hid = pl.BlockSpec((tm, D_FF), lambda i: (i, 0))
    wres = _resident((D_FF, D_MODEL))
    vec = _full((1, D_MODEL))
    in_specs = [row, wres, wres, wres, vec, vec]
    args = [x, w1t, w3t, w2, g, b]
    hid_shape = jax.ShapeDtypeStruct((n, D_FF), BF16)
    out_specs = [row, row, row, hid, hid]
    out_shape = ([jax.ShapeDtypeStruct((n, D_MODEL), F32)] * 2 + [jax.ShapeDtypeStruct((n, D_MODEL), BF16)]
                 + [hid_shape] * 2)
    if final:
        in_specs.append(row)
        args.append(target)
        out_specs.append(_full((1, LANES)))
        out_shape.append(jax.ShapeDtypeStruct((1, LANES), F32))
    else:
        out_specs.append(row)
        out_shape.append(jax.ShapeDtypeStruct((n, D_MODEL), BF16))
    return _hosted(
        body, comm, name=name, grid=(n // tm,), in_specs=in_specs, out_specs=out_specs, out_shape=out_shape,
        scratch_shapes=[], compiler_params=_params(("arbitrary",)), args=args)


def _ffn_bwd(name, h1s, h3s, z, dout, w1t, w3t, w2, g, b, comm=None):
    n = z.shape[0]
    tm = min(FFN_TILE, n)

    def body(h1_ref, h3_ref, z_ref, do_ref, w1_ref, w3_ref, w2_ref, g_ref, b_ref,
             dx_ref, a_ref, dh1_ref, dh3_ref, df_ref, dg_ref, db_ref):
        _, vjp = jax.vjp(_ln, z_ref[...], g_ref[...], b_ref[...])
        dz, dg, db = vjp(do_ref[...])

        @pl.when(pl.program_id(0) == 0)
        def _():
            dg_ref[...] = jnp.zeros_like(dg_ref)
            db_ref[...] = jnp.zeros_like(db_ref)

        dg_ref[...] += dg
        db_ref[...] += db
        df = _bf(0.5 * dz)
        df_ref[...] = df
        a, act_vjp = jax.vjp(_swiglu_act, h1_ref[...].astype(F32), h3_ref[...].astype(F32))
        dh1, dh3 = act_vjp(_mm_nt(df, w2_ref[...]))
        dh1 = _bf(dh1)
        dh3 = _bf(dh3)
        a_ref[...] = _bf(a)
        dh1_ref[...] = dh1
        dh3_ref[...] = dh3
        dx_ref[...] = DEEPNORM_ALPHA * dz + _mm(dh1, w1_ref[...]) + _mm(dh3, w3_ref[...])

    row = pl.BlockSpec((tm, D_MODEL), lambda i: (i, 0))
    hid = pl.BlockSpec((tm, D_FF), lambda i: (i, 0))
    wres = _resident((D_FF, D_MODEL))
    vec = _full((1, D_MODEL))
    hid_shape = jax.ShapeDtypeStruct((n, D_FF), BF16)
    return _hosted(
        body, comm, name=name, grid=(n // tm,),
        in_specs=[hid, hid, row, row, wres, wres, wres, vec, vec],
        out_specs=[row, hid, hid, hid, row, vec, vec],
        out_shape=[jax.ShapeDtypeStruct((n, D_MODEL), F32), hid_shape, hid_shape, hid_shape,
                   jax.ShapeDtypeStruct((n, D_MODEL), BF16),
                   jax.ShapeDtypeStruct((1, D_MODEL), F32), jax.ShapeDtypeStruct((1, D_MODEL), F32)],
        scratch_shapes=[], compiler_params=_params(("arbitrary",)),
        args=[h1s, h3s, z, dout, w1t, w3t, w2, g, b])


def _operand_spec(arr, mode, tn, width, parts):
    if mode == "shared":
        return pl.BlockSpec((tn, width), lambda s, k: (k, 0))
    assert mode == "cols" and arr.shape[1] == parts * width
    return pl.BlockSpec((tn, width), lambda s, k: (k, s))


def _grad_matmul(name, a, a_mode, ka, b, b_mode, kb, colsum=False, comm=None, parts=N_CHIPS):
    n = a.shape[-2]
    tn = min(GRAD_ROWS, n)
    nk = n // tn

    def body(*refs):
        if colsum:
            a_ref, b_ref, o_ref, cs_ref, acc = refs
        else:
            a_ref, b_ref, o_ref, acc = refs
        k = pl.program_id(1)
        av = a_ref[...]
        bv = b_ref[...]

        @pl.when(k == 0)
        def _():
            acc[...] = jnp.zeros_like(acc)
            if colsum:
                cs_ref[...] = jnp.zeros_like(cs_ref)

        acc[...] += _mm_tn(av, bv)
        if colsum:
            cs_ref[...] += jnp.sum(bv.astype(F32), axis=0, keepdims=True)

        @pl.when(k == nk - 1)
        def _():
            o_ref[0] = _bf(acc[...])

    out_specs = [pl.BlockSpec((1, ka, kb), lambda s, k: (s, 0, 0))]
    out_shape = [jax.ShapeDtypeStruct((parts, ka, kb), BF16)]
    if colsum:
        out_specs.append(pl.BlockSpec((1, kb), lambda s, k: (0, s)))
        out_shape.append(jax.ShapeDtypeStruct((1, parts * kb), F32))
    return _hosted(
        body, comm, name=name, grid=(parts, nk),
        in_specs=[_operand_spec(a, a_mode, tn, ka, parts), _operand_spec(b, b_mode, tn, kb, parts)],
        out_specs=out_specs, out_shape=out_shape,
        scratch_shapes=[pltpu.VMEM((ka, kb), F32)],
        compiler_params=_params(("arbitrary", "arbitrary")), args=[a, b])


def _kernel_block_of(ref_block):
    attn_blocks, gate_blocks = ATTN_WIDTH // LANES, GATES_WIDTH // LANES
    hgrn_blocks = HGRN_HEADS * HGRN_HEAD_WIDTH // LANES
    if ref_block < attn_blocks:
        return gate_blocks + hgrn_blocks + ref_block
    if ref_block < attn_blocks + hgrn_blocks:
        kind, head = divmod(ref_block - attn_blocks, HGRN_HEADS)
        return gate_blocks + head * (HGRN_HEAD_WIDTH // LANES) + kind
    return ref_block - attn_blocks - hgrn_blocks


def _reorder_w_in(name, w4, to_kernel_order):
    per = IN_SHARD // LANES
    tr = MIX_TILE

    def body(i_ref, o_ref):
        for g in range(D_IN // LANES):
            s, b = divmod(g, per)
            k = _kernel_block_of(g)
            if to_kernel_order:
                o_ref[:, k * LANES:(k + 1) * LANES] = i_ref[s, :, b * LANES:(b + 1) * LANES]
            else:
                ks, kb = divmod(k, per)
                o_ref[s, :, b * LANES:(b + 1) * LANES] = i_ref[ks, :, kb * LANES:(kb + 1) * LANES]

    in_spec = pl.BlockSpec((N_CHIPS, tr, IN_SHARD), lambda i: (0, i, 0))
    if to_kernel_order:
        out_spec, out_shape = pl.BlockSpec((tr, D_IN), lambda i: (i, 0)), (D_MODEL, D_IN)
    else:
        out_spec, out_shape = in_spec, (N_CHIPS, D_MODEL, IN_SHARD)
    return pl.pallas_call(
        body, name=name, grid=(D_MODEL // tr,), in_specs=[in_spec], out_specs=out_spec,
        out_shape=jax.ShapeDtypeStruct(out_shape, w4.dtype), compiler_params=_params(("arbitrary",)),
    )(w4)


def _in_proj(x1, w_in_g, b_in, comm=None):
    n = x1.shape[0]
    tm = min(ROW_TILE, n)

    def body(x_ref, w_ref, b_ref, o_ref):
        xv = x_ref[...]
        for j in range(N_CHIPS):
            cols = slice(j * IN_SHARD, (j + 1) * IN_SHARD)
            o_ref[:, cols] = _mm(xv, w_ref[:, cols]) + b_ref[:, cols]

    return _hosted(
        body, comm, name="in_proj", grid=(n // tm,),
        in_specs=[pl.BlockSpec((tm, D_MODEL), lambda i: (i, 0)),
                  _resident((D_MODEL, D_IN)), _full((1, D_IN))],
        out_specs=[pl.BlockSpec((tm, D_IN), lambda i: (i, 0))],
        out_shape=[jax.ShapeDtypeStruct((n, D_IN), F32)],
        scratch_shapes=[],
        compiler_params=_params(("arbitrary",)), args=[x1, w_in_g, b_in])


def _in_proj_dx(dproj, w_in_g, dz2, comm=None):
    n = dproj.shape[0]
    tm = min(ROW_TILE, n)

    def body(dp_ref, w_ref, dz_ref, o_ref):
        dx = DEEPNORM_ALPHA * dz_ref[...]
        for j in range(N_CHIPS):
            cols = slice(j * IN_SHARD, (j + 1) * IN_SHARD)
            dx = dx + _mm_nt(dp_ref[:, cols], w_ref[:, cols])
        o_ref[...] = dx

    return _hosted(
        body, comm, name="in_proj_dx", grid=(n // tm,),
        in_specs=[pl.BlockSpec((tm, D_IN), lambda i: (i, 0)),
                  _resident((D_MODEL, D_IN)),
                  pl.BlockSpec((tm, D_MODEL), lambda i: (i, 0))],
        out_specs=[pl.BlockSpec((tm, D_MODEL), lambda i: (i, 0))],
        out_shape=[jax.ShapeDtypeStruct((n, D_MODEL), F32)],
        scratch_shapes=[],
        compiler_params=_params(("arbitrary",)), args=[dproj, w_in_g, dz2])


def _rope_tables(seq_len):
    pos = jnp.arange(seq_len, dtype=F32)
    inv_freq = ROPE_THETA ** (-jnp.arange(0, ROPE_DIM, 2, dtype=F32) / ROPE_DIM)
    ang = pos[:, None] * inv_freq[None, :]
    cos, sin = jnp.cos(ang), jnp.sin(ang)
    half = ROPE_DIM // 2
    rest = HEAD_DIM - ROPE_DIM
    ones = jnp.ones((seq_len, rest), F32)
    zeros = jnp.zeros((seq_len, rest), F32)
    zh = jnp.zeros((seq_len, half), F32)
    c = jnp.concatenate([cos, cos, ones], axis=1)
    sa = jnp.concatenate([-sin, zh, zeros], axis=1)
    sb = jnp.concatenate([zh, sin, zeros], axis=1)
    reps = LANES // HEAD_DIM
    return tuple(jnp.tile(t, (1, reps)) for t in (c, sa, sb))


def _rope(t, c, sa, sb):
    w = t.shape[1]
    reps = w // LANES
    half = ROPE_DIM // 2
    return (t * jnp.tile(c, (1, reps)) + pltpu.roll(t, w - half, 1) * jnp.tile(sa, (1, reps))
            + pltpu.roll(t, half, 1) * jnp.tile(sb, (1, reps)))


def _rope_transposed(g, c, sa, sb):
    w = g.shape[1]
    reps = w // LANES
    half = ROPE_DIM // 2
    return (g * jnp.tile(c, (1, reps)) + pltpu.roll(g * jnp.tile(sa, (1, reps)), half, 1)
            + pltpu.roll(g * jnp.tile(sb, (1, reps)), w - half, 1))


GROUP = N_Q_HEADS // N_KV_HEADS


def _both_halves(t_pair, which):
    lo = lax.broadcasted_iota(jnp.int32, t_pair.shape, 1) < HEAD_DIM
    swapped = pltpu.roll(t_pair, HEAD_DIM, 1)
    return _bf(jnp.where(lo, t_pair, swapped) if which == 0 else jnp.where(lo, swapped, t_pair))


def _stack_heads(ref_or_val, kh):
    lo = lax.broadcasted_iota(jnp.int32, (ATTN_BLOCK, LANES), 1) < HEAD_DIM
    rows = []
    for gp in range(GROUP // 2):
        pair = kh * (GROUP // 2) + gp
        t = ref_or_val[:, pair * LANES:(pair + 1) * LANES]
        rows += [jnp.where(lo, t, jnp.zeros_like(t)), jnp.where(lo, jnp.zeros_like(t), t)]
    return jnp.concatenate(rows, axis=0)


def _unstack_pairs(stacked):
    lo = lax.broadcasted_iota(jnp.int32, (ATTN_BLOCK, LANES), 1) < HEAD_DIM
    b = ATTN_BLOCK
    return [jnp.where(lo, stacked[2 * gp * b:(2 * gp + 1) * b], stacked[(2 * gp + 1) * b:(2 * gp + 2) * b])
            for gp in range(GROUP // 2)]


def _attn_mask_t(n):
    cols = GROUP * ATTN_BLOCK
    kj = lax.broadcasted_iota(jnp.int32, (2 * ATTN_BLOCK, cols), 0)
    qi = lax.broadcasted_iota(jnp.int32, (2 * ATTN_BLOCK, cols), 1) % ATTN_BLOCK
    dist = qi + ATTN_BLOCK - kj
    return (dist >= 0) & (dist < ATTN_BLOCK) & (n * ATTN_BLOCK + kj - ATTN_BLOCK >= 0)


def _sink_row(sink_ref, kh):
    col = lax.broadcasted_iota(jnp.int32, (1, GROUP * ATTN_BLOCK), 1)
    row = jnp.full((1, GROUP * ATTN_BLOCK), sink_ref[0, kh * GROUP + GROUP - 1], F32)
    for i in reversed(range(GROUP - 1)):
        row = jnp.where(col < (i + 1) * ATTN_BLOCK, sink_ref[0, kh * GROUP + i], row)
    return row


def _attn_probs_t(q_masked, k_sel, mask_t, sink):
    s = _mm_nt(k_sel, q_masked) * (HEAD_DIM ** -0.5)
    s = jnp.where(mask_t, s, NEG_INF)
    m = jnp.maximum(jnp.max(s, axis=0, keepdims=True), sink)
    p = jnp.exp(s - m)
    e_sink = jnp.exp(sink - m)
    denom = jnp.sum(p, axis=0, keepdims=True) + e_sink
    return p / denom, e_sink / denom


def _attn_fwd(proj, tabs, sinks, comm=None):
    n_tok = proj.shape[0]
    nb = n_tok // ATTN_BLOCK

    def body(q_ref, k_ref, v_ref, c_ref, sa_ref, sb_ref, sink_ref, y_ref, kprev, vprev):
        n = pl.program_id(0)

        @pl.when(n == 0)
        def _():
            kprev[...] = jnp.zeros_like(kprev)
            vprev[...] = jnp.zeros_like(vprev)

        c, sa, sb = c_ref[...], sa_ref[...], sb_ref[...]
        qr = _bf(_rope(q_ref[...], c, sa, sb))
        kr = _rope(k_ref[...], c, sa, sb)
        vc = v_ref[...]
        kk = jnp.concatenate([kprev[...], kr], axis=0)
        vv = jnp.concatenate([vprev[...], vc], axis=0)
        kprev[...] = kr
        vprev[...] = vc
        mask = _attn_mask_t(n)
        for kh in range(N_KV_HEADS):
            r, which = divmod(kh, 2)
            kb = _both_halves(kk[:, r * LANES:(r + 1) * LANES], which)
            vb = _both_halves(vv[:, r * LANES:(r + 1) * LANES], which)
            probs, _ = _attn_probs_t(_stack_heads(qr, kh), kb, mask, _sink_row(sink_ref, kh))
            for gp, out in enumerate(_unstack_pairs(_mm_tn(_bf(probs), vb))):
                pair = kh * (GROUP // 2) + gp
                y_ref[:, pair * LANES:(pair + 1) * LANES] = _bf(out)

    blk = lambda width, col: pl.BlockSpec((ATTN_BLOCK, width), lambda n: (n, col))
    tab = pl.BlockSpec((ATTN_BLOCK, LANES), lambda n: (n, 0))
    kvw = N_KV_HEADS * HEAD_DIM
    return _hosted(
        body, comm, name="attn_fwd", grid=(nb,),
        in_specs=[blk(D_MODEL, COL_Q), blk(kvw, COL_K), blk(kvw, COL_V), tab, tab, tab,
                  pl.BlockSpec(memory_space=pltpu.SMEM)],
        out_specs=[pl.BlockSpec((ATTN_BLOCK, D_MODEL), lambda n: (n, 0))],
        out_shape=[jax.ShapeDtypeStruct((n_tok, D_MODEL), BF16)],
        scratch_shapes=[pltpu.VMEM((ATTN_BLOCK, kvw), F32), pltpu.VMEM((ATTN_BLOCK, kvw), F32)],
        compiler_params=_params(("arbitrary",)), args=[proj, proj, proj, *tabs, sinks])


def _attn_bwd(proj, dy, tabs, sinks, dproj, comm=None):
    n_tok = proj.shape[0]
    nb = n_tok // ATTN_BLOCK
    kvw = N_KV_HEADS * HEAD_DIM

    def body(q_ref, k_ref, v_ref, do_ref, c_ref, sa_ref, sb_ref, cp_ref, sap_ref, sbp_ref, sink_ref, _,
             dqkv_ref, dsink_ref, kprev, vprev, dkc, dvc, dqc):
        n = pl.program_id(0)

        @pl.when(n == 0)
        def _():
            for ref in (kprev, vprev, dkc, dvc, dqc, dsink_ref):
                ref[...] = jnp.zeros_like(ref)

        prev_tabs = (cp_ref[...], sap_ref[...], sbp_ref[...])

        @pl.when(n < nb)
        def _():
            c, sa, sb = c_ref[...], sa_ref[...], sb_ref[...]
            qr = _bf(_rope(q_ref[...], c, sa, sb))
            kr = _rope(k_ref[...], c, sa, sb)
            vc = v_ref[...]
            kk = jnp.concatenate([kprev[...], kr], axis=0)
            vv = jnp.concatenate([vprev[...], vc], axis=0)
            kprev[...] = kr
            vprev[...] = vc
            mask = _attn_mask_t(n)
            lane = lax.broadcasted_iota(jnp.int32, (1, LANES), 1)
            lo2 = lax.broadcasted_iota(jnp.int32, (2 * ATTN_BLOCK, LANES), 1) < HEAD_DIM
            dsink = jnp.zeros((1, LANES), F32)
            dq_pairs = []
            dk_full = []
            dv_full = []
            for kh in range(N_KV_HEADS):
                r, which = divmod(kh, 2)
                kb = _both_halves(kk[:, r * LANES:(r + 1) * LANES], which)
                vb = _both_halves(vv[:, r * LANES:(r + 1) * LANES], which)
                qs = _stack_heads(qr, kh)
                dos = _stack_heads(do_ref, kh)
                probs, p_sink = _attn_probs_t(qs, kb, mask, _sink_row(sink_ref, kh))
                dp = _mm_nt(vb, dos)
                delta = jnp.sum(probs * dp, axis=0, keepdims=True)
                ds = _bf(probs * (dp - delta) * (HEAD_DIM ** -0.5))
                sink_terms = p_sink * delta
                for i in range(GROUP):
                    head_sum = jnp.sum(sink_terms[:, i * ATTN_BLOCK:(i + 1) * ATTN_BLOCK])
                    dsink = dsink + jnp.where(lane == kh * GROUP + i, -head_sum, 0.0)
                dq_pairs += _unstack_pairs(_mm_tn(ds, kb))
                dk_acc = _mm(ds, qs)
                dv_acc = _mm(_bf(probs), dos)
                dk_full.append(dk_acc + pltpu.roll(dk_acc, HEAD_DIM, 1))
                dv_full.append(dv_acc + pltpu.roll(dv_acc, HEAD_DIM, 1))
            dk_pairs = [jnp.where(lo2, dk_full[2 * r], dk_full[2 * r + 1]) for r in range(N_KV_HEADS // 2)]
            dv_pairs = [jnp.where(lo2, dv_full[2 * r], dv_full[2 * r + 1]) for r in range(N_KV_HEADS // 2)]
            dsink_ref[...] += dsink
            dqkv_ref[:, :D_MODEL] = _bf(dqc[...])
            dqc[...] = _rope_transposed(jnp.concatenate(dq_pairs, axis=1), c, sa, sb)
            dk_all = jnp.concatenate(dk_pairs, axis=1)
            dv_all = jnp.concatenate(dv_pairs, axis=1)
            dqkv_ref[:, D_MODEL:D_MODEL + kvw] = _bf(_rope_transposed(dkc[...] + dk_all[:ATTN_BLOCK], *prev_tabs))
            dqkv_ref[:, D_MODEL + kvw:] = _bf(dvc[...] + dv_all[:ATTN_BLOCK])
            dkc[...] = dk_all[ATTN_BLOCK:]
            dvc[...] = dv_all[ATTN_BLOCK:]

        @pl.when(n == nb)
        def _():
            dqkv_ref[:, :D_MODEL] = _bf(dqc[...])
            dqkv_ref[:, D_MODEL:D_MODEL + kvw] = _bf(_rope_transposed(dkc[...], *prev_tabs))
            dqkv_ref[:, D_MODEL + kvw:] = _bf(dvc[...])

    cur = lambda n: jnp.minimum(n, nb - 1)
    prev = lambda n: jnp.maximum(n - 1, 0)
    blk = lambda width, col: pl.BlockSpec((ATTN_BLOCK, width), lambda n: (cur(n), col))
    tab = pl.BlockSpec((ATTN_BLOCK, LANES), lambda n: (cur(n), 0))
    tabp = pl.BlockSpec((ATTN_BLOCK, LANES), lambda n: (prev(n), 0))
    return _hosted(
        body, comm, name="attn_bwd", grid=(nb + 1,),
        in_specs=[blk(D_MODEL, COL_Q), blk(kvw, COL_K), blk(kvw, COL_V), blk(D_MODEL, 0), tab, tab, tab, tabp, tabp, tabp,
                  pl.BlockSpec(memory_space=pltpu.SMEM), pl.BlockSpec(memory_space=pl.ANY)],
        out_specs=[pl.BlockSpec((ATTN_BLOCK, ATTN_WIDTH), lambda n: (prev(n), COL_ATTN)),
                   pl.BlockSpec((1, LANES), lambda n: (0, 0))],
        out_shape=[jax.ShapeDtypeStruct(dproj.shape, dproj.dtype), jax.ShapeDtypeStruct((1, LANES), F32)],
        scratch_shapes=[pltpu.VMEM((ATTN_BLOCK, kvw), F32)] * 4 + [pltpu.VMEM((ATTN_BLOCK, D_MODEL), F32)],
        compiler_params=_params(("arbitrary",)), args=[proj, proj, proj, dy, *tabs, *tabs, sinks, dproj],
        aliases={11: 0})


def _bmm(a, b):
    return lax.dot_general(a, b, (((2,), (1,)), ((0,), (0,))), preferred_element_type=F32)


def _bmm_nt(a, b):
    return lax.dot_general(a, b, (((2,), (2,)), ((0,), (0,))), preferred_element_type=F32)


def _bmm_tn(a, b):
    return lax.dot_general(a, b, (((1,), (1,)), ((0,), (0,))), preferred_element_type=F32)


def _tril(cb, upper=False):
    shape = (cb, HGRN_CHUNK, HGRN_CHUNK)
    r, c = lax.broadcasted_iota(jnp.int32, shape, 1), lax.broadcasted_iota(jnp.int32, shape, 2)
    return (r <= c) if upper else (r >= c)


def _tri_matmul(x, upper):
    return lax.dot_general(_tril(x.shape[0], upper).astype(F32), x, (((2,), (1,)), ((0,), (0,))),
                           precision=lax.Precision.HIGHEST, preferred_element_type=F32)


@jax.custom_vjp
def _chunk_cumsum(x):
    return _tri_matmul(x, False)


_chunk_cumsum.defvjp(lambda x: (_tri_matmul(x, False), None), lambda _, g: (_tri_matmul(g, True),))


def _hg_elem(fl, qh, lb):
    f = lb + (1.0 - lb) * _sig(fl)
    k = 1.0 - f
    gc = _chunk_cumsum(jnp.log(f))
    last = lax.broadcasted_iota(jnp.int32, gc.shape, 1) == HGRN_CHUNK - 1
    g_last = jnp.sum(jnp.where(last, gc, 0.0), axis=1, keepdims=True)
    q = qh * _sig(qh)
    return q * jnp.exp(gc), k * jnp.exp(-gc), k * jnp.exp(g_last - gc), jnp.exp(g_last)


def _hg_out(q_dec, k_inv, v, st):
    sc = jnp.where(_tril(q_dec.shape[0]), _bmm_nt(_bf(q_dec), _bf(k_inv)), 0.0)
    return _bmm(_bf(sc), _bf(v)) + _bmm_nt(_bf(q_dec), _bf(st)), sc


def _hg_post(o, og, ng):
    on = o * lax.rsqrt(jnp.mean(o * o, axis=-1, keepdims=True) + RMS_EPS) * ng
    return on * (og * _sig(og))


def _hgrn_specs(n_tok, rev):
    nc = n_tok // HGRN_CHUNK
    cb = min(HGRN_CHUNKS_PER_STEP, nc)
    nt = nc // cb
    rows = cb * HGRN_CHUNK
    tt = (lambda t: nt - 1 - t) if rev else (lambda t: t)
    col = lambda base: pl.BlockSpec((rows, LANES), lambda h, t: (tt(t), base + h))
    head_cols = pl.BlockSpec((rows, HGRN_HEAD_WIDTH), lambda h, t: (tt(t), COL_HGRN + h))
    head_vec = pl.BlockSpec((1, LANES), lambda h, t: (0, h))
    one_vec = pl.BlockSpec((1, LANES), lambda h, t: (0, 0))
    state = pl.BlockSpec((1, cb, HGRN_DK, HGRN_DK), lambda h, t: (h, tt(t), 0, 0))
    return nc, cb, nt, col, head_cols, head_vec, one_vec, state


def _hgrn_fwd(proj, lb, ng):
    n_tok = proj.shape[0]
    nc, cb, nt, col, head_cols, head_vec, one_vec, state = _hgrn_specs(n_tok, False)

    def body(in_ref, lb_ref, ng_ref, y_ref, st_ref, s_acc):
        @pl.when(pl.program_id(1) == 0)
        def _():
            s_acc[...] = jnp.zeros_like(s_acc)

        fl, qh, v, og = (in_ref[:, i * LANES:(i + 1) * LANES].reshape(cb, HGRN_CHUNK, LANES) for i in range(4))
        q_dec, k_inv, k_end, decay = _hg_elem(fl, qh, lb_ref[...])
        upd = _bmm_tn(_bf(v), _bf(k_end))
        st = s_acc[...]
        for ci in range(cb):
            st_ref[0, ci] = st
            st = st * decay[ci] + upd[ci]
        s_acc[...] = st
        o, _ = _hg_out(q_dec, k_inv, v, st_ref[0])
        y_ref[...] = _bf(_hg_post(o, og, ng_ref[...]).reshape(cb * HGRN_CHUNK, LANES))

    return pl.pallas_call(
        body, name="hgrn_fwd", grid=(HGRN_HEADS, nt),
        in_specs=[head_cols, head_vec, one_vec],
        out_specs=[col(0), state],
        out_shape=[jax.ShapeDtypeStruct((n_tok, D_MODEL), BF16),
                   jax.ShapeDtypeStruct((HGRN_HEADS, nc, HGRN_DK, HGRN_DK), F32)],
        scratch_shapes=[pltpu.VMEM((HGRN_DK, HGRN_DK), F32)],
        compiler_params=_params(("arbitrary", "arbitrary")),
    )(proj, lb, ng)


def _hgrn_bwd(proj, lb, ng, states, dy, dproj, comm=None):
    n_tok = proj.shape[0]
    nc, cb, nt, col, head_cols, head_vec, one_vec, state = _hgrn_specs(n_tok, True)

    def body(in_ref, lb_ref, ng_ref, st_ref, dy_ref, _, d_ref, dlb_ref, dng_ref, g_acc, g_all):
        h = pl.program_id(0)
        t = pl.program_id(1)

        @pl.when(t == 0)
        def _():
            g_acc[...] = jnp.zeros_like(g_acc)
            dlb_ref[...] = jnp.zeros_like(dlb_ref)

        @pl.when((t == 0) & (h == 0))
        def _():
            dng_ref[...] = jnp.zeros_like(dng_ref)

        fl, qh, v, og = (in_ref[:, i * LANES:(i + 1) * LANES].reshape(cb, HGRN_CHUNK, LANES) for i in range(4))
        (q_dec, k_inv, k_end, decay), elem_vjp = jax.vjp(_hg_elem, fl, qh, lb_ref[...])
        st = st_ref[0]
        o, sc = _hg_out(q_dec, k_inv, v, st)
        _, post_vjp = jax.vjp(_hg_post, o, og, ng_ref[...])
        do, dog, dng = post_vjp(dy_ref[...].reshape(cb, HGRN_CHUNK, LANES))
        dob, vb, qb = _bf(do), _bf(v), _bf(q_dec)
        dsc = _bf(jnp.where(_tril(cb), _bmm_nt(dob, vb), 0.0))
        p = _bmm_tn(dob, qb)
        g = g_acc[...]
        for ci in reversed(range(cb)):
            g_all[ci] = g
            g = g * decay[ci] + p[ci]
        g_acc[...] = g
        g = g_all[...]
        gb = _bf(g)
        dq_dec = _bmm(dsc, _bf(k_inv)) + _bmm(dob, _bf(st))
        dk_inv = _bmm_tn(dsc, qb)
        dv = _bmm_tn(_bf(sc), dob) + _bmm_nt(_bf(k_end), gb)
        dk_end = _bmm(vb, gb)
        ddecay = jnp.sum(st * g, axis=1, keepdims=True)
        dfl, dqh, dlb = elem_vjp((dq_dec, dk_inv, dk_end, ddecay))
        for i, val in enumerate((dfl, dqh, dv, dog)):
            d_ref[:, i * LANES:(i + 1) * LANES] = _bf(val.reshape(cb * HGRN_CHUNK, LANES))
        dlb_ref[...] += dlb
        dng_ref[...] += dng

    return _hosted(
        body, comm, name="hgrn_bwd", grid=(HGRN_HEADS, nt),
        in_specs=[head_cols, head_vec, one_vec, state, col(0), pl.BlockSpec(memory_space=pl.ANY)],
        out_specs=[head_cols, head_vec, one_vec],
        out_shape=[jax.ShapeDtypeStruct(dproj.shape, dproj.dtype),
                   jax.ShapeDtypeStruct((1, D_MODEL), F32), jax.ShapeDtypeStruct((1, LANES), F32)],
        scratch_shapes=[pltpu.VMEM((HGRN_DK, HGRN_DK), F32), pltpu.VMEM((cb, HGRN_DK, HGRN_DK), F32)],
        compiler_params=_params(("arbitrary", "arbitrary")), args=[proj, lb, ng, states, dy, dproj], aliases={5: 0})


def _lb_fwd(lb_logits):
    def lb_of(l0, l1):
        m = jnp.maximum(l0, l1)
        e0, e1 = jnp.exp(l0 - m), jnp.exp(l1 - m)
        return e0 / (e0 + e1)

    def body(l_ref, o_ref):
        o_ref[...] = lb_of(l_ref[0:1, :], l_ref[1:2, :])

    lb = pl.pallas_call(body, name="lb_fwd", out_shape=jax.ShapeDtypeStruct((1, D_MODEL), F32))(lb_logits)
    return lb, lb_of


def _mix_fwd(y_attn, y_hgrn, proj, x1, w_pa, w_ph, w_out, g, b):
    n = x1.shape[0]
    tm = min(ROW_TILE, n)

    def body(ya_ref, yh_ref, gt_ref, x_ref, wpa, wph, wo, g_ref, b_ref, z_ref, o_ref, pa_ref, ph_ref):
        ya = _mm(ya_ref[...], wpa[...])
        yh = _mm(yh_ref[...], wph[...])
        pa_ref[...] = _bf(ya)
        ph_ref[...] = _bf(yh)
        merged = _sig(gt_ref[:, :D_MODEL]) * ya + _sig(gt_ref[:, D_MODEL:]) * yh
        z = DEEPNORM_ALPHA * x_ref[...] + _mm(_bf(merged), wo[...])
        z_ref[...] = z
        o_ref[...] = _ln(z, g_ref[...], b_ref[...])

    row = pl.BlockSpec((tm, D_MODEL), lambda i: (i, 0))
    gates = pl.BlockSpec((tm, GATES_WIDTH), lambda i: (i, 0))
    sq = _resident((D_MODEL, D_MODEL))
    vec = _full((1, D_MODEL))
    return pl.pallas_call(
        body, name="mix_fwd", grid=(n // tm,),
        in_specs=[row, row, gates, row, sq, sq, sq, vec, vec],
        out_specs=[row, row, row, row],
        out_shape=[jax.ShapeDtypeStruct((n, D_MODEL), F32)] * 2 + [jax.ShapeDtypeStruct((n, D_MODEL), BF16)] * 2,
        compiler_params=_params(("arbitrary",)),
    )(y_attn, y_hgrn, proj, x1, w_pa, w_ph, w_out, g, b)


def _mix_bwd(dx2, z2, pa, ph, proj, w_pa, w_ph, w_out, g, b):
    n = z2.shape[0]
    tm = min(MIX_TILE, n)

    def body(do_ref, z_ref, ya_ref, yh_ref, gt_ref, wpa, wph, wo, g_ref, b_ref,
             dz_ref, dzb_ref, mg_ref, dya_ref, dyh_ref, dyat_ref, dyhg_ref, dgt_ref, dg_ref, db_ref):
        _, vjp = jax.vjp(_ln, z_ref[...], g_ref[...], b_ref[...])
        dz, dg, db = vjp(do_ref[...])

        @pl.when(pl.program_id(0) == 0)
        def _():
            dg_ref[...] = jnp.zeros_like(dg_ref)
            db_ref[...] = jnp.zeros_like(db_ref)

        dg_ref[...] += dg
        db_ref[...] += db
        dz_ref[...] = dz
        ya = ya_ref[...].astype(F32)
        yh = yh_ref[...].astype(F32)
        def merge(ga, gh, ya, yh):
            return _sig(ga) * ya + _sig(gh) * yh

        merged, merge_vjp = jax.vjp(merge, gt_ref[:, :D_MODEL], gt_ref[:, D_MODEL:], ya, yh)
        mg_ref[...] = _bf(merged)
        dzb = _bf(dz)
        dzb_ref[...] = dzb
        dmerged = _mm_nt(dzb, wo[...])
        dga, dgh, dya, dyh = merge_vjp(dmerged)
        dya = _bf(dya)
        dyh = _bf(dyh)
        dya_ref[...] = dya
        dyh_ref[...] = dyh
        dgt_ref[:, :D_MODEL] = _bf(dga)
        dgt_ref[:, D_MODEL:] = _bf(dgh)
        dyat_ref[...] = _bf(_mm_nt(dya, wpa[...]))
        dyhg_ref[...] = _mm_nt(dyh, wph[...])

    row = pl.BlockSpec((tm, D_MODEL), lambda i: (i, 0))
    gates = pl.BlockSpec((tm, GATES_WIDTH), lambda i: (i, 0))
    sq = _resident((D_MODEL, D_MODEL))
    vec = _full((1, D_MODEL))
    f32_row = jax.ShapeDtypeStruct((n, D_MODEL), F32)
    bf_row = jax.ShapeDtypeStruct((n, D_MODEL), BF16)
    vec_shape = jax.ShapeDtypeStruct((1, D_MODEL), F32)
    return pl.pallas_call(
        body, name="mix_bwd", grid=(n // tm,),
        in_specs=[row, row, row, row, gates, sq, sq, sq, vec, vec],
        out_specs=[row, row, row, row, row, row, row, gates, vec, vec],
        out_shape=[f32_row, bf_row, bf_row, bf_row, bf_row, bf_row, f32_row,
                   jax.ShapeDtypeStruct((n, D_IN), BF16), vec_shape, vec_shape],
        compiler_params=_params(("arbitrary",)),
    )(dx2, z2, pa, ph, proj, w_pa, w_ph, w_out, g, b)


def _position():
    x, y, c = lax.axis_index("x"), lax.axis_index("y"), lax.axis_index("c")
    chips = [(1 - x, y), (x, 1 - y), (1 - x, 1 - y)]
    return x, y, c, chips


def _any_specs(k):
    return [pl.BlockSpec(memory_space=pl.ANY)] * k


class _GatherWeights:
    def __init__(self, shards):
        nw = len(shards)
        self.inputs = list(shards)
        self.out_shape = [jax.ShapeDtypeStruct((N_CHIPS, *s.shape), s.dtype) for s in shards]
        self.scratch = [pltpu.SemaphoreType.DMA((nw,)), pltpu.SemaphoreType.DMA((nw * 6,)),
                        pltpu.SemaphoreType.DMA((nw * 6,))]

    def _copies(self, ins, outs, sems):
        nw = len(ins)
        local_sem, send_sem, recv_sem = sems
        x, y, c, chips = _position()
        me = 2 * x + y
        sibling = (x, y, 1 - c)
        half_rows = [s.shape[0] // 2 for s in self.inputs]

        def half(w, chip_idx, which):
            return outs[w].at[chip_idx, pl.ds(which * half_rows[w], half_rows[w])]

        def remote(w, k, src, dst, to):
            return pltpu.make_async_remote_copy(src_ref=src, dst_ref=dst, send_sem=send_sem.at[w * 6 + k],
                                                recv_sem=recv_sem.at[w * 6 + k], device_id=to, device_id_type=MESH)

        local = [pltpu.make_async_copy(ins[w], outs[w].at[me], local_sem.at[w]) for w in range(nw)]
        first = [remote(w, j, ins[w].at[pl.ds(c * half_rows[w], half_rows[w])], half(w, me, c), (px, py, c))
                 for w in range(nw) for j, (px, py) in enumerate(chips)]
        landed = [half(w, 2 * px + py, c) for w in range(nw) for (px, py) in chips]
        arrive = [remote(w, j, landed[w * 3 + j], landed[w * 3 + j], (px, py, c))
                  for w in range(nw) for j, (px, py) in enumerate(chips)]
        passed = [remote(w, 3 + j, landed[w * 3 + j], landed[w * 3 + j], sibling) for w in range(nw) for j in range(3)]
        from_sibling = [remote(w, 3 + j, half(w, 2 * px + py, 1 - c), half(w, 2 * px + py, 1 - c), sibling)
                        for w in range(nw) for j, (px, py) in enumerate(chips)]
        return local, first, arrive, passed, from_sibling

    def start(self, ins, outs, sems):
        local, first, _, _, _ = self._copies(ins, outs, sems)
        for cp in local + first:
            cp.start()

    def finish(self, ins, outs, sems):
        local, first, arrive, passed, from_sibling = self._copies(ins, outs, sems)
        for cp_in, cp_on in zip(arrive, passed):
            cp_in.wait_recv()
            cp_on.start()
        for cp in from_sibling:
            cp.wait_recv()
        for cp in first + passed:
            cp.wait_send()
        for cp in local:
            cp.wait()


class _ExchangeGrads:
    def __init__(self, grads, rows=None, into=None):
        nw = len(grads)
        self.rows = [rows or (0, g.shape[1]) for g in grads]
        self.inputs = list(grads) + list(into or [])
        self.aliases = {nw + i: i for i in range(nw)} if into else {}
        self.out_shape = [jax.ShapeDtypeStruct(g.shape, g.dtype) for g in grads]
        self.scratch = [pltpu.SemaphoreType.DMA((nw,)), pltpu.SemaphoreType.DMA((nw * 3,)),
                        pltpu.SemaphoreType.DMA((nw * 3,))]

    def _copies(self, ins, outs, sems):
        nw = len(outs)
        local_sem, send_sem, recv_sem = sems
        x, y, c, chips = _position()
        me = 2 * x + y
        rows = [pl.ds(*r) for r in self.rows]

        def remote(w, j, src, dst, chip):
            return pltpu.make_async_remote_copy(src_ref=src, dst_ref=dst, send_sem=send_sem.at[w * 3 + j],
                                                recv_sem=recv_sem.at[w * 3 + j], device_id=(*chip, c),
                                                device_id_type=MESH)

        local = [pltpu.make_async_copy(ins[w].at[me, rows[w]], outs[w].at[me, rows[w]], local_sem.at[w])
                 for w in range(nw)]
        sends = [remote(w, j, ins[w].at[2 * px + py, rows[w]], outs[w].at[me, rows[w]], (px, py))
                 for w in range(nw) for j, (px, py) in enumerate(chips)]
        arrive = [remote(w, j, outs[w].at[2 * px + py, rows[w]], outs[w].at[2 * px + py, rows[w]], (px, py))
                  for w in range(nw) for j, (px, py) in enumerate(chips)]
        return local, sends, arrive

    def start(self, ins, outs, sems):
        local, sends, _ = self._copies(ins, outs, sems)
        for cp in local + sends:
            cp.start()

    def finish(self, ins, outs, sems):
        local, sends, arrive = self._copies(ins, outs, sems)
        for cp in arrive:
            cp.wait_recv()
        for cp in sends:
            cp.wait_send()
        for cp in local:
            cp.wait()


def _hosted(body, comm, *, name, grid, in_specs, out_specs, out_shape, scratch_shapes, compiler_params, args,
            aliases=None):
    aliases = aliases or {}
    if comm is None:
        res = pl.pallas_call(body, name=name, grid=grid, in_specs=in_specs, out_specs=out_specs, out_shape=out_shape,
                             scratch_shapes=scratch_shapes, compiler_params=compiler_params,
                             input_output_aliases=aliases)(*args)
        return list(res), []
    n_in, n_out, n_scr = len(in_specs), len(out_specs), len(scratch_shapes)
    c_in, c_out = len(comm.inputs), len(comm.out_shape)
    aliases = {**aliases, **{n_in + i: n_out + o for i, o in getattr(comm, "aliases", {}).items()}}

    def hosted_body(*refs):
        refs = list(refs)
        cut = lambda k: (refs[:k], refs[k:])
        main_in, refs = cut(n_in)
        comm_in, refs = cut(c_in)
        main_out, refs = cut(n_out)
        comm_out, refs = cut(c_out)
        main_scr, comm_scr = cut(n_scr)
        ids = [pl.program_id(a) for a in range(len(grid))]
        first = functools.reduce(jnp.logical_and, [i == 0 for i in ids])
        last = functools.reduce(jnp.logical_and, [i == g - 1 for i, g in zip(ids, grid)])

        @pl.when(first)
        def _():
            comm.start(comm_in, comm_out, comm_scr)

        body(*main_in, *main_out, *main_scr)

        @pl.when(last)
        def _():
            comm.finish(comm_in, comm_out, comm_scr)

    res = pl.pallas_call(
        hosted_body, name=name, grid=grid, in_specs=[*in_specs, *_any_specs(c_in)],
        out_specs=[*out_specs, *_any_specs(c_out)], out_shape=[*out_shape, *comm.out_shape],
        scratch_shapes=[*scratch_shapes, *comm.scratch], compiler_params=compiler_params,
        input_output_aliases=aliases,
    )(*args, *comm.inputs)
    return list(res[:n_out]), list(res[n_out:])


def _run_comm(name, comm):
    def body(*refs):
        refs = list(refs)
        c_in, c_out = len(comm.inputs), len(comm.out_shape)
        ins, outs, sems = refs[:c_in], refs[c_in:c_in + c_out], refs[c_in + c_out:]
        comm.start(ins, outs, sems)
        comm.finish(ins, outs, sems)

    return list(pl.pallas_call(
        body, name=name, in_specs=_any_specs(len(comm.inputs)), out_specs=_any_specs(len(comm.out_shape)),
        out_shape=comm.out_shape, scratch_shapes=comm.scratch,
    )(*comm.inputs))


def _sum_slots(name, slots):
    k = len(slots)
    _, rows, cols = slots[0].shape
    tr = _update_rows(rows)

    def body(*refs):
        for s_ref, o_ref in zip(refs[:k], refs[k:]):
            acc = s_ref[0].astype(F32)
            for i in range(1, N_CHIPS):
                acc = acc + s_ref[i].astype(F32)
            o_ref[...] = acc

    return pl.pallas_call(
        body, name=name, grid=(rows // tr,),
        in_specs=[pl.BlockSpec((N_CHIPS, tr, cols), lambda i: (0, i, 0))] * k,
        out_specs=[pl.BlockSpec((tr, cols), lambda i: (i, 0))] * k,
        out_shape=[jax.ShapeDtypeStruct((rows, cols), F32)] * k,
        compiler_params=_params(("arbitrary",)),
    )(*slots)


class _SwapWithSibling:
    def __init__(self, parts):
        self.inputs = list(parts)
        self.out_shape = [jax.ShapeDtypeStruct(p.shape, p.dtype) for p in parts]
        self.scratch = [pltpu.SemaphoreType.DMA((len(parts),)), pltpu.SemaphoreType.DMA((len(parts),))]

    def _copies(self, ins, outs, sems):
        send_sem, recv_sem = sems
        x, y, c, _ = _position()
        return [pltpu.make_async_remote_copy(src_ref=ins[w], dst_ref=outs[w], send_sem=send_sem.at[w],
                                             recv_sem=recv_sem.at[w], device_id=(x, y, 1 - c), device_id_type=MESH)
                for w in range(len(ins))]

    def start(self, ins, outs, sems):
        for cp in self._copies(ins, outs, sems):
            cp.start()

    def finish(self, ins, outs, sems):
        for cp in self._copies(ins, outs, sems):
            cp.wait()


class _Together:
    def __init__(self, first, second):
        self.parts = (first, second)
        self.inputs = first.inputs + second.inputs
        self.out_shape = first.out_shape + second.out_shape
        self.scratch = first.scratch + second.scratch

    def _split(self, ins, outs, sems):
        a = self.parts[0]
        ni, no, ns = len(a.inputs), len(a.out_shape), len(a.scratch)
        return ((ins[:ni], outs[:no], sems[:ns]), (ins[ni:], outs[no:], sems[ns:]))

    def start(self, ins, outs, sems):
        for part, args in zip(self.parts, self._split(ins, outs, sems)):
            part.start(*args)

    def finish(self, ins, outs, sems):
        for part, args in zip(self.parts, self._split(ins, outs, sems)):
            part.finish(*args)


def _sum_small(part):
    def body(p_ref, o_ref, buf, send_sem, recv_sem):
        x, y, c, _ = _position()
        me = 4 * x + 2 * y + c
        buf[me] = p_ref[...]
        copies = []
        for k in range(1, N_DEV):
            peer = tuple(1 - v if (k >> s) & 1 else v for v, s in ((x, 2), (y, 1), (c, 0)))
            copies.append(pltpu.make_async_remote_copy(src_ref=p_ref, dst_ref=buf.at[me], send_sem=send_sem.at[k - 1],
                                                       recv_sem=recv_sem.at[k - 1], device_id=peer, device_id_type=MESH))
        for cp in copies:
            cp.start()
        for cp in copies:
            cp.wait()
        acc = buf[0]
        for d in range(1, N_DEV):
            acc = acc + buf[d]
        o_ref[...] = acc

    vm = pl.BlockSpec(memory_space=pltpu.VMEM)
    return pl.pallas_call(
        body, name="sum_small", in_specs=[vm], out_specs=vm,
        out_shape=jax.ShapeDtypeStruct((1, SM_LEN), F32),
        scratch_shapes=[pltpu.VMEM((N_DEV, 1, SM_LEN), F32), pltpu.SemaphoreType.DMA((N_DEV - 1,)),
                        pltpu.SemaphoreType.DMA((N_DEV - 1,))],
    )(part)


def _adamw(w, g, m, v):
    m = ADAM_B1 * m + (1.0 - ADAM_B1) * g
    v = ADAM_B2 * v + (1.0 - ADAM_B2) * (g * g)
    m_hat = m / (1.0 - ADAM_B1 ** ADAM_STEP)
    v_hat = v / (1.0 - ADAM_B2 ** ADAM_STEP)
    delta = -ADAM_LR * (m_hat / (jnp.sqrt(v_hat) + ADAM_EPS) + ADAM_WD * w)
    return delta, m, v


def _adam_big(name, groups):
    k = len(groups)
    rows, cols = groups[0][2].shape
    tr = _update_rows(rows)

    def body(*refs):
        for i in range(k):
            p_ref, q_ref, w_ref, m_ref, v_ref = refs[5 * i:5 * i + 5]
            g_ref, d_ref, nm_ref, nv_ref = refs[5 * k + 4 * i:5 * k + 4 * i + 4]
            g = p_ref[...] + q_ref[...]
            g_ref[...] = g
            d_ref[...], nm_ref[...], nv_ref[...] = _adamw(w_ref[...], g, m_ref[...], v_ref[...])

    spec = pl.BlockSpec((tr, cols), lambda i: (i, 0))
    res = pl.pallas_call(
        body, name=name, grid=(rows // tr,), in_specs=[spec] * (5 * k), out_specs=[spec] * (4 * k),
        out_shape=[jax.ShapeDtypeStruct((rows, cols), F32)] * (4 * k),
        compiler_params=_params(("arbitrary",)),
    )(*[a for grp in groups for a in grp])
    return [res[4 * i:4 * i + 4] for i in range(k)]


_SMALL_AT = {"ln1_g": 0, "ln1_b": D_MODEL, "ln2_g": 2 * D_MODEL, "ln2_b": 3 * D_MODEL, "ln3_g": 4 * D_MODEL,
             "ln3_b": 5 * D_MODEL, "b_in": SM_BIN, "attn_sinks": SM_SINK, "hgrn_norm_g": SM_NG}


def _adam_small(total, w, m, v, lb_of):
    names = list(_SMALL)
    k = len(names)

    def body(*refs):
        t_ref = refs[0]
        w_refs, m_refs, v_refs = (refs[1 + i * k:1 + (i + 1) * k] for i in range(3))
        g_refs, d_refs, nm_refs, nv_refs = (refs[1 + (3 + i) * k:1 + (4 + i) * k] for i in range(4))
        for i, name in enumerate(names):
            if name == "hgrn_lb_logits":
                _, vjp = jax.vjp(lb_of, w_refs[i][0:1, :], w_refs[i][1:2, :])
                g_refs[i][0:1, :], g_refs[i][1:2, :] = vjp(t_ref[:, SM_LB:SM_LOSS])
            else:
                at = _SMALL_AT[name]
                g_refs[i][...] = t_ref[:, at:at + w_refs[i].shape[1]]
            d_refs[i][...], nm_refs[i][...], nv_refs[i][...] = _adamw(
                w_refs[i][...], g_refs[i][...], m_refs[i][...], v_refs[i][...])

    shapes = [jax.ShapeDtypeStruct(w[name].shape, F32) for name in names]
    res = pl.pallas_call(body, name="adam_small", out_shape=shapes * 4)(
        total, *[w[n] for n in names], *[m[n] for n in names], *[v[n] for n in names])
    return [dict(zip(names, res[i * k:(i + 1) * k])) for i in range(4)]


_BIG = ("ffn1_w1", "ffn1_w3", "ffn1_w2", "w_in", "w_proj_attn", "w_proj_hgrn", "w_out", "ffn2_w1", "ffn2_w3", "ffn2_w2")
_SMALL = ("ln1_g", "ln1_b", "ln2_g", "ln2_b", "ln3_g", "ln3_b", "b_in", "attn_sinks", "hgrn_norm_g", "hgrn_lb_logits")
_ORDER = ("ln1_g", "ln1_b", "ffn1_w1", "ffn1_w3", "ffn1_w2", "ln2_g", "ln2_b", "w_in", "b_in", "attn_sinks",
          "hgrn_lb_logits", "hgrn_norm_g", "w_proj_attn", "w_proj_hgrn", "w_out", "ln3_g", "ln3_b",
          "ffn2_w1", "ffn2_w3", "ffn2_w2")


_TRANSPOSED = ("ffn1_w1", "ffn1_w3", "ffn2_w1", "ffn2_w3")


def _local_view(name, arr):
    return arr[0].T if name in _TRANSPOSED else arr[0]


def _ffn_grad(name, hidden, other, comm=None):
    (dw,), comm_out = _grad_matmul(name, hidden, "cols", D_FF // FF_GRAD_PARTS, other, "shared", D_MODEL,
                                   comm=comm, parts=FF_GRAD_PARTS)
    return dw.reshape(N_CHIPS, D_FF // N_CHIPS, D_MODEL), comm_out


def kernel(x, ln1_g, ln1_b, ffn1_w1, ffn1_w3, ffn1_w2, ln2_g, ln2_b, w_in, b_in, attn_sinks, hgrn_lb_logits, hgrn_norm_g, w_proj_attn, w_proj_hgrn, w_out, ln3_g, ln3_b, ffn2_w1, ffn2_w3, ffn2_w2, loss_target, m_ln1_g, m_ln1_b, m_ffn1_w1, m_ffn1_w3, m_ffn1_w2, m_ln2_g, m_ln2_b, m_w_in, m_b_in, m_attn_sinks, m_hgrn_lb_logits, m_hgrn_norm_g, m_w_proj_attn, m_w_proj_hgrn, m_w_out, m_ln3_g, m_ln3_b, m_ffn2_w1, m_ffn2_w3, m_ffn2_w2, v_ln1_g, v_ln1_b, v_ffn1_w1, v_ffn1_w3, v_ffn1_w2, v_ln2_g, v_ln2_b, v_w_in, v_b_in, v_attn_sinks, v_hgrn_lb_logits, v_hgrn_norm_g, v_w_proj_attn, v_w_proj_hgrn, v_w_out, v_ln3_g, v_ln3_b, v_ffn2_w1, v_ffn2_w3, v_ffn2_w2):
    w = dict(ln1_g=ln1_g, ln1_b=ln1_b, ffn1_w1=ffn1_w1, ffn1_w3=ffn1_w3, ffn1_w2=ffn1_w2, ln2_g=ln2_g, ln2_b=ln2_b,
             w_in=w_in, b_in=b_in, attn_sinks=attn_sinks, hgrn_lb_logits=hgrn_lb_logits, hgrn_norm_g=hgrn_norm_g,
             w_proj_attn=w_proj_attn, w_proj_hgrn=w_proj_hgrn, w_out=w_out, ln3_g=ln3_g, ln3_b=ln3_b,
             ffn2_w1=ffn2_w1, ffn2_w3=ffn2_w3, ffn2_w2=ffn2_w2)
    mom = dict(ln1_g=m_ln1_g, ln1_b=m_ln1_b, ffn1_w1=m_ffn1_w1, ffn1_w3=m_ffn1_w3, ffn1_w2=m_ffn1_w2, ln2_g=m_ln2_g,
               ln2_b=m_ln2_b, w_in=m_w_in, b_in=m_b_in, attn_sinks=m_attn_sinks, hgrn_lb_logits=m_hgrn_lb_logits,
               hgrn_norm_g=m_hgrn_norm_g, w_proj_attn=m_w_proj_attn, w_proj_hgrn=m_w_proj_hgrn, w_out=m_w_out,
               ln3_g=m_ln3_g, ln3_b=m_ln3_b, ffn2_w1=m_ffn2_w1, ffn2_w3=m_ffn2_w3, ffn2_w2=m_ffn2_w2)
    var = dict(ln1_g=v_ln1_g, ln1_b=v_ln1_b, ffn1_w1=v_ffn1_w1, ffn1_w3=v_ffn1_w3, ffn1_w2=v_ffn1_w2, ln2_g=v_ln2_g,
               ln2_b=v_ln2_b, w_in=v_w_in, b_in=v_b_in, attn_sinks=v_attn_sinks, hgrn_lb_logits=v_hgrn_lb_logits,
               hgrn_norm_g=v_hgrn_norm_g, w_proj_attn=v_w_proj_attn, w_proj_hgrn=v_w_proj_hgrn, w_out=v_w_out,
               ln3_g=v_ln3_g, ln3_b=v_ln3_b, ffn2_w1=v_ffn2_w1, ffn2_w3=v_ffn2_w3, ffn2_w2=v_ffn2_w2)

    n_tok = x.shape[1]
    x0 = x.reshape(n_tok, D_MODEL)
    target = loss_target.reshape(n_tok, D_MODEL)

    shard = {k: _bf(_local_view(k, w[k])) for k in _BIG}
    gather = lambda keys: _GatherWeights([shard[k] for k in keys])
    slots = {}
    exchange = lambda keys: _ExchangeGrads([big[k] for k in keys])
    ffn1_keys = ("ffn1_w1", "ffn1_w3", "ffn1_w2")
    mixer_keys = ("w_in", "w_proj_attn", "w_proj_hgrn", "w_out")
    ffn2_keys = ("ffn2_w1", "ffn2_w3", "ffn2_w2")
    whole = lambda ts: [t.reshape(D_FF, D_MODEL) for t in ts]
    f1 = whole(_run_comm("gather_ffn1", gather(ffn1_keys)))

    tabs = _rope_tables(n_tok)
    lb, lb_of = _lb_fwd(hgrn_lb_logits)
    (z1, x1, x0b, h1_1, h3_1, x1b), (w_in_g, w_pa, w_ph, w_o) = _ffn_fwd(
        "ffn1_fwd", x0, *f1, ln1_g, ln1_b, comm=gather(mixer_keys))
    w_pa, w_ph, w_o = (t.reshape(D_MODEL, D_MODEL) for t in (w_pa, w_ph, w_o))
    w_in_g = _reorder_w_in("w_in_cols", w_in_g, True)
    (proj,), f2_up = _in_proj(x1b, w_in_g, _to_kernel_cols(b_in), comm=gather(ffn2_keys[:2]))
    (y_attn,), f2_down = _attn_fwd(proj, tabs, attn_sinks, comm=gather(ffn2_keys[2:]))
    f2 = whole(f2_up + f2_down)
    y_hgrn, states = _hgrn_fwd(proj, lb, hgrn_norm_g)
    z2, x2, proj_a, proj_h = _mix_fwd(y_attn, y_hgrn, proj, x1, w_pa, w_ph, w_o, ln2_g, ln2_b)
    (z3, dy, x2b, h1_2, h3_2, loss_part), _ = _ffn_fwd("ffn2_fwd", x2, *f2, ln3_g, ln3_b, target=target)

    big = {}
    small = {}
    (dx2, a2, dh1_2, dh3_2, df2, small["ln3_g"], small["ln3_b"]), _ = _ffn_bwd(
        "ffn2_bwd", h1_2, h3_2, z3, dy, *f2, ln3_g, ln3_b)
    big["ffn2_w1"], _ = _ffn_grad("ffn2_dw1", dh1_2, x2b)
    big["ffn2_w3"], _ = _ffn_grad("ffn2_dw3", dh3_2, x2b)
    big["ffn2_w2"], _ = _ffn_grad("ffn2_dw2", a2, df2)
    (dz2, dz2b, merged, dya, dyh, dy_attn, dy_hgrn, dproj, small["ln2_g"], small["ln2_b"]) = _mix_bwd(
        dx2, z2, proj_a, proj_h, proj, w_pa, w_ph, w_o, ln2_g, ln2_b)
    for key, name, lhs, rhs in (("w_out", "dw_out", merged, dz2b), ("w_proj_attn", "dw_proj_attn", y_attn, dya),
                                ("w_proj_hgrn", "dw_proj_hgrn", y_hgrn, dyh)):
        (dw,), _ = _grad_matmul(name, lhs, "cols", D_MODEL, rhs, "shared", D_MODEL, parts=1)
        big[key] = dw.reshape(N_CHIPS, PROJ_SHARD, D_MODEL)
    (dproj, dlb, small["hgrn_norm_g"]), early = _hgrn_bwd(
        proj, lb, hgrn_norm_g, states, dy_hgrn, dproj, comm=exchange(ffn2_keys[:2]))
    slots.update(zip(ffn2_keys[:2], early))
    attn_hosted = ffn2_keys[2:] + mixer_keys[1:]
    (dproj, dsinks), early = _attn_bwd(proj, dy_attn, tabs, attn_sinks, dproj, comm=exchange(attn_hosted))
    slots.update(zip(attn_hosted, early))
    (dw_in, db_in), _ = _grad_matmul("dw_in", x1b, "shared", D_MODEL, dproj, "cols", IN_SHARD, colsum=True)
    big["w_in"] = _reorder_w_in("dw_in_cols", dw_in, False)
    small["b_in"] = _from_kernel_cols(db_in)
    (dx1,), (w_in_slots,) = _in_proj_dx(
        dproj, w_in_g, dz2, comm=_ExchangeGrads([big["w_in"]], rows=(0, W_IN_ROWS_FIRST)))
    (grad_x, a1, dh1_1, dh3_1, df1, small["ln1_g"], small["ln1_b"]), _ = _ffn_bwd(
        "ffn1_bwd", h1_1, h3_1, z1, dx1, *f1, ln1_g, ln1_b)
    big["ffn1_w1"], (slots["w_in"],) = _ffn_grad("ffn1_dw1", dh1_1, x0b, comm=_ExchangeGrads(
        [big["w_in"]], rows=(W_IN_ROWS_FIRST, D_MODEL - W_IN_ROWS_FIRST), into=[w_in_slots]))
    big["ffn1_w3"], (slots["ffn1_w1"],) = _ffn_grad("ffn1_dw3", dh3_1, x0b, comm=exchange(("ffn1_w1",)))
    big["ffn1_w2"], (slots["ffn1_w3"],) = _ffn_grad("ffn1_dw2", a1, df1, comm=exchange(("ffn1_w3",)))

    last = "ffn1_w2"
    groups = [[k for k in ffn1_keys + ffn2_keys if k != last], list(mixer_keys[1:]), ["w_in"]]
    partial = {}
    for keys in groups:
        partial.update(zip(keys, _sum_slots("sum_" + keys[0], [slots[k] for k in keys])))
    swapped_keys = [k for keys in groups for k in keys]
    moved = _run_comm("swap_and_exchange_last",
                      _Together(_SwapWithSibling([partial[k] for k in swapped_keys]), exchange((last,))))
    from_sibling = dict(zip(swapped_keys, moved))
    (partial[last],) = _sum_slots("sum_" + last, [moved[-1]])
    (from_sibling[last],) = _run_comm("swap_last", _SwapWithSibling([partial[last]]))
    groups[0].append(last)

    outs = {"grad": {}, "delta": {}, "m": {}, "v": {}}
    for keys in groups:
        res = _adam_big("adam_" + keys[0], [(partial[k], from_sibling[k], _local_view(k, w[k]), _local_view(k, mom[k]),
                                             _local_view(k, var[k])) for k in keys])
        for k, four in zip(keys, res):
            for kind, r in zip(("grad", "delta", "m", "v"), four):
                outs[kind][k] = (r.T if k in _TRANSPOSED else r).reshape(w[k].shape)

    total = _sum_small(jnp.concatenate(
        [small[k] for k in ("ln1_g", "ln1_b", "ln2_g", "ln2_b", "ln3_g", "ln3_b", "b_in")]
        + [dsinks, small["hgrn_norm_g"], dlb, loss_part], axis=1))
    for kind, r in zip(("grad", "delta", "m", "v"), _adam_small(total, w, mom, var, lb_of)):
        outs[kind].update(r)
    loss = total[0, SM_LOSS]

    return (loss, grad_x.reshape(x.shape), *[outs["grad"][k] for k in _ORDER], *[outs["delta"][k] for k in _ORDER],
            *[outs["m"][k] for k in _ORDER], *[outs["v"][k] for k in _ORDER])
```
